```python
import jax, jax.numpy as jnp
from jax import lax
import numpy as np

D_MODEL = 1024
BATCH = 32
SEQ = 256
DEPTH = 2
DEC_BATCH = 2
DEC_SEQ = 1024
PAST_LEN = 256

GRID_W = 64
HEAD_DIM = 64
N_EVEN = (DEPTH + 1) // 2
N_ODD = DEPTH // 2
H_A = 8
Q_RANK = 256
KV_RANK = 128
NOPE_A = 64
ROPE_A = 32
V_A = 64
QK_A = NOPE_A + ROPE_A
H_B = 8
NA_ROWS = 8
NA_COLS = 16
H_C = 8
KV_C = 2
H_D = 8
KV_D = 2
SWA_HALF = 128
QBLOCK = 128
ROPE_THETA = 10000.0
EPS = 1e-6
NEG_INF = -1e30
EVEN_SPLITS = (Q_RANK, KV_RANK, ROPE_A, H_A * V_A, H_B * HEAD_DIM, H_B * HEAD_DIM, H_B * HEAD_DIM, H_B * HEAD_DIM)
EVEN_IN = sum(EVEN_SPLITS)
EVEN_MIX = H_A * V_A + H_B * HEAD_DIM
ODD_SPLITS = (H_C * HEAD_DIM, KV_C * HEAD_DIM, KV_C * HEAD_DIM, H_C * HEAD_DIM, H_D * HEAD_DIM, KV_D * HEAD_DIM, KV_D * HEAD_DIM, H_D * HEAD_DIM)
ODD_IN = sum(ODD_SPLITS)
ODD_MIX = H_C * HEAD_DIM + H_D * HEAD_DIM

kernel_name = 'hybrid_diffusion_prefix_trunk_step'


def _split(z, sizes):
    out, off = [], 0
    for s in sizes:
        out.append(z[..., off:off + s])
        off += s
    return out


def _rmsnorm(x, g):
    xf = x.astype(jnp.float32)
    y = xf * lax.rsqrt(jnp.mean(xf * xf, axis=-1, keepdims=True) + EPS)
    return (y * g.astype(jnp.float32)).astype(x.dtype)


def _heads(x, n):
    b, s, _ = x.shape
    return x.reshape(b, s, n, -1).transpose(0, 2, 1, 3)


def _merge(o):
    b, h, s, d = o.shape
    return o.transpose(0, 2, 1, 3).reshape(b, s, h * d)


def _groups(q, kvh):
    b, h, s, d = q.shape
    return q.reshape(b, kvh, h // kvh, s, d)


def _axial_rope(s, rot_dim):
    quarter = rot_dim // 4
    t = jnp.arange(s)
    inv = ROPE_THETA ** (-jnp.arange(quarter, dtype=jnp.float32) / quarter)
    row = (t // GRID_W).astype(jnp.float32)[:, None] * inv
    col = (t % GRID_W).astype(jnp.float32)[:, None] * inv
    ang = jnp.concatenate([row, col], axis=-1)
    return jnp.cos(ang), jnp.sin(ang)


def _rope(x, cos, sin):
    half = x.shape[-1] // 2
    x1 = x[..., :half].astype(jnp.float32)
    x2 = x[..., half:].astype(jnp.float32)
    return jnp.concatenate([x1 * cos - x2 * sin, x1 * sin + x2 * cos], axis=-1).astype(x.dtype)


def _rope_tail(x, cos, sin, n):
    return jnp.concatenate([x[..., :-n], _rope(x[..., -n:], cos, sin)], axis=-1)


def _modulate(x, cond, g, w_mod, b_mod):
    m = jax.nn.silu(cond) @ w_mod + b_mod
    if m.ndim == 2:
        m = m[:, None, :]
    shift, scale, gate = jnp.split(m, 3, axis=-1)
    return _rmsnorm(x, g) * (1 + scale) + shift, gate


def _attend_blocked(q, parts, sink=None):
    b, kh, g, s, dk = q.shape
    bq = min(QBLOCK, s)
    nb = s // bq
    qb = jnp.moveaxis(q.reshape(b, kh, g, nb, bq, dk), 3, 0)
    lens = [k.shape[2] for k, _ in parts]

    def block(qi):
        sc = jnp.concatenate([jnp.einsum('bkgqd,bkld->bkgql', qi, k) for k, _ in parts], axis=-1).astype(jnp.float32)
        if sink is not None:
            sk = jnp.broadcast_to(sink.astype(jnp.float32)[None, :, :, None, None], sc.shape[:-1] + (1,))
            sc = jnp.concatenate([sc, sk], axis=-1)
        p = jax.nn.softmax(sc, axis=-1)
        out, off = None, 0
        for (k, v), ln in zip(parts, lens):
            o = jnp.einsum('bkgql,bkld->bkgqd', p[..., off:off + ln].astype(v.dtype), v)
            out = o if out is None else out + o
            off += ln
        return out

    o = lax.map(block, qb)
    return jnp.moveaxis(o, 0, 3).reshape(b, kh * g, s, -1)


def _neighbourhood(q, k, v, k_ctx, v_ctx, rpb):
    b, h, s, d = q.shape
    rows = s // GRID_W
    kr = min(NA_ROWS, rows)
    kc = NA_COLS
    ncb = GRID_W // kc
    halo = 2 * kc
    r = jnp.arange(rows)
    row_idx = jnp.clip(r - kr // 2, 0, rows - kr)[:, None] + jnp.arange(kr)[None, :]
    j = jnp.arange(ncb)
    col_idx = jnp.clip(j * kc - kc // 2, 0, GRID_W - halo)[:, None] + jnp.arange(halo)[None, :]
    qcol = j[:, None] * kc + jnp.arange(kc)[None, :]
    cs = jnp.clip(qcol - kc // 2, 0, GRID_W - kc)
    valid = (col_idx[:, None, :] >= cs[..., None]) & (col_idx[:, None, :] < cs[..., None] + kc)
    dr = row_idx - r[:, None] + (NA_ROWS - 1)
    dc = jnp.clip(col_idx[:, None, :] - qcol[..., None] + (kc - 1), 0, 2 * kc - 2)
    bias = rpb[:, dr[:, None, None, :, None], dc[None, :, :, None, :]].astype(jnp.float32)
    bias = jnp.where(valid[None, None, :, :, None, :], bias, NEG_INF)
    qg = q.reshape(b, h, rows, ncb, kc, d)
    ri = row_idx[:, None, :, None]
    ci = col_idx[None, :, None, :]
    kg = k.reshape(b, h, rows, GRID_W, d)[:, :, ri, ci]
    vg = v.reshape(b, h, rows, GRID_W, d)[:, :, ri, ci]
    s_lat = jnp.einsum('bhrjqd,bhrjkwd->bhrjqkw', qg, kg).astype(jnp.float32) + bias[None]
    nl = kr * halo
    s_lat = s_lat.reshape(b, h, rows, ncb, kc, nl)
    s_ctx = jnp.einsum('bhrjqd,bhcd->bhrjqc', qg, k_ctx).astype(jnp.float32)
    p = jax.nn.softmax(jnp.concatenate([s_lat, s_ctx], axis=-1), axis=-1)
    o = (jnp.einsum('bhrjqn,bhrjnd->bhrjqd', p[..., :nl].astype(v.dtype), vg.reshape(b, h, rows, ncb, nl, d))
         + jnp.einsum('bhrjqc,bhcd->bhrjqd', p[..., nl:].astype(v.dtype), v_ctx))
    return o.reshape(b, h, s, d)


def _windowed(q, k, v, k_ctx, v_ctx, sink):
    b, kh, g, s, d = q.shape
    w = SWA_HALF
    nb = s // w
    pad = ((0, 0), (0, 0), (w, w), (0, 0))
    kp = jnp.pad(k, pad).reshape(b, kh, nb + 2, w, d)
    vp = jnp.pad(v, pad).reshape(b, kh, nb + 2, w, d)
    kb = jnp.concatenate([kp[:, :, 0:nb], kp[:, :, 1:nb + 1], kp[:, :, 2:nb + 2]], axis=3)
    vb = jnp.concatenate([vp[:, :, 0:nb], vp[:, :, 1:nb + 1], vp[:, :, 2:nb + 2]], axis=3)
    qb = q.reshape(b, kh, g, nb, w, d)
    qpos = jnp.arange(s).reshape(nb, w)
    kpos = jnp.arange(nb)[:, None] * w - w + jnp.arange(3 * w)[None, :]
    valid = ((kpos[:, None, :] >= 0) & (kpos[:, None, :] < s)
             & (jnp.abs(qpos[:, :, None] - kpos[:, None, :]) <= w))
    s_loc = jnp.where(valid, jnp.einsum('bkgnqd,bknld->bkgnql', qb, kb).astype(jnp.float32), NEG_INF)
    s_ctx = jnp.einsum('bkgnqd,bkcd->bkgnqc', qb, k_ctx).astype(jnp.float32)
    sk = jnp.broadcast_to(sink.astype(jnp.float32)[None, :, :, None, None, None], s_loc.shape[:-1] + (1,))
    p = jax.nn.softmax(jnp.concatenate([s_loc, s_ctx, sk], axis=-1), axis=-1)
    nl = 3 * w
    nc = k_ctx.shape[2]
    o = (jnp.einsum('bkgnql,bknld->bkgnqd', p[..., :nl].astype(v.dtype), vb)
         + jnp.einsum('bkgnqc,bkcd->bkgnqd', p[..., nl:nl + nc].astype(v.dtype), v_ctx))
    return o.reshape(b, kh * g, s, d)


def _gated_out(o1, g1, o2, g2, w_out):
    y = jnp.concatenate([_merge(o1) * jax.nn.silu(g1), _merge(o2) * jax.nn.silu(g2)], axis=-1)
    return y @ w_out


def _even_project(h, w_in, qa_g, w_q_up, kva_g, q_g, na_q_g, na_k_g):
    q_lat, ckv, krope, gate_a, q_b, k_b, v_b, gate_b = _split(h @ w_in, EVEN_SPLITS)
    q_a = _rmsnorm(_heads(_rmsnorm(q_lat, qa_g) @ w_q_up, H_A), q_g)
    q_b = _rmsnorm(_heads(q_b, H_B), na_q_g)
    k_b = _rmsnorm(_heads(k_b, H_B), na_k_g)
    return q_a, _rmsnorm(ckv, kva_g), krope, gate_a, q_b, k_b, _heads(v_b, H_B), gate_b


def _mla_kv(ckv, krope, w_kv_up, k_g):
    kv = _heads(ckv @ w_kv_up, H_A)
    b, h, l, _ = kv.shape
    k = jnp.concatenate([kv[..., :NOPE_A], jnp.broadcast_to(krope[:, None], (b, h, l, ROPE_A))], axis=-1)
    return _rmsnorm(k, k_g), kv[..., NOPE_A:]


def _even_context(h, pe):
    w_in, qa_g, w_q_up, kva_g, w_kv_up, q_g, k_g, na_q_g, na_k_g, rpb, w_out = pe
    q_a, ckv, krope, gate_a, q_b, k_b, v_b, gate_b = _even_project(h, w_in, qa_g, w_q_up, kva_g, q_g, na_q_g, na_k_g)
    k_a, v_a = _mla_kv(ckv, krope, w_kv_up, k_g)
    o_a = _attend_blocked(_groups(q_a * QK_A ** -0.5, H_A), [(k_a, v_a)])
    o_b = _attend_blocked(_groups(q_b * HEAD_DIM ** -0.5, H_B), [(k_b, v_b)])
    return _gated_out(o_a, gate_a, o_b, gate_b, w_out), (ckv, krope, k_b, v_b)


def _even_latent(h, ckv_c, krope_c, k_b_c, v_b_c, pe):
    w_in, qa_g, w_q_up, kva_g, w_kv_up, q_g, k_g, na_q_g, na_k_g, rpb, w_out = pe
    q_a, ckv, krope, gate_a, q_b, k_b, v_b, gate_b = _even_project(h, w_in, qa_g, w_q_up, kva_g, q_g, na_q_g, na_k_g)
    cos, sin = _axial_rope(h.shape[1], ROPE_A)
    q_a = _rope_tail(q_a, cos, sin, ROPE_A)
    k_a, v_a = _mla_kv(ckv, krope, w_kv_up, k_g)
    k_a = _rope_tail(k_a, cos, sin, ROPE_A)
    k_ac, v_ac = _mla_kv(ckv_c, krope_c, w_kv_up, k_g)
    o_a = _attend_blocked(_groups(q_a * QK_A ** -0.5, H_A), [(k_a, v_a), (k_ac, v_ac)])
    o_b = _neighbourhood(q_b * HEAD_DIM ** -0.5, k_b, v_b, k_b_c, v_b_c, rpb)
    return _gated_out(o_a, gate_a, o_b, gate_b, w_out)


def _odd_project(h, w_in, gq_g, gk_g, sq_g, sk_g):
    qc, kc, vc, gc, qd, kd, vd, gd = _split(h @ w_in, ODD_SPLITS)
    return (_rmsnorm(_heads(qc, H_C), gq_g), _rmsnorm(_heads(kc, KV_C), gk_g), _heads(vc, KV_C), gc,
            _rmsnorm(_heads(qd, H_D), sq_g), _rmsnorm(_heads(kd, KV_D), sk_g), _heads(vd, KV_D), gd)


def _odd_context(h, po):
    w_in, gq_g, gk_g, sq_g, sk_g, sink, w_out = po
    qc, kc, vc, gc, qd, kd, vd, gd = _odd_project(h, w_in, gq_g, gk_g, sq_g, sk_g)
    sc = HEAD_DIM ** -0.5
    o_c = _attend_blocked(_groups(qc * sc, KV_C), [(kc, vc)])
    o_d = _attend_blocked(_groups(qd * sc, KV_D), [(kd, vd)], sink.reshape(KV_D, H_D // KV_D))
    return _gated_out(o_c, gc, o_d, gd, w_out), (kc, vc, kd, vd)


def _odd_latent(h, kc_c, vc_c, kd_c, vd_c, po):
    w_in, gq_g, gk_g, sq_g, sk_g, sink, w_out = po
    qc, kc, vc, gc, qd, kd, vd, gd = _odd_project(h, w_in, gq_g, gk_g, sq_g, sk_g)
    cos, sin = _axial_rope(h.shape[1], HEAD_DIM)
    qc, kc, qd, kd = _rope(qc, cos, sin), _rope(kc, cos, sin), _rope(qd, cos, sin), _rope(kd, cos, sin)
    sc = HEAD_DIM ** -0.5
    o_c = _attend_blocked(_groups(qc * sc, KV_C), [(kc, vc), (kc_c, vc_c)])
    o_d = _windowed(_groups(qd * sc, KV_D), kd, vd, kd_c, vd_c, sink.reshape(KV_D, H_D // KV_D))
    return _gated_out(o_c, gc, o_d, gd, w_out)


def setup_inputs(seed: int = 0) -> dict:
    key = jax.random.key(seed)
    ks = iter(jax.random.split(key, 48))

    def nrm(shape, scale):
        return jax.random.normal(next(ks), shape, jnp.float32) * scale

    def gain(shape):
        return 1.0 + nrm(shape, 0.01)

    return {
        'x_prompt': nrm((BATCH, SEQ, D_MODEL), 1.0),
        'x_sample': nrm((DEC_BATCH, DEC_SEQ, D_MODEL), 1.0),
        'cache_mla_ckv': nrm((DEC_BATCH, N_EVEN, PAST_LEN, KV_RANK), 1.0),
        'cache_mla_krope': nrm((DEC_BATCH, N_EVEN, PAST_LEN, ROPE_A), 1.0),
        'cache_na_k': nrm((DEC_BATCH, N_EVEN, H_B, PAST_LEN, HEAD_DIM), 1.0),
        'cache_na_v': nrm((DEC_BATCH, N_EVEN, H_B, PAST_LEN, HEAD_DIM), 1.0),
        'cache_gqa_k': nrm((DEC_BATCH, N_ODD, KV_C, PAST_LEN, HEAD_DIM), 1.0),
        'cache_gqa_v': nrm((DEC_BATCH, N_ODD, KV_C, PAST_LEN, HEAD_DIM), 1.0),
        'cache_swa_k': nrm((DEC_BATCH, N_ODD, KV_D, PAST_LEN, HEAD_DIM), 1.0),
        'cache_swa_v': nrm((DEC_BATCH, N_ODD, KV_D, PAST_LEN, HEAD_DIM), 1.0),
        'c': nrm((DEC_BATCH, D_MODEL), 1.0),
        'c_ctx': nrm((D_MODEL,), 1.0),
        'norm_g': gain((DEPTH, D_MODEL)),
        'w_mod': nrm((DEPTH, D_MODEL, 3 * D_MODEL), 0.5 * D_MODEL ** -0.5),
        'b_mod': nrm((DEPTH, 3 * D_MODEL), 0.01),
        'w_in_even': nrm((N_EVEN, D_MODEL, EVEN_IN), D_MODEL ** -0.5),
        'mla_qa_g': gain((N_EVEN, Q_RANK)),
        'w_q_up': nrm((N_EVEN, Q_RANK, H_A * QK_A), Q_RANK ** -0.5),
        'mla_kva_g': gain((N_EVEN, KV_RANK)),
        'w_kv_up': nrm((N_EVEN, KV_RANK, H_A * (NOPE_A + V_A)), KV_RANK ** -0.5),
        'mla_q_g': gain((N_EVEN, QK_A)),
        'mla_k_g': gain((N_EVEN, QK_A)),
        'na_q_g': gain((N_EVEN, HEAD_DIM)),
        'na_k_g': gain((N_EVEN, HEAD_DIM)),
        'na_rpb': nrm((N_EVEN, H_B, 2 * NA_ROWS - 1, 2 * NA_COLS - 1), 0.1),
        'w_out_even': nrm((N_EVEN, EVEN_MIX, D_MODEL), EVEN_MIX ** -0.5),
        'w_in_odd': nrm((N_ODD, D_MODEL, ODD_IN), D_MODEL ** -0.5),
        'gqa_q_g': gain((N_ODD, HEAD_DIM)),
        'gqa_k_g': gain((N_ODD, HEAD_DIM)),
        'swa_q_g': gain((N_ODD, HEAD_DIM)),
        'swa_k_g': gain((N_ODD, HEAD_DIM)),
        'swa_sink': nrm((N_ODD, H_D), 1.0),
        'w_out_odd': nrm((N_ODD, ODD_MIX, D_MODEL), ODD_MIX ** -0.5),
    }


def reference(x_prompt, x_sample, cache_mla_ckv, cache_mla_krope, cache_na_k, cache_na_v, cache_gqa_k, cache_gqa_v,
              cache_swa_k, cache_swa_v, c, c_ctx, norm_g, w_mod, b_mod, w_in_even, mla_qa_g, w_q_up, mla_kva_g,
              w_kv_up, mla_q_g, mla_k_g, na_q_g, na_k_g, na_rpb, w_out_even, w_in_odd, gqa_q_g, gqa_k_g, swa_q_g,
              swa_k_g, swa_sink, w_out_odd):
    xp, xs = x_prompt, x_sample
    st_e = ([], [], [], [])
    st_o = ([], [], [], [])
    for l in range(DEPTH):
        i = l // 2
        hp, gp = _modulate(xp, c_ctx, norm_g[l], w_mod[l], b_mod[l])
        hs, gs = _modulate(xs, c, norm_g[l], w_mod[l], b_mod[l])
        if l % 2 == 0:
            pe = (w_in_even[i], mla_qa_g[i], w_q_up[i], mla_kva_g[i], w_kv_up[i], mla_q_g[i], mla_k_g[i],
                  na_q_g[i], na_k_g[i], na_rpb[i], w_out_even[i])
            yp, ctx = _even_context(hp, pe)
            ys = _even_latent(hs, cache_mla_ckv[:, i], cache_mla_krope[:, i], cache_na_k[:, i], cache_na_v[:, i], pe)
            for lst, t in zip(st_e, ctx):
                lst.append(t)
        else:
            po = (w_in_odd[i], gqa_q_g[i], gqa_k_g[i], swa_q_g[i], swa_k_g[i], swa_sink[i], w_out_odd[i])
            yp, ctx = _odd_context(hp, po)
            ys = _odd_latent(hs, cache_gqa_k[:, i], cache_gqa_v[:, i], cache_swa_k[:, i], cache_swa_v[:, i], po)
            for lst, t in zip(st_o, ctx):
                lst.append(t)
        xp = xp + gp * yp
        xs = xs + gs * ys
    new_mla_ckv = jnp.stack(st_e[0], axis=1)
    new_mla_krope = jnp.stack(st_e[1], axis=1)
    new_na_k = jnp.stack(st_e[2], axis=1)
    new_na_v = jnp.stack(st_e[3], axis=1)
    new_gqa_k = jnp.stack(st_o[0], axis=1)
    new_gqa_v = jnp.stack(st_o[1], axis=1)
    new_swa_k = jnp.stack(st_o[2], axis=1)
    new_swa_v = jnp.stack(st_o[3], axis=1)
    return (xp, xs, new_mla_ckv, new_mla_krope, new_na_k, new_na_v, new_gqa_k, new_gqa_v, new_swa_k, new_swa_v)
```

```python
import functools

import jax
import jax.numpy as jnp
from jax import lax
from jax.experimental import pallas as pl
from jax.experimental.pallas import tpu as pltpu

F32 = jnp.float32
BF16 = jnp.bfloat16

D_MODEL = 1024
SEQ = 256
DEC_SEQ = 1024
PAST_LEN = 256
GRID_W = 64
HEAD_DIM = 64
Q_RANK = 256
KV_RANK = 128
NOPE_A = 64
ROPE_A = 32
QK_A = NOPE_A + ROPE_A
N_HEADS = 8
NA_ROWS = 8
NA_COLS = 16
SWA_HALF = 128
ROPE_THETA = 10000.0
EPS = 1e-6
NEG_INF = -1e30

LANES = 128
Q_BLOCK = 256
N_PAIRS = N_HEADS // 2
RPB_ROWS = 2 * NA_ROWS - 1
RPB_COLS = 2 * NA_COLS - 1
BIAS_CHUNKS = 16
VMEM_LIMIT = 48 * 1024 * 1024

E_QLAT, E_CKV, E_KROPE, E_GA, E_QB, E_KB, E_VB, E_GB, E_END = 0, 256, 384, 512, 1024, 1536, 2048, 2560, 3072
O_QC, O_KC, O_VC, O_GC, O_QD, O_KD, O_VD, O_GD, O_END = 0, 512, 640, 768, 1280, 1792, 1920, 2048, 2560


def _dot(a, b):
    return lax.dot_general(a, b, (((1,), (0,)), ((), ())), preferred_element_type=F32)


def _dot_nt(a, b):
    return lax.dot_general(a, b, (((1,), (1,)), ((), ())), preferred_element_type=F32)


def _silu(x):
    return x / (1.0 + jnp.exp(-x))


def _rms(x, g, n):
    ss = jnp.sum(x * x, axis=-1, keepdims=True)
    return x * lax.rsqrt(ss / n + EPS) * g


def _rms_halves(x, g2, lo):
    x2 = x * x
    s_lo = jnp.sum(jnp.where(lo, x2, 0.0), axis=-1, keepdims=True)
    s_hi = jnp.sum(jnp.where(lo, 0.0, x2), axis=-1, keepdims=True)
    r = jnp.where(lo, lax.rsqrt(s_lo / HEAD_DIM + EPS), lax.rsqrt(s_hi / HEAD_DIM + EPS))
    return x * r * g2


def _modulate(x, g, m):
    d = x.shape[-1]
    xn = x * lax.rsqrt(jnp.mean(x * x, axis=-1, keepdims=True) + EPS) * g
    return xn * (1.0 + m[:, d:2 * d]) + m[:, :d], m[:, 2 * d:]


def _attend(parts, sink=None):
    mx = None
    for s, _ in parts:
        pm = jnp.max(s, axis=-1, keepdims=True)
        mx = pm if mx is None else jnp.maximum(mx, pm)
    if sink is not None:
        mx = jnp.maximum(mx, sink)
    den, out = None, None
    for s, v in parts:
        e = jnp.exp(s - mx)
        ps = jnp.sum(e, axis=-1, keepdims=True)
        po = _dot(e.astype(BF16), v)
        den = ps if den is None else den + ps
        out = po if out is None else out + po
    if sink is not None:
        den = den + jnp.exp(sink - mx)
    return out * (1.0 / den)


def _lane_lo():
    return lax.broadcasted_iota(jnp.int32, (1, LANES), 1) < HEAD_DIM


def _mod_kernel(c_ref, w_ref, b_ref, o_ref):
    s = _silu(c_ref[...])
    o_ref[0] = lax.dot_general(s, w_ref[0], (((1,), (0,)), ((), ())), precision=lax.Precision.HIGHEST,
                               preferred_element_type=F32) + b_ref[0]


def _modulation(cond8, w_mod, b_mod):
    depth = w_mod.shape[0]
    tn = 512
    return pl.pallas_call(
        _mod_kernel,
        grid=(depth, 3 * D_MODEL // tn),
        in_specs=[pl.BlockSpec((8, D_MODEL), lambda l, n: (0, 0)),
                  pl.BlockSpec((1, D_MODEL, tn), lambda l, n: (l, 0, n)),
                  pl.BlockSpec((1, 1, tn), lambda l, n: (l, 0, n))],
        out_specs=pl.BlockSpec((1, 8, tn), lambda l, n: (l, 0, n)),
        out_shape=jax.ShapeDtypeStruct((depth, 8, 3 * D_MODEL), F32),
        name="modulation",
    )(cond8, w_mod, b_mod.reshape(depth, 1, 3 * D_MODEL))


def _mla_keys(cb, kr, wkk_ref, wkv_ref, kg, rope=None):
    kk = _dot(cb, wkk_ref[...])
    keys = []
    for h in range(N_HEADS):
        k = _rms(kk[:, h * LANES:(h + 1) * LANES] + kr, kg, QK_A)
        if rope is not None:
            k = rope(k)
        keys.append(k.astype(BF16))
    return keys, _dot(cb, wkv_ref[...]).astype(BF16)


def _p0_kernel(x_ref, m_ref, ng_ref, win_ref, qag_ref, wq_ref, kvag_ref, wkk_ref, wkv_ref, qg_ref, kg_ref,
               naqg_ref, nakg_ref, wout_ref,
               xo_ref, ckv_ref, krope_ref, nak_ref, nav_ref, y_scr):
    x = x_ref[0]
    h, gate = _modulate(x, ng_ref[...], m_ref[0:1, :])
    hb = h.astype(BF16)
    lo = _lane_lo()

    qln = _rms(_dot(hb, win_ref[:, E_QLAT:E_CKV]), qag_ref[...], Q_RANK).astype(BF16)
    q_all = _dot(qln, wq_ref[...])
    ckv_n = _rms(_dot(hb, win_ref[:, E_CKV:E_KROPE]), kvag_ref[...], KV_RANK)
    ckv_ref[0, 0] = ckv_n
    kr = _dot(hb, win_ref[:, E_KROPE:E_GA])
    krope_ref[0, 0] = pltpu.roll(kr, LANES - NOPE_A, 1)[:, :ROPE_A]
    keys, vals = _mla_keys(ckv_n.astype(BF16), kr, wkk_ref, wkv_ref, kg_ref[...])
    qg = qg_ref[...] * (QK_A ** -0.5)

    ga = _dot(hb, win_ref[:, E_GA:E_QB])
    zq = _dot(hb, win_ref[:, E_QB:E_KB])
    zk = _dot(hb, win_ref[:, E_KB:E_VB])
    zv = _dot(hb, win_ref[:, E_VB:E_GB])
    gb = _dot(hb, win_ref[:, E_GB:E_END])
    naqg = naqg_ref[...] * (HEAD_DIM ** -0.5)

    for p in range(N_PAIRS):
        sl = slice(p * LANES, (p + 1) * LANES)
        vpair = vals[:, sl]
        o2 = []
        for hh in (2 * p, 2 * p + 1):
            qh = _rms(q_all[:, hh * LANES:(hh + 1) * LANES], qg, QK_A).astype(BF16)
            o2.append(_attend([(_dot_nt(qh, keys[hh]), vpair)]))
        oa = jnp.where(lo, o2[0], o2[1])
        y_scr[:, sl] = (oa * _silu(ga[:, sl])).astype(BF16)

        qb = _rms_halves(zq[:, sl], naqg, lo)
        kb = _rms_halves(zk[:, sl], nakg_ref[...], lo)
        vb = zv[:, sl]
        nak_ref[0, 0, 2 * p] = kb[:, :HEAD_DIM]
        nak_ref[0, 0, 2 * p + 1] = pltpu.roll(kb, HEAD_DIM, 1)[:, :HEAD_DIM]
        nav_ref[0, 0, 2 * p] = vb[:, :HEAD_DIM]
        nav_ref[0, 0, 2 * p + 1] = pltpu.roll(vb, HEAD_DIM, 1)[:, :HEAD_DIM]
        kbb, vbb = kb.astype(BF16), vb.astype(BF16)
        o2 = []
        for half in (0, 1):
            qm = jnp.where(lo if half == 0 else jnp.logical_not(lo), qb, 0.0).astype(BF16)
            o2.append(_attend([(_dot_nt(qm, kbb), vbb)]))
        ob = jnp.where(lo, o2[0], o2[1])
        y_scr[:, 4 * LANES + p * LANES:4 * LANES + (p + 1) * LANES] = (ob * _silu(gb[:, sl])).astype(BF16)

    xo_ref[0] = x + gate * _dot(y_scr[...], wout_ref[...])


def _full(shape):
    n = len(shape)
    return pl.BlockSpec(shape, lambda *_: (0,) * n)


def _prompt_even(x, m, ng, win, qag, wq, kvag, wkk, wkv, qg, kg, naqg, nakg, wout):
    nb = x.shape[0]
    ins = (m, ng, win, qag, wq, kvag, wkk, wkv, qg, kg, naqg, nakg, wout)
    return pl.pallas_call(
        _p0_kernel,
        grid=(nb,),
        in_specs=[pl.BlockSpec((1, SEQ, D_MODEL), lambda b: (b, 0, 0))] + [_full(a.shape) for a in ins],
        out_specs=[pl.BlockSpec((1, SEQ, D_MODEL), lambda b: (b, 0, 0)),
                   pl.BlockSpec((1, 1, SEQ, KV_RANK), lambda b: (b, 0, 0, 0)),
                   pl.BlockSpec((1, 1, SEQ, ROPE_A), lambda b: (b, 0, 0, 0)),
                   pl.BlockSpec((1, 1, N_HEADS, SEQ, HEAD_DIM), lambda b: (b, 0, 0, 0, 0)),
                   pl.BlockSpec((1, 1, N_HEADS, SEQ, HEAD_DIM), lambda b: (b, 0, 0, 0, 0))],
        out_shape=[jax.ShapeDtypeStruct((nb, SEQ, D_MODEL), F32),
                   jax.ShapeDtypeStruct((nb, 1, SEQ, KV_RANK), F32),
                   jax.ShapeDtypeStruct((nb, 1, SEQ, ROPE_A), F32),
                   jax.ShapeDtypeStruct((nb, 1, N_HEADS, SEQ, HEAD_DIM), F32),
                   jax.ShapeDtypeStruct((nb, 1, N_HEADS, SEQ, HEAD_DIM), F32)],
        scratch_shapes=[pltpu.VMEM((SEQ, D_MODEL), BF16)],
        compiler_params=pltpu.CompilerParams(dimension_semantics=("arbitrary",), vmem_limit_bytes=VMEM_LIMIT),
        name="prompt_even",
    )(x, *ins)


def _gqa_pair_operands(k, v, kg2, lo):
    kn = _rms_halves(k, kg2, lo)
    return kn, (kn.astype(BF16), pltpu.roll(kn, HEAD_DIM, 1).astype(BF16)), \
        (v.astype(BF16), pltpu.roll(v, HEAD_DIM, 1).astype(BF16))


def _p1_kernel(sink_ref, x_ref, m_ref, ng_ref, win_ref, gqg_ref, gkg_ref, sqg_ref, skg_ref, wout_ref,
               xo_ref, gk_ref, gv_ref, sk_ref, sv_ref, y_scr):
    x = x_ref[0]
    h, gate = _modulate(x, ng_ref[...], m_ref[0:1, :])
    hb = h.astype(BF16)
    lo = _lane_lo()
    sc = HEAD_DIM ** -0.5

    branches = ((O_QC, O_KC, O_VC, O_GC, gqg_ref, gkg_ref, gk_ref, gv_ref, False, 0),
                (O_QD, O_KD, O_VD, O_GD, sqg_ref, skg_ref, sk_ref, sv_ref, True, 4 * LANES))
    for oq, ok, ov, og, qg_ref, kg_ref, ck_ref, cv_ref, has_sink, yoff in branches:
        zq = _dot(hb, win_ref[:, oq:oq + 4 * LANES])
        zkv = _dot(hb, win_ref[:, ok:ok + 2 * LANES])
        zg = _dot(hb, win_ref[:, og:og + 4 * LANES])
        v = zkv[:, LANES:]
        kn, ks, vs = _gqa_pair_operands(zkv[:, :LANES], v, kg_ref[...], lo)
        ck_ref[0, 0, 0] = kn[:, :HEAD_DIM]
        ck_ref[0, 0, 1] = pltpu.roll(kn, HEAD_DIM, 1)[:, :HEAD_DIM]
        cv_ref[0, 0, 0] = v[:, :HEAD_DIM]
        cv_ref[0, 0, 1] = pltpu.roll(v, HEAD_DIM, 1)[:, :HEAD_DIM]
        qg = qg_ref[...] * sc
        for p in range(N_PAIRS):
            sl = slice(p * LANES, (p + 1) * LANES)
            qn = _rms_halves(zq[:, sl], qg, lo)
            kv = p // 2
            o2 = []
            for half in (0, 1):
                qm = jnp.where(lo if half == 0 else jnp.logical_not(lo), qn, 0.0).astype(BF16)
                swap = 0 if kv == half else 1
                sink = sink_ref[2 * p + half] if has_sink else None
                o2.append(_attend([(_dot_nt(qm, ks[swap]), vs[swap])], sink))
            o = jnp.where(lo, o2[0], o2[1])
            y_scr[:, yoff + p * LANES:yoff + (p + 1) * LANES] = (o * _silu(zg[:, sl])).astype(BF16)

    xo_ref[0] = x + gate * _dot(y_scr[...], wout_ref[...])


def _prompt_odd(sink, x, m, ng, win, gqg, gkg, sqg, skg, wout):
    nb = x.shape[0]
    ins = (m, ng, win, gqg, gkg, sqg, skg, wout)
    cache_spec = pl.BlockSpec((1, 1, 2, SEQ, HEAD_DIM), lambda b: (b, 0, 0, 0, 0))
    cache_shape = jax.ShapeDtypeStruct((nb, 1, 2, SEQ, HEAD_DIM), F32)
    return pl.pallas_call(
        _p1_kernel,
        grid=(nb,),
        in_specs=[pl.BlockSpec(memory_space=pltpu.SMEM),
                  pl.BlockSpec((1, SEQ, D_MODEL), lambda b: (b, 0, 0))] + [_full(a.shape) for a in ins],
        out_specs=[pl.BlockSpec((1, SEQ, D_MODEL), lambda b: (b, 0, 0))] + [cache_spec] * 4,
        out_shape=[jax.ShapeDtypeStruct((nb, SEQ, D_MODEL), F32)] + [cache_shape] * 4,
        scratch_shapes=[pltpu.VMEM((SEQ, D_MODEL), BF16)],
        compiler_params=pltpu.CompilerParams(dimension_semantics=("arbitrary",), vmem_limit_bytes=VMEM_LIMIT),
        name="prompt_odd",
    )(sink, x, *ins)


def _s0a_kernel(x_ref, m_ref, ng_ref, win_ref, qag_ref, wq_ref, kvag_ref, wkk_ref, wkv_ref, qg_ref, kg_ref,
                naqg_ref, nakg_ref, cos_ref, sa_ref, sb_ref,
                qa_ref, ka_ref, va_ref, qb_ref, kb_ref, vb_ref, g_ref):
    b = pl.program_id(0)
    x = x_ref[0]
    h, _ = _modulate(x, ng_ref[...], m_ref[pl.ds(1 + b, 1), :])
    hb = h.astype(BF16)
    lo = _lane_lo()
    cos, sa, sb = cos_ref[...], sa_ref[...], sb_ref[...]

    def rope(t):
        return t * cos + pltpu.roll(t, LANES - ROPE_A // 2, 1) * sa + pltpu.roll(t, ROPE_A // 2, 1) * sb

    qln = _rms(_dot(hb, win_ref[:, E_QLAT:E_CKV]), qag_ref[...], Q_RANK).astype(BF16)
    q_all = _dot(qln, wq_ref[...])
    qg = qg_ref[...] * (QK_A ** -0.5)
    for hh in range(N_HEADS):
        sl = slice(hh * LANES, (hh + 1) * LANES)
        qa_ref[0, :, sl] = rope(_rms(q_all[:, sl], qg, QK_A)).astype(BF16)

    ckv_n = _rms(_dot(hb, win_ref[:, E_CKV:E_KROPE]), kvag_ref[...], KV_RANK)
    kr = _dot(hb, win_ref[:, E_KROPE:E_GA])
    keys, vals = _mla_keys(ckv_n.astype(BF16), kr, wkk_ref, wkv_ref, kg_ref[...], rope)
    for hh in range(N_HEADS):
        ka_ref[0, :, hh * LANES:(hh + 1) * LANES] = keys[hh]
    va_ref[0] = vals

    zq = _dot(hb, win_ref[:, E_QB:E_KB])
    zk = _dot(hb, win_ref[:, E_KB:E_VB])
    naqg = naqg_ref[...] * (HEAD_DIM ** -0.5)
    for p in range(N_PAIRS):
        sl = slice(p * LANES, (p + 1) * LANES)
        qb_ref[0, :, sl] = _rms_halves(zq[:, sl], naqg, lo).astype(BF16)
        kb_ref[0, :, sl] = _rms_halves(zk[:, sl], nakg_ref[...], lo).astype(BF16)
    vb_ref[0] = _dot(hb, win_ref[:, E_VB:E_GB]).astype(BF16)
    g_ref[0, :, 0:4 * LANES] = _silu(_dot(hb, win_ref[:, E_GA:E_QB]))
    g_ref[0, :, 4 * LANES:8 * LANES] = _silu(_dot(hb, win_ref[:, E_GB:E_END]))


def _sample_even_proj(x, m, ng, win, qag, wq, kvag, wkk, wkv, qg, kg, naqg, nakg, cos, sa, sb):
    nb, s, _ = x.shape
    nq = s // Q_BLOCK
    ins = (m, ng, win, qag, wq, kvag, wkk, wkv, qg, kg, naqg, nakg)
    tab = pl.BlockSpec((Q_BLOCK, LANES), lambda b, j: (j, 0))

    def blk(w):
        return pl.BlockSpec((1, Q_BLOCK, w), lambda b, j: (b, j, 0))

    def shp(w, dt):
        return jax.ShapeDtypeStruct((nb, s, w), dt)

    return pl.pallas_call(
        _s0a_kernel,
        grid=(nb, nq),
        in_specs=[blk(D_MODEL)] + [_full(a.shape) for a in ins] + [tab, tab, tab],
        out_specs=[blk(1024), blk(1024), blk(512), blk(512), blk(512), blk(512), blk(1024)],
        out_shape=[shp(1024, BF16), shp(1024, BF16), shp(512, BF16), shp(512, BF16), shp(512, BF16),
                   shp(512, BF16), shp(1024, F32)],
        compiler_params=pltpu.CompilerParams(dimension_semantics=("arbitrary", "arbitrary"),
                                             vmem_limit_bytes=VMEM_LIMIT),
        name="sample_even_proj",
    )(x, *ins, cos, sa, sb)


def _build_bias_table(rpb_ref, tile_scr, tab_ref):
    qc = lax.broadcasted_iota(jnp.int32, (GRID_W, LANES), 0)
    lane = lax.broadcasted_iota(jnp.int32, (GRID_W, LANES), 1)
    kc = jnp.bitwise_and(lane, GRID_W - 1)
    lo = lane < GRID_W
    diff = kc - qc + (NA_COLS - 1)
    cs = jnp.clip(qc - NA_COLS // 2, 0, GRID_W - NA_COLS)
    valid = (kc >= cs) & (kc < cs + NA_COLS)
    tab_ref[...] = jnp.zeros(tab_ref.shape, F32)
    tile_scr[RPB_ROWS] = jnp.zeros((GRID_W, LANES), F32)

    def per_head(h, carry):
        for dr in range(RPB_ROWS):
            t = jnp.zeros((GRID_W, LANES), F32)
            for dc in range(RPB_COLS):
                t = jnp.where(diff == dc, rpb_ref[(h * RPB_ROWS + dr) * RPB_COLS + dc], t)
            tile_scr[dr] = jnp.where(valid, t, NEG_INF)
        for c in range(NA_ROWS // 2, NA_ROWS // 2 + NA_ROWS):
            d0 = 2 * c - NA_ROWS
            tab_ref[0, h, c] = jnp.where(lo, tile_scr[d0], tile_scr[d0 + 1])
            tab_ref[1, h, c] = jnp.where(lo, tile_scr[d0 - 1 if d0 > 0 else RPB_ROWS], tile_scr[d0])
        return carry

    lax.fori_loop(0, N_HEADS, per_head, 0)


def _s0b_kernel(rpb_ref, x_ref, m_ref, qa_ref, ka_ref, va_ref, qb_ref, kb_ref, vb_ref, g_ref,
                cckv_ref, ckr_ref, cnk_ref, cnv_ref, wkk_ref, wkv_ref, kg_ref, wout_ref,
                xo_ref, kca_scr, vca_scr, tile_scr, tab_scr, y_scr):
    b = pl.program_id(0)
    j = pl.program_id(1)
    lo = _lane_lo()
    n_lat = ka_ref.shape[1]

    @pl.when((b == 0) & (j == 0))
    def _():
        _build_bias_table(rpb_ref, tile_scr, tab_scr)

    @pl.when(j == 0)
    def _():
        keys, vals = _mla_keys(cckv_ref[0].astype(BF16), ckr_ref[0], wkk_ref, wkv_ref, kg_ref[...])
        for hh in range(N_HEADS):
            kca_scr[:, hh * LANES:(hh + 1) * LANES] = keys[hh]
        vca_scr[...] = vals

    kidx = lax.broadcasted_iota(jnp.int32, (1, n_lat), 1)
    for p in range(N_PAIRS):
        sl = slice(p * LANES, (p + 1) * LANES)
        o2 = []
        for hh in (2 * p, 2 * p + 1):
            hs = slice(hh * LANES, (hh + 1) * LANES)
            q = qa_ref[0, :, hs]
            o2.append(_attend([(_dot_nt(q, ka_ref[0, :, hs]), va_ref[0, :, sl]),
                               (_dot_nt(q, kca_scr[:, hs]), vca_scr[:, sl])]))
        oa = jnp.where(lo, o2[0], o2[1])
        y_scr[:, sl] = (oa * g_ref[0, :, sl]).astype(BF16)

        qb = qb_ref[0, :, sl]
        kb = kb_ref[0, :, sl]
        vb = vb_ref[0, :, sl]
        kcb = cnk_ref[0, :, sl].astype(BF16)
        vcb = cnv_ref[0, :, sl].astype(BF16)
        o2 = []
        for half in (0, 1):
            head = 2 * p + half
            qm = jnp.where(lo if half == 0 else jnp.logical_not(lo), qb, jnp.zeros_like(qb))
            s_lat = _dot_nt(qm, kb)
            rows = []
            for local in range(Q_BLOCK // GRID_W):
                qr = j * (Q_BLOCK // GRID_W) + local
                par = 0 if local % 2 == 1 else 1
                c0 = (RPB_ROWS + par - local) // 2 - 2 * j
                bias = jnp.concatenate([tab_scr[par, head, c0 + t] for t in range(n_lat // LANES)], axis=1)
                r0 = jnp.clip(qr - NA_ROWS // 2, 0, n_lat // GRID_W - NA_ROWS) * GRID_W
                ok = (kidx >= r0) & (kidx < r0 + NA_ROWS * GRID_W)
                rows.append(jnp.where(ok, s_lat[local * GRID_W:(local + 1) * GRID_W] + bias, NEG_INF))
            s_lat = jnp.concatenate(rows, axis=0)
            o2.append(_attend([(s_lat, vb), (_dot_nt(qm, kcb), vcb)]))
        ob = jnp.where(lo, o2[0], o2[1])
        ys = slice(4 * LANES + p * LANES, 4 * LANES + (p + 1) * LANES)
        y_scr[:, ys] = (ob * g_ref[0, :, ys]).astype(BF16)

    d = x_ref.shape[-1]
    gate = m_ref[pl.ds(1 + b, 1), 2 * d:]
    xo_ref[0] = x_ref[0] + gate * _dot(y_scr[...], wout_ref[...])


def _sample_even_attn(rpb, x, m, qa, ka, va, qb, kb, vb, g, cckv, ckr, cnk, cnv, wkk, wkv, kg, wout):
    nb, s, _ = x.shape
    nq = s // Q_BLOCK

    def blk(w):
        return pl.BlockSpec((1, Q_BLOCK, w), lambda b, j: (b, j, 0))

    def per_batch(a):
        return pl.BlockSpec((1,) + a.shape[1:], lambda b, j: (b, 0, 0))

    return pl.pallas_call(
        _s0b_kernel,
        grid=(nb, nq),
        in_specs=[pl.BlockSpec(memory_space=pltpu.SMEM), blk(D_MODEL), _full(m.shape),
                  blk(1024), per_batch(ka), per_batch(va), blk(512), per_batch(kb), per_batch(vb), blk(1024),
                  per_batch(cckv), per_batch(ckr), per_batch(cnk), per_batch(cnv),
                  _full(wkk.shape), _full(wkv.shape), _full(kg.shape), _full(wout.shape)],
        out_specs=blk(D_MODEL),
        out_shape=jax.ShapeDtypeStruct(x.shape, F32),
        scratch_shapes=[pltpu.VMEM((PAST_LEN, N_HEADS * LANES), BF16),
                        pltpu.VMEM((PAST_LEN, N_HEADS * HEAD_DIM), BF16),
                        pltpu.VMEM((RPB_ROWS + 1, GRID_W, LANES), F32),
                        pltpu.VMEM((2, N_HEADS, BIAS_CHUNKS, GRID_W, LANES), F32),
                        pltpu.VMEM((Q_BLOCK, D_MODEL), BF16)],
        compiler_params=pltpu.CompilerParams(dimension_semantics=("arbitrary", "arbitrary"),
                                             vmem_limit_bytes=VMEM_LIMIT),
        name="sample_even_attn",
    )(rpb, x, m, qa, ka, va, qb, kb, vb, g, cckv, ckr, cnk, cnv, wkk, wkv, kg, wout)


def _s1a_kernel(x_ref, m_ref, ng_ref, win_ref, gqg_ref, gkg_ref, sqg_ref, skg_ref, cos_ref, sa_ref, sb_ref,
                qc_ref, kc_ref, vc_ref, qd_ref, kd_ref, vd_ref, g_ref):
    b = pl.program_id(0)
    x = x_ref[0]
    h, _ = _modulate(x, ng_ref[...], m_ref[pl.ds(1 + b, 1), :])
    hb = h.astype(BF16)
    lo = _lane_lo()
    cos, sa, sb = cos_ref[...], sa_ref[...], sb_ref[...]
    sc = HEAD_DIM ** -0.5

    def rope(t):
        return t * cos + pltpu.roll(t, LANES - HEAD_DIM // 2, 1) * sa + pltpu.roll(t, HEAD_DIM // 2, 1) * sb

    branches = ((O_QC, O_KC, O_GC, gqg_ref, gkg_ref, qc_ref, kc_ref, vc_ref, 0),
                (O_QD, O_KD, O_GD, sqg_ref, skg_ref, qd_ref, kd_ref, vd_ref, 4 * LANES))
    for oq, ok, og, qg_ref, kg_ref, q_out, k_out, v_out, goff in branches:
        zq = _dot(hb, win_ref[:, oq:oq + 4 * LANES])
        zkv = _dot(hb, win_ref[:, ok:ok + 2 * LANES])
        qg = qg_ref[...] * sc
        for p in range(N_PAIRS):
            sl = slice(p * LANES, (p + 1) * LANES)
            q_out[0, :, sl] = rope(_rms_halves(zq[:, sl], qg, lo)).astype(BF16)
        kn = rope(_rms_halves(zkv[:, :LANES], kg_ref[...], lo))
        v = zkv[:, LANES:]
        k_out[0, :, 0:LANES] = kn.astype(BF16)
        k_out[0, :, LANES:2 * LANES] = pltpu.roll(kn, HEAD_DIM, 1).astype(BF16)
        v_out[0, :, 0:LANES] = v.astype(BF16)
        v_out[0, :, LANES:2 * LANES] = pltpu.roll(v, HEAD_DIM, 1).astype(BF16)
        g_ref[0, :, goff:goff + 4 * LANES] = _silu(_dot(hb, win_ref[:, og:og + 4 * LANES]))


def _sample_odd_proj(x, m, ng, win, gqg, gkg, sqg, skg, cos, sa, sb):
    nb, s, _ = x.shape
    nq = s // Q_BLOCK
    ins = (m, ng, win, gqg, gkg, sqg, skg)
    tab = pl.BlockSpec((Q_BLOCK, LANES), lambda b, j: (j, 0))

    def blk(w):
        return pl.BlockSpec((1, Q_BLOCK, w), lambda b, j: (b, j, 0))

    def shp(w, dt):
        return jax.ShapeDtypeStruct((nb, s, w), dt)

    return pl.pallas_call(
        _s1a_kernel,
        grid=(nb, nq),
        in_specs=[blk(D_MODEL)] + [_full(a.shape) for a in ins] + [tab, tab, tab],
        out_specs=[blk(512), blk(256), blk(256), blk(512), blk(256), blk(256), blk(1024)],
        out_shape=[shp(512, BF16), shp(256, BF16), shp(256, BF16), shp(512, BF16), shp(256, BF16),
                   shp(256, BF16), shp(1024, F32)],
        compiler_params=pltpu.CompilerParams(dimension_semantics=("arbitrary", "arbitrary"),
                                             vmem_limit_bytes=VMEM_LIMIT),
        name="sample_odd_proj",
    )(x, *ins, cos, sa, sb)


def _s1b_kernel(sink_ref, x_ref, m_ref, qc_ref, kc_ref, vc_ref, qd_ref, kd_ref, vd_ref, g_ref,
                cgk_ref, cgv_ref, csk_ref, csv_ref, wout_ref, xo_ref, y_scr):
    b = pl.program_id(0)
    j = pl.program_id(1)
    lo = _lane_lo()
    n_lat = kc_ref.shape[1]
    win_keys = Q_BLOCK + 2 * SWA_HALF

    def ctx_pair(ref):
        a = ref[0]
        return a.astype(BF16), pltpu.roll(a, HEAD_DIM, 1).astype(BF16)

    cgk, cgv, csk, csv = ctx_pair(cgk_ref), ctx_pair(cgv_ref), ctx_pair(csk_ref), ctx_pair(csv_ref)

    ks = pl.multiple_of(jnp.clip(j * Q_BLOCK - SWA_HALF, 0, n_lat - win_keys), SWA_HALF)
    qpos = j * Q_BLOCK + lax.broadcasted_iota(jnp.int32, (Q_BLOCK, win_keys), 0)
    kpos = ks + lax.broadcasted_iota(jnp.int32, (Q_BLOCK, win_keys), 1)
    band = jnp.abs(qpos - kpos) <= SWA_HALF

    for p in range(N_PAIRS):
        sl = slice(p * LANES, (p + 1) * LANES)
        kv = p // 2
        qc = qc_ref[0, :, sl]
        qd = qd_ref[0, :, sl]
        oc2, od2 = [], []
        for half in (0, 1):
            swap = 0 if kv == half else 1
            ws = slice(swap * LANES, (swap + 1) * LANES)
            keep = lo if half == 0 else jnp.logical_not(lo)
            qm = jnp.where(keep, qc, jnp.zeros_like(qc))
            oc2.append(_attend([(_dot_nt(qm, kc_ref[0, :, ws]), vc_ref[0, :, ws]),
                                (_dot_nt(qm, cgk[swap]), cgv[swap])]))
            qm = jnp.where(keep, qd, jnp.zeros_like(qd))
            s_loc = jnp.where(band, _dot_nt(qm, kd_ref[0, pl.ds(ks, win_keys), ws]), NEG_INF)
            od2.append(_attend([(s_loc, vd_ref[0, pl.ds(ks, win_keys), ws]),
                                (_dot_nt(qm, csk[swap]), csv[swap])], sink_ref[2 * p + half]))
        y_scr[:, sl] = (jnp.where(lo, oc2[0], oc2[1]) * g_ref[0, :, sl]).astype(BF16)
        ys = slice(4 * LANES + p * LANES, 4 * LANES + (p + 1) * LANES)
        y_scr[:, ys] = (jnp.where(lo, od2[0], od2[1]) * g_ref[0, :, ys]).astype(BF16)

    d = x_ref.shape[-1]
    gate = m_ref[pl.ds(1 + b, 1), 2 * d:]
    xo_ref[0] = x_ref[0] + gate * _dot(y_scr[...], wout_ref[...])


def _sample_odd_attn(sink, x, m, qc, kc, vc, qd, kd, vd, g, cgk, cgv, csk, csv, wout):
    nb, s, _ = x.shape
    nq = s // Q_BLOCK

    def blk(w):
        return pl.BlockSpec((1, Q_BLOCK, w), lambda b, j: (b, j, 0))

    def per_batch(a):
        return pl.BlockSpec((1,) + a.shape[1:], lambda b, j: (b, 0, 0))

    return pl.pallas_call(
        _s1b_kernel,
        grid=(nb, nq),
        in_specs=[pl.BlockSpec(memory_space=pltpu.SMEM), blk(D_MODEL), _full(m.shape),
                  blk(512), per_batch(kc), per_batch(vc), blk(512), per_batch(kd), per_batch(vd), blk(1024),
                  per_batch(cgk), per_batch(cgv), per_batch(csk), per_batch(csv), _full(wout.shape)],
        out_specs=blk(D_MODEL),
        out_shape=jax.ShapeDtypeStruct(x.shape, F32),
        scratch_shapes=[pltpu.VMEM((Q_BLOCK, D_MODEL), BF16)],
        compiler_params=pltpu.CompilerParams(dimension_semantics=("arbitrary", "arbitrary"),
                                             vmem_limit_bytes=VMEM_LIMIT),
        name="sample_odd_attn",
    )(sink, x, m, qc, kc, vc, qd, kd, vd, g, cgk, cgv, csk, csv, wout)


def _pad_cols(w, left, width):
    return jnp.pad(w, ((0, 0), (left, width - left - w.shape[1])))


def _per_head_slabs(w, n_heads, head_w, take):
    k = w.shape[0]
    wh = w.reshape(k, n_heads, head_w)[:, :, :take]
    return jnp.pad(wh, ((0, 0), (0, 0), (0, LANES - take))).reshape(k, n_heads * LANES)


def _merge_heads(c):
    b, h, l, d = c.shape
    return c.transpose(0, 2, 1, 3).reshape(b, l, h * d)


def _rope_tables(s, rot_dim, period, start):
    quarter = rot_dim // 4
    half = rot_dim // 2
    t = jnp.arange(s)
    inv = ROPE_THETA ** (-jnp.arange(quarter, dtype=F32) / quarter)
    row = (t // GRID_W).astype(F32)[:, None] * inv
    col = (t % GRID_W).astype(F32)[:, None] * inv
    ang = jnp.concatenate([row, col], axis=-1)
    cos, sin = jnp.cos(ang), jnp.sin(ang)
    zeros = jnp.zeros_like(sin)
    pre = jnp.ones((s, start), F32)
    post = jnp.zeros((s, period - start - rot_dim), F32)
    c = jnp.concatenate([pre, cos, cos, post], axis=-1)
    a = jnp.concatenate([0 * pre, -sin, zeros, post], axis=-1)
    bt = jnp.concatenate([0 * pre, zeros, sin, post], axis=-1)
    rep = LANES // period
    return jnp.tile(c, (1, rep)), jnp.tile(a, (1, rep)), jnp.tile(bt, (1, rep))


def _row(v, width=None):
    v = v.reshape(1, -1).astype(F32)
    return v if width is None else jnp.pad(v, ((0, 0), (0, width - v.shape[1])))


def kernel(x_prompt, x_sample, cache_mla_ckv, cache_mla_krope, cache_na_k, cache_na_v, cache_gqa_k, cache_gqa_v, cache_swa_k, cache_swa_v, c, c_ctx, norm_g, w_mod, b_mod, w_in_even, mla_qa_g, w_q_up, mla_kva_g, w_kv_up, mla_q_g, mla_k_g, na_q_g, na_k_g, na_rpb, w_out_even, w_in_odd, gqa_q_g, gqa_k_g, swa_q_g, swa_k_g, swa_sink, w_out_odd):
    n_dec = x_sample.shape[0]
    assert w_mod.shape[0] == 2 and n_dec + 1 <= 8

    cond8 = jnp.zeros((8, D_MODEL), F32).at[0].set(c_ctx).at[1:1 + n_dec].set(c)
    m_all = _modulation(cond8, w_mod, b_mod)

    we = w_in_even[0]
    q_lat, ckv, krope, ga, qb, kb, vb, gb = jnp.split(we, [256, 384, 416, 928, 1440, 1952, 2464], axis=1)
    win_e = jnp.concatenate([q_lat, ckv, _pad_cols(krope, NOPE_A, LANES), ga, qb, kb, vb, gb], axis=1).astype(BF16)
    wq = _per_head_slabs(w_q_up[0], N_HEADS, QK_A, QK_A).astype(BF16)
    wkk = _per_head_slabs(w_kv_up[0], N_HEADS, NOPE_A + HEAD_DIM, NOPE_A).astype(BF16)
    wkv = w_kv_up[0].reshape(KV_RANK, N_HEADS, NOPE_A + HEAD_DIM)[:, :, NOPE_A:].reshape(KV_RANK, -1).astype(BF16)
    even = (_row(norm_g[0]), win_e, _row(mla_qa_g[0]), wq, _row(mla_kva_g[0]), wkk, wkv,
            _row(mla_q_g[0], LANES), _row(mla_k_g[0], LANES),
            _row(jnp.tile(na_q_g[0], 2)), _row(jnp.tile(na_k_g[0], 2)))
    wout_e = w_out_even[0].astype(BF16)

    win_o = w_in_odd[0].astype(BF16)
    odd = (_row(norm_g[1]), win_o, _row(jnp.tile(gqa_q_g[0], 2)), _row(jnp.tile(gqa_k_g[0], 2)),
           _row(jnp.tile(swa_q_g[0], 2)), _row(jnp.tile(swa_k_g[0], 2)))
    wout_o = w_out_odd[0].astype(BF16)
    sink = swa_sink[0].astype(F32)

    xp1, new_ckv, new_krope, new_na_k, new_na_v = _prompt_even(x_prompt, m_all[0], *even, wout_e)
    xp2, new_gqa_k, new_gqa_v, new_swa_k, new_swa_v = _prompt_odd(sink, xp1, m_all[1], *odd, wout_o)

    cos_e, sa_e, sb_e = _rope_tables(DEC_SEQ, ROPE_A, LANES, NOPE_A)
    qa, ka, va, qbs, kbs, vbs, g0 = _sample_even_proj(x_sample, m_all[0], *even, cos_e, sa_e, sb_e)
    ckr = jnp.pad(cache_mla_krope[:, 0], ((0, 0), (0, 0), (NOPE_A, LANES - NOPE_A - ROPE_A)))
    xs1 = _sample_even_attn(na_rpb[0].reshape(-1), x_sample, m_all[0], qa, ka, va, qbs, kbs, vbs, g0,
                            cache_mla_ckv[:, 0], ckr, _merge_heads(cache_na_k[:, 0]), _merge_heads(cache_na_v[:, 0]),
                            wkk, wkv, even[8], wout_e)
    cos_o, sa_o, sb_o = _rope_tables(DEC_SEQ, HEAD_DIM, HEAD_DIM, 0)
    qc, kc, vc, qd, kd, vd, g1 = _sample_odd_proj(xs1, m_all[1], *odd, cos_o, sa_o, sb_o)
    xs2 = _sample_odd_attn(sink, xs1, m_all[1], qc, kc, vc, qd, kd, vd, g1,
                           _merge_heads(cache_gqa_k[:, 0]), _merge_heads(cache_gqa_v[:, 0]),
                           _merge_heads(cache_swa_k[:, 0]), _merge_heads(cache_swa_v[:, 0]), wout_o)

    return (xp2, xs2, new_ckv, new_krope, new_na_k, new_na_v, new_gqa_k, new_gqa_v, new_swa_k, new_swa_v)
```

```python
import functools

import jax
import jax.numpy as jnp
from jax import lax
from jax.experimental import pallas as pl
from jax.experimental.pallas import tpu as pltpu

F32 = jnp.float32
BF16 = jnp.bfloat16

D_MODEL = 1024
SEQ = 256
DEC_SEQ = 1024
PAST_LEN = 256
GRID_W = 64
HEAD_DIM = 64
Q_RANK = 256
KV_RANK = 128
NOPE_A = 64
ROPE_A = 32
QK_A = NOPE_A + ROPE_A
N_HEADS = 8
NA_ROWS = 8
NA_COLS = 16
SWA_HALF = 128
ROPE_THETA = 10000.0
EPS = 1e-6
NEG_INF = -1e30

LANES = 128
Q_BLOCK = 256
N_PAIRS = N_HEADS // 2
RPB_ROWS = 2 * NA_ROWS - 1
RPB_COLS = 2 * NA_COLS - 1
BIAS_CHUNKS = 16
VMEM_LIMIT = 48 * 1024 * 1024

E_QLAT, E_CKV, E_KROPE, E_GA, E_QB, E_KB, E_VB, E_GB, E_END = 0, 256, 384, 512, 1024, 1536, 2048, 2560, 3072
O_QC, O_KC, O_VC, O_GC, O_QD, O_KD, O_VD, O_GD, O_END = 0, 512, 640, 768, 1280, 1792, 1920, 2048, 2560


def _dot(a, b):
    return lax.dot_general(a, b, (((1,), (0,)), ((), ())), preferred_element_type=F32)


def _dot_nt(a, b):
    return lax.dot_general(a, b, (((1,), (1,)), ((), ())), preferred_element_type=F32)


def _silu(x):
    return x / (1.0 + jnp.exp(-x))


def _rms(x, g, n):
    ss = jnp.sum(x * x, axis=-1, keepdims=True)
    return x * lax.rsqrt(ss / n + EPS) * g


def _rms_halves(x, g2, lo):
    x2 = x * x
    s_lo = jnp.sum(jnp.where(lo, x2, 0.0), axis=-1, keepdims=True)
    s_hi = jnp.sum(jnp.where(lo, 0.0, x2), axis=-1, keepdims=True)
    r = jnp.where(lo, lax.rsqrt(s_lo / HEAD_DIM + EPS), lax.rsqrt(s_hi / HEAD_DIM + EPS))
    return x * r * g2


def _modulate(x, g, m):
    d = x.shape[-1]
    xn = x * lax.rsqrt(jnp.mean(x * x, axis=-1, keepdims=True) + EPS) * g
    return xn * (1.0 + m[:, d:2 * d]) + m[:, :d], m[:, 2 * d:]


def _attend(parts, sink=None):
    mx = None
    for s, _, _ in parts:
        pm = jnp.max(s, axis=-1, keepdims=True)
        mx = pm if mx is None else jnp.maximum(mx, pm)
    if sink is not None:
        mx = jnp.maximum(mx, sink)
    den, out = None, None
    for s, v, v_t in parts:
        e = jnp.exp(s - mx)
        ps = jnp.sum(e, axis=-1, keepdims=True)
        po = (_dot_nt if v_t else _dot)(e.astype(BF16), v)
        den = ps if den is None else den + ps
        out = po if out is None else out + po
    if sink is not None:
        den = den + jnp.exp(sink - mx)
    return out * (1.0 / den)


def _lane_lo():
    return lax.broadcasted_iota(jnp.int32, (1, LANES), 1) < HEAD_DIM


def _store_pair_transposed(ref, p, x):
    xt = x.T
    ref[0, 0, 2 * p] = xt[:HEAD_DIM]
    ref[0, 0, 2 * p + 1] = xt[HEAD_DIM:]


def _swap_halves(a):
    return jnp.concatenate([a[HEAD_DIM:], a[:HEAD_DIM]], axis=0)


def _mod_kernel(c_ref, w_ref, b_ref, o_ref):
    s = _silu(c_ref[...])
    o_ref[0] = lax.dot_general(s, w_ref[0], (((1,), (0,)), ((), ())), precision=lax.Precision.HIGHEST,
                               preferred_element_type=F32) + b_ref[0]


def _modulation(cond8, w_mod, b_mod):
    depth = w_mod.shape[0]
    tn = 512
    return pl.pallas_call(
        _mod_kernel,
        grid=(depth, 3 * D_MODEL // tn),
        in_specs=[pl.BlockSpec((8, D_MODEL), lambda l, n: (0, 0)),
                  pl.BlockSpec((1, D_MODEL, tn), lambda l, n: (l, 0, n)),
                  pl.BlockSpec((1, 1, tn), lambda l, n: (l, 0, n))],
        out_specs=pl.BlockSpec((1, 8, tn), lambda l, n: (l, 0, n)),
        out_shape=jax.ShapeDtypeStruct((depth, 8, 3 * D_MODEL), F32),
        name="modulation",
    )(cond8, w_mod, b_mod.reshape(depth, 1, 3 * D_MODEL))


def _mla_keys(cb, kr, wkk_ref, wkv_ref, kg, rope=None):
    kk = _dot(cb, wkk_ref[...])
    keys = []
    for h in range(N_HEADS):
        k = _rms(kk[:, h * LANES:(h + 1) * LANES] + kr, kg, QK_A)
        if rope is not None:
            k = rope(k)
        keys.append(k.astype(BF16))
    return keys, _dot(cb, wkv_ref[...]).astype(BF16)


def _p0_kernel(x_ref, m_ref, ng_ref, win_ref, qag_ref, wq_ref, kvag_ref, wkk_ref, wkv_ref, qg_ref, kg_ref,
               naqg_ref, nakg_ref, wout_ref,
               xo_ref, ckv_ref, krope_ref, nak_ref, nav_ref, y_scr):
    x = x_ref[0]
    h, gate = _modulate(x, ng_ref[...], m_ref[0:1, :])
    hb = h.astype(BF16)
    lo = _lane_lo()

    qln = _rms(_dot(hb, win_ref[:, E_QLAT:E_CKV]), qag_ref[...], Q_RANK).astype(BF16)
    q_all = _dot(qln, wq_ref[...])
    ckv_n = _rms(_dot(hb, win_ref[:, E_CKV:E_KROPE]), kvag_ref[...], KV_RANK)
    ckv_ref[0, 0] = ckv_n
    kr = _dot(hb, win_ref[:, E_KROPE:E_GA])
    krope_ref[0, 0] = kr.T[NOPE_A:QK_A]
    keys, vals = _mla_keys(ckv_n.astype(BF16), kr, wkk_ref, wkv_ref, kg_ref[...])
    qg = qg_ref[...] * (QK_A ** -0.5)

    ga = _dot(hb, win_ref[:, E_GA:E_QB])
    zq = _dot(hb, win_ref[:, E_QB:E_KB])
    zk = _dot(hb, win_ref[:, E_KB:E_VB])
    zv = _dot(hb, win_ref[:, E_VB:E_GB])
    gb = _dot(hb, win_ref[:, E_GB:E_END])
    naqg = naqg_ref[...] * (HEAD_DIM ** -0.5)

    for p in range(N_PAIRS):
        sl = slice(p * LANES, (p + 1) * LANES)
        vpair = vals[:, sl]
        o2 = []
        for hh in (2 * p, 2 * p + 1):
            qh = _rms(q_all[:, hh * LANES:(hh + 1) * LANES], qg, QK_A).astype(BF16)
            o2.append(_attend([(_dot_nt(qh, keys[hh]), vpair, False)]))
        oa = jnp.where(lo, o2[0], o2[1])
        y_scr[:, sl] = (oa * _silu(ga[:, sl])).astype(BF16)

        qb = _rms_halves(zq[:, sl], naqg, lo)
        kb = _rms_halves(zk[:, sl], nakg_ref[...], lo)
        vb = zv[:, sl]
        _store_pair_transposed(nak_ref, p, kb)
        _store_pair_transposed(nav_ref, p, vb)
        kbb, vbb = kb.astype(BF16), vb.astype(BF16)
        o2 = []
        for half in (0, 1):
            qm = jnp.where(lo if half == 0 else jnp.logical_not(lo), qb, 0.0).astype(BF16)
            o2.append(_attend([(_dot_nt(qm, kbb), vbb, False)]))
        ob = jnp.where(lo, o2[0], o2[1])
        y_scr[:, 4 * LANES + p * LANES:4 * LANES + (p + 1) * LANES] = (ob * _silu(gb[:, sl])).astype(BF16)

    xo_ref[0] = x + gate * _dot(y_scr[...], wout_ref[...])


def _full(shape):
    n = len(shape)
    return pl.BlockSpec(shape, lambda *_: (0,) * n)


def _prompt_even(x, m, ng, win, qag, wq, kvag, wkk, wkv, qg, kg, naqg, nakg, wout):
    nb = x.shape[0]
    ins = (m, ng, win, qag, wq, kvag, wkk, wkv, qg, kg, naqg, nakg, wout)
    return pl.pallas_call(
        _p0_kernel,
        grid=(nb,),
        in_specs=[pl.BlockSpec((1, SEQ, D_MODEL), lambda b: (b, 0, 0))] + [_full(a.shape) for a in ins],
        out_specs=[pl.BlockSpec((1, SEQ, D_MODEL), lambda b: (b, 0, 0)),
                   pl.BlockSpec((1, 1, SEQ, KV_RANK), lambda b: (b, 0, 0, 0)),
                   pl.BlockSpec((1, 1, ROPE_A, SEQ), lambda b: (b, 0, 0, 0)),
                   pl.BlockSpec((1, 1, N_HEADS, HEAD_DIM, SEQ), lambda b: (b, 0, 0, 0, 0)),
                   pl.BlockSpec((1, 1, N_HEADS, HEAD_DIM, SEQ), lambda b: (b, 0, 0, 0, 0))],
        out_shape=[jax.ShapeDtypeStruct((nb, SEQ, D_MODEL), F32),
                   jax.ShapeDtypeStruct((nb, 1, SEQ, KV_RANK), F32),
                   jax.ShapeDtypeStruct((nb, 1, ROPE_A, SEQ), F32),
                   jax.ShapeDtypeStruct((nb, 1, N_HEADS, HEAD_DIM, SEQ), F32),
                   jax.ShapeDtypeStruct((nb, 1, N_HEADS, HEAD_DIM, SEQ), F32)],
        scratch_shapes=[pltpu.VMEM((SEQ, D_MODEL), BF16)],
        compiler_params=pltpu.CompilerParams(dimension_semantics=("arbitrary",), vmem_limit_bytes=VMEM_LIMIT),
        name="prompt_even",
    )(x, *ins)


def _gqa_pair_operands(k, v, kg2, lo):
    kn = _rms_halves(k, kg2, lo)
    return kn, (kn.astype(BF16), pltpu.roll(kn, HEAD_DIM, 1).astype(BF16)), \
        (v.astype(BF16), pltpu.roll(v, HEAD_DIM, 1).astype(BF16))


def _p1_kernel(sink_ref, x_ref, m_ref, ng_ref, win_ref, gqg_ref, gkg_ref, sqg_ref, skg_ref, wout_ref,
               xo_ref, gk_ref, gv_ref, sk_ref, sv_ref, y_scr):
    x = x_ref[0]
    h, gate = _modulate(x, ng_ref[...], m_ref[0:1, :])
    hb = h.astype(BF16)
    lo = _lane_lo()
    sc = HEAD_DIM ** -0.5

    branches = ((O_QC, O_KC, O_VC, O_GC, gqg_ref, gkg_ref, gk_ref, gv_ref, False, 0),
                (O_QD, O_KD, O_VD, O_GD, sqg_ref, skg_ref, sk_ref, sv_ref, True, 4 * LANES))
    for oq, ok, ov, og, qg_ref, kg_ref, ck_ref, cv_ref, has_sink, yoff in branches:
        zq = _dot(hb, win_ref[:, oq:oq + 4 * LANES])
        zkv = _dot(hb, win_ref[:, ok:ok + 2 * LANES])
        zg = _dot(hb, win_ref[:, og:og + 4 * LANES])
        v = zkv[:, LANES:]
        kn, ks, vs = _gqa_pair_operands(zkv[:, :LANES], v, kg_ref[...], lo)
        _store_pair_transposed(ck_ref, 0, kn)
        _store_pair_transposed(cv_ref, 0, v)
        qg = qg_ref[...] * sc
        for p in range(N_PAIRS):
            sl = slice(p * LANES, (p + 1) * LANES)
            qn = _rms_halves(zq[:, sl], qg, lo)
            kv = p // 2
            o2 = []
            for half in (0, 1):
                qm = jnp.where(lo if half == 0 else jnp.logical_not(lo), qn, 0.0).astype(BF16)
                swap = 0 if kv == half else 1
                sink = sink_ref[2 * p + half] if has_sink else None
                o2.append(_attend([(_dot_nt(qm, ks[swap]), vs[swap], False)], sink))
            o = jnp.where(lo, o2[0], o2[1])
            y_scr[:, yoff + p * LANES:yoff + (p + 1) * LANES] = (o * _silu(zg[:, sl])).astype(BF16)

    xo_ref[0] = x + gate * _dot(y_scr[...], wout_ref[...])


def _prompt_odd(sink, x, m, ng, win, gqg, gkg, sqg, skg, wout):
    nb = x.shape[0]
    ins = (m, ng, win, gqg, gkg, sqg, skg, wout)
    cache_spec = pl.BlockSpec((1, 1, 2, HEAD_DIM, SEQ), lambda b: (b, 0, 0, 0, 0))
    cache_shape = jax.ShapeDtypeStruct((nb, 1, 2, HEAD_DIM, SEQ), F32)
    return pl.pallas_call(
        _p1_kernel,
        grid=(nb,),
        in_specs=[pl.BlockSpec(memory_space=pltpu.SMEM),
                  pl.BlockSpec((1, SEQ, D_MODEL), lambda b: (b, 0, 0))] + [_full(a.shape) for a in ins],
        out_specs=[pl.BlockSpec((1, SEQ, D_MODEL), lambda b: (b, 0, 0))] + [cache_spec] * 4,
        out_shape=[jax.ShapeDtypeStruct((nb, SEQ, D_MODEL), F32)] + [cache_shape] * 4,
        scratch_shapes=[pltpu.VMEM((SEQ, D_MODEL), BF16)],
        compiler_params=pltpu.CompilerParams(dimension_semantics=("arbitrary",), vmem_limit_bytes=VMEM_LIMIT),
        name="prompt_odd",
    )(sink, x, *ins)


def _s0a_kernel(x_ref, m_ref, ng_ref, win_ref, qag_ref, wq_ref, kvag_ref, wkk_ref, wkv_ref, qg_ref, kg_ref,
                naqg_ref, nakg_ref, cos_ref, sa_ref, sb_ref,
                qa_ref, ka_ref, va_ref, qb_ref, kb_ref, vb_ref, g_ref):
    b = pl.program_id(0)
    x = x_ref[0]
    h, _ = _modulate(x, ng_ref[...], m_ref[pl.ds(1 + b, 1), :])
    hb = h.astype(BF16)
    lo = _lane_lo()
    cos, sa, sb = cos_ref[...], sa_ref[...], sb_ref[...]

    def rope(t):
        return t * cos + pltpu.roll(t, LANES - ROPE_A // 2, 1) * sa + pltpu.roll(t, ROPE_A // 2, 1) * sb

    qln = _rms(_dot(hb, win_ref[:, E_QLAT:E_CKV]), qag_ref[...], Q_RANK).astype(BF16)
    q_all = _dot(qln, wq_ref[...])
    qg = qg_ref[...] * (QK_A ** -0.5)
    for hh in range(N_HEADS):
        sl = slice(hh * LANES, (hh + 1) * LANES)
        qa_ref[0, :, sl] = rope(_rms(q_all[:, sl], qg, QK_A)).astype(BF16)

    ckv_n = _rms(_dot(hb, win_ref[:, E_CKV:E_KROPE]), kvag_ref[...], KV_RANK)
    kr = _dot(hb, win_ref[:, E_KROPE:E_GA])
    keys, vals = _mla_keys(ckv_n.astype(BF16), kr, wkk_ref, wkv_ref, kg_ref[...], rope)
    for hh in range(N_HEADS):
        ka_ref[0, :, hh * LANES:(hh + 1) * LANES] = keys[hh]
    va_ref[0] = vals

    zq = _dot(hb, win_ref[:, E_QB:E_KB])
    zk = _dot(hb, win_ref[:, E_KB:E_VB])
    naqg = naqg_ref[...] * (HEAD_DIM ** -0.5)
    for p in range(N_PAIRS):
        sl = slice(p * LANES, (p + 1) * LANES)
        qb_ref[0, :, sl] = _rms_halves(zq[:, sl], naqg, lo).astype(BF16)
        kb_ref[0, :, sl] = _rms_halves(zk[:, sl], nakg_ref[...], lo).astype(BF16)
    vb_ref[0] = _dot(hb, win_ref[:, E_VB:E_GB]).astype(BF16)
    g_ref[0, :, 0:4 * LANES] = _silu(_dot(hb, win_ref[:, E_GA:E_QB]))
    g_ref[0, :, 4 * LANES:8 * LANES] = _silu(_dot(hb, win_ref[:, E_GB:E_END]))


def _sample_even_proj(x, m, ng, win, qag, wq, kvag, wkk, wkv, qg, kg, naqg, nakg, cos, sa, sb):
    nb, s, _ = x.shape
    nq = s // Q_BLOCK
    ins = (m, ng, win, qag, wq, kvag, wkk, wkv, qg, kg, naqg, nakg)
    tab = pl.BlockSpec((Q_BLOCK, LANES), lambda b, j: (j, 0))

    def blk(w):
        return pl.BlockSpec((1, Q_BLOCK, w), lambda b, j: (b, j, 0))

    def shp(w, dt):
        return jax.ShapeDtypeStruct((nb, s, w), dt)

    return pl.pallas_call(
        _s0a_kernel,
        grid=(nb, nq),
        in_specs=[blk(D_MODEL)] + [_full(a.shape) for a in ins] + [tab, tab, tab],
        out_specs=[blk(1024), blk(1024), blk(512), blk(512), blk(512), blk(512), blk(1024)],
        out_shape=[shp(1024, BF16), shp(1024, BF16), shp(512, BF16), shp(512, BF16), shp(512, BF16),
                   shp(512, BF16), shp(1024, F32)],
        compiler_params=pltpu.CompilerParams(dimension_semantics=("arbitrary", "arbitrary"),
                                             vmem_limit_bytes=VMEM_LIMIT),
        name="sample_even_proj",
    )(x, *ins, cos, sa, sb)


def _build_bias_table(rpb_ref, tile_scr, tab_ref):
    qc = lax.broadcasted_iota(jnp.int32, (GRID_W, LANES), 0)
    lane = lax.broadcasted_iota(jnp.int32, (GRID_W, LANES), 1)
    kc = jnp.bitwise_and(lane, GRID_W - 1)
    lo = lane < GRID_W
    diff = kc - qc + (NA_COLS - 1)
    cs = jnp.clip(qc - NA_COLS // 2, 0, GRID_W - NA_COLS)
    valid = (kc >= cs) & (kc < cs + NA_COLS)
    tab_ref[...] = jnp.zeros(tab_ref.shape, F32)
    tile_scr[RPB_ROWS] = jnp.zeros((GRID_W, LANES), F32)

    def per_head(h, carry):
        for dr in range(RPB_ROWS):
            t = jnp.zeros((GRID_W, LANES), F32)
            for dc in range(RPB_COLS):
                t = jnp.where(diff == dc, rpb_ref[(h * RPB_ROWS + dr) * RPB_COLS + dc], t)
            tile_scr[dr] = jnp.where(valid, t, NEG_INF)
        for c in range(NA_ROWS // 2, NA_ROWS // 2 + NA_ROWS):
            d0 = 2 * c - NA_ROWS
            tab_ref[0, h, c] = jnp.where(lo, tile_scr[d0], tile_scr[d0 + 1])
            tab_ref[1, h, c] = jnp.where(lo, tile_scr[d0 - 1 if d0 > 0 else RPB_ROWS], tile_scr[d0])
        return carry

    lax.fori_loop(0, N_HEADS, per_head, 0)


def _s0b_kernel(rpb_ref, x_ref, m_ref, qa_ref, ka_ref, va_ref, qb_ref, kb_ref, vb_ref, g_ref,
                cckv_ref, ckr_ref, cnk_ref, cnv_ref, wkk_ref, wkv_ref, kg_ref, wout_ref,
                xo_ref, kca_scr, vca_scr, tile_scr, tab_scr, y_scr):
    b = pl.program_id(0)
    j = pl.program_id(1)
    lo = _lane_lo()
    n_lat = ka_ref.shape[1]

    @pl.when((b == 0) & (j == 0))
    def _():
        _build_bias_table(rpb_ref, tile_scr, tab_scr)

    @pl.when(j == 0)
    def _():
        kr_t = jnp.concatenate([jnp.zeros((NOPE_A, PAST_LEN), F32), ckr_ref[0],
                                jnp.zeros((LANES - QK_A, PAST_LEN), F32)], axis=0)
        keys, vals = _mla_keys(cckv_ref[0].astype(BF16), kr_t.T, wkk_ref, wkv_ref, kg_ref[...])
        for hh in range(N_HEADS):
            kca_scr[:, hh * LANES:(hh + 1) * LANES] = keys[hh]
        vca_scr[...] = vals

    kidx = lax.broadcasted_iota(jnp.int32, (1, n_lat), 1)
    for p in range(N_PAIRS):
        sl = slice(p * LANES, (p + 1) * LANES)
        o2 = []
        for hh in (2 * p, 2 * p + 1):
            hs = slice(hh * LANES, (hh + 1) * LANES)
            q = qa_ref[0, :, hs]
            o2.append(_attend([(_dot_nt(q, ka_ref[0, :, hs]), va_ref[0, :, sl], False),
                               (_dot_nt(q, kca_scr[:, hs]), vca_scr[:, sl], False)]))
        oa = jnp.where(lo, o2[0], o2[1])
        y_scr[:, sl] = (oa * g_ref[0, :, sl]).astype(BF16)

        qb = qb_ref[0, :, sl]
        kb = kb_ref[0, :, sl]
        vb = vb_ref[0, :, sl]
        kcb = cnk_ref[0, sl, :].astype(BF16)
        vcb = cnv_ref[0, sl, :].astype(BF16)
        o2 = []
        for half in (0, 1):
            head = 2 * p + half
            qm = jnp.where(lo if half == 0 else jnp.logical_not(lo), qb, jnp.zeros_like(qb))
            s_lat = _dot_nt(qm, kb)
            rows = []
            for local in range(Q_BLOCK // GRID_W):
                qr = j * (Q_BLOCK // GRID_W) + local
                par = 0 if local % 2 == 1 else 1
                c0 = (RPB_ROWS + par - local) // 2 - 2 * j
                bias = jnp.concatenate([tab_scr[par, head, c0 + t] for t in range(n_lat // LANES)], axis=1)
                r0 = jnp.clip(qr - NA_ROWS // 2, 0, n_lat // GRID_W - NA_ROWS) * GRID_W
                ok = (kidx >= r0) & (kidx < r0 + NA_ROWS * GRID_W)
                rows.append(jnp.where(ok, s_lat[local * GRID_W:(local + 1) * GRID_W] + bias, NEG_INF))
            s_lat = jnp.concatenate(rows, axis=0)
            o2.append(_attend([(s_lat, vb, False), (_dot(qm, kcb), vcb, True)]))
        ob = jnp.where(lo, o2[0], o2[1])
        ys = slice(4 * LANES + p * LANES, 4 * LANES + (p + 1) * LANES)
        y_scr[:, ys] = (ob * g_ref[0, :, ys]).astype(BF16)

    d = x_ref.shape[-1]
    gate = m_ref[pl.ds(1 + b, 1), 2 * d:]
    xo_ref[0] = x_ref[0] + gate * _dot(y_scr[...], wout_ref[...])


def _sample_even_attn(rpb, x, m, qa, ka, va, qb, kb, vb, g, cckv, ckr, cnk, cnv, wkk, wkv, kg, wout):
    nb, s, _ = x.shape
    nq = s // Q_BLOCK

    def blk(w):
        return pl.BlockSpec((1, Q_BLOCK, w), lambda b, j: (b, j, 0))

    def per_batch(a):
        return pl.BlockSpec((1,) + a.shape[1:], lambda b, j: (b, 0, 0))

    return pl.pallas_call(
        _s0b_kernel,
        grid=(nb, nq),
        in_specs=[pl.BlockSpec(memory_space=pltpu.SMEM), blk(D_MODEL), _full(m.shape),
                  blk(1024), per_batch(ka), per_batch(va), blk(512), per_batch(kb), per_batch(vb), blk(1024),
                  per_batch(cckv), per_batch(ckr), per_batch(cnk), per_batch(cnv),
                  _full(wkk.shape), _full(wkv.shape), _full(kg.shape), _full(wout.shape)],
        out_specs=blk(D_MODEL),
        out_shape=jax.ShapeDtypeStruct(x.shape, F32),
        scratch_shapes=[pltpu.VMEM((PAST_LEN, N_HEADS * LANES), BF16),
                        pltpu.VMEM((PAST_LEN, N_HEADS * HEAD_DIM), BF16),
                        pltpu.VMEM((RPB_ROWS + 1, GRID_W, LANES), F32),
                        pltpu.VMEM((2, N_HEADS, BIAS_CHUNKS, GRID_W, LANES), F32),
                        pltpu.VMEM((Q_BLOCK, D_MODEL), BF16)],
        compiler_params=pltpu.CompilerParams(dimension_semantics=("arbitrary", "arbitrary"),
                                             vmem_limit_bytes=VMEM_LIMIT),
        name="sample_even_attn",
    )(rpb, x, m, qa, ka, va, qb, kb, vb, g, cckv, ckr, cnk, cnv, wkk, wkv, kg, wout)


def _s1a_kernel(x_ref, m_ref, ng_ref, win_ref, gqg_ref, gkg_ref, sqg_ref, skg_ref, cos_ref, sa_ref, sb_ref,
                qc_ref, kc_ref, vc_ref, qd_ref, kd_ref, vd_ref, g_ref):
    b = pl.program_id(0)
    x = x_ref[0]
    h, _ = _modulate(x, ng_ref[...], m_ref[pl.ds(1 + b, 1), :])
    hb = h.astype(BF16)
    lo = _lane_lo()
    cos, sa, sb = cos_ref[...], sa_ref[...], sb_ref[...]
    sc = HEAD_DIM ** -0.5

    def rope(t):
        return t * cos + pltpu.roll(t, LANES - HEAD_DIM // 2, 1) * sa + pltpu.roll(t, HEAD_DIM // 2, 1) * sb

    branches = ((O_QC, O_KC, O_GC, gqg_ref, gkg_ref, qc_ref, kc_ref, vc_ref, 0),
                (O_QD, O_KD, O_GD, sqg_ref, skg_ref, qd_ref, kd_ref, vd_ref, 4 * LANES))
    for oq, ok, og, qg_ref, kg_ref, q_out, k_out, v_out, goff in branches:
        zq = _dot(hb, win_ref[:, oq:oq + 4 * LANES])
        zkv = _dot(hb, win_ref[:, ok:ok + 2 * LANES])
        qg = qg_ref[...] * sc
        for p in range(N_PAIRS):
            sl = slice(p * LANES, (p + 1) * LANES)
            q_out[0, :, sl] = rope(_rms_halves(zq[:, sl], qg, lo)).astype(BF16)
        kn = rope(_rms_halves(zkv[:, :LANES], kg_ref[...], lo))
        v = zkv[:, LANES:]
        k_out[0, :, 0:LANES] = kn.astype(BF16)
        k_out[0, :, LANES:2 * LANES] = pltpu.roll(kn, HEAD_DIM, 1).astype(BF16)
        v_out[0, :, 0:LANES] = v.astype(BF16)
        v_out[0, :, LANES:2 * LANES] = pltpu.roll(v, HEAD_DIM, 1).astype(BF16)
        g_ref[0, :, goff:goff + 4 * LANES] = _silu(_dot(hb, win_ref[:, og:og + 4 * LANES]))


def _sample_odd_proj(x, m, ng, win, gqg, gkg, sqg, skg, cos, sa, sb):
    nb, s, _ = x.shape
    nq = s // Q_BLOCK
    ins = (m, ng, win, gqg, gkg, sqg, skg)
    tab = pl.BlockSpec((Q_BLOCK, LANES), lambda b, j: (j, 0))

    def blk(w):
        return pl.BlockSpec((1, Q_BLOCK, w), lambda b, j: (b, j, 0))

    def shp(w, dt):
        return jax.ShapeDtypeStruct((nb, s, w), dt)

    return pl.pallas_call(
        _s1a_kernel,
        grid=(nb, nq),
        in_specs=[blk(D_MODEL)] + [_full(a.shape) for a in ins] + [tab, tab, tab],
        out_specs=[blk(512), blk(256), blk(256), blk(512), blk(256), blk(256), blk(1024)],
        out_shape=[shp(512, BF16), shp(256, BF16), shp(256, BF16), shp(512, BF16), shp(256, BF16),
                   shp(256, BF16), shp(1024, F32)],
        compiler_params=pltpu.CompilerParams(dimension_semantics=("arbitrary", "arbitrary"),
                                             vmem_limit_bytes=VMEM_LIMIT),
        name="sample_odd_proj",
    )(x, *ins, cos, sa, sb)


def _s1b_kernel(sink_ref, x_ref, m_ref, qc_ref, kc_ref, vc_ref, qd_ref, kd_ref, vd_ref, g_ref,
                cgk_ref, cgv_ref, csk_ref, csv_ref, wout_ref, xo_ref, y_scr):
    b = pl.program_id(0)
    j = pl.program_id(1)
    lo = _lane_lo()
    n_lat = kc_ref.shape[1]
    win_keys = Q_BLOCK + 2 * SWA_HALF

    def ctx_pair(ref):
        a = ref[0].astype(BF16)
        return a, _swap_halves(a)

    cgk, cgv, csk, csv = ctx_pair(cgk_ref), ctx_pair(cgv_ref), ctx_pair(csk_ref), ctx_pair(csv_ref)

    ks = pl.multiple_of(jnp.clip(j * Q_BLOCK - SWA_HALF, 0, n_lat - win_keys), SWA_HALF)
    qpos = j * Q_BLOCK + lax.broadcasted_iota(jnp.int32, (Q_BLOCK, win_keys), 0)
    kpos = ks + lax.broadcasted_iota(jnp.int32, (Q_BLOCK, win_keys), 1)
    band = jnp.abs(qpos - kpos) <= SWA_HALF

    for p in range(N_PAIRS):
        sl = slice(p * LANES, (p + 1) * LANES)
        kv = p // 2
        qc = qc_ref[0, :, sl]
        qd = qd_ref[0, :, sl]
        oc2, od2 = [], []
        for half in (0, 1):
            swap = 0 if kv == half else 1
            ws = slice(swap * LANES, (swap + 1) * LANES)
            keep = lo if half == 0 else jnp.logical_not(lo)
            qm = jnp.where(keep, qc, jnp.zeros_like(qc))
            oc2.append(_attend([(_dot_nt(qm, kc_ref[0, :, ws]), vc_ref[0, :, ws], False),
                                (_dot(qm, cgk[swap]), cgv[swap], True)]))
            qm = jnp.where(keep, qd, jnp.zeros_like(qd))
            s_loc = jnp.where(band, _dot_nt(qm, kd_ref[0, pl.ds(ks, win_keys), ws]), NEG_INF)
            od2.append(_attend([(s_loc, vd_ref[0, pl.ds(ks, win_keys), ws], False),
                                (_dot(qm, csk[swap]), csv[swap], True)], sink_ref[2 * p + half]))
        y_scr[:, sl] = (jnp.where(lo, oc2[0], oc2[1]) * g_ref[0, :, sl]).astype(BF16)
        ys = slice(4 * LANES + p * LANES, 4 * LANES + (p + 1) * LANES)
        y_scr[:, ys] = (jnp.where(lo, od2[0], od2[1]) * g_ref[0, :, ys]).astype(BF16)

    d = x_ref.shape[-1]
    gate = m_ref[pl.ds(1 + b, 1), 2 * d:]
    xo_ref[0] = x_ref[0] + gate * _dot(y_scr[...], wout_ref[...])


def _sample_odd_attn(sink, x, m, qc, kc, vc, qd, kd, vd, g, cgk, cgv, csk, csv, wout):
    nb, s, _ = x.shape
    nq = s // Q_BLOCK

    def blk(w):
        return pl.BlockSpec((1, Q_BLOCK, w), lambda b, j: (b, j, 0))

    def per_batch(a):
        return pl.BlockSpec((1,) + a.shape[1:], lambda b, j: (b, 0, 0))

    return pl.pallas_call(
        _s1b_kernel,
        grid=(nb, nq),
        in_specs=[pl.BlockSpec(memory_space=pltpu.SMEM), blk(D_MODEL), _full(m.shape),
                  blk(512), per_batch(kc), per_batch(vc), blk(512), per_batch(kd), per_batch(vd), blk(1024),
                  per_batch(cgk), per_batch(cgv), per_batch(csk), per_batch(csv), _full(wout.shape)],
        out_specs=blk(D_MODEL),
        out_shape=jax.ShapeDtypeStruct(x.shape, F32),
        scratch_shapes=[pltpu.VMEM((Q_BLOCK, D_MODEL), BF16)],
        compiler_params=pltpu.CompilerParams(dimension_semantics=("arbitrary", "arbitrary"),
                                             vmem_limit_bytes=VMEM_LIMIT),
        name="sample_odd_attn",
    )(sink, x, m, qc, kc, vc, qd, kd, vd, g, cgk, cgv, csk, csv, wout)


def _pad_cols(w, left, width):
    return jnp.pad(w, ((0, 0), (left, width - left - w.shape[1])))


def _per_head_slabs(w, n_heads, head_w, take):
    k = w.shape[0]
    wh = w.reshape(k, n_heads, head_w)[:, :, :take]
    return jnp.pad(wh, ((0, 0), (0, 0), (0, LANES - take))).reshape(k, n_heads * LANES)


def _feature_major(c):
    b, h, l, d = c.shape
    return jnp.swapaxes(c, -1, -2).reshape(b, h * d, l)


def _token_major(c):
    return jnp.swapaxes(c, -1, -2)


def _rope_tables(s, rot_dim, period, start):
    quarter = rot_dim // 4
    half = rot_dim // 2
    t = jnp.arange(s)
    inv = ROPE_THETA ** (-jnp.arange(quarter, dtype=F32) / quarter)
    row = (t // GRID_W).astype(F32)[:, None] * inv
    col = (t % GRID_W).astype(F32)[:, None] * inv
    ang = jnp.concatenate([row, col], axis=-1)
    cos, sin = jnp.cos(ang), jnp.sin(ang)
    zeros = jnp.zeros_like(sin)
    pre = jnp.ones((s, start), F32)
    post = jnp.zeros((s, period - start - rot_dim), F32)
    c = jnp.concatenate([pre, cos, cos, post], axis=-1)
    a = jnp.concatenate([0 * pre, -sin, zeros, post], axis=-1)
    bt = jnp.concatenate([0 * pre, zeros, sin, post], axis=-1)
    rep = LANES // period
    return jnp.tile(c, (1, rep)), jnp.tile(a, (1, rep)), jnp.tile(bt, (1, rep))


def _row(v, width=None):
    v = v.reshape(1, -1).astype(F32)
    return v if width is None else jnp.pad(v, ((0, 0), (0, width - v.shape[1])))


def kernel(x_prompt, x_sample, cache_mla_ckv, cache_mla_krope, cache_na_k, cache_na_v, cache_gqa_k, cache_gqa_v, cache_swa_k, cache_swa_v, c, c_ctx, norm_g, w_mod, b_mod, w_in_even, mla_qa_g, w_q_up, mla_kva_g, w_kv_up, mla_q_g, mla_k_g, na_q_g, na_k_g, na_rpb, w_out_even, w_in_odd, gqa_q_g, gqa_k_g, swa_q_g, swa_k_g, swa_sink, w_out_odd):
    n_dec = x_sample.shape[0]
    assert w_mod.shape[0] == 2 and n_dec + 1 <= 8

    cond8 = jnp.zeros((8, D_MODEL), F32).at[0].set(c_ctx).at[1:1 + n_dec].set(c)
    m_all = _modulation(cond8, w_mod, b_mod)

    we = w_in_even[0]
    q_lat, ckv, krope, ga, qb, kb, vb, gb = jnp.split(we, [256, 384, 416, 928, 1440, 1952, 2464], axis=1)
    win_e = jnp.concatenate([q_lat, ckv, _pad_cols(krope, NOPE_A, LANES), ga, qb, kb, vb, gb], axis=1).astype(BF16)
    wq = _per_head_slabs(w_q_up[0], N_HEADS, QK_A, QK_A).astype(BF16)
    wkk = _per_head_slabs(w_kv_up[0], N_HEADS, NOPE_A + HEAD_DIM, NOPE_A).astype(BF16)
    wkv = w_kv_up[0].reshape(KV_RANK, N_HEADS, NOPE_A + HEAD_DIM)[:, :, NOPE_A:].reshape(KV_RANK, -1).astype(BF16)
    even = (_row(norm_g[0]), win_e, _row(mla_qa_g[0]), wq, _row(mla_kva_g[0]), wkk, wkv,
            _row(mla_q_g[0], LANES), _row(mla_k_g[0], LANES),
            _row(jnp.tile(na_q_g[0], 2)), _row(jnp.tile(na_k_g[0], 2)))
    wout_e = w_out_even[0].astype(BF16)

    win_o = w_in_odd[0].astype(BF16)
    odd = (_row(norm_g[1]), win_o, _row(jnp.tile(gqa_q_g[0], 2)), _row(jnp.tile(gqa_k_g[0], 2)),
           _row(jnp.tile(swa_q_g[0], 2)), _row(jnp.tile(swa_k_g[0], 2)))
    wout_o = w_out_odd[0].astype(BF16)
    sink = swa_sink[0].astype(F32)

    xp1, new_ckv, new_krope, new_na_k, new_na_v = _prompt_even(x_prompt, m_all[0], *even, wout_e)
    xp2, new_gqa_k, new_gqa_v, new_swa_k, new_swa_v = _prompt_odd(sink, xp1, m_all[1], *odd, wout_o)

    cos_e, sa_e, sb_e = _rope_tables(DEC_SEQ, ROPE_A, LANES, NOPE_A)
    qa, ka, va, qbs, kbs, vbs, g0 = _sample_even_proj(x_sample, m_all[0], *even, cos_e, sa_e, sb_e)
    ckr = jnp.swapaxes(cache_mla_krope[:, 0], -1, -2)
    xs1 = _sample_even_attn(na_rpb[0].reshape(-1), x_sample, m_all[0], qa, ka, va, qbs, kbs, vbs, g0,
                            cache_mla_ckv[:, 0], ckr, _feature_major(cache_na_k[:, 0]),
                            _feature_major(cache_na_v[:, 0]), wkk, wkv, even[8], wout_e)
    cos_o, sa_o, sb_o = _rope_tables(DEC_SEQ, HEAD_DIM, HEAD_DIM, 0)
    qc, kc, vc, qd, kd, vd, g1 = _sample_odd_proj(xs1, m_all[1], *odd, cos_o, sa_o, sb_o)
    xs2 = _sample_odd_attn(sink, xs1, m_all[1], qc, kc, vc, qd, kd, vd, g1,
                           _feature_major(cache_gqa_k[:, 0]), _feature_major(cache_gqa_v[:, 0]),
                           _feature_major(cache_swa_k[:, 0]), _feature_major(cache_swa_v[:, 0]), wout_o)

    caches = (new_krope, new_na_k, new_na_v, new_gqa_k, new_gqa_v, new_swa_k, new_swa_v)
    return (xp2, xs2, new_ckv) + tuple(_token_major(c) for c in caches)
```

```python
import functools

import jax
import jax.numpy as jnp
from jax import lax
from jax.experimental import pallas as pl
from jax.experimental.pallas import tpu as pltpu

F32 = jnp.float32
BF16 = jnp.bfloat16

D_MODEL = 1024
SEQ = 256
DEC_SEQ = 1024
PAST_LEN = 256
GRID_W = 64
HEAD_DIM = 64
Q_RANK = 256
KV_RANK = 128
NOPE_A = 64
ROPE_A = 32
QK_A = NOPE_A + ROPE_A
N_HEADS = 8
NA_ROWS = 8
NA_COLS = 16
SWA_HALF = 128
ROPE_THETA = 10000.0
EPS = 1e-6
NEG_INF = -1e30

LANES = 128
Q_BLOCK = 256
PROMPT_BATCHES_PER_STEP = 2
N_PAIRS = N_HEADS // 2
RPB_ROWS = 2 * NA_ROWS - 1
RPB_COLS = 2 * NA_COLS - 1
BIAS_CHUNKS = 16
VMEM_LIMIT = 48 * 1024 * 1024

E_QLAT, E_CKV, E_KROPE, E_GA, E_QB, E_KB, E_VB, E_GB, E_END = 0, 256, 384, 512, 1024, 1536, 2048, 2560, 3072
O_QC, O_KC, O_VC, O_GC, O_QD, O_KD, O_VD, O_GD, O_END = 0, 512, 640, 768, 1280, 1792, 1920, 2048, 2560


def _dot(a, b):
    return lax.dot_general(a, b, (((1,), (0,)), ((), ())), preferred_element_type=F32)


def _dot_nt(a, b):
    return lax.dot_general(a, b, (((1,), (1,)), ((), ())), preferred_element_type=F32)


def _silu(x):
    return x / (1.0 + jnp.exp(-x))


def _rms(x, g, n):
    ss = jnp.sum(x * x, axis=-1, keepdims=True)
    return x * lax.rsqrt(ss / n + EPS) * g


def _rms_halves(x, g2, lo):
    x2 = x * x
    s_lo = jnp.sum(jnp.where(lo, x2, 0.0), axis=-1, keepdims=True)
    s_hi = jnp.sum(jnp.where(lo, 0.0, x2), axis=-1, keepdims=True)
    r = jnp.where(lo, lax.rsqrt(s_lo / HEAD_DIM + EPS), lax.rsqrt(s_hi / HEAD_DIM + EPS))
    return x * r * g2


def _modulate(x, g, m):
    d = x.shape[-1]
    xn = x * lax.rsqrt(jnp.mean(x * x, axis=-1, keepdims=True) + EPS) * g
    return xn * (1.0 + m[:, d:2 * d]) + m[:, :d], m[:, 2 * d:]


def _attend(parts, sink=None):
    mx = None
    for s, _, _ in parts:
        pm = jnp.max(s, axis=-1, keepdims=True)
        mx = pm if mx is None else jnp.maximum(mx, pm)
    if sink is not None:
        mx = jnp.maximum(mx, sink)
    den, out = None, None
    for s, v, v_t in parts:
        e = jnp.exp(s - mx)
        ps = jnp.sum(e, axis=-1, keepdims=True)
        po = (_dot_nt if v_t else _dot)(e.astype(BF16), v)
        den = ps if den is None else den + ps
        out = po if out is None else out + po
    if sink is not None:
        den = den + jnp.exp(sink - mx)
    return out * (1.0 / den)


def _lane_lo():
    return lax.broadcasted_iota(jnp.int32, (1, LANES), 1) < HEAD_DIM


def _store_pair_transposed(ref, bi, p, x):
    xt = x.T
    ref[bi, 0, 2 * p] = xt[:HEAD_DIM]
    ref[bi, 0, 2 * p + 1] = xt[HEAD_DIM:]


def _swap_halves(a):
    return jnp.concatenate([a[HEAD_DIM:], a[:HEAD_DIM]], axis=0)


def _mod_kernel(c_ref, w_ref, b_ref, o_ref):
    s = _silu(c_ref[...])
    o_ref[0] = lax.dot_general(s, w_ref[0], (((1,), (0,)), ((), ())), precision=lax.Precision.HIGHEST,
                               preferred_element_type=F32) + b_ref[0]


def _modulation(cond8, w_mod, b_mod):
    depth = w_mod.shape[0]
    tn = 512
    return pl.pallas_call(
        _mod_kernel,
        grid=(depth, 3 * D_MODEL // tn),
        in_specs=[pl.BlockSpec((8, D_MODEL), lambda l, n: (0, 0)),
                  pl.BlockSpec((1, D_MODEL, tn), lambda l, n: (l, 0, n)),
                  pl.BlockSpec((1, 1, tn), lambda l, n: (l, 0, n))],
        out_specs=pl.BlockSpec((1, 8, tn), lambda l, n: (l, 0, n)),
        out_shape=jax.ShapeDtypeStruct((depth, 8, 3 * D_MODEL), F32),
        name="modulation",
    )(cond8, w_mod, b_mod.reshape(depth, 1, 3 * D_MODEL))


def _mla_keys(cb, kr, wkk_ref, wkv_ref, kg, rope=None):
    kk = _dot(cb, wkk_ref[...])
    keys = []
    for h in range(N_HEADS):
        k = _rms(kk[:, h * LANES:(h + 1) * LANES] + kr, kg, QK_A)
        if rope is not None:
            k = rope(k)
        keys.append(k.astype(BF16))
    return keys, _dot(cb, wkv_ref[...]).astype(BF16)


def _p0_kernel(x_ref, m_ref, ng_ref, win_ref, qag_ref, wq_ref, kvag_ref, wkk_ref, wkv_ref, qg_ref, kg_ref,
               naqg_ref, nakg_ref, wout_ref,
               xo_ref, ckv_ref, krope_ref, nak_ref, nav_ref, y_scr):
    nbs = x_ref.shape[0]
    x = x_ref[...].reshape(nbs * SEQ, D_MODEL)
    h, gate = _modulate(x, ng_ref[...], m_ref[0:1, :])
    hb = h.astype(BF16)
    lo = _lane_lo()
    hi = jnp.logical_not(lo)
    rows = [slice(bi * SEQ, (bi + 1) * SEQ) for bi in range(nbs)]

    qln = _rms(_dot(hb, win_ref[:, E_QLAT:E_CKV]), qag_ref[...], Q_RANK).astype(BF16)
    q_all = _dot(qln, wq_ref[...])
    ckv_n = _rms(_dot(hb, win_ref[:, E_CKV:E_KROPE]), kvag_ref[...], KV_RANK)
    kr = _dot(hb, win_ref[:, E_KROPE:E_GA])
    for bi, rs in enumerate(rows):
        ckv_ref[bi, 0] = ckv_n[rs]
        krope_ref[bi, 0] = kr[rs].T[NOPE_A:QK_A]
    keys, vals = _mla_keys(ckv_n.astype(BF16), kr, wkk_ref, wkv_ref, kg_ref[...])
    qg = qg_ref[...] * (QK_A ** -0.5)

    ga = _dot(hb, win_ref[:, E_GA:E_QB])
    zq = _dot(hb, win_ref[:, E_QB:E_KB])
    zk = _dot(hb, win_ref[:, E_KB:E_VB])
    zv = _dot(hb, win_ref[:, E_VB:E_GB])
    gb = _dot(hb, win_ref[:, E_GB:E_END])
    naqg = naqg_ref[...] * (HEAD_DIM ** -0.5)

    for p in range(N_PAIRS):
        sl = slice(p * LANES, (p + 1) * LANES)
        ys = slice(4 * LANES + p * LANES, 4 * LANES + (p + 1) * LANES)
        qhs = [_rms(q_all[:, hh * LANES:(hh + 1) * LANES], qg, QK_A).astype(BF16) for hh in (2 * p, 2 * p + 1)]
        qb = _rms_halves(zq[:, sl], naqg, lo)
        kb = _rms_halves(zk[:, sl], nakg_ref[...], lo)
        vb = zv[:, sl]
        kbb, vbb = kb.astype(BF16), vb.astype(BF16)
        qms = [jnp.where(keep, qb, 0.0).astype(BF16) for keep in (lo, hi)]
        for bi, rs in enumerate(rows):
            o2 = [_attend([(_dot_nt(qhs[i][rs], keys[2 * p + i][rs]), vals[rs, sl], False)]) for i in (0, 1)]
            y_scr[rs, sl] = (jnp.where(lo, o2[0], o2[1]) * _silu(ga[rs, sl])).astype(BF16)
            _store_pair_transposed(nak_ref, bi, p, kb[rs])
            _store_pair_transposed(nav_ref, bi, p, vb[rs])
            o2 = [_attend([(_dot_nt(qms[i][rs], kbb[rs]), vbb[rs], False)]) for i in (0, 1)]
            y_scr[rs, ys] = (jnp.where(lo, o2[0], o2[1]) * _silu(gb[rs, sl])).astype(BF16)

    xo_ref[...] = (x + gate * _dot(y_scr[...], wout_ref[...])).reshape(nbs, SEQ, D_MODEL)


def _full(shape):
    n = len(shape)
    return pl.BlockSpec(shape, lambda *_: (0,) * n)


def _prompt_even(x, m, ng, win, qag, wq, kvag, wkk, wkv, qg, kg, naqg, nakg, wout):
    nb = x.shape[0]
    nbs = PROMPT_BATCHES_PER_STEP
    assert nb % nbs == 0
    ins = (m, ng, win, qag, wq, kvag, wkk, wkv, qg, kg, naqg, nakg, wout)
    return pl.pallas_call(
        _p0_kernel,
        grid=(nb // nbs,),
        in_specs=[pl.BlockSpec((nbs, SEQ, D_MODEL), lambda b: (b, 0, 0))] + [_full(a.shape) for a in ins],
        out_specs=[pl.BlockSpec((nbs, SEQ, D_MODEL), lambda b: (b, 0, 0)),
                   pl.BlockSpec((nbs, 1, SEQ, KV_RANK), lambda b: (b, 0, 0, 0)),
                   pl.BlockSpec((nbs, 1, ROPE_A, SEQ), lambda b: (b, 0, 0, 0)),
                   pl.BlockSpec((nbs, 1, N_HEADS, HEAD_DIM, SEQ), lambda b: (b, 0, 0, 0, 0)),
                   pl.BlockSpec((nbs, 1, N_HEADS, HEAD_DIM, SEQ), lambda b: (b, 0, 0, 0, 0))],
        out_shape=[jax.ShapeDtypeStruct((nb, SEQ, D_MODEL), F32),
                   jax.ShapeDtypeStruct((nb, 1, SEQ, KV_RANK), F32),
                   jax.ShapeDtypeStruct((nb, 1, ROPE_A, SEQ), F32),
                   jax.ShapeDtypeStruct((nb, 1, N_HEADS, HEAD_DIM, SEQ), F32),
                   jax.ShapeDtypeStruct((nb, 1, N_HEADS, HEAD_DIM, SEQ), F32)],
        scratch_shapes=[pltpu.VMEM((nbs * SEQ, D_MODEL), BF16)],
        compiler_params=pltpu.CompilerParams(dimension_semantics=("arbitrary",), vmem_limit_bytes=VMEM_LIMIT),
        name="prompt_even",
    )(x, *ins)


def _gqa_pair_operands(k, v, kg2, lo):
    kn = _rms_halves(k, kg2, lo)
    return kn, (kn.astype(BF16), pltpu.roll(kn, HEAD_DIM, 1).astype(BF16)), \
        (v.astype(BF16), pltpu.roll(v, HEAD_DIM, 1).astype(BF16))


def _p1_kernel(sink_ref, x_ref, m_ref, ng_ref, win_ref, gqg_ref, gkg_ref, sqg_ref, skg_ref, wout_ref,
               xo_ref, gk_ref, gv_ref, sk_ref, sv_ref, y_scr):
    nbs = x_ref.shape[0]
    x = x_ref[...].reshape(nbs * SEQ, D_MODEL)
    h, gate = _modulate(x, ng_ref[...], m_ref[0:1, :])
    hb = h.astype(BF16)
    lo = _lane_lo()
    hi = jnp.logical_not(lo)
    sc = HEAD_DIM ** -0.5
    rows = [slice(bi * SEQ, (bi + 1) * SEQ) for bi in range(nbs)]

    branches = ((O_QC, O_KC, O_VC, O_GC, gqg_ref, gkg_ref, gk_ref, gv_ref, False, 0),
                (O_QD, O_KD, O_VD, O_GD, sqg_ref, skg_ref, sk_ref, sv_ref, True, 4 * LANES))
    for oq, ok, ov, og, qg_ref, kg_ref, ck_ref, cv_ref, has_sink, yoff in branches:
        zq = _dot(hb, win_ref[:, oq:oq + 4 * LANES])
        zkv = _dot(hb, win_ref[:, ok:ok + 2 * LANES])
        zg = _dot(hb, win_ref[:, og:og + 4 * LANES])
        v = zkv[:, LANES:]
        kn, ks, vs = _gqa_pair_operands(zkv[:, :LANES], v, kg_ref[...], lo)
        for bi, rs in enumerate(rows):
            _store_pair_transposed(ck_ref, bi, 0, kn[rs])
            _store_pair_transposed(cv_ref, bi, 0, v[rs])
        qg = qg_ref[...] * sc
        for p in range(N_PAIRS):
            sl = slice(p * LANES, (p + 1) * LANES)
            qn = _rms_halves(zq[:, sl], qg, lo)
            qms = [jnp.where(keep, qn, 0.0).astype(BF16) for keep in (lo, hi)]
            kv = p // 2
            for bi, rs in enumerate(rows):
                o2 = []
                for half in (0, 1):
                    swap = 0 if kv == half else 1
                    sink = sink_ref[2 * p + half] if has_sink else None
                    o2.append(_attend([(_dot_nt(qms[half][rs], ks[swap][rs]), vs[swap][rs], False)], sink))
                o = jnp.where(lo, o2[0], o2[1])
                y_scr[rs, yoff + p * LANES:yoff + (p + 1) * LANES] = (o * _silu(zg[rs, sl])).astype(BF16)

    xo_ref[...] = (x + gate * _dot(y_scr[...], wout_ref[...])).reshape(nbs, SEQ, D_MODEL)


def _prompt_odd(sink, x, m, ng, win, gqg, gkg, sqg, skg, wout):
    nb = x.shape[0]
    nbs = PROMPT_BATCHES_PER_STEP
    assert nb % nbs == 0
    ins = (m, ng, win, gqg, gkg, sqg, skg, wout)
    cache_spec = pl.BlockSpec((nbs, 1, 2, HEAD_DIM, SEQ), lambda b: (b, 0, 0, 0, 0))
    cache_shape = jax.ShapeDtypeStruct((nb, 1, 2, HEAD_DIM, SEQ), F32)
    return pl.pallas_call(
        _p1_kernel,
        grid=(nb // nbs,),
        in_specs=[pl.BlockSpec(memory_space=pltpu.SMEM),
                  pl.BlockSpec((nbs, SEQ, D_MODEL), lambda b: (b, 0, 0))] + [_full(a.shape) for a in ins],
        out_specs=[pl.BlockSpec((nbs, SEQ, D_MODEL), lambda b: (b, 0, 0))] + [cache_spec] * 4,
        out_shape=[jax.ShapeDtypeStruct((nb, SEQ, D_MODEL), F32)] + [cache_shape] * 4,
        scratch_shapes=[pltpu.VMEM((nbs * SEQ, D_MODEL), BF16)],
        compiler_params=pltpu.CompilerParams(dimension_semantics=("arbitrary",), vmem_limit_bytes=VMEM_LIMIT),
        name="prompt_odd",
    )(sink, x, *ins)


def _s0a_kernel(x_ref, m_ref, ng_ref, win_ref, qag_ref, wq_ref, kvag_ref, wkk_ref, wkv_ref, qg_ref, kg_ref,
                naqg_ref, nakg_ref, cos_ref, sa_ref, sb_ref,
                qa_ref, ka_ref, va_ref, qb_ref, kb_ref, vb_ref, g_ref):
    b = pl.program_id(0)
    x = x_ref[0]
    h, _ = _modulate(x, ng_ref[...], m_ref[pl.ds(1 + b, 1), :])
    hb = h.astype(BF16)
    lo = _lane_lo()
    cos, sa, sb = cos_ref[...], sa_ref[...], sb_ref[...]

    def rope(t):
        return t * cos + pltpu.roll(t, LANES - ROPE_A // 2, 1) * sa + pltpu.roll(t, ROPE_A // 2, 1) * sb

    qln = _rms(_dot(hb, win_ref[:, E_QLAT:E_CKV]), qag_ref[...], Q_RANK).astype(BF16)
    q_all = _dot(qln, wq_ref[...])
    qg = qg_ref[...] * (QK_A ** -0.5)
    for hh in range(N_HEADS):
        sl = slice(hh * LANES, (hh + 1) * LANES)
        qa_ref[0, :, sl] = rope(_rms(q_all[:, sl], qg, QK_A)).astype(BF16)

    ckv_n = _rms(_dot(hb, win_ref[:, E_CKV:E_KROPE]), kvag_ref[...], KV_RANK)
    kr = _dot(hb, win_ref[:, E_KROPE:E_GA])
    keys, vals = _mla_keys(ckv_n.astype(BF16), kr, wkk_ref, wkv_ref, kg_ref[...], rope)
    for hh in range(N_HEADS):
        ka_ref[0, :, hh * LANES:(hh + 1) * LANES] = keys[hh]
    va_ref[0] = vals

    zq = _dot(hb, win_ref[:, E_QB:E_KB])
    zk = _dot(hb, win_ref[:, E_KB:E_VB])
    naqg = naqg_ref[...] * (HEAD_DIM ** -0.5)
    for p in range(N_PAIRS):
        sl = slice(p * LANES, (p + 1) * LANES)
        qb_ref[0, :, sl] = _rms_halves(zq[:, sl], naqg, lo).astype(BF16)
        kb_ref[0, :, sl] = _rms_halves(zk[:, sl], nakg_ref[...], lo).astype(BF16)
    vb_ref[0] = _dot(hb, win_ref[:, E_VB:E_GB]).astype(BF16)
    g_ref[0, :, 0:4 * LANES] = _silu(_dot(hb, win_ref[:, E_GA:E_QB]))
    g_ref[0, :, 4 * LANES:8 * LANES] = _silu(_dot(hb, win_ref[:, E_GB:E_END]))


def _sample_even_proj(x, m, ng, win, qag, wq, kvag, wkk, wkv, qg, kg, naqg, nakg, cos, sa, sb):
    nb, s, _ = x.shape
    nq = s // Q_BLOCK
    ins = (m, ng, win, qag, wq, kvag, wkk, wkv, qg, kg, naqg, nakg)
    tab = pl.BlockSpec((Q_BLOCK, LANES), lambda b, j: (j, 0))

    def blk(w):
        return pl.BlockSpec((1, Q_BLOCK, w), lambda b, j: (b, j, 0))

    def shp(w, dt):
        return jax.ShapeDtypeStruct((nb, s, w), dt)

    return pl.pallas_call(
        _s0a_kernel,
        grid=(nb, nq),
        in_specs=[blk(D_MODEL)] + [_full(a.shape) for a in ins] + [tab, tab, tab],
        out_specs=[blk(1024), blk(1024), blk(512), blk(512), blk(512), blk(512), blk(1024)],
        out_shape=[shp(1024, BF16), shp(1024, BF16), shp(512, BF16), shp(512, BF16), shp(512, BF16),
                   shp(512, BF16), shp(1024, F32)],
        compiler_params=pltpu.CompilerParams(dimension_semantics=("arbitrary", "arbitrary"),
                                             vmem_limit_bytes=VMEM_LIMIT),
        name="sample_even_proj",
    )(x, *ins, cos, sa, sb)


def _build_bias_table(rpb_ref, tile_scr, tab_ref):
    qc = lax.broadcasted_iota(jnp.int32, (GRID_W, LANES), 0)
    lane = lax.broadcasted_iota(jnp.int32, (GRID_W, LANES), 1)
    kc = jnp.bitwise_and(lane, GRID_W - 1)
    lo = lane < GRID_W
    diff = kc - qc + (NA_COLS - 1)
    cs = jnp.clip(qc - NA_COLS // 2, 0, GRID_W - NA_COLS)
    valid = (kc >= cs) & (kc < cs + NA_COLS)
    tab_ref[...] = jnp.zeros(tab_ref.shape, F32)
    tile_scr[RPB_ROWS] = jnp.zeros((GRID_W, LANES), F32)

    def per_head(h, carry):
        for dr in range(RPB_ROWS):
            t = jnp.zeros((GRID_W, LANES), F32)
            for dc in range(RPB_COLS):
                t = jnp.where(diff == dc, rpb_ref[(h * RPB_ROWS + dr) * RPB_COLS + dc], t)
            tile_scr[dr] = jnp.where(valid, t, NEG_INF)
        for c in range(NA_ROWS // 2, NA_ROWS // 2 + NA_ROWS):
            d0 = 2 * c - NA_ROWS
            tab_ref[0, h, c] = jnp.where(lo, tile_scr[d0], tile_scr[d0 + 1])
            tab_ref[1, h, c] = jnp.where(lo, tile_scr[d0 - 1 if d0 > 0 else RPB_ROWS], tile_scr[d0])
        return carry

    lax.fori_loop(0, N_HEADS, per_head, 0)


def _s0b_kernel(rpb_ref, x_ref, m_ref, qa_ref, ka_ref, va_ref, qb_ref, kb_ref, vb_ref, g_ref,
                cckv_ref, ckr_ref, cnk_ref, cnv_ref, wkk_ref, wkv_ref, kg_ref, wout_ref,
                xo_ref, kca_scr, vca_scr, tile_scr, tab_scr, y_scr):
    b = pl.program_id(0)
    j = pl.program_id(1)
    lo = _lane_lo()
    n_lat = ka_ref.shape[1]

    @pl.when((b == 0) & (j == 0))
    def _():
        _build_bias_table(rpb_ref, tile_scr, tab_scr)

    @pl.when(j == 0)
    def _():
        kr_t = jnp.concatenate([jnp.zeros((NOPE_A, PAST_LEN), F32), ckr_ref[0],
                                jnp.zeros((LANES - QK_A, PAST_LEN), F32)], axis=0)
        keys, vals = _mla_keys(cckv_ref[0].astype(BF16), kr_t.T, wkk_ref, wkv_ref, kg_ref[...])
        for hh in range(N_HEADS):
            kca_scr[:, hh * LANES:(hh + 1) * LANES] = keys[hh]
        vca_scr[...] = vals

    kidx = lax.broadcasted_iota(jnp.int32, (1, n_lat), 1)
    for p in range(N_PAIRS):
        sl = slice(p * LANES, (p + 1) * LANES)
        o2 = []
        for hh in (2 * p, 2 * p + 1):
            hs = slice(hh * LANES, (hh + 1) * LANES)
            q = qa_ref[0, :, hs]
            o2.append(_attend([(_dot_nt(q, ka_ref[0, :, hs]), va_ref[0, :, sl], False),
                               (_dot_nt(q, kca_scr[:, hs]), vca_scr[:, sl], False)]))
        oa = jnp.where(lo, o2[0], o2[1])
        y_scr[:, sl] = (oa * g_ref[0, :, sl]).astype(BF16)

        qb = qb_ref[0, :, sl]
        kb = kb_ref[0, :, sl]
        vb = vb_ref[0, :, sl]
        kcb = cnk_ref[0, sl, :].astype(BF16)
        vcb = cnv_ref[0, sl, :].astype(BF16)
        o2 = []
        for half in (0, 1):
            head = 2 * p + half
            qm = jnp.where(lo if half == 0 else jnp.logical_not(lo), qb, jnp.zeros_like(qb))
            s_lat = _dot_nt(qm, kb)
            rows = []
            for local in range(Q_BLOCK // GRID_W):
                qr = j * (Q_BLOCK // GRID_W) + local
                par = 0 if local % 2 == 1 else 1
                c0 = (RPB_ROWS + par - local) // 2 - 2 * j
                bias = jnp.concatenate([tab_scr[par, head, c0 + t] for t in range(n_lat // LANES)], axis=1)
                r0 = jnp.clip(qr - NA_ROWS // 2, 0, n_lat // GRID_W - NA_ROWS) * GRID_W
                ok = (kidx >= r0) & (kidx < r0 + NA_ROWS * GRID_W)
                rows.append(jnp.where(ok, s_lat[local * GRID_W:(local + 1) * GRID_W] + bias, NEG_INF))
            s_lat = jnp.concatenate(rows, axis=0)
            o2.append(_attend([(s_lat, vb, False), (_dot(qm, kcb), vcb, True)]))
        ob = jnp.where(lo, o2[0], o2[1])
        ys = slice(4 * LANES + p * LANES, 4 * LANES + (p + 1) * LANES)
        y_scr[:, ys] = (ob * g_ref[0, :, ys]).astype(BF16)

    d = x_ref.shape[-1]
    gate = m_ref[pl.ds(1 + b, 1), 2 * d:]
    xo_ref[0] = x_ref[0] + gate * _dot(y_scr[...], wout_ref[...])


def _sample_even_attn(rpb, x, m, qa, ka, va, qb, kb, vb, g, cckv, ckr, cnk, cnv, wkk, wkv, kg, wout):
    nb, s, _ = x.shape
    nq = s // Q_BLOCK

    def blk(w):
        return pl.BlockSpec((1, Q_BLOCK, w), lambda b, j: (b, j, 0))

    def per_batch(a):
        return pl.BlockSpec((1,) + a.shape[1:], lambda b, j: (b, 0, 0))

    return pl.pallas_call(
        _s0b_kernel,
        grid=(nb, nq),
        in_specs=[pl.BlockSpec(memory_space=pltpu.SMEM), blk(D_MODEL), _full(m.shape),
                  blk(1024), per_batch(ka), per_batch(va), blk(512), per_batch(kb), per_batch(vb), blk(1024),
                  per_batch(cckv), per_batch(ckr), per_batch(cnk), per_batch(cnv),
                  _full(wkk.shape), _full(wkv.shape), _full(kg.shape), _full(wout.shape)],
        out_specs=blk(D_MODEL),
        out_shape=jax.ShapeDtypeStruct(x.shape, F32),
        scratch_shapes=[pltpu.VMEM((PAST_LEN, N_HEADS * LANES), BF16),
                        pltpu.VMEM((PAST_LEN, N_HEADS * HEAD_DIM), BF16),
                        pltpu.VMEM((RPB_ROWS + 1, GRID_W, LANES), F32),
                        pltpu.VMEM((2, N_HEADS, BIAS_CHUNKS, GRID_W, LANES), F32),
                        pltpu.VMEM((Q_BLOCK, D_MODEL), BF16)],
        compiler_params=pltpu.CompilerParams(dimension_semantics=("arbitrary", "arbitrary"),
                                             vmem_limit_bytes=VMEM_LIMIT),
        name="sample_even_attn",
    )(rpb, x, m, qa, ka, va, qb, kb, vb, g, cckv, ckr, cnk, cnv, wkk, wkv, kg, wout)


def _s1a_kernel(x_ref, m_ref, ng_ref, win_ref, gqg_ref, gkg_ref, sqg_ref, skg_ref, cos_ref, sa_ref, sb_ref,
                qc_ref, kc_ref, vc_ref, qd_ref, kd_ref, vd_ref, g_ref):
    b = pl.program_id(0)
    x = x_ref[0]
    h, _ = _modulate(x, ng_ref[...], m_ref[pl.ds(1 + b, 1), :])
    hb = h.astype(BF16)
    lo = _lane_lo()
    cos, sa, sb = cos_ref[...], sa_ref[...], sb_ref[...]
    sc = HEAD_DIM ** -0.5

    def rope(t):
        return t * cos + pltpu.roll(t, LANES - HEAD_DIM // 2, 1) * sa + pltpu.roll(t, HEAD_DIM // 2, 1) * sb

    branches = ((O_QC, O_KC, O_GC, gqg_ref, gkg_ref, qc_ref, kc_ref, vc_ref, 0),
                (O_QD, O_KD, O_GD, sqg_ref, skg_ref, qd_ref, kd_ref, vd_ref, 4 * LANES))
    for oq, ok, og, qg_ref, kg_ref, q_out, k_out, v_out, goff in branches:
        zq = _dot(hb, win_ref[:, oq:oq + 4 * LANES])
        zkv = _dot(hb, win_ref[:, ok:ok + 2 * LANES])
        qg = qg_ref[...] * sc
        for p in range(N_PAIRS):
            sl = slice(p * LANES, (p + 1) * LANES)
            q_out[0, :, sl] = rope(_rms_halves(zq[:, sl], qg, lo)).astype(BF16)
        kn = rope(_rms_halves(zkv[:, :LANES], kg_ref[...], lo))
        v = zkv[:, LANES:]
        k_out[0, :, 0:LANES] = kn.astype(BF16)
        k_out[0, :, LANES:2 * LANES] = pltpu.roll(kn, HEAD_DIM, 1).astype(BF16)
        v_out[0, :, 0:LANES] = v.astype(BF16)
        v_out[0, :, LANES:2 * LANES] = pltpu.roll(v, HEAD_DIM, 1).astype(BF16)
        g_ref[0, :, goff:goff + 4 * LANES] = _silu(_dot(hb, win_ref[:, og:og + 4 * LANES]))


def _sample_odd_proj(x, m, ng, win, gqg, gkg, sqg, skg, cos, sa, sb):
    nb, s, _ = x.shape
    nq = s // Q_BLOCK
    ins = (m, ng, win, gqg, gkg, sqg, skg)
    tab = pl.BlockSpec((Q_BLOCK, LANES), lambda b, j: (j, 0))

    def blk(w):
        return pl.BlockSpec((1, Q_BLOCK, w), lambda b, j: (b, j, 0))

    def shp(w, dt):
        return jax.ShapeDtypeStruct((nb, s, w), dt)

    return pl.pallas_call(
        _s1a_kernel,
        grid=(nb, nq),
        in_specs=[blk(D_MODEL)] + [_full(a.shape) for a in ins] + [tab, tab, tab],
        out_specs=[blk(512), blk(256), blk(256), blk(512), blk(256), blk(256), blk(1024)],
        out_shape=[shp(512, BF16), shp(256, BF16), shp(256, BF16), shp(512, BF16), shp(256, BF16),
                   shp(256, BF16), shp(1024, F32)],
        compiler_params=pltpu.CompilerParams(dimension_semantics=("arbitrary", "arbitrary"),
                                             vmem_limit_bytes=VMEM_LIMIT),
        name="sample_odd_proj",
    )(x, *ins, cos, sa, sb)


def _s1b_kernel(sink_ref, x_ref, m_ref, qc_ref, kc_ref, vc_ref, qd_ref, kd_ref, vd_ref, g_ref,
                cgk_ref, cgv_ref, csk_ref, csv_ref, wout_ref, xo_ref, y_scr):
    b = pl.program_id(0)
    j = pl.program_id(1)
    lo = _lane_lo()
    n_lat = kc_ref.shape[1]
    win_keys = Q_BLOCK + 2 * SWA_HALF

    def ctx_pair(ref):
        a = ref[0].astype(BF16)
        return a, _swap_halves(a)

    cgk, cgv, csk, csv = ctx_pair(cgk_ref), ctx_pair(cgv_ref), ctx_pair(csk_ref), ctx_pair(csv_ref)

    ks = pl.multiple_of(jnp.clip(j * Q_BLOCK - SWA_HALF, 0, n_lat - win_keys), SWA_HALF)
    qpos = j * Q_BLOCK + lax.broadcasted_iota(jnp.int32, (Q_BLOCK, win_keys), 0)
    kpos = ks + lax.broadcasted_iota(jnp.int32, (Q_BLOCK, win_keys), 1)
    band = jnp.abs(qpos - kpos) <= SWA_HALF

    for p in range(N_PAIRS):
        sl = slice(p * LANES, (p + 1) * LANES)
        kv = p // 2
        qc = qc_ref[0, :, sl]
        qd = qd_ref[0, :, sl]
        oc2, od2 = [], []
        for half in (0, 1):
            swap = 0 if kv == half else 1
            ws = slice(swap * LANES, (swap + 1) * LANES)
            keep = lo if half == 0 else jnp.logical_not(lo)
            qm = jnp.where(keep, qc, jnp.zeros_like(qc))
            oc2.append(_attend([(_dot_nt(qm, kc_ref[0, :, ws]), vc_ref[0, :, ws], False),
                                (_dot(qm, cgk[swap]), cgv[swap], True)]))
            qm = jnp.where(keep, qd, jnp.zeros_like(qd))
            s_loc = jnp.where(band, _dot_nt(qm, kd_ref[0, pl.ds(ks, win_keys), ws]), NEG_INF)
            od2.append(_attend([(s_loc, vd_ref[0, pl.ds(ks, win_keys), ws], False),
                                (_dot(qm, csk[swap]), csv[swap], True)], sink_ref[2 * p + half]))
        y_scr[:, sl] = (jnp.where(lo, oc2[0], oc2[1]) * g_ref[0, :, sl]).astype(BF16)
        ys = slice(4 * LANES + p * LANES, 4 * LANES + (p + 1) * LANES)
        y_scr[:, ys] = (jnp.where(lo, od2[0], od2[1]) * g_ref[0, :, ys]).astype(BF16)

    d = x_ref.shape[-1]
    gate = m_ref[pl.ds(1 + b, 1), 2 * d:]
    xo_ref[0] = x_ref[0] + gate * _dot(y_scr[...], wout_ref[...])


def _sample_odd_attn(sink, x, m, qc, kc, vc, qd, kd, vd, g, cgk, cgv, csk, csv, wout):
    nb, s, _ = x.shape
    nq = s // Q_BLOCK

    def blk(w):
        return pl.BlockSpec((1, Q_BLOCK, w), lambda b, j: (b, j, 0))

    def per_batch(a):
        return pl.BlockSpec((1,) + a.shape[1:], lambda b, j: (b, 0, 0))

    return pl.pallas_call(
        _s1b_kernel,
        grid=(nb, nq),
        in_specs=[pl.BlockSpec(memory_space=pltpu.SMEM), blk(D_MODEL), _full(m.shape),
                  blk(512), per_batch(kc), per_batch(vc), blk(512), per_batch(kd), per_batch(vd), blk(1024),
                  per_batch(cgk), per_batch(cgv), per_batch(csk), per_batch(csv), _full(wout.shape)],
        out_specs=blk(D_MODEL),
        out_shape=jax.ShapeDtypeStruct(x.shape, F32),
        scratch_shapes=[pltpu.VMEM((Q_BLOCK, D_MODEL), BF16)],
        compiler_params=pltpu.CompilerParams(dimension_semantics=("arbitrary", "arbitrary"),
                                             vmem_limit_bytes=VMEM_LIMIT),
        name="sample_odd_attn",
    )(sink, x, m, qc, kc, vc, qd, kd, vd, g, cgk, cgv, csk, csv, wout)


def _pad_cols(w, left, width):
    return jnp.pad(w, ((0, 0), (left, width - left - w.shape[1])))


def _per_head_slabs(w, n_heads, head_w, take):
    k = w.shape[0]
    wh = w.reshape(k, n_heads, head_w)[:, :, :take]
    return jnp.pad(wh, ((0, 0), (0, 0), (0, LANES - take))).reshape(k, n_heads * LANES)


def _feature_major(c):
    b, h, l, d = c.shape
    return jnp.swapaxes(c, -1, -2).reshape(b, h * d, l)


def _token_major(c):
    return jnp.swapaxes(c, -1, -2)


def _rope_tables(s, rot_dim, period, start):
    quarter = rot_dim // 4
    half = rot_dim // 2
    t = jnp.arange(s)
    inv = ROPE_THETA ** (-jnp.arange(quarter, dtype=F32) / quarter)
    row = (t // GRID_W).astype(F32)[:, None] * inv
    col = (t % GRID_W).astype(F32)[:, None] * inv
    ang = jnp.concatenate([row, col], axis=-1)
    cos, sin = jnp.cos(ang), jnp.sin(ang)
    zeros = jnp.zeros_like(sin)
    pre = jnp.ones((s, start), F32)
    post = jnp.zeros((s, period - start - rot_dim), F32)
    c = jnp.concatenate([pre, cos, cos, post], axis=-1)
    a = jnp.concatenate([0 * pre, -sin, zeros, post], axis=-1)
    bt = jnp.concatenate([0 * pre, zeros, sin, post], axis=-1)
    rep = LANES // period
    return jnp.tile(c, (1, rep)), jnp.tile(a, (1, rep)), jnp.tile(bt, (1, rep))


def _row(v, width=None):
    v = v.reshape(1, -1).astype(F32)
    return v if width is None else jnp.pad(v, ((0, 0), (0, width - v.shape[1])))


def kernel(x_prompt, x_sample, cache_mla_ckv, cache_mla_krope, cache_na_k, cache_na_v, cache_gqa_k, cache_gqa_v, cache_swa_k, cache_swa_v, c, c_ctx, norm_g, w_mod, b_mod, w_in_even, mla_qa_g, w_q_up, mla_kva_g, w_kv_up, mla_q_g, mla_k_g, na_q_g, na_k_g, na_rpb, w_out_even, w_in_odd, gqa_q_g, gqa_k_g, swa_q_g, swa_k_g, swa_sink, w_out_odd):
    n_dec = x_sample.shape[0]
    assert w_mod.shape[0] == 2 and n_dec + 1 <= 8

    cond8 = jnp.zeros((8, D_MODEL), F32).at[0].set(c_ctx).at[1:1 + n_dec].set(c)
    m_all = _modulation(cond8, w_mod, b_mod)

    we = w_in_even[0]
    q_lat, ckv, krope, ga, qb, kb, vb, gb = jnp.split(we, [256, 384, 416, 928, 1440, 1952, 2464], axis=1)
    win_e = jnp.concatenate([q_lat, ckv, _pad_cols(krope, NOPE_A, LANES), ga, qb, kb, vb, gb], axis=1).astype(BF16)
    wq = _per_head_slabs(w_q_up[0], N_HEADS, QK_A, QK_A).astype(BF16)
    wkk = _per_head_slabs(w_kv_up[0], N_HEADS, NOPE_A + HEAD_DIM, NOPE_A).astype(BF16)
    wkv = w_kv_up[0].reshape(KV_RANK, N_HEADS, NOPE_A + HEAD_DIM)[:, :, NOPE_A:].reshape(KV_RANK, -1).astype(BF16)
    even = (_row(norm_g[0]), win_e, _row(mla_qa_g[0]), wq, _row(mla_kva_g[0]), wkk, wkv,
            _row(mla_q_g[0], LANES), _row(mla_k_g[0], LANES),
            _row(jnp.tile(na_q_g[0], 2)), _row(jnp.tile(na_k_g[0], 2)))
    wout_e = w_out_even[0].astype(BF16)

    win_o = w_in_odd[0].astype(BF16)
    odd = (_row(norm_g[1]), win_o, _row(jnp.tile(gqa_q_g[0], 2)), _row(jnp.tile(gqa_k_g[0], 2)),
           _row(jnp.tile(swa_q_g[0], 2)), _row(jnp.tile(swa_k_g[0], 2)))
    wout_o = w_out_odd[0].astype(BF16)
    sink = swa_sink[0].astype(F32)

    xp1, new_ckv, new_krope, new_na_k, new_na_v = _prompt_even(x_prompt, m_all[0], *even, wout_e)
    xp2, new_gqa_k, new_gqa_v, new_swa_k, new_swa_v = _prompt_odd(sink, xp1, m_all[1], *odd, wout_o)

    cos_e, sa_e, sb_e = _rope_tables(DEC_SEQ, ROPE_A, LANES, NOPE_A)
    qa, ka, va, qbs, kbs, vbs, g0 = _sample_even_proj(x_sample, m_all[0], *even, cos_e, sa_e, sb_e)
    ckr = jnp.swapaxes(cache_mla_krope[:, 0], -1, -2)
    xs1 = _sample_even_attn(na_rpb[0].reshape(-1), x_sample, m_all[0], qa, ka, va, qbs, kbs, vbs, g0,
                            cache_mla_ckv[:, 0], ckr, _feature_major(cache_na_k[:, 0]),
                            _feature_major(cache_na_v[:, 0]), wkk, wkv, even[8], wout_e)
    cos_o, sa_o, sb_o = _rope_tables(DEC_SEQ, HEAD_DIM, HEAD_DIM, 0)
    qc, kc, vc, qd, kd, vd, g1 = _sample_odd_proj(xs1, m_all[1], *odd, cos_o, sa_o, sb_o)
    xs2 = _sample_odd_attn(sink, xs1, m_all[1], qc, kc, vc, qd, kd, vd, g1,
                           _feature_major(cache_gqa_k[:, 0]), _feature_major(cache_gqa_v[:, 0]),
                           _feature_major(cache_swa_k[:, 0]), _feature_major(cache_swa_v[:, 0]), wout_o)

    caches = (new_krope, new_na_k, new_na_v, new_gqa_k, new_gqa_v, new_swa_k, new_swa_v)
    return (xp2, xs2, new_ckv) + tuple(_token_major(c) for c in caches)
```

```python
import functools

import jax
import jax.numpy as jnp
from jax import lax
from jax.experimental import pallas as pl
from jax.experimental.pallas import tpu as pltpu

F32 = jnp.float32
BF16 = jnp.bfloat16

D_MODEL = 1024
SEQ = 256
DEC_SEQ = 1024
PAST_LEN = 256
GRID_W = 64
HEAD_DIM = 64
Q_RANK = 256
KV_RANK = 128
NOPE_A = 64
ROPE_A = 32
QK_A = NOPE_A + ROPE_A
N_HEADS = 8
NA_ROWS = 8
NA_COLS = 16
SWA_HALF = 128
ROPE_THETA = 10000.0
EPS = 1e-6
NEG_INF = -1e30

LANES = 128
Q_BLOCK = 256
PROMPT_BATCHES_PER_STEP = 2
N_PAIRS = N_HEADS // 2
RPB_ROWS = 2 * NA_ROWS - 1
RPB_COLS = 2 * NA_COLS - 1
BIAS_CHUNKS = 16
VMEM_LIMIT = 48 * 1024 * 1024

E_QLAT, E_CKV, E_KROPE, E_GA, E_QB, E_KB, E_VB, E_GB, E_END = 0, 256, 384, 512, 1024, 1536, 2048, 2560, 3072
O_QC, O_KC, O_VC, O_GC, O_QD, O_KD, O_VD, O_GD, O_END = 0, 512, 640, 768, 1280, 1792, 1920, 2048, 2560


def _dot(a, b):
    return lax.dot_general(a, b, (((1,), (0,)), ((), ())), preferred_element_type=F32)


def _dot_nt(a, b):
    return lax.dot_general(a, b, (((1,), (1,)), ((), ())), preferred_element_type=F32)


def _silu(x):
    return x / (1.0 + jnp.exp(-x))


def _rms(x, g, n):
    ss = jnp.sum(x * x, axis=-1, keepdims=True)
    return x * lax.rsqrt(ss / n + EPS) * g


def _rms_halves(x, g2, lo):
    x2 = x * x
    s_lo = jnp.sum(jnp.where(lo, x2, 0.0), axis=-1, keepdims=True)
    s_hi = jnp.sum(jnp.where(lo, 0.0, x2), axis=-1, keepdims=True)
    r = jnp.where(lo, lax.rsqrt(s_lo / HEAD_DIM + EPS), lax.rsqrt(s_hi / HEAD_DIM + EPS))
    return x * r * g2


def _modulate(x, g, m):
    d = x.shape[-1]
    xn = x * lax.rsqrt(jnp.mean(x * x, axis=-1, keepdims=True) + EPS) * g
    return xn * (1.0 + m[:, d:2 * d]) + m[:, :d], m[:, 2 * d:]


def _split_lanes(x):
    hi = x.astype(BF16)
    lo = (x - hi.astype(F32)).astype(BF16)
    return jnp.concatenate([hi, lo], axis=1)


def _lane_matrix2(entries):
    i = lax.broadcasted_iota(jnp.int32, (LANES, LANES), 0)
    j = lax.broadcasted_iota(jnp.int32, (LANES, LANES), 1)
    m = entries(i, j).astype(BF16)
    return jnp.concatenate([m, m], axis=0)


def _group_ones2(width):
    shift = width.bit_length() - 1
    return _lane_matrix2(lambda i, j: jnp.where((i >> shift) == (j >> shift), 1.0, 0.0))


def _rope_matrix2(rot_dim, period, start):
    half = rot_dim // 2

    def entries(i, j):
        pos = jnp.bitwise_and(j, period - 1) - start
        neg = (pos >= 0) & (pos < half) & (i == j + half)
        plus = (pos >= half) & (pos < rot_dim) & (i == j - half)
        return jnp.where(neg, -1.0, jnp.where(plus, 1.0, 0.0))

    return _lane_matrix2(entries)


def _swap_matrix2():
    return _lane_matrix2(lambda i, j: jnp.where(i == jnp.bitwise_xor(j, HEAD_DIM), 1.0, 0.0))


def _lane_mix(x, m2):
    return _dot(_split_lanes(x), m2)


def _rms_mxu(x, g, n, ones2):
    x2 = x * x
    nt = x.shape[-1] // LANES
    sq = x2[:, :LANES]
    for t in range(1, nt):
        sq = sq + x2[:, t * LANES:(t + 1) * LANES]
    r = lax.rsqrt(_lane_mix(sq, ones2) / n + EPS)
    return x * (r if nt == 1 else jnp.tile(r, (1, nt))) * g


def _modulate_mxu(x, g, m, ones2):
    d = x.shape[-1]
    return _rms_mxu(x, g, d, ones2) * (1.0 + m[:, d:2 * d]) + m[:, :d]


def _attend(parts, sink=None):
    mx = None
    for s, _, _ in parts:
        pm = jnp.max(s, axis=-1, keepdims=True)
        mx = pm if mx is None else jnp.maximum(mx, pm)
    if sink is not None:
        mx = jnp.maximum(mx, sink)
    den, out = None, None
    for s, v, v_t in parts:
        e = jnp.exp(s - mx)
        ps = jnp.sum(e, axis=-1, keepdims=True)
        po = (_dot_nt if v_t else _dot)(e.astype(BF16), v)
        den = ps if den is None else den + ps
        out = po if out is None else out + po
    if sink is not None:
        den = den + jnp.exp(sink - mx)
    return out * (1.0 / den)


def _lane_lo():
    return lax.broadcasted_iota(jnp.int32, (1, LANES), 1) < HEAD_DIM


def _store_pair_transposed(ref, bi, p, x):
    xt = x.T
    ref[bi, 0, 2 * p] = xt[:HEAD_DIM]
    ref[bi, 0, 2 * p + 1] = xt[HEAD_DIM:]


def _swap_halves(a):
    return jnp.concatenate([a[HEAD_DIM:], a[:HEAD_DIM]], axis=0)


def _mod_kernel(n_cond, c_ref, w_ref, b_ref, o_ref):
    s = _silu(c_ref[...])
    w = w_ref[0]
    for r in range(n_cond):
        o_ref[0, r:r + 1, :] = jnp.sum(w * s[:, r:r + 1], axis=0, keepdims=True) + b_ref[0]
    o_ref[0, n_cond:, :] = jnp.zeros((o_ref.shape[1] - n_cond, o_ref.shape[2]), F32)


def _modulation(cond_t, n_cond, w_mod, b_mod):
    depth = w_mod.shape[0]
    tn = 512
    return pl.pallas_call(
        functools.partial(_mod_kernel, n_cond),
        grid=(depth, 3 * D_MODEL // tn),
        in_specs=[pl.BlockSpec((D_MODEL, 8), lambda l, n: (0, 0)),
                  pl.BlockSpec((1, D_MODEL, tn), lambda l, n: (l, 0, n)),
                  pl.BlockSpec((1, 1, tn), lambda l, n: (l, 0, n))],
        out_specs=pl.BlockSpec((1, 8, tn), lambda l, n: (l, 0, n)),
        out_shape=jax.ShapeDtypeStruct((depth, 8, 3 * D_MODEL), F32),
        name="modulation",
    )(cond_t, w_mod, b_mod.reshape(depth, 1, 3 * D_MODEL))


def _mla_keys(cb, kr, wkk_ref, wkv_ref, kg, rope=None):
    kk = _dot(cb, wkk_ref[...])
    keys = []
    for h in range(N_HEADS):
        k = _rms(kk[:, h * LANES:(h + 1) * LANES] + kr, kg, QK_A)
        if rope is not None:
            k = rope(k)
        keys.append(k.astype(BF16))
    return keys, _dot(cb, wkv_ref[...]).astype(BF16)


def _p0_kernel(x_ref, m_ref, ng_ref, win_ref, qag_ref, wq_ref, kvag_ref, wkk_ref, wkv_ref, qg_ref, kg_ref,
               naqg_ref, nakg_ref, wout_ref,
               xo_ref, ckv_ref, krope_ref, nak_ref, nav_ref, y_scr):
    nbs = x_ref.shape[0]
    x = x_ref[...].reshape(nbs * SEQ, D_MODEL)
    h, gate = _modulate(x, ng_ref[...], m_ref[0:1, :])
    hb = h.astype(BF16)
    lo = _lane_lo()
    hi = jnp.logical_not(lo)
    rows = [slice(bi * SEQ, (bi + 1) * SEQ) for bi in range(nbs)]

    qln = _rms(_dot(hb, win_ref[:, E_QLAT:E_CKV]), qag_ref[...], Q_RANK).astype(BF16)
    q_all = _dot(qln, wq_ref[...])
    ckv_n = _rms(_dot(hb, win_ref[:, E_CKV:E_KROPE]), kvag_ref[...], KV_RANK)
    kr = _dot(hb, win_ref[:, E_KROPE:E_GA])
    for bi, rs in enumerate(rows):
        ckv_ref[bi, 0] = ckv_n[rs]
        krope_ref[bi, 0] = kr[rs].T[NOPE_A:QK_A]
    keys, vals = _mla_keys(ckv_n.astype(BF16), kr, wkk_ref, wkv_ref, kg_ref[...])
    qg = qg_ref[...] * (QK_A ** -0.5)

    ga = _dot(hb, win_ref[:, E_GA:E_QB])
    zq = _dot(hb, win_ref[:, E_QB:E_KB])
    zk = _dot(hb, win_ref[:, E_KB:E_VB])
    zv = _dot(hb, win_ref[:, E_VB:E_GB])
    gb = _dot(hb, win_ref[:, E_GB:E_END])
    naqg = naqg_ref[...] * (HEAD_DIM ** -0.5)

    for p in range(N_PAIRS):
        sl = slice(p * LANES, (p + 1) * LANES)
        ys = slice(4 * LANES + p * LANES, 4 * LANES + (p + 1) * LANES)
        qhs = [_rms(q_all[:, hh * LANES:(hh + 1) * LANES], qg, QK_A).astype(BF16) for hh in (2 * p, 2 * p + 1)]
        qb = _rms_halves(zq[:, sl], naqg, lo)
        kb = _rms_halves(zk[:, sl], nakg_ref[...], lo)
        vb = zv[:, sl]
        kbb, vbb = kb.astype(BF16), vb.astype(BF16)
        qms = [jnp.where(keep, qb, 0.0).astype(BF16) for keep in (lo, hi)]
        for bi, rs in enumerate(rows):
            o2 = [_attend([(_dot_nt(qhs[i][rs], keys[2 * p + i][rs]), vals[rs, sl], False)]) for i in (0, 1)]
            y_scr[rs, sl] = (jnp.where(lo, o2[0], o2[1]) * _silu(ga[rs, sl])).astype(BF16)
            _store_pair_transposed(nak_ref, bi, p, kb[rs])
            _store_pair_transposed(nav_ref, bi, p, vb[rs])
            o2 = [_attend([(_dot_nt(qms[i][rs], kbb[rs]), vbb[rs], False)]) for i in (0, 1)]
            y_scr[rs, ys] = (jnp.where(lo, o2[0], o2[1]) * _silu(gb[rs, sl])).astype(BF16)

    xo_ref[...] = (x + gate * _dot(y_scr[...], wout_ref[...])).reshape(nbs, SEQ, D_MODEL)


def _full(shape):
    n = len(shape)
    return pl.BlockSpec(shape, lambda *_: (0,) * n)


def _prompt_even(x, m, ng, win, qag, wq, kvag, wkk, wkv, qg, kg, naqg, nakg, wout):
    nb = x.shape[0]
    nbs = PROMPT_BATCHES_PER_STEP
    assert nb % nbs == 0
    ins = (m, ng, win, qag, wq, kvag, wkk, wkv, qg, kg, naqg, nakg, wout)
    return pl.pallas_call(
        _p0_kernel,
        grid=(nb // nbs,),
        in_specs=[pl.BlockSpec((nbs, SEQ, D_MODEL), lambda b: (b, 0, 0))] + [_full(a.shape) for a in ins],
        out_specs=[pl.BlockSpec((nbs, SEQ, D_MODEL), lambda b: (b, 0, 0)),
                   pl.BlockSpec((nbs, 1, SEQ, KV_RANK), lambda b: (b, 0, 0, 0)),
                   pl.BlockSpec((nbs, 1, ROPE_A, SEQ), lambda b: (b, 0, 0, 0)),
                   pl.BlockSpec((nbs, 1, N_HEADS, HEAD_DIM, SEQ), lambda b: (b, 0, 0, 0, 0)),
                   pl.BlockSpec((nbs, 1, N_HEADS, HEAD_DIM, SEQ), lambda b: (b, 0, 0, 0, 0))],
        out_shape=[jax.ShapeDtypeStruct((nb, SEQ, D_MODEL), F32),
                   jax.ShapeDtypeStruct((nb, 1, SEQ, KV_RANK), F32),
                   jax.ShapeDtypeStruct((nb, 1, ROPE_A, SEQ), F32),
                   jax.ShapeDtypeStruct((nb, 1, N_HEADS, HEAD_DIM, SEQ), F32),
                   jax.ShapeDtypeStruct((nb, 1, N_HEADS, HEAD_DIM, SEQ), F32)],
        scratch_shapes=[pltpu.VMEM((nbs * SEQ, D_MODEL), BF16)],
        compiler_params=pltpu.CompilerParams(dimension_semantics=("arbitrary",), vmem_limit_bytes=VMEM_LIMIT),
        name="prompt_even",
    )(x, *ins)


def _gqa_pair_operands(k, v, kg2, lo):
    kn = _rms_halves(k, kg2, lo)
    return kn, (kn.astype(BF16), pltpu.roll(kn, HEAD_DIM, 1).astype(BF16)), \
        (v.astype(BF16), pltpu.roll(v, HEAD_DIM, 1).astype(BF16))


def _p1_kernel(sink_ref, x_ref, m_ref, ng_ref, win_ref, gqg_ref, gkg_ref, sqg_ref, skg_ref, wout_ref,
               xo_ref, gk_ref, gv_ref, sk_ref, sv_ref, y_scr):
    nbs = x_ref.shape[0]
    x = x_ref[...].reshape(nbs * SEQ, D_MODEL)
    h, gate = _modulate(x, ng_ref[...], m_ref[0:1, :])
    hb = h.astype(BF16)
    lo = _lane_lo()
    hi = jnp.logical_not(lo)
    sc = HEAD_DIM ** -0.5
    rows = [slice(bi * SEQ, (bi + 1) * SEQ) for bi in range(nbs)]

    branches = ((O_QC, O_KC, O_VC, O_GC, gqg_ref, gkg_ref, gk_ref, gv_ref, False, 0),
                (O_QD, O_KD, O_VD, O_GD, sqg_ref, skg_ref, sk_ref, sv_ref, True, 4 * LANES))
    for oq, ok, ov, og, qg_ref, kg_ref, ck_ref, cv_ref, has_sink, yoff in branches:
        zq = _dot(hb, win_ref[:, oq:oq + 4 * LANES])
        zkv = _dot(hb, win_ref[:, ok:ok + 2 * LANES])
        zg = _dot(hb, win_ref[:, og:og + 4 * LANES])
        v = zkv[:, LANES:]
        kn, ks, vs = _gqa_pair_operands(zkv[:, :LANES], v, kg_ref[...], lo)
        for bi, rs in enumerate(rows):
            _store_pair_transposed(ck_ref, bi, 0, kn[rs])
            _store_pair_transposed(cv_ref, bi, 0, v[rs])
        qg = qg_ref[...] * sc
        for p in range(N_PAIRS):
            sl = slice(p * LANES, (p + 1) * LANES)
            qn = _rms_halves(zq[:, sl], qg, lo)
            qms = [jnp.where(keep, qn, 0.0).astype(BF16) for keep in (lo, hi)]
            kv = p // 2
            for bi, rs in enumerate(rows):
                o2 = []
                for half in (0, 1):
                    swap = 0 if kv == half else 1
                    sink = sink_ref[2 * p + half] if has_sink else None
                    o2.append(_attend([(_dot_nt(qms[half][rs], ks[swap][rs]), vs[swap][rs], False)], sink))
                o = jnp.where(lo, o2[0], o2[1])
                y_scr[rs, yoff + p * LANES:yoff + (p + 1) * LANES] = (o * _silu(zg[rs, sl])).astype(BF16)

    xo_ref[...] = (x + gate * _dot(y_scr[...], wout_ref[...])).reshape(nbs, SEQ, D_MODEL)


def _prompt_odd(sink, x, m, ng, win, gqg, gkg, sqg, skg, wout):
    nb = x.shape[0]
    nbs = PROMPT_BATCHES_PER_STEP
    assert nb % nbs == 0
    ins = (m, ng, win, gqg, gkg, sqg, skg, wout)
    cache_spec = pl.BlockSpec((nbs, 1, 2, HEAD_DIM, SEQ), lambda b: (b, 0, 0, 0, 0))
    cache_shape = jax.ShapeDtypeStruct((nb, 1, 2, HEAD_DIM, SEQ), F32)
    return pl.pallas_call(
        _p1_kernel,
        grid=(nb // nbs,),
        in_specs=[pl.BlockSpec(memory_space=pltpu.SMEM),
                  pl.BlockSpec((nbs, SEQ, D_MODEL), lambda b: (b, 0, 0))] + [_full(a.shape) for a in ins],
        out_specs=[pl.BlockSpec((nbs, SEQ, D_MODEL), lambda b: (b, 0, 0))] + [cache_spec] * 4,
        out_shape=[jax.ShapeDtypeStruct((nb, SEQ, D_MODEL), F32)] + [cache_shape] * 4,
        scratch_shapes=[pltpu.VMEM((nbs * SEQ, D_MODEL), BF16)],
        compiler_params=pltpu.CompilerParams(dimension_semantics=("arbitrary",), vmem_limit_bytes=VMEM_LIMIT),
        name="prompt_odd",
    )(sink, x, *ins)


def _s0a_kernel(x_ref, m_ref, ng_ref, win_ref, qag_ref, wq_ref, kvag_ref, wkk_ref, wkv_ref, qg_ref, kg_ref,
                naqg_ref, nakg_ref, cos_ref, sin_ref,
                qa_ref, ka_ref, va_ref, qb_ref, kb_ref, vb_ref, g_ref):
    b = pl.program_id(0)
    lo = _lane_lo()
    ones_all = _group_ones2(LANES)
    partner = _rope_matrix2(ROPE_A, LANES, NOPE_A)
    hb = _modulate(x_ref[0], ng_ref[...], m_ref[pl.ds(1 + b, 1), :])[0].astype(BF16)
    cos, sin = cos_ref[...], sin_ref[...]

    qln = _rms(_dot(hb, win_ref[:, E_QLAT:E_CKV]), qag_ref[...], Q_RANK).astype(BF16)
    q_all = _dot(qln, wq_ref[...])
    ckv_n = _rms(_dot(hb, win_ref[:, E_CKV:E_KROPE]), kvag_ref[...], KV_RANK)
    kr = _dot(hb, win_ref[:, E_KROPE:E_GA])
    cb = ckv_n.astype(BF16)
    kk = _dot(cb, wkk_ref[...])
    va_ref[0] = _dot(cb, wkv_ref[...]).astype(BF16)
    zq = _dot(hb, win_ref[:, E_QB:E_KB])
    zk = _dot(hb, win_ref[:, E_KB:E_VB])
    vb_ref[0] = _dot(hb, win_ref[:, E_VB:E_GB]).astype(BF16)
    g_ref[0, :, 0:4 * LANES] = _silu(_dot(hb, win_ref[:, E_GA:E_QB]))
    g_ref[0, :, 4 * LANES:8 * LANES] = _silu(_dot(hb, win_ref[:, E_GB:E_END]))

    qg = qg_ref[...] * (QK_A ** -0.5)
    kg = kg_ref[...]
    k_partner = _lane_mix(kr * kg, partner) * sin
    for hh in range(N_HEADS):
        sl = slice(hh * LANES, (hh + 1) * LANES)
        qn = _rms_mxu(q_all[:, sl], qg, QK_A, ones_all)
        qa_ref[0, :, sl] = (qn * cos + _lane_mix(qn, partner) * sin).astype(BF16)
        k_raw = kk[:, sl] + kr
        k_inv = lax.rsqrt(_lane_mix(k_raw * k_raw, ones_all) / QK_A + EPS)
        ka_ref[0, :, sl] = ((k_raw * kg * cos + k_partner) * k_inv).astype(BF16)
    naqg = naqg_ref[...] * (HEAD_DIM ** -0.5)
    for p in range(N_PAIRS):
        sl = slice(p * LANES, (p + 1) * LANES)
        qb_ref[0, :, sl] = _rms_halves(zq[:, sl], naqg, lo).astype(BF16)
        kb_ref[0, :, sl] = _rms_halves(zk[:, sl], nakg_ref[...], lo).astype(BF16)


def _sample_even_proj(x, m, ng, win, qag, wq, kvag, wkk, wkv, qg, kg, naqg, nakg, cos, sin):
    nb, s, _ = x.shape
    nq = s // Q_BLOCK
    ins = (m, ng, win, qag, wq, kvag, wkk, wkv, qg, kg, naqg, nakg)
    tab = pl.BlockSpec((Q_BLOCK, LANES), lambda b, j: (j, 0))

    def blk(w):
        return pl.BlockSpec((1, Q_BLOCK, w), lambda b, j: (b, j, 0))

    def shp(w, dt):
        return jax.ShapeDtypeStruct((nb, s, w), dt)

    return pl.pallas_call(
        _s0a_kernel,
        grid=(nb, nq),
        in_specs=[blk(D_MODEL)] + [_full(a.shape) for a in ins] + [tab, tab],
        out_specs=[blk(1024), blk(1024), blk(512), blk(512), blk(512), blk(512), blk(1024)],
        out_shape=[shp(1024, BF16), shp(1024, BF16), shp(512, BF16), shp(512, BF16), shp(512, BF16),
                   shp(512, BF16), shp(1024, F32)],
        compiler_params=pltpu.CompilerParams(dimension_semantics=("arbitrary", "arbitrary"),
                                             vmem_limit_bytes=VMEM_LIMIT),
        name="sample_even_proj",
    )(x, *ins, cos, sin)


def _build_bias_table(rpb_ref, tile_scr, tab_ref):
    qc = lax.broadcasted_iota(jnp.int32, (GRID_W, LANES), 0)
    lane = lax.broadcasted_iota(jnp.int32, (GRID_W, LANES), 1)
    kc = jnp.bitwise_and(lane, GRID_W - 1)
    lo = lane < GRID_W
    diff = kc - qc + (NA_COLS - 1)
    cs = jnp.clip(qc - NA_COLS // 2, 0, GRID_W - NA_COLS)
    valid = (kc >= cs) & (kc < cs + NA_COLS)
    tab_ref[...] = jnp.zeros(tab_ref.shape, F32)
    tile_scr[RPB_ROWS] = jnp.zeros((GRID_W, LANES), F32)

    def per_head(h, carry):
        for dr in range(RPB_ROWS):
            t = jnp.zeros((GRID_W, LANES), F32)
            for dc in range(RPB_COLS):
                t = jnp.where(diff == dc, rpb_ref[(h * RPB_ROWS + dr) * RPB_COLS + dc], t)
            tile_scr[dr] = jnp.where(valid, t, NEG_INF)
        for c in range(NA_ROWS // 2, NA_ROWS // 2 + NA_ROWS):
            d0 = 2 * c - NA_ROWS
            tab_ref[0, h, c] = jnp.where(lo, tile_scr[d0], tile_scr[d0 + 1])
            tab_ref[1, h, c] = jnp.where(lo, tile_scr[d0 - 1 if d0 > 0 else RPB_ROWS], tile_scr[d0])
        return carry

    lax.fori_loop(0, N_HEADS, per_head, 0)


def _s0b_kernel(rpb_ref, x_ref, m_ref, qa_ref, ka_ref, va_ref, qb_ref, kb_ref, vb_ref, g_ref,
                cckv_ref, ckr_ref, cnk_ref, cnv_ref, wkk_ref, wkv_ref, kg_ref, wout_ref,
                xo_ref, kca_scr, vca_scr, tile_scr, tab_scr, y_scr):
    b = pl.program_id(0)
    j = pl.program_id(1)
    lo = _lane_lo()
    n_lat = ka_ref.shape[1]

    @pl.when((b == 0) & (j == 0))
    def _():
        _build_bias_table(rpb_ref, tile_scr, tab_scr)

    @pl.when(j == 0)
    def _():
        kr_t = jnp.concatenate([jnp.zeros((NOPE_A, PAST_LEN), F32), ckr_ref[0],
                                jnp.zeros((LANES - QK_A, PAST_LEN), F32)], axis=0)
        keys, vals = _mla_keys(cckv_ref[0].astype(BF16), kr_t.T, wkk_ref, wkv_ref, kg_ref[...])
        for hh in range(N_HEADS):
            kca_scr[:, hh * LANES:(hh + 1) * LANES] = keys[hh]
        vca_scr[...] = vals

    kidx = lax.broadcasted_iota(jnp.int32, (1, n_lat), 1)
    for p in range(N_PAIRS):
        sl = slice(p * LANES, (p + 1) * LANES)
        o2 = []
        for hh in (2 * p, 2 * p + 1):
            hs = slice(hh * LANES, (hh + 1) * LANES)
            q = qa_ref[0, :, hs]
            o2.append(_attend([(_dot_nt(q, ka_ref[0, :, hs]), va_ref[0, :, sl], False),
                               (_dot_nt(q, kca_scr[:, hs]), vca_scr[:, sl], False)]))
        oa = jnp.where(lo, o2[0], o2[1])
        y_scr[:, sl] = (oa * g_ref[0, :, sl]).astype(BF16)

        qb = qb_ref[0, :, sl]
        kb = kb_ref[0, :, sl]
        vb = vb_ref[0, :, sl]
        kcb = cnk_ref[0, sl, :].astype(BF16)
        vcb = cnv_ref[0, sl, :].astype(BF16)
        o2 = []
        for half in (0, 1):
            head = 2 * p + half
            qm = jnp.where(lo if half == 0 else jnp.logical_not(lo), qb, jnp.zeros_like(qb))
            s_lat = _dot_nt(qm, kb)
            rows = []
            for local in range(Q_BLOCK // GRID_W):
                qr = j * (Q_BLOCK // GRID_W) + local
                par = 0 if local % 2 == 1 else 1
                c0 = (RPB_ROWS + par - local) // 2 - 2 * j
                bias = jnp.concatenate([tab_scr[par, head, c0 + t] for t in range(n_lat // LANES)], axis=1)
                r0 = jnp.clip(qr - NA_ROWS // 2, 0, n_lat // GRID_W - NA_ROWS) * GRID_W
                ok = (kidx >= r0) & (kidx < r0 + NA_ROWS * GRID_W)
                rows.append(jnp.where(ok, s_lat[local * GRID_W:(local + 1) * GRID_W] + bias, NEG_INF))
            s_lat = jnp.concatenate(rows, axis=0)
            o2.append(_attend([(s_lat, vb, False), (_dot(qm, kcb), vcb, True)]))
        ob = jnp.where(lo, o2[0], o2[1])
        ys = slice(4 * LANES + p * LANES, 4 * LANES + (p + 1) * LANES)
        y_scr[:, ys] = (ob * g_ref[0, :, ys]).astype(BF16)

    d = x_ref.shape[-1]
    gate = m_ref[pl.ds(1 + b, 1), 2 * d:]
    xo_ref[0] = x_ref[0] + gate * _dot(y_scr[...], wout_ref[...])


def _sample_even_attn(rpb, x, m, qa, ka, va, qb, kb, vb, g, cckv, ckr, cnk, cnv, wkk, wkv, kg, wout):
    nb, s, _ = x.shape
    nq = s // Q_BLOCK

    def blk(w):
        return pl.BlockSpec((1, Q_BLOCK, w), lambda b, j: (b, j, 0))

    def per_batch(a):
        return pl.BlockSpec((1,) + a.shape[1:], lambda b, j: (b, 0, 0))

    return pl.pallas_call(
        _s0b_kernel,
        grid=(nb, nq),
        in_specs=[pl.BlockSpec(memory_space=pltpu.SMEM), blk(D_MODEL), _full(m.shape),
                  blk(1024), per_batch(ka), per_batch(va), blk(512), per_batch(kb), per_batch(vb), blk(1024),
                  per_batch(cckv), per_batch(ckr), per_batch(cnk), per_batch(cnv),
                  _full(wkk.shape), _full(wkv.shape), _full(kg.shape), _full(wout.shape)],
        out_specs=blk(D_MODEL),
        out_shape=jax.ShapeDtypeStruct(x.shape, F32),
        scratch_shapes=[pltpu.VMEM((PAST_LEN, N_HEADS * LANES), BF16),
                        pltpu.VMEM((PAST_LEN, N_HEADS * HEAD_DIM), BF16),
                        pltpu.VMEM((RPB_ROWS + 1, GRID_W, LANES), F32),
                        pltpu.VMEM((2, N_HEADS, BIAS_CHUNKS, GRID_W, LANES), F32),
                        pltpu.VMEM((Q_BLOCK, D_MODEL), BF16)],
        compiler_params=pltpu.CompilerParams(dimension_semantics=("arbitrary", "arbitrary"),
                                             vmem_limit_bytes=VMEM_LIMIT),
        name="sample_even_attn",
    )(rpb, x, m, qa, ka, va, qb, kb, vb, g, cckv, ckr, cnk, cnv, wkk, wkv, kg, wout)


def _s1a_kernel(x_ref, m_ref, ng_ref, win_ref, gqg_ref, gkg_ref, sqg_ref, skg_ref, cos_ref, sin_ref,
                qc_ref, kc_ref, vc_ref, qd_ref, kd_ref, vd_ref, g_ref):
    b = pl.program_id(0)
    lo = _lane_lo()
    partner = _rope_matrix2(HEAD_DIM, HEAD_DIM, 0)
    swap = _swap_matrix2()[:LANES]
    hb = _modulate(x_ref[0], ng_ref[...], m_ref[pl.ds(1 + b, 1), :])[0].astype(BF16)
    cos, sin = cos_ref[...], sin_ref[...]
    sc = HEAD_DIM ** -0.5

    def rope(t):
        return t * cos + _lane_mix(t, partner) * sin

    branches = ((O_QC, O_KC, O_GC, gqg_ref, gkg_ref, qc_ref, kc_ref, vc_ref, 0),
                (O_QD, O_KD, O_GD, sqg_ref, skg_ref, qd_ref, kd_ref, vd_ref, 4 * LANES))
    for oq, ok, og, qg_ref, kg_ref, q_out, k_out, v_out, goff in branches:
        zq = _dot(hb, win_ref[:, oq:oq + 4 * LANES])
        zkv = _dot(hb, win_ref[:, ok:ok + 2 * LANES])
        qg = qg_ref[...] * sc
        for p in range(N_PAIRS):
            sl = slice(p * LANES, (p + 1) * LANES)
            q_out[0, :, sl] = rope(_rms_halves(zq[:, sl], qg, lo)).astype(BF16)
        kn = rope(_rms_halves(zkv[:, :LANES], kg_ref[...], lo))
        v = zkv[:, LANES:]
        for out, val in ((k_out, kn.astype(BF16)), (v_out, v.astype(BF16))):
            out[0, :, 0:LANES] = val
            out[0, :, LANES:2 * LANES] = _dot(val, swap).astype(BF16)
        g_ref[0, :, goff:goff + 4 * LANES] = _silu(_dot(hb, win_ref[:, og:og + 4 * LANES]))


def _sample_odd_proj(x, m, ng, win, gqg, gkg, sqg, skg, cos, sin):
    nb, s, _ = x.shape
    nq = s // Q_BLOCK
    ins = (m, ng, win, gqg, gkg, sqg, skg)
    tab = pl.BlockSpec((Q_BLOCK, LANES), lambda b, j: (j, 0))

    def blk(w):
        return pl.BlockSpec((1, Q_BLOCK, w), lambda b, j: (b, j, 0))

    def shp(w, dt):
        return jax.ShapeDtypeStruct((nb, s, w), dt)

    return pl.pallas_call(
        _s1a_kernel,
        grid=(nb, nq),
        in_specs=[blk(D_MODEL)] + [_full(a.shape) for a in ins] + [tab, tab],
        out_specs=[blk(512), blk(256), blk(256), blk(512), blk(256), blk(256), blk(1024)],
        out_shape=[shp(512, BF16), shp(256, BF16), shp(256, BF16), shp(512, BF16), shp(256, BF16),
                   shp(256, BF16), shp(1024, F32)],
        compiler_params=pltpu.CompilerParams(dimension_semantics=("arbitrary", "arbitrary"),
                                             vmem_limit_bytes=VMEM_LIMIT),
        name="sample_odd_proj",
    )(x, *ins, cos, sin)


def _s1b_kernel(sink_ref, x_ref, m_ref, qc_ref, kc_ref, vc_ref, qd_ref, kd_ref, vd_ref, g_ref,
                cgk_ref, cgv_ref, csk_ref, csv_ref, wout_ref, xo_ref, y_scr):
    b = pl.program_id(0)
    j = pl.program_id(1)
    lo = _lane_lo()
    n_lat = kc_ref.shape[1]
    win_keys = Q_BLOCK + 2 * SWA_HALF

    def ctx_pair(ref):
        a = ref[0].astype(BF16)
        return a, _swap_halves(a)

    cgk, cgv, csk, csv = ctx_pair(cgk_ref), ctx_pair(cgv_ref), ctx_pair(csk_ref), ctx_pair(csv_ref)

    ks = pl.multiple_of(jnp.clip(j * Q_BLOCK - SWA_HALF, 0, n_lat - win_keys), SWA_HALF)
    qpos = j * Q_BLOCK + lax.broadcasted_iota(jnp.int32, (Q_BLOCK, win_keys), 0)
    kpos = ks + lax.broadcasted_iota(jnp.int32, (Q_BLOCK, win_keys), 1)
    band = jnp.abs(qpos - kpos) <= SWA_HALF

    for p in range(N_PAIRS):
        sl = slice(p * LANES, (p + 1) * LANES)
        kv = p // 2
        qc = qc_ref[0, :, sl]
        qd = qd_ref[0, :, sl]
        oc2, od2 = [], []
        for half in (0, 1):
            swap = 0 if kv == half else 1
            ws = slice(swap * LANES, (swap + 1) * LANES)
            keep = lo if half == 0 else jnp.logical_not(lo)
            qm = jnp.where(keep, qc, jnp.zeros_like(qc))
            oc2.append(_attend([(_dot_nt(qm, kc_ref[0, :, ws]), vc_ref[0, :, ws], False),
                                (_dot(qm, cgk[swap]), cgv[swap], True)]))
            qm = jnp.where(keep, qd, jnp.zeros_like(qd))
            s_loc = jnp.where(band, _dot_nt(qm, kd_ref[0, pl.ds(ks, win_keys), ws]), NEG_INF)
            od2.append(_attend([(s_loc, vd_ref[0, pl.ds(ks, win_keys), ws], False),
                                (_dot(qm, csk[swap]), csv[swap], True)], sink_ref[2 * p + half]))
        y_scr[:, sl] = (jnp.where(lo, oc2[0], oc2[1]) * g_ref[0, :, sl]).astype(BF16)
        ys = slice(4 * LANES + p * LANES, 4 * LANES + (p + 1) * LANES)
        y_scr[:, ys] = (jnp.where(lo, od2[0], od2[1]) * g_ref[0, :, ys]).astype(BF16)

    d = x_ref.shape[-1]
    gate = m_ref[pl.ds(1 + b, 1), 2 * d:]
    xo_ref[0] = x_ref[0] + gate * _dot(y_scr[...], wout_ref[...])


def _sample_odd_attn(sink, x, m, qc, kc, vc, qd, kd, vd, g, cgk, cgv, csk, csv, wout):
    nb, s, _ = x.shape
    nq = s // Q_BLOCK

    def blk(w):
        return pl.BlockSpec((1, Q_BLOCK, w), lambda b, j: (b, j, 0))

    def per_batch(a):
        return pl.BlockSpec((1,) + a.shape[1:], lambda b, j: (b, 0, 0))

    return pl.pallas_call(
        _s1b_kernel,
        grid=(nb, nq),
        in_specs=[pl.BlockSpec(memory_space=pltpu.SMEM), blk(D_MODEL), _full(m.shape),
                  blk(512), per_batch(kc), per_batch(vc), blk(512), per_batch(kd), per_batch(vd), blk(1024),
                  per_batch(cgk), per_batch(cgv), per_batch(csk), per_batch(csv), _full(wout.shape)],
        out_specs=blk(D_MODEL),
        out_shape=jax.ShapeDtypeStruct(x.shape, F32),
        scratch_shapes=[pltpu.VMEM((Q_BLOCK, D_MODEL), BF16)],
        compiler_params=pltpu.CompilerParams(dimension_semantics=("arbitrary", "arbitrary"),
                                             vmem_limit_bytes=VMEM_LIMIT),
        name="sample_odd_attn",
    )(sink, x, m, qc, kc, vc, qd, kd, vd, g, cgk, cgv, csk, csv, wout)


def _pad_cols(w, left, width):
    return jnp.pad(w, ((0, 0), (left, width - left - w.shape[1])))


def _per_head_slabs(w, n_heads, head_w, take):
    k = w.shape[0]
    wh = w.reshape(k, n_heads, head_w)[:, :, :take]
    return jnp.pad(wh, ((0, 0), (0, 0), (0, LANES - take))).reshape(k, n_heads * LANES)


def _feature_major(c):
    b, h, l, d = c.shape
    return jnp.swapaxes(c, -1, -2).reshape(b, h * d, l)


def _token_major(c):
    return jnp.swapaxes(c, -1, -2)


def _rope_tables(s, rot_dim, period, start):
    quarter = rot_dim // 4
    t = jnp.arange(s)
    inv = ROPE_THETA ** (-jnp.arange(quarter, dtype=F32) / quarter)
    row = (t // GRID_W).astype(F32)[:, None] * inv
    col = (t % GRID_W).astype(F32)[:, None] * inv
    ang = jnp.concatenate([row, col], axis=-1)
    cos, sin = jnp.cos(ang), jnp.sin(ang)
    pre = jnp.ones((s, start), F32)
    post = jnp.zeros((s, period - start - rot_dim), F32)
    c = jnp.concatenate([pre, cos, cos, post], axis=-1)
    sn = jnp.concatenate([0 * pre, sin, sin, post], axis=-1)
    rep = LANES // period
    return jnp.tile(c, (1, rep)), jnp.tile(sn, (1, rep))


def _row(v, width=None):
    v = v.reshape(1, -1).astype(F32)
    return v if width is None else jnp.pad(v, ((0, 0), (0, width - v.shape[1])))


def kernel(x_prompt, x_sample, cache_mla_ckv, cache_mla_krope, cache_na_k, cache_na_v, cache_gqa_k, cache_gqa_v, cache_swa_k, cache_swa_v, c, c_ctx, norm_g, w_mod, b_mod, w_in_even, mla_qa_g, w_q_up, mla_kva_g, w_kv_up, mla_q_g, mla_k_g, na_q_g, na_k_g, na_rpb, w_out_even, w_in_odd, gqa_q_g, gqa_k_g, swa_q_g, swa_k_g, swa_sink, w_out_odd):
    n_dec = x_sample.shape[0]
    assert w_mod.shape[0] == 2 and n_dec + 1 <= 8

    cond_t = jnp.concatenate([c_ctx[:, None], c.T, jnp.zeros((D_MODEL, 7 - n_dec), F32)], axis=1)
    m_all = _modulation(cond_t, 1 + n_dec, w_mod, b_mod)

    we = w_in_even[0]
    q_lat, ckv, krope, ga, qb, kb, vb, gb = jnp.split(we, [256, 384, 416, 928, 1440, 1952, 2464], axis=1)
    win_e = jnp.concatenate([q_lat, ckv, _pad_cols(krope, NOPE_A, LANES), ga, qb, kb, vb, gb], axis=1).astype(BF16)
    wq = _per_head_slabs(w_q_up[0], N_HEADS, QK_A, QK_A).astype(BF16)
    wkk = _per_head_slabs(w_kv_up[0], N_HEADS, NOPE_A + HEAD_DIM, NOPE_A).astype(BF16)
    wkv = w_kv_up[0].reshape(KV_RANK, N_HEADS, NOPE_A + HEAD_DIM)[:, :, NOPE_A:].reshape(KV_RANK, -1).astype(BF16)
    even = (_row(norm_g[0]), win_e, _row(mla_qa_g[0]), wq, _row(mla_kva_g[0]), wkk, wkv,
            _row(mla_q_g[0], LANES), _row(mla_k_g[0], LANES),
            _row(jnp.tile(na_q_g[0], 2)), _row(jnp.tile(na_k_g[0], 2)))
    wout_e = w_out_even[0].astype(BF16)

    win_o = w_in_odd[0].astype(BF16)
    odd = (_row(norm_g[1]), win_o, _row(jnp.tile(gqa_q_g[0], 2)), _row(jnp.tile(gqa_k_g[0], 2)),
           _row(jnp.tile(swa_q_g[0], 2)), _row(jnp.tile(swa_k_g[0], 2)))
    wout_o = w_out_odd[0].astype(BF16)
    sink = swa_sink[0].astype(F32)

    xp1, new_ckv, new_krope, new_na_k, new_na_v = _prompt_even(x_prompt, m_all[0], *even, wout_e)
    xp2, new_gqa_k, new_gqa_v, new_swa_k, new_swa_v = _prompt_odd(sink, xp1, m_all[1], *odd, wout_o)

    cos_e, sin_e = _rope_tables(DEC_SEQ, ROPE_A, LANES, NOPE_A)
    qa, ka, va, qbs, kbs, vbs, g0 = _sample_even_proj(x_sample, m_all[0], *even, cos_e, sin_e)
    ckr = jnp.swapaxes(cache_mla_krope[:, 0], -1, -2)
    xs1 = _sample_even_attn(na_rpb[0].reshape(-1), x_sample, m_all[0], qa, ka, va, qbs, kbs, vbs, g0,
                            cache_mla_ckv[:, 0], ckr, _feature_major(cache_na_k[:, 0]),
                            _feature_major(cache_na_v[:, 0]), wkk, wkv, even[8], wout_e)
    cos_o, sin_o = _rope_tables(DEC_SEQ, HEAD_DIM, HEAD_DIM, 0)
    qc, kc, vc, qd, kd, vd, g1 = _sample_odd_proj(xs1, m_all[1], *odd, cos_o, sin_o)
    xs2 = _sample_odd_attn(sink, xs1, m_all[1], qc, kc, vc, qd, kd, vd, g1,
                           _feature_major(cache_gqa_k[:, 0]), _feature_major(cache_gqa_v[:, 0]),
                           _feature_major(cache_swa_k[:, 0]), _feature_major(cache_swa_v[:, 0]), wout_o)

    caches = (new_krope, new_na_k, new_na_v, new_gqa_k, new_gqa_v, new_swa_k, new_swa_v)
    return (xp2, xs2, new_ckv) + tuple(_token_major(c) for c in caches)
```

```python
import functools

import jax
import jax.numpy as jnp
from jax import lax
from jax.experimental import pallas as pl
from jax.experimental.pallas import tpu as pltpu

F32 = jnp.float32
BF16 = jnp.bfloat16

D_MODEL = 1024
SEQ = 256
DEC_SEQ = 1024
PAST_LEN = 256
GRID_W = 64
HEAD_DIM = 64
Q_RANK = 256
KV_RANK = 128
NOPE_A = 64
ROPE_A = 32
QK_A = NOPE_A + ROPE_A
N_HEADS = 8
NA_ROWS = 8
NA_COLS = 16
SWA_HALF = 128
ROPE_THETA = 10000.0
EPS = 1e-6
NEG_INF = -1e30
LOG2E = 1.4426950408889634

LANES = 128
Q_BLOCK = 256
PROMPT_BATCHES_PER_STEP = 2
N_PAIRS = N_HEADS // 2
RPB_ROWS = 2 * NA_ROWS - 1
RPB_COLS = 2 * NA_COLS - 1
BIAS_CHUNKS = 16
VMEM_LIMIT = 48 * 1024 * 1024

E_QLAT, E_CKV, E_KROPE, E_GA, E_QB, E_KB, E_VB, E_GB, E_END = 0, 256, 384, 512, 1024, 1536, 2048, 2560, 3072
O_QC, O_KC, O_VC, O_GC, O_QD, O_KD, O_VD, O_GD, O_END = 0, 512, 640, 768, 1280, 1792, 1920, 2048, 2560


def _dot(a, b):
    return lax.dot_general(a, b, (((1,), (0,)), ((), ())), preferred_element_type=F32)


def _dot_nt(a, b):
    return lax.dot_general(a, b, (((1,), (1,)), ((), ())), preferred_element_type=F32)


def _silu(x):
    return x / (1.0 + jnp.exp(-x))


def _rms(x, g, n):
    ss = jnp.sum(x * x, axis=-1, keepdims=True)
    return x * lax.rsqrt(ss / n + EPS) * g


def _rms_halves(x, g2, lo):
    x2 = x * x
    s_lo = jnp.sum(jnp.where(lo, x2, 0.0), axis=-1, keepdims=True)
    s_hi = jnp.sum(jnp.where(lo, 0.0, x2), axis=-1, keepdims=True)
    r = jnp.where(lo, lax.rsqrt(s_lo / HEAD_DIM + EPS), lax.rsqrt(s_hi / HEAD_DIM + EPS))
    return x * r * g2


def _modulate(x, g, m):
    d = x.shape[-1]
    xn = x * lax.rsqrt(jnp.mean(x * x, axis=-1, keepdims=True) + EPS) * g
    return xn * (1.0 + m[:, d:2 * d]) + m[:, :d], m[:, 2 * d:]


def _split_lanes(x):
    hi = x.astype(BF16)
    lo = (x - hi.astype(F32)).astype(BF16)
    return jnp.concatenate([hi, lo], axis=1)


def _lane_matrix2(entries):
    i = lax.broadcasted_iota(jnp.int32, (LANES, LANES), 0)
    j = lax.broadcasted_iota(jnp.int32, (LANES, LANES), 1)
    m = entries(i, j).astype(BF16)
    return jnp.concatenate([m, m], axis=0)


def _group_ones2(width):
    shift = width.bit_length() - 1
    return _lane_matrix2(lambda i, j: jnp.where((i >> shift) == (j >> shift), 1.0, 0.0))


def _rope_matrix2(rot_dim, period, start):
    half = rot_dim // 2

    def entries(i, j):
        pos = jnp.bitwise_and(j, period - 1) - start
        neg = (pos >= 0) & (pos < half) & (i == j + half)
        plus = (pos >= half) & (pos < rot_dim) & (i == j - half)
        return jnp.where(neg, -1.0, jnp.where(plus, 1.0, 0.0))

    return _lane_matrix2(entries)


def _swap_matrix2():
    return _lane_matrix2(lambda i, j: jnp.where(i == jnp.bitwise_xor(j, HEAD_DIM), 1.0, 0.0))


def _lane_mix(x, m2):
    return _dot(_split_lanes(x), m2)


def _rms_mxu(x, g, n, ones2):
    x2 = x * x
    nt = x.shape[-1] // LANES
    sq = x2[:, :LANES]
    for t in range(1, nt):
        sq = sq + x2[:, t * LANES:(t + 1) * LANES]
    r = lax.rsqrt(_lane_mix(sq, ones2) / n + EPS)
    return x * (r if nt == 1 else jnp.tile(r, (1, nt))) * g


def _modulate_mxu(x, g, m, ones2):
    d = x.shape[-1]
    return _rms_mxu(x, g, d, ones2) * (1.0 + m[:, d:2 * d]) + m[:, :d]


def _with_ones(v, transposed=False):
    if transposed:
        return jnp.concatenate([v, jnp.ones((LANES, v.shape[1]), v.dtype)], axis=0)
    return jnp.concatenate([v, jnp.ones((v.shape[0], LANES), v.dtype)], axis=1)


def _attend(parts, sink=None):
    mx = None
    for s, _, _ in parts:
        pm = jnp.max(s, axis=-1, keepdims=True)
        mx = pm if mx is None else jnp.maximum(mx, pm)
    if sink is not None:
        mx = jnp.maximum(mx, sink)
    acc = None
    for s, v, v_t in parts:
        po = (_dot_nt if v_t else _dot)(jnp.exp2(s - mx).astype(BF16), v)
        acc = po if acc is None else acc + po
    den = acc[:, LANES:]
    if sink is not None:
        den = den + jnp.exp2(sink - mx)
    return acc[:, :LANES] * (1.0 / den)


def _lane_lo():
    return lax.broadcasted_iota(jnp.int32, (1, LANES), 1) < HEAD_DIM


def _store_pair_transposed(ref, bi, p, x):
    xt = x.T
    ref[bi, 0, 2 * p] = xt[:HEAD_DIM]
    ref[bi, 0, 2 * p + 1] = xt[HEAD_DIM:]


def _swap_halves(a):
    return jnp.concatenate([a[HEAD_DIM:], a[:HEAD_DIM]], axis=0)


def _mod_kernel(n_cond, c_ref, w_ref, b_ref, o_ref):
    @pl.when(pl.program_id(1) == 0)
    def _():
        o_ref[0, :n_cond, :] = jnp.broadcast_to(b_ref[0], (n_cond, o_ref.shape[2]))
        o_ref[0, n_cond:, :] = jnp.zeros((o_ref.shape[1] - n_cond, o_ref.shape[2]), F32)

    s = _silu(c_ref[...])
    w = w_ref[0]
    for r in range(n_cond):
        o_ref[0, r:r + 1, :] += jnp.sum(w * s[:, r:r + 1], axis=0, keepdims=True)


def _modulation(cond_t, n_cond, w_mod, b_mod):
    depth = w_mod.shape[0]
    tk = 256
    return pl.pallas_call(
        functools.partial(_mod_kernel, n_cond),
        grid=(depth, D_MODEL // tk),
        in_specs=[pl.BlockSpec((tk, 8), lambda l, k: (k, 0)),
                  pl.BlockSpec((1, tk, 3 * D_MODEL), lambda l, k: (l, k, 0)),
                  pl.BlockSpec((1, 1, 3 * D_MODEL), lambda l, k: (l, 0, 0))],
        out_specs=pl.BlockSpec((1, 8, 3 * D_MODEL), lambda l, k: (l, 0, 0)),
        out_shape=jax.ShapeDtypeStruct((depth, 8, 3 * D_MODEL), F32),
        compiler_params=pltpu.CompilerParams(dimension_semantics=("arbitrary", "arbitrary")),
        name="modulation",
    )(cond_t, w_mod, b_mod.reshape(depth, 1, 3 * D_MODEL))


def _mla_keys(cb, kr, wkk_ref, wkv_ref, kg, rope=None):
    kk = _dot(cb, wkk_ref[...])
    keys = []
    for h in range(N_HEADS):
        k = _rms(kk[:, h * LANES:(h + 1) * LANES] + kr, kg, QK_A)
        if rope is not None:
            k = rope(k)
        keys.append(k.astype(BF16))
    return keys, _dot(cb, wkv_ref[...]).astype(BF16)


def _p0_kernel(x_ref, m_ref, ng_ref, win_ref, qag_ref, wq_ref, kvag_ref, wkk_ref, wkv_ref, qg_ref, kg_ref,
               naqg_ref, nakg_ref, wout_ref,
               xo_ref, ckv_ref, krope_ref, nak_ref, nav_ref, y_scr):
    nbs = x_ref.shape[0]
    x = x_ref[...].reshape(nbs * SEQ, D_MODEL)
    h, gate = _modulate(x, ng_ref[...], m_ref[0:1, :])
    hb = h.astype(BF16)
    lo = _lane_lo()
    hi = jnp.logical_not(lo)
    rows = [slice(bi * SEQ, (bi + 1) * SEQ) for bi in range(nbs)]

    qln = _rms(_dot(hb, win_ref[:, E_QLAT:E_CKV]), qag_ref[...], Q_RANK).astype(BF16)
    q_all = _dot(qln, wq_ref[...])
    ckv_n = _rms(_dot(hb, win_ref[:, E_CKV:E_KROPE]), kvag_ref[...], KV_RANK)
    kr = _dot(hb, win_ref[:, E_KROPE:E_GA])
    for bi, rs in enumerate(rows):
        ckv_ref[bi, 0] = ckv_n[rs]
        krope_ref[bi, 0] = kr[rs].T[NOPE_A:QK_A]
    keys, vals = _mla_keys(ckv_n.astype(BF16), kr, wkk_ref, wkv_ref, kg_ref[...])
    qg = qg_ref[...] * (QK_A ** -0.5 * LOG2E)

    ga = _dot(hb, win_ref[:, E_GA:E_QB])
    zq = _dot(hb, win_ref[:, E_QB:E_KB])
    zk = _dot(hb, win_ref[:, E_KB:E_VB])
    zv = _dot(hb, win_ref[:, E_VB:E_GB])
    gb = _dot(hb, win_ref[:, E_GB:E_END])
    naqg = naqg_ref[...] * (HEAD_DIM ** -0.5 * LOG2E)

    for p in range(N_PAIRS):
        sl = slice(p * LANES, (p + 1) * LANES)
        ys = slice(4 * LANES + p * LANES, 4 * LANES + (p + 1) * LANES)
        qhs = [_rms(q_all[:, hh * LANES:(hh + 1) * LANES], qg, QK_A).astype(BF16) for hh in (2 * p, 2 * p + 1)]
        qb = _rms_halves(zq[:, sl], naqg, lo)
        kb = _rms_halves(zk[:, sl], nakg_ref[...], lo)
        vb = zv[:, sl]
        kbb, vbb = kb.astype(BF16), _with_ones(vb.astype(BF16))
        va = _with_ones(vals[:, sl])
        qms = [jnp.where(keep, qb, 0.0).astype(BF16) for keep in (lo, hi)]
        for bi, rs in enumerate(rows):
            o2 = [_attend([(_dot_nt(qhs[i][rs], keys[2 * p + i][rs]), va[rs], False)]) for i in (0, 1)]
            y_scr[rs, sl] = (jnp.where(lo, o2[0], o2[1]) * _silu(ga[rs, sl])).astype(BF16)
            _store_pair_transposed(nak_ref, bi, p, kb[rs])
            _store_pair_transposed(nav_ref, bi, p, vb[rs])
            o2 = [_attend([(_dot_nt(qms[i][rs], kbb[rs]), vbb[rs], False)]) for i in (0, 1)]
            y_scr[rs, ys] = (jnp.where(lo, o2[0], o2[1]) * _silu(gb[rs, sl])).astype(BF16)

    xo_ref[...] = (x + gate * _dot(y_scr[...], wout_ref[...])).reshape(nbs, SEQ, D_MODEL)


def _full(shape):
    n = len(shape)
    return pl.BlockSpec(shape, lambda *_: (0,) * n)


def _prompt_even(x, m, ng, win, qag, wq, kvag, wkk, wkv, qg, kg, naqg, nakg, wout):
    nb = x.shape[0]
    nbs = PROMPT_BATCHES_PER_STEP
    assert nb % nbs == 0
    ins = (m, ng, win, qag, wq, kvag, wkk, wkv, qg, kg, naqg, nakg, wout)
    return pl.pallas_call(
        _p0_kernel,
        grid=(nb // nbs,),
        in_specs=[pl.BlockSpec((nbs, SEQ, D_MODEL), lambda b: (b, 0, 0))] + [_full(a.shape) for a in ins],
        out_specs=[pl.BlockSpec((nbs, SEQ, D_MODEL), lambda b: (b, 0, 0)),
                   pl.BlockSpec((nbs, 1, SEQ, KV_RANK), lambda b: (b, 0, 0, 0)),
                   pl.BlockSpec((nbs, 1, ROPE_A, SEQ), lambda b: (b, 0, 0, 0)),
                   pl.BlockSpec((nbs, 1, N_HEADS, HEAD_DIM, SEQ), lambda b: (b, 0, 0, 0, 0)),
                   pl.BlockSpec((nbs, 1, N_HEADS, HEAD_DIM, SEQ), lambda b: (b, 0, 0, 0, 0))],
        out_shape=[jax.ShapeDtypeStruct((nb, SEQ, D_MODEL), F32),
                   jax.ShapeDtypeStruct((nb, 1, SEQ, KV_RANK), F32),
                   jax.ShapeDtypeStruct((nb, 1, ROPE_A, SEQ), F32),
                   jax.ShapeDtypeStruct((nb, 1, N_HEADS, HEAD_DIM, SEQ), F32),
                   jax.ShapeDtypeStruct((nb, 1, N_HEADS, HEAD_DIM, SEQ), F32)],
        scratch_shapes=[pltpu.VMEM((nbs * SEQ, D_MODEL), BF16)],
        compiler_params=pltpu.CompilerParams(dimension_semantics=("arbitrary",), vmem_limit_bytes=VMEM_LIMIT),
        name="prompt_even",
    )(x, *ins)


def _gqa_pair_operands(k, v, kg2, lo):
    kn = _rms_halves(k, kg2, lo)
    return kn, (kn.astype(BF16), pltpu.roll(kn, HEAD_DIM, 1).astype(BF16)), \
        (_with_ones(v.astype(BF16)), _with_ones(pltpu.roll(v, HEAD_DIM, 1).astype(BF16)))


def _p1_kernel(sink_ref, x_ref, m_ref, ng_ref, win_ref, gqg_ref, gkg_ref, sqg_ref, skg_ref, wout_ref,
               xo_ref, gk_ref, gv_ref, sk_ref, sv_ref, y_scr):
    nbs = x_ref.shape[0]
    x = x_ref[...].reshape(nbs * SEQ, D_MODEL)
    h, gate = _modulate(x, ng_ref[...], m_ref[0:1, :])
    hb = h.astype(BF16)
    lo = _lane_lo()
    hi = jnp.logical_not(lo)
    sc = HEAD_DIM ** -0.5 * LOG2E
    rows = [slice(bi * SEQ, (bi + 1) * SEQ) for bi in range(nbs)]

    branches = ((O_QC, O_KC, O_VC, O_GC, gqg_ref, gkg_ref, gk_ref, gv_ref, False, 0),
                (O_QD, O_KD, O_VD, O_GD, sqg_ref, skg_ref, sk_ref, sv_ref, True, 4 * LANES))
    for oq, ok, ov, og, qg_ref, kg_ref, ck_ref, cv_ref, has_sink, yoff in branches:
        zq = _dot(hb, win_ref[:, oq:oq + 4 * LANES])
        zkv = _dot(hb, win_ref[:, ok:ok + 2 * LANES])
        zg = _dot(hb, win_ref[:, og:og + 4 * LANES])
        v = zkv[:, LANES:]
        kn, ks, vs = _gqa_pair_operands(zkv[:, :LANES], v, kg_ref[...], lo)
        for bi, rs in enumerate(rows):
            _store_pair_transposed(ck_ref, bi, 0, kn[rs])
            _store_pair_transposed(cv_ref, bi, 0, v[rs])
        qg = qg_ref[...] * sc
        for p in range(N_PAIRS):
            sl = slice(p * LANES, (p + 1) * LANES)
            qn = _rms_halves(zq[:, sl], qg, lo)
            qms = [jnp.where(keep, qn, 0.0).astype(BF16) for keep in (lo, hi)]
            kv = p // 2
            for bi, rs in enumerate(rows):
                o2 = []
                for half in (0, 1):
                    swap = 0 if kv == half else 1
                    sink = sink_ref[2 * p + half] * LOG2E if has_sink else None
                    o2.append(_attend([(_dot_nt(qms[half][rs], ks[swap][rs]), vs[swap][rs], False)], sink))
                o = jnp.where(lo, o2[0], o2[1])
                y_scr[rs, yoff + p * LANES:yoff + (p + 1) * LANES] = (o * _silu(zg[rs, sl])).astype(BF16)

    xo_ref[...] = (x + gate * _dot(y_scr[...], wout_ref[...])).reshape(nbs, SEQ, D_MODEL)


def _prompt_odd(sink, x, m, ng, win, gqg, gkg, sqg, skg, wout):
    nb = x.shape[0]
    nbs = PROMPT_BATCHES_PER_STEP
    assert nb % nbs == 0
    ins = (m, ng, win, gqg, gkg, sqg, skg, wout)
    cache_spec = pl.BlockSpec((nbs, 1, 2, HEAD_DIM, SEQ), lambda b: (b, 0, 0, 0, 0))
    cache_shape = jax.ShapeDtypeStruct((nb, 1, 2, HEAD_DIM, SEQ), F32)
    return pl.pallas_call(
        _p1_kernel,
        grid=(nb // nbs,),
        in_specs=[pl.BlockSpec(memory_space=pltpu.SMEM),
                  pl.BlockSpec((nbs, SEQ, D_MODEL), lambda b: (b, 0, 0))] + [_full(a.shape) for a in ins],
        out_specs=[pl.BlockSpec((nbs, SEQ, D_MODEL), lambda b: (b, 0, 0))] + [cache_spec] * 4,
        out_shape=[jax.ShapeDtypeStruct((nb, SEQ, D_MODEL), F32)] + [cache_shape] * 4,
        scratch_shapes=[pltpu.VMEM((nbs * SEQ, D_MODEL), BF16)],
        compiler_params=pltpu.CompilerParams(dimension_semantics=("arbitrary",), vmem_limit_bytes=VMEM_LIMIT),
        name="prompt_odd",
    )(sink, x, *ins)


def _s0a_kernel(x_ref, m_ref, ng_ref, win_ref, qag_ref, wq_ref, kvag_ref, wkk_ref, wkv_ref, qg_ref, kg_ref,
                naqg_ref, nakg_ref, cos_ref, sin_ref,
                qa_ref, ka_ref, va_ref, qb_ref, kb_ref, vb_ref, g_ref):
    b = pl.program_id(0)
    lo = _lane_lo()
    ones_all = _group_ones2(LANES)
    partner = _rope_matrix2(ROPE_A, LANES, NOPE_A)
    hb = _modulate(x_ref[0], ng_ref[...], m_ref[pl.ds(1 + b, 1), :])[0].astype(BF16)
    cos, sin = cos_ref[...], sin_ref[...]

    qln = _rms(_dot(hb, win_ref[:, E_QLAT:E_CKV]), qag_ref[...], Q_RANK).astype(BF16)
    q_all = _dot(qln, wq_ref[...])
    ckv_n = _rms(_dot(hb, win_ref[:, E_CKV:E_KROPE]), kvag_ref[...], KV_RANK)
    kr = _dot(hb, win_ref[:, E_KROPE:E_GA])
    cb = ckv_n.astype(BF16)
    kk = _dot(cb, wkk_ref[...])
    va_ref[0] = _dot(cb, wkv_ref[...]).astype(BF16)
    zq = _dot(hb, win_ref[:, E_QB:E_KB])
    zk = _dot(hb, win_ref[:, E_KB:E_VB])
    vb_ref[0] = _dot(hb, win_ref[:, E_VB:E_GB]).astype(BF16)
    g_ref[0, :, 0:4 * LANES] = _silu(_dot(hb, win_ref[:, E_GA:E_QB]))
    g_ref[0, :, 4 * LANES:8 * LANES] = _silu(_dot(hb, win_ref[:, E_GB:E_END]))

    qg = qg_ref[...] * (QK_A ** -0.5 * LOG2E)
    kg = kg_ref[...]
    k_partner = _lane_mix(kr * kg, partner) * sin
    for hh in range(N_HEADS):
        sl = slice(hh * LANES, (hh + 1) * LANES)
        qn = _rms_mxu(q_all[:, sl], qg, QK_A, ones_all)
        qa_ref[0, :, sl] = (qn * cos + _lane_mix(qn, partner) * sin).astype(BF16)
        k_raw = kk[:, sl] + kr
        k_inv = lax.rsqrt(_lane_mix(k_raw * k_raw, ones_all) / QK_A + EPS)
        ka_ref[0, :, sl] = ((k_raw * kg * cos + k_partner) * k_inv).astype(BF16)
    naqg = naqg_ref[...] * (HEAD_DIM ** -0.5 * LOG2E)
    for p in range(N_PAIRS):
        sl = slice(p * LANES, (p + 1) * LANES)
        qb_ref[0, :, sl] = _rms_halves(zq[:, sl], naqg, lo).astype(BF16)
        kb_ref[0, :, sl] = _rms_halves(zk[:, sl], nakg_ref[...], lo).astype(BF16)


def _sample_even_proj(x, m, ng, win, qag, wq, kvag, wkk, wkv, qg, kg, naqg, nakg, cos, sin):
    nb, s, _ = x.shape
    nq = s // Q_BLOCK
    ins = (m, ng, win, qag, wq, kvag, wkk, wkv, qg, kg, naqg, nakg)
    tab = pl.BlockSpec((Q_BLOCK, LANES), lambda b, j: (j, 0))

    def blk(w):
        return pl.BlockSpec((1, Q_BLOCK, w), lambda b, j: (b, j, 0))

    def shp(w, dt):
        return jax.ShapeDtypeStruct((nb, s, w), dt)

    return pl.pallas_call(
        _s0a_kernel,
        grid=(nb, nq),
        in_specs=[blk(D_MODEL)] + [_full(a.shape) for a in ins] + [tab, tab],
        out_specs=[blk(1024), blk(1024), blk(512), blk(512), blk(512), blk(512), blk(1024)],
        out_shape=[shp(1024, BF16), shp(1024, BF16), shp(512, BF16), shp(512, BF16), shp(512, BF16),
                   shp(512, BF16), shp(1024, F32)],
        compiler_params=pltpu.CompilerParams(dimension_semantics=("arbitrary", "arbitrary"),
                                             vmem_limit_bytes=VMEM_LIMIT),
        name="sample_even_proj",
    )(x, *ins, cos, sin)


def _build_bias_table(rpb_ref, tile_scr, tab_ref):
    qc = lax.broadcasted_iota(jnp.int32, (GRID_W, LANES), 0)
    lane = lax.broadcasted_iota(jnp.int32, (GRID_W, LANES), 1)
    kc = jnp.bitwise_and(lane, GRID_W - 1)
    lo = lane < GRID_W
    diff = kc - qc + (NA_COLS - 1)
    cs = jnp.clip(qc - NA_COLS // 2, 0, GRID_W - NA_COLS)
    valid = (kc >= cs) & (kc < cs + NA_COLS)
    tab_ref[...] = jnp.zeros(tab_ref.shape, F32)
    tile_scr[RPB_ROWS] = jnp.zeros((GRID_W, LANES), F32)

    def per_head(h, carry):
        for dr in range(RPB_ROWS):
            t = jnp.zeros((GRID_W, LANES), F32)
            for dc in range(RPB_COLS):
                t = jnp.where(diff == dc, rpb_ref[(h * RPB_ROWS + dr) * RPB_COLS + dc], t)
            tile_scr[dr] = jnp.where(valid, t * LOG2E, NEG_INF)
        for c in range(NA_ROWS // 2, NA_ROWS // 2 + NA_ROWS):
            d0 = 2 * c - NA_ROWS
            tab_ref[0, h, c] = jnp.where(lo, tile_scr[d0], tile_scr[d0 + 1])
            tab_ref[1, h, c] = jnp.where(lo, tile_scr[d0 - 1 if d0 > 0 else RPB_ROWS], tile_scr[d0])
        return carry

    lax.fori_loop(0, N_HEADS, per_head, 0)


def _s0b_kernel(rpb_ref, x_ref, m_ref, qa_ref, ka_ref, va_ref, qb_ref, kb_ref, vb_ref, g_ref,
                cckv_ref, ckr_ref, cnk_ref, cnv_ref, wkk_ref, wkv_ref, kg_ref, wout_ref,
                xo_ref, kca_scr, vca_scr, tile_scr, tab_scr, y_scr):
    b = pl.program_id(0)
    j = pl.program_id(1)
    lo = _lane_lo()
    n_lat = ka_ref.shape[1]

    @pl.when((b == 0) & (j == 0))
    def _():
        _build_bias_table(rpb_ref, tile_scr, tab_scr)

    @pl.when(j == 0)
    def _():
        kr_t = jnp.concatenate([jnp.zeros((NOPE_A, PAST_LEN), F32), ckr_ref[0],
                                jnp.zeros((LANES - QK_A, PAST_LEN), F32)], axis=0)
        keys, vals = _mla_keys(cckv_ref[0].astype(BF16), kr_t.T, wkk_ref, wkv_ref, kg_ref[...])
        for hh in range(N_HEADS):
            kca_scr[:, hh * LANES:(hh + 1) * LANES] = keys[hh]
        vca_scr[...] = vals

    kidx = lax.broadcasted_iota(jnp.int32, (1, n_lat), 1)
    for p in range(N_PAIRS):
        sl = slice(p * LANES, (p + 1) * LANES)
        o2 = []
        va = _with_ones(va_ref[0, :, sl])
        vca = _with_ones(vca_scr[:, sl])
        for hh in (2 * p, 2 * p + 1):
            hs = slice(hh * LANES, (hh + 1) * LANES)
            q = qa_ref[0, :, hs]
            o2.append(_attend([(_dot_nt(q, ka_ref[0, :, hs]), va, False),
                               (_dot_nt(q, kca_scr[:, hs]), vca, False)]))
        oa = jnp.where(lo, o2[0], o2[1])
        y_scr[:, sl] = (oa * g_ref[0, :, sl]).astype(BF16)

        qb = qb_ref[0, :, sl]
        kb = kb_ref[0, :, sl]
        vb = _with_ones(vb_ref[0, :, sl])
        kcb = cnk_ref[0, sl, :].astype(BF16)
        vcb = _with_ones(cnv_ref[0, sl, :].astype(BF16), transposed=True)
        o2 = []
        for half in (0, 1):
            head = 2 * p + half
            qm = jnp.where(lo if half == 0 else jnp.logical_not(lo), qb, jnp.zeros_like(qb))
            s_lat = _dot_nt(qm, kb)
            rows = []
            for local in range(Q_BLOCK // GRID_W):
                qr = j * (Q_BLOCK // GRID_W) + local
                par = 0 if local % 2 == 1 else 1
                c0 = (RPB_ROWS + par - local) // 2 - 2 * j
                bias = jnp.concatenate([tab_scr[par, head, c0 + t] for t in range(n_lat // LANES)], axis=1)
                r0 = jnp.clip(qr - NA_ROWS // 2, 0, n_lat // GRID_W - NA_ROWS) * GRID_W
                ok = (kidx >= r0) & (kidx < r0 + NA_ROWS * GRID_W)
                rows.append(jnp.where(ok, s_lat[local * GRID_W:(local + 1) * GRID_W] + bias, NEG_INF))
            s_lat = jnp.concatenate(rows, axis=0)
            o2.append(_attend([(s_lat, vb, False), (_dot(qm, kcb), vcb, True)]))
        ob = jnp.where(lo, o2[0], o2[1])
        ys = slice(4 * LANES + p * LANES, 4 * LANES + (p + 1) * LANES)
        y_scr[:, ys] = (ob * g_ref[0, :, ys]).astype(BF16)

    d = x_ref.shape[-1]
    gate = m_ref[pl.ds(1 + b, 1), 2 * d:]
    xo_ref[0] = x_ref[0] + gate * _dot(y_scr[...], wout_ref[...])


def _sample_even_attn(rpb, x, m, qa, ka, va, qb, kb, vb, g, cckv, ckr, cnk, cnv, wkk, wkv, kg, wout):
    nb, s, _ = x.shape
    nq = s // Q_BLOCK

    def blk(w):
        return pl.BlockSpec((1, Q_BLOCK, w), lambda b, j: (b, j, 0))

    def per_batch(a):
        return pl.BlockSpec((1,) + a.shape[1:], lambda b, j: (b, 0, 0))

    return pl.pallas_call(
        _s0b_kernel,
        grid=(nb, nq),
        in_specs=[pl.BlockSpec(memory_space=pltpu.SMEM), blk(D_MODEL), _full(m.shape),
                  blk(1024), per_batch(ka), per_batch(va), blk(512), per_batch(kb), per_batch(vb), blk(1024),
                  per_batch(cckv), per_batch(ckr), per_batch(cnk), per_batch(cnv),
                  _full(wkk.shape), _full(wkv.shape), _full(kg.shape), _full(wout.shape)],
        out_specs=blk(D_MODEL),
        out_shape=jax.ShapeDtypeStruct(x.shape, F32),
        scratch_shapes=[pltpu.VMEM((PAST_LEN, N_HEADS * LANES), BF16),
                        pltpu.VMEM((PAST_LEN, N_HEADS * HEAD_DIM), BF16),
                        pltpu.VMEM((RPB_ROWS + 1, GRID_W, LANES), F32),
                        pltpu.VMEM((2, N_HEADS, BIAS_CHUNKS, GRID_W, LANES), F32),
                        pltpu.VMEM((Q_BLOCK, D_MODEL), BF16)],
        compiler_params=pltpu.CompilerParams(dimension_semantics=("arbitrary", "arbitrary"),
                                             vmem_limit_bytes=VMEM_LIMIT),
        name="sample_even_attn",
    )(rpb, x, m, qa, ka, va, qb, kb, vb, g, cckv, ckr, cnk, cnv, wkk, wkv, kg, wout)


def _s1a_kernel(x_ref, m_ref, ng_ref, win_ref, gqg_ref, gkg_ref, sqg_ref, skg_ref, cos_ref, sin_ref,
                qc_ref, kc_ref, vc_ref, qd_ref, kd_ref, vd_ref, g_ref):
    b = pl.program_id(0)
    lo = _lane_lo()
    partner = _rope_matrix2(HEAD_DIM, HEAD_DIM, 0)
    swap = _swap_matrix2()[:LANES]
    hb = _modulate(x_ref[0], ng_ref[...], m_ref[pl.ds(1 + b, 1), :])[0].astype(BF16)
    cos, sin = cos_ref[...], sin_ref[...]
    sc = HEAD_DIM ** -0.5 * LOG2E

    def rope(t):
        return t * cos + _lane_mix(t, partner) * sin

    branches = ((O_QC, O_KC, O_GC, gqg_ref, gkg_ref, qc_ref, kc_ref, vc_ref, 0),
                (O_QD, O_KD, O_GD, sqg_ref, skg_ref, qd_ref, kd_ref, vd_ref, 4 * LANES))
    for oq, ok, og, qg_ref, kg_ref, q_out, k_out, v_out, goff in branches:
        zq = _dot(hb, win_ref[:, oq:oq + 4 * LANES])
        zkv = _dot(hb, win_ref[:, ok:ok + 2 * LANES])
        qg = qg_ref[...] * sc
        for p in range(N_PAIRS):
            sl = slice(p * LANES, (p + 1) * LANES)
            q_out[0, :, sl] = rope(_rms_halves(zq[:, sl], qg, lo)).astype(BF16)
        kn = rope(_rms_halves(zkv[:, :LANES], kg_ref[...], lo))
        v = zkv[:, LANES:]
        for out, val in ((k_out, kn.astype(BF16)), (v_out, v.astype(BF16))):
            out[0, :, 0:LANES] = val
            out[0, :, LANES:2 * LANES] = _dot(val, swap).astype(BF16)
        g_ref[0, :, goff:goff + 4 * LANES] = _silu(_dot(hb, win_ref[:, og:og + 4 * LANES]))


def _sample_odd_proj(x, m, ng, win, gqg, gkg, sqg, skg, cos, sin):
    nb, s, _ = x.shape
    nq = s // Q_BLOCK
    ins = (m, ng, win, gqg, gkg, sqg, skg)
    tab = pl.BlockSpec((Q_BLOCK, LANES), lambda b, j: (j, 0))

    def blk(w):
        return pl.BlockSpec((1, Q_BLOCK, w), lambda b, j: (b, j, 0))

    def shp(w, dt):
        return jax.ShapeDtypeStruct((nb, s, w), dt)

    return pl.pallas_call(
        _s1a_kernel,
        grid=(nb, nq),
        in_specs=[blk(D_MODEL)] + [_full(a.shape) for a in ins] + [tab, tab],
        out_specs=[blk(512), blk(256), blk(256), blk(512), blk(256), blk(256), blk(1024)],
        out_shape=[shp(512, BF16), shp(256, BF16), shp(256, BF16), shp(512, BF16), shp(256, BF16),
                   shp(256, BF16), shp(1024, F32)],
        compiler_params=pltpu.CompilerParams(dimension_semantics=("arbitrary", "arbitrary"),
                                             vmem_limit_bytes=VMEM_LIMIT),
        name="sample_odd_proj",
    )(x, *ins, cos, sin)


def _s1b_kernel(sink_ref, x_ref, m_ref, qc_ref, kc_ref, vc_ref, qd_ref, kd_ref, vd_ref, g_ref,
                cgk_ref, cgv_ref, csk_ref, csv_ref, wout_ref, xo_ref, y_scr):
    b = pl.program_id(0)
    j = pl.program_id(1)
    lo = _lane_lo()
    n_lat = kc_ref.shape[1]
    win_keys = Q_BLOCK + 2 * SWA_HALF

    def ctx_pair(ref, values=False):
        a = ref[0].astype(BF16)
        pair = (a, _swap_halves(a))
        return tuple(_with_ones(t, transposed=True) for t in pair) if values else pair

    cgk, cgv, csk, csv = ctx_pair(cgk_ref), ctx_pair(cgv_ref, True), ctx_pair(csk_ref), ctx_pair(csv_ref, True)
    vcs = [_with_ones(vc_ref[0, :, w * LANES:(w + 1) * LANES]) for w in (0, 1)]

    ks = pl.multiple_of(jnp.clip(j * Q_BLOCK - SWA_HALF, 0, n_lat - win_keys), SWA_HALF)
    qpos = j * Q_BLOCK + lax.broadcasted_iota(jnp.int32, (Q_BLOCK, win_keys), 0)
    kpos = ks + lax.broadcasted_iota(jnp.int32, (Q_BLOCK, win_keys), 1)
    band = jnp.abs(qpos - kpos) <= SWA_HALF
    vds = [_with_ones(vd_ref[0, pl.ds(ks, win_keys), w * LANES:(w + 1) * LANES]) for w in (0, 1)]

    for p in range(N_PAIRS):
        sl = slice(p * LANES, (p + 1) * LANES)
        kv = p // 2
        qc = qc_ref[0, :, sl]
        qd = qd_ref[0, :, sl]
        oc2, od2 = [], []
        for half in (0, 1):
            swap = 0 if kv == half else 1
            ws = slice(swap * LANES, (swap + 1) * LANES)
            keep = lo if half == 0 else jnp.logical_not(lo)
            qm = jnp.where(keep, qc, jnp.zeros_like(qc))
            oc2.append(_attend([(_dot_nt(qm, kc_ref[0, :, ws]), vcs[swap], False),
                                (_dot(qm, cgk[swap]), cgv[swap], True)]))
            qm = jnp.where(keep, qd, jnp.zeros_like(qd))
            s_loc = jnp.where(band, _dot_nt(qm, kd_ref[0, pl.ds(ks, win_keys), ws]), NEG_INF)
            od2.append(_attend([(s_loc, vds[swap], False),
                                (_dot(qm, csk[swap]), csv[swap], True)], sink_ref[2 * p + half] * LOG2E))
        y_scr[:, sl] = (jnp.where(lo, oc2[0], oc2[1]) * g_ref[0, :, sl]).astype(BF16)
        ys = slice(4 * LANES + p * LANES, 4 * LANES + (p + 1) * LANES)
        y_scr[:, ys] = (jnp.where(lo, od2[0], od2[1]) * g_ref[0, :, ys]).astype(BF16)

    d = x_ref.shape[-1]
    gate = m_ref[pl.ds(1 + b, 1), 2 * d:]
    xo_ref[0] = x_ref[0] + gate * _dot(y_scr[...], wout_ref[...])


def _sample_odd_attn(sink, x, m, qc, kc, vc, qd, kd, vd, g, cgk, cgv, csk, csv, wout):
    nb, s, _ = x.shape
    nq = s // Q_BLOCK

    def blk(w):
        return pl.BlockSpec((1, Q_BLOCK, w), lambda b, j: (b, j, 0))

    def per_batch(a):
        return pl.BlockSpec((1,) + a.shape[1:], lambda b, j: (b, 0, 0))

    return pl.pallas_call(
        _s1b_kernel,
        grid=(nb, nq),
        in_specs=[pl.BlockSpec(memory_space=pltpu.SMEM), blk(D_MODEL), _full(m.shape),
                  blk(512), per_batch(kc), per_batch(vc), blk(512), per_batch(kd), per_batch(vd), blk(1024),
                  per_batch(cgk), per_batch(cgv), per_batch(csk), per_batch(csv), _full(wout.shape)],
        out_specs=blk(D_MODEL),
        out_shape=jax.ShapeDtypeStruct(x.shape, F32),
        scratch_shapes=[pltpu.VMEM((Q_BLOCK, D_MODEL), BF16)],
        compiler_params=pltpu.CompilerParams(dimension_semantics=("arbitrary", "arbitrary"),
                                             vmem_limit_bytes=VMEM_LIMIT),
        name="sample_odd_attn",
    )(sink, x, m, qc, kc, vc, qd, kd, vd, g, cgk, cgv, csk, csv, wout)


def _pad_cols(w, left, width):
    return jnp.pad(w, ((0, 0), (left, width - left - w.shape[1])))


def _per_head_slabs(w, n_heads, head_w, take):
    k = w.shape[0]
    wh = w.reshape(k, n_heads, head_w)[:, :, :take]
    return jnp.pad(wh, ((0, 0), (0, 0), (0, LANES - take))).reshape(k, n_heads * LANES)


def _feature_major(c):
    b, h, l, d = c.shape
    return jnp.swapaxes(c, -1, -2).reshape(b, h * d, l)


def _token_major(c):
    return jnp.swapaxes(c, -1, -2)


def _rope_tables(s, rot_dim, period, start):
    quarter = rot_dim // 4
    t = jnp.arange(s)
    inv = ROPE_THETA ** (-jnp.arange(quarter, dtype=F32) / quarter)
    row = (t // GRID_W).astype(F32)[:, None] * inv
    col = (t % GRID_W).astype(F32)[:, None] * inv
    ang = jnp.concatenate([row, col], axis=-1)
    cos, sin = jnp.cos(ang), jnp.sin(ang)
    pre = jnp.ones((s, start), F32)
    post = jnp.zeros((s, period - start - rot_dim), F32)
    c = jnp.concatenate([pre, cos, cos, post], axis=-1)
    sn = jnp.concatenate([0 * pre, sin, sin, post], axis=-1)
    rep = LANES // period
    return jnp.tile(c, (1, rep)), jnp.tile(sn, (1, rep))


def _row(v, width=None):
    v = v.reshape(1, -1).astype(F32)
    return v if width is None else jnp.pad(v, ((0, 0), (0, width - v.shape[1])))


def kernel(x_prompt, x_sample, cache_mla_ckv, cache_mla_krope, cache_na_k, cache_na_v, cache_gqa_k, cache_gqa_v, cache_swa_k, cache_swa_v, c, c_ctx, norm_g, w_mod, b_mod, w_in_even, mla_qa_g, w_q_up, mla_kva_g, w_kv_up, mla_q_g, mla_k_g, na_q_g, na_k_g, na_rpb, w_out_even, w_in_odd, gqa_q_g, gqa_k_g, swa_q_g, swa_k_g, swa_sink, w_out_odd):
    n_dec = x_sample.shape[0]
    assert w_mod.shape[0] == 2 and n_dec + 1 <= 8

    cond_t = jnp.concatenate([c_ctx[:, None], c.T, jnp.zeros((D_MODEL, 7 - n_dec), F32)], axis=1)
    m_all = _modulation(cond_t, 1 + n_dec, w_mod, b_mod)

    we = w_in_even[0]
    q_lat, ckv, krope, ga, qb, kb, vb, gb = jnp.split(we, [256, 384, 416, 928, 1440, 1952, 2464], axis=1)
    win_e = jnp.concatenate([q_lat, ckv, _pad_cols(krope, NOPE_A, LANES), ga, qb, kb, vb, gb], axis=1).astype(BF16)
    wq = _per_head_slabs(w_q_up[0], N_HEADS, QK_A, QK_A).astype(BF16)
    wkk = _per_head_slabs(w_kv_up[0], N_HEADS, NOPE_A + HEAD_DIM, NOPE_A).astype(BF16)
    wkv = w_kv_up[0].reshape(KV_RANK, N_HEADS, NOPE_A + HEAD_DIM)[:, :, NOPE_A:].reshape(KV_RANK, -1).astype(BF16)
    even = (_row(norm_g[0]), win_e, _row(mla_qa_g[0]), wq, _row(mla_kva_g[0]), wkk, wkv,
            _row(mla_q_g[0], LANES), _row(mla_k_g[0], LANES),
            _row(jnp.tile(na_q_g[0], 2)), _row(jnp.tile(na_k_g[0], 2)))
    wout_e = w_out_even[0].astype(BF16)

    win_o = w_in_odd[0].astype(BF16)
    odd = (_row(norm_g[1]), win_o, _row(jnp.tile(gqa_q_g[0], 2)), _row(jnp.tile(gqa_k_g[0], 2)),
           _row(jnp.tile(swa_q_g[0], 2)), _row(jnp.tile(swa_k_g[0], 2)))
    wout_o = w_out_odd[0].astype(BF16)
    sink = swa_sink[0].astype(F32)

    xp1, new_ckv, new_krope, new_na_k, new_na_v = _prompt_even(x_prompt, m_all[0], *even, wout_e)
    xp2, new_gqa_k, new_gqa_v, new_swa_k, new_swa_v = _prompt_odd(sink, xp1, m_all[1], *odd, wout_o)

    cos_e, sin_e = _rope_tables(DEC_SEQ, ROPE_A, LANES, NOPE_A)
    qa, ka, va, qbs, kbs, vbs, g0 = _sample_even_proj(x_sample, m_all[0], *even, cos_e, sin_e)
    ckr = jnp.swapaxes(cache_mla_krope[:, 0], -1, -2)
    xs1 = _sample_even_attn(na_rpb[0].reshape(-1), x_sample, m_all[0], qa, ka, va, qbs, kbs, vbs, g0,
                            cache_mla_ckv[:, 0], ckr, _feature_major(cache_na_k[:, 0]),
                            _feature_major(cache_na_v[:, 0]), wkk, wkv, even[8], wout_e)
    cos_o, sin_o = _rope_tables(DEC_SEQ, HEAD_DIM, HEAD_DIM, 0)
    qc, kc, vc, qd, kd, vd, g1 = _sample_odd_proj(xs1, m_all[1], *odd, cos_o, sin_o)
    xs2 = _sample_odd_attn(sink, xs1, m_all[1], qc, kc, vc, qd, kd, vd, g1,
                           _feature_major(cache_gqa_k[:, 0]), _feature_major(cache_gqa_v[:, 0]),
                           _feature_major(cache_swa_k[:, 0]), _feature_major(cache_swa_v[:, 0]), wout_o)

    caches = (new_krope, new_na_k, new_na_v, new_gqa_k, new_gqa_v, new_swa_k, new_swa_v)
    return (xp2, xs2, new_ckv) + tuple(_token_major(c) for c in caches)
```

```python
import functools

import jax
import jax.numpy as jnp
from jax import lax
from jax.experimental import pallas as pl
from jax.experimental.pallas import tpu as pltpu

F32 = jnp.float32
BF16 = jnp.bfloat16

D_MODEL = 1024
SEQ = 256
DEC_SEQ = 1024
PAST_LEN = 256
GRID_W = 64
HEAD_DIM = 64
Q_RANK = 256
KV_RANK = 128
NOPE_A = 64
ROPE_A = 32
QK_A = NOPE_A + ROPE_A
N_HEADS = 8
NA_ROWS = 8
NA_COLS = 16
SWA_HALF = 128
ROPE_THETA = 10000.0
EPS = 1e-6
NEG_INF = -1e30
LOG2E = 1.4426950408889634

LANES = 128
Q_BLOCK = 256
PROMPT_BATCHES_PER_STEP = 2
N_PAIRS = N_HEADS // 2
RPB_ROWS = 2 * NA_ROWS - 1
RPB_COLS = 2 * NA_COLS - 1
BIAS_CHUNKS = 16
VMEM_LIMIT = 48 * 1024 * 1024

E_QLAT, E_CKV, E_KROPE, E_GA, E_QB, E_KB, E_VB, E_GB, E_END = 0, 256, 384, 416, 928, 1440, 1952, 2464, 2976
O_QC, O_KC, O_VC, O_GC, O_QD, O_KD, O_VD, O_GD, O_END = 0, 512, 640, 768, 1280, 1792, 1920, 2048, 2560


def _dot(a, b):
    return lax.dot_general(a, b, (((1,), (0,)), ((), ())), preferred_element_type=F32)


def _dot_nt(a, b):
    return lax.dot_general(a, b, (((1,), (1,)), ((), ())), preferred_element_type=F32)


def _silu(x):
    return x / (1.0 + jnp.exp(-x))


def _rms(x, g, n):
    ss = jnp.sum(x * x, axis=-1, keepdims=True)
    return x * lax.rsqrt(ss / n + EPS) * g


def _rms_halves(x, g2, lo):
    x2 = x * x
    s_lo = jnp.sum(jnp.where(lo, x2, 0.0), axis=-1, keepdims=True)
    s_hi = jnp.sum(jnp.where(lo, 0.0, x2), axis=-1, keepdims=True)
    r = jnp.where(lo, lax.rsqrt(s_lo / HEAD_DIM + EPS), lax.rsqrt(s_hi / HEAD_DIM + EPS))
    return x * r * g2


def _modulate(x, g, m):
    d = x.shape[-1]
    xn = x * lax.rsqrt(jnp.mean(x * x, axis=-1, keepdims=True) + EPS) * g
    return xn * (1.0 + m[:, d:2 * d]) + m[:, :d], m[:, 2 * d:]


def _split_lanes(x):
    hi = x.astype(BF16)
    lo = (x - hi.astype(F32)).astype(BF16)
    return jnp.concatenate([hi, lo], axis=1)


def _lane_matrix2(entries):
    i = lax.broadcasted_iota(jnp.int32, (LANES, LANES), 0)
    j = lax.broadcasted_iota(jnp.int32, (LANES, LANES), 1)
    m = entries(i, j).astype(BF16)
    return jnp.concatenate([m, m], axis=0)


def _group_ones2(width):
    shift = width.bit_length() - 1
    return _lane_matrix2(lambda i, j: jnp.where((i >> shift) == (j >> shift), 1.0, 0.0))


def _rope_matrix2(rot_dim, period, start):
    half = rot_dim // 2

    def entries(i, j):
        pos = jnp.bitwise_and(j, period - 1) - start
        neg = (pos >= 0) & (pos < half) & (i == j + half)
        plus = (pos >= half) & (pos < rot_dim) & (i == j - half)
        return jnp.where(neg, -1.0, jnp.where(plus, 1.0, 0.0))

    return _lane_matrix2(entries)


def _swap_matrix2():
    return _lane_matrix2(lambda i, j: jnp.where(i == jnp.bitwise_xor(j, HEAD_DIM), 1.0, 0.0))


def _lane_mix(x, m2):
    return _dot(_split_lanes(x), m2)


def _rms_mxu(x, g, n, ones2):
    x2 = x * x
    nt = x.shape[-1] // LANES
    sq = x2[:, :LANES]
    for t in range(1, nt):
        sq = sq + x2[:, t * LANES:(t + 1) * LANES]
    r = lax.rsqrt(_lane_mix(sq, ones2) / n + EPS)
    return x * (r if nt == 1 else jnp.tile(r, (1, nt))) * g


def _modulate_mxu(x, g, m, ones2):
    d = x.shape[-1]
    return _rms_mxu(x, g, d, ones2) * (1.0 + m[:, d:2 * d]) + m[:, :d]


def _with_ones(v, transposed=False):
    if transposed:
        return jnp.concatenate([v, jnp.ones((LANES, v.shape[1]), v.dtype)], axis=0)
    return jnp.concatenate([v, jnp.ones((v.shape[0], LANES), v.dtype)], axis=1)


def _attend(parts, sink=None):
    mx = None
    for s, _, _ in parts:
        pm = jnp.max(s, axis=-1, keepdims=True)
        mx = pm if mx is None else jnp.maximum(mx, pm)
    if sink is not None:
        mx = jnp.maximum(mx, sink)
    acc = None
    for s, v, v_t in parts:
        po = (_dot_nt if v_t else _dot)(jnp.exp2(s - mx).astype(BF16), v)
        acc = po if acc is None else acc + po
    den = acc[:, LANES:]
    if sink is not None:
        den = den + jnp.exp2(sink - mx)
    return acc[:, :LANES] * (1.0 / den)


def _lane_lo():
    return lax.broadcasted_iota(jnp.int32, (1, LANES), 1) < HEAD_DIM


def _store_pair_transposed(ref, bi, p, x):
    xt = x.T
    ref[bi, 0, 2 * p] = xt[:HEAD_DIM]
    ref[bi, 0, 2 * p + 1] = xt[HEAD_DIM:]


def _rope_key_slab(win_ref):
    d = win_ref.shape[1]
    return jnp.concatenate([jnp.zeros((NOPE_A, d), BF16), win_ref[E_KROPE:E_GA, :],
                            jnp.zeros((LANES - QK_A, d), BF16)], axis=0)


def _swap_halves(a):
    return jnp.concatenate([a[HEAD_DIM:], a[:HEAD_DIM]], axis=0)


def _mod_kernel(n_cond, c_ref, w_ref, b_ref, o_ref):
    @pl.when(pl.program_id(1) == 0)
    def _():
        o_ref[0, :n_cond, :] = jnp.broadcast_to(b_ref[0], (n_cond, o_ref.shape[2]))
        o_ref[0, n_cond:, :] = jnp.zeros((o_ref.shape[1] - n_cond, o_ref.shape[2]), F32)

    s = _silu(c_ref[...])
    w = w_ref[0]
    for r in range(n_cond):
        o_ref[0, r:r + 1, :] += jnp.sum(w * s[:, r:r + 1], axis=0, keepdims=True)


def _modulation(cond_t, n_cond, w_mod, b_mod):
    depth = w_mod.shape[0]
    tk = 256
    return pl.pallas_call(
        functools.partial(_mod_kernel, n_cond),
        grid=(depth, D_MODEL // tk),
        in_specs=[pl.BlockSpec((tk, 8), lambda l, k: (k, 0)),
                  pl.BlockSpec((1, tk, 3 * D_MODEL), lambda l, k: (l, k, 0)),
                  pl.BlockSpec((1, 1, 3 * D_MODEL), lambda l, k: (l, 0, 0))],
        out_specs=pl.BlockSpec((1, 8, 3 * D_MODEL), lambda l, k: (l, 0, 0)),
        out_shape=jax.ShapeDtypeStruct((depth, 8, 3 * D_MODEL), F32),
        compiler_params=pltpu.CompilerParams(dimension_semantics=("arbitrary", "arbitrary")),
        name="modulation",
    )(cond_t, w_mod, b_mod.reshape(depth, 1, 3 * D_MODEL))


def _mla_keys(cb, kr, wkk_ref, wkv_ref, kg, rope=None):
    kk = _dot(cb, wkk_ref[...])
    keys = []
    for h in range(N_HEADS):
        k = _rms(kk[:, h * LANES:(h + 1) * LANES] + kr, kg, QK_A)
        if rope is not None:
            k = rope(k)
        keys.append(k.astype(BF16))
    return keys, _dot(cb, wkv_ref[...]).astype(BF16)


def _p0_kernel(x_ref, m_ref, ng_ref, win_ref, qag_ref, wq_ref, kvag_ref, wkk_ref, wkv_ref, qg_ref, kg_ref,
               naqg_ref, nakg_ref, wout_ref,
               xo_ref, ckv_ref, krope_ref, nak_ref, nav_ref, y_scr):
    nbs = x_ref.shape[0]
    x = x_ref[...].reshape(nbs * SEQ, D_MODEL)
    h, gate = _modulate(x, ng_ref[...], m_ref[0:1, :])
    hb = h.astype(BF16)
    lo = _lane_lo()
    hi = jnp.logical_not(lo)
    rows = [slice(bi * SEQ, (bi + 1) * SEQ) for bi in range(nbs)]

    qln = _rms(_dot_nt(hb, win_ref[E_QLAT:E_CKV, :]), qag_ref[...], Q_RANK).astype(BF16)
    q_all = _dot(qln, wq_ref[...])
    ckv_n = _rms(_dot_nt(hb, win_ref[E_CKV:E_KROPE, :]), kvag_ref[...], KV_RANK)
    kr = _dot_nt(hb, _rope_key_slab(win_ref))
    for bi, rs in enumerate(rows):
        ckv_ref[bi, 0] = ckv_n[rs]
        krope_ref[bi, 0] = kr[rs].T[NOPE_A:QK_A]
    keys, vals = _mla_keys(ckv_n.astype(BF16), kr, wkk_ref, wkv_ref, kg_ref[...])
    qg = qg_ref[...] * (QK_A ** -0.5 * LOG2E)

    ga = _dot_nt(hb, win_ref[E_GA:E_QB, :])
    zq = _dot_nt(hb, win_ref[E_QB:E_KB, :])
    zk = _dot_nt(hb, win_ref[E_KB:E_VB, :])
    zv = _dot_nt(hb, win_ref[E_VB:E_GB, :])
    gb = _dot_nt(hb, win_ref[E_GB:E_END, :])
    naqg = naqg_ref[...] * (HEAD_DIM ** -0.5 * LOG2E)

    for p in range(N_PAIRS):
        sl = slice(p * LANES, (p + 1) * LANES)
        ys = slice(4 * LANES + p * LANES, 4 * LANES + (p + 1) * LANES)
        qhs = [_rms(q_all[:, hh * LANES:(hh + 1) * LANES], qg, QK_A).astype(BF16) for hh in (2 * p, 2 * p + 1)]
        qb = _rms_halves(zq[:, sl], naqg, lo)
        kb = _rms_halves(zk[:, sl], nakg_ref[...], lo)
        vb = zv[:, sl]
        kbb, vbb = kb.astype(BF16), _with_ones(vb.astype(BF16))
        va = _with_ones(vals[:, sl])
        qms = [jnp.where(keep, qb, 0.0).astype(BF16) for keep in (lo, hi)]
        for bi, rs in enumerate(rows):
            o2 = [_attend([(_dot_nt(qhs[i][rs], keys[2 * p + i][rs]), va[rs], False)]) for i in (0, 1)]
            y_scr[rs, sl] = (jnp.where(lo, o2[0], o2[1]) * _silu(ga[rs, sl])).astype(BF16)
            _store_pair_transposed(nak_ref, bi, p, kb[rs])
            _store_pair_transposed(nav_ref, bi, p, vb[rs])
            o2 = [_attend([(_dot_nt(qms[i][rs], kbb[rs]), vbb[rs], False)]) for i in (0, 1)]
            y_scr[rs, ys] = (jnp.where(lo, o2[0], o2[1]) * _silu(gb[rs, sl])).astype(BF16)

    xo_ref[...] = (x + gate * _dot(y_scr[...], wout_ref[...])).reshape(nbs, SEQ, D_MODEL)


def _full(shape):
    n = len(shape)
    return pl.BlockSpec(shape, lambda *_: (0,) * n)


def _prompt_even(x, m, ng, win, qag, wq, kvag, wkk, wkv, qg, kg, naqg, nakg, wout):
    nb = x.shape[0]
    nbs = PROMPT_BATCHES_PER_STEP
    assert nb % nbs == 0
    ins = (m, ng, win, qag, wq, kvag, wkk, wkv, qg, kg, naqg, nakg, wout)
    return pl.pallas_call(
        _p0_kernel,
        grid=(nb // nbs,),
        in_specs=[pl.BlockSpec((nbs, SEQ, D_MODEL), lambda b: (b, 0, 0))] + [_full(a.shape) for a in ins],
        out_specs=[pl.BlockSpec((nbs, SEQ, D_MODEL), lambda b: (b, 0, 0)),
                   pl.BlockSpec((nbs, 1, SEQ, KV_RANK), lambda b: (b, 0, 0, 0)),
                   pl.BlockSpec((nbs, 1, ROPE_A, SEQ), lambda b: (b, 0, 0, 0)),
                   pl.BlockSpec((nbs, 1, N_HEADS, HEAD_DIM, SEQ), lambda b: (b, 0, 0, 0, 0)),
                   pl.BlockSpec((nbs, 1, N_HEADS, HEAD_DIM, SEQ), lambda b: (b, 0, 0, 0, 0))],
        out_shape=[jax.ShapeDtypeStruct((nb, SEQ, D_MODEL), F32),
                   jax.ShapeDtypeStruct((nb, 1, SEQ, KV_RANK), F32),
                   jax.ShapeDtypeStruct((nb, 1, ROPE_A, SEQ), F32),
                   jax.ShapeDtypeStruct((nb, 1, N_HEADS, HEAD_DIM, SEQ), F32),
                   jax.ShapeDtypeStruct((nb, 1, N_HEADS, HEAD_DIM, SEQ), F32)],
        scratch_shapes=[pltpu.VMEM((nbs * SEQ, D_MODEL), BF16)],
        compiler_params=pltpu.CompilerParams(dimension_semantics=("arbitrary",), vmem_limit_bytes=VMEM_LIMIT),
        name="prompt_even",
    )(x, *ins)


def _gqa_pair_operands(k, v, kg2, lo):
    kn = _rms_halves(k, kg2, lo)
    return kn, (kn.astype(BF16), pltpu.roll(kn, HEAD_DIM, 1).astype(BF16)), \
        (_with_ones(v.astype(BF16)), _with_ones(pltpu.roll(v, HEAD_DIM, 1).astype(BF16)))


def _p1_kernel(sink_ref, x_ref, m_ref, ng_ref, win_ref, gqg_ref, gkg_ref, sqg_ref, skg_ref, wout_ref,
               xo_ref, gk_ref, gv_ref, sk_ref, sv_ref, y_scr):
    nbs = x_ref.shape[0]
    x = x_ref[...].reshape(nbs * SEQ, D_MODEL)
    h, gate = _modulate(x, ng_ref[...], m_ref[0:1, :])
    hb = h.astype(BF16)
    lo = _lane_lo()
    hi = jnp.logical_not(lo)
    sc = HEAD_DIM ** -0.5 * LOG2E
    rows = [slice(bi * SEQ, (bi + 1) * SEQ) for bi in range(nbs)]

    branches = ((O_QC, O_KC, O_VC, O_GC, gqg_ref, gkg_ref, gk_ref, gv_ref, False, 0),
                (O_QD, O_KD, O_VD, O_GD, sqg_ref, skg_ref, sk_ref, sv_ref, True, 4 * LANES))
    for oq, ok, ov, og, qg_ref, kg_ref, ck_ref, cv_ref, has_sink, yoff in branches:
        zq = _dot(hb, win_ref[:, oq:oq + 4 * LANES])
        zkv = _dot(hb, win_ref[:, ok:ok + 2 * LANES])
        zg = _dot(hb, win_ref[:, og:og + 4 * LANES])
        v = zkv[:, LANES:]
        kn, ks, vs = _gqa_pair_operands(zkv[:, :LANES], v, kg_ref[...], lo)
        for bi, rs in enumerate(rows):
            _store_pair_transposed(ck_ref, bi, 0, kn[rs])
            _store_pair_transposed(cv_ref, bi, 0, v[rs])
        qg = qg_ref[...] * sc
        for p in range(N_PAIRS):
            sl = slice(p * LANES, (p + 1) * LANES)
            qn = _rms_halves(zq[:, sl], qg, lo)
            qms = [jnp.where(keep, qn, 0.0).astype(BF16) for keep in (lo, hi)]
            kv = p // 2
            for bi, rs in enumerate(rows):
                o2 = []
                for half in (0, 1):
                    swap = 0 if kv == half else 1
                    sink = sink_ref[2 * p + half] * LOG2E if has_sink else None
                    o2.append(_attend([(_dot_nt(qms[half][rs], ks[swap][rs]), vs[swap][rs], False)], sink))
                o = jnp.where(lo, o2[0], o2[1])
                y_scr[rs, yoff + p * LANES:yoff + (p + 1) * LANES] = (o * _silu(zg[rs, sl])).astype(BF16)

    xo_ref[...] = (x + gate * _dot(y_scr[...], wout_ref[...])).reshape(nbs, SEQ, D_MODEL)


def _prompt_odd(sink, x, m, ng, win, gqg, gkg, sqg, skg, wout):
    nb = x.shape[0]
    nbs = PROMPT_BATCHES_PER_STEP
    assert nb % nbs == 0
    ins = (m, ng, win, gqg, gkg, sqg, skg, wout)
    cache_spec = pl.BlockSpec((nbs, 1, 2, HEAD_DIM, SEQ), lambda b: (b, 0, 0, 0, 0))
    cache_shape = jax.ShapeDtypeStruct((nb, 1, 2, HEAD_DIM, SEQ), F32)
    return pl.pallas_call(
        _p1_kernel,
        grid=(nb // nbs,),
        in_specs=[pl.BlockSpec(memory_space=pltpu.SMEM),
                  pl.BlockSpec((nbs, SEQ, D_MODEL), lambda b: (b, 0, 0))] + [_full(a.shape) for a in ins],
        out_specs=[pl.BlockSpec((nbs, SEQ, D_MODEL), lambda b: (b, 0, 0))] + [cache_spec] * 4,
        out_shape=[jax.ShapeDtypeStruct((nb, SEQ, D_MODEL), F32)] + [cache_shape] * 4,
        scratch_shapes=[pltpu.VMEM((nbs * SEQ, D_MODEL), BF16)],
        compiler_params=pltpu.CompilerParams(dimension_semantics=("arbitrary",), vmem_limit_bytes=VMEM_LIMIT),
        name="prompt_odd",
    )(sink, x, *ins)


def _s0a_kernel(x_ref, m_ref, ng_ref, win_ref, qag_ref, wq_ref, kvag_ref, wkk_ref, wkv_ref, qg_ref, kg_ref,
                naqg_ref, nakg_ref, cos_ref, sin_ref,
                qa_ref, ka_ref, va_ref, qb_ref, kb_ref, vb_ref, g_ref):
    b = pl.program_id(0)
    lo = _lane_lo()
    ones_all = _group_ones2(LANES)
    partner = _rope_matrix2(ROPE_A, LANES, NOPE_A)
    hb = _modulate(x_ref[0], ng_ref[...], m_ref[pl.ds(1 + b, 1), :])[0].astype(BF16)
    cos, sin = cos_ref[...], sin_ref[...]

    qln = _rms(_dot_nt(hb, win_ref[E_QLAT:E_CKV, :]), qag_ref[...], Q_RANK).astype(BF16)
    q_all = _dot(qln, wq_ref[...])
    ckv_n = _rms(_dot_nt(hb, win_ref[E_CKV:E_KROPE, :]), kvag_ref[...], KV_RANK)
    kr = _dot_nt(hb, _rope_key_slab(win_ref))
    cb = ckv_n.astype(BF16)
    kk = _dot(cb, wkk_ref[...])
    va_ref[0] = _dot(cb, wkv_ref[...]).astype(BF16)
    zq = _dot_nt(hb, win_ref[E_QB:E_KB, :])
    zk = _dot_nt(hb, win_ref[E_KB:E_VB, :])
    vb_ref[0] = _dot_nt(hb, win_ref[E_VB:E_GB, :]).astype(BF16)
    g_ref[0, :, 0:4 * LANES] = _silu(_dot_nt(hb, win_ref[E_GA:E_QB, :]))
    g_ref[0, :, 4 * LANES:8 * LANES] = _silu(_dot_nt(hb, win_ref[E_GB:E_END, :]))

    qg = qg_ref[...] * (QK_A ** -0.5 * LOG2E)
    kg = kg_ref[...]
    k_partner = _lane_mix(kr * kg, partner) * sin
    for hh in range(N_HEADS):
        sl = slice(hh * LANES, (hh + 1) * LANES)
        qn = _rms_mxu(q_all[:, sl], qg, QK_A, ones_all)
        qa_ref[0, :, sl] = (qn * cos + _lane_mix(qn, partner) * sin).astype(BF16)
        k_raw = kk[:, sl] + kr
        k_inv = lax.rsqrt(_lane_mix(k_raw * k_raw, ones_all) / QK_A + EPS)
        ka_ref[0, :, sl] = ((k_raw * kg * cos + k_partner) * k_inv).astype(BF16)
    naqg = naqg_ref[...] * (HEAD_DIM ** -0.5 * LOG2E)
    for p in range(N_PAIRS):
        sl = slice(p * LANES, (p + 1) * LANES)
        qb_ref[0, :, sl] = _rms_halves(zq[:, sl], naqg, lo).astype(BF16)
        kb_ref[0, :, sl] = _rms_halves(zk[:, sl], nakg_ref[...], lo).astype(BF16)


def _sample_even_proj(x, m, ng, win, qag, wq, kvag, wkk, wkv, qg, kg, naqg, nakg, cos, sin):
    nb, s, _ = x.shape
    nq = s // Q_BLOCK
    ins = (m, ng, win, qag, wq, kvag, wkk, wkv, qg, kg, naqg, nakg)
    tab = pl.BlockSpec((Q_BLOCK, LANES), lambda b, j: (j, 0))

    def blk(w):
        return pl.BlockSpec((1, Q_BLOCK, w), lambda b, j: (b, j, 0))

    def shp(w, dt):
        return jax.ShapeDtypeStruct((nb, s, w), dt)

    return pl.pallas_call(
        _s0a_kernel,
        grid=(nb, nq),
        in_specs=[blk(D_MODEL)] + [_full(a.shape) for a in ins] + [tab, tab],
        out_specs=[blk(1024), blk(1024), blk(512), blk(512), blk(512), blk(512), blk(1024)],
        out_shape=[shp(1024, BF16), shp(1024, BF16), shp(512, BF16), shp(512, BF16), shp(512, BF16),
                   shp(512, BF16), shp(1024, F32)],
        compiler_params=pltpu.CompilerParams(dimension_semantics=("arbitrary", "arbitrary"),
                                             vmem_limit_bytes=VMEM_LIMIT),
        name="sample_even_proj",
    )(x, *ins, cos, sin)


def _build_bias_table(rpb_ref, tile_scr, tab_ref):
    qc = lax.broadcasted_iota(jnp.int32, (GRID_W, LANES), 0)
    lane = lax.broadcasted_iota(jnp.int32, (GRID_W, LANES), 1)
    kc = jnp.bitwise_and(lane, GRID_W - 1)
    lo = lane < GRID_W
    diff = kc - qc + (NA_COLS - 1)
    cs = jnp.clip(qc - NA_COLS // 2, 0, GRID_W - NA_COLS)
    valid = (kc >= cs) & (kc < cs + NA_COLS)
    tab_ref[...] = jnp.zeros(tab_ref.shape, F32)
    tile_scr[RPB_ROWS] = jnp.zeros((GRID_W, LANES), F32)

    def per_head(h, carry):
        for dr in range(RPB_ROWS):
            t = jnp.zeros((GRID_W, LANES), F32)
            for dc in range(RPB_COLS):
                t = jnp.where(diff == dc, rpb_ref[(h * RPB_ROWS + dr) * RPB_COLS + dc], t)
            tile_scr[dr] = jnp.where(valid, t * LOG2E, NEG_INF)
        for c in range(NA_ROWS // 2, NA_ROWS // 2 + NA_ROWS):
            d0 = 2 * c - NA_ROWS
            tab_ref[0, h, c] = jnp.where(lo, tile_scr[d0], tile_scr[d0 + 1])
            tab_ref[1, h, c] = jnp.where(lo, tile_scr[d0 - 1 if d0 > 0 else RPB_ROWS], tile_scr[d0])
        return carry

    lax.fori_loop(0, N_HEADS, per_head, 0)


def _s0b_kernel(rpb_ref, x_ref, m_ref, qa_ref, ka_ref, va_ref, qb_ref, kb_ref, vb_ref, g_ref,
                cckv_ref, ckr_ref, cnk_ref, cnv_ref, wkk_ref, wkv_ref, kg_ref, wout_ref,
                xo_ref, kca_scr, vca_scr, tile_scr, tab_scr, y_scr):
    b = pl.program_id(0)
    j = pl.program_id(1)
    lo = _lane_lo()
    n_lat = ka_ref.shape[1]

    @pl.when((b == 0) & (j == 0))
    def _():
        _build_bias_table(rpb_ref, tile_scr, tab_scr)

    @pl.when(j == 0)
    def _():
        kr_t = jnp.concatenate([jnp.zeros((NOPE_A, PAST_LEN), F32), ckr_ref[0],
                                jnp.zeros((LANES - QK_A, PAST_LEN), F32)], axis=0)
        keys, vals = _mla_keys(cckv_ref[0].astype(BF16), kr_t.T, wkk_ref, wkv_ref, kg_ref[...])
        for hh in range(N_HEADS):
            kca_scr[:, hh * LANES:(hh + 1) * LANES] = keys[hh]
        vca_scr[...] = vals

    kidx = lax.broadcasted_iota(jnp.int32, (1, n_lat), 1)
    for p in range(N_PAIRS):
        sl = slice(p * LANES, (p + 1) * LANES)
        o2 = []
        va = _with_ones(va_ref[0, :, sl])
        vca = _with_ones(vca_scr[:, sl])
        for hh in (2 * p, 2 * p + 1):
            hs = slice(hh * LANES, (hh + 1) * LANES)
            q = qa_ref[0, :, hs]
            o2.append(_attend([(_dot_nt(q, ka_ref[0, :, hs]), va, False),
                               (_dot_nt(q, kca_scr[:, hs]), vca, False)]))
        oa = jnp.where(lo, o2[0], o2[1])
        y_scr[:, sl] = (oa * g_ref[0, :, sl]).astype(BF16)

        qb = qb_ref[0, :, sl]
        kb = kb_ref[0, :, sl]
        vb = _with_ones(vb_ref[0, :, sl])
        kcb = cnk_ref[0, sl, :].astype(BF16)
        vcb = _with_ones(cnv_ref[0, sl, :].astype(BF16), transposed=True)
        o2 = []
        for half in (0, 1):
            head = 2 * p + half
            qm = jnp.where(lo if half == 0 else jnp.logical_not(lo), qb, jnp.zeros_like(qb))
            s_lat = _dot_nt(qm, kb)
            rows = []
            for local in range(Q_BLOCK // GRID_W):
                qr = j * (Q_BLOCK // GRID_W) + local
                par = 0 if local % 2 == 1 else 1
                c0 = (RPB_ROWS + par - local) // 2 - 2 * j
                bias = jnp.concatenate([tab_scr[par, head, c0 + t] for t in range(n_lat // LANES)], axis=1)
                r0 = jnp.clip(qr - NA_ROWS // 2, 0, n_lat // GRID_W - NA_ROWS) * GRID_W
                ok = (kidx >= r0) & (kidx < r0 + NA_ROWS * GRID_W)
                rows.append(jnp.where(ok, s_lat[local * GRID_W:(local + 1) * GRID_W] + bias, NEG_INF))
            s_lat = jnp.concatenate(rows, axis=0)
            o2.append(_attend([(s_lat, vb, False), (_dot(qm, kcb), vcb, True)]))
        ob = jnp.where(lo, o2[0], o2[1])
        ys = slice(4 * LANES + p * LANES, 4 * LANES + (p + 1) * LANES)
        y_scr[:, ys] = (ob * g_ref[0, :, ys]).astype(BF16)

    d = x_ref.shape[-1]
    gate = m_ref[pl.ds(1 + b, 1), 2 * d:]
    xo_ref[0] = x_ref[0] + gate * _dot(y_scr[...], wout_ref[...])


def _sample_even_attn(rpb, x, m, qa, ka, va, qb, kb, vb, g, cckv, ckr, cnk, cnv, wkk, wkv, kg, wout):
    nb, s, _ = x.shape
    nq = s // Q_BLOCK

    def blk(w):
        return pl.BlockSpec((1, Q_BLOCK, w), lambda b, j: (b, j, 0))

    def per_batch(a):
        return pl.BlockSpec((1,) + a.shape[1:], lambda b, j: (b, 0, 0))

    return pl.pallas_call(
        _s0b_kernel,
        grid=(nb, nq),
        in_specs=[pl.BlockSpec(memory_space=pltpu.SMEM), blk(D_MODEL), _full(m.shape),
                  blk(1024), per_batch(ka), per_batch(va), blk(512), per_batch(kb), per_batch(vb), blk(1024),
                  per_batch(cckv), per_batch(ckr), per_batch(cnk), per_batch(cnv),
                  _full(wkk.shape), _full(wkv.shape), _full(kg.shape), _full(wout.shape)],
        out_specs=blk(D_MODEL),
        out_shape=jax.ShapeDtypeStruct(x.shape, F32),
        scratch_shapes=[pltpu.VMEM((PAST_LEN, N_HEADS * LANES), BF16),
                        pltpu.VMEM((PAST_LEN, N_HEADS * HEAD_DIM), BF16),
                        pltpu.VMEM((RPB_ROWS + 1, GRID_W, LANES), F32),
                        pltpu.VMEM((2, N_HEADS, BIAS_CHUNKS, GRID_W, LANES), F32),
                        pltpu.VMEM((Q_BLOCK, D_MODEL), BF16)],
        compiler_params=pltpu.CompilerParams(dimension_semantics=("arbitrary", "arbitrary"),
                                             vmem_limit_bytes=VMEM_LIMIT),
        name="sample_even_attn",
    )(rpb, x, m, qa, ka, va, qb, kb, vb, g, cckv, ckr, cnk, cnv, wkk, wkv, kg, wout)


def _s1a_kernel(x_ref, m_ref, ng_ref, win_ref, gqg_ref, gkg_ref, sqg_ref, skg_ref, cos_ref, sin_ref,
                qc_ref, kc_ref, vc_ref, qd_ref, kd_ref, vd_ref, g_ref):
    b = pl.program_id(0)
    lo = _lane_lo()
    partner = _rope_matrix2(HEAD_DIM, HEAD_DIM, 0)
    swap = _swap_matrix2()[:LANES]
    hb = _modulate(x_ref[0], ng_ref[...], m_ref[pl.ds(1 + b, 1), :])[0].astype(BF16)
    cos, sin = cos_ref[...], sin_ref[...]
    sc = HEAD_DIM ** -0.5 * LOG2E

    def rope(t):
        return t * cos + _lane_mix(t, partner) * sin

    branches = ((O_QC, O_KC, O_GC, gqg_ref, gkg_ref, qc_ref, kc_ref, vc_ref, 0),
                (O_QD, O_KD, O_GD, sqg_ref, skg_ref, qd_ref, kd_ref, vd_ref, 4 * LANES))
    for oq, ok, og, qg_ref, kg_ref, q_out, k_out, v_out, goff in branches:
        zq = _dot(hb, win_ref[:, oq:oq + 4 * LANES])
        zkv = _dot(hb, win_ref[:, ok:ok + 2 * LANES])
        qg = qg_ref[...] * sc
        for p in range(N_PAIRS):
            sl = slice(p * LANES, (p + 1) * LANES)
            q_out[0, :, sl] = rope(_rms_halves(zq[:, sl], qg, lo)).astype(BF16)
        kn = rope(_rms_halves(zkv[:, :LANES], kg_ref[...], lo))
        v = zkv[:, LANES:]
        for out, val in ((k_out, kn.astype(BF16)), (v_out, v.astype(BF16))):
            out[0, :, 0:LANES] = val
            out[0, :, LANES:2 * LANES] = _dot(val, swap).astype(BF16)
        g_ref[0, :, goff:goff + 4 * LANES] = _silu(_dot(hb, win_ref[:, og:og + 4 * LANES]))


def _sample_odd_proj(x, m, ng, win, gqg, gkg, sqg, skg, cos, sin):
    nb, s, _ = x.shape
    nq = s // Q_BLOCK
    ins = (m, ng, win, gqg, gkg, sqg, skg)
    tab = pl.BlockSpec((Q_BLOCK, LANES), lambda b, j: (j, 0))

    def blk(w):
        return pl.BlockSpec((1, Q_BLOCK, w), lambda b, j: (b, j, 0))

    def shp(w, dt):
        return jax.ShapeDtypeStruct((nb, s, w), dt)

    return pl.pallas_call(
        _s1a_kernel,
        grid=(nb, nq),
        in_specs=[blk(D_MODEL)] + [_full(a.shape) for a in ins] + [tab, tab],
        out_specs=[blk(512), blk(256), blk(256), blk(512), blk(256), blk(256), blk(1024)],
        out_shape=[shp(512, BF16), shp(256, BF16), shp(256, BF16), shp(512, BF16), shp(256, BF16),
                   shp(256, BF16), shp(1024, F32)],
        compiler_params=pltpu.CompilerParams(dimension_semantics=("arbitrary", "arbitrary"),
                                             vmem_limit_bytes=VMEM_LIMIT),
        name="sample_odd_proj",
    )(x, *ins, cos, sin)


def _s1b_kernel(sink_ref, x_ref, m_ref, qc_ref, kc_ref, vc_ref, qd_ref, kd_ref, vd_ref, g_ref,
                cgk_ref, cgv_ref, csk_ref, csv_ref, wout_ref, xo_ref, y_scr):
    b = pl.program_id(0)
    j = pl.program_id(1)
    lo = _lane_lo()
    n_lat = kc_ref.shape[1]
    win_keys = Q_BLOCK + 2 * SWA_HALF

    def ctx_pair(ref, values=False):
        a = ref[0].astype(BF16)
        pair = (a, _swap_halves(a))
        return tuple(_with_ones(t, transposed=True) for t in pair) if values else pair

    cgk, cgv, csk, csv = ctx_pair(cgk_ref), ctx_pair(cgv_ref, True), ctx_pair(csk_ref), ctx_pair(csv_ref, True)
    vcs = [_with_ones(vc_ref[0, :, w * LANES:(w + 1) * LANES]) for w in (0, 1)]

    ks = pl.multiple_of(jnp.clip(j * Q_BLOCK - SWA_HALF, 0, n_lat - win_keys), SWA_HALF)
    qpos = j * Q_BLOCK + lax.broadcasted_iota(jnp.int32, (Q_BLOCK, win_keys), 0)
    kpos = ks + lax.broadcasted_iota(jnp.int32, (Q_BLOCK, win_keys), 1)
    band = jnp.abs(qpos - kpos) <= SWA_HALF
    vds = [_with_ones(vd_ref[0, pl.ds(ks, win_keys), w * LANES:(w + 1) * LANES]) for w in (0, 1)]

    for p in range(N_PAIRS):
        sl = slice(p * LANES, (p + 1) * LANES)
        kv = p // 2
        qc = qc_ref[0, :, sl]
        qd = qd_ref[0, :, sl]
        oc2, od2 = [], []
        for half in (0, 1):
            swap = 0 if kv == half else 1
            ws = slice(swap * LANES, (swap + 1) * LANES)
            keep = lo if half == 0 else jnp.logical_not(lo)
            qm = jnp.where(keep, qc, jnp.zeros_like(qc))
            oc2.append(_attend([(_dot_nt(qm, kc_ref[0, :, ws]), vcs[swap], False),
                                (_dot(qm, cgk[swap]), cgv[swap], True)]))
            qm = jnp.where(keep, qd, jnp.zeros_like(qd))
            s_loc = jnp.where(band, _dot_nt(qm, kd_ref[0, pl.ds(ks, win_keys), ws]), NEG_INF)
            od2.append(_attend([(s_loc, vds[swap], False),
                                (_dot(qm, csk[swap]), csv[swap], True)], sink_ref[2 * p + half] * LOG2E))
        y_scr[:, sl] = (jnp.where(lo, oc2[0], oc2[1]) * g_ref[0, :, sl]).astype(BF16)
        ys = slice(4 * LANES + p * LANES, 4 * LANES + (p + 1) * LANES)
        y_scr[:, ys] = (jnp.where(lo, od2[0], od2[1]) * g_ref[0, :, ys]).astype(BF16)

    d = x_ref.shape[-1]
    gate = m_ref[pl.ds(1 + b, 1), 2 * d:]
    xo_ref[0] = x_ref[0] + gate * _dot(y_scr[...], wout_ref[...])


def _sample_odd_attn(sink, x, m, qc, kc, vc, qd, kd, vd, g, cgk, cgv, csk, csv, wout):
    nb, s, _ = x.shape
    nq = s // Q_BLOCK

    def blk(w):
        return pl.BlockSpec((1, Q_BLOCK, w), lambda b, j: (b, j, 0))

    def per_batch(a):
        return pl.BlockSpec((1,) + a.shape[1:], lambda b, j: (b, 0, 0))

    return pl.pallas_call(
        _s1b_kernel,
        grid=(nb, nq),
        in_specs=[pl.BlockSpec(memory_space=pltpu.SMEM), blk(D_MODEL), _full(m.shape),
                  blk(512), per_batch(kc), per_batch(vc), blk(512), per_batch(kd), per_batch(vd), blk(1024),
                  per_batch(cgk), per_batch(cgv), per_batch(csk), per_batch(csv), _full(wout.shape)],
        out_specs=blk(D_MODEL),
        out_shape=jax.ShapeDtypeStruct(x.shape, F32),
        scratch_shapes=[pltpu.VMEM((Q_BLOCK, D_MODEL), BF16)],
        compiler_params=pltpu.CompilerParams(dimension_semantics=("arbitrary", "arbitrary"),
                                             vmem_limit_bytes=VMEM_LIMIT),
        name="sample_odd_attn",
    )(sink, x, m, qc, kc, vc, qd, kd, vd, g, cgk, cgv, csk, csv, wout)


WEIGHT_PREP_STEPS = 8


def _weight_prep_kernel(wie_ref, wio_ref, woe_ref, woo_ref, wqu_ref, wkv_ref,
                        win_e_ref, win_o_ref, wout_e_ref, wout_o_ref, wq_ref, wkk_ref, wkvv_ref):
    win_e_ref[...] = wie_ref[...].astype(BF16)
    win_o_ref[...] = wio_ref[0].astype(BF16)
    wout_e_ref[...] = woe_ref[0].astype(BF16)
    wout_o_ref[...] = woo_ref[0].astype(BF16)

    wq_ref[...] = jnp.zeros(wq_ref.shape, BF16)
    for h in range(N_HEADS):
        wq_ref[:, h * LANES:h * LANES + QK_A] = wqu_ref[0, :, h * QK_A:(h + 1) * QK_A].astype(BF16)
    lo = _lane_lo()
    for p in range(N_PAIRS):
        a = wkv_ref[0, :, (2 * p) * LANES:(2 * p + 1) * LANES]
        c = wkv_ref[0, :, (2 * p + 1) * LANES:(2 * p + 2) * LANES]
        wkk_ref[:, (2 * p) * LANES:(2 * p + 1) * LANES] = jnp.where(lo, a, 0.0).astype(BF16)
        wkk_ref[:, (2 * p + 1) * LANES:(2 * p + 2) * LANES] = jnp.where(lo, c, 0.0).astype(BF16)
        wkvv_ref[:, p * LANES:(p + 1) * LANES] = jnp.where(lo, pltpu.roll(a, HEAD_DIM, 1), c).astype(BF16)


def _weight_prep(w_in_even_t, w_in_odd, w_out_even, w_out_odd, w_q_up, w_kv_up):
    n = WEIGHT_PREP_STEPS
    ins = (w_in_odd, w_out_even, w_out_odd, w_q_up, w_kv_up)
    out_cols = (O_END, D_MODEL, D_MODEL, N_HEADS * LANES, N_HEADS * LANES, N_HEADS * HEAD_DIM)
    out_rows = (D_MODEL, D_MODEL, D_MODEL, Q_RANK, KV_RANK, KV_RANK)
    te = E_END // 6
    assert te * 6 == E_END and te % 16 == 0
    even_spec = pl.BlockSpec((te, D_MODEL), lambda i: (jnp.minimum(i, 5), 0))
    return pl.pallas_call(
        _weight_prep_kernel,
        grid=(n,),
        in_specs=[even_spec] + [pl.BlockSpec((1, a.shape[1] // n, a.shape[2]), lambda i: (0, i, 0)) for a in ins],
        out_specs=[even_spec] + [pl.BlockSpec((r // n, c), lambda i: (i, 0)) for r, c in zip(out_rows, out_cols)],
        out_shape=[jax.ShapeDtypeStruct((E_END, D_MODEL), BF16)]
        + [jax.ShapeDtypeStruct((r, c), BF16) for r, c in zip(out_rows, out_cols)],
        compiler_params=pltpu.CompilerParams(dimension_semantics=("arbitrary",), vmem_limit_bytes=VMEM_LIMIT),
        name="weight_prep",
    )(w_in_even_t, *ins)


def _pad_cols(w, left, width):
    return jnp.pad(w, ((0, 0), (left, width - left - w.shape[1])))


def _per_head_slabs(w, n_heads, head_w, take):
    k = w.shape[0]
    wh = w.reshape(k, n_heads, head_w)[:, :, :take]
    return jnp.pad(wh, ((0, 0), (0, 0), (0, LANES - take))).reshape(k, n_heads * LANES)


def _feature_major(c):
    b, h, l, d = c.shape
    return jnp.swapaxes(c, -1, -2).reshape(b, h * d, l)


def _token_major(c):
    return jnp.swapaxes(c, -1, -2)


def _rope_tables(s, rot_dim, period, start):
    quarter = rot_dim // 4
    t = jnp.arange(s)
    inv = ROPE_THETA ** (-jnp.arange(quarter, dtype=F32) / quarter)
    row = (t // GRID_W).astype(F32)[:, None] * inv
    col = (t % GRID_W).astype(F32)[:, None] * inv
    ang = jnp.concatenate([row, col], axis=-1)
    cos, sin = jnp.cos(ang), jnp.sin(ang)
    pre = jnp.ones((s, start), F32)
    post = jnp.zeros((s, period - start - rot_dim), F32)
    c = jnp.concatenate([pre, cos, cos, post], axis=-1)
    sn = jnp.concatenate([0 * pre, sin, sin, post], axis=-1)
    rep = LANES // period
    return jnp.tile(c, (1, rep)), jnp.tile(sn, (1, rep))


def _row(v, width=None):
    v = v.reshape(1, -1).astype(F32)
    return v if width is None else jnp.pad(v, ((0, 0), (0, width - v.shape[1])))


def kernel(x_prompt, x_sample, cache_mla_ckv, cache_mla_krope, cache_na_k, cache_na_v, cache_gqa_k, cache_gqa_v, cache_swa_k, cache_swa_v, c, c_ctx, norm_g, w_mod, b_mod, w_in_even, mla_qa_g, w_q_up, mla_kva_g, w_kv_up, mla_q_g, mla_k_g, na_q_g, na_k_g, na_rpb, w_out_even, w_in_odd, gqa_q_g, gqa_k_g, swa_q_g, swa_k_g, swa_sink, w_out_odd):
    n_dec = x_sample.shape[0]
    assert w_mod.shape[0] == 2 and n_dec + 1 <= 8

    cond_t = jnp.concatenate([c_ctx[:, None], c.T, jnp.zeros((D_MODEL, 7 - n_dec), F32)], axis=1)
    m_all = _modulation(cond_t, 1 + n_dec, w_mod, b_mod)

    win_e, win_o, wout_e, wout_o, wq, wkk, wkv = _weight_prep(jnp.swapaxes(w_in_even[0], 0, 1), w_in_odd,
                                                              w_out_even, w_out_odd, w_q_up, w_kv_up)
    even = (_row(norm_g[0]), win_e, _row(mla_qa_g[0]), wq, _row(mla_kva_g[0]), wkk, wkv,
            _row(mla_q_g[0], LANES), _row(mla_k_g[0], LANES),
            _row(jnp.tile(na_q_g[0], 2)), _row(jnp.tile(na_k_g[0], 2)))
    odd = (_row(norm_g[1]), win_o, _row(jnp.tile(gqa_q_g[0], 2)), _row(jnp.tile(gqa_k_g[0], 2)),
           _row(jnp.tile(swa_q_g[0], 2)), _row(jnp.tile(swa_k_g[0], 2)))
    sink = swa_sink[0].astype(F32)

    xp1, new_ckv, new_krope, new_na_k, new_na_v = _prompt_even(x_prompt, m_all[0], *even, wout_e)
    xp2, new_gqa_k, new_gqa_v, new_swa_k, new_swa_v = _prompt_odd(sink, xp1, m_all[1], *odd, wout_o)

    cos_e, sin_e = _rope_tables(DEC_SEQ, ROPE_A, LANES, NOPE_A)
    qa, ka, va, qbs, kbs, vbs, g0 = _sample_even_proj(x_sample, m_all[0], *even, cos_e, sin_e)
    ckr = jnp.swapaxes(cache_mla_krope[:, 0], -1, -2)
    xs1 = _sample_even_attn(na_rpb[0].reshape(-1), x_sample, m_all[0], qa, ka, va, qbs, kbs, vbs, g0,
                            cache_mla_ckv[:, 0], ckr, _feature_major(cache_na_k[:, 0]),
                            _feature_major(cache_na_v[:, 0]), wkk, wkv, even[8], wout_e)
    cos_o, sin_o = _rope_tables(DEC_SEQ, HEAD_DIM, HEAD_DIM, 0)
    qc, kc, vc, qd, kd, vd, g1 = _sample_odd_proj(xs1, m_all[1], *odd, cos_o, sin_o)
    xs2 = _sample_odd_attn(sink, xs1, m_all[1], qc, kc, vc, qd, kd, vd, g1,
                           _feature_major(cache_gqa_k[:, 0]), _feature_major(cache_gqa_v[:, 0]),
                           _feature_major(cache_swa_k[:, 0]), _feature_major(cache_swa_v[:, 0]), wout_o)

    caches = (new_krope, new_na_k, new_na_v, new_gqa_k, new_gqa_v, new_swa_k, new_swa_v)
    return (xp2, xs2, new_ckv) + tuple(_token_major(c) for c in caches)
```

```python
import functools
from typing import NamedTuple

import jax
import jax.numpy as jnp
import numpy as np
from jax import lax
from jax.experimental import pallas as pl
from jax.experimental.pallas import tpu as pltpu

F32 = jnp.float32
BF16 = jnp.bfloat16

D_MODEL = 1024
SEQ = 256
DEC_SEQ = 1024
PAST_LEN = 256
GRID_W = 64
HEAD_DIM = 64
Q_RANK = 256
KV_RANK = 128
NOPE_A = 64
ROPE_A = 32
QK_A = NOPE_A + ROPE_A
N_HEADS = 8
NA_ROWS = 8
NA_COLS = 16
SWA_HALF = 128
ROPE_THETA = 10000.0
EPS = 1e-6
NEG_INF = -1e30
LOG2E = 1.4426950408889634

LANES = 128
Q_BLOCK = 256
PROMPT_BATCHES_PER_STEP = 2
N_PAIRS = N_HEADS // 2
RPB_ROWS = 2 * NA_ROWS - 1
RPB_COLS = 2 * NA_COLS - 1
BIAS_CHUNKS = 16
VMEM_LIMIT = 48 * 1024 * 1024

E_QLAT, E_CKV, E_KROPE, E_GA, E_QB, E_KB, E_VB, E_GB, E_END = 0, 256, 384, 416, 928, 1440, 1952, 2464, 2976
O_QC, O_KC, O_VC, O_GC, O_QD, O_KD, O_VD, O_GD, O_END = 0, 512, 640, 768, 1280, 1792, 1920, 2048, 2560


def _dot(a, b):
    return lax.dot_general(a, b, (((1,), (0,)), ((), ())), preferred_element_type=F32)


def _dot_nt(a, b):
    return lax.dot_general(a, b, (((1,), (1,)), ((), ())), preferred_element_type=F32)


def _silu(x):
    return x / (1.0 + jnp.exp(-x))


def _rms(x, g, n):
    ss = jnp.sum(x * x, axis=-1, keepdims=True)
    return x * lax.rsqrt(ss / n + EPS) * g


def _rms_halves(x, g2, lo):
    x2 = x * x
    s_lo = jnp.sum(jnp.where(lo, x2, 0.0), axis=-1, keepdims=True)
    s_hi = jnp.sum(jnp.where(lo, 0.0, x2), axis=-1, keepdims=True)
    r = jnp.where(lo, lax.rsqrt(s_lo / HEAD_DIM + EPS), lax.rsqrt(s_hi / HEAD_DIM + EPS))
    return x * r * g2


def _modulate(x, g, m):
    d = x.shape[-1]
    xn = x * lax.rsqrt(jnp.mean(x * x, axis=-1, keepdims=True) + EPS) * g
    return xn * (1.0 + m[:, d:2 * d]) + m[:, :d], m[:, 2 * d:]


def _split_lanes(x):
    hi = x.astype(BF16)
    lo = (x - hi.astype(F32)).astype(BF16)
    return jnp.concatenate([hi, lo], axis=1)


def _lane_matrix2(entries):
    i = lax.broadcasted_iota(jnp.int32, (LANES, LANES), 0)
    j = lax.broadcasted_iota(jnp.int32, (LANES, LANES), 1)
    m = entries(i, j).astype(BF16)
    return jnp.concatenate([m, m], axis=0)


def _group_ones2(width):
    shift = width.bit_length() - 1
    return _lane_matrix2(lambda i, j: jnp.where((i >> shift) == (j >> shift), 1.0, 0.0))


def _rope_matrix2(rot_dim, period, start):
    half = rot_dim // 2

    def entries(i, j):
        pos = jnp.bitwise_and(j, period - 1) - start
        neg = (pos >= 0) & (pos < half) & (i == j + half)
        plus = (pos >= half) & (pos < rot_dim) & (i == j - half)
        return jnp.where(neg, -1.0, jnp.where(plus, 1.0, 0.0))

    return _lane_matrix2(entries)


def _swap_matrix2():
    return _lane_matrix2(lambda i, j: jnp.where(i == jnp.bitwise_xor(j, HEAD_DIM), 1.0, 0.0))


def _lane_mix(x, m2):
    return _dot(_split_lanes(x), m2)


def _rms_mxu(x, g, n, ones2):
    x2 = x * x
    nt = x.shape[-1] // LANES
    sq = x2[:, :LANES]
    for t in range(1, nt):
        sq = sq + x2[:, t * LANES:(t + 1) * LANES]
    r = lax.rsqrt(_lane_mix(sq, ones2) / n + EPS)
    return x * (r if nt == 1 else jnp.tile(r, (1, nt))) * g


def _with_ones(v, transposed=False):
    if transposed:
        return jnp.concatenate([v, jnp.ones((LANES, v.shape[1]), v.dtype)], axis=0)
    return jnp.concatenate([v, jnp.ones((v.shape[0], LANES), v.dtype)], axis=1)


def _attend(parts, sink=None):
    mx = None
    for s, _, _ in parts:
        pm = jnp.max(s, axis=-1, keepdims=True)
        mx = pm if mx is None else jnp.maximum(mx, pm)
    if sink is not None:
        mx = jnp.maximum(mx, sink)
    acc = None
    for s, v, v_t in parts:
        po = (_dot_nt if v_t else _dot)(jnp.exp2(s - mx).astype(BF16), v)
        acc = po if acc is None else acc + po
    den = acc[:, LANES:]
    if sink is not None:
        den = den + jnp.exp2(sink - mx)
    return acc[:, :LANES] * (1.0 / den)


def _lane_lo():
    return lax.broadcasted_iota(jnp.int32, (1, LANES), 1) < HEAD_DIM


def _store_pair_transposed(ref, bi, p, x):
    xt = x.T
    ref[bi, 0, 2 * p] = xt[:HEAD_DIM]
    ref[bi, 0, 2 * p + 1] = xt[HEAD_DIM:]


def _rope_key_slab(win_ref):
    d = win_ref.shape[1]
    return jnp.concatenate([jnp.zeros((NOPE_A, d), BF16), win_ref[E_KROPE:E_GA, :],
                            jnp.zeros((LANES - QK_A, d), BF16)], axis=0)


def _swap_halves(a):
    return jnp.concatenate([a[HEAD_DIM:], a[:HEAD_DIM]], axis=0)


def _mod_kernel(n_cond, c_ref, w_ref, b_ref, o_ref):
    @pl.when(pl.program_id(1) == 0)
    def _():
        bias = b_ref[pl.ds(pl.program_id(0), 1), :]
        o_ref[0, :n_cond, :] = jnp.broadcast_to(bias, (n_cond, o_ref.shape[2]))
        o_ref[0, n_cond:, :] = jnp.zeros((o_ref.shape[1] - n_cond, o_ref.shape[2]), F32)

    s = _silu(c_ref[...])
    w = w_ref[0]
    for r in range(n_cond):
        o_ref[0, r:r + 1, :] += jnp.sum(w * s[:, r:r + 1], axis=0, keepdims=True)


def _modulation(cond_t, n_cond, w_mod, b_mod):
    depth = w_mod.shape[0]
    tk = 256
    return pl.pallas_call(
        functools.partial(_mod_kernel, n_cond),
        grid=(depth, D_MODEL // tk),
        in_specs=[pl.BlockSpec((tk, 8), lambda l, k: (k, 0)),
                  pl.BlockSpec((1, tk, 3 * D_MODEL), lambda l, k: (l, k, 0)),
                  pl.BlockSpec(b_mod.shape, lambda l, k: (0, 0))],
        out_specs=pl.BlockSpec((1, 8, 3 * D_MODEL), lambda l, k: (l, 0, 0)),
        out_shape=jax.ShapeDtypeStruct((depth, 8, 3 * D_MODEL), F32),
        compiler_params=pltpu.CompilerParams(dimension_semantics=("arbitrary", "arbitrary")),
        name="modulation",
    )(cond_t, w_mod, b_mod)


def _mla_keys(cb, kr, wkk_ref, wkv_ref, kg, rope=None):
    kk = _dot(cb, wkk_ref[...])
    keys = []
    for h in range(N_HEADS):
        k = _rms(kk[:, h * LANES:(h + 1) * LANES] + kr, kg, QK_A)
        if rope is not None:
            k = rope(k)
        keys.append(k.astype(BF16))
    return keys, _dot(cb, wkv_ref[...]).astype(BF16)


def _p0_kernel(x_ref, m_ref, ng_ref, win_ref, qag_ref, wq_ref, kvag_ref, wkk_ref, wkv_ref, qg_ref, kg_ref,
               naqg_ref, nakg_ref, wout_ref,
               xo_ref, ckv_ref, krope_ref, nak_ref, nav_ref, y_scr):
    nbs = x_ref.shape[0]
    x = x_ref[...].reshape(nbs * SEQ, D_MODEL)
    h, gate = _modulate(x, ng_ref[...], m_ref[0:1, :])
    hb = h.astype(BF16)
    lo = _lane_lo()
    hi = jnp.logical_not(lo)
    rows = [slice(bi * SEQ, (bi + 1) * SEQ) for bi in range(nbs)]

    qln = _rms(_dot_nt(hb, win_ref[E_QLAT:E_CKV, :]), qag_ref[...], Q_RANK).astype(BF16)
    q_all = _dot(qln, wq_ref[...])
    ckv_n = _rms(_dot_nt(hb, win_ref[E_CKV:E_KROPE, :]), kvag_ref[...], KV_RANK)
    kr = _dot_nt(hb, _rope_key_slab(win_ref))
    for bi, rs in enumerate(rows):
        ckv_ref[bi, 0] = ckv_n[rs]
        krope_ref[bi, 0] = kr[rs].T[NOPE_A:QK_A]
    keys, vals = _mla_keys(ckv_n.astype(BF16), kr, wkk_ref, wkv_ref, kg_ref[...])
    qg = qg_ref[...] * (QK_A ** -0.5 * LOG2E)

    ga = _dot_nt(hb, win_ref[E_GA:E_QB, :])
    zq = _dot_nt(hb, win_ref[E_QB:E_KB, :])
    zk = _dot_nt(hb, win_ref[E_KB:E_VB, :])
    zv = _dot_nt(hb, win_ref[E_VB:E_GB, :])
    gb = _dot_nt(hb, win_ref[E_GB:E_END, :])
    naqg = naqg_ref[...] * (HEAD_DIM ** -0.5 * LOG2E)

    for p in range(N_PAIRS):
        sl = slice(p * LANES, (p + 1) * LANES)
        ys = slice(4 * LANES + p * LANES, 4 * LANES + (p + 1) * LANES)
        qhs = [_rms(q_all[:, hh * LANES:(hh + 1) * LANES], qg, QK_A).astype(BF16) for hh in (2 * p, 2 * p + 1)]
        qb = _rms_halves(zq[:, sl], naqg, lo)
        kb = _rms_halves(zk[:, sl], nakg_ref[...], lo)
        vb = zv[:, sl]
        kbb, vbb = kb.astype(BF16), _with_ones(vb.astype(BF16))
        va = _with_ones(vals[:, sl])
        qms = [jnp.where(keep, qb, 0.0).astype(BF16) for keep in (lo, hi)]
        for bi, rs in enumerate(rows):
            o2 = [_attend([(_dot_nt(qhs[i][rs], keys[2 * p + i][rs]), va[rs], False)]) for i in (0, 1)]
            y_scr[rs, sl] = (jnp.where(lo, o2[0], o2[1]) * _silu(ga[rs, sl])).astype(BF16)
            _store_pair_transposed(nak_ref, bi, p, kb[rs])
            _store_pair_transposed(nav_ref, bi, p, vb[rs])
            o2 = [_attend([(_dot_nt(qms[i][rs], kbb[rs]), vbb[rs], False)]) for i in (0, 1)]
            y_scr[rs, ys] = (jnp.where(lo, o2[0], o2[1]) * _silu(gb[rs, sl])).astype(BF16)

    xo_ref[...] = (x + gate * _dot(y_scr[...], wout_ref[...])).reshape(nbs, SEQ, D_MODEL)


def _full(shape):
    n = len(shape)
    return pl.BlockSpec(shape, lambda *_: (0,) * n)


class _Row(NamedTuple):
    table: jax.Array
    row: int


def _spec(a):
    if isinstance(a, _Row):
        idx = (a.row,) + (0,) * (a.table.ndim - 1)
        return pl.BlockSpec((None,) + a.table.shape[1:], lambda *_: idx)
    return _full(a.shape)


def _arr(a):
    return a.table if isinstance(a, _Row) else a


def _prompt_even(x, m, ng, win, qag, wq, kvag, wkk, wkv, qg, kg, naqg, nakg, wout):
    nb = x.shape[0]
    nbs = PROMPT_BATCHES_PER_STEP
    assert nb % nbs == 0
    ins = (m, ng, win, qag, wq, kvag, wkk, wkv, qg, kg, naqg, nakg, wout)
    return pl.pallas_call(
        _p0_kernel,
        grid=(nb // nbs,),
        in_specs=[pl.BlockSpec((nbs, SEQ, D_MODEL), lambda b: (b, 0, 0))] + [_spec(a) for a in ins],
        out_specs=[pl.BlockSpec((nbs, SEQ, D_MODEL), lambda b: (b, 0, 0)),
                   pl.BlockSpec((nbs, 1, SEQ, KV_RANK), lambda b: (b, 0, 0, 0)),
                   pl.BlockSpec((nbs, 1, ROPE_A, SEQ), lambda b: (b, 0, 0, 0)),
                   pl.BlockSpec((nbs, 1, N_HEADS, HEAD_DIM, SEQ), lambda b: (b, 0, 0, 0, 0)),
                   pl.BlockSpec((nbs, 1, N_HEADS, HEAD_DIM, SEQ), lambda b: (b, 0, 0, 0, 0))],
        out_shape=[jax.ShapeDtypeStruct((nb, SEQ, D_MODEL), F32),
                   jax.ShapeDtypeStruct((nb, 1, SEQ, KV_RANK), F32),
                   jax.ShapeDtypeStruct((nb, 1, ROPE_A, SEQ), F32),
                   jax.ShapeDtypeStruct((nb, 1, N_HEADS, HEAD_DIM, SEQ), F32),
                   jax.ShapeDtypeStruct((nb, 1, N_HEADS, HEAD_DIM, SEQ), F32)],
        scratch_shapes=[pltpu.VMEM((nbs * SEQ, D_MODEL), BF16)],
        compiler_params=pltpu.CompilerParams(dimension_semantics=("arbitrary",), vmem_limit_bytes=VMEM_LIMIT),
        name="prompt_even",
    )(x, *map(_arr, ins))


def _gqa_pair_operands(k, v, kg2, lo):
    kn = _rms_halves(k, kg2, lo)
    return kn, (kn.astype(BF16), pltpu.roll(kn, HEAD_DIM, 1).astype(BF16)), \
        (_with_ones(v.astype(BF16)), _with_ones(pltpu.roll(v, HEAD_DIM, 1).astype(BF16)))


def _p1_kernel(sink_ref, x_ref, m_ref, ng_ref, win_ref, gqg_ref, gkg_ref, sqg_ref, skg_ref, wout_ref,
               xo_ref, gk_ref, gv_ref, sk_ref, sv_ref, y_scr):
    nbs = x_ref.shape[0]
    x = x_ref[...].reshape(nbs * SEQ, D_MODEL)
    h, gate = _modulate(x, ng_ref[...], m_ref[0:1, :])
    hb = h.astype(BF16)
    lo = _lane_lo()
    hi = jnp.logical_not(lo)
    sc = HEAD_DIM ** -0.5 * LOG2E
    rows = [slice(bi * SEQ, (bi + 1) * SEQ) for bi in range(nbs)]

    branches = ((O_QC, O_KC, O_VC, O_GC, gqg_ref, gkg_ref, gk_ref, gv_ref, False, 0),
                (O_QD, O_KD, O_VD, O_GD, sqg_ref, skg_ref, sk_ref, sv_ref, True, 4 * LANES))
    for oq, ok, ov, og, qg_ref, kg_ref, ck_ref, cv_ref, has_sink, yoff in branches:
        zq = _dot(hb, win_ref[:, oq:oq + 4 * LANES])
        zkv = _dot(hb, win_ref[:, ok:ok + 2 * LANES])
        zg = _dot(hb, win_ref[:, og:og + 4 * LANES])
        v = zkv[:, LANES:]
        kn, ks, vs = _gqa_pair_operands(zkv[:, :LANES], v, kg_ref[...], lo)
        for bi, rs in enumerate(rows):
            _store_pair_transposed(ck_ref, bi, 0, kn[rs])
            _store_pair_transposed(cv_ref, bi, 0, v[rs])
        qg = qg_ref[...] * sc
        for p in range(N_PAIRS):
            sl = slice(p * LANES, (p + 1) * LANES)
            qn = _rms_halves(zq[:, sl], qg, lo)
            qms = [jnp.where(keep, qn, 0.0).astype(BF16) for keep in (lo, hi)]
            kv = p // 2
            for bi, rs in enumerate(rows):
                o2 = []
                for half in (0, 1):
                    swap = 0 if kv == half else 1
                    sink = sink_ref[2 * p + half] * LOG2E if has_sink else None
                    o2.append(_attend([(_dot_nt(qms[half][rs], ks[swap][rs]), vs[swap][rs], False)], sink))
                o = jnp.where(lo, o2[0], o2[1])
                y_scr[rs, yoff + p * LANES:yoff + (p + 1) * LANES] = (o * _silu(zg[rs, sl])).astype(BF16)

    xo_ref[...] = (x + gate * _dot(y_scr[...], wout_ref[...])).reshape(nbs, SEQ, D_MODEL)


def _prompt_odd(sink, x, m, ng, win, gqg, gkg, sqg, skg, wout):
    nb = x.shape[0]
    nbs = PROMPT_BATCHES_PER_STEP
    assert nb % nbs == 0
    ins = (m, ng, win, gqg, gkg, sqg, skg, wout)
    cache_spec = pl.BlockSpec((nbs, 1, 2, HEAD_DIM, SEQ), lambda b: (b, 0, 0, 0, 0))
    cache_shape = jax.ShapeDtypeStruct((nb, 1, 2, HEAD_DIM, SEQ), F32)
    return pl.pallas_call(
        _p1_kernel,
        grid=(nb // nbs,),
        in_specs=[pl.BlockSpec(memory_space=pltpu.SMEM),
                  pl.BlockSpec((nbs, SEQ, D_MODEL), lambda b: (b, 0, 0))] + [_spec(a) for a in ins],
        out_specs=[pl.BlockSpec((nbs, SEQ, D_MODEL), lambda b: (b, 0, 0))] + [cache_spec] * 4,
        out_shape=[jax.ShapeDtypeStruct((nb, SEQ, D_MODEL), F32)] + [cache_shape] * 4,
        scratch_shapes=[pltpu.VMEM((nbs * SEQ, D_MODEL), BF16)],
        compiler_params=pltpu.CompilerParams(dimension_semantics=("arbitrary",), vmem_limit_bytes=VMEM_LIMIT),
        name="prompt_odd",
    )(sink, x, *map(_arr, ins))


def _s0a_kernel(x_ref, m_ref, ng_ref, win_ref, qag_ref, wq_ref, kvag_ref, wkk_ref, wkv_ref, qg_ref, kg_ref,
                naqg_ref, nakg_ref, cos_ref, sin_ref,
                qa_ref, ka_ref, va_ref, qb_ref, kb_ref, vb_ref, g_ref):
    b = pl.program_id(0)
    lo = _lane_lo()
    ones_all = _group_ones2(LANES)
    partner = _rope_matrix2(ROPE_A, LANES, NOPE_A)
    hb = _modulate(x_ref[0], ng_ref[...], m_ref[pl.ds(1 + b, 1), :])[0].astype(BF16)
    cos, sin = cos_ref[...], sin_ref[...]

    qln = _rms(_dot_nt(hb, win_ref[E_QLAT:E_CKV, :]), qag_ref[...], Q_RANK).astype(BF16)
    q_all = _dot(qln, wq_ref[...])
    ckv_n = _rms(_dot_nt(hb, win_ref[E_CKV:E_KROPE, :]), kvag_ref[...], KV_RANK)
    kr = _dot_nt(hb, _rope_key_slab(win_ref))
    cb = ckv_n.astype(BF16)
    kk = _dot(cb, wkk_ref[...])
    va_ref[0] = _dot(cb, wkv_ref[...]).astype(BF16)
    zq = _dot_nt(hb, win_ref[E_QB:E_KB, :])
    zk = _dot_nt(hb, win_ref[E_KB:E_VB, :])
    vb_ref[0] = _dot_nt(hb, win_ref[E_VB:E_GB, :]).astype(BF16)
    g_ref[0, :, 0:4 * LANES] = _silu(_dot_nt(hb, win_ref[E_GA:E_QB, :]))
    g_ref[0, :, 4 * LANES:8 * LANES] = _silu(_dot_nt(hb, win_ref[E_GB:E_END, :]))

    qg = qg_ref[...] * (QK_A ** -0.5 * LOG2E)
    kg = kg_ref[...]
    k_partner = _lane_mix(kr * kg, partner) * sin
    for hh in range(N_HEADS):
        sl = slice(hh * LANES, (hh + 1) * LANES)
        qn = _rms_mxu(q_all[:, sl], qg, QK_A, ones_all)
        qa_ref[0, :, sl] = (qn * cos + _lane_mix(qn, partner) * sin).astype(BF16)
        k_raw = kk[:, sl] + kr
        k_inv = lax.rsqrt(_lane_mix(k_raw * k_raw, ones_all) / QK_A + EPS)
        ka_ref[0, :, sl] = ((k_raw * kg * cos + k_partner) * k_inv).astype(BF16)
    naqg = naqg_ref[...] * (HEAD_DIM ** -0.5 * LOG2E)
    for p in range(N_PAIRS):
        sl = slice(p * LANES, (p + 1) * LANES)
        qb_ref[0, :, sl] = _rms_halves(zq[:, sl], naqg, lo).astype(BF16)
        kb_ref[0, :, sl] = _rms_halves(zk[:, sl], nakg_ref[...], lo).astype(BF16)


def _sample_even_proj(x, m, ng, win, qag, wq, kvag, wkk, wkv, qg, kg, naqg, nakg, cos, sin):
    nb, s, _ = x.shape
    nq = s // Q_BLOCK
    ins = (m, ng, win, qag, wq, kvag, wkk, wkv, qg, kg, naqg, nakg)
    tab = pl.BlockSpec((Q_BLOCK, LANES), lambda b, j: (j, 0))

    def blk(w):
        return pl.BlockSpec((1, Q_BLOCK, w), lambda b, j: (b, j, 0))

    def shp(w, dt):
        return jax.ShapeDtypeStruct((nb, s, w), dt)

    return pl.pallas_call(
        _s0a_kernel,
        grid=(nb, nq),
        in_specs=[blk(D_MODEL)] + [_spec(a) for a in ins] + [tab, tab],
        out_specs=[blk(1024), blk(1024), blk(512), blk(512), blk(512), blk(512), blk(1024)],
        out_shape=[shp(1024, BF16), shp(1024, BF16), shp(512, BF16), shp(512, BF16), shp(512, BF16),
                   shp(512, BF16), shp(1024, F32)],
        compiler_params=pltpu.CompilerParams(dimension_semantics=("arbitrary", "arbitrary"),
                                             vmem_limit_bytes=VMEM_LIMIT),
        name="sample_even_proj",
    )(x, *map(_arr, ins), cos, sin)


def _build_bias_table(rpb_ref, tile_scr, tab_ref):
    qc = lax.broadcasted_iota(jnp.int32, (GRID_W, LANES), 0)
    lane = lax.broadcasted_iota(jnp.int32, (GRID_W, LANES), 1)
    kc = jnp.bitwise_and(lane, GRID_W - 1)
    lo = lane < GRID_W
    diff = kc - qc + (NA_COLS - 1)
    cs = jnp.clip(qc - NA_COLS // 2, 0, GRID_W - NA_COLS)
    valid = (kc >= cs) & (kc < cs + NA_COLS)
    tab_ref[...] = jnp.zeros(tab_ref.shape, F32)
    tile_scr[RPB_ROWS] = jnp.zeros((GRID_W, LANES), F32)

    def per_head(h, carry):
        for dr in range(RPB_ROWS):
            t = jnp.zeros((GRID_W, LANES), F32)
            for dc in range(RPB_COLS):
                t = jnp.where(diff == dc, rpb_ref[(h * RPB_ROWS + dr) * RPB_COLS + dc], t)
            tile_scr[dr] = jnp.where(valid, t * LOG2E, NEG_INF)
        for c in range(NA_ROWS // 2, NA_ROWS // 2 + NA_ROWS):
            d0 = 2 * c - NA_ROWS
            tab_ref[0, h, c] = jnp.where(lo, tile_scr[d0], tile_scr[d0 + 1])
            tab_ref[1, h, c] = jnp.where(lo, tile_scr[d0 - 1 if d0 > 0 else RPB_ROWS], tile_scr[d0])
        return carry

    lax.fori_loop(0, N_HEADS, per_head, 0)


def _s0b_kernel(rpb_ref, x_ref, m_ref, qa_ref, ka_ref, va_ref, qb_ref, kb_ref, vb_ref, g_ref,
                cckv_ref, ckr_ref, cnk_ref, cnv_ref, wkk_ref, wkv_ref, kg_ref, wout_ref,
                xo_ref, kca_scr, vca_scr, tile_scr, tab_scr, y_scr):
    b = pl.program_id(0)
    j = pl.program_id(1)
    lo = _lane_lo()
    n_lat = ka_ref.shape[1]

    @pl.when((b == 0) & (j == 0))
    def _():
        _build_bias_table(rpb_ref, tile_scr, tab_scr)

    @pl.when(j == 0)
    def _():
        kr_t = jnp.concatenate([jnp.zeros((NOPE_A, PAST_LEN), F32), ckr_ref[0],
                                jnp.zeros((LANES - QK_A, PAST_LEN), F32)], axis=0)
        keys, vals = _mla_keys(cckv_ref[0].astype(BF16), kr_t.T, wkk_ref, wkv_ref, kg_ref[...])
        for hh in range(N_HEADS):
            kca_scr[:, hh * LANES:(hh + 1) * LANES] = keys[hh]
        vca_scr[...] = vals

    na_keys = 3 * Q_BLOCK
    ks = pl.multiple_of((j // 2) * Q_BLOCK, Q_BLOCK)
    kidx = ks + lax.broadcasted_iota(jnp.int32, (1, na_keys), 1)
    for p in range(N_PAIRS):
        sl = slice(p * LANES, (p + 1) * LANES)
        o2 = []
        va = _with_ones(va_ref[0, :, sl])
        vca = _with_ones(vca_scr[:, sl])
        for hh in (2 * p, 2 * p + 1):
            hs = slice(hh * LANES, (hh + 1) * LANES)
            q = qa_ref[0, :, hs]
            o2.append(_attend([(_dot_nt(q, ka_ref[0, :, hs]), va, False),
                               (_dot_nt(q, kca_scr[:, hs]), vca, False)]))
        oa = jnp.where(lo, o2[0], o2[1])
        y_scr[:, sl] = (oa * g_ref[0, :, sl]).astype(BF16)

        qb = qb_ref[0, :, sl]
        kb = kb_ref[0, pl.ds(ks, na_keys), sl]
        vb = _with_ones(vb_ref[0, pl.ds(ks, na_keys), sl])
        kcb = cnk_ref[0, sl, :].astype(BF16)
        vcb = _with_ones(cnv_ref[0, sl, :].astype(BF16), transposed=True)
        o2 = []
        for half in (0, 1):
            head = 2 * p + half
            qm = jnp.where(lo if half == 0 else jnp.logical_not(lo), qb, jnp.zeros_like(qb))
            s_lat = _dot_nt(qm, kb)
            rows = []
            for local in range(Q_BLOCK // GRID_W):
                qr = j * (Q_BLOCK // GRID_W) + local
                par = 0 if local % 2 == 1 else 1
                c0 = (RPB_ROWS + par - local) // 2 - 2 * j + ks // LANES
                bias = jnp.concatenate([tab_scr[par, head, c0 + t] for t in range(na_keys // LANES)], axis=1)
                r0 = jnp.clip(qr - NA_ROWS // 2, 0, n_lat // GRID_W - NA_ROWS) * GRID_W
                ok = (kidx >= r0) & (kidx < r0 + NA_ROWS * GRID_W)
                rows.append(jnp.where(ok, s_lat[local * GRID_W:(local + 1) * GRID_W] + bias, NEG_INF))
            s_lat = jnp.concatenate(rows, axis=0)
            o2.append(_attend([(s_lat, vb, False), (_dot(qm, kcb), vcb, True)]))
        ob = jnp.where(lo, o2[0], o2[1])
        ys = slice(4 * LANES + p * LANES, 4 * LANES + (p + 1) * LANES)
        y_scr[:, ys] = (ob * g_ref[0, :, ys]).astype(BF16)

    d = x_ref.shape[-1]
    gate = m_ref[pl.ds(1 + b, 1), 2 * d:]
    xo_ref[0] = x_ref[0] + gate * _dot(y_scr[...], wout_ref[...])


def _sample_even_attn(rpb, x, m, qa, ka, va, qb, kb, vb, g, cckv, ckr, cnk, cnv, wkk, wkv, kg, wout):
    nb, s, _ = x.shape
    nq = s // Q_BLOCK

    def blk(w):
        return pl.BlockSpec((1, Q_BLOCK, w), lambda b, j: (b, j, 0))

    def per_batch(a):
        return pl.BlockSpec((1,) + a.shape[1:], lambda b, j: (b, 0, 0))

    return pl.pallas_call(
        _s0b_kernel,
        grid=(nb, nq),
        in_specs=[pl.BlockSpec(memory_space=pltpu.SMEM), blk(D_MODEL), _spec(m),
                  blk(1024), per_batch(ka), per_batch(va), blk(512), per_batch(kb), per_batch(vb), blk(1024),
                  per_batch(cckv), per_batch(ckr), per_batch(cnk), per_batch(cnv),
                  _full(wkk.shape), _full(wkv.shape), _spec(kg), _full(wout.shape)],
        out_specs=blk(D_MODEL),
        out_shape=jax.ShapeDtypeStruct(x.shape, F32),
        scratch_shapes=[pltpu.VMEM((PAST_LEN, N_HEADS * LANES), BF16),
                        pltpu.VMEM((PAST_LEN, N_HEADS * HEAD_DIM), BF16),
                        pltpu.VMEM((RPB_ROWS + 1, GRID_W, LANES), F32),
                        pltpu.VMEM((2, N_HEADS, BIAS_CHUNKS, GRID_W, LANES), F32),
                        pltpu.VMEM((Q_BLOCK, D_MODEL), BF16)],
        compiler_params=pltpu.CompilerParams(dimension_semantics=("arbitrary", "arbitrary"),
                                             vmem_limit_bytes=VMEM_LIMIT),
        name="sample_even_attn",
    )(rpb, x, _arr(m), qa, ka, va, qb, kb, vb, g, cckv, ckr, cnk, cnv, wkk, wkv, _arr(kg), wout)


def _s1a_kernel(x_ref, m_ref, ng_ref, win_ref, gqg_ref, gkg_ref, sqg_ref, skg_ref, cos_ref, sin_ref,
                qc_ref, kc_ref, vc_ref, qd_ref, kd_ref, vd_ref, g_ref):
    b = pl.program_id(0)
    lo = _lane_lo()
    partner = _rope_matrix2(HEAD_DIM, HEAD_DIM, 0)
    swap = _swap_matrix2()[:LANES]
    hb = _modulate(x_ref[0], ng_ref[...], m_ref[pl.ds(1 + b, 1), :])[0].astype(BF16)
    cos, sin = cos_ref[...], sin_ref[...]
    sc = HEAD_DIM ** -0.5 * LOG2E

    def rope(t):
        return t * cos + _lane_mix(t, partner) * sin

    branches = ((O_QC, O_KC, O_GC, gqg_ref, gkg_ref, qc_ref, kc_ref, vc_ref, 0),
                (O_QD, O_KD, O_GD, sqg_ref, skg_ref, qd_ref, kd_ref, vd_ref, 4 * LANES))
    for oq, ok, og, qg_ref, kg_ref, q_out, k_out, v_out, goff in branches:
        zq = _dot(hb, win_ref[:, oq:oq + 4 * LANES])
        zkv = _dot(hb, win_ref[:, ok:ok + 2 * LANES])
        qg = qg_ref[...] * sc
        for p in range(N_PAIRS):
            sl = slice(p * LANES, (p + 1) * LANES)
            q_out[0, :, sl] = rope(_rms_halves(zq[:, sl], qg, lo)).astype(BF16)
        kn = rope(_rms_halves(zkv[:, :LANES], kg_ref[...], lo))
        v = zkv[:, LANES:]
        for out, val in ((k_out, kn.astype(BF16)), (v_out, v.astype(BF16))):
            out[0, :, 0:LANES] = val
            out[0, :, LANES:2 * LANES] = _dot(val, swap).astype(BF16)
        g_ref[0, :, goff:goff + 4 * LANES] = _silu(_dot(hb, win_ref[:, og:og + 4 * LANES]))


def _sample_odd_proj(x, m, ng, win, gqg, gkg, sqg, skg, cos, sin):
    nb, s, _ = x.shape
    nq = s // Q_BLOCK
    ins = (m, ng, win, gqg, gkg, sqg, skg)
    tab = pl.BlockSpec((Q_BLOCK, LANES), lambda b, j: (j, 0))

    def blk(w):
        return pl.BlockSpec((1, Q_BLOCK, w), lambda b, j: (b, j, 0))

    def shp(w, dt):
        return jax.ShapeDtypeStruct((nb, s, w), dt)

    return pl.pallas_call(
        _s1a_kernel,
        grid=(nb, nq),
        in_specs=[blk(D_MODEL)] + [_spec(a) for a in ins] + [tab, tab],
        out_specs=[blk(512), blk(256), blk(256), blk(512), blk(256), blk(256), blk(1024)],
        out_shape=[shp(512, BF16), shp(256, BF16), shp(256, BF16), shp(512, BF16), shp(256, BF16),
                   shp(256, BF16), shp(1024, F32)],
        compiler_params=pltpu.CompilerParams(dimension_semantics=("arbitrary", "arbitrary"),
                                             vmem_limit_bytes=VMEM_LIMIT),
        name="sample_odd_proj",
    )(x, *map(_arr, ins), cos, sin)


def _s1b_kernel(sink_ref, x_ref, m_ref, qc_ref, kc_ref, vc_ref, qd_ref, kd_ref, vd_ref, g_ref,
                cgk_ref, cgv_ref, csk_ref, csv_ref, wout_ref, xo_ref, y_scr):
    b = pl.program_id(0)
    j = pl.program_id(1)
    lo = _lane_lo()
    n_lat = kc_ref.shape[1]
    win_keys = Q_BLOCK + 2 * SWA_HALF

    def ctx_pair(ref, values=False):
        a = ref[0].astype(BF16)
        pair = (a, _swap_halves(a))
        return tuple(_with_ones(t, transposed=True) for t in pair) if values else pair

    cgk, cgv, csk, csv = ctx_pair(cgk_ref), ctx_pair(cgv_ref, True), ctx_pair(csk_ref), ctx_pair(csv_ref, True)
    vcs = [_with_ones(vc_ref[0, :, w * LANES:(w + 1) * LANES]) for w in (0, 1)]

    ks = pl.multiple_of(jnp.clip(j * Q_BLOCK - SWA_HALF, 0, n_lat - win_keys), SWA_HALF)
    qpos = j * Q_BLOCK + lax.broadcasted_iota(jnp.int32, (Q_BLOCK, win_keys), 0)
    kpos = ks + lax.broadcasted_iota(jnp.int32, (Q_BLOCK, win_keys), 1)
    band = jnp.abs(qpos - kpos) <= SWA_HALF
    vds = [_with_ones(vd_ref[0, pl.ds(ks, win_keys), w * LANES:(w + 1) * LANES]) for w in (0, 1)]

    for p in range(N_PAIRS):
        sl = slice(p * LANES, (p + 1) * LANES)
        kv = p // 2
        qc = qc_ref[0, :, sl]
        qd = qd_ref[0, :, sl]
        oc2, od2 = [], []
        for half in (0, 1):
            swap = 0 if kv == half else 1
            ws = slice(swap * LANES, (swap + 1) * LANES)
            keep = lo if half == 0 else jnp.logical_not(lo)
            qm = jnp.where(keep, qc, jnp.zeros_like(qc))
            oc2.append(_attend([(_dot_nt(qm, kc_ref[0, :, ws]), vcs[swap], False),
                                (_dot(qm, cgk[swap]), cgv[swap], True)]))
            qm = jnp.where(keep, qd, jnp.zeros_like(qd))
            s_loc = jnp.where(band, _dot_nt(qm, kd_ref[0, pl.ds(ks, win_keys), ws]), NEG_INF)
            od2.append(_attend([(s_loc, vds[swap], False),
                                (_dot(qm, csk[swap]), csv[swap], True)], sink_ref[2 * p + half] * LOG2E))
        y_scr[:, sl] = (jnp.where(lo, oc2[0], oc2[1]) * g_ref[0, :, sl]).astype(BF16)
        ys = slice(4 * LANES + p * LANES, 4 * LANES + (p + 1) * LANES)
        y_scr[:, ys] = (jnp.where(lo, od2[0], od2[1]) * g_ref[0, :, ys]).astype(BF16)

    d = x_ref.shape[-1]
    gate = m_ref[pl.ds(1 + b, 1), 2 * d:]
    xo_ref[0] = x_ref[0] + gate * _dot(y_scr[...], wout_ref[...])


def _sample_odd_attn(sink, x, m, qc, kc, vc, qd, kd, vd, g, cgk, cgv, csk, csv, wout):
    nb, s, _ = x.shape
    nq = s // Q_BLOCK

    def blk(w):
        return pl.BlockSpec((1, Q_BLOCK, w), lambda b, j: (b, j, 0))

    def per_batch(a):
        return pl.BlockSpec((1,) + a.shape[1:], lambda b, j: (b, 0, 0))

    return pl.pallas_call(
        _s1b_kernel,
        grid=(nb, nq),
        in_specs=[pl.BlockSpec(memory_space=pltpu.SMEM), blk(D_MODEL), _spec(m),
                  blk(512), per_batch(kc), per_batch(vc), blk(512), per_batch(kd), per_batch(vd), blk(1024),
                  per_batch(cgk), per_batch(cgv), per_batch(csk), per_batch(csv), _full(wout.shape)],
        out_specs=blk(D_MODEL),
        out_shape=jax.ShapeDtypeStruct(x.shape, F32),
        scratch_shapes=[pltpu.VMEM((Q_BLOCK, D_MODEL), BF16)],
        compiler_params=pltpu.CompilerParams(dimension_semantics=("arbitrary", "arbitrary"),
                                             vmem_limit_bytes=VMEM_LIMIT),
        name="sample_odd_attn",
    )(sink, x, _arr(m), qc, kc, vc, qd, kd, vd, g, cgk, cgv, csk, csv, wout)


WEIGHT_PREP_STEPS = 8


G_MLA_Q, G_MLA_K, G_NA_Q, G_NA_K, G_GQA_Q, G_GQA_K, G_SWA_Q, G_SWA_K, N_GAINS = range(9)


def _weight_prep_kernel(wie_ref, wio_ref, woe_ref, woo_ref, wqu_ref, wkv_ref, ng_ref, *refs):
    gain_refs = refs[:N_GAINS]
    win_e_ref, win_o_ref, wout_e_ref, wout_o_ref, wq_ref, wkk_ref, wkvv_ref, gt_ref, ngt_ref = refs[N_GAINS:]
    gt_ref[...] = jnp.zeros(gt_ref.shape, F32)
    for r, g_ref in enumerate(gain_refs):
        w = g_ref.shape[1]
        for off in range(0, LANES - w + 1, w):
            gt_ref[r, :, off:off + w] = g_ref[...]
    for layer in range(ngt_ref.shape[0]):
        ngt_ref[layer] = ng_ref[layer:layer + 1, :]

    win_e_ref[...] = wie_ref[...].astype(BF16)
    win_o_ref[...] = wio_ref[0].astype(BF16)
    wout_e_ref[...] = woe_ref[0].astype(BF16)
    wout_o_ref[...] = woo_ref[0].astype(BF16)

    wq_ref[...] = jnp.zeros(wq_ref.shape, BF16)
    for h in range(N_HEADS):
        wq_ref[:, h * LANES:h * LANES + QK_A] = wqu_ref[0, :, h * QK_A:(h + 1) * QK_A].astype(BF16)
    lo = _lane_lo()
    for p in range(N_PAIRS):
        a = wkv_ref[0, :, (2 * p) * LANES:(2 * p + 1) * LANES]
        c = wkv_ref[0, :, (2 * p + 1) * LANES:(2 * p + 2) * LANES]
        wkk_ref[:, (2 * p) * LANES:(2 * p + 1) * LANES] = jnp.where(lo, a, 0.0).astype(BF16)
        wkk_ref[:, (2 * p + 1) * LANES:(2 * p + 2) * LANES] = jnp.where(lo, c, 0.0).astype(BF16)
        wkvv_ref[:, p * LANES:(p + 1) * LANES] = jnp.where(lo, pltpu.roll(a, HEAD_DIM, 1), c).astype(BF16)


def _weight_prep(w_in_even_t, w_in_odd, w_out_even, w_out_odd, w_q_up, w_kv_up, norm_g, gains):
    n = WEIGHT_PREP_STEPS
    assert len(gains) == N_GAINS
    small = (norm_g,) + tuple(gains)
    ins = (w_in_odd, w_out_even, w_out_odd, w_q_up, w_kv_up)
    out_cols = (O_END, D_MODEL, D_MODEL, N_HEADS * LANES, N_HEADS * LANES, N_HEADS * HEAD_DIM)
    out_rows = (D_MODEL, D_MODEL, D_MODEL, Q_RANK, KV_RANK, KV_RANK)
    te = E_END // 6
    assert te * 6 == E_END and te % 16 == 0
    even_spec = pl.BlockSpec((te, D_MODEL), lambda i: (jnp.minimum(i, 5), 0))
    return pl.pallas_call(
        _weight_prep_kernel,
        grid=(n,),
        in_specs=[even_spec] + [pl.BlockSpec((1, a.shape[1] // n, a.shape[2]), lambda i: (0, i, 0)) for a in ins]
        + [_full(a.shape) for a in small],
        out_specs=[even_spec] + [pl.BlockSpec((r // n, c), lambda i: (i, 0)) for r, c in zip(out_rows, out_cols)]
        + [_full((N_GAINS, 1, LANES)), _full((norm_g.shape[0], 1, D_MODEL))],
        out_shape=[jax.ShapeDtypeStruct((E_END, D_MODEL), BF16)]
        + [jax.ShapeDtypeStruct((r, c), BF16) for r, c in zip(out_rows, out_cols)]
        + [jax.ShapeDtypeStruct((N_GAINS, 1, LANES), F32), jax.ShapeDtypeStruct((norm_g.shape[0], 1, D_MODEL), F32)],
        compiler_params=pltpu.CompilerParams(dimension_semantics=("arbitrary",), vmem_limit_bytes=VMEM_LIMIT),
        name="weight_prep",
    )(w_in_even_t, *ins, *small)


def _feature_major(c):
    b, h, l, d = c.shape
    return jnp.swapaxes(c, -1, -2).reshape(b, h * d, l)


def _token_major(c):
    return jnp.swapaxes(c, -1, -2)


def _rope_tables(s, rot_dim, period, start):
    quarter = rot_dim // 4
    t = np.arange(s)
    inv = ROPE_THETA ** (-np.arange(quarter, dtype=np.float64) / quarter)
    row = (t // GRID_W).astype(np.float64)[:, None] * inv
    col = (t % GRID_W).astype(np.float64)[:, None] * inv
    ang = np.concatenate([row, col], axis=-1)
    cos, sin = np.cos(ang), np.sin(ang)
    pre = np.ones((s, start))
    post = np.zeros((s, period - start - rot_dim))
    c = np.concatenate([pre, cos, cos, post], axis=-1)
    sn = np.concatenate([0 * pre, sin, sin, post], axis=-1)
    rep = LANES // period
    return jnp.asarray(np.tile(c, (1, rep)), F32), jnp.asarray(np.tile(sn, (1, rep)), F32)


def kernel(x_prompt, x_sample, cache_mla_ckv, cache_mla_krope, cache_na_k, cache_na_v, cache_gqa_k, cache_gqa_v, cache_swa_k, cache_swa_v, c, c_ctx, norm_g, w_mod, b_mod, w_in_even, mla_qa_g, w_q_up, mla_kva_g, w_kv_up, mla_q_g, mla_k_g, na_q_g, na_k_g, na_rpb, w_out_even, w_in_odd, gqa_q_g, gqa_k_g, swa_q_g, swa_k_g, swa_sink, w_out_odd):
    n_dec = x_sample.shape[0]
    assert w_mod.shape[0] == 2 and n_dec + 1 <= 8

    cond_t = jnp.concatenate([c_ctx[:, None], c.T, jnp.zeros((D_MODEL, 7 - n_dec), F32)], axis=1)
    m_all = _modulation(cond_t, 1 + n_dec, w_mod, b_mod)

    gains = (mla_q_g, mla_k_g, na_q_g, na_k_g, gqa_q_g, gqa_k_g, swa_q_g, swa_k_g)
    win_e, win_o, wout_e, wout_o, wq, wkk, wkv, gt, ngt = _weight_prep(
        jnp.swapaxes(w_in_even[0], 0, 1), w_in_odd, w_out_even, w_out_odd, w_q_up, w_kv_up, norm_g, gains)
    m_even, m_odd = _Row(m_all, 0), _Row(m_all, 1)
    even = (_Row(ngt, 0), win_e, mla_qa_g, wq, mla_kva_g, wkk, wkv,
            _Row(gt, G_MLA_Q), _Row(gt, G_MLA_K), _Row(gt, G_NA_Q), _Row(gt, G_NA_K))
    odd = (_Row(ngt, 1), win_o, _Row(gt, G_GQA_Q), _Row(gt, G_GQA_K), _Row(gt, G_SWA_Q), _Row(gt, G_SWA_K))
    sink = swa_sink[0].astype(F32)

    xp1, new_ckv, new_krope, new_na_k, new_na_v = _prompt_even(x_prompt, m_even, *even, wout_e)
    xp2, new_gqa_k, new_gqa_v, new_swa_k, new_swa_v = _prompt_odd(sink, xp1, m_odd, *odd, wout_o)

    cos_e, sin_e = _rope_tables(DEC_SEQ, ROPE_A, LANES, NOPE_A)
    qa, ka, va, qbs, kbs, vbs, g0 = _sample_even_proj(x_sample, m_even, *even, cos_e, sin_e)
    ckr = jnp.swapaxes(cache_mla_krope[:, 0], -1, -2)
    xs1 = _sample_even_attn(na_rpb[0].reshape(-1), x_sample, m_even, qa, ka, va, qbs, kbs, vbs, g0,
                            cache_mla_ckv[:, 0], ckr, _feature_major(cache_na_k[:, 0]),
                            _feature_major(cache_na_v[:, 0]), wkk, wkv, _Row(gt, G_MLA_K), wout_e)
    cos_o, sin_o = _rope_tables(DEC_SEQ, HEAD_DIM, HEAD_DIM, 0)
    qc, kc, vc, qd, kd, vd, g1 = _sample_odd_proj(xs1, m_odd, *odd, cos_o, sin_o)
    xs2 = _sample_odd_attn(sink, xs1, m_odd, qc, kc, vc, qd, kd, vd, g1,
                           _feature_major(cache_gqa_k[:, 0]), _feature_major(cache_gqa_v[:, 0]),
                           _feature_major(cache_swa_k[:, 0]), _feature_major(cache_swa_v[:, 0]), wout_o)

    caches = (new_krope, new_na_k, new_na_v, new_gqa_k, new_gqa_v, new_swa_k, new_swa_v)
    return (xp2, xs2, new_ckv) + tuple(_token_major(c) for c in caches)
```

```python
import functools
from typing import NamedTuple

import jax
import jax.numpy as jnp
import numpy as np
from jax import lax
from jax.experimental import pallas as pl
from jax.experimental.pallas import tpu as pltpu

F32 = jnp.float32
BF16 = jnp.bfloat16

D_MODEL = 1024
SEQ = 256
DEC_SEQ = 1024
PAST_LEN = 256
GRID_W = 64
HEAD_DIM = 64
Q_RANK = 256
KV_RANK = 128
NOPE_A = 64
ROPE_A = 32
QK_A = NOPE_A + ROPE_A
N_HEADS = 8
NA_ROWS = 8
NA_COLS = 16
SWA_HALF = 128
ROPE_THETA = 10000.0
EPS = 1e-6
NEG_INF = -1e30
LOG2E = 1.4426950408889634

LANES = 128
Q_BLOCK = 256
PROJ_BLOCK = 512
PROMPT_BATCHES_PER_STEP = 2
N_PAIRS = N_HEADS // 2
RPB_ROWS = 2 * NA_ROWS - 1
RPB_COLS = 2 * NA_COLS - 1
BIAS_CHUNKS = 16
VMEM_LIMIT = 48 * 1024 * 1024

E_QLAT, E_CKV, E_KROPE, E_GA, E_QB, E_KB, E_VB, E_GB, E_END = 0, 256, 384, 416, 928, 1440, 1952, 2464, 2976
O_QC, O_KC, O_VC, O_GC, O_QD, O_KD, O_VD, O_GD, O_END = 0, 512, 640, 768, 1280, 1792, 1920, 2048, 2560


def _dot(a, b):
    return lax.dot_general(a, b, (((1,), (0,)), ((), ())), preferred_element_type=F32)


def _dot_nt(a, b):
    return lax.dot_general(a, b, (((1,), (1,)), ((), ())), preferred_element_type=F32)


def _silu(x):
    return x / (1.0 + jnp.exp(-x))


def _rms(x, g, n):
    ss = jnp.sum(x * x, axis=-1, keepdims=True)
    return x * lax.rsqrt(ss / n + EPS) * g


def _rms_halves(x, g2, lo):
    x2 = x * x
    s_lo = jnp.sum(jnp.where(lo, x2, 0.0), axis=-1, keepdims=True)
    s_hi = jnp.sum(jnp.where(lo, 0.0, x2), axis=-1, keepdims=True)
    r = jnp.where(lo, lax.rsqrt(s_lo / HEAD_DIM + EPS), lax.rsqrt(s_hi / HEAD_DIM + EPS))
    return x * r * g2


def _modulate(x, g, m):
    d = x.shape[-1]
    xn = x * lax.rsqrt(jnp.mean(x * x, axis=-1, keepdims=True) + EPS) * g
    return xn * (1.0 + m[:, d:2 * d]) + m[:, :d], m[:, 2 * d:]


def _split_lanes(x):
    hi = x.astype(BF16)
    lo = (x - hi.astype(F32)).astype(BF16)
    return jnp.concatenate([hi, lo], axis=1)


def _lane_matrix2(entries):
    i = lax.broadcasted_iota(jnp.int32, (LANES, LANES), 0)
    j = lax.broadcasted_iota(jnp.int32, (LANES, LANES), 1)
    m = entries(i, j).astype(BF16)
    return jnp.concatenate([m, m], axis=0)


def _group_ones2(width):
    shift = width.bit_length() - 1
    return _lane_matrix2(lambda i, j: jnp.where((i >> shift) == (j >> shift), 1.0, 0.0))


def _rope_matrix2(rot_dim, period, start):
    half = rot_dim // 2

    def entries(i, j):
        pos = jnp.bitwise_and(j, period - 1) - start
        neg = (pos >= 0) & (pos < half) & (i == j + half)
        plus = (pos >= half) & (pos < rot_dim) & (i == j - half)
        return jnp.where(neg, -1.0, jnp.where(plus, 1.0, 0.0))

    return _lane_matrix2(entries)


def _swap_matrix2():
    return _lane_matrix2(lambda i, j: jnp.where(i == jnp.bitwise_xor(j, HEAD_DIM), 1.0, 0.0))


def _lane_mix(x, m2):
    return _dot(_split_lanes(x), m2)


def _rms_mxu(x, g, n, ones2):
    x2 = x * x
    nt = x.shape[-1] // LANES
    sq = x2[:, :LANES]
    for t in range(1, nt):
        sq = sq + x2[:, t * LANES:(t + 1) * LANES]
    r = lax.rsqrt(_lane_mix(sq, ones2) / n + EPS)
    return x * (r if nt == 1 else jnp.tile(r, (1, nt))) * g


def _with_ones(v, transposed=False):
    if transposed:
        return jnp.concatenate([v, jnp.ones((LANES, v.shape[1]), v.dtype)], axis=0)
    return jnp.concatenate([v, jnp.ones((v.shape[0], LANES), v.dtype)], axis=1)


def _attend(parts, sink=None):
    mx = None
    for s, _, _ in parts:
        pm = jnp.max(s, axis=-1, keepdims=True)
        mx = pm if mx is None else jnp.maximum(mx, pm)
    if sink is not None:
        mx = jnp.maximum(mx, sink)
    acc = None
    for s, v, v_t in parts:
        po = (_dot_nt if v_t else _dot)(jnp.exp2(s - mx).astype(BF16), v)
        acc = po if acc is None else acc + po
    den = acc[:, LANES:]
    if sink is not None:
        den = den + jnp.exp2(sink - mx)
    return acc[:, :LANES] * (1.0 / den)


def _lane_lo():
    return lax.broadcasted_iota(jnp.int32, (1, LANES), 1) < HEAD_DIM


def _store_pair_transposed(ref, bi, p, x):
    xt = x.T
    ref[bi, 0, 2 * p] = xt[:HEAD_DIM]
    ref[bi, 0, 2 * p + 1] = xt[HEAD_DIM:]


def _rope_key_slab(win_ref):
    d = win_ref.shape[1]
    return jnp.concatenate([jnp.zeros((NOPE_A, d), BF16), win_ref[E_KROPE:E_GA, :],
                            jnp.zeros((LANES - QK_A, d), BF16)], axis=0)


def _swap_halves(a):
    return jnp.concatenate([a[HEAD_DIM:], a[:HEAD_DIM]], axis=0)


def _mod_kernel(n_cond, c_ref, w_ref, b_ref, o_ref):
    @pl.when(pl.program_id(1) == 0)
    def _():
        bias = b_ref[pl.ds(pl.program_id(0), 1), :]
        o_ref[0, :n_cond, :] = jnp.broadcast_to(bias, (n_cond, o_ref.shape[2]))
        o_ref[0, n_cond:, :] = jnp.zeros((o_ref.shape[1] - n_cond, o_ref.shape[2]), F32)

    s = _silu(c_ref[...])
    cols = [jnp.broadcast_to(s[:, r:r + 1], (s.shape[0], LANES)) for r in range(n_cond)]
    for t in range(w_ref.shape[2] // LANES):
        sl = slice(t * LANES, (t + 1) * LANES)
        w = w_ref[0, :, sl]
        for r in range(n_cond):
            o_ref[0, r:r + 1, sl] += jnp.sum(w * cols[r], axis=0, keepdims=True)


def _modulation(cond_t, n_cond, w_mod, b_mod):
    depth = w_mod.shape[0]
    tk = 256
    return pl.pallas_call(
        functools.partial(_mod_kernel, n_cond),
        grid=(depth, D_MODEL // tk),
        in_specs=[pl.BlockSpec((tk, 8), lambda l, k: (k, 0)),
                  pl.BlockSpec((1, tk, 3 * D_MODEL), lambda l, k: (l, k, 0)),
                  pl.BlockSpec(b_mod.shape, lambda l, k: (0, 0))],
        out_specs=pl.BlockSpec((1, 8, 3 * D_MODEL), lambda l, k: (l, 0, 0)),
        out_shape=jax.ShapeDtypeStruct((depth, 8, 3 * D_MODEL), F32),
        compiler_params=pltpu.CompilerParams(dimension_semantics=("arbitrary", "arbitrary")),
        name="modulation",
    )(cond_t, w_mod, b_mod)


def _mla_keys(cb, kr, wkk_ref, wkv_ref, kg, rope=None):
    kk = _dot(cb, wkk_ref[...])
    keys = []
    for h in range(N_HEADS):
        k = _rms(kk[:, h * LANES:(h + 1) * LANES] + kr, kg, QK_A)
        if rope is not None:
            k = rope(k)
        keys.append(k.astype(BF16))
    return keys, _dot(cb, wkv_ref[...]).astype(BF16)


def _p0_kernel(x_ref, m_ref, ng_ref, win_ref, qag_ref, wq_ref, kvag_ref, wkk_ref, wkv_ref, qg_ref, kg_ref,
               naqg_ref, nakg_ref, wout_ref,
               xo_ref, ckv_ref, krope_ref, nak_ref, nav_ref, y_scr):
    nbs = x_ref.shape[0]
    x = x_ref[...].reshape(nbs * SEQ, D_MODEL)
    h, gate = _modulate(x, ng_ref[...], m_ref[0:1, :])
    hb = h.astype(BF16)
    lo = _lane_lo()
    hi = jnp.logical_not(lo)
    rows = [slice(bi * SEQ, (bi + 1) * SEQ) for bi in range(nbs)]

    qln = _rms(_dot_nt(hb, win_ref[E_QLAT:E_CKV, :]), qag_ref[...], Q_RANK).astype(BF16)
    q_all = _dot(qln, wq_ref[...])
    ckv_n = _rms(_dot_nt(hb, win_ref[E_CKV:E_KROPE, :]), kvag_ref[...], KV_RANK)
    kr = _dot_nt(hb, _rope_key_slab(win_ref))
    for bi, rs in enumerate(rows):
        ckv_ref[bi, 0] = ckv_n[rs]
        krope_ref[bi, 0] = kr[rs].T[NOPE_A:QK_A]
    keys, vals = _mla_keys(ckv_n.astype(BF16), kr, wkk_ref, wkv_ref, kg_ref[...])
    qg = qg_ref[...] * (QK_A ** -0.5 * LOG2E)

    ga = _dot_nt(hb, win_ref[E_GA:E_QB, :])
    zq = _dot_nt(hb, win_ref[E_QB:E_KB, :])
    zk = _dot_nt(hb, win_ref[E_KB:E_VB, :])
    zv = _dot_nt(hb, win_ref[E_VB:E_GB, :])
    gb = _dot_nt(hb, win_ref[E_GB:E_END, :])
    naqg = naqg_ref[...] * (HEAD_DIM ** -0.5 * LOG2E)

    for p in range(N_PAIRS):
        sl = slice(p * LANES, (p + 1) * LANES)
        ys = slice(4 * LANES + p * LANES, 4 * LANES + (p + 1) * LANES)
        qhs = [_rms(q_all[:, hh * LANES:(hh + 1) * LANES], qg, QK_A).astype(BF16) for hh in (2 * p, 2 * p + 1)]
        qb = _rms_halves(zq[:, sl], naqg, lo)
        kb = _rms_halves(zk[:, sl], nakg_ref[...], lo)
        vb = zv[:, sl]
        kbb, vbb = kb.astype(BF16), _with_ones(vb.astype(BF16))
        va = _with_ones(vals[:, sl])
        qms = [jnp.where(keep, qb, 0.0).astype(BF16) for keep in (lo, hi)]
        for bi, rs in enumerate(rows):
            o2 = [_attend([(_dot_nt(qhs[i][rs], keys[2 * p + i][rs]), va[rs], False)]) for i in (0, 1)]
            y_scr[rs, sl] = (jnp.where(lo, o2[0], o2[1]) * _silu(ga[rs, sl])).astype(BF16)
            _store_pair_transposed(nak_ref, bi, p, kb[rs])
            _store_pair_transposed(nav_ref, bi, p, vb[rs])
            o2 = [_attend([(_dot_nt(qms[i][rs], kbb[rs]), vbb[rs], False)]) for i in (0, 1)]
            y_scr[rs, ys] = (jnp.where(lo, o2[0], o2[1]) * _silu(gb[rs, sl])).astype(BF16)

    xo_ref[...] = (x + gate * _dot(y_scr[...], wout_ref[...])).reshape(nbs, SEQ, D_MODEL)


def _full(shape):
    n = len(shape)
    return pl.BlockSpec(shape, lambda *_: (0,) * n)


class _Row(NamedTuple):
    table: jax.Array
    row: int


def _spec(a):
    if isinstance(a, _Row):
        idx = (a.row,) + (0,) * (a.table.ndim - 1)
        return pl.BlockSpec((None,) + a.table.shape[1:], lambda *_: idx)
    return _full(a.shape)


def _arr(a):
    return a.table if isinstance(a, _Row) else a


def _prompt_even(x, m, ng, win, qag, wq, kvag, wkk, wkv, qg, kg, naqg, nakg, wout):
    nb = x.shape[0]
    nbs = PROMPT_BATCHES_PER_STEP
    assert nb % nbs == 0
    ins = (m, ng, win, qag, wq, kvag, wkk, wkv, qg, kg, naqg, nakg, wout)
    return pl.pallas_call(
        _p0_kernel,
        grid=(nb // nbs,),
        in_specs=[pl.BlockSpec((nbs, SEQ, D_MODEL), lambda b: (b, 0, 0))] + [_spec(a) for a in ins],
        out_specs=[pl.BlockSpec((nbs, SEQ, D_MODEL), lambda b: (b, 0, 0)),
                   pl.BlockSpec((nbs, 1, SEQ, KV_RANK), lambda b: (b, 0, 0, 0)),
                   pl.BlockSpec((nbs, 1, ROPE_A, SEQ), lambda b: (b, 0, 0, 0)),
                   pl.BlockSpec((nbs, 1, N_HEADS, HEAD_DIM, SEQ), lambda b: (b, 0, 0, 0, 0)),
                   pl.BlockSpec((nbs, 1, N_HEADS, HEAD_DIM, SEQ), lambda b: (b, 0, 0, 0, 0))],
        out_shape=[jax.ShapeDtypeStruct((nb, SEQ, D_MODEL), F32),
                   jax.ShapeDtypeStruct((nb, 1, SEQ, KV_RANK), F32),
                   jax.ShapeDtypeStruct((nb, 1, ROPE_A, SEQ), F32),
                   jax.ShapeDtypeStruct((nb, 1, N_HEADS, HEAD_DIM, SEQ), F32),
                   jax.ShapeDtypeStruct((nb, 1, N_HEADS, HEAD_DIM, SEQ), F32)],
        scratch_shapes=[pltpu.VMEM((nbs * SEQ, D_MODEL), BF16)],
        compiler_params=pltpu.CompilerParams(dimension_semantics=("arbitrary",), vmem_limit_bytes=VMEM_LIMIT),
        name="prompt_even",
    )(x, *map(_arr, ins))


def _gqa_pair_operands(k, v, kg2, lo):
    kn = _rms_halves(k, kg2, lo)
    return kn, (kn.astype(BF16), pltpu.roll(kn, HEAD_DIM, 1).astype(BF16)), \
        (_with_ones(v.astype(BF16)), _with_ones(pltpu.roll(v, HEAD_DIM, 1).astype(BF16)))


def _p1_kernel(sink_ref, x_ref, m_ref, ng_ref, win_ref, gqg_ref, gkg_ref, sqg_ref, skg_ref, wout_ref,
               xo_ref, gk_ref, gv_ref, sk_ref, sv_ref, y_scr):
    nbs = x_ref.shape[0]
    x = x_ref[...].reshape(nbs * SEQ, D_MODEL)
    h, gate = _modulate(x, ng_ref[...], m_ref[0:1, :])
    hb = h.astype(BF16)
    lo = _lane_lo()
    hi = jnp.logical_not(lo)
    sc = HEAD_DIM ** -0.5 * LOG2E
    rows = [slice(bi * SEQ, (bi + 1) * SEQ) for bi in range(nbs)]

    branches = ((O_QC, O_KC, O_VC, O_GC, gqg_ref, gkg_ref, gk_ref, gv_ref, False, 0),
                (O_QD, O_KD, O_VD, O_GD, sqg_ref, skg_ref, sk_ref, sv_ref, True, 4 * LANES))
    for oq, ok, ov, og, qg_ref, kg_ref, ck_ref, cv_ref, has_sink, yoff in branches:
        zq = _dot(hb, win_ref[:, oq:oq + 4 * LANES])
        zkv = _dot(hb, win_ref[:, ok:ok + 2 * LANES])
        zg = _dot(hb, win_ref[:, og:og + 4 * LANES])
        v = zkv[:, LANES:]
        kn, ks, vs = _gqa_pair_operands(zkv[:, :LANES], v, kg_ref[...], lo)
        for bi, rs in enumerate(rows):
            _store_pair_transposed(ck_ref, bi, 0, kn[rs])
            _store_pair_transposed(cv_ref, bi, 0, v[rs])
        qg = qg_ref[...] * sc
        for p in range(N_PAIRS):
            sl = slice(p * LANES, (p + 1) * LANES)
            qn = _rms_halves(zq[:, sl], qg, lo)
            qms = [jnp.where(keep, qn, 0.0).astype(BF16) for keep in (lo, hi)]
            kv = p // 2
            for bi, rs in enumerate(rows):
                o2 = []
                for half in (0, 1):
                    swap = 0 if kv == half else 1
                    sink = sink_ref[2 * p + half] * LOG2E if has_sink else None
                    o2.append(_attend([(_dot_nt(qms[half][rs], ks[swap][rs]), vs[swap][rs], False)], sink))
                o = jnp.where(lo, o2[0], o2[1])
                y_scr[rs, yoff + p * LANES:yoff + (p + 1) * LANES] = (o * _silu(zg[rs, sl])).astype(BF16)

    xo_ref[...] = (x + gate * _dot(y_scr[...], wout_ref[...])).reshape(nbs, SEQ, D_MODEL)


def _prompt_odd(sink, x, m, ng, win, gqg, gkg, sqg, skg, wout):
    nb = x.shape[0]
    nbs = PROMPT_BATCHES_PER_STEP
    assert nb % nbs == 0
    ins = (m, ng, win, gqg, gkg, sqg, skg, wout)
    cache_spec = pl.BlockSpec((nbs, 1, 2, HEAD_DIM, SEQ), lambda b: (b, 0, 0, 0, 0))
    cache_shape = jax.ShapeDtypeStruct((nb, 1, 2, HEAD_DIM, SEQ), F32)
    return pl.pallas_call(
        _p1_kernel,
        grid=(nb // nbs,),
        in_specs=[pl.BlockSpec(memory_space=pltpu.SMEM),
                  pl.BlockSpec((nbs, SEQ, D_MODEL), lambda b: (b, 0, 0))] + [_spec(a) for a in ins],
        out_specs=[pl.BlockSpec((nbs, SEQ, D_MODEL), lambda b: (b, 0, 0))] + [cache_spec] * 4,
        out_shape=[jax.ShapeDtypeStruct((nb, SEQ, D_MODEL), F32)] + [cache_shape] * 4,
        scratch_shapes=[pltpu.VMEM((nbs * SEQ, D_MODEL), BF16)],
        compiler_params=pltpu.CompilerParams(dimension_semantics=("arbitrary",), vmem_limit_bytes=VMEM_LIMIT),
        name="prompt_odd",
    )(sink, x, *map(_arr, ins))


def _s0a_kernel(x_ref, m_ref, ng_ref, win_ref, qag_ref, wq_ref, kvag_ref, wkk_ref, wkv_ref, qg_ref, kg_ref,
                naqg_ref, nakg_ref, cos_ref, sin_ref,
                qa_ref, ka_ref, va_ref, qb_ref, kb_ref, vb_ref, g_ref):
    b = pl.program_id(0)
    lo = _lane_lo()
    ones_all = _group_ones2(LANES)
    partner = _rope_matrix2(ROPE_A, LANES, NOPE_A)
    hb = _modulate(x_ref[0], ng_ref[...], m_ref[pl.ds(1 + b, 1), :])[0].astype(BF16)
    cos, sin = cos_ref[...], sin_ref[...]

    qln = _rms(_dot_nt(hb, win_ref[E_QLAT:E_CKV, :]), qag_ref[...], Q_RANK).astype(BF16)
    q_all = _dot(qln, wq_ref[...])
    ckv_n = _rms(_dot_nt(hb, win_ref[E_CKV:E_KROPE, :]), kvag_ref[...], KV_RANK)
    kr = _dot_nt(hb, _rope_key_slab(win_ref))
    cb = ckv_n.astype(BF16)
    kk = _dot(cb, wkk_ref[...])
    va_ref[0] = _dot(cb, wkv_ref[...]).astype(BF16)
    zq = _dot_nt(hb, win_ref[E_QB:E_KB, :])
    zk = _dot_nt(hb, win_ref[E_KB:E_VB, :])
    vb_ref[0] = _dot_nt(hb, win_ref[E_VB:E_GB, :]).astype(BF16)
    g_ref[0, :, 0:4 * LANES] = _silu(_dot_nt(hb, win_ref[E_GA:E_QB, :]))
    g_ref[0, :, 4 * LANES:8 * LANES] = _silu(_dot_nt(hb, win_ref[E_GB:E_END, :]))

    qg = qg_ref[...] * (QK_A ** -0.5 * LOG2E)
    kg = kg_ref[...]
    k_partner = _lane_mix(kr * kg, partner) * sin
    for hh in range(N_HEADS):
        sl = slice(hh * LANES, (hh + 1) * LANES)
        qn = _rms_mxu(q_all[:, sl], qg, QK_A, ones_all)
        qa_ref[0, :, sl] = (qn * cos + _lane_mix(qn, partner) * sin).astype(BF16)
        k_raw = kk[:, sl] + kr
        k_inv = lax.rsqrt(_lane_mix(k_raw * k_raw, ones_all) / QK_A + EPS)
        ka_ref[0, :, sl] = ((k_raw * kg * cos + k_partner) * k_inv).astype(BF16)
    naqg = naqg_ref[...] * (HEAD_DIM ** -0.5 * LOG2E)
    for p in range(N_PAIRS):
        sl = slice(p * LANES, (p + 1) * LANES)
        qb_ref[0, :, sl] = _rms_halves(zq[:, sl], naqg, lo).astype(BF16)
        kb_ref[0, :, sl] = _rms_halves(zk[:, sl], nakg_ref[...], lo).astype(BF16)


def _sample_even_proj(x, m, ng, win, qag, wq, kvag, wkk, wkv, qg, kg, naqg, nakg, cos, sin):
    nb, s, _ = x.shape
    nq = s // PROJ_BLOCK
    ins = (m, ng, win, qag, wq, kvag, wkk, wkv, qg, kg, naqg, nakg)
    tab = pl.BlockSpec((PROJ_BLOCK, LANES), lambda b, j: (j, 0))

    def blk(w):
        return pl.BlockSpec((1, PROJ_BLOCK, w), lambda b, j: (b, j, 0))

    def shp(w, dt):
        return jax.ShapeDtypeStruct((nb, s, w), dt)

    return pl.pallas_call(
        _s0a_kernel,
        grid=(nb, nq),
        in_specs=[blk(D_MODEL)] + [_spec(a) for a in ins] + [tab, tab],
        out_specs=[blk(1024), blk(1024), blk(512), blk(512), blk(512), blk(512), blk(1024)],
        out_shape=[shp(1024, BF16), shp(1024, BF16), shp(512, BF16), shp(512, BF16), shp(512, BF16),
                   shp(512, BF16), shp(1024, F32)],
        compiler_params=pltpu.CompilerParams(dimension_semantics=("arbitrary", "arbitrary"),
                                             vmem_limit_bytes=VMEM_LIMIT),
        name="sample_even_proj",
    )(x, *map(_arr, ins), cos, sin)


def _build_bias_table(rpb_ref, tile_scr, tab_ref):
    qc = lax.broadcasted_iota(jnp.int32, (GRID_W, LANES), 0)
    lane = lax.broadcasted_iota(jnp.int32, (GRID_W, LANES), 1)
    kc = jnp.bitwise_and(lane, GRID_W - 1)
    lo = lane < GRID_W
    diff = kc - qc + (NA_COLS - 1)
    cs = jnp.clip(qc - NA_COLS // 2, 0, GRID_W - NA_COLS)
    valid = (kc >= cs) & (kc < cs + NA_COLS)
    tab_ref[...] = jnp.zeros(tab_ref.shape, F32)
    tile_scr[RPB_ROWS] = jnp.zeros((GRID_W, LANES), F32)

    def per_head(h, carry):
        for dr in range(RPB_ROWS):
            t = jnp.zeros((GRID_W, LANES), F32)
            for dc in range(RPB_COLS):
                t = jnp.where(diff == dc, rpb_ref[(h * RPB_ROWS + dr) * RPB_COLS + dc], t)
            tile_scr[dr] = jnp.where(valid, t * LOG2E, NEG_INF)
        for c in range(NA_ROWS // 2, NA_ROWS // 2 + NA_ROWS):
            d0 = 2 * c - NA_ROWS
            tab_ref[0, h, c] = jnp.where(lo, tile_scr[d0], tile_scr[d0 + 1])
            tab_ref[1, h, c] = jnp.where(lo, tile_scr[d0 - 1 if d0 > 0 else RPB_ROWS], tile_scr[d0])
        return carry

    lax.fori_loop(0, N_HEADS, per_head, 0)


def _s0b_kernel(rpb_ref, x_ref, m_ref, qa_ref, ka_ref, va_ref, qb_ref, kb_ref, vb_ref, g_ref,
                cckv_ref, ckr_ref, cnk_ref, cnv_ref, wkk_ref, wkv_ref, kg_ref, wout_ref,
                xo_ref, kca_scr, vca_scr, tile_scr, tab_scr, y_scr):
    b = pl.program_id(0)
    j = pl.program_id(1)
    lo = _lane_lo()
    n_lat = ka_ref.shape[1]

    @pl.when((b == 0) & (j == 0))
    def _():
        _build_bias_table(rpb_ref, tile_scr, tab_scr)

    @pl.when(j == 0)
    def _():
        kr_t = jnp.concatenate([jnp.zeros((NOPE_A, PAST_LEN), F32), ckr_ref[0],
                                jnp.zeros((LANES - QK_A, PAST_LEN), F32)], axis=0)
        keys, vals = _mla_keys(cckv_ref[0].astype(BF16), kr_t.T, wkk_ref, wkv_ref, kg_ref[...])
        for hh in range(N_HEADS):
            kca_scr[:, hh * LANES:(hh + 1) * LANES] = keys[hh]
        vca_scr[...] = vals

    kidx = lax.broadcasted_iota(jnp.int32, (1, n_lat), 1)
    for p in range(N_PAIRS):
        sl = slice(p * LANES, (p + 1) * LANES)
        o2 = []
        va = _with_ones(va_ref[0, :, sl])
        vca = _with_ones(vca_scr[:, sl])
        for hh in (2 * p, 2 * p + 1):
            hs = slice(hh * LANES, (hh + 1) * LANES)
            q = qa_ref[0, :, hs]
            o2.append(_attend([(_dot_nt(q, ka_ref[0, :, hs]), va, False),
                               (_dot_nt(q, kca_scr[:, hs]), vca, False)]))
        oa = jnp.where(lo, o2[0], o2[1])
        y_scr[:, sl] = (oa * g_ref[0, :, sl]).astype(BF16)

        qb = qb_ref[0, :, sl]
        kb = kb_ref[0, :, sl]
        vb = _with_ones(vb_ref[0, :, sl])
        kcb = cnk_ref[0, sl, :].astype(BF16)
        vcb = _with_ones(cnv_ref[0, sl, :].astype(BF16), transposed=True)
        o2 = []
        for half in (0, 1):
            head = 2 * p + half
            qm = jnp.where(lo if half == 0 else jnp.logical_not(lo), qb, jnp.zeros_like(qb))
            s_lat = _dot_nt(qm, kb)
            rows = []
            for local in range(Q_BLOCK // GRID_W):
                qr = j * (Q_BLOCK // GRID_W) + local
                par = 0 if local % 2 == 1 else 1
                c0 = (RPB_ROWS + par - local) // 2 - 2 * j
                bias = jnp.concatenate([tab_scr[par, head, c0 + t] for t in range(n_lat // LANES)], axis=1)
                r0 = jnp.clip(qr - NA_ROWS // 2, 0, n_lat // GRID_W - NA_ROWS) * GRID_W
                ok = (kidx >= r0) & (kidx < r0 + NA_ROWS * GRID_W)
                rows.append(jnp.where(ok, s_lat[local * GRID_W:(local + 1) * GRID_W] + bias, NEG_INF))
            s_lat = jnp.concatenate(rows, axis=0)
            o2.append(_attend([(s_lat, vb, False), (_dot(qm, kcb), vcb, True)]))
        ob = jnp.where(lo, o2[0], o2[1])
        ys = slice(4 * LANES + p * LANES, 4 * LANES + (p + 1) * LANES)
        y_scr[:, ys] = (ob * g_ref[0, :, ys]).astype(BF16)

    d = x_ref.shape[-1]
    gate = m_ref[pl.ds(1 + b, 1), 2 * d:]
    xo_ref[0] = x_ref[0] + gate * _dot(y_scr[...], wout_ref[...])


def _sample_even_attn(rpb, x, m, qa, ka, va, qb, kb, vb, g, cckv, ckr, cnk, cnv, wkk, wkv, kg, wout):
    nb, s, _ = x.shape
    nq = s // Q_BLOCK

    def blk(w):
        return pl.BlockSpec((1, Q_BLOCK, w), lambda b, j: (b, j, 0))

    def per_batch(a):
        return pl.BlockSpec((1,) + a.shape[1:], lambda b, j: (b, 0, 0))

    return pl.pallas_call(
        _s0b_kernel,
        grid=(nb, nq),
        in_specs=[pl.BlockSpec(memory_space=pltpu.SMEM), blk(D_MODEL), _spec(m),
                  blk(1024), per_batch(ka), per_batch(va), blk(512), per_batch(kb), per_batch(vb), blk(1024),
                  per_batch(cckv), per_batch(ckr), per_batch(cnk), per_batch(cnv),
                  _full(wkk.shape), _full(wkv.shape), _spec(kg), _full(wout.shape)],
        out_specs=blk(D_MODEL),
        out_shape=jax.ShapeDtypeStruct(x.shape, F32),
        scratch_shapes=[pltpu.VMEM((PAST_LEN, N_HEADS * LANES), BF16),
                        pltpu.VMEM((PAST_LEN, N_HEADS * HEAD_DIM), BF16),
                        pltpu.VMEM((RPB_ROWS + 1, GRID_W, LANES), F32),
                        pltpu.VMEM((2, N_HEADS, BIAS_CHUNKS, GRID_W, LANES), F32),
                        pltpu.VMEM((Q_BLOCK, D_MODEL), BF16)],
        compiler_params=pltpu.CompilerParams(dimension_semantics=("arbitrary", "arbitrary"),
                                             vmem_limit_bytes=VMEM_LIMIT),
        name="sample_even_attn",
    )(rpb, x, _arr(m), qa, ka, va, qb, kb, vb, g, cckv, ckr, cnk, cnv, wkk, wkv, _arr(kg), wout)


def _s1a_kernel(x_ref, m_ref, ng_ref, win_ref, gqg_ref, gkg_ref, sqg_ref, skg_ref, cos_ref, sin_ref,
                qc_ref, kc_ref, vc_ref, qd_ref, kd_ref, vd_ref, g_ref):
    b = pl.program_id(0)
    lo = _lane_lo()
    partner = _rope_matrix2(HEAD_DIM, HEAD_DIM, 0)
    swap = _swap_matrix2()[:LANES]
    hb = _modulate(x_ref[0], ng_ref[...], m_ref[pl.ds(1 + b, 1), :])[0].astype(BF16)
    cos, sin = cos_ref[...], sin_ref[...]
    sc = HEAD_DIM ** -0.5 * LOG2E

    def rope(t):
        return t * cos + _lane_mix(t, partner) * sin

    branches = ((O_QC, O_KC, O_GC, gqg_ref, gkg_ref, qc_ref, kc_ref, vc_ref, 0),
                (O_QD, O_KD, O_GD, sqg_ref, skg_ref, qd_ref, kd_ref, vd_ref, 4 * LANES))
    for oq, ok, og, qg_ref, kg_ref, q_out, k_out, v_out, goff in branches:
        zq = _dot(hb, win_ref[:, oq:oq + 4 * LANES])
        zkv = _dot(hb, win_ref[:, ok:ok + 2 * LANES])
        qg = qg_ref[...] * sc
        for p in range(N_PAIRS):
            sl = slice(p * LANES, (p + 1) * LANES)
            q_out[0, :, sl] = rope(_rms_halves(zq[:, sl], qg, lo)).astype(BF16)
        kn = rope(_rms_halves(zkv[:, :LANES], kg_ref[...], lo))
        v = zkv[:, LANES:]
        for out, val in ((k_out, kn.astype(BF16)), (v_out, v.astype(BF16))):
            out[0, :, 0:LANES] = val
            out[0, :, LANES:2 * LANES] = _dot(val, swap).astype(BF16)
        g_ref[0, :, goff:goff + 4 * LANES] = _silu(_dot(hb, win_ref[:, og:og + 4 * LANES]))


def _sample_odd_proj(x, m, ng, win, gqg, gkg, sqg, skg, cos, sin):
    nb, s, _ = x.shape
    nq = s // PROJ_BLOCK
    ins = (m, ng, win, gqg, gkg, sqg, skg)
    tab = pl.BlockSpec((PROJ_BLOCK, LANES), lambda b, j: (j, 0))

    def blk(w):
        return pl.BlockSpec((1, PROJ_BLOCK, w), lambda b, j: (b, j, 0))

    def shp(w, dt):
        return jax.ShapeDtypeStruct((nb, s, w), dt)

    return pl.pallas_call(
        _s1a_kernel,
        grid=(nb, nq),
        in_specs=[blk(D_MODEL)] + [_spec(a) for a in ins] + [tab, tab],
        out_specs=[blk(512), blk(256), blk(256), blk(512), blk(256), blk(256), blk(1024)],
        out_shape=[shp(512, BF16), shp(256, BF16), shp(256, BF16), shp(512, BF16), shp(256, BF16),
                   shp(256, BF16), shp(1024, F32)],
        compiler_params=pltpu.CompilerParams(dimension_semantics=("arbitrary", "arbitrary"),
                                             vmem_limit_bytes=VMEM_LIMIT),
        name="sample_odd_proj",
    )(x, *map(_arr, ins), cos, sin)


def _s1b_kernel(sink_ref, x_ref, m_ref, qc_ref, kc_ref, vc_ref, qd_ref, kd_ref, vd_ref, g_ref,
                cgk_ref, cgv_ref, csk_ref, csv_ref, wout_ref, xo_ref, y_scr):
    b = pl.program_id(0)
    j = pl.program_id(1)
    lo = _lane_lo()
    n_lat = kc_ref.shape[1]
    win_keys = Q_BLOCK + 2 * SWA_HALF

    def ctx_pair(ref, values=False):
        a = ref[0].astype(BF16)
        pair = (a, _swap_halves(a))
        return tuple(_with_ones(t, transposed=True) for t in pair) if values else pair

    cgk, cgv, csk, csv = ctx_pair(cgk_ref), ctx_pair(cgv_ref, True), ctx_pair(csk_ref), ctx_pair(csv_ref, True)
    vcs = [_with_ones(vc_ref[0, :, w * LANES:(w + 1) * LANES]) for w in (0, 1)]

    ks = pl.multiple_of(jnp.clip(j * Q_BLOCK - SWA_HALF, 0, n_lat - win_keys), SWA_HALF)
    qpos = j * Q_BLOCK + lax.broadcasted_iota(jnp.int32, (Q_BLOCK, win_keys), 0)
    kpos = ks + lax.broadcasted_iota(jnp.int32, (Q_BLOCK, win_keys), 1)
    band = jnp.abs(qpos - kpos) <= SWA_HALF
    vds = [_with_ones(vd_ref[0, pl.ds(ks, win_keys), w * LANES:(w + 1) * LANES]) for w in (0, 1)]

    for p in range(N_PAIRS):
        sl = slice(p * LANES, (p + 1) * LANES)
        kv = p // 2
        qc = qc_ref[0, :, sl]
        qd = qd_ref[0, :, sl]
        oc2, od2 = [], []
        for half in (0, 1):
            swap = 0 if kv == half else 1
            ws = slice(swap * LANES, (swap + 1) * LANES)
            keep = lo if half == 0 else jnp.logical_not(lo)
            qm = jnp.where(keep, qc, jnp.zeros_like(qc))
            oc2.append(_attend([(_dot_nt(qm, kc_ref[0, :, ws]), vcs[swap], False),
                                (_dot(qm, cgk[swap]), cgv[swap], True)]))
            qm = jnp.where(keep, qd, jnp.zeros_like(qd))
            s_loc = jnp.where(band, _dot_nt(qm, kd_ref[0, pl.ds(ks, win_keys), ws]), NEG_INF)
            od2.append(_attend([(s_loc, vds[swap], False),
                                (_dot(qm, csk[swap]), csv[swap], True)], sink_ref[2 * p + half] * LOG2E))
        y_scr[:, sl] = (jnp.where(lo, oc2[0], oc2[1]) * g_ref[0, :, sl]).astype(BF16)
        ys = slice(4 * LANES + p * LANES, 4 * LANES + (p + 1) * LANES)
        y_scr[:, ys] = (jnp.where(lo, od2[0], od2[1]) * g_ref[0, :, ys]).astype(BF16)

    d = x_ref.shape[-1]
    gate = m_ref[pl.ds(1 + b, 1), 2 * d:]
    xo_ref[0] = x_ref[0] + gate * _dot(y_scr[...], wout_ref[...])


def _sample_odd_attn(sink, x, m, qc, kc, vc, qd, kd, vd, g, cgk, cgv, csk, csv, wout):
    nb, s, _ = x.shape
    nq = s // Q_BLOCK

    def blk(w):
        return pl.BlockSpec((1, Q_BLOCK, w), lambda b, j: (b, j, 0))

    def per_batch(a):
        return pl.BlockSpec((1,) + a.shape[1:], lambda b, j: (b, 0, 0))

    return pl.pallas_call(
        _s1b_kernel,
        grid=(nb, nq),
        in_specs=[pl.BlockSpec(memory_space=pltpu.SMEM), blk(D_MODEL), _spec(m),
                  blk(512), per_batch(kc), per_batch(vc), blk(512), per_batch(kd), per_batch(vd), blk(1024),
                  per_batch(cgk), per_batch(cgv), per_batch(csk), per_batch(csv), _full(wout.shape)],
        out_specs=blk(D_MODEL),
        out_shape=jax.ShapeDtypeStruct(x.shape, F32),
        scratch_shapes=[pltpu.VMEM((Q_BLOCK, D_MODEL), BF16)],
        compiler_params=pltpu.CompilerParams(dimension_semantics=("arbitrary", "arbitrary"),
                                             vmem_limit_bytes=VMEM_LIMIT),
        name="sample_odd_attn",
    )(sink, x, _arr(m), qc, kc, vc, qd, kd, vd, g, cgk, cgv, csk, csv, wout)


WEIGHT_PREP_STEPS = 8


G_MLA_Q, G_MLA_K, G_NA_Q, G_NA_K, G_GQA_Q, G_GQA_K, G_SWA_Q, G_SWA_K, N_GAINS = range(9)


def _weight_prep_kernel(wie_ref, wio_ref, woe_ref, woo_ref, wqu_ref, wkv_ref, ng_ref, *refs):
    gain_refs = refs[:N_GAINS]
    win_e_ref, win_o_ref, wout_e_ref, wout_o_ref, wq_ref, wkk_ref, wkvv_ref, gt_ref, ngt_ref = refs[N_GAINS:]
    gt_ref[...] = jnp.zeros(gt_ref.shape, F32)
    for r, g_ref in enumerate(gain_refs):
        w = g_ref.shape[1]
        for off in range(0, LANES - w + 1, w):
            gt_ref[r, :, off:off + w] = g_ref[...]
    for layer in range(ngt_ref.shape[0]):
        ngt_ref[layer] = ng_ref[layer:layer + 1, :]

    win_e_ref[...] = wie_ref[...].astype(BF16)
    win_o_ref[...] = wio_ref[0].astype(BF16)
    wout_e_ref[...] = woe_ref[0].astype(BF16)
    wout_o_ref[...] = woo_ref[0].astype(BF16)

    wq_ref[...] = jnp.zeros(wq_ref.shape, BF16)
    for h in range(N_HEADS):
        wq_ref[:, h * LANES:h * LANES + QK_A] = wqu_ref[0, :, h * QK_A:(h + 1) * QK_A].astype(BF16)
    lo = _lane_lo()
    for p in range(N_PAIRS):
        a = wkv_ref[0, :, (2 * p) * LANES:(2 * p + 1) * LANES]
        c = wkv_ref[0, :, (2 * p + 1) * LANES:(2 * p + 2) * LANES]
        wkk_ref[:, (2 * p) * LANES:(2 * p + 1) * LANES] = jnp.where(lo, a, 0.0).astype(BF16)
        wkk_ref[:, (2 * p + 1) * LANES:(2 * p + 2) * LANES] = jnp.where(lo, c, 0.0).astype(BF16)
        wkvv_ref[:, p * LANES:(p + 1) * LANES] = jnp.where(lo, pltpu.roll(a, HEAD_DIM, 1), c).astype(BF16)


def _weight_prep(w_in_even_t, w_in_odd, w_out_even, w_out_odd, w_q_up, w_kv_up, norm_g, gains):
    n = WEIGHT_PREP_STEPS
    assert len(gains) == N_GAINS
    small = (norm_g,) + tuple(gains)
    ins = (w_in_odd, w_out_even, w_out_odd, w_q_up, w_kv_up)
    out_cols = (O_END, D_MODEL, D_MODEL, N_HEADS * LANES, N_HEADS * LANES, N_HEADS * HEAD_DIM)
    out_rows = (D_MODEL, D_MODEL, D_MODEL, Q_RANK, KV_RANK, KV_RANK)
    te = E_END // 6
    assert te * 6 == E_END and te % 16 == 0
    even_spec = pl.BlockSpec((te, D_MODEL), lambda i: (jnp.minimum(i, 5), 0))
    return pl.pallas_call(
        _weight_prep_kernel,
        grid=(n,),
        in_specs=[even_spec] + [pl.BlockSpec((1, a.shape[1] // n, a.shape[2]), lambda i: (0, i, 0)) for a in ins]
        + [_full(a.shape) for a in small],
        out_specs=[even_spec] + [pl.BlockSpec((r // n, c), lambda i: (i, 0)) for r, c in zip(out_rows, out_cols)]
        + [_full((N_GAINS, 1, LANES)), _full((norm_g.shape[0], 1, D_MODEL))],
        out_shape=[jax.ShapeDtypeStruct((E_END, D_MODEL), BF16)]
        + [jax.ShapeDtypeStruct((r, c), BF16) for r, c in zip(out_rows, out_cols)]
        + [jax.ShapeDtypeStruct((N_GAINS, 1, LANES), F32), jax.ShapeDtypeStruct((norm_g.shape[0], 1, D_MODEL), F32)],
        compiler_params=pltpu.CompilerParams(dimension_semantics=("arbitrary",), vmem_limit_bytes=VMEM_LIMIT),
        name="weight_prep",
    )(w_in_even_t, *ins, *small)


def _feature_major(c):
    b, h, l, d = c.shape
    return jnp.swapaxes(c, -1, -2).reshape(b, h * d, l)


def _token_major(c):
    return jnp.swapaxes(c, -1, -2)


def _rope_tables(s, rot_dim, period, start):
    quarter = rot_dim // 4
    t = np.arange(s)
    inv = ROPE_THETA ** (-np.arange(quarter, dtype=np.float64) / quarter)
    row = (t // GRID_W).astype(np.float64)[:, None] * inv
    col = (t % GRID_W).astype(np.float64)[:, None] * inv
    ang = np.concatenate([row, col], axis=-1)
    cos, sin = np.cos(ang), np.sin(ang)
    pre = np.ones((s, start))
    post = np.zeros((s, period - start - rot_dim))
    c = np.concatenate([pre, cos, cos, post], axis=-1)
    sn = np.concatenate([0 * pre, sin, sin, post], axis=-1)
    rep = LANES // period
    return jnp.asarray(np.tile(c, (1, rep)), F32), jnp.asarray(np.tile(sn, (1, rep)), F32)


def kernel(x_prompt, x_sample, cache_mla_ckv, cache_mla_krope, cache_na_k, cache_na_v, cache_gqa_k, cache_gqa_v, cache_swa_k, cache_swa_v, c, c_ctx, norm_g, w_mod, b_mod, w_in_even, mla_qa_g, w_q_up, mla_kva_g, w_kv_up, mla_q_g, mla_k_g, na_q_g, na_k_g, na_rpb, w_out_even, w_in_odd, gqa_q_g, gqa_k_g, swa_q_g, swa_k_g, swa_sink, w_out_odd):
    n_dec = x_sample.shape[0]
    assert w_mod.shape[0] == 2 and n_dec + 1 <= 8

    cond_t = jnp.concatenate([c_ctx[:, None], c.T, jnp.zeros((D_MODEL, 7 - n_dec), F32)], axis=1)
    m_all = _modulation(cond_t, 1 + n_dec, w_mod, b_mod)

    gains = (mla_q_g, mla_k_g, na_q_g, na_k_g, gqa_q_g, gqa_k_g, swa_q_g, swa_k_g)
    win_e, win_o, wout_e, wout_o, wq, wkk, wkv, gt, ngt = _weight_prep(
        jnp.swapaxes(w_in_even[0], 0, 1), w_in_odd, w_out_even, w_out_odd, w_q_up, w_kv_up, norm_g, gains)
    m_even, m_odd = _Row(m_all, 0), _Row(m_all, 1)
    even = (_Row(ngt, 0), win_e, mla_qa_g, wq, mla_kva_g, wkk, wkv,
            _Row(gt, G_MLA_Q), _Row(gt, G_MLA_K), _Row(gt, G_NA_Q), _Row(gt, G_NA_K))
    odd = (_Row(ngt, 1), win_o, _Row(gt, G_GQA_Q), _Row(gt, G_GQA_K), _Row(gt, G_SWA_Q), _Row(gt, G_SWA_K))
    sink = swa_sink[0].astype(F32)

    xp1, new_ckv, new_krope, new_na_k, new_na_v = _prompt_even(x_prompt, m_even, *even, wout_e)
    xp2, new_gqa_k, new_gqa_v, new_swa_k, new_swa_v = _prompt_odd(sink, xp1, m_odd, *odd, wout_o)

    cos_e, sin_e = _rope_tables(DEC_SEQ, ROPE_A, LANES, NOPE_A)
    qa, ka, va, qbs, kbs, vbs, g0 = _sample_even_proj(x_sample, m_even, *even, cos_e, sin_e)
    ckr = jnp.swapaxes(cache_mla_krope[:, 0], -1, -2)
    xs1 = _sample_even_attn(na_rpb[0].reshape(-1), x_sample, m_even, qa, ka, va, qbs, kbs, vbs, g0,
                            cache_mla_ckv[:, 0], ckr, _feature_major(cache_na_k[:, 0]),
                            _feature_major(cache_na_v[:, 0]), wkk, wkv, _Row(gt, G_MLA_K), wout_e)
    cos_o, sin_o = _rope_tables(DEC_SEQ, HEAD_DIM, HEAD_DIM, 0)
    qc, kc, vc, qd, kd, vd, g1 = _sample_odd_proj(xs1, m_odd, *odd, cos_o, sin_o)
    xs2 = _sample_odd_attn(sink, xs1, m_odd, qc, kc, vc, qd, kd, vd, g1,
                           _feature_major(cache_gqa_k[:, 0]), _feature_major(cache_gqa_v[:, 0]),
                           _feature_major(cache_swa_k[:, 0]), _feature_major(cache_swa_v[:, 0]), wout_o)

    caches = (new_krope, new_na_k, new_na_v, new_gqa_k, new_gqa_v, new_swa_k, new_swa_v)
    return (xp2, xs2, new_ckv) + tuple(_token_major(c) for c in caches)
```

```python
import functools
from typing import NamedTuple

import jax
import jax.numpy as jnp
import numpy as np
from jax import lax
from jax.experimental import pallas as pl
from jax.experimental.pallas import tpu as pltpu

F32 = jnp.float32
BF16 = jnp.bfloat16

D_MODEL = 1024
SEQ = 256
DEC_SEQ = 1024
PAST_LEN = 256
GRID_W = 64
HEAD_DIM = 64
Q_RANK = 256
KV_RANK = 128
NOPE_A = 64
ROPE_A = 32
QK_A = NOPE_A + ROPE_A
N_HEADS = 8
NA_ROWS = 8
NA_COLS = 16
SWA_HALF = 128
ROPE_THETA = 10000.0
EPS = 1e-6
NEG_INF = -1e30
LOG2E = 1.4426950408889634

LANES = 128
Q_BLOCK = 512
PROJ_BLOCK = 512
PROMPT_BATCHES_PER_STEP = 2
N_PAIRS = N_HEADS // 2
RPB_ROWS = 2 * NA_ROWS - 1
RPB_COLS = 2 * NA_COLS - 1
BIAS_CHUNKS = 16
VMEM_LIMIT = 48 * 1024 * 1024

E_QLAT, E_CKV, E_KROPE, E_GA, E_QB, E_KB, E_VB, E_GB, E_END = 0, 256, 384, 416, 928, 1440, 1952, 2464, 2976
O_QC, O_KC, O_VC, O_GC, O_QD, O_KD, O_VD, O_GD, O_END = 0, 512, 640, 768, 1280, 1792, 1920, 2048, 2560


def _dot(a, b):
    return lax.dot_general(a, b, (((1,), (0,)), ((), ())), preferred_element_type=F32)


def _dot_nt(a, b):
    return lax.dot_general(a, b, (((1,), (1,)), ((), ())), preferred_element_type=F32)


def _silu(x):
    return x / (1.0 + jnp.exp(-x))


def _rms(x, g, n):
    ss = jnp.sum(x * x, axis=-1, keepdims=True)
    return x * lax.rsqrt(ss / n + EPS) * g


def _rms_halves(x, g2, lo):
    x2 = x * x
    s_lo = jnp.sum(jnp.where(lo, x2, 0.0), axis=-1, keepdims=True)
    s_hi = jnp.sum(jnp.where(lo, 0.0, x2), axis=-1, keepdims=True)
    r = jnp.where(lo, lax.rsqrt(s_lo / HEAD_DIM + EPS), lax.rsqrt(s_hi / HEAD_DIM + EPS))
    return x * r * g2


def _modulate(x, g, m):
    d = x.shape[-1]
    xn = x * lax.rsqrt(jnp.mean(x * x, axis=-1, keepdims=True) + EPS) * g
    return xn * (1.0 + m[:, d:2 * d]) + m[:, :d], m[:, 2 * d:]


def _split_lanes(x):
    hi = x.astype(BF16)
    lo = (x - hi.astype(F32)).astype(BF16)
    return jnp.concatenate([hi, lo], axis=1)


def _lane_matrix2(entries):
    i = lax.broadcasted_iota(jnp.int32, (LANES, LANES), 0)
    j = lax.broadcasted_iota(jnp.int32, (LANES, LANES), 1)
    m = entries(i, j).astype(BF16)
    return jnp.concatenate([m, m], axis=0)


def _group_ones2(width):
    shift = width.bit_length() - 1
    return _lane_matrix2(lambda i, j: jnp.where((i >> shift) == (j >> shift), 1.0, 0.0))


def _rope_matrix2(rot_dim, period, start):
    half = rot_dim // 2

    def entries(i, j):
        pos = jnp.bitwise_and(j, period - 1) - start
        neg = (pos >= 0) & (pos < half) & (i == j + half)
        plus = (pos >= half) & (pos < rot_dim) & (i == j - half)
        return jnp.where(neg, -1.0, jnp.where(plus, 1.0, 0.0))

    return _lane_matrix2(entries)


def _swap_matrix2():
    return _lane_matrix2(lambda i, j: jnp.where(i == jnp.bitwise_xor(j, HEAD_DIM), 1.0, 0.0))


def _lane_mix(x, m2):
    return _dot(_split_lanes(x), m2)


def _rms_mxu(x, g, n, ones2):
    x2 = x * x
    nt = x.shape[-1] // LANES
    sq = x2[:, :LANES]
    for t in range(1, nt):
        sq = sq + x2[:, t * LANES:(t + 1) * LANES]
    r = lax.rsqrt(_lane_mix(sq, ones2) / n + EPS)
    return x * (r if nt == 1 else jnp.tile(r, (1, nt))) * g


def _with_ones(v, transposed=False):
    if transposed:
        return jnp.concatenate([v, jnp.ones((LANES, v.shape[1]), v.dtype)], axis=0)
    return jnp.concatenate([v, jnp.ones((v.shape[0], LANES), v.dtype)], axis=1)


def _attend(parts, sink=None):
    mx = None
    for s, _, _ in parts:
        pm = jnp.max(s, axis=-1, keepdims=True)
        mx = pm if mx is None else jnp.maximum(mx, pm)
    if sink is not None:
        mx = jnp.maximum(mx, sink)
    acc = None
    for s, v, v_t in parts:
        po = (_dot_nt if v_t else _dot)(jnp.exp2(s - mx).astype(BF16), v)
        acc = po if acc is None else acc + po
    den = acc[:, LANES:]
    if sink is not None:
        den = den + jnp.exp2(sink - mx)
    return acc[:, :LANES] * (1.0 / den)


def _lane_lo():
    return lax.broadcasted_iota(jnp.int32, (1, LANES), 1) < HEAD_DIM


def _store_pair_transposed(ref, bi, p, x):
    xt = x.T
    ref[bi, 0, 2 * p] = xt[:HEAD_DIM]
    ref[bi, 0, 2 * p + 1] = xt[HEAD_DIM:]


def _rope_key_slab(win_ref):
    d = win_ref.shape[1]
    return jnp.concatenate([jnp.zeros((NOPE_A, d), BF16), win_ref[E_KROPE:E_GA, :],
                            jnp.zeros((LANES - QK_A, d), BF16)], axis=0)


def _swap_halves(a):
    return jnp.concatenate([a[HEAD_DIM:], a[:HEAD_DIM]], axis=0)


def _mod_kernel(n_cond, c_ref, w_ref, b_ref, o_ref):
    @pl.when(pl.program_id(1) == 0)
    def _():
        bias = b_ref[pl.ds(pl.program_id(0), 1), :]
        o_ref[0, :n_cond, :] = jnp.broadcast_to(bias, (n_cond, o_ref.shape[2]))
        o_ref[0, n_cond:, :] = jnp.zeros((o_ref.shape[1] - n_cond, o_ref.shape[2]), F32)

    s = _silu(c_ref[...])
    cols = [jnp.broadcast_to(s[:, r:r + 1], (s.shape[0], LANES)) for r in range(n_cond)]
    for t in range(w_ref.shape[2] // LANES):
        sl = slice(t * LANES, (t + 1) * LANES)
        w = w_ref[0, :, sl]
        for r in range(n_cond):
            o_ref[0, r:r + 1, sl] += jnp.sum(w * cols[r], axis=0, keepdims=True)


def _modulation(cond_t, n_cond, w_mod, b_mod):
    depth = w_mod.shape[0]
    tk = 256
    return pl.pallas_call(
        functools.partial(_mod_kernel, n_cond),
        grid=(depth, D_MODEL // tk),
        in_specs=[pl.BlockSpec((tk, 8), lambda l, k: (k, 0)),
                  pl.BlockSpec((1, tk, 3 * D_MODEL), lambda l, k: (l, k, 0)),
                  pl.BlockSpec(b_mod.shape, lambda l, k: (0, 0))],
        out_specs=pl.BlockSpec((1, 8, 3 * D_MODEL), lambda l, k: (l, 0, 0)),
        out_shape=jax.ShapeDtypeStruct((depth, 8, 3 * D_MODEL), F32),
        compiler_params=pltpu.CompilerParams(dimension_semantics=("arbitrary", "arbitrary")),
        name="modulation",
    )(cond_t, w_mod, b_mod)


def _mla_keys(cb, kr, wkk_ref, wkv_ref, kg, rope=None):
    kk = _dot(cb, wkk_ref[...])
    keys = []
    for h in range(N_HEADS):
        k = _rms(kk[:, h * LANES:(h + 1) * LANES] + kr, kg, QK_A)
        if rope is not None:
            k = rope(k)
        keys.append(k.astype(BF16))
    return keys, _dot(cb, wkv_ref[...]).astype(BF16)


def _p0_kernel(x_ref, m_ref, ng_ref, win_ref, qag_ref, wq_ref, kvag_ref, wkk_ref, wkv_ref, qg_ref, kg_ref,
               naqg_ref, nakg_ref, wout_ref,
               xo_ref, ckv_ref, krope_ref, nak_ref, nav_ref, y_scr):
    nbs = x_ref.shape[0]
    x = x_ref[...].reshape(nbs * SEQ, D_MODEL)
    h, gate = _modulate(x, ng_ref[...], m_ref[0:1, :])
    hb = h.astype(BF16)
    lo = _lane_lo()
    hi = jnp.logical_not(lo)
    rows = [slice(bi * SEQ, (bi + 1) * SEQ) for bi in range(nbs)]

    qln = _rms(_dot_nt(hb, win_ref[E_QLAT:E_CKV, :]), qag_ref[...], Q_RANK).astype(BF16)
    q_all = _dot(qln, wq_ref[...])
    ckv_n = _rms(_dot_nt(hb, win_ref[E_CKV:E_KROPE, :]), kvag_ref[...], KV_RANK)
    kr = _dot_nt(hb, _rope_key_slab(win_ref))
    for bi, rs in enumerate(rows):
        ckv_ref[bi, 0] = ckv_n[rs]
        krope_ref[bi, 0] = kr[rs].T[NOPE_A:QK_A]
    keys, vals = _mla_keys(ckv_n.astype(BF16), kr, wkk_ref, wkv_ref, kg_ref[...])
    qg = qg_ref[...] * (QK_A ** -0.5 * LOG2E)

    ga = _dot_nt(hb, win_ref[E_GA:E_QB, :])
    zq = _dot_nt(hb, win_ref[E_QB:E_KB, :])
    zk = _dot_nt(hb, win_ref[E_KB:E_VB, :])
    zv = _dot_nt(hb, win_ref[E_VB:E_GB, :])
    gb = _dot_nt(hb, win_ref[E_GB:E_END, :])
    naqg = naqg_ref[...] * (HEAD_DIM ** -0.5 * LOG2E)

    for p in range(N_PAIRS):
        sl = slice(p * LANES, (p + 1) * LANES)
        ys = slice(4 * LANES + p * LANES, 4 * LANES + (p + 1) * LANES)
        qhs = [_rms(q_all[:, hh * LANES:(hh + 1) * LANES], qg, QK_A).astype(BF16) for hh in (2 * p, 2 * p + 1)]
        qb = _rms_halves(zq[:, sl], naqg, lo)
        kb = _rms_halves(zk[:, sl], nakg_ref[...], lo)
        vb = zv[:, sl]
        kbb, vbb = kb.astype(BF16), _with_ones(vb.astype(BF16))
        va = _with_ones(vals[:, sl])
        qms = [jnp.where(keep, qb, 0.0).astype(BF16) for keep in (lo, hi)]
        for bi, rs in enumerate(rows):
            o2 = [_attend([(_dot_nt(qhs[i][rs], keys[2 * p + i][rs]), va[rs], False)]) for i in (0, 1)]
            y_scr[rs, sl] = (jnp.where(lo, o2[0], o2[1]) * _silu(ga[rs, sl])).astype(BF16)
            _store_pair_transposed(nak_ref, bi, p, kb[rs])
            _store_pair_transposed(nav_ref, bi, p, vb[rs])
            o2 = [_attend([(_dot_nt(qms[i][rs], kbb[rs]), vbb[rs], False)]) for i in (0, 1)]
            y_scr[rs, ys] = (jnp.where(lo, o2[0], o2[1]) * _silu(gb[rs, sl])).astype(BF16)

    xo_ref[...] = (x + gate * _dot(y_scr[...], wout_ref[...])).reshape(nbs, SEQ, D_MODEL)


def _full(shape):
    n = len(shape)
    return pl.BlockSpec(shape, lambda *_: (0,) * n, pipeline_mode=pl.Buffered(1))


class _Row(NamedTuple):
    table: jax.Array
    row: int


def _spec(a):
    if isinstance(a, _Row):
        idx = (a.row,) + (0,) * (a.table.ndim - 1)
        return pl.BlockSpec((None,) + a.table.shape[1:], lambda *_: idx, pipeline_mode=pl.Buffered(1))
    return _full(a.shape)


def _arr(a):
    return a.table if isinstance(a, _Row) else a


def _prompt_even(x, m, ng, win, qag, wq, kvag, wkk, wkv, qg, kg, naqg, nakg, wout):
    nb = x.shape[0]
    nbs = PROMPT_BATCHES_PER_STEP
    assert nb % nbs == 0
    ins = (m, ng, win, qag, wq, kvag, wkk, wkv, qg, kg, naqg, nakg, wout)
    return pl.pallas_call(
        _p0_kernel,
        grid=(nb // nbs,),
        in_specs=[pl.BlockSpec((nbs, SEQ, D_MODEL), lambda b: (b, 0, 0))] + [_spec(a) for a in ins],
        out_specs=[pl.BlockSpec((nbs, SEQ, D_MODEL), lambda b: (b, 0, 0)),
                   pl.BlockSpec((nbs, 1, SEQ, KV_RANK), lambda b: (b, 0, 0, 0)),
                   pl.BlockSpec((nbs, 1, ROPE_A, SEQ), lambda b: (b, 0, 0, 0)),
                   pl.BlockSpec((nbs, 1, N_HEADS, HEAD_DIM, SEQ), lambda b: (b, 0, 0, 0, 0)),
                   pl.BlockSpec((nbs, 1, N_HEADS, HEAD_DIM, SEQ), lambda b: (b, 0, 0, 0, 0))],
        out_shape=[jax.ShapeDtypeStruct((nb, SEQ, D_MODEL), F32),
                   jax.ShapeDtypeStruct((nb, 1, SEQ, KV_RANK), F32),
                   jax.ShapeDtypeStruct((nb, 1, ROPE_A, SEQ), F32),
                   jax.ShapeDtypeStruct((nb, 1, N_HEADS, HEAD_DIM, SEQ), F32),
                   jax.ShapeDtypeStruct((nb, 1, N_HEADS, HEAD_DIM, SEQ), F32)],
        scratch_shapes=[pltpu.VMEM((nbs * SEQ, D_MODEL), BF16)],
        compiler_params=pltpu.CompilerParams(dimension_semantics=("arbitrary",), vmem_limit_bytes=VMEM_LIMIT),
        name="prompt_even",
    )(x, *map(_arr, ins))


def _gqa_pair_operands(k, v, kg2, lo):
    kn = _rms_halves(k, kg2, lo)
    return kn, (kn.astype(BF16), pltpu.roll(kn, HEAD_DIM, 1).astype(BF16)), \
        (_with_ones(v.astype(BF16)), _with_ones(pltpu.roll(v, HEAD_DIM, 1).astype(BF16)))


def _p1_kernel(sink_ref, x_ref, m_ref, ng_ref, win_ref, gqg_ref, gkg_ref, sqg_ref, skg_ref, wout_ref,
               xo_ref, gk_ref, gv_ref, sk_ref, sv_ref, y_scr):
    nbs = x_ref.shape[0]
    x = x_ref[...].reshape(nbs * SEQ, D_MODEL)
    h, gate = _modulate(x, ng_ref[...], m_ref[0:1, :])
    hb = h.astype(BF16)
    lo = _lane_lo()
    hi = jnp.logical_not(lo)
    sc = HEAD_DIM ** -0.5 * LOG2E
    rows = [slice(bi * SEQ, (bi + 1) * SEQ) for bi in range(nbs)]

    branches = ((O_QC, O_KC, O_VC, O_GC, gqg_ref, gkg_ref, gk_ref, gv_ref, False, 0),
                (O_QD, O_KD, O_VD, O_GD, sqg_ref, skg_ref, sk_ref, sv_ref, True, 4 * LANES))
    for oq, ok, ov, og, qg_ref, kg_ref, ck_ref, cv_ref, has_sink, yoff in branches:
        zq = _dot(hb, win_ref[:, oq:oq + 4 * LANES])
        zkv = _dot(hb, win_ref[:, ok:ok + 2 * LANES])
        zg = _dot(hb, win_ref[:, og:og + 4 * LANES])
        v = zkv[:, LANES:]
        kn, ks, vs = _gqa_pair_operands(zkv[:, :LANES], v, kg_ref[...], lo)
        for bi, rs in enumerate(rows):
            _store_pair_transposed(ck_ref, bi, 0, kn[rs])
            _store_pair_transposed(cv_ref, bi, 0, v[rs])
        qg = qg_ref[...] * sc
        for p in range(N_PAIRS):
            sl = slice(p * LANES, (p + 1) * LANES)
            qn = _rms_halves(zq[:, sl], qg, lo)
            qms = [jnp.where(keep, qn, 0.0).astype(BF16) for keep in (lo, hi)]
            kv = p // 2
            for bi, rs in enumerate(rows):
                o2 = []
                for half in (0, 1):
                    swap = 0 if kv == half else 1
                    sink = sink_ref[2 * p + half] * LOG2E if has_sink else None
                    o2.append(_attend([(_dot_nt(qms[half][rs], ks[swap][rs]), vs[swap][rs], False)], sink))
                o = jnp.where(lo, o2[0], o2[1])
                y_scr[rs, yoff + p * LANES:yoff + (p + 1) * LANES] = (o * _silu(zg[rs, sl])).astype(BF16)

    xo_ref[...] = (x + gate * _dot(y_scr[...], wout_ref[...])).reshape(nbs, SEQ, D_MODEL)


def _prompt_odd(sink, x, m, ng, win, gqg, gkg, sqg, skg, wout):
    nb = x.shape[0]
    nbs = PROMPT_BATCHES_PER_STEP
    assert nb % nbs == 0
    ins = (m, ng, win, gqg, gkg, sqg, skg, wout)
    cache_spec = pl.BlockSpec((nbs, 1, 2, HEAD_DIM, SEQ), lambda b: (b, 0, 0, 0, 0))
    cache_shape = jax.ShapeDtypeStruct((nb, 1, 2, HEAD_DIM, SEQ), F32)
    return pl.pallas_call(
        _p1_kernel,
        grid=(nb // nbs,),
        in_specs=[pl.BlockSpec(memory_space=pltpu.SMEM),
                  pl.BlockSpec((nbs, SEQ, D_MODEL), lambda b: (b, 0, 0))] + [_spec(a) for a in ins],
        out_specs=[pl.BlockSpec((nbs, SEQ, D_MODEL), lambda b: (b, 0, 0))] + [cache_spec] * 4,
        out_shape=[jax.ShapeDtypeStruct((nb, SEQ, D_MODEL), F32)] + [cache_shape] * 4,
        scratch_shapes=[pltpu.VMEM((nbs * SEQ, D_MODEL), BF16)],
        compiler_params=pltpu.CompilerParams(dimension_semantics=("arbitrary",), vmem_limit_bytes=VMEM_LIMIT),
        name="prompt_odd",
    )(sink, x, *map(_arr, ins))


def _s0a_kernel(x_ref, m_ref, ng_ref, win_ref, qag_ref, wq_ref, kvag_ref, wkk_ref, wkv_ref, qg_ref, kg_ref,
                naqg_ref, nakg_ref, cos_ref, sin_ref,
                qa_ref, ka_ref, va_ref, qb_ref, kb_ref, vb_ref, g_ref):
    b = pl.program_id(0)
    lo = _lane_lo()
    ones_all = _group_ones2(LANES)
    partner = _rope_matrix2(ROPE_A, LANES, NOPE_A)
    hb = _modulate(x_ref[0], ng_ref[...], m_ref[pl.ds(1 + b, 1), :])[0].astype(BF16)
    cos, sin = cos_ref[...], sin_ref[...]

    qln = _rms(_dot_nt(hb, win_ref[E_QLAT:E_CKV, :]), qag_ref[...], Q_RANK).astype(BF16)
    q_all = _dot(qln, wq_ref[...])
    ckv_n = _rms(_dot_nt(hb, win_ref[E_CKV:E_KROPE, :]), kvag_ref[...], KV_RANK)
    kr = _dot_nt(hb, _rope_key_slab(win_ref))
    cb = ckv_n.astype(BF16)
    kk = _dot(cb, wkk_ref[...])
    va_ref[0] = _dot(cb, wkv_ref[...]).astype(BF16)
    zq = _dot_nt(hb, win_ref[E_QB:E_KB, :])
    zk = _dot_nt(hb, win_ref[E_KB:E_VB, :])
    vb_ref[0] = _dot_nt(hb, win_ref[E_VB:E_GB, :]).astype(BF16)
    g_ref[0, :, 0:4 * LANES] = _silu(_dot_nt(hb, win_ref[E_GA:E_QB, :]))
    g_ref[0, :, 4 * LANES:8 * LANES] = _silu(_dot_nt(hb, win_ref[E_GB:E_END, :]))

    qg = qg_ref[...] * (QK_A ** -0.5 * LOG2E)
    kg = kg_ref[...]
    k_partner = _lane_mix(kr * kg, partner) * sin
    for hh in range(N_HEADS):
        sl = slice(hh * LANES, (hh + 1) * LANES)
        qn = _rms_mxu(q_all[:, sl], qg, QK_A, ones_all)
        qa_ref[0, :, sl] = (qn * cos + _lane_mix(qn, partner) * sin).astype(BF16)
        k_raw = kk[:, sl] + kr
        k_inv = lax.rsqrt(_lane_mix(k_raw * k_raw, ones_all) / QK_A + EPS)
        ka_ref[0, :, sl] = ((k_raw * kg * cos + k_partner) * k_inv).astype(BF16)
    naqg = naqg_ref[...] * (HEAD_DIM ** -0.5 * LOG2E)
    for p in range(N_PAIRS):
        sl = slice(p * LANES, (p + 1) * LANES)
        qb_ref[0, :, sl] = _rms_halves(zq[:, sl], naqg, lo).astype(BF16)
        kb_ref[0, :, sl] = _rms_halves(zk[:, sl], nakg_ref[...], lo).astype(BF16)


def _sample_even_proj(x, m, ng, win, qag, wq, kvag, wkk, wkv, qg, kg, naqg, nakg, cos, sin):
    nb, s, _ = x.shape
    nq = s // PROJ_BLOCK
    ins = (m, ng, win, qag, wq, kvag, wkk, wkv, qg, kg, naqg, nakg)
    tab = pl.BlockSpec((PROJ_BLOCK, LANES), lambda b, j: (j, 0))

    def blk(w):
        return pl.BlockSpec((1, PROJ_BLOCK, w), lambda b, j: (b, j, 0))

    def shp(w, dt):
        return jax.ShapeDtypeStruct((nb, s, w), dt)

    return pl.pallas_call(
        _s0a_kernel,
        grid=(nb, nq),
        in_specs=[blk(D_MODEL)] + [_spec(a) for a in ins] + [tab, tab],
        out_specs=[blk(1024), blk(1024), blk(512), blk(512), blk(512), blk(512), blk(1024)],
        out_shape=[shp(1024, BF16), shp(1024, BF16), shp(512, BF16), shp(512, BF16), shp(512, BF16),
                   shp(512, BF16), shp(1024, F32)],
        compiler_params=pltpu.CompilerParams(dimension_semantics=("arbitrary", "arbitrary"),
                                             vmem_limit_bytes=VMEM_LIMIT),
        name="sample_even_proj",
    )(x, *map(_arr, ins), cos, sin)


def _build_bias_table(rpb_ref, tile_scr, tab_ref):
    qc = lax.broadcasted_iota(jnp.int32, (GRID_W, LANES), 0)
    lane = lax.broadcasted_iota(jnp.int32, (GRID_W, LANES), 1)
    kc = jnp.bitwise_and(lane, GRID_W - 1)
    lo = lane < GRID_W
    diff = kc - qc + (NA_COLS - 1)
    cs = jnp.clip(qc - NA_COLS // 2, 0, GRID_W - NA_COLS)
    valid = (kc >= cs) & (kc < cs + NA_COLS)
    tab_ref[...] = jnp.zeros(tab_ref.shape, F32)
    tile_scr[RPB_ROWS] = jnp.zeros((GRID_W, LANES), F32)

    def per_head(h, carry):
        for dr in range(RPB_ROWS):
            t = jnp.zeros((GRID_W, LANES), F32)
            for dc in range(RPB_COLS):
                t = jnp.where(diff == dc, rpb_ref[(h * RPB_ROWS + dr) * RPB_COLS + dc], t)
            tile_scr[dr] = jnp.where(valid, t * LOG2E, NEG_INF)
        for c in range(NA_ROWS // 2, NA_ROWS // 2 + NA_ROWS):
            d0 = 2 * c - NA_ROWS
            tab_ref[0, h, c] = jnp.where(lo, tile_scr[d0], tile_scr[d0 + 1])
            tab_ref[1, h, c] = jnp.where(lo, tile_scr[d0 - 1 if d0 > 0 else RPB_ROWS], tile_scr[d0])
        return carry

    lax.fori_loop(0, N_HEADS, per_head, 0)


def _s0b_kernel(rpb_ref, x_ref, m_ref, qa_ref, ka_ref, va_ref, qb_ref, kb_ref, vb_ref, g_ref,
                cckv_ref, ckr_ref, cnk_ref, cnv_ref, wkk_ref, wkv_ref, kg_ref, wout_ref,
                xo_ref, kca_scr, vca_scr, tile_scr, tab_scr, y_scr):
    b = pl.program_id(0)
    j = pl.program_id(1)
    lo = _lane_lo()
    n_lat = ka_ref.shape[1]

    @pl.when((b == 0) & (j == 0))
    def _():
        _build_bias_table(rpb_ref, tile_scr, tab_scr)

    @pl.when(j == 0)
    def _():
        kr_t = jnp.concatenate([jnp.zeros((NOPE_A, PAST_LEN), F32), ckr_ref[0],
                                jnp.zeros((LANES - QK_A, PAST_LEN), F32)], axis=0)
        keys, vals = _mla_keys(cckv_ref[0].astype(BF16), kr_t.T, wkk_ref, wkv_ref, kg_ref[...])
        for hh in range(N_HEADS):
            kca_scr[:, hh * LANES:(hh + 1) * LANES] = keys[hh]
        vca_scr[...] = vals

    kidx = lax.broadcasted_iota(jnp.int32, (1, n_lat), 1)
    for p in range(N_PAIRS):
        sl = slice(p * LANES, (p + 1) * LANES)
        o2 = []
        va = _with_ones(va_ref[0, :, sl])
        vca = _with_ones(vca_scr[:, sl])
        for hh in (2 * p, 2 * p + 1):
            hs = slice(hh * LANES, (hh + 1) * LANES)
            q = qa_ref[0, :, hs]
            o2.append(_attend([(_dot_nt(q, ka_ref[0, :, hs]), va, False),
                               (_dot_nt(q, kca_scr[:, hs]), vca, False)]))
        oa = jnp.where(lo, o2[0], o2[1])
        y_scr[:, sl] = (oa * g_ref[0, :, sl]).astype(BF16)

        qb = qb_ref[0, :, sl]
        kb = kb_ref[0, :, sl]
        vb = _with_ones(vb_ref[0, :, sl])
        kcb = cnk_ref[0, sl, :].astype(BF16)
        vcb = _with_ones(cnv_ref[0, sl, :].astype(BF16), transposed=True)
        o2 = []
        for half in (0, 1):
            head = 2 * p + half
            qm = jnp.where(lo if half == 0 else jnp.logical_not(lo), qb, jnp.zeros_like(qb))
            s_lat = _dot_nt(qm, kb)
            rows = []
            for local in range(Q_BLOCK // GRID_W):
                qr = j * (Q_BLOCK // GRID_W) + local
                par = 0 if local % 2 == 1 else 1
                c0 = (RPB_ROWS + par - local) // 2 - (Q_BLOCK // GRID_W // 2) * j
                bias = jnp.concatenate([tab_scr[par, head, c0 + t] for t in range(n_lat // LANES)], axis=1)
                r0 = jnp.clip(qr - NA_ROWS // 2, 0, n_lat // GRID_W - NA_ROWS) * GRID_W
                ok = (kidx >= r0) & (kidx < r0 + NA_ROWS * GRID_W)
                rows.append(jnp.where(ok, s_lat[local * GRID_W:(local + 1) * GRID_W] + bias, NEG_INF))
            s_lat = jnp.concatenate(rows, axis=0)
            o2.append(_attend([(s_lat, vb, False), (_dot(qm, kcb), vcb, True)]))
        ob = jnp.where(lo, o2[0], o2[1])
        ys = slice(4 * LANES + p * LANES, 4 * LANES + (p + 1) * LANES)
        y_scr[:, ys] = (ob * g_ref[0, :, ys]).astype(BF16)

    d = x_ref.shape[-1]
    gate = m_ref[pl.ds(1 + b, 1), 2 * d:]
    xo_ref[0] = x_ref[0] + gate * _dot(y_scr[...], wout_ref[...])


def _sample_even_attn(rpb, x, m, qa, ka, va, qb, kb, vb, g, cckv, ckr, cnk, cnv, wkk, wkv, kg, wout):
    nb, s, _ = x.shape
    nq = s // Q_BLOCK

    def blk(w):
        return pl.BlockSpec((1, Q_BLOCK, w), lambda b, j: (b, j, 0))

    def per_batch(a):
        return pl.BlockSpec((1,) + a.shape[1:], lambda b, j: (b, 0, 0))

    return pl.pallas_call(
        _s0b_kernel,
        grid=(nb, nq),
        in_specs=[pl.BlockSpec(memory_space=pltpu.SMEM), blk(D_MODEL), _spec(m),
                  blk(1024), per_batch(ka), per_batch(va), blk(512), per_batch(kb), per_batch(vb), blk(1024),
                  per_batch(cckv), per_batch(ckr), per_batch(cnk), per_batch(cnv),
                  _full(wkk.shape), _full(wkv.shape), _spec(kg), _full(wout.shape)],
        out_specs=blk(D_MODEL),
        out_shape=jax.ShapeDtypeStruct(x.shape, F32),
        scratch_shapes=[pltpu.VMEM((PAST_LEN, N_HEADS * LANES), BF16),
                        pltpu.VMEM((PAST_LEN, N_HEADS * HEAD_DIM), BF16),
                        pltpu.VMEM((RPB_ROWS + 1, GRID_W, LANES), F32),
                        pltpu.VMEM((2, N_HEADS, BIAS_CHUNKS, GRID_W, LANES), F32),
                        pltpu.VMEM((Q_BLOCK, D_MODEL), BF16)],
        compiler_params=pltpu.CompilerParams(dimension_semantics=("arbitrary", "arbitrary"),
                                             vmem_limit_bytes=VMEM_LIMIT),
        name="sample_even_attn",
    )(rpb, x, _arr(m), qa, ka, va, qb, kb, vb, g, cckv, ckr, cnk, cnv, wkk, wkv, _arr(kg), wout)


def _s1a_kernel(x_ref, m_ref, ng_ref, win_ref, gqg_ref, gkg_ref, sqg_ref, skg_ref, cos_ref, sin_ref,
                qc_ref, kc_ref, vc_ref, qd_ref, kd_ref, vd_ref, g_ref):
    b = pl.program_id(0)
    lo = _lane_lo()
    partner = _rope_matrix2(HEAD_DIM, HEAD_DIM, 0)
    swap = _swap_matrix2()[:LANES]
    hb = _modulate(x_ref[0], ng_ref[...], m_ref[pl.ds(1 + b, 1), :])[0].astype(BF16)
    cos, sin = cos_ref[...], sin_ref[...]
    sc = HEAD_DIM ** -0.5 * LOG2E

    def rope(t):
        return t * cos + _lane_mix(t, partner) * sin

    branches = ((O_QC, O_KC, O_GC, gqg_ref, gkg_ref, qc_ref, kc_ref, vc_ref, 0),
                (O_QD, O_KD, O_GD, sqg_ref, skg_ref, qd_ref, kd_ref, vd_ref, 4 * LANES))
    for oq, ok, og, qg_ref, kg_ref, q_out, k_out, v_out, goff in branches:
        zq = _dot(hb, win_ref[:, oq:oq + 4 * LANES])
        zkv = _dot(hb, win_ref[:, ok:ok + 2 * LANES])
        qg = qg_ref[...] * sc
        for p in range(N_PAIRS):
            sl = slice(p * LANES, (p + 1) * LANES)
            q_out[0, :, sl] = rope(_rms_halves(zq[:, sl], qg, lo)).astype(BF16)
        kn = rope(_rms_halves(zkv[:, :LANES], kg_ref[...], lo))
        v = zkv[:, LANES:]
        for out, val in ((k_out, kn.astype(BF16)), (v_out, v.astype(BF16))):
            out[0, :, 0:LANES] = val
            out[0, :, LANES:2 * LANES] = _dot(val, swap).astype(BF16)
        g_ref[0, :, goff:goff + 4 * LANES] = _silu(_dot(hb, win_ref[:, og:og + 4 * LANES]))


def _sample_odd_proj(x, m, ng, win, gqg, gkg, sqg, skg, cos, sin):
    nb, s, _ = x.shape
    nq = s // PROJ_BLOCK
    ins = (m, ng, win, gqg, gkg, sqg, skg)
    tab = pl.BlockSpec((PROJ_BLOCK, LANES), lambda b, j: (j, 0))

    def blk(w):
        return pl.BlockSpec((1, PROJ_BLOCK, w), lambda b, j: (b, j, 0))

    def shp(w, dt):
        return jax.ShapeDtypeStruct((nb, s, w), dt)

    return pl.pallas_call(
        _s1a_kernel,
        grid=(nb, nq),
        in_specs=[blk(D_MODEL)] + [_spec(a) for a in ins] + [tab, tab],
        out_specs=[blk(512), blk(256), blk(256), blk(512), blk(256), blk(256), blk(1024)],
        out_shape=[shp(512, BF16), shp(256, BF16), shp(256, BF16), shp(512, BF16), shp(256, BF16),
                   shp(256, BF16), shp(1024, F32)],
        compiler_params=pltpu.CompilerParams(dimension_semantics=("arbitrary", "arbitrary"),
                                             vmem_limit_bytes=VMEM_LIMIT),
        name="sample_odd_proj",
    )(x, *map(_arr, ins), cos, sin)


def _s1b_kernel(sink_ref, x_ref, m_ref, qc_ref, kc_ref, vc_ref, qd_ref, kd_ref, vd_ref, g_ref,
                cgk_ref, cgv_ref, csk_ref, csv_ref, wout_ref, xo_ref, y_scr):
    b = pl.program_id(0)
    j = pl.program_id(1)
    lo = _lane_lo()
    n_lat = kc_ref.shape[1]
    win_keys = Q_BLOCK + 2 * SWA_HALF

    def ctx_pair(ref, values=False):
        a = ref[0].astype(BF16)
        pair = (a, _swap_halves(a))
        return tuple(_with_ones(t, transposed=True) for t in pair) if values else pair

    cgk, cgv, csk, csv = ctx_pair(cgk_ref), ctx_pair(cgv_ref, True), ctx_pair(csk_ref), ctx_pair(csv_ref, True)
    vcs = [_with_ones(vc_ref[0, :, w * LANES:(w + 1) * LANES]) for w in (0, 1)]

    ks = pl.multiple_of(jnp.clip(j * Q_BLOCK - SWA_HALF, 0, n_lat - win_keys), SWA_HALF)
    qpos = j * Q_BLOCK + lax.broadcasted_iota(jnp.int32, (Q_BLOCK, win_keys), 0)
    kpos = ks + lax.broadcasted_iota(jnp.int32, (Q_BLOCK, win_keys), 1)
    band = jnp.abs(qpos - kpos) <= SWA_HALF
    vds = [_with_ones(vd_ref[0, pl.ds(ks, win_keys), w * LANES:(w + 1) * LANES]) for w in (0, 1)]

    for p in range(N_PAIRS):
        sl = slice(p * LANES, (p + 1) * LANES)
        kv = p // 2
        qc = qc_ref[0, :, sl]
        qd = qd_ref[0, :, sl]
        oc2, od2 = [], []
        for half in (0, 1):
            swap = 0 if kv == half else 1
            ws = slice(swap * LANES, (swap + 1) * LANES)
            keep = lo if half == 0 else jnp.logical_not(lo)
            qm = jnp.where(keep, qc, jnp.zeros_like(qc))
            oc2.append(_attend([(_dot_nt(qm, kc_ref[0, :, ws]), vcs[swap], False),
                                (_dot(qm, cgk[swap]), cgv[swap], True)]))
            qm = jnp.where(keep, qd, jnp.zeros_like(qd))
            s_loc = jnp.where(band, _dot_nt(qm, kd_ref[0, pl.ds(ks, win_keys), ws]), NEG_INF)
            od2.append(_attend([(s_loc, vds[swap], False),
                                (_dot(qm, csk[swap]), csv[swap], True)], sink_ref[2 * p + half] * LOG2E))
        y_scr[:, sl] = (jnp.where(lo, oc2[0], oc2[1]) * g_ref[0, :, sl]).astype(BF16)
        ys = slice(4 * LANES + p * LANES, 4 * LANES + (p + 1) * LANES)
        y_scr[:, ys] = (jnp.where(lo, od2[0], od2[1]) * g_ref[0, :, ys]).astype(BF16)

    d = x_ref.shape[-1]
    gate = m_ref[pl.ds(1 + b, 1), 2 * d:]
    xo_ref[0] = x_ref[0] + gate * _dot(y_scr[...], wout_ref[...])


def _sample_odd_attn(sink, x, m, qc, kc, vc, qd, kd, vd, g, cgk, cgv, csk, csv, wout):
    nb, s, _ = x.shape
    nq = s // Q_BLOCK

    def blk(w):
        return pl.BlockSpec((1, Q_BLOCK, w), lambda b, j: (b, j, 0))

    def per_batch(a):
        return pl.BlockSpec((1,) + a.shape[1:], lambda b, j: (b, 0, 0))

    return pl.pallas_call(
        _s1b_kernel,
        grid=(nb, nq),
        in_specs=[pl.BlockSpec(memory_space=pltpu.SMEM), blk(D_MODEL), _spec(m),
                  blk(512), per_batch(kc), per_batch(vc), blk(512), per_batch(kd), per_batch(vd), blk(1024),
                  per_batch(cgk), per_batch(cgv), per_batch(csk), per_batch(csv), _full(wout.shape)],
        out_specs=blk(D_MODEL),
        out_shape=jax.ShapeDtypeStruct(x.shape, F32),
        scratch_shapes=[pltpu.VMEM((Q_BLOCK, D_MODEL), BF16)],
        compiler_params=pltpu.CompilerParams(dimension_semantics=("arbitrary", "arbitrary"),
                                             vmem_limit_bytes=VMEM_LIMIT),
        name="sample_odd_attn",
    )(sink, x, _arr(m), qc, kc, vc, qd, kd, vd, g, cgk, cgv, csk, csv, wout)


WEIGHT_PREP_STEPS = 8


G_MLA_Q, G_MLA_K, G_NA_Q, G_NA_K, G_GQA_Q, G_GQA_K, G_SWA_Q, G_SWA_K, N_GAINS = range(9)


def _weight_prep_kernel(wie_ref, wio_ref, woe_ref, woo_ref, wqu_ref, wkv_ref, ng_ref, *refs):
    gain_refs = refs[:N_GAINS]
    win_e_ref, win_o_ref, wout_e_ref, wout_o_ref, wq_ref, wkk_ref, wkvv_ref, gt_ref, ngt_ref = refs[N_GAINS:]
    gt_ref[...] = jnp.zeros(gt_ref.shape, F32)
    for r, g_ref in enumerate(gain_refs):
        w = g_ref.shape[1]
        for off in range(0, LANES - w + 1, w):
            gt_ref[r, :, off:off + w] = g_ref[...]
    for layer in range(ngt_ref.shape[0]):
        ngt_ref[layer] = ng_ref[layer:layer + 1, :]

    win_e_ref[...] = wie_ref[...].astype(BF16)
    win_o_ref[...] = wio_ref[0].astype(BF16)
    wout_e_ref[...] = woe_ref[0].astype(BF16)
    wout_o_ref[...] = woo_ref[0].astype(BF16)

    wq_ref[...] = jnp.zeros(wq_ref.shape, BF16)
    for h in range(N_HEADS):
        wq_ref[:, h * LANES:h * LANES + QK_A] = wqu_ref[0, :, h * QK_A:(h + 1) * QK_A].astype(BF16)
    lo = _lane_lo()
    for p in range(N_PAIRS):
        a = wkv_ref[0, :, (2 * p) * LANES:(2 * p + 1) * LANES]
        c = wkv_ref[0, :, (2 * p + 1) * LANES:(2 * p + 2) * LANES]
        wkk_ref[:, (2 * p) * LANES:(2 * p + 1) * LANES] = jnp.where(lo, a, 0.0).astype(BF16)
        wkk_ref[:, (2 * p + 1) * LANES:(2 * p + 2) * LANES] = jnp.where(lo, c, 0.0).astype(BF16)
        wkvv_ref[:, p * LANES:(p + 1) * LANES] = jnp.where(lo, pltpu.roll(a, HEAD_DIM, 1), c).astype(BF16)


def _weight_prep(w_in_even_t, w_in_odd, w_out_even, w_out_odd, w_q_up, w_kv_up, norm_g, gains):
    n = WEIGHT_PREP_STEPS
    assert len(gains) == N_GAINS
    small = (norm_g,) + tuple(gains)
    ins = (w_in_odd, w_out_even, w_out_odd, w_q_up, w_kv_up)
    out_cols = (O_END, D_MODEL, D_MODEL, N_HEADS * LANES, N_HEADS * LANES, N_HEADS * HEAD_DIM)
    out_rows = (D_MODEL, D_MODEL, D_MODEL, Q_RANK, KV_RANK, KV_RANK)
    te = E_END // 6
    assert te * 6 == E_END and te % 16 == 0
    even_spec = pl.BlockSpec((te, D_MODEL), lambda i: (jnp.minimum(i, 5), 0))
    return pl.pallas_call(
        _weight_prep_kernel,
        grid=(n,),
        in_specs=[even_spec] + [pl.BlockSpec((1, a.shape[1] // n, a.shape[2]), lambda i: (0, i, 0)) for a in ins]
        + [_full(a.shape) for a in small],
        out_specs=[even_spec] + [pl.BlockSpec((r // n, c), lambda i: (i, 0)) for r, c in zip(out_rows, out_cols)]
        + [_full((N_GAINS, 1, LANES)), _full((norm_g.shape[0], 1, D_MODEL))],
        out_shape=[jax.ShapeDtypeStruct((E_END, D_MODEL), BF16)]
        + [jax.ShapeDtypeStruct((r, c), BF16) for r, c in zip(out_rows, out_cols)]
        + [jax.ShapeDtypeStruct((N_GAINS, 1, LANES), F32), jax.ShapeDtypeStruct((norm_g.shape[0], 1, D_MODEL), F32)],
        compiler_params=pltpu.CompilerParams(dimension_semantics=("arbitrary",), vmem_limit_bytes=VMEM_LIMIT),
        name="weight_prep",
    )(w_in_even_t, *ins, *small)


def _feature_major(c):
    b, h, l, d = c.shape
    return jnp.swapaxes(c, -1, -2).reshape(b, h * d, l)


def _token_major(c):
    return jnp.swapaxes(c, -1, -2)


def _rope_tables(s, rot_dim, period, start):
    quarter = rot_dim // 4
    t = np.arange(s)
    inv = ROPE_THETA ** (-np.arange(quarter, dtype=np.float64) / quarter)
    row = (t // GRID_W).astype(np.float64)[:, None] * inv
    col = (t % GRID_W).astype(np.float64)[:, None] * inv
    ang = np.concatenate([row, col], axis=-1)
    cos, sin = np.cos(ang), np.sin(ang)
    pre = np.ones((s, start))
    post = np.zeros((s, period - start - rot_dim))
    c = np.concatenate([pre, cos, cos, post], axis=-1)
    sn = np.concatenate([0 * pre, sin, sin, post], axis=-1)
    rep = LANES // period
    return jnp.asarray(np.tile(c, (1, rep)), F32), jnp.asarray(np.tile(sn, (1, rep)), F32)


def kernel(x_prompt, x_sample, cache_mla_ckv, cache_mla_krope, cache_na_k, cache_na_v, cache_gqa_k, cache_gqa_v, cache_swa_k, cache_swa_v, c, c_ctx, norm_g, w_mod, b_mod, w_in_even, mla_qa_g, w_q_up, mla_kva_g, w_kv_up, mla_q_g, mla_k_g, na_q_g, na_k_g, na_rpb, w_out_even, w_in_odd, gqa_q_g, gqa_k_g, swa_q_g, swa_k_g, swa_sink, w_out_odd):
    n_dec = x_sample.shape[0]
    assert w_mod.shape[0] == 2 and n_dec + 1 <= 8

    cond_t = jnp.concatenate([c_ctx[:, None], c.T, jnp.zeros((D_MODEL, 7 - n_dec), F32)], axis=1)
    m_all = _modulation(cond_t, 1 + n_dec, w_mod, b_mod)

    gains = (mla_q_g, mla_k_g, na_q_g, na_k_g, gqa_q_g, gqa_k_g, swa_q_g, swa_k_g)
    win_e, win_o, wout_e, wout_o, wq, wkk, wkv, gt, ngt = _weight_prep(
        jnp.swapaxes(w_in_even[0], 0, 1), w_in_odd, w_out_even, w_out_odd, w_q_up, w_kv_up, norm_g, gains)
    m_even, m_odd = _Row(m_all, 0), _Row(m_all, 1)
    even = (_Row(ngt, 0), win_e, mla_qa_g, wq, mla_kva_g, wkk, wkv,
            _Row(gt, G_MLA_Q), _Row(gt, G_MLA_K), _Row(gt, G_NA_Q), _Row(gt, G_NA_K))
    odd = (_Row(ngt, 1), win_o, _Row(gt, G_GQA_Q), _Row(gt, G_GQA_K), _Row(gt, G_SWA_Q), _Row(gt, G_SWA_K))
    sink = swa_sink[0].astype(F32)

    xp1, new_ckv, new_krope, new_na_k, new_na_v = _prompt_even(x_prompt, m_even, *even, wout_e)
    xp2, new_gqa_k, new_gqa_v, new_swa_k, new_swa_v = _prompt_odd(sink, xp1, m_odd, *odd, wout_o)

    cos_e, sin_e = _rope_tables(DEC_SEQ, ROPE_A, LANES, NOPE_A)
    qa, ka, va, qbs, kbs, vbs, g0 = _sample_even_proj(x_sample, m_even, *even, cos_e, sin_e)
    ckr = jnp.swapaxes(cache_mla_krope[:, 0], -1, -2)
    xs1 = _sample_even_attn(na_rpb[0].reshape(-1), x_sample, m_even, qa, ka, va, qbs, kbs, vbs, g0,
                            cache_mla_ckv[:, 0], ckr, _feature_major(cache_na_k[:, 0]),
                            _feature_major(cache_na_v[:, 0]), wkk, wkv, _Row(gt, G_MLA_K), wout_e)
    cos_o, sin_o = _rope_tables(DEC_SEQ, HEAD_DIM, HEAD_DIM, 0)
    qc, kc, vc, qd, kd, vd, g1 = _sample_odd_proj(xs1, m_odd, *odd, cos_o, sin_o)
    xs2 = _sample_odd_attn(sink, xs1, m_odd, qc, kc, vc, qd, kd, vd, g1,
                           _feature_major(cache_gqa_k[:, 0]), _feature_major(cache_gqa_v[:, 0]),
                           _feature_major(cache_swa_k[:, 0]), _feature_major(cache_swa_v[:, 0]), wout_o)

    caches = (new_krope, new_na_k, new_na_v, new_gqa_k, new_gqa_v, new_swa_k, new_swa_v)
    return (xp2, xs2, new_ckv) + tuple(_token_major(c) for c in caches)
```

```python
import functools
from typing import NamedTuple

import jax
import jax.numpy as jnp
import numpy as np
from jax import lax
from jax.experimental import pallas as pl
from jax.experimental.pallas import tpu as pltpu

F32 = jnp.float32
BF16 = jnp.bfloat16

D_MODEL = 1024
SEQ = 256
DEC_SEQ = 1024
PAST_LEN = 256
GRID_W = 64
HEAD_DIM = 64
Q_RANK = 256
KV_RANK = 128
NOPE_A = 64
ROPE_A = 32
QK_A = NOPE_A + ROPE_A
N_HEADS = 8
NA_ROWS = 8
NA_COLS = 16
SWA_HALF = 128
ROPE_THETA = 10000.0
EPS = 1e-6
NEG_INF = -1e30
LOG2E = 1.4426950408889634

LANES = 128
Q_BLOCK = 512
NA_Q_BLOCK = 256
PROJ_BLOCK = 512
PROMPT_BATCHES_PER_STEP = 2
N_PAIRS = N_HEADS // 2
RPB_ROWS = 2 * NA_ROWS - 1
RPB_COLS = 2 * NA_COLS - 1
BIAS_CHUNKS = 16
VMEM_LIMIT = 48 * 1024 * 1024

E_QLAT, E_CKV, E_KROPE, E_GA, E_QB, E_KB, E_VB, E_GB, E_END = 0, 256, 384, 416, 928, 1440, 1952, 2464, 2976
O_QC, O_KC, O_VC, O_GC, O_QD, O_KD, O_VD, O_GD, O_END = 0, 512, 640, 768, 1280, 1792, 1920, 2048, 2560


def _dot(a, b):
    return lax.dot_general(a, b, (((1,), (0,)), ((), ())), preferred_element_type=F32)


def _dot_nt(a, b):
    return lax.dot_general(a, b, (((1,), (1,)), ((), ())), preferred_element_type=F32)


def _silu(x):
    return x / (1.0 + jnp.exp(-x))


def _rms(x, g, n):
    ss = jnp.sum(x * x, axis=-1, keepdims=True)
    return x * lax.rsqrt(ss / n + EPS) * g


def _rms_halves(x, g2, lo):
    x2 = x * x
    s_lo = jnp.sum(jnp.where(lo, x2, 0.0), axis=-1, keepdims=True)
    s_hi = jnp.sum(jnp.where(lo, 0.0, x2), axis=-1, keepdims=True)
    r = jnp.where(lo, lax.rsqrt(s_lo / HEAD_DIM + EPS), lax.rsqrt(s_hi / HEAD_DIM + EPS))
    return x * r * g2


def _modulate(x, g, m):
    d = x.shape[-1]
    xn = x * lax.rsqrt(jnp.mean(x * x, axis=-1, keepdims=True) + EPS) * g
    return xn * (1.0 + m[:, d:2 * d]) + m[:, :d], m[:, 2 * d:]


def _split_lanes(x):
    hi = x.astype(BF16)
    lo = (x - hi.astype(F32)).astype(BF16)
    return jnp.concatenate([hi, lo], axis=1)


def _lane_matrix2(entries):
    i = lax.broadcasted_iota(jnp.int32, (LANES, LANES), 0)
    j = lax.broadcasted_iota(jnp.int32, (LANES, LANES), 1)
    m = entries(i, j).astype(BF16)
    return jnp.concatenate([m, m], axis=0)


def _group_ones2(width):
    shift = width.bit_length() - 1
    return _lane_matrix2(lambda i, j: jnp.where((i >> shift) == (j >> shift), 1.0, 0.0))


def _rope_matrix2(rot_dim, period, start):
    half = rot_dim // 2

    def entries(i, j):
        pos = jnp.bitwise_and(j, period - 1) - start
        neg = (pos >= 0) & (pos < half) & (i == j + half)
        plus = (pos >= half) & (pos < rot_dim) & (i == j - half)
        return jnp.where(neg, -1.0, jnp.where(plus, 1.0, 0.0))

    return _lane_matrix2(entries)


def _swap_matrix2():
    return _lane_matrix2(lambda i, j: jnp.where(i == jnp.bitwise_xor(j, HEAD_DIM), 1.0, 0.0))


def _lane_mix(x, m2):
    return _dot(_split_lanes(x), m2)


def _rms_mxu(x, g, n, ones2):
    x2 = x * x
    nt = x.shape[-1] // LANES
    sq = x2[:, :LANES]
    for t in range(1, nt):
        sq = sq + x2[:, t * LANES:(t + 1) * LANES]
    r = lax.rsqrt(_lane_mix(sq, ones2) / n + EPS)
    return x * (r if nt == 1 else jnp.tile(r, (1, nt))) * g


def _with_ones(v, transposed=False):
    if transposed:
        return jnp.concatenate([v, jnp.ones((LANES, v.shape[1]), v.dtype)], axis=0)
    return jnp.concatenate([v, jnp.ones((v.shape[0], LANES), v.dtype)], axis=1)


def _attend(parts, sink=None):
    mx = None
    for s, _, _ in parts:
        pm = jnp.max(s, axis=-1, keepdims=True)
        mx = pm if mx is None else jnp.maximum(mx, pm)
    if sink is not None:
        mx = jnp.maximum(mx, sink)
    acc, den = None, None
    for s, v, v_t in parts:
        e = jnp.exp2(s - mx)
        po = (_dot_nt if v_t else _dot)(e.astype(BF16), v)
        acc = po if acc is None else acc + po
        if po.shape[1] == LANES:
            ps = jnp.sum(e, axis=-1, keepdims=True)
            den = ps if den is None else den + ps
    if den is None:
        den = acc[:, LANES:]
    if sink is not None:
        den = den + jnp.exp2(sink - mx)
    return acc[:, :LANES] * (1.0 / den)


def _lane_lo():
    return lax.broadcasted_iota(jnp.int32, (1, LANES), 1) < HEAD_DIM


def _store_pair_transposed(ref, bi, p, x):
    xt = x.T
    ref[bi, 0, 2 * p] = xt[:HEAD_DIM]
    ref[bi, 0, 2 * p + 1] = xt[HEAD_DIM:]


def _rope_key_slab(win_ref):
    d = win_ref.shape[1]
    return jnp.concatenate([jnp.zeros((NOPE_A, d), BF16), win_ref[E_KROPE:E_GA, :],
                            jnp.zeros((LANES - QK_A, d), BF16)], axis=0)


def _swap_halves(a):
    return jnp.concatenate([a[HEAD_DIM:], a[:HEAD_DIM]], axis=0)


def _mod_step(n_cond, is_first, bias_row, c_ref, w_ref, o_ref):
    @pl.when(is_first)
    def _():
        o_ref[:n_cond, :] = jnp.broadcast_to(bias_row, (n_cond, o_ref.shape[1]))
        o_ref[n_cond:, :] = jnp.zeros((o_ref.shape[0] - n_cond, o_ref.shape[1]), F32)

    s = _silu(c_ref[...])
    cols = [jnp.broadcast_to(s[:, r:r + 1], (s.shape[0], LANES)) for r in range(n_cond)]
    for t in range(w_ref.shape[1] // LANES):
        sl = slice(t * LANES, (t + 1) * LANES)
        w = w_ref[:, sl]
        for r in range(n_cond):
            o_ref[r:r + 1, sl] += jnp.sum(w * cols[r], axis=0, keepdims=True)


def _mod_kernel(n_cond, layer, c_ref, w_ref, b_ref, o_ref):
    _mod_step(n_cond, pl.program_id(0) == 0, b_ref[layer:layer + 1, :], c_ref, w_ref, o_ref)


def _modulation(cond_t, n_cond, w_mod, b_mod, layer):
    tk = 256
    return pl.pallas_call(
        functools.partial(_mod_kernel, n_cond, layer),
        grid=(D_MODEL // tk,),
        in_specs=[pl.BlockSpec((tk, 8), lambda k: (k, 0)),
                  pl.BlockSpec((None, tk, 3 * D_MODEL), lambda k: (layer, k, 0)),
                  pl.BlockSpec(b_mod.shape, lambda k: (0, 0))],
        out_specs=pl.BlockSpec((8, 3 * D_MODEL), lambda k: (0, 0)),
        out_shape=jax.ShapeDtypeStruct((8, 3 * D_MODEL), F32),
        compiler_params=pltpu.CompilerParams(dimension_semantics=("arbitrary",)),
        name="modulation",
    )(cond_t, w_mod, b_mod)


def _mla_keys(cb, kr, wkk_ref, wkv_ref, kg, rope=None):
    kk = _dot(cb, wkk_ref[...])
    keys = []
    for h in range(N_HEADS):
        k = _rms(kk[:, h * LANES:(h + 1) * LANES] + kr, kg, QK_A)
        if rope is not None:
            k = rope(k)
        keys.append(k.astype(BF16))
    return keys, _dot(cb, wkv_ref[...]).astype(BF16)


def _p0_kernel(n_cond, x_ref, m_ref, ng_ref, win_ref, qag_ref, wq_ref, kvag_ref, wkk_ref, wkv_ref, qg_ref, kg_ref,
               naqg_ref, nakg_ref, wout_ref, ct_ref, wm_ref, bm_ref, wio_ref, woo_ref,
               xo_ref, ckv_ref, krope_ref, nak_ref, nav_ref, mo_ref, wino_ref, wouto_ref, y_scr):
    _mod_step(n_cond, pl.program_id(0) == 0, bm_ref[1:2, :], ct_ref, wm_ref, mo_ref)
    wino_ref[...] = wio_ref[0].astype(BF16)
    wouto_ref[...] = woo_ref[0].astype(BF16)

    nbs = x_ref.shape[0]
    x = x_ref[...].reshape(nbs * SEQ, D_MODEL)
    h, gate = _modulate(x, ng_ref[...], m_ref[0:1, :])
    hb = h.astype(BF16)
    lo = _lane_lo()
    hi = jnp.logical_not(lo)
    rows = [slice(bi * SEQ, (bi + 1) * SEQ) for bi in range(nbs)]

    qln = _rms(_dot_nt(hb, win_ref[E_QLAT:E_CKV, :]), qag_ref[...], Q_RANK).astype(BF16)
    q_all = _dot(qln, wq_ref[...])
    ckv_n = _rms(_dot_nt(hb, win_ref[E_CKV:E_KROPE, :]), kvag_ref[...], KV_RANK)
    kr = _dot_nt(hb, _rope_key_slab(win_ref))
    for bi, rs in enumerate(rows):
        ckv_ref[bi, 0] = ckv_n[rs]
        krope_ref[bi, 0] = kr[rs].T[NOPE_A:QK_A]
    keys, vals = _mla_keys(ckv_n.astype(BF16), kr, wkk_ref, wkv_ref, kg_ref[...])
    qg = qg_ref[...] * (QK_A ** -0.5 * LOG2E)

    ga = _dot_nt(hb, win_ref[E_GA:E_QB, :])
    zq = _dot_nt(hb, win_ref[E_QB:E_KB, :])
    zk = _dot_nt(hb, win_ref[E_KB:E_VB, :])
    zv = _dot_nt(hb, win_ref[E_VB:E_GB, :])
    gb = _dot_nt(hb, win_ref[E_GB:E_END, :])
    naqg = naqg_ref[...] * (HEAD_DIM ** -0.5 * LOG2E)

    for p in range(N_PAIRS):
        sl = slice(p * LANES, (p + 1) * LANES)
        ys = slice(4 * LANES + p * LANES, 4 * LANES + (p + 1) * LANES)
        qhs = [_rms(q_all[:, hh * LANES:(hh + 1) * LANES], qg, QK_A).astype(BF16) for hh in (2 * p, 2 * p + 1)]
        qb = _rms_halves(zq[:, sl], naqg, lo)
        kb = _rms_halves(zk[:, sl], nakg_ref[...], lo)
        vb = zv[:, sl]
        kbb, vbb = kb.astype(BF16), vb.astype(BF16)
        va = vals[:, sl]
        qms = [jnp.where(keep, qb, 0.0).astype(BF16) for keep in (lo, hi)]
        for bi, rs in enumerate(rows):
            o2 = [_attend([(_dot_nt(qhs[i][rs], keys[2 * p + i][rs]), va[rs], False)]) for i in (0, 1)]
            y_scr[rs, sl] = (jnp.where(lo, o2[0], o2[1]) * _silu(ga[rs, sl])).astype(BF16)
            _store_pair_transposed(nak_ref, bi, p, kb[rs])
            _store_pair_transposed(nav_ref, bi, p, vb[rs])
            o2 = [_attend([(_dot_nt(qms[i][rs], kbb[rs]), vbb[rs], False)]) for i in (0, 1)]
            y_scr[rs, ys] = (jnp.where(lo, o2[0], o2[1]) * _silu(gb[rs, sl])).astype(BF16)

    xo_ref[...] = (x + gate * _dot(y_scr[...], wout_ref[...])).reshape(nbs, SEQ, D_MODEL)


def _full(shape):
    n = len(shape)
    return pl.BlockSpec(shape, lambda *_: (0,) * n, pipeline_mode=pl.Buffered(1))


class _Row(NamedTuple):
    table: jax.Array
    row: int


def _spec(a):
    if isinstance(a, _Row):
        idx = (a.row,) + (0,) * (a.table.ndim - 1)
        return pl.BlockSpec((None,) + a.table.shape[1:], lambda *_: idx, pipeline_mode=pl.Buffered(1))
    return _full(a.shape)


def _arr(a):
    return a.table if isinstance(a, _Row) else a


def _prompt_even(x, m, ng, win, qag, wq, kvag, wkk, wkv, qg, kg, naqg, nakg, wout,
                 cond_t, n_cond, w_mod, b_mod, w_in_odd, w_out_odd):
    nb = x.shape[0]
    nbs = PROMPT_BATCHES_PER_STEP
    steps = nb // nbs
    assert nb % nbs == 0 and D_MODEL % (16 * steps) == 0
    tr = D_MODEL // steps
    ins = (m, ng, win, qag, wq, kvag, wkk, wkv, qg, kg, naqg, nakg, wout)
    return pl.pallas_call(
        functools.partial(_p0_kernel, n_cond),
        grid=(steps,),
        in_specs=[pl.BlockSpec((nbs, SEQ, D_MODEL), lambda b: (b, 0, 0))] + [_spec(a) for a in ins]
        + [pl.BlockSpec((tr, 8), lambda b: (b, 0)),
           pl.BlockSpec((None, tr, 3 * D_MODEL), lambda b: (1, b, 0)),
           _full(b_mod.shape),
           pl.BlockSpec((1, tr, O_END), lambda b: (0, b, 0)),
           pl.BlockSpec((1, tr, D_MODEL), lambda b: (0, b, 0))],
        out_specs=[pl.BlockSpec((nbs, SEQ, D_MODEL), lambda b: (b, 0, 0)),
                   pl.BlockSpec((nbs, 1, SEQ, KV_RANK), lambda b: (b, 0, 0, 0)),
                   pl.BlockSpec((nbs, 1, ROPE_A, SEQ), lambda b: (b, 0, 0, 0)),
                   pl.BlockSpec((nbs, 1, N_HEADS, HEAD_DIM, SEQ), lambda b: (b, 0, 0, 0, 0)),
                   pl.BlockSpec((nbs, 1, N_HEADS, HEAD_DIM, SEQ), lambda b: (b, 0, 0, 0, 0)),
                   pl.BlockSpec((8, 3 * D_MODEL), lambda b: (0, 0)),
                   pl.BlockSpec((tr, O_END), lambda b: (b, 0)),
                   pl.BlockSpec((tr, D_MODEL), lambda b: (b, 0))],
        out_shape=[jax.ShapeDtypeStruct((nb, SEQ, D_MODEL), F32),
                   jax.ShapeDtypeStruct((nb, 1, SEQ, KV_RANK), F32),
                   jax.ShapeDtypeStruct((nb, 1, ROPE_A, SEQ), F32),
                   jax.ShapeDtypeStruct((nb, 1, N_HEADS, HEAD_DIM, SEQ), F32),
                   jax.ShapeDtypeStruct((nb, 1, N_HEADS, HEAD_DIM, SEQ), F32),
                   jax.ShapeDtypeStruct((8, 3 * D_MODEL), F32),
                   jax.ShapeDtypeStruct((D_MODEL, O_END), BF16),
                   jax.ShapeDtypeStruct((D_MODEL, D_MODEL), BF16)],
        scratch_shapes=[pltpu.VMEM((nbs * SEQ, D_MODEL), BF16)],
        compiler_params=pltpu.CompilerParams(dimension_semantics=("arbitrary",), vmem_limit_bytes=VMEM_LIMIT),
        name="prompt_even",
    )(x, *map(_arr, ins), cond_t, w_mod, b_mod, w_in_odd, w_out_odd)


def _gqa_pair_operands(k, v, kg2, lo):
    kn = _rms_halves(k, kg2, lo)
    return kn, (kn.astype(BF16), pltpu.roll(kn, HEAD_DIM, 1).astype(BF16)), \
        (_with_ones(v.astype(BF16)), _with_ones(pltpu.roll(v, HEAD_DIM, 1).astype(BF16)))


def _p1_kernel(sink_ref, x_ref, m_ref, ng_ref, win_ref, gqg_ref, gkg_ref, sqg_ref, skg_ref, wout_ref,
               xo_ref, gk_ref, gv_ref, sk_ref, sv_ref, y_scr):
    nbs = x_ref.shape[0]
    x = x_ref[...].reshape(nbs * SEQ, D_MODEL)
    h, gate = _modulate(x, ng_ref[...], m_ref[0:1, :])
    hb = h.astype(BF16)
    lo = _lane_lo()
    hi = jnp.logical_not(lo)
    sc = HEAD_DIM ** -0.5 * LOG2E
    rows = [slice(bi * SEQ, (bi + 1) * SEQ) for bi in range(nbs)]

    branches = ((O_QC, O_KC, O_VC, O_GC, gqg_ref, gkg_ref, gk_ref, gv_ref, False, 0),
                (O_QD, O_KD, O_VD, O_GD, sqg_ref, skg_ref, sk_ref, sv_ref, True, 4 * LANES))
    for oq, ok, ov, og, qg_ref, kg_ref, ck_ref, cv_ref, has_sink, yoff in branches:
        zq = _dot(hb, win_ref[:, oq:oq + 4 * LANES])
        zkv = _dot(hb, win_ref[:, ok:ok + 2 * LANES])
        zg = _dot(hb, win_ref[:, og:og + 4 * LANES])
        v = zkv[:, LANES:]
        kn, ks, vs = _gqa_pair_operands(zkv[:, :LANES], v, kg_ref[...], lo)
        for bi, rs in enumerate(rows):
            _store_pair_transposed(ck_ref, bi, 0, kn[rs])
            _store_pair_transposed(cv_ref, bi, 0, v[rs])
        qg = qg_ref[...] * sc
        for p in range(N_PAIRS):
            sl = slice(p * LANES, (p + 1) * LANES)
            qn = _rms_halves(zq[:, sl], qg, lo)
            qms = [jnp.where(keep, qn, 0.0).astype(BF16) for keep in (lo, hi)]
            kv = p // 2
            for bi, rs in enumerate(rows):
                o2 = []
                for half in (0, 1):
                    swap = 0 if kv == half else 1
                    sink = sink_ref[2 * p + half] * LOG2E if has_sink else None
                    o2.append(_attend([(_dot_nt(qms[half][rs], ks[swap][rs]), vs[swap][rs], False)], sink))
                o = jnp.where(lo, o2[0], o2[1])
                y_scr[rs, yoff + p * LANES:yoff + (p + 1) * LANES] = (o * _silu(zg[rs, sl])).astype(BF16)

    xo_ref[...] = (x + gate * _dot(y_scr[...], wout_ref[...])).reshape(nbs, SEQ, D_MODEL)


def _prompt_odd(sink, x, m, ng, win, gqg, gkg, sqg, skg, wout):
    nb = x.shape[0]
    nbs = PROMPT_BATCHES_PER_STEP
    assert nb % nbs == 0
    ins = (m, ng, win, gqg, gkg, sqg, skg, wout)
    cache_spec = pl.BlockSpec((nbs, 1, 2, HEAD_DIM, SEQ), lambda b: (b, 0, 0, 0, 0))
    cache_shape = jax.ShapeDtypeStruct((nb, 1, 2, HEAD_DIM, SEQ), F32)
    return pl.pallas_call(
        _p1_kernel,
        grid=(nb // nbs,),
        in_specs=[pl.BlockSpec(memory_space=pltpu.SMEM),
                  pl.BlockSpec((nbs, SEQ, D_MODEL), lambda b: (b, 0, 0))] + [_spec(a) for a in ins],
        out_specs=[pl.BlockSpec((nbs, SEQ, D_MODEL), lambda b: (b, 0, 0))] + [cache_spec] * 4,
        out_shape=[jax.ShapeDtypeStruct((nb, SEQ, D_MODEL), F32)] + [cache_shape] * 4,
        scratch_shapes=[pltpu.VMEM((nbs * SEQ, D_MODEL), BF16)],
        compiler_params=pltpu.CompilerParams(dimension_semantics=("arbitrary",), vmem_limit_bytes=VMEM_LIMIT),
        name="prompt_odd",
    )(sink, x, *map(_arr, ins))


def _s0a_kernel(x_ref, m_ref, ng_ref, win_ref, qag_ref, wq_ref, kvag_ref, wkk_ref, wkv_ref, qg_ref, kg_ref,
                naqg_ref, nakg_ref, cos_ref, sin_ref,
                qa_ref, ka_ref, va_ref, qb_ref, kb_ref, vb_ref, g_ref):
    b = pl.program_id(0)
    lo = _lane_lo()
    ones_all = _group_ones2(LANES)
    partner = _rope_matrix2(ROPE_A, LANES, NOPE_A)
    hb = _modulate(x_ref[0], ng_ref[...], m_ref[pl.ds(1 + b, 1), :])[0].astype(BF16)
    cos, sin = cos_ref[...], sin_ref[...]

    qln = _rms(_dot_nt(hb, win_ref[E_QLAT:E_CKV, :]), qag_ref[...], Q_RANK).astype(BF16)
    q_all = _dot(qln, wq_ref[...])
    ckv_n = _rms(_dot_nt(hb, win_ref[E_CKV:E_KROPE, :]), kvag_ref[...], KV_RANK)
    kr = _dot_nt(hb, _rope_key_slab(win_ref))
    cb = ckv_n.astype(BF16)
    kk = _dot(cb, wkk_ref[...])
    va_ref[0] = _dot(cb, wkv_ref[...]).astype(BF16)
    zq = _dot_nt(hb, win_ref[E_QB:E_KB, :])
    zk = _dot_nt(hb, win_ref[E_KB:E_VB, :])
    vb_ref[0] = _dot_nt(hb, win_ref[E_VB:E_GB, :]).astype(BF16)
    g_ref[0, :, 0:4 * LANES] = _silu(_dot_nt(hb, win_ref[E_GA:E_QB, :]))
    g_ref[0, :, 4 * LANES:8 * LANES] = _silu(_dot_nt(hb, win_ref[E_GB:E_END, :]))

    qg = qg_ref[...] * (QK_A ** -0.5 * LOG2E)
    kg = kg_ref[...]
    k_partner = _lane_mix(kr * kg, partner) * sin
    for hh in range(N_HEADS):
        sl = slice(hh * LANES, (hh + 1) * LANES)
        qn = _rms_mxu(q_all[:, sl], qg, QK_A, ones_all)
        qa_ref[0, :, sl] = (qn * cos + _lane_mix(qn, partner) * sin).astype(BF16)
        k_raw = kk[:, sl] + kr
        k_inv = lax.rsqrt(_lane_mix(k_raw * k_raw, ones_all) / QK_A + EPS)
        ka_ref[0, :, sl] = ((k_raw * kg * cos + k_partner) * k_inv).astype(BF16)
    naqg = naqg_ref[...] * (HEAD_DIM ** -0.5 * LOG2E)
    for p in range(N_PAIRS):
        sl = slice(p * LANES, (p + 1) * LANES)
        qb_ref[0, :, sl] = _rms_halves(zq[:, sl], naqg, lo).astype(BF16)
        kb_ref[0, :, sl] = _rms_halves(zk[:, sl], nakg_ref[...], lo).astype(BF16)


def _sample_even_proj(x, m, ng, win, qag, wq, kvag, wkk, wkv, qg, kg, naqg, nakg, cos, sin):
    nb, s, _ = x.shape
    nq = s // PROJ_BLOCK
    ins = (m, ng, win, qag, wq, kvag, wkk, wkv, qg, kg, naqg, nakg)
    tab = pl.BlockSpec((PROJ_BLOCK, LANES), lambda b, j: (j, 0))

    def blk(w):
        return pl.BlockSpec((1, PROJ_BLOCK, w), lambda b, j: (b, j, 0))

    def shp(w, dt):
        return jax.ShapeDtypeStruct((nb, s, w), dt)

    return pl.pallas_call(
        _s0a_kernel,
        grid=(nb, nq),
        in_specs=[blk(D_MODEL)] + [_spec(a) for a in ins] + [tab, tab],
        out_specs=[blk(1024), blk(1024), blk(512), blk(512), blk(512), blk(512), blk(1024)],
        out_shape=[shp(1024, BF16), shp(1024, BF16), shp(512, BF16), shp(512, BF16), shp(512, BF16),
                   shp(512, BF16), shp(1024, F32)],
        compiler_params=pltpu.CompilerParams(dimension_semantics=("arbitrary", "arbitrary"),
                                             vmem_limit_bytes=VMEM_LIMIT),
        name="sample_even_proj",
    )(x, *map(_arr, ins), cos, sin)


def _build_bias_table(rpb_ref, tile_scr, tab_ref):
    qc = lax.broadcasted_iota(jnp.int32, (GRID_W, LANES), 0)
    lane = lax.broadcasted_iota(jnp.int32, (GRID_W, LANES), 1)
    kc = jnp.bitwise_and(lane, GRID_W - 1)
    lo = lane < GRID_W
    diff = kc - qc + (NA_COLS - 1)
    cs = jnp.clip(qc - NA_COLS // 2, 0, GRID_W - NA_COLS)
    valid = (kc >= cs) & (kc < cs + NA_COLS)
    tab_ref[...] = jnp.zeros(tab_ref.shape, F32)
    tile_scr[RPB_ROWS] = jnp.zeros((GRID_W, LANES), F32)

    def per_head(h, carry):
        for dr in range(RPB_ROWS):
            t = jnp.zeros((GRID_W, LANES), F32)
            for dc in range(RPB_COLS):
                t = jnp.where(diff == dc, rpb_ref[(h * RPB_ROWS + dr) * RPB_COLS + dc], t)
            tile_scr[dr] = jnp.where(valid, t * LOG2E, NEG_INF)
        for c in range(NA_ROWS // 2, NA_ROWS // 2 + NA_ROWS):
            d0 = 2 * c - NA_ROWS
            tab_ref[0, h, c] = jnp.where(lo, tile_scr[d0], tile_scr[d0 + 1])
            tab_ref[1, h, c] = jnp.where(lo, tile_scr[d0 - 1 if d0 > 0 else RPB_ROWS], tile_scr[d0])
        return carry

    lax.fori_loop(0, N_HEADS, per_head, 0)


def _s0b_kernel(rpb_ref, x_ref, m_ref, qa_ref, ka_ref, va_ref, qb_ref, kb_ref, vb_ref, g_ref,
                cckv_ref, ckr_ref, cnk_ref, cnv_ref, wkk_ref, wkv_ref, kg_ref, wout_ref,
                xo_ref, kca_scr, vca_scr, tile_scr, tab_scr, y_scr):
    b = pl.program_id(0)
    j = pl.program_id(1)
    lo = _lane_lo()
    n_lat = ka_ref.shape[1]

    @pl.when((b == 0) & (j == 0))
    def _():
        _build_bias_table(rpb_ref, tile_scr, tab_scr)

    @pl.when(j == 0)
    def _():
        kr_t = jnp.concatenate([jnp.zeros((NOPE_A, PAST_LEN), F32), ckr_ref[0],
                                jnp.zeros((LANES - QK_A, PAST_LEN), F32)], axis=0)
        keys, vals = _mla_keys(cckv_ref[0].astype(BF16), kr_t.T, wkk_ref, wkv_ref, kg_ref[...])
        for hh in range(N_HEADS):
            kca_scr[:, hh * LANES:(hh + 1) * LANES] = keys[hh]
        vca_scr[...] = vals

    kidx = lax.broadcasted_iota(jnp.int32, (1, n_lat), 1)
    for p in range(N_PAIRS):
        sl = slice(p * LANES, (p + 1) * LANES)
        o2 = []
        va = _with_ones(va_ref[0, :, sl])
        vca = _with_ones(vca_scr[:, sl])
        for hh in (2 * p, 2 * p + 1):
            hs = slice(hh * LANES, (hh + 1) * LANES)
            q = qa_ref[0, :, hs]
            o2.append(_attend([(_dot_nt(q, ka_ref[0, :, hs]), va, False),
                               (_dot_nt(q, kca_scr[:, hs]), vca, False)]))
        oa = jnp.where(lo, o2[0], o2[1])
        y_scr[:, sl] = (oa * g_ref[0, :, sl]).astype(BF16)

        qb = qb_ref[0, :, sl]
        kb = kb_ref[0, :, sl]
        vb = _with_ones(vb_ref[0, :, sl])
        kcb = cnk_ref[0, sl, :].astype(BF16)
        vcb = _with_ones(cnv_ref[0, sl, :].astype(BF16), transposed=True)
        o2 = []
        for half in (0, 1):
            head = 2 * p + half
            qm = jnp.where(lo if half == 0 else jnp.logical_not(lo), qb, jnp.zeros_like(qb))
            s_lat = _dot_nt(qm, kb)
            rows = []
            for local in range(NA_Q_BLOCK // GRID_W):
                qr = j * (NA_Q_BLOCK // GRID_W) + local
                par = 0 if local % 2 == 1 else 1
                c0 = (RPB_ROWS + par - local) // 2 - (NA_Q_BLOCK // GRID_W // 2) * j
                bias = jnp.concatenate([tab_scr[par, head, c0 + t] for t in range(n_lat // LANES)], axis=1)
                r0 = jnp.clip(qr - NA_ROWS // 2, 0, n_lat // GRID_W - NA_ROWS) * GRID_W
                ok = (kidx >= r0) & (kidx < r0 + NA_ROWS * GRID_W)
                rows.append(jnp.where(ok, s_lat[local * GRID_W:(local + 1) * GRID_W] + bias, NEG_INF))
            s_lat = jnp.concatenate(rows, axis=0)
            o2.append(_attend([(s_lat, vb, False), (_dot(qm, kcb), vcb, True)]))
        ob = jnp.where(lo, o2[0], o2[1])
        ys = slice(4 * LANES + p * LANES, 4 * LANES + (p + 1) * LANES)
        y_scr[:, ys] = (ob * g_ref[0, :, ys]).astype(BF16)

    d = x_ref.shape[-1]
    gate = m_ref[pl.ds(1 + b, 1), 2 * d:]
    xo_ref[0] = x_ref[0] + gate * _dot(y_scr[...], wout_ref[...])


def _sample_even_attn(rpb, x, m, qa, ka, va, qb, kb, vb, g, cckv, ckr, cnk, cnv, wkk, wkv, kg, wout):
    nb, s, _ = x.shape
    nq = s // NA_Q_BLOCK

    def blk(w):
        return pl.BlockSpec((1, NA_Q_BLOCK, w), lambda b, j: (b, j, 0))

    def per_batch(a):
        return pl.BlockSpec((1,) + a.shape[1:], lambda b, j: (b, 0, 0))

    return pl.pallas_call(
        _s0b_kernel,
        grid=(nb, nq),
        in_specs=[pl.BlockSpec(memory_space=pltpu.SMEM), blk(D_MODEL), _spec(m),
                  blk(1024), per_batch(ka), per_batch(va), blk(512), per_batch(kb), per_batch(vb), blk(1024),
                  per_batch(cckv), per_batch(ckr), per_batch(cnk), per_batch(cnv),
                  _full(wkk.shape), _full(wkv.shape), _spec(kg), _full(wout.shape)],
        out_specs=blk(D_MODEL),
        out_shape=jax.ShapeDtypeStruct(x.shape, F32),
        scratch_shapes=[pltpu.VMEM((PAST_LEN, N_HEADS * LANES), BF16),
                        pltpu.VMEM((PAST_LEN, N_HEADS * HEAD_DIM), BF16),
                        pltpu.VMEM((RPB_ROWS + 1, GRID_W, LANES), F32),
                        pltpu.VMEM((2, N_HEADS, BIAS_CHUNKS, GRID_W, LANES), F32),
                        pltpu.VMEM((NA_Q_BLOCK, D_MODEL), BF16)],
        compiler_params=pltpu.CompilerParams(dimension_semantics=("arbitrary", "arbitrary"),
                                             vmem_limit_bytes=VMEM_LIMIT),
        name="sample_even_attn",
    )(rpb, x, _arr(m), qa, ka, va, qb, kb, vb, g, cckv, ckr, cnk, cnv, wkk, wkv, _arr(kg), wout)


def _s1a_kernel(x_ref, m_ref, ng_ref, win_ref, gqg_ref, gkg_ref, sqg_ref, skg_ref, cos_ref, sin_ref,
                qc_ref, kc_ref, vc_ref, qd_ref, kd_ref, vd_ref, g_ref):
    b = pl.program_id(0)
    lo = _lane_lo()
    partner = _rope_matrix2(HEAD_DIM, HEAD_DIM, 0)
    swap = _swap_matrix2()[:LANES]
    hb = _modulate(x_ref[0], ng_ref[...], m_ref[pl.ds(1 + b, 1), :])[0].astype(BF16)
    cos, sin = cos_ref[...], sin_ref[...]
    sc = HEAD_DIM ** -0.5 * LOG2E

    def rope(t):
        return t * cos + _lane_mix(t, partner) * sin

    branches = ((O_QC, O_KC, O_GC, gqg_ref, gkg_ref, qc_ref, kc_ref, vc_ref, 0),
                (O_QD, O_KD, O_GD, sqg_ref, skg_ref, qd_ref, kd_ref, vd_ref, 4 * LANES))
    for oq, ok, og, qg_ref, kg_ref, q_out, k_out, v_out, goff in branches:
        zq = _dot(hb, win_ref[:, oq:oq + 4 * LANES])
        zkv = _dot(hb, win_ref[:, ok:ok + 2 * LANES])
        qg = qg_ref[...] * sc
        for p in range(N_PAIRS):
            sl = slice(p * LANES, (p + 1) * LANES)
            q_out[0, :, sl] = rope(_rms_halves(zq[:, sl], qg, lo)).astype(BF16)
        kn = rope(_rms_halves(zkv[:, :LANES], kg_ref[...], lo))
        v = zkv[:, LANES:]
        for out, val in ((k_out, kn.astype(BF16)), (v_out, v.astype(BF16))):
            out[0, :, 0:LANES] = val
            out[0, :, LANES:2 * LANES] = _dot(val, swap).astype(BF16)
        g_ref[0, :, goff:goff + 4 * LANES] = _silu(_dot(hb, win_ref[:, og:og + 4 * LANES]))


def _sample_odd_proj(x, m, ng, win, gqg, gkg, sqg, skg, cos, sin):
    nb, s, _ = x.shape
    nq = s // PROJ_BLOCK
    ins = (m, ng, win, gqg, gkg, sqg, skg)
    tab = pl.BlockSpec((PROJ_BLOCK, LANES), lambda b, j: (j, 0))

    def blk(w):
        return pl.BlockSpec((1, PROJ_BLOCK, w), lambda b, j: (b, j, 0))

    def shp(w, dt):
        return jax.ShapeDtypeStruct((nb, s, w), dt)

    return pl.pallas_call(
        _s1a_kernel,
        grid=(nb, nq),
        in_specs=[blk(D_MODEL)] + [_spec(a) for a in ins] + [tab, tab],
        out_specs=[blk(512), blk(256), blk(256), blk(512), blk(256), blk(256), blk(1024)],
        out_shape=[shp(512, BF16), shp(256, BF16), shp(256, BF16), shp(512, BF16), shp(256, BF16),
                   shp(256, BF16), shp(1024, F32)],
        compiler_params=pltpu.CompilerParams(dimension_semantics=("arbitrary", "arbitrary"),
                                             vmem_limit_bytes=VMEM_LIMIT),
        name="sample_odd_proj",
    )(x, *map(_arr, ins), cos, sin)


def _s1b_kernel(sink_ref, x_ref, m_ref, qc_ref, kc_ref, vc_ref, qd_ref, kd_ref, vd_ref, g_ref,
                cgk_ref, cgv_ref, csk_ref, csv_ref, wout_ref, xo_ref, y_scr):
    b = pl.program_id(0)
    j = pl.program_id(1)
    lo = _lane_lo()
    n_lat = kc_ref.shape[1]
    win_keys = Q_BLOCK + 2 * SWA_HALF

    def ctx_pair(ref, values=False):
        a = ref[0].astype(BF16)
        pair = (a, _swap_halves(a))
        return tuple(_with_ones(t, transposed=True) for t in pair) if values else pair

    cgk, cgv, csk, csv = ctx_pair(cgk_ref), ctx_pair(cgv_ref, True), ctx_pair(csk_ref), ctx_pair(csv_ref, True)
    vcs = [_with_ones(vc_ref[0, :, w * LANES:(w + 1) * LANES]) for w in (0, 1)]

    ks = pl.multiple_of(jnp.clip(j * Q_BLOCK - SWA_HALF, 0, n_lat - win_keys), SWA_HALF)
    qpos = j * Q_BLOCK + lax.broadcasted_iota(jnp.int32, (Q_BLOCK, win_keys), 0)
    kpos = ks + lax.broadcasted_iota(jnp.int32, (Q_BLOCK, win_keys), 1)
    band = jnp.abs(qpos - kpos) <= SWA_HALF
    vds = [_with_ones(vd_ref[0, pl.ds(ks, win_keys), w * LANES:(w + 1) * LANES]) for w in (0, 1)]

    for p in range(N_PAIRS):
        sl = slice(p * LANES, (p + 1) * LANES)
        kv = p // 2
        qc = qc_ref[0, :, sl]
        qd = qd_ref[0, :, sl]
        oc2, od2 = [], []
        for half in (0, 1):
            swap = 0 if kv == half else 1
            ws = slice(swap * LANES, (swap + 1) * LANES)
            keep = lo if half == 0 else jnp.logical_not(lo)
            qm = jnp.where(keep, qc, jnp.zeros_like(qc))
            oc2.append(_attend([(_dot_nt(qm, kc_ref[0, :, ws]), vcs[swap], False),
                                (_dot(qm, cgk[swap]), cgv[swap], True)]))
            qm = jnp.where(keep, qd, jnp.zeros_like(qd))
            s_loc = jnp.where(band, _dot_nt(qm, kd_ref[0, pl.ds(ks, win_keys), ws]), NEG_INF)
            od2.append(_attend([(s_loc, vds[swap], False),
                                (_dot(qm, csk[swap]), csv[swap], True)], sink_ref[2 * p + half] * LOG2E))
        y_scr[:, sl] = (jnp.where(lo, oc2[0], oc2[1]) * g_ref[0, :, sl]).astype(BF16)
        ys = slice(4 * LANES + p * LANES, 4 * LANES + (p + 1) * LANES)
        y_scr[:, ys] = (jnp.where(lo, od2[0], od2[1]) * g_ref[0, :, ys]).astype(BF16)

    d = x_ref.shape[-1]
    gate = m_ref[pl.ds(1 + b, 1), 2 * d:]
    xo_ref[0] = x_ref[0] + gate * _dot(y_scr[...], wout_ref[...])


def _sample_odd_attn(sink, x, m, qc, kc, vc, qd, kd, vd, g, cgk, cgv, csk, csv, wout):
    nb, s, _ = x.shape
    nq = s // Q_BLOCK

    def blk(w):
        return pl.BlockSpec((1, Q_BLOCK, w), lambda b, j: (b, j, 0))

    def per_batch(a):
        return pl.BlockSpec((1,) + a.shape[1:], lambda b, j: (b, 0, 0))

    return pl.pallas_call(
        _s1b_kernel,
        grid=(nb, nq),
        in_specs=[pl.BlockSpec(memory_space=pltpu.SMEM), blk(D_MODEL), _spec(m),
                  blk(512), per_batch(kc), per_batch(vc), blk(512), per_batch(kd), per_batch(vd), blk(1024),
                  per_batch(cgk), per_batch(cgv), per_batch(csk), per_batch(csv), _full(wout.shape)],
        out_specs=blk(D_MODEL),
        out_shape=jax.ShapeDtypeStruct(x.shape, F32),
        scratch_shapes=[pltpu.VMEM((Q_BLOCK, D_MODEL), BF16)],
        compiler_params=pltpu.CompilerParams(dimension_semantics=("arbitrary", "arbitrary"),
                                             vmem_limit_bytes=VMEM_LIMIT),
        name="sample_odd_attn",
    )(sink, x, _arr(m), qc, kc, vc, qd, kd, vd, g, cgk, cgv, csk, csv, wout)


WEIGHT_PREP_STEPS = 8


G_MLA_Q, G_MLA_K, G_NA_Q, G_NA_K, G_GQA_Q, G_GQA_K, G_SWA_Q, G_SWA_K, N_GAINS = range(9)


def _weight_prep_kernel(wie_ref, woe_ref, wqu_ref, wkv_ref, ng_ref, *refs):
    gain_refs = refs[:N_GAINS]
    win_e_ref, wout_e_ref, wq_ref, wkk_ref, wkvv_ref, gt_ref, ngt_ref = refs[N_GAINS:]
    gt_ref[...] = jnp.zeros(gt_ref.shape, F32)
    for r, g_ref in enumerate(gain_refs):
        w = g_ref.shape[1]
        for off in range(0, LANES - w + 1, w):
            gt_ref[r, :, off:off + w] = g_ref[...]
    for layer in range(ngt_ref.shape[0]):
        ngt_ref[layer] = ng_ref[layer:layer + 1, :]

    win_e_ref[...] = wie_ref[...].astype(BF16)
    wout_e_ref[...] = woe_ref[0].astype(BF16)

    wq_ref[...] = jnp.zeros(wq_ref.shape, BF16)
    for h in range(N_HEADS):
        wq_ref[:, h * LANES:h * LANES + QK_A] = wqu_ref[0, :, h * QK_A:(h + 1) * QK_A].astype(BF16)
    lo = _lane_lo()
    for p in range(N_PAIRS):
        a = wkv_ref[0, :, (2 * p) * LANES:(2 * p + 1) * LANES]
        c = wkv_ref[0, :, (2 * p + 1) * LANES:(2 * p + 2) * LANES]
        wkk_ref[:, (2 * p) * LANES:(2 * p + 1) * LANES] = jnp.where(lo, a, 0.0).astype(BF16)
        wkk_ref[:, (2 * p + 1) * LANES:(2 * p + 2) * LANES] = jnp.where(lo, c, 0.0).astype(BF16)
        wkvv_ref[:, p * LANES:(p + 1) * LANES] = jnp.where(lo, pltpu.roll(a, HEAD_DIM, 1), c).astype(BF16)


def _weight_prep(w_in_even_t, w_out_even, w_q_up, w_kv_up, norm_g, gains):
    n = WEIGHT_PREP_STEPS
    assert len(gains) == N_GAINS
    small = (norm_g,) + tuple(gains)
    ins = (w_out_even, w_q_up, w_kv_up)
    out_cols = (D_MODEL, N_HEADS * LANES, N_HEADS * LANES, N_HEADS * HEAD_DIM)
    out_rows = (D_MODEL, Q_RANK, KV_RANK, KV_RANK)
    te = E_END // 6
    assert te * 6 == E_END and te % 16 == 0
    even_spec = pl.BlockSpec((te, D_MODEL), lambda i: (jnp.minimum(i, 5), 0))
    return pl.pallas_call(
        _weight_prep_kernel,
        grid=(n,),
        in_specs=[even_spec] + [pl.BlockSpec((1, a.shape[1] // n, a.shape[2]), lambda i: (0, i, 0)) for a in ins]
        + [_full(a.shape) for a in small],
        out_specs=[even_spec] + [pl.BlockSpec((r // n, c), lambda i: (i, 0)) for r, c in zip(out_rows, out_cols)]
        + [_full((N_GAINS, 1, LANES)), _full((norm_g.shape[0], 1, D_MODEL))],
        out_shape=[jax.ShapeDtypeStruct((E_END, D_MODEL), BF16)]
        + [jax.ShapeDtypeStruct((r, c), BF16) for r, c in zip(out_rows, out_cols)]
        + [jax.ShapeDtypeStruct((N_GAINS, 1, LANES), F32), jax.ShapeDtypeStruct((norm_g.shape[0], 1, D_MODEL), F32)],
        compiler_params=pltpu.CompilerParams(dimension_semantics=("arbitrary",), vmem_limit_bytes=VMEM_LIMIT),
        name="weight_prep",
    )(w_in_even_t, *ins, *small)


def _feature_major(c):
    b, h, l, d = c.shape
    return jnp.swapaxes(c, -1, -2).reshape(b, h * d, l)


def _token_major(c):
    return jnp.swapaxes(c, -1, -2)


def _rope_tables(s, rot_dim, period, start):
    quarter = rot_dim // 4
    t = np.arange(s)
    inv = ROPE_THETA ** (-np.arange(quarter, dtype=np.float64) / quarter)
    row = (t // GRID_W).astype(np.float64)[:, None] * inv
    col = (t % GRID_W).astype(np.float64)[:, None] * inv
    ang = np.concatenate([row, col], axis=-1)
    cos, sin = np.cos(ang), np.sin(ang)
    pre = np.ones((s, start))
    post = np.zeros((s, period - start - rot_dim))
    c = np.concatenate([pre, cos, cos, post], axis=-1)
    sn = np.concatenate([0 * pre, sin, sin, post], axis=-1)
    rep = LANES // period
    return jnp.asarray(np.tile(c, (1, rep)), F32), jnp.asarray(np.tile(sn, (1, rep)), F32)


def kernel(x_prompt, x_sample, cache_mla_ckv, cache_mla_krope, cache_na_k, cache_na_v, cache_gqa_k, cache_gqa_v, cache_swa_k, cache_swa_v, c, c_ctx, norm_g, w_mod, b_mod, w_in_even, mla_qa_g, w_q_up, mla_kva_g, w_kv_up, mla_q_g, mla_k_g, na_q_g, na_k_g, na_rpb, w_out_even, w_in_odd, gqa_q_g, gqa_k_g, swa_q_g, swa_k_g, swa_sink, w_out_odd):
    n_dec = x_sample.shape[0]
    assert w_mod.shape[0] == 2 and n_dec + 1 <= 8

    cond_t = jnp.concatenate([c_ctx[:, None], c.T, jnp.zeros((D_MODEL, 7 - n_dec), F32)], axis=1)
    n_cond = 1 + n_dec
    m_even = _modulation(cond_t, n_cond, w_mod, b_mod, 0)

    gains = (mla_q_g, mla_k_g, na_q_g, na_k_g, gqa_q_g, gqa_k_g, swa_q_g, swa_k_g)
    win_e, wout_e, wq, wkk, wkv, gt, ngt = _weight_prep(
        jnp.swapaxes(w_in_even[0], 0, 1), w_out_even, w_q_up, w_kv_up, norm_g, gains)
    even = (_Row(ngt, 0), win_e, mla_qa_g, wq, mla_kva_g, wkk, wkv,
            _Row(gt, G_MLA_Q), _Row(gt, G_MLA_K), _Row(gt, G_NA_Q), _Row(gt, G_NA_K))
    sink = swa_sink[0].astype(F32)

    xp1, new_ckv, new_krope, new_na_k, new_na_v, m_odd, win_o, wout_o = _prompt_even(
        x_prompt, m_even, *even, wout_e, cond_t, n_cond, w_mod, b_mod, w_in_odd, w_out_odd)
    odd = (_Row(ngt, 1), win_o, _Row(gt, G_GQA_Q), _Row(gt, G_GQA_K), _Row(gt, G_SWA_Q), _Row(gt, G_SWA_K))
    xp2, new_gqa_k, new_gqa_v, new_swa_k, new_swa_v = _prompt_odd(sink, xp1, m_odd, *odd, wout_o)

    cos_e, sin_e = _rope_tables(DEC_SEQ, ROPE_A, LANES, NOPE_A)
    qa, ka, va, qbs, kbs, vbs, g0 = _sample_even_proj(x_sample, m_even, *even, cos_e, sin_e)
    ckr = jnp.swapaxes(cache_mla_krope[:, 0], -1, -2)
    xs1 = _sample_even_attn(na_rpb[0].reshape(-1), x_sample, m_even, qa, ka, va, qbs, kbs, vbs, g0,
                            cache_mla_ckv[:, 0], ckr, _feature_major(cache_na_k[:, 0]),
                            _feature_major(cache_na_v[:, 0]), wkk, wkv, _Row(gt, G_MLA_K), wout_e)
    cos_o, sin_o = _rope_tables(DEC_SEQ, HEAD_DIM, HEAD_DIM, 0)
    qc, kc, vc, qd, kd, vd, g1 = _sample_odd_proj(xs1, m_odd, *odd, cos_o, sin_o)
    xs2 = _sample_odd_attn(sink, xs1, m_odd, qc, kc, vc, qd, kd, vd, g1,
                           _feature_major(cache_gqa_k[:, 0]), _feature_major(cache_gqa_v[:, 0]),
                           _feature_major(cache_swa_k[:, 0]), _feature_major(cache_swa_v[:, 0]), wout_o)

    caches = (new_krope, new_na_k, new_na_v, new_gqa_k, new_gqa_v, new_swa_k, new_swa_v)
    return (xp2, xs2, new_ckv) + tuple(_token_major(c) for c in caches)
```

```python
import functools
from typing import NamedTuple

import jax
import jax.numpy as jnp
import numpy as np
from jax import lax
from jax.experimental import pallas as pl
from jax.experimental.pallas import tpu as pltpu

F32 = jnp.float32
BF16 = jnp.bfloat16

D_MODEL = 1024
SEQ = 256
DEC_SEQ = 1024
PAST_LEN = 256
GRID_W = 64
HEAD_DIM = 64
Q_RANK = 256
KV_RANK = 128
NOPE_A = 64
ROPE_A = 32
QK_A = NOPE_A + ROPE_A
N_HEADS = 8
NA_ROWS = 8
NA_COLS = 16
SWA_HALF = 128
ROPE_THETA = 10000.0
EPS = 1e-6
NEG_INF = -1e30
LOG2E = 1.4426950408889634

LANES = 128
Q_BLOCK = 512
NA_Q_BLOCK = 256
PROJ_BLOCK = 512
PROMPT_BATCHES_PER_STEP = 2
N_PAIRS = N_HEADS // 2
RPB_ROWS = 2 * NA_ROWS - 1
RPB_COLS = 2 * NA_COLS - 1
BIAS_CHUNKS = 16
VMEM_LIMIT = 48 * 1024 * 1024

E_QLAT, E_CKV, E_KROPE, E_GA, E_QB, E_KB, E_VB, E_GB, E_END = 0, 256, 384, 416, 928, 1440, 1952, 2464, 2976
O_QC, O_KC, O_VC, O_GC, O_QD, O_KD, O_VD, O_GD, O_END = 0, 512, 640, 768, 1280, 1792, 1920, 2048, 2560


def _dot(a, b):
    return lax.dot_general(a, b, (((1,), (0,)), ((), ())), preferred_element_type=F32)


def _dot_nt(a, b):
    return lax.dot_general(a, b, (((1,), (1,)), ((), ())), preferred_element_type=F32)


def _silu(x):
    return x / (1.0 + jnp.exp(-x))


def _rms(x, g, n):
    ss = jnp.sum(x * x, axis=-1, keepdims=True)
    return x * lax.rsqrt(ss / n + EPS) * g


def _rms_halves(x, g2, lo):
    x2 = x * x
    s_lo = jnp.sum(jnp.where(lo, x2, 0.0), axis=-1, keepdims=True)
    s_hi = jnp.sum(jnp.where(lo, 0.0, x2), axis=-1, keepdims=True)
    r = jnp.where(lo, lax.rsqrt(s_lo / HEAD_DIM + EPS), lax.rsqrt(s_hi / HEAD_DIM + EPS))
    return x * r * g2


def _modulate(x, g, m):
    d = x.shape[-1]
    xn = x * lax.rsqrt(jnp.mean(x * x, axis=-1, keepdims=True) + EPS) * g
    return xn * (1.0 + m[:, d:2 * d]) + m[:, :d], m[:, 2 * d:]


def _split_lanes(x):
    hi = x.astype(BF16)
    lo = (x - hi.astype(F32)).astype(BF16)
    return jnp.concatenate([hi, lo], axis=1)


def _lane_matrix2(entries):
    i = lax.broadcasted_iota(jnp.int32, (LANES, LANES), 0)
    j = lax.broadcasted_iota(jnp.int32, (LANES, LANES), 1)
    m = entries(i, j).astype(BF16)
    return jnp.concatenate([m, m], axis=0)


def _group_ones2(width):
    shift = width.bit_length() - 1
    return _lane_matrix2(lambda i, j: jnp.where((i >> shift) == (j >> shift), 1.0, 0.0))


def _rope_matrix2(rot_dim, period, start):
    half = rot_dim // 2

    def entries(i, j):
        pos = jnp.bitwise_and(j, period - 1) - start
        neg = (pos >= 0) & (pos < half) & (i == j + half)
        plus = (pos >= half) & (pos < rot_dim) & (i == j - half)
        return jnp.where(neg, -1.0, jnp.where(plus, 1.0, 0.0))

    return _lane_matrix2(entries)


def _swap_matrix2():
    return _lane_matrix2(lambda i, j: jnp.where(i == jnp.bitwise_xor(j, HEAD_DIM), 1.0, 0.0))


def _lane_mix(x, m2):
    return _dot(_split_lanes(x), m2)


def _rms_mxu(x, g, n, ones2):
    x2 = x * x
    nt = x.shape[-1] // LANES
    sq = x2[:, :LANES]
    for t in range(1, nt):
        sq = sq + x2[:, t * LANES:(t + 1) * LANES]
    r = lax.rsqrt(_lane_mix(sq, ones2) / n + EPS)
    return x * (r if nt == 1 else jnp.tile(r, (1, nt))) * g


def _with_ones(v, transposed=False):
    if transposed:
        return jnp.concatenate([v, jnp.ones((LANES, v.shape[1]), v.dtype)], axis=0)
    return jnp.concatenate([v, jnp.ones((v.shape[0], LANES), v.dtype)], axis=1)


def _attend(parts, sink=None):
    mx = None
    for s, _, _ in parts:
        pm = jnp.max(s, axis=-1, keepdims=True)
        mx = pm if mx is None else jnp.maximum(mx, pm)
    if sink is not None:
        mx = jnp.maximum(mx, sink)
    acc, den = None, None
    for s, v, v_t in parts:
        e = jnp.exp2(s - mx)
        po = (_dot_nt if v_t else _dot)(e.astype(BF16), v)
        acc = po if acc is None else acc + po
        if po.shape[1] == LANES:
            ps = jnp.sum(e, axis=-1, keepdims=True)
            den = ps if den is None else den + ps
    if den is None:
        den = acc[:, LANES:]
    if sink is not None:
        den = den + jnp.exp2(sink - mx)
    return acc[:, :LANES] * (1.0 / den)


def _lane_lo():
    return lax.broadcasted_iota(jnp.int32, (1, LANES), 1) < HEAD_DIM


def _store_pair_transposed(ref, bi, p, x):
    xt = x.T
    ref[bi, 0, 2 * p] = xt[:HEAD_DIM]
    ref[bi, 0, 2 * p + 1] = xt[HEAD_DIM:]


def _rope_key_slab(win_ref):
    d = win_ref.shape[1]
    return jnp.concatenate([jnp.zeros((NOPE_A, d), BF16), win_ref[E_KROPE:E_GA, :],
                            jnp.zeros((LANES - QK_A, d), BF16)], axis=0)


def _swap_halves(a):
    return jnp.concatenate([a[HEAD_DIM:], a[:HEAD_DIM]], axis=0)


def _mod_step(n_cond, is_first, bias_row, c_ref, w_ref, o_ref):
    @pl.when(is_first)
    def _():
        o_ref[:n_cond, :] = jnp.broadcast_to(bias_row, (n_cond, o_ref.shape[1]))
        o_ref[n_cond:, :] = jnp.zeros((o_ref.shape[0] - n_cond, o_ref.shape[1]), F32)

    s = _silu(c_ref[...])
    cols = [jnp.broadcast_to(s[:, r:r + 1], (s.shape[0], LANES)) for r in range(n_cond)]
    for t in range(w_ref.shape[1] // LANES):
        sl = slice(t * LANES, (t + 1) * LANES)
        w = w_ref[:, sl]
        for r in range(n_cond):
            o_ref[r:r + 1, sl] += jnp.sum(w * cols[r], axis=0, keepdims=True)


def _mla_keys(cb, kr, wkk_ref, wkv_ref, kg, rope=None):
    kk = _dot(cb, wkk_ref[...])
    keys = []
    for h in range(N_HEADS):
        k = _rms(kk[:, h * LANES:(h + 1) * LANES] + kr, kg, QK_A)
        if rope is not None:
            k = rope(k)
        keys.append(k.astype(BF16))
    return keys, _dot(cb, wkv_ref[...]).astype(BF16)


def _p0_kernel(n_cond, x_ref, m_ref, ng_ref, win_ref, qag_ref, wq_ref, kvag_ref, wkk_ref, wkv_ref, qg_ref, kg_ref,
               naqg_ref, nakg_ref, wout_ref, ct_ref, wm_ref, bm_ref, wio_ref, woo_ref,
               xo_ref, ckv_ref, krope_ref, nak_ref, nav_ref, mo_ref, wino_ref, wouto_ref, y_scr):
    _mod_step(n_cond, pl.program_id(0) == 0, bm_ref[1:2, :], ct_ref, wm_ref, mo_ref)
    wino_ref[...] = wio_ref[0].astype(BF16)
    wouto_ref[...] = woo_ref[0].astype(BF16)

    nbs = x_ref.shape[0]
    x = x_ref[...].reshape(nbs * SEQ, D_MODEL)
    h, gate = _modulate(x, ng_ref[...], m_ref[0:1, :])
    hb = h.astype(BF16)
    lo = _lane_lo()
    hi = jnp.logical_not(lo)
    rows = [slice(bi * SEQ, (bi + 1) * SEQ) for bi in range(nbs)]

    qln = _rms(_dot_nt(hb, win_ref[E_QLAT:E_CKV, :]), qag_ref[...], Q_RANK).astype(BF16)
    q_all = _dot(qln, wq_ref[...])
    ckv_n = _rms(_dot_nt(hb, win_ref[E_CKV:E_KROPE, :]), kvag_ref[...], KV_RANK)
    kr = _dot_nt(hb, _rope_key_slab(win_ref))
    for bi, rs in enumerate(rows):
        ckv_ref[bi, 0] = ckv_n[rs]
        krope_ref[bi, 0] = kr[rs].T[NOPE_A:QK_A]
    keys, vals = _mla_keys(ckv_n.astype(BF16), kr, wkk_ref, wkv_ref, kg_ref[...])
    qg = qg_ref[...] * (QK_A ** -0.5 * LOG2E)

    ga = _dot_nt(hb, win_ref[E_GA:E_QB, :])
    zq = _dot_nt(hb, win_ref[E_QB:E_KB, :])
    zk = _dot_nt(hb, win_ref[E_KB:E_VB, :])
    zv = _dot_nt(hb, win_ref[E_VB:E_GB, :])
    gb = _dot_nt(hb, win_ref[E_GB:E_END, :])
    naqg = naqg_ref[...] * (HEAD_DIM ** -0.5 * LOG2E)

    for p in range(N_PAIRS):
        sl = slice(p * LANES, (p + 1) * LANES)
        ys = slice(4 * LANES + p * LANES, 4 * LANES + (p + 1) * LANES)
        qhs = [_rms(q_all[:, hh * LANES:(hh + 1) * LANES], qg, QK_A).astype(BF16) for hh in (2 * p, 2 * p + 1)]
        qb = _rms_halves(zq[:, sl], naqg, lo)
        kb = _rms_halves(zk[:, sl], nakg_ref[...], lo)
        vb = zv[:, sl]
        kbb, vbb = kb.astype(BF16), vb.astype(BF16)
        va = vals[:, sl]
        qms = [jnp.where(keep, qb, 0.0).astype(BF16) for keep in (lo, hi)]
        for bi, rs in enumerate(rows):
            o2 = [_attend([(_dot_nt(qhs[i][rs], keys[2 * p + i][rs]), va[rs], False)]) for i in (0, 1)]
            y_scr[rs, sl] = (jnp.where(lo, o2[0], o2[1]) * _silu(ga[rs, sl])).astype(BF16)
            _store_pair_transposed(nak_ref, bi, p, kb[rs])
            _store_pair_transposed(nav_ref, bi, p, vb[rs])
            o2 = [_attend([(_dot_nt(qms[i][rs], kbb[rs]), vbb[rs], False)]) for i in (0, 1)]
            y_scr[rs, ys] = (jnp.where(lo, o2[0], o2[1]) * _silu(gb[rs, sl])).astype(BF16)

    xo_ref[...] = (x + gate * _dot(y_scr[...], wout_ref[...])).reshape(nbs, SEQ, D_MODEL)


def _full(shape):
    n = len(shape)
    return pl.BlockSpec(shape, lambda *_: (0,) * n, pipeline_mode=pl.Buffered(1))


class _Row(NamedTuple):
    table: jax.Array
    row: int


def _spec(a):
    if isinstance(a, _Row):
        idx = (a.row,) + (0,) * (a.table.ndim - 1)
        return pl.BlockSpec((None,) + a.table.shape[1:], lambda *_: idx, pipeline_mode=pl.Buffered(1))
    return _full(a.shape)


def _arr(a):
    return a.table if isinstance(a, _Row) else a


def _prompt_even(x, m, ng, win, qag, wq, kvag, wkk, wkv, qg, kg, naqg, nakg, wout,
                 cond_t, n_cond, w_mod, b_mod, w_in_odd, w_out_odd):
    nb = x.shape[0]
    nbs = PROMPT_BATCHES_PER_STEP
    steps = nb // nbs
    assert nb % nbs == 0 and D_MODEL % (16 * steps) == 0
    tr = D_MODEL // steps
    ins = (m, ng, win, qag, wq, kvag, wkk, wkv, qg, kg, naqg, nakg, wout)
    return pl.pallas_call(
        functools.partial(_p0_kernel, n_cond),
        grid=(steps,),
        in_specs=[pl.BlockSpec((nbs, SEQ, D_MODEL), lambda b: (b, 0, 0))] + [_spec(a) for a in ins]
        + [pl.BlockSpec((tr, 8), lambda b: (b, 0)),
           pl.BlockSpec((None, tr, 3 * D_MODEL), lambda b: (1, b, 0)),
           _full(b_mod.shape),
           pl.BlockSpec((1, tr, O_END), lambda b: (0, b, 0)),
           pl.BlockSpec((1, tr, D_MODEL), lambda b: (0, b, 0))],
        out_specs=[pl.BlockSpec((nbs, SEQ, D_MODEL), lambda b: (b, 0, 0)),
                   pl.BlockSpec((nbs, 1, SEQ, KV_RANK), lambda b: (b, 0, 0, 0)),
                   pl.BlockSpec((nbs, 1, ROPE_A, SEQ), lambda b: (b, 0, 0, 0)),
                   pl.BlockSpec((nbs, 1, N_HEADS, HEAD_DIM, SEQ), lambda b: (b, 0, 0, 0, 0)),
                   pl.BlockSpec((nbs, 1, N_HEADS, HEAD_DIM, SEQ), lambda b: (b, 0, 0, 0, 0)),
                   pl.BlockSpec((8, 3 * D_MODEL), lambda b: (0, 0)),
                   pl.BlockSpec((tr, O_END), lambda b: (b, 0)),
                   pl.BlockSpec((tr, D_MODEL), lambda b: (b, 0))],
        out_shape=[jax.ShapeDtypeStruct((nb, SEQ, D_MODEL), F32),
                   jax.ShapeDtypeStruct((nb, 1, SEQ, KV_RANK), F32),
                   jax.ShapeDtypeStruct((nb, 1, ROPE_A, SEQ), F32),
                   jax.ShapeDtypeStruct((nb, 1, N_HEADS, HEAD_DIM, SEQ), F32),
                   jax.ShapeDtypeStruct((nb, 1, N_HEADS, HEAD_DIM, SEQ), F32),
                   jax.ShapeDtypeStruct((8, 3 * D_MODEL), F32),
                   jax.ShapeDtypeStruct((D_MODEL, O_END), BF16),
                   jax.ShapeDtypeStruct((D_MODEL, D_MODEL), BF16)],
        scratch_shapes=[pltpu.VMEM((nbs * SEQ, D_MODEL), BF16)],
        compiler_params=pltpu.CompilerParams(dimension_semantics=("arbitrary",), vmem_limit_bytes=VMEM_LIMIT),
        name="prompt_even",
    )(x, *map(_arr, ins), cond_t, w_mod, b_mod, w_in_odd, w_out_odd)


def _gqa_pair_operands(k, v, kg2, lo):
    kn = _rms_halves(k, kg2, lo)
    return kn, (kn.astype(BF16), pltpu.roll(kn, HEAD_DIM, 1).astype(BF16)), \
        (_with_ones(v.astype(BF16)), _with_ones(pltpu.roll(v, HEAD_DIM, 1).astype(BF16)))


def _p1_kernel(sink_ref, x_ref, m_ref, ng_ref, win_ref, gqg_ref, gkg_ref, sqg_ref, skg_ref, wout_ref,
               xo_ref, gk_ref, gv_ref, sk_ref, sv_ref, y_scr):
    nbs = x_ref.shape[0]
    x = x_ref[...].reshape(nbs * SEQ, D_MODEL)
    h, gate = _modulate(x, ng_ref[...], m_ref[0:1, :])
    hb = h.astype(BF16)
    lo = _lane_lo()
    hi = jnp.logical_not(lo)
    sc = HEAD_DIM ** -0.5 * LOG2E
    rows = [slice(bi * SEQ, (bi + 1) * SEQ) for bi in range(nbs)]

    branches = ((O_QC, O_KC, O_VC, O_GC, gqg_ref, gkg_ref, gk_ref, gv_ref, False, 0),
                (O_QD, O_KD, O_VD, O_GD, sqg_ref, skg_ref, sk_ref, sv_ref, True, 4 * LANES))
    for oq, ok, ov, og, qg_ref, kg_ref, ck_ref, cv_ref, has_sink, yoff in branches:
        zq = _dot(hb, win_ref[:, oq:oq + 4 * LANES])
        zkv = _dot(hb, win_ref[:, ok:ok + 2 * LANES])
        zg = _dot(hb, win_ref[:, og:og + 4 * LANES])
        v = zkv[:, LANES:]
        kn, ks, vs = _gqa_pair_operands(zkv[:, :LANES], v, kg_ref[...], lo)
        for bi, rs in enumerate(rows):
            _store_pair_transposed(ck_ref, bi, 0, kn[rs])
            _store_pair_transposed(cv_ref, bi, 0, v[rs])
        qg = qg_ref[...] * sc
        for p in range(N_PAIRS):
            sl = slice(p * LANES, (p + 1) * LANES)
            qn = _rms_halves(zq[:, sl], qg, lo)
            qms = [jnp.where(keep, qn, 0.0).astype(BF16) for keep in (lo, hi)]
            kv = p // 2
            for bi, rs in enumerate(rows):
                o2 = []
                for half in (0, 1):
                    swap = 0 if kv == half else 1
                    sink = sink_ref[2 * p + half] * LOG2E if has_sink else None
                    o2.append(_attend([(_dot_nt(qms[half][rs], ks[swap][rs]), vs[swap][rs], False)], sink))
                o = jnp.where(lo, o2[0], o2[1])
                y_scr[rs, yoff + p * LANES:yoff + (p + 1) * LANES] = (o * _silu(zg[rs, sl])).astype(BF16)

    xo_ref[...] = (x + gate * _dot(y_scr[...], wout_ref[...])).reshape(nbs, SEQ, D_MODEL)


def _prompt_odd(sink, x, m, ng, win, gqg, gkg, sqg, skg, wout):
    nb = x.shape[0]
    nbs = PROMPT_BATCHES_PER_STEP
    assert nb % nbs == 0
    ins = (m, ng, win, gqg, gkg, sqg, skg, wout)
    cache_spec = pl.BlockSpec((nbs, 1, 2, HEAD_DIM, SEQ), lambda b: (b, 0, 0, 0, 0))
    cache_shape = jax.ShapeDtypeStruct((nb, 1, 2, HEAD_DIM, SEQ), F32)
    return pl.pallas_call(
        _p1_kernel,
        grid=(nb // nbs,),
        in_specs=[pl.BlockSpec(memory_space=pltpu.SMEM),
                  pl.BlockSpec((nbs, SEQ, D_MODEL), lambda b: (b, 0, 0))] + [_spec(a) for a in ins],
        out_specs=[pl.BlockSpec((nbs, SEQ, D_MODEL), lambda b: (b, 0, 0))] + [cache_spec] * 4,
        out_shape=[jax.ShapeDtypeStruct((nb, SEQ, D_MODEL), F32)] + [cache_shape] * 4,
        scratch_shapes=[pltpu.VMEM((nbs * SEQ, D_MODEL), BF16)],
        compiler_params=pltpu.CompilerParams(dimension_semantics=("arbitrary",), vmem_limit_bytes=VMEM_LIMIT),
        name="prompt_odd",
    )(sink, x, *map(_arr, ins))


def _s0a_kernel(x_ref, m_ref, ng_ref, win_ref, qag_ref, wq_ref, kvag_ref, wkk_ref, wkv_ref, qg_ref, kg_ref,
                naqg_ref, nakg_ref, cos_ref, sin_ref,
                qa_ref, ka_ref, va_ref, qb_ref, kb_ref, vb_ref, g_ref):
    b = pl.program_id(0)
    lo = _lane_lo()
    ones_all = _group_ones2(LANES)
    partner = _rope_matrix2(ROPE_A, LANES, NOPE_A)
    hb = _modulate(x_ref[0], ng_ref[...], m_ref[pl.ds(1 + b, 1), :])[0].astype(BF16)
    cos, sin = cos_ref[...], sin_ref[...]

    qln = _rms(_dot_nt(hb, win_ref[E_QLAT:E_CKV, :]), qag_ref[...], Q_RANK).astype(BF16)
    q_all = _dot(qln, wq_ref[...])
    ckv_n = _rms(_dot_nt(hb, win_ref[E_CKV:E_KROPE, :]), kvag_ref[...], KV_RANK)
    kr = _dot_nt(hb, _rope_key_slab(win_ref))
    cb = ckv_n.astype(BF16)
    kk = _dot(cb, wkk_ref[...])
    va_ref[0] = _dot(cb, wkv_ref[...]).astype(BF16)
    zq = _dot_nt(hb, win_ref[E_QB:E_KB, :])
    zk = _dot_nt(hb, win_ref[E_KB:E_VB, :])
    vb_ref[0] = _dot_nt(hb, win_ref[E_VB:E_GB, :]).astype(BF16)
    g_ref[0, :, 0:4 * LANES] = _silu(_dot_nt(hb, win_ref[E_GA:E_QB, :]))
    g_ref[0, :, 4 * LANES:8 * LANES] = _silu(_dot_nt(hb, win_ref[E_GB:E_END, :]))

    qg = qg_ref[...] * (QK_A ** -0.5 * LOG2E)
    kg = kg_ref[...]
    k_partner = _lane_mix(kr * kg, partner) * sin
    for hh in range(N_HEADS):
        sl = slice(hh * LANES, (hh + 1) * LANES)
        qn = _rms_mxu(q_all[:, sl], qg, QK_A, ones_all)
        qa_ref[0, :, sl] = (qn * cos + _lane_mix(qn, partner) * sin).astype(BF16)
        k_raw = kk[:, sl] + kr
        k_inv = lax.rsqrt(_lane_mix(k_raw * k_raw, ones_all) / QK_A + EPS)
        ka_ref[0, :, sl] = ((k_raw * kg * cos + k_partner) * k_inv).astype(BF16)
    naqg = naqg_ref[...] * (HEAD_DIM ** -0.5 * LOG2E)
    for p in range(N_PAIRS):
        sl = slice(p * LANES, (p + 1) * LANES)
        qb_ref[0, :, sl] = _rms_halves(zq[:, sl], naqg, lo).astype(BF16)
        kb_ref[0, :, sl] = _rms_halves(zk[:, sl], nakg_ref[...], lo).astype(BF16)


def _sample_even_proj(x, m, ng, win, qag, wq, kvag, wkk, wkv, qg, kg, naqg, nakg, cos, sin):
    nb, s, _ = x.shape
    nq = s // PROJ_BLOCK
    ins = (m, ng, win, qag, wq, kvag, wkk, wkv, qg, kg, naqg, nakg)
    tab = pl.BlockSpec((PROJ_BLOCK, LANES), lambda b, j: (j, 0))

    def blk(w):
        return pl.BlockSpec((1, PROJ_BLOCK, w), lambda b, j: (b, j, 0))

    def shp(w, dt):
        return jax.ShapeDtypeStruct((nb, s, w), dt)

    return pl.pallas_call(
        _s0a_kernel,
        grid=(nb, nq),
        in_specs=[blk(D_MODEL)] + [_spec(a) for a in ins] + [tab, tab],
        out_specs=[blk(1024), blk(1024), blk(512), blk(512), blk(512), blk(512), blk(1024)],
        out_shape=[shp(1024, BF16), shp(1024, BF16), shp(512, BF16), shp(512, BF16), shp(512, BF16),
                   shp(512, BF16), shp(1024, F32)],
        compiler_params=pltpu.CompilerParams(dimension_semantics=("arbitrary", "arbitrary"),
                                             vmem_limit_bytes=VMEM_LIMIT),
        name="sample_even_proj",
    )(x, *map(_arr, ins), cos, sin)


def _build_bias_table(rpb_ref, tile_scr, tab_ref):
    qc = lax.broadcasted_iota(jnp.int32, (GRID_W, LANES), 0)
    lane = lax.broadcasted_iota(jnp.int32, (GRID_W, LANES), 1)
    kc = jnp.bitwise_and(lane, GRID_W - 1)
    lo = lane < GRID_W
    diff = kc - qc + (NA_COLS - 1)
    cs = jnp.clip(qc - NA_COLS // 2, 0, GRID_W - NA_COLS)
    valid = (kc >= cs) & (kc < cs + NA_COLS)
    tab_ref[...] = jnp.zeros(tab_ref.shape, F32)
    tile_scr[RPB_ROWS] = jnp.zeros((GRID_W, LANES), F32)

    def per_head(h, carry):
        for dr in range(RPB_ROWS):
            t = jnp.zeros((GRID_W, LANES), F32)
            for dc in range(RPB_COLS):
                t = jnp.where(diff == dc, rpb_ref[(h * RPB_ROWS + dr) * RPB_COLS + dc], t)
            tile_scr[dr] = jnp.where(valid, t * LOG2E, NEG_INF)
        for c in range(NA_ROWS // 2, NA_ROWS // 2 + NA_ROWS):
            d0 = 2 * c - NA_ROWS
            tab_ref[0, h, c] = jnp.where(lo, tile_scr[d0], tile_scr[d0 + 1])
            tab_ref[1, h, c] = jnp.where(lo, tile_scr[d0 - 1 if d0 > 0 else RPB_ROWS], tile_scr[d0])
        return carry

    lax.fori_loop(0, N_HEADS, per_head, 0)


def _s0b_kernel(rpb_ref, x_ref, m_ref, qa_ref, ka_ref, va_ref, qb_ref, kb_ref, vb_ref, g_ref,
                cckv_ref, ckr_ref, cnk_ref, cnv_ref, wkk_ref, wkv_ref, kg_ref, wout_ref,
                xo_ref, kca_scr, vca_scr, tile_scr, tab_scr, y_scr):
    b = pl.program_id(0)
    j = pl.program_id(1)
    lo = _lane_lo()
    n_lat = ka_ref.shape[1]

    @pl.when((b == 0) & (j == 0))
    def _():
        _build_bias_table(rpb_ref, tile_scr, tab_scr)

    @pl.when(j == 0)
    def _():
        kr_t = jnp.concatenate([jnp.zeros((NOPE_A, PAST_LEN), F32), ckr_ref[0],
                                jnp.zeros((LANES - QK_A, PAST_LEN), F32)], axis=0)
        keys, vals = _mla_keys(cckv_ref[0].astype(BF16), kr_t.T, wkk_ref, wkv_ref, kg_ref[...])
        for hh in range(N_HEADS):
            kca_scr[:, hh * LANES:(hh + 1) * LANES] = keys[hh]
        vca_scr[...] = vals

    kidx = lax.broadcasted_iota(jnp.int32, (1, n_lat), 1)
    for p in range(N_PAIRS):
        sl = slice(p * LANES, (p + 1) * LANES)
        o2 = []
        va = _with_ones(va_ref[0, :, sl])
        vca = _with_ones(vca_scr[:, sl])
        for hh in (2 * p, 2 * p + 1):
            hs = slice(hh * LANES, (hh + 1) * LANES)
            q = qa_ref[0, :, hs]
            o2.append(_attend([(_dot_nt(q, ka_ref[0, :, hs]), va, False),
                               (_dot_nt(q, kca_scr[:, hs]), vca, False)]))
        oa = jnp.where(lo, o2[0], o2[1])
        y_scr[:, sl] = (oa * g_ref[0, :, sl]).astype(BF16)

        qb = qb_ref[0, :, sl]
        kb = kb_ref[0, :, sl]
        vb = _with_ones(vb_ref[0, :, sl])
        kcb = cnk_ref[0, sl, :].astype(BF16)
        vcb = _with_ones(cnv_ref[0, sl, :].astype(BF16), transposed=True)
        o2 = []
        for half in (0, 1):
            head = 2 * p + half
            qm = jnp.where(lo if half == 0 else jnp.logical_not(lo), qb, jnp.zeros_like(qb))
            s_lat = _dot_nt(qm, kb)
            rows = []
            for local in range(NA_Q_BLOCK // GRID_W):
                qr = j * (NA_Q_BLOCK // GRID_W) + local
                par = 0 if local % 2 == 1 else 1
                c0 = (RPB_ROWS + par - local) // 2 - (NA_Q_BLOCK // GRID_W // 2) * j
                bias = jnp.concatenate([tab_scr[par, head, c0 + t] for t in range(n_lat // LANES)], axis=1)
                r0 = jnp.clip(qr - NA_ROWS // 2, 0, n_lat // GRID_W - NA_ROWS) * GRID_W
                ok = (kidx >= r0) & (kidx < r0 + NA_ROWS * GRID_W)
                rows.append(jnp.where(ok, s_lat[local * GRID_W:(local + 1) * GRID_W] + bias, NEG_INF))
            s_lat = jnp.concatenate(rows, axis=0)
            o2.append(_attend([(s_lat, vb, False), (_dot(qm, kcb), vcb, True)]))
        ob = jnp.where(lo, o2[0], o2[1])
        ys = slice(4 * LANES + p * LANES, 4 * LANES + (p + 1) * LANES)
        y_scr[:, ys] = (ob * g_ref[0, :, ys]).astype(BF16)

    d = x_ref.shape[-1]
    gate = m_ref[pl.ds(1 + b, 1), 2 * d:]
    xo_ref[0] = x_ref[0] + gate * _dot(y_scr[...], wout_ref[...])


def _sample_even_attn(rpb, x, m, qa, ka, va, qb, kb, vb, g, cckv, ckr, cnk, cnv, wkk, wkv, kg, wout):
    nb, s, _ = x.shape
    nq = s // NA_Q_BLOCK

    def blk(w):
        return pl.BlockSpec((1, NA_Q_BLOCK, w), lambda b, j: (b, j, 0))

    def per_batch(a):
        return pl.BlockSpec((1,) + a.shape[1:], lambda b, j: (b, 0, 0))

    return pl.pallas_call(
        _s0b_kernel,
        grid=(nb, nq),
        in_specs=[pl.BlockSpec(memory_space=pltpu.SMEM), blk(D_MODEL), _spec(m),
                  blk(1024), per_batch(ka), per_batch(va), blk(512), per_batch(kb), per_batch(vb), blk(1024),
                  per_batch(cckv), per_batch(ckr), per_batch(cnk), per_batch(cnv),
                  _full(wkk.shape), _full(wkv.shape), _spec(kg), _full(wout.shape)],
        out_specs=blk(D_MODEL),
        out_shape=jax.ShapeDtypeStruct(x.shape, F32),
        scratch_shapes=[pltpu.VMEM((PAST_LEN, N_HEADS * LANES), BF16),
                        pltpu.VMEM((PAST_LEN, N_HEADS * HEAD_DIM), BF16),
                        pltpu.VMEM((RPB_ROWS + 1, GRID_W, LANES), F32),
                        pltpu.VMEM((2, N_HEADS, BIAS_CHUNKS, GRID_W, LANES), F32),
                        pltpu.VMEM((NA_Q_BLOCK, D_MODEL), BF16)],
        compiler_params=pltpu.CompilerParams(dimension_semantics=("arbitrary", "arbitrary"),
                                             vmem_limit_bytes=VMEM_LIMIT),
        name="sample_even_attn",
    )(rpb, x, _arr(m), qa, ka, va, qb, kb, vb, g, cckv, ckr, cnk, cnv, wkk, wkv, _arr(kg), wout)


def _s1a_kernel(x_ref, m_ref, ng_ref, win_ref, gqg_ref, gkg_ref, sqg_ref, skg_ref, cos_ref, sin_ref,
                qc_ref, kc_ref, vc_ref, qd_ref, kd_ref, vd_ref, g_ref):
    b = pl.program_id(0)
    lo = _lane_lo()
    partner = _rope_matrix2(HEAD_DIM, HEAD_DIM, 0)
    swap = _swap_matrix2()[:LANES]
    hb = _modulate(x_ref[0], ng_ref[...], m_ref[pl.ds(1 + b, 1), :])[0].astype(BF16)
    cos, sin = cos_ref[...], sin_ref[...]
    sc = HEAD_DIM ** -0.5 * LOG2E

    def rope(t):
        return t * cos + _lane_mix(t, partner) * sin

    branches = ((O_QC, O_KC, O_GC, gqg_ref, gkg_ref, qc_ref, kc_ref, vc_ref, 0),
                (O_QD, O_KD, O_GD, sqg_ref, skg_ref, qd_ref, kd_ref, vd_ref, 4 * LANES))
    for oq, ok, og, qg_ref, kg_ref, q_out, k_out, v_out, goff in branches:
        zq = _dot(hb, win_ref[:, oq:oq + 4 * LANES])
        zkv = _dot(hb, win_ref[:, ok:ok + 2 * LANES])
        qg = qg_ref[...] * sc
        for p in range(N_PAIRS):
            sl = slice(p * LANES, (p + 1) * LANES)
            q_out[0, :, sl] = rope(_rms_halves(zq[:, sl], qg, lo)).astype(BF16)
        kn = rope(_rms_halves(zkv[:, :LANES], kg_ref[...], lo))
        v = zkv[:, LANES:]
        for out, val in ((k_out, kn.astype(BF16)), (v_out, v.astype(BF16))):
            out[0, :, 0:LANES] = val
            out[0, :, LANES:2 * LANES] = _dot(val, swap).astype(BF16)
        g_ref[0, :, goff:goff + 4 * LANES] = _silu(_dot(hb, win_ref[:, og:og + 4 * LANES]))


def _sample_odd_proj(x, m, ng, win, gqg, gkg, sqg, skg, cos, sin):
    nb, s, _ = x.shape
    nq = s // PROJ_BLOCK
    ins = (m, ng, win, gqg, gkg, sqg, skg)
    tab = pl.BlockSpec((PROJ_BLOCK, LANES), lambda b, j: (j, 0))

    def blk(w):
        return pl.BlockSpec((1, PROJ_BLOCK, w), lambda b, j: (b, j, 0))

    def shp(w, dt):
        return jax.ShapeDtypeStruct((nb, s, w), dt)

    return pl.pallas_call(
        _s1a_kernel,
        grid=(nb, nq),
        in_specs=[blk(D_MODEL)] + [_spec(a) for a in ins] + [tab, tab],
        out_specs=[blk(512), blk(256), blk(256), blk(512), blk(256), blk(256), blk(1024)],
        out_shape=[shp(512, BF16), shp(256, BF16), shp(256, BF16), shp(512, BF16), shp(256, BF16),
                   shp(256, BF16), shp(1024, F32)],
        compiler_params=pltpu.CompilerParams(dimension_semantics=("arbitrary", "arbitrary"),
                                             vmem_limit_bytes=VMEM_LIMIT),
        name="sample_odd_proj",
    )(x, *map(_arr, ins), cos, sin)


def _s1b_kernel(sink_ref, x_ref, m_ref, qc_ref, kc_ref, vc_ref, qd_ref, kd_ref, vd_ref, g_ref,
                cgk_ref, cgv_ref, csk_ref, csv_ref, wout_ref, xo_ref, y_scr):
    b = pl.program_id(0)
    j = pl.program_id(1)
    lo = _lane_lo()
    n_lat = kc_ref.shape[1]
    win_keys = Q_BLOCK + 2 * SWA_HALF

    def ctx_pair(ref, values=False):
        a = ref[0].astype(BF16)
        pair = (a, _swap_halves(a))
        return tuple(_with_ones(t, transposed=True) for t in pair) if values else pair

    cgk, cgv, csk, csv = ctx_pair(cgk_ref), ctx_pair(cgv_ref, True), ctx_pair(csk_ref), ctx_pair(csv_ref, True)
    vcs = [_with_ones(vc_ref[0, :, w * LANES:(w + 1) * LANES]) for w in (0, 1)]

    ks = pl.multiple_of(jnp.clip(j * Q_BLOCK - SWA_HALF, 0, n_lat - win_keys), SWA_HALF)
    qpos = j * Q_BLOCK + lax.broadcasted_iota(jnp.int32, (Q_BLOCK, win_keys), 0)
    kpos = ks + lax.broadcasted_iota(jnp.int32, (Q_BLOCK, win_keys), 1)
    band = jnp.abs(qpos - kpos) <= SWA_HALF
    vds = [_with_ones(vd_ref[0, pl.ds(ks, win_keys), w * LANES:(w + 1) * LANES]) for w in (0, 1)]

    for p in range(N_PAIRS):
        sl = slice(p * LANES, (p + 1) * LANES)
        kv = p // 2
        qc = qc_ref[0, :, sl]
        qd = qd_ref[0, :, sl]
        oc2, od2 = [], []
        for half in (0, 1):
            swap = 0 if kv == half else 1
            ws = slice(swap * LANES, (swap + 1) * LANES)
            keep = lo if half == 0 else jnp.logical_not(lo)
            qm = jnp.where(keep, qc, jnp.zeros_like(qc))
            oc2.append(_attend([(_dot_nt(qm, kc_ref[0, :, ws]), vcs[swap], False),
                                (_dot(qm, cgk[swap]), cgv[swap], True)]))
            qm = jnp.where(keep, qd, jnp.zeros_like(qd))
            s_loc = jnp.where(band, _dot_nt(qm, kd_ref[0, pl.ds(ks, win_keys), ws]), NEG_INF)
            od2.append(_attend([(s_loc, vds[swap], False),
                                (_dot(qm, csk[swap]), csv[swap], True)], sink_ref[2 * p + half] * LOG2E))
        y_scr[:, sl] = (jnp.where(lo, oc2[0], oc2[1]) * g_ref[0, :, sl]).astype(BF16)
        ys = slice(4 * LANES + p * LANES, 4 * LANES + (p + 1) * LANES)
        y_scr[:, ys] = (jnp.where(lo, od2[0], od2[1]) * g_ref[0, :, ys]).astype(BF16)

    d = x_ref.shape[-1]
    gate = m_ref[pl.ds(1 + b, 1), 2 * d:]
    xo_ref[0] = x_ref[0] + gate * _dot(y_scr[...], wout_ref[...])


def _sample_odd_attn(sink, x, m, qc, kc, vc, qd, kd, vd, g, cgk, cgv, csk, csv, wout):
    nb, s, _ = x.shape
    nq = s // Q_BLOCK

    def blk(w):
        return pl.BlockSpec((1, Q_BLOCK, w), lambda b, j: (b, j, 0))

    def per_batch(a):
        return pl.BlockSpec((1,) + a.shape[1:], lambda b, j: (b, 0, 0))

    return pl.pallas_call(
        _s1b_kernel,
        grid=(nb, nq),
        in_specs=[pl.BlockSpec(memory_space=pltpu.SMEM), blk(D_MODEL), _spec(m),
                  blk(512), per_batch(kc), per_batch(vc), blk(512), per_batch(kd), per_batch(vd), blk(1024),
                  per_batch(cgk), per_batch(cgv), per_batch(csk), per_batch(csv), _full(wout.shape)],
        out_specs=blk(D_MODEL),
        out_shape=jax.ShapeDtypeStruct(x.shape, F32),
        scratch_shapes=[pltpu.VMEM((Q_BLOCK, D_MODEL), BF16)],
        compiler_params=pltpu.CompilerParams(dimension_semantics=("arbitrary", "arbitrary"),
                                             vmem_limit_bytes=VMEM_LIMIT),
        name="sample_odd_attn",
    )(sink, x, _arr(m), qc, kc, vc, qd, kd, vd, g, cgk, cgv, csk, csv, wout)


WEIGHT_PREP_STEPS = 8


G_MLA_Q, G_MLA_K, G_NA_Q, G_NA_K, G_GQA_Q, G_GQA_K, G_SWA_Q, G_SWA_K, N_GAINS = range(9)


def _weight_prep_kernel(n_cond, wie_ref, woe_ref, wqu_ref, wkv_ref, ct_ref, wm_ref, bm_ref, ng_ref, *refs):
    gain_refs = refs[:N_GAINS]
    win_e_ref, wout_e_ref, wq_ref, wkk_ref, wkvv_ref, gt_ref, ngt_ref, mo_ref = refs[N_GAINS:]
    _mod_step(n_cond, pl.program_id(0) == 0, bm_ref[0:1, :], ct_ref, wm_ref, mo_ref)
    gt_ref[...] = jnp.zeros(gt_ref.shape, F32)
    for r, g_ref in enumerate(gain_refs):
        w = g_ref.shape[1]
        for off in range(0, LANES - w + 1, w):
            gt_ref[r, :, off:off + w] = g_ref[...]
    for layer in range(ngt_ref.shape[0]):
        ngt_ref[layer] = ng_ref[layer:layer + 1, :]

    win_e_ref[...] = wie_ref[...].astype(BF16)
    wout_e_ref[...] = woe_ref[0].astype(BF16)

    wq_ref[...] = jnp.zeros(wq_ref.shape, BF16)
    for h in range(N_HEADS):
        wq_ref[:, h * LANES:h * LANES + QK_A] = wqu_ref[0, :, h * QK_A:(h + 1) * QK_A].astype(BF16)
    lo = _lane_lo()
    for p in range(N_PAIRS):
        a = wkv_ref[0, :, (2 * p) * LANES:(2 * p + 1) * LANES]
        c = wkv_ref[0, :, (2 * p + 1) * LANES:(2 * p + 2) * LANES]
        wkk_ref[:, (2 * p) * LANES:(2 * p + 1) * LANES] = jnp.where(lo, a, 0.0).astype(BF16)
        wkk_ref[:, (2 * p + 1) * LANES:(2 * p + 2) * LANES] = jnp.where(lo, c, 0.0).astype(BF16)
        wkvv_ref[:, p * LANES:(p + 1) * LANES] = jnp.where(lo, pltpu.roll(a, HEAD_DIM, 1), c).astype(BF16)


def _weight_prep(w_in_even_t, w_out_even, w_q_up, w_kv_up, cond_t, n_cond, w_mod, b_mod, norm_g, gains):
    n = WEIGHT_PREP_STEPS
    assert len(gains) == N_GAINS
    small = (norm_g,) + tuple(gains)
    tk = D_MODEL // n
    mod_specs = [pl.BlockSpec((tk, 8), lambda i: (i, 0)),
                 pl.BlockSpec((None, tk, 3 * D_MODEL), lambda i: (0, i, 0)),
                 _full(b_mod.shape)]
    ins = (w_out_even, w_q_up, w_kv_up)
    out_cols = (D_MODEL, N_HEADS * LANES, N_HEADS * LANES, N_HEADS * HEAD_DIM)
    out_rows = (D_MODEL, Q_RANK, KV_RANK, KV_RANK)
    te = E_END // 6
    assert te * 6 == E_END and te % 16 == 0
    even_spec = pl.BlockSpec((te, D_MODEL), lambda i: (jnp.minimum(i, 5), 0))
    return pl.pallas_call(
        functools.partial(_weight_prep_kernel, n_cond),
        grid=(n,),
        in_specs=[even_spec] + [pl.BlockSpec((1, a.shape[1] // n, a.shape[2]), lambda i: (0, i, 0)) for a in ins]
        + mod_specs + [_full(a.shape) for a in small],
        out_specs=[even_spec] + [pl.BlockSpec((r // n, c), lambda i: (i, 0)) for r, c in zip(out_rows, out_cols)]
        + [_full((N_GAINS, 1, LANES)), _full((norm_g.shape[0], 1, D_MODEL)), _full((8, 3 * D_MODEL))],
        out_shape=[jax.ShapeDtypeStruct((E_END, D_MODEL), BF16)]
        + [jax.ShapeDtypeStruct((r, c), BF16) for r, c in zip(out_rows, out_cols)]
        + [jax.ShapeDtypeStruct((N_GAINS, 1, LANES), F32), jax.ShapeDtypeStruct((norm_g.shape[0], 1, D_MODEL), F32),
           jax.ShapeDtypeStruct((8, 3 * D_MODEL), F32)],
        compiler_params=pltpu.CompilerParams(dimension_semantics=("arbitrary",), vmem_limit_bytes=VMEM_LIMIT),
        name="weight_prep",
    )(w_in_even_t, *ins, cond_t, w_mod, b_mod, *small)


def _feature_major(c):
    b, h, l, d = c.shape
    return jnp.swapaxes(c, -1, -2).reshape(b, h * d, l)


def _token_major(c):
    return jnp.swapaxes(c, -1, -2)


def _rope_tables(s, rot_dim, period, start):
    quarter = rot_dim // 4
    t = np.arange(s)
    inv = ROPE_THETA ** (-np.arange(quarter, dtype=np.float64) / quarter)
    row = (t // GRID_W).astype(np.float64)[:, None] * inv
    col = (t % GRID_W).astype(np.float64)[:, None] * inv
    ang = np.concatenate([row, col], axis=-1)
    cos, sin = np.cos(ang), np.sin(ang)
    pre = np.ones((s, start))
    post = np.zeros((s, period - start - rot_dim))
    c = np.concatenate([pre, cos, cos, post], axis=-1)
    sn = np.concatenate([0 * pre, sin, sin, post], axis=-1)
    rep = LANES // period
    return jnp.asarray(np.tile(c, (1, rep)), F32), jnp.asarray(np.tile(sn, (1, rep)), F32)


def kernel(x_prompt, x_sample, cache_mla_ckv, cache_mla_krope, cache_na_k, cache_na_v, cache_gqa_k, cache_gqa_v, cache_swa_k, cache_swa_v, c, c_ctx, norm_g, w_mod, b_mod, w_in_even, mla_qa_g, w_q_up, mla_kva_g, w_kv_up, mla_q_g, mla_k_g, na_q_g, na_k_g, na_rpb, w_out_even, w_in_odd, gqa_q_g, gqa_k_g, swa_q_g, swa_k_g, swa_sink, w_out_odd):
    n_dec = x_sample.shape[0]
    assert w_mod.shape[0] == 2 and n_dec + 1 <= 8

    cond_t = jnp.concatenate([c_ctx[:, None], c.T, jnp.zeros((D_MODEL, 7 - n_dec), F32)], axis=1)
    n_cond = 1 + n_dec
    gains = (mla_q_g, mla_k_g, na_q_g, na_k_g, gqa_q_g, gqa_k_g, swa_q_g, swa_k_g)
    win_e, wout_e, wq, wkk, wkv, gt, ngt, m_even = _weight_prep(
        jnp.swapaxes(w_in_even[0], 0, 1), w_out_even, w_q_up, w_kv_up, cond_t, n_cond, w_mod, b_mod, norm_g, gains)
    even = (_Row(ngt, 0), win_e, mla_qa_g, wq, mla_kva_g, wkk, wkv,
            _Row(gt, G_MLA_Q), _Row(gt, G_MLA_K), _Row(gt, G_NA_Q), _Row(gt, G_NA_K))
    sink = swa_sink[0].astype(F32)

    xp1, new_ckv, new_krope, new_na_k, new_na_v, m_odd, win_o, wout_o = _prompt_even(
        x_prompt, m_even, *even, wout_e, cond_t, n_cond, w_mod, b_mod, w_in_odd, w_out_odd)
    odd = (_Row(ngt, 1), win_o, _Row(gt, G_GQA_Q), _Row(gt, G_GQA_K), _Row(gt, G_SWA_Q), _Row(gt, G_SWA_K))
    xp2, new_gqa_k, new_gqa_v, new_swa_k, new_swa_v = _prompt_odd(sink, xp1, m_odd, *odd, wout_o)

    cos_e, sin_e = _rope_tables(DEC_SEQ, ROPE_A, LANES, NOPE_A)
    qa, ka, va, qbs, kbs, vbs, g0 = _sample_even_proj(x_sample, m_even, *even, cos_e, sin_e)
    ckr = jnp.swapaxes(cache_mla_krope[:, 0], -1, -2)
    xs1 = _sample_even_attn(na_rpb[0].reshape(-1), x_sample, m_even, qa, ka, va, qbs, kbs, vbs, g0,
                            cache_mla_ckv[:, 0], ckr, _feature_major(cache_na_k[:, 0]),
                            _feature_major(cache_na_v[:, 0]), wkk, wkv, _Row(gt, G_MLA_K), wout_e)
    cos_o, sin_o = _rope_tables(DEC_SEQ, HEAD_DIM, HEAD_DIM, 0)
    qc, kc, vc, qd, kd, vd, g1 = _sample_odd_proj(xs1, m_odd, *odd, cos_o, sin_o)
    xs2 = _sample_odd_attn(sink, xs1, m_odd, qc, kc, vc, qd, kd, vd, g1,
                           _feature_major(cache_gqa_k[:, 0]), _feature_major(cache_gqa_v[:, 0]),
                           _feature_major(cache_swa_k[:, 0]), _feature_major(cache_swa_v[:, 0]), wout_o)

    caches = (new_krope, new_na_k, new_na_v, new_gqa_k, new_gqa_v, new_swa_k, new_swa_v)
    return (xp2, xs2, new_ckv) + tuple(_token_major(c) for c in caches)
```

```python
import functools
from typing import NamedTuple

import jax
import jax.numpy as jnp
import numpy as np
from jax import lax
from jax.experimental import pallas as pl
from jax.experimental.pallas import tpu as pltpu

F32 = jnp.float32
BF16 = jnp.bfloat16

D_MODEL = 1024
SEQ = 256
DEC_SEQ = 1024
PAST_LEN = 256
GRID_W = 64
HEAD_DIM = 64
Q_RANK = 256
KV_RANK = 128
NOPE_A = 64
ROPE_A = 32
QK_A = NOPE_A + ROPE_A
N_HEADS = 8
NA_ROWS = 8
NA_COLS = 16
SWA_HALF = 128
ROPE_THETA = 10000.0
EPS = 1e-6
NEG_INF = -1e30
LOG2E = 1.4426950408889634

LANES = 128
Q_BLOCK = 512
NA_Q_BLOCK = 256
PROJ_BLOCK = 512
PROMPT_BATCHES_PER_STEP = 2
N_PAIRS = N_HEADS // 2
RPB_ROWS = 2 * NA_ROWS - 1
RPB_COLS = 2 * NA_COLS - 1
BIAS_CHUNKS = 16
VMEM_LIMIT = 48 * 1024 * 1024

E_QLAT, E_CKV, E_KROPE, E_GA, E_QB, E_KB, E_VB, E_GB, E_END = 0, 256, 384, 416, 928, 1440, 1952, 2464, 2976
O_QC, O_KC, O_VC, O_GC, O_QD, O_KD, O_VD, O_GD, O_END = 0, 512, 640, 768, 1280, 1792, 1920, 2048, 2560


def _dot(a, b):
    return lax.dot_general(a, b, (((1,), (0,)), ((), ())), preferred_element_type=F32)


def _dot_nt(a, b):
    return lax.dot_general(a, b, (((1,), (1,)), ((), ())), preferred_element_type=F32)


def _silu(x):
    return x / (1.0 + jnp.exp(-x))


def _rms(x, g, n):
    ss = jnp.sum(x * x, axis=-1, keepdims=True)
    return x * lax.rsqrt(ss / n + EPS) * g


def _rms_halves(x, g2, lo):
    x2 = x * x
    s_lo = jnp.sum(jnp.where(lo, x2, 0.0), axis=-1, keepdims=True)
    s_hi = jnp.sum(jnp.where(lo, 0.0, x2), axis=-1, keepdims=True)
    r = jnp.where(lo, lax.rsqrt(s_lo / HEAD_DIM + EPS), lax.rsqrt(s_hi / HEAD_DIM + EPS))
    return x * r * g2


def _modulate(x, g, m):
    d = x.shape[-1]
    xn = x * lax.rsqrt(jnp.mean(x * x, axis=-1, keepdims=True) + EPS) * g
    return xn * (1.0 + m[:, d:2 * d]) + m[:, :d], m[:, 2 * d:]


def _split_lanes(x):
    hi = x.astype(BF16)
    lo = (x - hi.astype(F32)).astype(BF16)
    return jnp.concatenate([hi, lo], axis=1)


def _lane_matrix2(entries):
    i = lax.broadcasted_iota(jnp.int32, (LANES, LANES), 0)
    j = lax.broadcasted_iota(jnp.int32, (LANES, LANES), 1)
    m = entries(i, j).astype(BF16)
    return jnp.concatenate([m, m], axis=0)


def _group_ones2(width):
    shift = width.bit_length() - 1
    return _lane_matrix2(lambda i, j: jnp.where((i >> shift) == (j >> shift), 1.0, 0.0))


def _rope_matrix2(rot_dim, period, start):
    half = rot_dim // 2

    def entries(i, j):
        pos = jnp.bitwise_and(j, period - 1) - start
        neg = (pos >= 0) & (pos < half) & (i == j + half)
        plus = (pos >= half) & (pos < rot_dim) & (i == j - half)
        return jnp.where(neg, -1.0, jnp.where(plus, 1.0, 0.0))

    return _lane_matrix2(entries)


def _swap_matrix2():
    return _lane_matrix2(lambda i, j: jnp.where(i == jnp.bitwise_xor(j, HEAD_DIM), 1.0, 0.0))


def _lane_mix(x, m2):
    return _dot(_split_lanes(x), m2)


def _rms_mxu(x, g, n, ones2):
    x2 = x * x
    nt = x.shape[-1] // LANES
    sq = x2[:, :LANES]
    for t in range(1, nt):
        sq = sq + x2[:, t * LANES:(t + 1) * LANES]
    r = lax.rsqrt(_lane_mix(sq, ones2) / n + EPS)
    return x * (r if nt == 1 else jnp.tile(r, (1, nt))) * g


def _with_ones(v, transposed=False):
    if transposed:
        return jnp.concatenate([v, jnp.ones((LANES, v.shape[1]), v.dtype)], axis=0)
    return jnp.concatenate([v, jnp.ones((v.shape[0], LANES), v.dtype)], axis=1)


def _attend(parts, sink=None):
    mx = None
    for s, _, _ in parts:
        pm = jnp.max(s, axis=-1, keepdims=True)
        mx = pm if mx is None else jnp.maximum(mx, pm)
    if sink is not None:
        mx = jnp.maximum(mx, sink)
    acc, den = None, None
    for s, v, v_t in parts:
        e = jnp.exp2(s - mx)
        po = (_dot_nt if v_t else _dot)(e.astype(BF16), v)
        acc = po if acc is None else acc + po
        if po.shape[1] == LANES:
            ps = jnp.sum(e, axis=-1, keepdims=True)
            den = ps if den is None else den + ps
    if den is None:
        den = acc[:, LANES:]
    if sink is not None:
        den = den + jnp.exp2(sink - mx)
    return acc[:, :LANES] * (1.0 / den)


def _lane_lo():
    return lax.broadcasted_iota(jnp.int32, (1, LANES), 1) < HEAD_DIM


def _store_pair_transposed(ref, bi, p, x):
    xt = x.T
    ref[bi, 0, 2 * p] = xt[:HEAD_DIM]
    ref[bi, 0, 2 * p + 1] = xt[HEAD_DIM:]


def _rope_key_slab(win_ref):
    d = win_ref.shape[1]
    return jnp.concatenate([jnp.zeros((NOPE_A, d), BF16), win_ref[E_KROPE:E_GA, :],
                            jnp.zeros((LANES - QK_A, d), BF16)], axis=0)


def _swap_halves(a):
    return jnp.concatenate([a[HEAD_DIM:], a[:HEAD_DIM]], axis=0)


def _mod_step(n_cond, is_first, bias_row, c_ref, w_ref, o_ref):
    @pl.when(is_first)
    def _():
        o_ref[:n_cond, :] = jnp.broadcast_to(bias_row, (n_cond, o_ref.shape[1]))
        o_ref[n_cond:, :] = jnp.zeros((o_ref.shape[0] - n_cond, o_ref.shape[1]), F32)

    s = _silu(c_ref[...])
    cols = [jnp.broadcast_to(s[:, r:r + 1], (s.shape[0], LANES)) for r in range(n_cond)]
    for t in range(w_ref.shape[1] // LANES):
        sl = slice(t * LANES, (t + 1) * LANES)
        w = w_ref[:, sl]
        for r in range(n_cond):
            o_ref[r:r + 1, sl] += jnp.sum(w * cols[r], axis=0, keepdims=True)


def _mla_keys(cb, kr, wkk_ref, wkv_ref, kg, rope=None):
    kk = _dot(cb, wkk_ref[...])
    keys = []
    for h in range(N_HEADS):
        k = _rms(kk[:, h * LANES:(h + 1) * LANES] + kr, kg, QK_A)
        if rope is not None:
            k = rope(k)
        keys.append(k.astype(BF16))
    return keys, _dot(cb, wkv_ref[...]).astype(BF16)


def _p0_kernel(n_cond, x_ref, m_ref, ng_ref, win_ref, qag_ref, wq_ref, kvag_ref, wkk_ref, wkv_ref, qg_ref, kg_ref,
               naqg_ref, nakg_ref, wout_ref, ct_ref, wm_ref, bm_ref, wio_ref, woo_ref,
               xo_ref, ckv_ref, krope_ref, nak_ref, nav_ref, mo_ref, wino_ref, wouto_ref, y_scr):
    _mod_step(n_cond, pl.program_id(0) == 0, bm_ref[1:2, :], ct_ref, wm_ref, mo_ref)
    wino_ref[...] = wio_ref[0].astype(BF16)
    wouto_ref[...] = woo_ref[0].astype(BF16)

    nbs = x_ref.shape[0]
    x = x_ref[...].reshape(nbs * SEQ, D_MODEL)
    h, gate = _modulate(x, ng_ref[...], m_ref[0:1, :])
    hb = h.astype(BF16)
    lo = _lane_lo()
    hi = jnp.logical_not(lo)
    rows = [slice(bi * SEQ, (bi + 1) * SEQ) for bi in range(nbs)]

    qln = _rms(_dot_nt(hb, win_ref[E_QLAT:E_CKV, :]), qag_ref[...], Q_RANK).astype(BF16)
    q_all = _dot(qln, wq_ref[...])
    ckv_n = _rms(_dot_nt(hb, win_ref[E_CKV:E_KROPE, :]), kvag_ref[...], KV_RANK)
    kr = _dot_nt(hb, _rope_key_slab(win_ref))
    for bi, rs in enumerate(rows):
        ckv_ref[bi, 0] = ckv_n[rs]
        krope_ref[bi, 0] = kr[rs].T[NOPE_A:QK_A]
    keys, vals = _mla_keys(ckv_n.astype(BF16), kr, wkk_ref, wkv_ref, kg_ref[...])
    qg = qg_ref[...] * (QK_A ** -0.5 * LOG2E)

    ga = _dot_nt(hb, win_ref[E_GA:E_QB, :])
    zq = _dot_nt(hb, win_ref[E_QB:E_KB, :])
    zk = _dot_nt(hb, win_ref[E_KB:E_VB, :])
    zv = _dot_nt(hb, win_ref[E_VB:E_GB, :])
    gb = _dot_nt(hb, win_ref[E_GB:E_END, :])
    naqg = naqg_ref[...] * (HEAD_DIM ** -0.5 * LOG2E)

    for p in range(N_PAIRS):
        sl = slice(p * LANES, (p + 1) * LANES)
        ys = slice(4 * LANES + p * LANES, 4 * LANES + (p + 1) * LANES)
        qhs = [_rms(q_all[:, hh * LANES:(hh + 1) * LANES], qg, QK_A).astype(BF16) for hh in (2 * p, 2 * p + 1)]
        qb = _rms_halves(zq[:, sl], naqg, lo)
        kb = _rms_halves(zk[:, sl], nakg_ref[...], lo)
        vb = zv[:, sl]
        kbb, vbb = kb.astype(BF16), vb.astype(BF16)
        va = vals[:, sl]
        qms = [jnp.where(keep, qb, 0.0).astype(BF16) for keep in (lo, hi)]
        for bi, rs in enumerate(rows):
            o2 = [_attend([(_dot_nt(qhs[i][rs], keys[2 * p + i][rs]), va[rs], False)]) for i in (0, 1)]
            y_scr[rs, sl] = (jnp.where(lo, o2[0], o2[1]) * _silu(ga[rs, sl])).astype(BF16)
            _store_pair_transposed(nak_ref, bi, p, kb[rs])
            _store_pair_transposed(nav_ref, bi, p, vb[rs])
            o2 = [_attend([(_dot_nt(qms[i][rs], kbb[rs]), vbb[rs], False)]) for i in (0, 1)]
            y_scr[rs, ys] = (jnp.where(lo, o2[0], o2[1]) * _silu(gb[rs, sl])).astype(BF16)

    xo_ref[...] = (x + gate * _dot(y_scr[...], wout_ref[...])).reshape(nbs, SEQ, D_MODEL)


def _full(shape):
    n = len(shape)
    return pl.BlockSpec(shape, lambda *_: (0,) * n, pipeline_mode=pl.Buffered(1))


class _Row(NamedTuple):
    table: jax.Array
    row: int


def _spec(a):
    if isinstance(a, _Row):
        idx = (a.row,) + (0,) * (a.table.ndim - 1)
        return pl.BlockSpec((None,) + a.table.shape[1:], lambda *_: idx, pipeline_mode=pl.Buffered(1))
    return _full(a.shape)


def _arr(a):
    return a.table if isinstance(a, _Row) else a


def _prompt_even(x, m, ng, win, qag, wq, kvag, wkk, wkv, qg, kg, naqg, nakg, wout,
                 cond_t, n_cond, w_mod, b_mod, w_in_odd, w_out_odd):
    nb = x.shape[0]
    nbs = PROMPT_BATCHES_PER_STEP
    steps = nb // nbs
    assert nb % nbs == 0 and D_MODEL % (16 * steps) == 0
    tr = D_MODEL // steps
    ins = (m, ng, win, qag, wq, kvag, wkk, wkv, qg, kg, naqg, nakg, wout)
    return pl.pallas_call(
        functools.partial(_p0_kernel, n_cond),
        grid=(steps,),
        in_specs=[pl.BlockSpec((nbs, SEQ, D_MODEL), lambda b: (b, 0, 0))] + [_spec(a) for a in ins]
        + [pl.BlockSpec((tr, 8), lambda b: (b, 0)),
           pl.BlockSpec((None, tr, 3 * D_MODEL), lambda b: (1, b, 0)),
           _full(b_mod.shape),
           pl.BlockSpec((1, tr, O_END), lambda b: (0, b, 0)),
           pl.BlockSpec((1, tr, D_MODEL), lambda b: (0, b, 0))],
        out_specs=[pl.BlockSpec((nbs, SEQ, D_MODEL), lambda b: (b, 0, 0)),
                   pl.BlockSpec((nbs, 1, SEQ, KV_RANK), lambda b: (b, 0, 0, 0)),
                   pl.BlockSpec((nbs, 1, ROPE_A, SEQ), lambda b: (b, 0, 0, 0)),
                   pl.BlockSpec((nbs, 1, N_HEADS, HEAD_DIM, SEQ), lambda b: (b, 0, 0, 0, 0)),
                   pl.BlockSpec((nbs, 1, N_HEADS, HEAD_DIM, SEQ), lambda b: (b, 0, 0, 0, 0)),
                   pl.BlockSpec((8, 3 * D_MODEL), lambda b: (0, 0)),
                   pl.BlockSpec((tr, O_END), lambda b: (b, 0)),
                   pl.BlockSpec((tr, D_MODEL), lambda b: (b, 0))],
        out_shape=[jax.ShapeDtypeStruct((nb, SEQ, D_MODEL), F32),
                   jax.ShapeDtypeStruct((nb, 1, SEQ, KV_RANK), F32),
                   jax.ShapeDtypeStruct((nb, 1, ROPE_A, SEQ), F32),
                   jax.ShapeDtypeStruct((nb, 1, N_HEADS, HEAD_DIM, SEQ), F32),
                   jax.ShapeDtypeStruct((nb, 1, N_HEADS, HEAD_DIM, SEQ), F32),
                   jax.ShapeDtypeStruct((8, 3 * D_MODEL), F32),
                   jax.ShapeDtypeStruct((D_MODEL, O_END), BF16),
                   jax.ShapeDtypeStruct((D_MODEL, D_MODEL), BF16)],
        scratch_shapes=[pltpu.VMEM((nbs * SEQ, D_MODEL), BF16)],
        compiler_params=pltpu.CompilerParams(dimension_semantics=("arbitrary",), vmem_limit_bytes=VMEM_LIMIT),
        name="prompt_even",
    )(x, *map(_arr, ins), cond_t, w_mod, b_mod, w_in_odd, w_out_odd)


def _gqa_pair_operands(k, v, kg2, lo):
    kn = _rms_halves(k, kg2, lo)
    return kn, (kn.astype(BF16), pltpu.roll(kn, HEAD_DIM, 1).astype(BF16)), \
        (_with_ones(v.astype(BF16)), _with_ones(pltpu.roll(v, HEAD_DIM, 1).astype(BF16)))


def _p1_kernel(sink_ref, x_ref, m_ref, ng_ref, win_ref, gqg_ref, gkg_ref, sqg_ref, skg_ref, wout_ref,
               xo_ref, gk_ref, gv_ref, sk_ref, sv_ref, y_scr):
    nbs = x_ref.shape[0]
    x = x_ref[...].reshape(nbs * SEQ, D_MODEL)
    h, gate = _modulate(x, ng_ref[...], m_ref[0:1, :])
    hb = h.astype(BF16)
    lo = _lane_lo()
    hi = jnp.logical_not(lo)
    sc = HEAD_DIM ** -0.5 * LOG2E
    rows = [slice(bi * SEQ, (bi + 1) * SEQ) for bi in range(nbs)]

    branches = ((O_QC, O_KC, O_VC, O_GC, gqg_ref, gkg_ref, gk_ref, gv_ref, False, 0),
                (O_QD, O_KD, O_VD, O_GD, sqg_ref, skg_ref, sk_ref, sv_ref, True, 4 * LANES))
    for oq, ok, ov, og, qg_ref, kg_ref, ck_ref, cv_ref, has_sink, yoff in branches:
        zq = _dot(hb, win_ref[:, oq:oq + 4 * LANES])
        zkv = _dot(hb, win_ref[:, ok:ok + 2 * LANES])
        zg = _dot(hb, win_ref[:, og:og + 4 * LANES])
        v = zkv[:, LANES:]
        kn, ks, vs = _gqa_pair_operands(zkv[:, :LANES], v, kg_ref[...], lo)
        for bi, rs in enumerate(rows):
            _store_pair_transposed(ck_ref, bi, 0, kn[rs])
            _store_pair_transposed(cv_ref, bi, 0, v[rs])
        qg = qg_ref[...] * sc
        for p in range(N_PAIRS):
            sl = slice(p * LANES, (p + 1) * LANES)
            qn = _rms_halves(zq[:, sl], qg, lo)
            qms = [jnp.where(keep, qn, 0.0).astype(BF16) for keep in (lo, hi)]
            kv = p // 2
            for bi, rs in enumerate(rows):
                o2 = []
                for half in (0, 1):
                    swap = 0 if kv == half else 1
                    sink = sink_ref[2 * p + half] * LOG2E if has_sink else None
                    o2.append(_attend([(_dot_nt(qms[half][rs], ks[swap][rs]), vs[swap][rs], False)], sink))
                o = jnp.where(lo, o2[0], o2[1])
                y_scr[rs, yoff + p * LANES:yoff + (p + 1) * LANES] = (o * _silu(zg[rs, sl])).astype(BF16)

    xo_ref[...] = (x + gate * _dot(y_scr[...], wout_ref[...])).reshape(nbs, SEQ, D_MODEL)


def _prompt_odd(sink, x, m, ng, win, gqg, gkg, sqg, skg, wout):
    nb = x.shape[0]
    nbs = PROMPT_BATCHES_PER_STEP
    assert nb % nbs == 0
    ins = (m, ng, win, gqg, gkg, sqg, skg, wout)
    cache_spec = pl.BlockSpec((nbs, 1, 2, HEAD_DIM, SEQ), lambda b: (b, 0, 0, 0, 0))
    cache_shape = jax.ShapeDtypeStruct((nb, 1, 2, HEAD_DIM, SEQ), F32)
    return pl.pallas_call(
        _p1_kernel,
        grid=(nb // nbs,),
        in_specs=[pl.BlockSpec(memory_space=pltpu.SMEM),
                  pl.BlockSpec((nbs, SEQ, D_MODEL), lambda b: (b, 0, 0))] + [_spec(a) for a in ins],
        out_specs=[pl.BlockSpec((nbs, SEQ, D_MODEL), lambda b: (b, 0, 0))] + [cache_spec] * 4,
        out_shape=[jax.ShapeDtypeStruct((nb, SEQ, D_MODEL), F32)] + [cache_shape] * 4,
        scratch_shapes=[pltpu.VMEM((nbs * SEQ, D_MODEL), BF16)],
        compiler_params=pltpu.CompilerParams(dimension_semantics=("arbitrary",), vmem_limit_bytes=VMEM_LIMIT),
        name="prompt_odd",
    )(sink, x, *map(_arr, ins))


def _s0a_kernel(x_ref, m_ref, ng_ref, win_ref, qag_ref, wq_ref, kvag_ref, wkk_ref, wkv_ref, qg_ref, kg_ref,
                naqg_ref, nakg_ref, cos_ref, sin_ref,
                qa_ref, ka_ref, va_ref, qb_ref, kb_ref, vb_ref, g_ref):
    b = pl.program_id(0)
    lo = _lane_lo()
    ones_all = _group_ones2(LANES)
    partner = _rope_matrix2(ROPE_A, LANES, NOPE_A)
    hb = _modulate(x_ref[0], ng_ref[...], m_ref[pl.ds(1 + b, 1), :])[0].astype(BF16)
    cos, sin = cos_ref[...], sin_ref[...]

    qln = _rms(_dot_nt(hb, win_ref[E_QLAT:E_CKV, :]), qag_ref[...], Q_RANK).astype(BF16)
    q_all = _dot(qln, wq_ref[...])
    ckv_n = _rms(_dot_nt(hb, win_ref[E_CKV:E_KROPE, :]), kvag_ref[...], KV_RANK)
    kr = _dot_nt(hb, _rope_key_slab(win_ref))
    cb = ckv_n.astype(BF16)
    kk = _dot(cb, wkk_ref[...])
    va_ref[0] = _dot(cb, wkv_ref[...]).astype(BF16)
    zq = _dot_nt(hb, win_ref[E_QB:E_KB, :])
    zk = _dot_nt(hb, win_ref[E_KB:E_VB, :])
    vb_ref[0] = _dot_nt(hb, win_ref[E_VB:E_GB, :]).astype(BF16)
    g_ref[0, :, 0:4 * LANES] = _silu(_dot_nt(hb, win_ref[E_GA:E_QB, :]))
    g_ref[0, :, 4 * LANES:8 * LANES] = _silu(_dot_nt(hb, win_ref[E_GB:E_END, :]))

    qg = qg_ref[...] * (QK_A ** -0.5 * LOG2E)
    kg = kg_ref[...]
    k_partner = _lane_mix(kr * kg, partner) * sin
    for hh in range(N_HEADS):
        sl = slice(hh * LANES, (hh + 1) * LANES)
        qn = _rms_mxu(q_all[:, sl], qg, QK_A, ones_all)
        qa_ref[0, :, sl] = (qn * cos + _lane_mix(qn, partner) * sin).astype(BF16)
        k_raw = kk[:, sl] + kr
        k_inv = lax.rsqrt(_lane_mix(k_raw * k_raw, ones_all) / QK_A + EPS)
        ka_ref[0, :, sl] = ((k_raw * kg * cos + k_partner) * k_inv).astype(BF16)
    naqg = naqg_ref[...] * (HEAD_DIM ** -0.5 * LOG2E)
    for p in range(N_PAIRS):
        sl = slice(p * LANES, (p + 1) * LANES)
        qb_ref[0, :, sl] = _rms_halves(zq[:, sl], naqg, lo).astype(BF16)
        kb_ref[0, :, sl] = _rms_halves(zk[:, sl], nakg_ref[...], lo).astype(BF16)


def _sample_even_proj(x, m, ng, win, qag, wq, kvag, wkk, wkv, qg, kg, naqg, nakg, cos, sin):
    nb, s, _ = x.shape
    nq = s // PROJ_BLOCK
    ins = (m, ng, win, qag, wq, kvag, wkk, wkv, qg, kg, naqg, nakg)
    tab = pl.BlockSpec((PROJ_BLOCK, LANES), lambda b, j: (j, 0))

    def blk(w):
        return pl.BlockSpec((1, PROJ_BLOCK, w), lambda b, j: (b, j, 0))

    def shp(w, dt):
        return jax.ShapeDtypeStruct((nb, s, w), dt)

    return pl.pallas_call(
        _s0a_kernel,
        grid=(nb, nq),
        in_specs=[blk(D_MODEL)] + [_spec(a) for a in ins] + [tab, tab],
        out_specs=[blk(1024), blk(1024), blk(512), blk(512), blk(512), blk(512), blk(1024)],
        out_shape=[shp(1024, BF16), shp(1024, BF16), shp(512, BF16), shp(512, BF16), shp(512, BF16),
                   shp(512, BF16), shp(1024, F32)],
        compiler_params=pltpu.CompilerParams(dimension_semantics=("arbitrary", "arbitrary"),
                                             vmem_limit_bytes=VMEM_LIMIT),
        name="sample_even_proj",
    )(x, *map(_arr, ins), cos, sin)


def _build_bias_table(rpb_ref, tile_scr, tab_ref):
    qc = lax.broadcasted_iota(jnp.int32, (GRID_W, LANES), 0)
    lane = lax.broadcasted_iota(jnp.int32, (GRID_W, LANES), 1)
    kc = jnp.bitwise_and(lane, GRID_W - 1)
    lo = lane < GRID_W
    diff = kc - qc + (NA_COLS - 1)
    cs = jnp.clip(qc - NA_COLS // 2, 0, GRID_W - NA_COLS)
    valid = (kc >= cs) & (kc < cs + NA_COLS)
    tab_ref[...] = jnp.zeros(tab_ref.shape, F32)
    tile_scr[RPB_ROWS] = jnp.zeros((GRID_W, LANES), F32)

    def per_head(h, carry):
        for dr in range(RPB_ROWS):
            t = jnp.zeros((GRID_W, LANES), F32)
            for dc in range(RPB_COLS):
                t = jnp.where(diff == dc, rpb_ref[(h * RPB_ROWS + dr) * RPB_COLS + dc], t)
            tile_scr[dr] = jnp.where(valid, t * LOG2E, NEG_INF)
        for c in range(NA_ROWS // 2, NA_ROWS // 2 + NA_ROWS):
            d0 = 2 * c - NA_ROWS
            tab_ref[0, h, c] = jnp.where(lo, tile_scr[d0], tile_scr[d0 + 1])
            tab_ref[1, h, c] = jnp.where(lo, tile_scr[d0 - 1 if d0 > 0 else RPB_ROWS], tile_scr[d0])
        return carry

    lax.fori_loop(0, N_HEADS, per_head, 0)


def _s0b_kernel(rpb_ref, x_ref, m_ref, qa_ref, ka_ref, va_ref, qb_ref, kb_ref, vb_ref, g_ref,
                cckv_ref, ckr_ref, cnk_ref, cnv_ref, wkk_ref, wkv_ref, kg_ref, wout_ref,
                xo_ref, kca_scr, vca_scr, tile_scr, tab_scr, y_scr):
    b = pl.program_id(0)
    j = pl.program_id(1)
    lo = _lane_lo()
    n_lat = ka_ref.shape[1]

    @pl.when((b == 0) & (j == 0))
    def _():
        _build_bias_table(rpb_ref, tile_scr, tab_scr)

    @pl.when(j == 0)
    def _():
        kr_t = jnp.concatenate([jnp.zeros((NOPE_A, PAST_LEN), F32), ckr_ref[0],
                                jnp.zeros((LANES - QK_A, PAST_LEN), F32)], axis=0)
        keys, vals = _mla_keys(cckv_ref[0].astype(BF16), kr_t.T, wkk_ref, wkv_ref, kg_ref[...])
        for hh in range(N_HEADS):
            kca_scr[:, hh * LANES:(hh + 1) * LANES] = keys[hh]
        vca_scr[...] = vals

    kidx = lax.broadcasted_iota(jnp.int32, (1, n_lat), 1)
    for p in range(N_PAIRS):
        sl = slice(p * LANES, (p + 1) * LANES)
        o2 = []
        va = _with_ones(va_ref[0, :, sl])
        vca = _with_ones(vca_scr[:, sl])
        for hh in (2 * p, 2 * p + 1):
            hs = slice(hh * LANES, (hh + 1) * LANES)
            q = qa_ref[0, :, hs]
            o2.append(_attend([(_dot_nt(q, ka_ref[0, :, hs]), va, False),
                               (_dot_nt(q, kca_scr[:, hs]), vca, False)]))
        oa = jnp.where(lo, o2[0], o2[1])
        y_scr[:, sl] = (oa * g_ref[0, :, sl]).astype(BF16)

        qb = qb_ref[0, :, sl]
        kb = kb_ref[0, :, sl]
        vb = _with_ones(vb_ref[0, :, sl])
        kcb = cnk_ref[0, sl, :].astype(BF16)
        vcb = _with_ones(cnv_ref[0, sl, :].astype(BF16), transposed=True)
        o2 = []
        for half in (0, 1):
            head = 2 * p + half
            qm = jnp.where(lo if half == 0 else jnp.logical_not(lo), qb, jnp.zeros_like(qb))
            s_lat = _dot_nt(qm, kb)
            rows = []
            for local in range(NA_Q_BLOCK // GRID_W):
                qr = j * (NA_Q_BLOCK // GRID_W) + local
                par = 0 if local % 2 == 1 else 1
                c0 = (RPB_ROWS + par - local) // 2 - (NA_Q_BLOCK // GRID_W // 2) * j
                bias = jnp.concatenate([tab_scr[par, head, c0 + t] for t in range(n_lat // LANES)], axis=1)
                r0 = jnp.clip(qr - NA_ROWS // 2, 0, n_lat // GRID_W - NA_ROWS) * GRID_W
                ok = (kidx >= r0) & (kidx < r0 + NA_ROWS * GRID_W)
                rows.append(jnp.where(ok, s_lat[local * GRID_W:(local + 1) * GRID_W] + bias, NEG_INF))
            s_lat = jnp.concatenate(rows, axis=0)
            o2.append(_attend([(s_lat, vb, False), (_dot(qm, kcb), vcb, True)]))
        ob = jnp.where(lo, o2[0], o2[1])
        ys = slice(4 * LANES + p * LANES, 4 * LANES + (p + 1) * LANES)
        y_scr[:, ys] = (ob * g_ref[0, :, ys]).astype(BF16)

    d = x_ref.shape[-1]
    gate = m_ref[pl.ds(1 + b, 1), 2 * d:]
    xo_ref[0] = x_ref[0] + gate * _dot(y_scr[...], wout_ref[...])


def _sample_even_attn(rpb, x, m, qa, ka, va, qb, kb, vb, g, cckv, ckr, cnk, cnv, wkk, wkv, kg, wout):
    nb, s, _ = x.shape
    nq = s // NA_Q_BLOCK

    def blk(w):
        return pl.BlockSpec((1, NA_Q_BLOCK, w), lambda b, j: (b, j, 0))

    def per_batch(a):
        return pl.BlockSpec((1,) + a.shape[1:], lambda b, j: (b, 0, 0))

    return pl.pallas_call(
        _s0b_kernel,
        grid=(nb, nq),
        in_specs=[pl.BlockSpec(memory_space=pltpu.SMEM), blk(D_MODEL), _spec(m),
                  blk(1024), per_batch(ka), per_batch(va), blk(512), per_batch(kb), per_batch(vb), blk(1024),
                  per_batch(cckv), per_batch(ckr), per_batch(cnk), per_batch(cnv),
                  _full(wkk.shape), _full(wkv.shape), _spec(kg), _full(wout.shape)],
        out_specs=blk(D_MODEL),
        out_shape=jax.ShapeDtypeStruct(x.shape, F32),
        scratch_shapes=[pltpu.VMEM((PAST_LEN, N_HEADS * LANES), BF16),
                        pltpu.VMEM((PAST_LEN, N_HEADS * HEAD_DIM), BF16),
                        pltpu.VMEM((RPB_ROWS + 1, GRID_W, LANES), F32),
                        pltpu.VMEM((2, N_HEADS, BIAS_CHUNKS, GRID_W, LANES), F32),
                        pltpu.VMEM((NA_Q_BLOCK, D_MODEL), BF16)],
        compiler_params=pltpu.CompilerParams(dimension_semantics=("arbitrary", "arbitrary"),
                                             vmem_limit_bytes=VMEM_LIMIT),
        name="sample_even_attn",
    )(rpb, x, _arr(m), qa, ka, va, qb, kb, vb, g, cckv, ckr, cnk, cnv, wkk, wkv, _arr(kg), wout)


def _s1a_kernel(x_ref, m_ref, ng_ref, win_ref, gqg_ref, gkg_ref, sqg_ref, skg_ref, cos_ref, sin_ref,
                qc_ref, kc_ref, vc_ref, qd_ref, kd_ref, vd_ref, g_ref):
    b = pl.program_id(0)
    lo = _lane_lo()
    partner = _rope_matrix2(HEAD_DIM, HEAD_DIM, 0)
    swap = _swap_matrix2()[:LANES]
    hb = _modulate(x_ref[0], ng_ref[...], m_ref[pl.ds(1 + b, 1), :])[0].astype(BF16)
    cos, sin = cos_ref[...], sin_ref[...]
    sc = HEAD_DIM ** -0.5 * LOG2E

    def rope(t):
        return t * cos + _lane_mix(t, partner) * sin

    branches = ((O_QC, O_KC, O_GC, gqg_ref, gkg_ref, qc_ref, kc_ref, vc_ref, 0),
                (O_QD, O_KD, O_GD, sqg_ref, skg_ref, qd_ref, kd_ref, vd_ref, 4 * LANES))
    for oq, ok, og, qg_ref, kg_ref, q_out, k_out, v_out, goff in branches:
        zq = _dot(hb, win_ref[:, oq:oq + 4 * LANES])
        zkv = _dot(hb, win_ref[:, ok:ok + 2 * LANES])
        qg = qg_ref[...] * sc
        for p in range(N_PAIRS):
            sl = slice(p * LANES, (p + 1) * LANES)
            q_out[0, :, sl] = rope(_rms_halves(zq[:, sl], qg, lo)).astype(BF16)
        kn = rope(_rms_halves(zkv[:, :LANES], kg_ref[...], lo))
        v = zkv[:, LANES:]
        for out, val in ((k_out, kn.astype(BF16)), (v_out, v.astype(BF16))):
            out[0, :, 0:LANES] = val
            out[0, :, LANES:2 * LANES] = _dot(val, swap).astype(BF16)
        g_ref[0, :, goff:goff + 4 * LANES] = _silu(_dot(hb, win_ref[:, og:og + 4 * LANES]))


def _sample_odd_proj(x, m, ng, win, gqg, gkg, sqg, skg, cos, sin):
    nb, s, _ = x.shape
    nq = s // PROJ_BLOCK
    ins = (m, ng, win, gqg, gkg, sqg, skg)
    tab = pl.BlockSpec((PROJ_BLOCK, LANES), lambda b, j: (j, 0))

    def blk(w):
        return pl.BlockSpec((1, PROJ_BLOCK, w), lambda b, j: (b, j, 0))

    def shp(w, dt):
        return jax.ShapeDtypeStruct((nb, s, w), dt)

    return pl.pallas_call(
        _s1a_kernel,
        grid=(nb, nq),
        in_specs=[blk(D_MODEL)] + [_spec(a) for a in ins] + [tab, tab],
        out_specs=[blk(512), blk(256), blk(256), blk(512), blk(256), blk(256), blk(1024)],
        out_shape=[shp(512, BF16), shp(256, BF16), shp(256, BF16), shp(512, BF16), shp(256, BF16),
                   shp(256, BF16), shp(1024, F32)],
        compiler_params=pltpu.CompilerParams(dimension_semantics=("arbitrary", "arbitrary"),
                                             vmem_limit_bytes=VMEM_LIMIT),
        name="sample_odd_proj",
    )(x, *map(_arr, ins), cos, sin)


def _s1b_kernel(sink_ref, x_ref, m_ref, qc_ref, kc_ref, vc_ref, qd_ref, kd_ref, vd_ref, g_ref,
                cgk_ref, cgv_ref, csk_ref, csv_ref, wout_ref, xo_ref, y_scr):
    b = pl.program_id(0)
    j = pl.program_id(1)
    lo = _lane_lo()
    n_lat = kc_ref.shape[1]
    win_keys = Q_BLOCK + 2 * SWA_HALF

    def ctx_pair(ref, values=False):
        a = ref[0].astype(BF16)
        pair = (a, _swap_halves(a))
        return tuple(_with_ones(t, transposed=True) for t in pair) if values else pair

    cgk, cgv, csk, csv = ctx_pair(cgk_ref), ctx_pair(cgv_ref, True), ctx_pair(csk_ref), ctx_pair(csv_ref, True)
    vcs = [_with_ones(vc_ref[0, :, w * LANES:(w + 1) * LANES]) for w in (0, 1)]

    ks = pl.multiple_of(jnp.clip(j * Q_BLOCK - SWA_HALF, 0, n_lat - win_keys), SWA_HALF)
    qpos = j * Q_BLOCK + lax.broadcasted_iota(jnp.int32, (Q_BLOCK, win_keys), 0)
    kpos = ks + lax.broadcasted_iota(jnp.int32, (Q_BLOCK, win_keys), 1)
    band = jnp.abs(qpos - kpos) <= SWA_HALF
    vds = [_with_ones(vd_ref[0, pl.ds(ks, win_keys), w * LANES:(w + 1) * LANES]) for w in (0, 1)]

    for p in range(N_PAIRS):
        sl = slice(p * LANES, (p + 1) * LANES)
        kv = p // 2
        qc = qc_ref[0, :, sl]
        qd = qd_ref[0, :, sl]
        oc2, od2 = [], []
        for half in (0, 1):
            swap = 0 if kv == half else 1
            ws = slice(swap * LANES, (swap + 1) * LANES)
            keep = lo if half == 0 else jnp.logical_not(lo)
            qm = jnp.where(keep, qc, jnp.zeros_like(qc))
            oc2.append(_attend([(_dot_nt(qm, kc_ref[0, :, ws]), vcs[swap], False),
                                (_dot(qm, cgk[swap]), cgv[swap], True)]))
            qm = jnp.where(keep, qd, jnp.zeros_like(qd))
            s_loc = jnp.where(band, _dot_nt(qm, kd_ref[0, pl.ds(ks, win_keys), ws]), NEG_INF)
            od2.append(_attend([(s_loc, vds[swap], False),
                                (_dot(qm, csk[swap]), csv[swap], True)], sink_ref[2 * p + half] * LOG2E))
        y_scr[:, sl] = (jnp.where(lo, oc2[0], oc2[1]) * g_ref[0, :, sl]).astype(BF16)
        ys = slice(4 * LANES + p * LANES, 4 * LANES + (p + 1) * LANES)
        y_scr[:, ys] = (jnp.where(lo, od2[0], od2[1]) * g_ref[0, :, ys]).astype(BF16)

    d = x_ref.shape[-1]
    gate = m_ref[pl.ds(1 + b, 1), 2 * d:]
    xo_ref[0] = x_ref[0] + gate * _dot(y_scr[...], wout_ref[...])


def _sample_odd_attn(sink, x, m, qc, kc, vc, qd, kd, vd, g, cgk, cgv, csk, csv, wout):
    nb, s, _ = x.shape
    nq = s // Q_BLOCK

    def blk(w):
        return pl.BlockSpec((1, Q_BLOCK, w), lambda b, j: (b, j, 0))

    def per_batch(a):
        return pl.BlockSpec((1,) + a.shape[1:], lambda b, j: (b, 0, 0))

    return pl.pallas_call(
        _s1b_kernel,
        grid=(nb, nq),
        in_specs=[pl.BlockSpec(memory_space=pltpu.SMEM), blk(D_MODEL), _spec(m),
                  blk(512), per_batch(kc), per_batch(vc), blk(512), per_batch(kd), per_batch(vd), blk(1024),
                  per_batch(cgk), per_batch(cgv), per_batch(csk), per_batch(csv), _full(wout.shape)],
        out_specs=blk(D_MODEL),
        out_shape=jax.ShapeDtypeStruct(x.shape, F32),
        scratch_shapes=[pltpu.VMEM((Q_BLOCK, D_MODEL), BF16)],
        compiler_params=pltpu.CompilerParams(dimension_semantics=("arbitrary", "arbitrary"),
                                             vmem_limit_bytes=VMEM_LIMIT),
        name="sample_odd_attn",
    )(sink, x, _arr(m), qc, kc, vc, qd, kd, vd, g, cgk, cgv, csk, csv, wout)


WEIGHT_PREP_STEPS = 8


G_MLA_Q, G_MLA_K, G_NA_Q, G_NA_K, G_GQA_Q, G_GQA_K, G_SWA_Q, G_SWA_K, N_GAINS = range(9)


GAIN_WIDTHS = (QK_A, QK_A) + (HEAD_DIM,) * 6


def _cond_prep_kernel(n_cond, ct_ref, wm_ref, bm_ref, ng_ref, gains_ref, mo_ref, gt_ref, ngt_ref):
    _mod_step(n_cond, pl.program_id(0) == 0, bm_ref[0:1, :], ct_ref, wm_ref, mo_ref)
    gt_ref[...] = jnp.zeros(gt_ref.shape, F32)
    start = 0
    for r, w in enumerate(GAIN_WIDTHS):
        g = gains_ref[:, start:start + w]
        start += w
        for off in range(0, LANES - w + 1, w):
            gt_ref[r, :, off:off + w] = g
    for layer in range(ngt_ref.shape[0]):
        ngt_ref[layer] = ng_ref[layer:layer + 1, :]


def _cond_prep(cond_t, n_cond, w_mod, b_mod, norm_g, gains):
    assert len(gains) == N_GAINS and tuple(g.shape[-1] for g in gains) == GAIN_WIDTHS
    gains_row = jnp.concatenate([g.reshape(1, -1) for g in gains], axis=1)
    tk = 256
    return pl.pallas_call(
        functools.partial(_cond_prep_kernel, n_cond),
        grid=(D_MODEL // tk,),
        in_specs=[pl.BlockSpec((tk, 8), lambda k: (k, 0)),
                  pl.BlockSpec((None, tk, 3 * D_MODEL), lambda k: (0, k, 0)),
                  _full(b_mod.shape), _full(norm_g.shape), _full(gains_row.shape)],
        out_specs=[_full((8, 3 * D_MODEL)), _full((N_GAINS, 1, LANES)), _full((norm_g.shape[0], 1, D_MODEL))],
        out_shape=[jax.ShapeDtypeStruct((8, 3 * D_MODEL), F32), jax.ShapeDtypeStruct((N_GAINS, 1, LANES), F32),
                   jax.ShapeDtypeStruct((norm_g.shape[0], 1, D_MODEL), F32)],
        compiler_params=pltpu.CompilerParams(dimension_semantics=("arbitrary",)),
        name="cond_prep",
    )(cond_t, w_mod, b_mod, norm_g, gains_row)


def _weight_prep_kernel(wie_ref, woe_ref, wqu_ref, wkv_ref, win_e_ref, wout_e_ref, wq_ref, wkk_ref, wkvv_ref):
    win_e_ref[...] = wie_ref[...].astype(BF16)
    wout_e_ref[...] = woe_ref[0].astype(BF16)

    wq_ref[...] = jnp.zeros(wq_ref.shape, BF16)
    for h in range(N_HEADS):
        wq_ref[:, h * LANES:h * LANES + QK_A] = wqu_ref[0, :, h * QK_A:(h + 1) * QK_A].astype(BF16)
    lo = _lane_lo()
    for p in range(N_PAIRS):
        a = wkv_ref[0, :, (2 * p) * LANES:(2 * p + 1) * LANES]
        c = wkv_ref[0, :, (2 * p + 1) * LANES:(2 * p + 2) * LANES]
        wkk_ref[:, (2 * p) * LANES:(2 * p + 1) * LANES] = jnp.where(lo, a, 0.0).astype(BF16)
        wkk_ref[:, (2 * p + 1) * LANES:(2 * p + 2) * LANES] = jnp.where(lo, c, 0.0).astype(BF16)
        wkvv_ref[:, p * LANES:(p + 1) * LANES] = jnp.where(lo, pltpu.roll(a, HEAD_DIM, 1), c).astype(BF16)


def _weight_prep(w_in_even_t, w_out_even, w_q_up, w_kv_up):
    n = WEIGHT_PREP_STEPS
    ins = (w_out_even, w_q_up, w_kv_up)
    out_cols = (D_MODEL, N_HEADS * LANES, N_HEADS * LANES, N_HEADS * HEAD_DIM)
    out_rows = (D_MODEL, Q_RANK, KV_RANK, KV_RANK)
    te = E_END // 6
    assert te * 6 == E_END and te % 16 == 0
    even_spec = pl.BlockSpec((te, D_MODEL), lambda i: (jnp.minimum(i, 5), 0))
    return pl.pallas_call(
        _weight_prep_kernel,
        grid=(n,),
        in_specs=[even_spec] + [pl.BlockSpec((1, a.shape[1] // n, a.shape[2]), lambda i: (0, i, 0)) for a in ins],
        out_specs=[even_spec] + [pl.BlockSpec((r // n, c), lambda i: (i, 0)) for r, c in zip(out_rows, out_cols)],
        out_shape=[jax.ShapeDtypeStruct((E_END, D_MODEL), BF16)]
        + [jax.ShapeDtypeStruct((r, c), BF16) for r, c in zip(out_rows, out_cols)],
        compiler_params=pltpu.CompilerParams(dimension_semantics=("arbitrary",), vmem_limit_bytes=VMEM_LIMIT),
        name="weight_prep",
    )(w_in_even_t, *ins)


def _feature_major(c):
    b, h, l, d = c.shape
    return jnp.swapaxes(c, -1, -2).reshape(b, h * d, l)


def _token_major(c):
    return jnp.swapaxes(c, -1, -2)


def _rope_tables(s, rot_dim, period, start):
    quarter = rot_dim // 4
    t = np.arange(s)
    inv = ROPE_THETA ** (-np.arange(quarter, dtype=np.float64) / quarter)
    row = (t // GRID_W).astype(np.float64)[:, None] * inv
    col = (t % GRID_W).astype(np.float64)[:, None] * inv
    ang = np.concatenate([row, col], axis=-1)
    cos, sin = np.cos(ang), np.sin(ang)
    pre = np.ones((s, start))
    post = np.zeros((s, period - start - rot_dim))
    c = np.concatenate([pre, cos, cos, post], axis=-1)
    sn = np.concatenate([0 * pre, sin, sin, post], axis=-1)
    rep = LANES // period
    return jnp.asarray(np.tile(c, (1, rep)), F32), jnp.asarray(np.tile(sn, (1, rep)), F32)


def kernel(x_prompt, x_sample, cache_mla_ckv, cache_mla_krope, cache_na_k, cache_na_v, cache_gqa_k, cache_gqa_v, cache_swa_k, cache_swa_v, c, c_ctx, norm_g, w_mod, b_mod, w_in_even, mla_qa_g, w_q_up, mla_kva_g, w_kv_up, mla_q_g, mla_k_g, na_q_g, na_k_g, na_rpb, w_out_even, w_in_odd, gqa_q_g, gqa_k_g, swa_q_g, swa_k_g, swa_sink, w_out_odd):
    n_dec = x_sample.shape[0]
    assert w_mod.shape[0] == 2 and n_dec + 1 <= 8

    cond_t = jnp.concatenate([c_ctx[:, None], c.T, jnp.zeros((D_MODEL, 7 - n_dec), F32)], axis=1)
    n_cond = 1 + n_dec
    gains = (mla_q_g, mla_k_g, na_q_g, na_k_g, gqa_q_g, gqa_k_g, swa_q_g, swa_k_g)
    win_e, wout_e, wq, wkk, wkv = _weight_prep(jnp.swapaxes(w_in_even[0], 0, 1), w_out_even, w_q_up, w_kv_up)
    m_even, gt, ngt = _cond_prep(cond_t, n_cond, w_mod, b_mod, norm_g, gains)
    even = (_Row(ngt, 0), win_e, mla_qa_g, wq, mla_kva_g, wkk, wkv,
            _Row(gt, G_MLA_Q), _Row(gt, G_MLA_K), _Row(gt, G_NA_Q), _Row(gt, G_NA_K))
    sink = swa_sink[0].astype(F32)

    xp1, new_ckv, new_krope, new_na_k, new_na_v, m_odd, win_o, wout_o = _prompt_even(
        x_prompt, m_even, *even, wout_e, cond_t, n_cond, w_mod, b_mod, w_in_odd, w_out_odd)
    odd = (_Row(ngt, 1), win_o, _Row(gt, G_GQA_Q), _Row(gt, G_GQA_K), _Row(gt, G_SWA_Q), _Row(gt, G_SWA_K))
    xp2, new_gqa_k, new_gqa_v, new_swa_k, new_swa_v = _prompt_odd(sink, xp1, m_odd, *odd, wout_o)

    cos_e, sin_e = _rope_tables(DEC_SEQ, ROPE_A, LANES, NOPE_A)
    qa, ka, va, qbs, kbs, vbs, g0 = _sample_even_proj(x_sample, m_even, *even, cos_e, sin_e)
    ckr = jnp.swapaxes(cache_mla_krope[:, 0], -1, -2)
    xs1 = _sample_even_attn(na_rpb[0].reshape(-1), x_sample, m_even, qa, ka, va, qbs, kbs, vbs, g0,
                            cache_mla_ckv[:, 0], ckr, _feature_major(cache_na_k[:, 0]),
                            _feature_major(cache_na_v[:, 0]), wkk, wkv, _Row(gt, G_MLA_K), wout_e)
    cos_o, sin_o = _rope_tables(DEC_SEQ, HEAD_DIM, HEAD_DIM, 0)
    qc, kc, vc, qd, kd, vd, g1 = _sample_odd_proj(xs1, m_odd, *odd, cos_o, sin_o)
    xs2 = _sample_odd_attn(sink, xs1, m_odd, qc, kc, vc, qd, kd, vd, g1,
                           _feature_major(cache_gqa_k[:, 0]), _feature_major(cache_gqa_v[:, 0]),
                           _feature_major(cache_swa_k[:, 0]), _feature_major(cache_swa_v[:, 0]), wout_o)

    caches = (new_krope, new_na_k, new_na_v, new_gqa_k, new_gqa_v, new_swa_k, new_swa_v)
    return (xp2, xs2, new_ckv) + tuple(_token_major(c) for c in caches)
```

```python
import functools
from typing import NamedTuple

import jax
import jax.numpy as jnp
import numpy as np
from jax import lax
from jax.experimental import pallas as pl
from jax.experimental.pallas import tpu as pltpu

F32 = jnp.float32
BF16 = jnp.bfloat16

D_MODEL = 1024
SEQ = 256
DEC_SEQ = 1024
PAST_LEN = 256
GRID_W = 64
HEAD_DIM = 64
Q_RANK = 256
KV_RANK = 128
NOPE_A = 64
ROPE_A = 32
QK_A = NOPE_A + ROPE_A
N_HEADS = 8
NA_ROWS = 8
NA_COLS = 16
SWA_HALF = 128
ROPE_THETA = 10000.0
EPS = 1e-6
NEG_INF = -1e30
LOG2E = 1.4426950408889634

LANES = 128
Q_BLOCK = 512
NA_Q_BLOCK = 256
PROJ_BLOCK = 512
PROMPT_BATCHES_PER_STEP = 2
N_PAIRS = N_HEADS // 2
RPB_ROWS = 2 * NA_ROWS - 1
RPB_COLS = 2 * NA_COLS - 1
BIAS_CHUNKS = 16
VMEM_LIMIT = 48 * 1024 * 1024

E_QLAT, E_CKV, E_KROPE, E_GA, E_QB, E_KB, E_VB, E_GB, E_END = 0, 256, 384, 416, 928, 1440, 1952, 2464, 2976
O_QC, O_KC, O_VC, O_GC, O_QD, O_KD, O_VD, O_GD, O_END = 0, 512, 640, 768, 1280, 1792, 1920, 2048, 2560


def _dot(a, b):
    return lax.dot_general(a, b, (((1,), (0,)), ((), ())), preferred_element_type=F32)


def _dot_nt(a, b):
    return lax.dot_general(a, b, (((1,), (1,)), ((), ())), preferred_element_type=F32)


def _silu(x):
    return x / (1.0 + jnp.exp(-x))


def _rms(x, g, n):
    ss = jnp.sum(x * x, axis=-1, keepdims=True)
    return x * lax.rsqrt(ss / n + EPS) * g


def _rms_halves(x, g2, lo):
    x2 = x * x
    s_lo = jnp.sum(jnp.where(lo, x2, 0.0), axis=-1, keepdims=True)
    s_hi = jnp.sum(jnp.where(lo, 0.0, x2), axis=-1, keepdims=True)
    r = jnp.where(lo, lax.rsqrt(s_lo / HEAD_DIM + EPS), lax.rsqrt(s_hi / HEAD_DIM + EPS))
    return x * r * g2


def _modulate(x, g, m):
    d = x.shape[-1]
    xn = x * lax.rsqrt(jnp.mean(x * x, axis=-1, keepdims=True) + EPS) * g
    return xn * (1.0 + m[:, d:2 * d]) + m[:, :d], m[:, 2 * d:]


def _split_lanes(x):
    hi = x.astype(BF16)
    lo = (x - hi.astype(F32)).astype(BF16)
    return jnp.concatenate([hi, lo], axis=1)


def _lane_matrix2(entries):
    i = lax.broadcasted_iota(jnp.int32, (LANES, LANES), 0)
    j = lax.broadcasted_iota(jnp.int32, (LANES, LANES), 1)
    m = entries(i, j).astype(BF16)
    return jnp.concatenate([m, m], axis=0)


def _rope_matrix2(rot_dim, period, start):
    half = rot_dim // 2

    def entries(i, j):
        pos = jnp.bitwise_and(j, period - 1) - start
        neg = (pos >= 0) & (pos < half) & (i == j + half)
        plus = (pos >= half) & (pos < rot_dim) & (i == j - half)
        return jnp.where(neg, -1.0, jnp.where(plus, 1.0, 0.0))

    return _lane_matrix2(entries)


def _swap_matrix2():
    return _lane_matrix2(lambda i, j: jnp.where(i == jnp.bitwise_xor(j, HEAD_DIM), 1.0, 0.0))


def _lane_mix(x, m2):
    return _dot(_split_lanes(x), m2)


def _with_ones(v, transposed=False):
    if transposed:
        return jnp.concatenate([v, jnp.ones((LANES, v.shape[1]), v.dtype)], axis=0)
    return jnp.concatenate([v, jnp.ones((v.shape[0], LANES), v.dtype)], axis=1)


def _attend(parts, sink=None):
    mx = None
    for s, _, _ in parts:
        pm = jnp.max(s, axis=-1, keepdims=True)
        mx = pm if mx is None else jnp.maximum(mx, pm)
    if sink is not None:
        mx = jnp.maximum(mx, sink)
    acc, den = None, None
    for s, v, v_t in parts:
        e = jnp.exp2(s - mx)
        po = (_dot_nt if v_t else _dot)(e.astype(BF16), v)
        acc = po if acc is None else acc + po
        if po.shape[1] == LANES:
            ps = jnp.sum(e, axis=-1, keepdims=True)
            den = ps if den is None else den + ps
    if den is None:
        den = acc[:, LANES:]
    if sink is not None:
        den = den + jnp.exp2(sink - mx)
    return acc[:, :LANES] * (1.0 / den)


def _lane_lo():
    return lax.broadcasted_iota(jnp.int32, (1, LANES), 1) < HEAD_DIM


def _store_pair_transposed(ref, bi, p, x):
    xt = x.T
    ref[bi, 0, 2 * p] = xt[:HEAD_DIM]
    ref[bi, 0, 2 * p + 1] = xt[HEAD_DIM:]


def _rope_key_slab(win_ref):
    d = win_ref.shape[1]
    return jnp.concatenate([jnp.zeros((NOPE_A, d), BF16), win_ref[E_KROPE:E_GA, :],
                            jnp.zeros((LANES - QK_A, d), BF16)], axis=0)


def _swap_halves(a):
    return jnp.concatenate([a[HEAD_DIM:], a[:HEAD_DIM]], axis=0)


def _mod_step(n_cond, is_first, bias_row, c_ref, w_ref, o_ref):
    @pl.when(is_first)
    def _():
        o_ref[:n_cond, :] = jnp.broadcast_to(bias_row, (n_cond, o_ref.shape[1]))
        o_ref[n_cond:, :] = jnp.zeros((o_ref.shape[0] - n_cond, o_ref.shape[1]), F32)

    s = _silu(c_ref[...])
    cols = [jnp.broadcast_to(s[:, r:r + 1], (s.shape[0], LANES)) for r in range(n_cond)]
    for t in range(w_ref.shape[1] // LANES):
        sl = slice(t * LANES, (t + 1) * LANES)
        w = w_ref[:, sl]
        for r in range(n_cond):
            o_ref[r:r + 1, sl] += jnp.sum(w * cols[r], axis=0, keepdims=True)


def _mla_keys(cb, kr, wkk_ref, wkv_ref, kg, rope=None):
    kk = _dot(cb, wkk_ref[...])
    keys = []
    for h in range(N_HEADS):
        k = _rms(kk[:, h * LANES:(h + 1) * LANES] + kr, kg, QK_A)
        if rope is not None:
            k = rope(k)
        keys.append(k.astype(BF16))
    return keys, _dot(cb, wkv_ref[...]).astype(BF16)


def _p0_kernel(n_cond, x_ref, m_ref, ng_ref, win_ref, qag_ref, wq_ref, kvag_ref, wkk_ref, wkv_ref, qg_ref, kg_ref,
               naqg_ref, nakg_ref, wout_ref, ct_ref, wm_ref, bm_ref, wio_ref, woo_ref,
               xo_ref, ckv_ref, krope_ref, nak_ref, nav_ref, mo_ref, wino_ref, wouto_ref, y_scr):
    _mod_step(n_cond, pl.program_id(0) == 0, bm_ref[1:2, :], ct_ref, wm_ref, mo_ref)
    wino_ref[...] = wio_ref[0].astype(BF16)
    wouto_ref[...] = woo_ref[0].astype(BF16)

    nbs = x_ref.shape[0]
    x = x_ref[...].reshape(nbs * SEQ, D_MODEL)
    h, gate = _modulate(x, ng_ref[...], m_ref[0:1, :])
    hb = h.astype(BF16)
    lo = _lane_lo()
    hi = jnp.logical_not(lo)
    rows = [slice(bi * SEQ, (bi + 1) * SEQ) for bi in range(nbs)]

    qln = _rms(_dot_nt(hb, win_ref[E_QLAT:E_CKV, :]), qag_ref[...], Q_RANK).astype(BF16)
    q_all = _dot(qln, wq_ref[...])
    ckv_n = _rms(_dot_nt(hb, win_ref[E_CKV:E_KROPE, :]), kvag_ref[...], KV_RANK)
    kr = _dot_nt(hb, _rope_key_slab(win_ref))
    for bi, rs in enumerate(rows):
        ckv_ref[bi, 0] = ckv_n[rs]
        krope_ref[bi, 0] = kr[rs].T[NOPE_A:QK_A]
    keys, vals = _mla_keys(ckv_n.astype(BF16), kr, wkk_ref, wkv_ref, kg_ref[...])
    qg = qg_ref[...] * (QK_A ** -0.5 * LOG2E)

    ga = _dot_nt(hb, win_ref[E_GA:E_QB, :])
    zq = _dot_nt(hb, win_ref[E_QB:E_KB, :])
    zk = _dot_nt(hb, win_ref[E_KB:E_VB, :])
    zv = _dot_nt(hb, win_ref[E_VB:E_GB, :])
    gb = _dot_nt(hb, win_ref[E_GB:E_END, :])
    naqg = naqg_ref[...] * (HEAD_DIM ** -0.5 * LOG2E)

    for p in range(N_PAIRS):
        sl = slice(p * LANES, (p + 1) * LANES)
        ys = slice(4 * LANES + p * LANES, 4 * LANES + (p + 1) * LANES)
        qhs = [_rms(q_all[:, hh * LANES:(hh + 1) * LANES], qg, QK_A).astype(BF16) for hh in (2 * p, 2 * p + 1)]
        qb = _rms_halves(zq[:, sl], naqg, lo)
        kb = _rms_halves(zk[:, sl], nakg_ref[...], lo)
        vb = zv[:, sl]
        kbb, vbb = kb.astype(BF16), vb.astype(BF16)
        va = vals[:, sl]
        qms = [jnp.where(keep, qb, 0.0).astype(BF16) for keep in (lo, hi)]
        for bi, rs in enumerate(rows):
            o2 = [_attend([(_dot_nt(qhs[i][rs], keys[2 * p + i][rs]), va[rs], False)]) for i in (0, 1)]
            y_scr[rs, sl] = (jnp.where(lo, o2[0], o2[1]) * _silu(ga[rs, sl])).astype(BF16)
            _store_pair_transposed(nak_ref, bi, p, kb[rs])
            _store_pair_transposed(nav_ref, bi, p, vb[rs])
            o2 = [_attend([(_dot_nt(qms[i][rs], kbb[rs]), vbb[rs], False)]) for i in (0, 1)]
            y_scr[rs, ys] = (jnp.where(lo, o2[0], o2[1]) * _silu(gb[rs, sl])).astype(BF16)

    xo_ref[...] = (x + gate * _dot(y_scr[...], wout_ref[...])).reshape(nbs, SEQ, D_MODEL)


def _full(shape):
    n = len(shape)
    return pl.BlockSpec(shape, lambda *_: (0,) * n, pipeline_mode=pl.Buffered(1))


class _Row(NamedTuple):
    table: jax.Array
    row: int


def _spec(a):
    if isinstance(a, _Row):
        idx = (a.row,) + (0,) * (a.table.ndim - 1)
        return pl.BlockSpec((None,) + a.table.shape[1:], lambda *_: idx, pipeline_mode=pl.Buffered(1))
    return _full(a.shape)


def _arr(a):
    return a.table if isinstance(a, _Row) else a


def _prompt_even(x, m, ng, win, qag, wq, kvag, wkk, wkv, qg, kg, naqg, nakg, wout,
                 cond_t, n_cond, w_mod, b_mod, w_in_odd, w_out_odd):
    nb = x.shape[0]
    nbs = PROMPT_BATCHES_PER_STEP
    steps = nb // nbs
    assert nb % nbs == 0 and D_MODEL % (16 * steps) == 0
    tr = D_MODEL // steps
    ins = (m, ng, win, qag, wq, kvag, wkk, wkv, qg, kg, naqg, nakg, wout)
    return pl.pallas_call(
        functools.partial(_p0_kernel, n_cond),
        grid=(steps,),
        in_specs=[pl.BlockSpec((nbs, SEQ, D_MODEL), lambda b: (b, 0, 0))] + [_spec(a) for a in ins]
        + [pl.BlockSpec((tr, 8), lambda b: (b, 0)),
           pl.BlockSpec((None, tr, 3 * D_MODEL), lambda b: (1, b, 0)),
           _full(b_mod.shape),
           pl.BlockSpec((1, tr, O_END), lambda b: (0, b, 0)),
           pl.BlockSpec((1, tr, D_MODEL), lambda b: (0, b, 0))],
        out_specs=[pl.BlockSpec((nbs, SEQ, D_MODEL), lambda b: (b, 0, 0)),
                   pl.BlockSpec((nbs, 1, SEQ, KV_RANK), lambda b: (b, 0, 0, 0)),
                   pl.BlockSpec((nbs, 1, ROPE_A, SEQ), lambda b: (b, 0, 0, 0)),
                   pl.BlockSpec((nbs, 1, N_HEADS, HEAD_DIM, SEQ), lambda b: (b, 0, 0, 0, 0)),
                   pl.BlockSpec((nbs, 1, N_HEADS, HEAD_DIM, SEQ), lambda b: (b, 0, 0, 0, 0)),
                   pl.BlockSpec((8, 3 * D_MODEL), lambda b: (0, 0)),
                   pl.BlockSpec((tr, O_END), lambda b: (b, 0)),
                   pl.BlockSpec((tr, D_MODEL), lambda b: (b, 0))],
        out_shape=[jax.ShapeDtypeStruct((nb, SEQ, D_MODEL), F32),
                   jax.ShapeDtypeStruct((nb, 1, SEQ, KV_RANK), F32),
                   jax.ShapeDtypeStruct((nb, 1, ROPE_A, SEQ), F32),
                   jax.ShapeDtypeStruct((nb, 1, N_HEADS, HEAD_DIM, SEQ), F32),
                   jax.ShapeDtypeStruct((nb, 1, N_HEADS, HEAD_DIM, SEQ), F32),
                   jax.ShapeDtypeStruct((8, 3 * D_MODEL), F32),
                   jax.ShapeDtypeStruct((D_MODEL, O_END), BF16),
                   jax.ShapeDtypeStruct((D_MODEL, D_MODEL), BF16)],
        scratch_shapes=[pltpu.VMEM((nbs * SEQ, D_MODEL), BF16)],
        compiler_params=pltpu.CompilerParams(dimension_semantics=("arbitrary",), vmem_limit_bytes=VMEM_LIMIT),
        name="prompt_even",
    )(x, *map(_arr, ins), cond_t, w_mod, b_mod, w_in_odd, w_out_odd)


def _gqa_pair_operands(k, v, kg2, lo):
    kn = _rms_halves(k, kg2, lo)
    return kn, (kn.astype(BF16), pltpu.roll(kn, HEAD_DIM, 1).astype(BF16)), \
        (_with_ones(v.astype(BF16)), _with_ones(pltpu.roll(v, HEAD_DIM, 1).astype(BF16)))


def _p1_kernel(sink_ref, x_ref, m_ref, ng_ref, win_ref, gqg_ref, gkg_ref, sqg_ref, skg_ref, wout_ref,
               xo_ref, gk_ref, gv_ref, sk_ref, sv_ref, y_scr):
    nbs = x_ref.shape[0]
    x = x_ref[...].reshape(nbs * SEQ, D_MODEL)
    h, gate = _modulate(x, ng_ref[...], m_ref[0:1, :])
    hb = h.astype(BF16)
    lo = _lane_lo()
    hi = jnp.logical_not(lo)
    sc = HEAD_DIM ** -0.5 * LOG2E
    rows = [slice(bi * SEQ, (bi + 1) * SEQ) for bi in range(nbs)]

    branches = ((O_QC, O_KC, O_VC, O_GC, gqg_ref, gkg_ref, gk_ref, gv_ref, False, 0),
                (O_QD, O_KD, O_VD, O_GD, sqg_ref, skg_ref, sk_ref, sv_ref, True, 4 * LANES))
    for oq, ok, ov, og, qg_ref, kg_ref, ck_ref, cv_ref, has_sink, yoff in branches:
        zq = _dot(hb, win_ref[:, oq:oq + 4 * LANES])
        zkv = _dot(hb, win_ref[:, ok:ok + 2 * LANES])
        zg = _dot(hb, win_ref[:, og:og + 4 * LANES])
        v = zkv[:, LANES:]
        kn, ks, vs = _gqa_pair_operands(zkv[:, :LANES], v, kg_ref[...], lo)
        for bi, rs in enumerate(rows):
            _store_pair_transposed(ck_ref, bi, 0, kn[rs])
            _store_pair_transposed(cv_ref, bi, 0, v[rs])
        qg = qg_ref[...] * sc
        for p in range(N_PAIRS):
            sl = slice(p * LANES, (p + 1) * LANES)
            qn = _rms_halves(zq[:, sl], qg, lo)
            qms = [jnp.where(keep, qn, 0.0).astype(BF16) for keep in (lo, hi)]
            kv = p // 2
            for bi, rs in enumerate(rows):
                o2 = []
                for half in (0, 1):
                    swap = 0 if kv == half else 1
                    sink = sink_ref[2 * p + half] * LOG2E if has_sink else None
                    o2.append(_attend([(_dot_nt(qms[half][rs], ks[swap][rs]), vs[swap][rs], False)], sink))
                o = jnp.where(lo, o2[0], o2[1])
                y_scr[rs, yoff + p * LANES:yoff + (p + 1) * LANES] = (o * _silu(zg[rs, sl])).astype(BF16)

    xo_ref[...] = (x + gate * _dot(y_scr[...], wout_ref[...])).reshape(nbs, SEQ, D_MODEL)


def _prompt_odd(sink, x, m, ng, win, gqg, gkg, sqg, skg, wout):
    nb = x.shape[0]
    nbs = PROMPT_BATCHES_PER_STEP
    assert nb % nbs == 0
    ins = (m, ng, win, gqg, gkg, sqg, skg, wout)
    cache_spec = pl.BlockSpec((nbs, 1, 2, HEAD_DIM, SEQ), lambda b: (b, 0, 0, 0, 0))
    cache_shape = jax.ShapeDtypeStruct((nb, 1, 2, HEAD_DIM, SEQ), F32)
    return pl.pallas_call(
        _p1_kernel,
        grid=(nb // nbs,),
        in_specs=[pl.BlockSpec(memory_space=pltpu.SMEM),
                  pl.BlockSpec((nbs, SEQ, D_MODEL), lambda b: (b, 0, 0))] + [_spec(a) for a in ins],
        out_specs=[pl.BlockSpec((nbs, SEQ, D_MODEL), lambda b: (b, 0, 0))] + [cache_spec] * 4,
        out_shape=[jax.ShapeDtypeStruct((nb, SEQ, D_MODEL), F32)] + [cache_shape] * 4,
        scratch_shapes=[pltpu.VMEM((nbs * SEQ, D_MODEL), BF16)],
        compiler_params=pltpu.CompilerParams(dimension_semantics=("arbitrary",), vmem_limit_bytes=VMEM_LIMIT),
        name="prompt_odd",
    )(sink, x, *map(_arr, ins))


def _s0a_kernel(x_ref, m_ref, ng_ref, win_ref, qag_ref, wq_ref, kvag_ref, wkk_ref, wkv_ref, qg_ref, kg_ref,
                naqg_ref, nakg_ref, cos_ref, sin_ref,
                qa_ref, ka_ref, va_ref, qb_ref, kb_ref, vb_ref, g_ref):
    b = pl.program_id(0)
    lo = _lane_lo()
    partner = _rope_matrix2(ROPE_A, LANES, NOPE_A)
    hb = _modulate(x_ref[0], ng_ref[...], m_ref[pl.ds(1 + b, 1), :])[0].astype(BF16)
    cos, sin = cos_ref[...], sin_ref[...]

    qln = _rms(_dot_nt(hb, win_ref[E_QLAT:E_CKV, :]), qag_ref[...], Q_RANK).astype(BF16)
    q_all = _dot(qln, wq_ref[...])
    ckv_n = _rms(_dot_nt(hb, win_ref[E_CKV:E_KROPE, :]), kvag_ref[...], KV_RANK)
    kr = _dot_nt(hb, _rope_key_slab(win_ref))
    cb = ckv_n.astype(BF16)
    kk = _dot(cb, wkk_ref[...])
    va_ref[0] = _dot(cb, wkv_ref[...]).astype(BF16)
    zq = _dot_nt(hb, win_ref[E_QB:E_KB, :])
    zk = _dot_nt(hb, win_ref[E_KB:E_VB, :])
    vb_ref[0] = _dot_nt(hb, win_ref[E_VB:E_GB, :]).astype(BF16)
    g_ref[0, :, 0:4 * LANES] = _silu(_dot_nt(hb, win_ref[E_GA:E_QB, :]))
    g_ref[0, :, 4 * LANES:8 * LANES] = _silu(_dot_nt(hb, win_ref[E_GB:E_END, :]))

    qg = qg_ref[...] * (QK_A ** -0.5 * LOG2E)
    kg = kg_ref[...]
    k_partner = _lane_mix(kr * kg, partner) * sin
    for hh in range(N_HEADS):
        sl = slice(hh * LANES, (hh + 1) * LANES)
        qn = _rms(q_all[:, sl], qg, QK_A)
        qa_ref[0, :, sl] = (qn * cos + _lane_mix(qn, partner) * sin).astype(BF16)
        k_raw = kk[:, sl] + kr
        k_inv = lax.rsqrt(jnp.sum(k_raw * k_raw, axis=-1, keepdims=True) / QK_A + EPS)
        ka_ref[0, :, sl] = ((k_raw * kg * cos + k_partner) * k_inv).astype(BF16)
    naqg = naqg_ref[...] * (HEAD_DIM ** -0.5 * LOG2E)
    for p in range(N_PAIRS):
        sl = slice(p * LANES, (p + 1) * LANES)
        qb_ref[0, :, sl] = _rms_halves(zq[:, sl], naqg, lo).astype(BF16)
        kb_ref[0, :, sl] = _rms_halves(zk[:, sl], nakg_ref[...], lo).astype(BF16)


def _sample_even_proj(x, m, ng, win, qag, wq, kvag, wkk, wkv, qg, kg, naqg, nakg, cos, sin):
    nb, s, _ = x.shape
    nq = s // PROJ_BLOCK
    ins = (m, ng, win, qag, wq, kvag, wkk, wkv, qg, kg, naqg, nakg)
    tab = pl.BlockSpec((PROJ_BLOCK, LANES), lambda b, j: (j, 0))

    def blk(w):
        return pl.BlockSpec((1, PROJ_BLOCK, w), lambda b, j: (b, j, 0))

    def shp(w, dt):
        return jax.ShapeDtypeStruct((nb, s, w), dt)

    return pl.pallas_call(
        _s0a_kernel,
        grid=(nb, nq),
        in_specs=[blk(D_MODEL)] + [_spec(a) for a in ins] + [tab, tab],
        out_specs=[blk(1024), blk(1024), blk(512), blk(512), blk(512), blk(512), blk(1024)],
        out_shape=[shp(1024, BF16), shp(1024, BF16), shp(512, BF16), shp(512, BF16), shp(512, BF16),
                   shp(512, BF16), shp(1024, F32)],
        compiler_params=pltpu.CompilerParams(dimension_semantics=("arbitrary", "arbitrary"),
                                             vmem_limit_bytes=VMEM_LIMIT),
        name="sample_even_proj",
    )(x, *map(_arr, ins), cos, sin)


def _build_bias_table(rpb_ref, tile_scr, tab_ref):
    qc = lax.broadcasted_iota(jnp.int32, (GRID_W, LANES), 0)
    lane = lax.broadcasted_iota(jnp.int32, (GRID_W, LANES), 1)
    kc = jnp.bitwise_and(lane, GRID_W - 1)
    lo = lane < GRID_W
    diff = kc - qc + (NA_COLS - 1)
    cs = jnp.clip(qc - NA_COLS // 2, 0, GRID_W - NA_COLS)
    valid = (kc >= cs) & (kc < cs + NA_COLS)
    tab_ref[...] = jnp.zeros(tab_ref.shape, F32)
    tile_scr[RPB_ROWS] = jnp.zeros((GRID_W, LANES), F32)

    def per_head(h, carry):
        for dr in range(RPB_ROWS):
            t = jnp.zeros((GRID_W, LANES), F32)
            for dc in range(RPB_COLS):
                t = jnp.where(diff == dc, rpb_ref[(h * RPB_ROWS + dr) * RPB_COLS + dc], t)
            tile_scr[dr] = jnp.where(valid, t * LOG2E, NEG_INF)
        for c in range(NA_ROWS // 2, NA_ROWS // 2 + NA_ROWS):
            d0 = 2 * c - NA_ROWS
            tab_ref[0, h, c] = jnp.where(lo, tile_scr[d0], tile_scr[d0 + 1])
            tab_ref[1, h, c] = jnp.where(lo, tile_scr[d0 - 1 if d0 > 0 else RPB_ROWS], tile_scr[d0])
        return carry

    lax.fori_loop(0, N_HEADS, per_head, 0)


def _s0b_kernel(rpb_ref, x_ref, m_ref, qa_ref, ka_ref, va_ref, qb_ref, kb_ref, vb_ref, g_ref,
                cckv_ref, ckr_ref, cnk_ref, cnv_ref, wkk_ref, wkv_ref, kg_ref, wout_ref,
                xo_ref, kca_scr, vca_scr, tile_scr, tab_scr, y_scr):
    b = pl.program_id(0)
    j = pl.program_id(1)
    lo = _lane_lo()
    n_lat = ka_ref.shape[1]

    @pl.when((b == 0) & (j == 0))
    def _():
        _build_bias_table(rpb_ref, tile_scr, tab_scr)

    @pl.when(j == 0)
    def _():
        kr_t = jnp.concatenate([jnp.zeros((NOPE_A, PAST_LEN), F32), ckr_ref[0],
                                jnp.zeros((LANES - QK_A, PAST_LEN), F32)], axis=0)
        keys, vals = _mla_keys(cckv_ref[0].astype(BF16), kr_t.T, wkk_ref, wkv_ref, kg_ref[...])
        for hh in range(N_HEADS):
            kca_scr[:, hh * LANES:(hh + 1) * LANES] = keys[hh]
        vca_scr[...] = vals

    kidx = lax.broadcasted_iota(jnp.int32, (1, n_lat), 1)
    for p in range(N_PAIRS):
        sl = slice(p * LANES, (p + 1) * LANES)
        o2 = []
        va = _with_ones(va_ref[0, :, sl])
        vca = _with_ones(vca_scr[:, sl])
        for hh in (2 * p, 2 * p + 1):
            hs = slice(hh * LANES, (hh + 1) * LANES)
            q = qa_ref[0, :, hs]
            o2.append(_attend([(_dot_nt(q, ka_ref[0, :, hs]), va, False),
                               (_dot_nt(q, kca_scr[:, hs]), vca, False)]))
        oa = jnp.where(lo, o2[0], o2[1])
        y_scr[:, sl] = (oa * g_ref[0, :, sl]).astype(BF16)

        qb = qb_ref[0, :, sl]
        kb = kb_ref[0, :, sl]
        vb = _with_ones(vb_ref[0, :, sl])
        kcb = cnk_ref[0, sl, :].astype(BF16)
        vcb = _with_ones(cnv_ref[0, sl, :].astype(BF16), transposed=True)
        o2 = []
        for half in (0, 1):
            head = 2 * p + half
            qm = jnp.where(lo if half == 0 else jnp.logical_not(lo), qb, jnp.zeros_like(qb))
            s_lat = _dot_nt(qm, kb)
            rows = []
            for local in range(NA_Q_BLOCK // GRID_W):
                qr = j * (NA_Q_BLOCK // GRID_W) + local
                par = 0 if local % 2 == 1 else 1
                c0 = (RPB_ROWS + par - local) // 2 - (NA_Q_BLOCK // GRID_W // 2) * j
                bias = jnp.concatenate([tab_scr[par, head, c0 + t] for t in range(n_lat // LANES)], axis=1)
                r0 = jnp.clip(qr - NA_ROWS // 2, 0, n_lat // GRID_W - NA_ROWS) * GRID_W
                ok = (kidx >= r0) & (kidx < r0 + NA_ROWS * GRID_W)
                rows.append(jnp.where(ok, s_lat[local * GRID_W:(local + 1) * GRID_W] + bias, NEG_INF))
            s_lat = jnp.concatenate(rows, axis=0)
            o2.append(_attend([(s_lat, vb, False), (_dot(qm, kcb), vcb, True)]))
        ob = jnp.where(lo, o2[0], o2[1])
        ys = slice(4 * LANES + p * LANES, 4 * LANES + (p + 1) * LANES)
        y_scr[:, ys] = (ob * g_ref[0, :, ys]).astype(BF16)

    d = x_ref.shape[-1]
    gate = m_ref[pl.ds(1 + b, 1), 2 * d:]
    xo_ref[0] = x_ref[0] + gate * _dot(y_scr[...], wout_ref[...])


def _sample_even_attn(rpb, x, m, qa, ka, va, qb, kb, vb, g, cckv, ckr, cnk, cnv, wkk, wkv, kg, wout):
    nb, s, _ = x.shape
    nq = s // NA_Q_BLOCK

    def blk(w):
        return pl.BlockSpec((1, NA_Q_BLOCK, w), lambda b, j: (b, j, 0))

    def per_batch(a):
        return pl.BlockSpec((1,) + a.shape[1:], lambda b, j: (b, 0, 0))

    return pl.pallas_call(
        _s0b_kernel,
        grid=(nb, nq),
        in_specs=[pl.BlockSpec(memory_space=pltpu.SMEM), blk(D_MODEL), _spec(m),
                  blk(1024), per_batch(ka), per_batch(va), blk(512), per_batch(kb), per_batch(vb), blk(1024),
                  per_batch(cckv), per_batch(ckr), per_batch(cnk), per_batch(cnv),
                  _full(wkk.shape), _full(wkv.shape), _spec(kg), _full(wout.shape)],
        out_specs=blk(D_MODEL),
        out_shape=jax.ShapeDtypeStruct(x.shape, F32),
        scratch_shapes=[pltpu.VMEM((PAST_LEN, N_HEADS * LANES), BF16),
                        pltpu.VMEM((PAST_LEN, N_HEADS * HEAD_DIM), BF16),
                        pltpu.VMEM((RPB_ROWS + 1, GRID_W, LANES), F32),
                        pltpu.VMEM((2, N_HEADS, BIAS_CHUNKS, GRID_W, LANES), F32),
                        pltpu.VMEM((NA_Q_BLOCK, D_MODEL), BF16)],
        compiler_params=pltpu.CompilerParams(dimension_semantics=("arbitrary", "arbitrary"),
                                             vmem_limit_bytes=VMEM_LIMIT),
        name="sample_even_attn",
    )(rpb, x, _arr(m), qa, ka, va, qb, kb, vb, g, cckv, ckr, cnk, cnv, wkk, wkv, _arr(kg), wout)


def _s1a_kernel(x_ref, m_ref, ng_ref, win_ref, gqg_ref, gkg_ref, sqg_ref, skg_ref, cos_ref, sin_ref,
                qc_ref, kc_ref, vc_ref, qd_ref, kd_ref, vd_ref, g_ref):
    b = pl.program_id(0)
    lo = _lane_lo()
    partner = _rope_matrix2(HEAD_DIM, HEAD_DIM, 0)
    swap = _swap_matrix2()[:LANES]
    hb = _modulate(x_ref[0], ng_ref[...], m_ref[pl.ds(1 + b, 1), :])[0].astype(BF16)
    cos, sin = cos_ref[...], sin_ref[...]
    sc = HEAD_DIM ** -0.5 * LOG2E

    def rope(t):
        return t * cos + _lane_mix(t, partner) * sin

    branches = ((O_QC, O_KC, O_GC, gqg_ref, gkg_ref, qc_ref, kc_ref, vc_ref, 0),
                (O_QD, O_KD, O_GD, sqg_ref, skg_ref, qd_ref, kd_ref, vd_ref, 4 * LANES))
    for oq, ok, og, qg_ref, kg_ref, q_out, k_out, v_out, goff in branches:
        zq = _dot(hb, win_ref[:, oq:oq + 4 * LANES])
        zkv = _dot(hb, win_ref[:, ok:ok + 2 * LANES])
        qg = qg_ref[...] * sc
        for p in range(N_PAIRS):
            sl = slice(p * LANES, (p + 1) * LANES)
            q_out[0, :, sl] = rope(_rms_halves(zq[:, sl], qg, lo)).astype(BF16)
        kn = rope(_rms_halves(zkv[:, :LANES], kg_ref[...], lo))
        v = zkv[:, LANES:]
        for out, val in ((k_out, kn.astype(BF16)), (v_out, v.astype(BF16))):
            out[0, :, 0:LANES] = val
            out[0, :, LANES:2 * LANES] = _dot(val, swap).astype(BF16)
        g_ref[0, :, goff:goff + 4 * LANES] = _silu(_dot(hb, win_ref[:, og:og + 4 * LANES]))


def _sample_odd_proj(x, m, ng, win, gqg, gkg, sqg, skg, cos, sin):
    nb, s, _ = x.shape
    nq = s // PROJ_BLOCK
    ins = (m, ng, win, gqg, gkg, sqg, skg)
    tab = pl.BlockSpec((PROJ_BLOCK, LANES), lambda b, j: (j, 0))

    def blk(w):
        return pl.BlockSpec((1, PROJ_BLOCK, w), lambda b, j: (b, j, 0))

    def shp(w, dt):
        return jax.ShapeDtypeStruct((nb, s, w), dt)

    return pl.pallas_call(
        _s1a_kernel,
        grid=(nb, nq),
        in_specs=[blk(D_MODEL)] + [_spec(a) for a in ins] + [tab, tab],
        out_specs=[blk(512), blk(256), blk(256), blk(512), blk(256), blk(256), blk(1024)],
        out_shape=[shp(512, BF16), shp(256, BF16), shp(256, BF16), shp(512, BF16), shp(256, BF16),
                   shp(256, BF16), shp(1024, F32)],
        compiler_params=pltpu.CompilerParams(dimension_semantics=("arbitrary", "arbitrary"),
                                             vmem_limit_bytes=VMEM_LIMIT),
        name="sample_odd_proj",
    )(x, *map(_arr, ins), cos, sin)


def _s1b_kernel(sink_ref, x_ref, m_ref, qc_ref, kc_ref, vc_ref, qd_ref, kd_ref, vd_ref, g_ref,
                cgk_ref, cgv_ref, csk_ref, csv_ref, wout_ref, xo_ref, y_scr):
    b = pl.program_id(0)
    j = pl.program_id(1)
    lo = _lane_lo()
    n_lat = kc_ref.shape[1]
    win_keys = Q_BLOCK + 2 * SWA_HALF

    def ctx_pair(ref, values=False):
        a = ref[0].astype(BF16)
        pair = (a, _swap_halves(a))
        return tuple(_with_ones(t, transposed=True) for t in pair) if values else pair

    cgk, cgv, csk, csv = ctx_pair(cgk_ref), ctx_pair(cgv_ref, True), ctx_pair(csk_ref), ctx_pair(csv_ref, True)
    vcs = [_with_ones(vc_ref[0, :, w * LANES:(w + 1) * LANES]) for w in (0, 1)]

    ks = pl.multiple_of(jnp.clip(j * Q_BLOCK - SWA_HALF, 0, n_lat - win_keys), SWA_HALF)
    qpos = j * Q_BLOCK + lax.broadcasted_iota(jnp.int32, (Q_BLOCK, win_keys), 0)
    kpos = ks + lax.broadcasted_iota(jnp.int32, (Q_BLOCK, win_keys), 1)
    band = jnp.abs(qpos - kpos) <= SWA_HALF
    vds = [_with_ones(vd_ref[0, pl.ds(ks, win_keys), w * LANES:(w + 1) * LANES]) for w in (0, 1)]

    for p in range(N_PAIRS):
        sl = slice(p * LANES, (p + 1) * LANES)
        kv = p // 2
        qc = qc_ref[0, :, sl]
        qd = qd_ref[0, :, sl]
        oc2, od2 = [], []
        for half in (0, 1):
            swap = 0 if kv == half else 1
            ws = slice(swap * LANES, (swap + 1) * LANES)
            keep = lo if half == 0 else jnp.logical_not(lo)
            qm = jnp.where(keep, qc, jnp.zeros_like(qc))
            oc2.append(_attend([(_dot_nt(qm, kc_ref[0, :, ws]), vcs[swap], False),
                                (_dot(qm, cgk[swap]), cgv[swap], True)]))
            qm = jnp.where(keep, qd, jnp.zeros_like(qd))
            s_loc = jnp.where(band, _dot_nt(qm, kd_ref[0, pl.ds(ks, win_keys), ws]), NEG_INF)
            od2.append(_attend([(s_loc, vds[swap], False),
                                (_dot(qm, csk[swap]), csv[swap], True)], sink_ref[2 * p + half] * LOG2E))
        y_scr[:, sl] = (jnp.where(lo, oc2[0], oc2[1]) * g_ref[0, :, sl]).astype(BF16)
        ys = slice(4 * LANES + p * LANES, 4 * LANES + (p + 1) * LANES)
        y_scr[:, ys] = (jnp.where(lo, od2[0], od2[1]) * g_ref[0, :, ys]).astype(BF16)

    d = x_ref.shape[-1]
    gate = m_ref[pl.ds(1 + b, 1), 2 * d:]
    xo_ref[0] = x_ref[0] + gate * _dot(y_scr[...], wout_ref[...])


def _sample_odd_attn(sink, x, m, qc, kc, vc, qd, kd, vd, g, cgk, cgv, csk, csv, wout):
    nb, s, _ = x.shape
    nq = s // Q_BLOCK

    def blk(w):
        return pl.BlockSpec((1, Q_BLOCK, w), lambda b, j: (b, j, 0))

    def per_batch(a):
        return pl.BlockSpec((1,) + a.shape[1:], lambda b, j: (b, 0, 0))

    return pl.pallas_call(
        _s1b_kernel,
        grid=(nb, nq),
        in_specs=[pl.BlockSpec(memory_space=pltpu.SMEM), blk(D_MODEL), _spec(m),
                  blk(512), per_batch(kc), per_batch(vc), blk(512), per_batch(kd), per_batch(vd), blk(1024),
                  per_batch(cgk), per_batch(cgv), per_batch(csk), per_batch(csv), _full(wout.shape)],
        out_specs=blk(D_MODEL),
        out_shape=jax.ShapeDtypeStruct(x.shape, F32),
        scratch_shapes=[pltpu.VMEM((Q_BLOCK, D_MODEL), BF16)],
        compiler_params=pltpu.CompilerParams(dimension_semantics=("arbitrary", "arbitrary"),
                                             vmem_limit_bytes=VMEM_LIMIT),
        name="sample_odd_attn",
    )(sink, x, _arr(m), qc, kc, vc, qd, kd, vd, g, cgk, cgv, csk, csv, wout)


WEIGHT_PREP_STEPS = 8


G_MLA_Q, G_MLA_K, G_NA_Q, G_NA_K, G_GQA_Q, G_GQA_K, G_SWA_Q, G_SWA_K, N_GAINS = range(9)


GAIN_WIDTHS = (QK_A, QK_A) + (HEAD_DIM,) * 6


def _cond_prep_kernel(n_cond, ct_ref, wm_ref, bm_ref, ng_ref, gains_ref, mo_ref, gt_ref, ngt_ref):
    _mod_step(n_cond, pl.program_id(0) == 0, bm_ref[0:1, :], ct_ref, wm_ref, mo_ref)
    gt_ref[...] = jnp.zeros(gt_ref.shape, F32)
    start = 0
    for r, w in enumerate(GAIN_WIDTHS):
        g = gains_ref[:, start:start + w]
        start += w
        for off in range(0, LANES - w + 1, w):
            gt_ref[r, :, off:off + w] = g
    for layer in range(ngt_ref.shape[0]):
        ngt_ref[layer] = ng_ref[layer:layer + 1, :]


def _cond_prep(cond_t, n_cond, w_mod, b_mod, norm_g, gains):
    assert len(gains) == N_GAINS and tuple(g.shape[-1] for g in gains) == GAIN_WIDTHS
    gains_row = jnp.concatenate([g.reshape(1, -1) for g in gains], axis=1)
    tk = 256
    return pl.pallas_call(
        functools.partial(_cond_prep_kernel, n_cond),
        grid=(D_MODEL // tk,),
        in_specs=[pl.BlockSpec((tk, 8), lambda k: (k, 0)),
                  pl.BlockSpec((None, tk, 3 * D_MODEL), lambda k: (0, k, 0)),
                  _full(b_mod.shape), _full(norm_g.shape), _full(gains_row.shape)],
        out_specs=[_full((8, 3 * D_MODEL)), _full((N_GAINS, 1, LANES)), _full((norm_g.shape[0], 1, D_MODEL))],
        out_shape=[jax.ShapeDtypeStruct((8, 3 * D_MODEL), F32), jax.ShapeDtypeStruct((N_GAINS, 1, LANES), F32),
                   jax.ShapeDtypeStruct((norm_g.shape[0], 1, D_MODEL), F32)],
        compiler_params=pltpu.CompilerParams(dimension_semantics=("arbitrary",)),
        name="cond_prep",
    )(cond_t, w_mod, b_mod, norm_g, gains_row)


def _weight_prep_kernel(wie_ref, woe_ref, wqu_ref, wkv_ref, win_e_ref, wout_e_ref, wq_ref, wkk_ref, wkvv_ref):
    win_e_ref[...] = wie_ref[...].astype(BF16)
    wout_e_ref[...] = woe_ref[0].astype(BF16)

    wq_ref[...] = jnp.zeros(wq_ref.shape, BF16)
    for h in range(N_HEADS):
        wq_ref[:, h * LANES:h * LANES + QK_A] = wqu_ref[0, :, h * QK_A:(h + 1) * QK_A].astype(BF16)
    lo = _lane_lo()
    for p in range(N_PAIRS):
        a = wkv_ref[0, :, (2 * p) * LANES:(2 * p + 1) * LANES]
        c = wkv_ref[0, :, (2 * p + 1) * LANES:(2 * p + 2) * LANES]
        wkk_ref[:, (2 * p) * LANES:(2 * p + 1) * LANES] = jnp.where(lo, a, 0.0).astype(BF16)
        wkk_ref[:, (2 * p + 1) * LANES:(2 * p + 2) * LANES] = jnp.where(lo, c, 0.0).astype(BF16)
        wkvv_ref[:, p * LANES:(p + 1) * LANES] = jnp.where(lo, pltpu.roll(a, HEAD_DIM, 1), c).astype(BF16)


def _weight_prep(w_in_even_t, w_out_even, w_q_up, w_kv_up):
    n = WEIGHT_PREP_STEPS
    ins = (w_out_even, w_q_up, w_kv_up)
    out_cols = (D_MODEL, N_HEADS * LANES, N_HEADS * LANES, N_HEADS * HEAD_DIM)
    out_rows = (D_MODEL, Q_RANK, KV_RANK, KV_RANK)
    te = E_END // 6
    assert te * 6 == E_END and te % 16 == 0
    even_spec = pl.BlockSpec((te, D_MODEL), lambda i: (jnp.minimum(i, 5), 0))
    return pl.pallas_call(
        _weight_prep_kernel,
        grid=(n,),
        in_specs=[even_spec] + [pl.BlockSpec((1, a.shape[1] // n, a.shape[2]), lambda i: (0, i, 0)) for a in ins],
        out_specs=[even_spec] + [pl.BlockSpec((r // n, c), lambda i: (i, 0)) for r, c in zip(out_rows, out_cols)],
        out_shape=[jax.ShapeDtypeStruct((E_END, D_MODEL), BF16)]
        + [jax.ShapeDtypeStruct((r, c), BF16) for r, c in zip(out_rows, out_cols)],
        compiler_params=pltpu.CompilerParams(dimension_semantics=("arbitrary",), vmem_limit_bytes=VMEM_LIMIT),
        name="weight_prep",
    )(w_in_even_t, *ins)


def _feature_major(c):
    b, h, l, d = c.shape
    return jnp.swapaxes(c, -1, -2).reshape(b, h * d, l)


def _token_major(c):
    return jnp.swapaxes(c, -1, -2)


def _rope_tables(s, rot_dim, period, start):
    quarter = rot_dim // 4
    t = np.arange(s)
    inv = ROPE_THETA ** (-np.arange(quarter, dtype=np.float64) / quarter)
    row = (t // GRID_W).astype(np.float64)[:, None] * inv
    col = (t % GRID_W).astype(np.float64)[:, None] * inv
    ang = np.concatenate([row, col], axis=-1)
    cos, sin = np.cos(ang), np.sin(ang)
    pre = np.ones((s, start))
    post = np.zeros((s, period - start - rot_dim))
    c = np.concatenate([pre, cos, cos, post], axis=-1)
    sn = np.concatenate([0 * pre, sin, sin, post], axis=-1)
    rep = LANES // period
    return jnp.asarray(np.tile(c, (1, rep)), F32), jnp.asarray(np.tile(sn, (1, rep)), F32)


def kernel(x_prompt, x_sample, cache_mla_ckv, cache_mla_krope, cache_na_k, cache_na_v, cache_gqa_k, cache_gqa_v, cache_swa_k, cache_swa_v, c, c_ctx, norm_g, w_mod, b_mod, w_in_even, mla_qa_g, w_q_up, mla_kva_g, w_kv_up, mla_q_g, mla_k_g, na_q_g, na_k_g, na_rpb, w_out_even, w_in_odd, gqa_q_g, gqa_k_g, swa_q_g, swa_k_g, swa_sink, w_out_odd):
    n_dec = x_sample.shape[0]
    assert w_mod.shape[0] == 2 and n_dec + 1 <= 8

    cond_t = jnp.concatenate([c_ctx[:, None], c.T, jnp.zeros((D_MODEL, 7 - n_dec), F32)], axis=1)
    n_cond = 1 + n_dec
    gains = (mla_q_g, mla_k_g, na_q_g, na_k_g, gqa_q_g, gqa_k_g, swa_q_g, swa_k_g)
    win_e, wout_e, wq, wkk, wkv = _weight_prep(jnp.swapaxes(w_in_even[0], 0, 1), w_out_even, w_q_up, w_kv_up)
    m_even, gt, ngt = _cond_prep(cond_t, n_cond, w_mod, b_mod, norm_g, gains)
    even = (_Row(ngt, 0), win_e, mla_qa_g, wq, mla_kva_g, wkk, wkv,
            _Row(gt, G_MLA_Q), _Row(gt, G_MLA_K), _Row(gt, G_NA_Q), _Row(gt, G_NA_K))
    sink = swa_sink[0].astype(F32)

    xp1, new_ckv, new_krope, new_na_k, new_na_v, m_odd, win_o, wout_o = _prompt_even(
        x_prompt, m_even, *even, wout_e, cond_t, n_cond, w_mod, b_mod, w_in_odd, w_out_odd)
    odd = (_Row(ngt, 1), win_o, _Row(gt, G_GQA_Q), _Row(gt, G_GQA_K), _Row(gt, G_SWA_Q), _Row(gt, G_SWA_K))
    xp2, new_gqa_k, new_gqa_v, new_swa_k, new_swa_v = _prompt_odd(sink, xp1, m_odd, *odd, wout_o)

    cos_e, sin_e = _rope_tables(DEC_SEQ, ROPE_A, LANES, NOPE_A)
    qa, ka, va, qbs, kbs, vbs, g0 = _sample_even_proj(x_sample, m_even, *even, cos_e, sin_e)
    ckr = jnp.swapaxes(cache_mla_krope[:, 0], -1, -2)
    xs1 = _sample_even_attn(na_rpb[0].reshape(-1), x_sample, m_even, qa, ka, va, qbs, kbs, vbs, g0,
                            cache_mla_ckv[:, 0], ckr, _feature_major(cache_na_k[:, 0]),
                            _feature_major(cache_na_v[:, 0]), wkk, wkv, _Row(gt, G_MLA_K), wout_e)
    cos_o, sin_o = _rope_tables(DEC_SEQ, HEAD_DIM, HEAD_DIM, 0)
    qc, kc, vc, qd, kd, vd, g1 = _sample_odd_proj(xs1, m_odd, *odd, cos_o, sin_o)
    xs2 = _sample_odd_attn(sink, xs1, m_odd, qc, kc, vc, qd, kd, vd, g1,
                           _feature_major(cache_gqa_k[:, 0]), _feature_major(cache_gqa_v[:, 0]),
                           _feature_major(cache_swa_k[:, 0]), _feature_major(cache_swa_v[:, 0]), wout_o)

    caches = (new_krope, new_na_k, new_na_v, new_gqa_k, new_gqa_v, new_swa_k, new_swa_v)
    return (xp2, xs2, new_ckv) + tuple(_token_major(c) for c in caches)
```

```python
import functools
from typing import NamedTuple

import jax
import jax.numpy as jnp
import numpy as np
from jax import lax
from jax.experimental import pallas as pl
from jax.experimental.pallas import tpu as pltpu

F32 = jnp.float32
BF16 = jnp.bfloat16

D_MODEL = 1024
SEQ = 256
DEC_SEQ = 1024
PAST_LEN = 256
GRID_W = 64
HEAD_DIM = 64
Q_RANK = 256
KV_RANK = 128
NOPE_A = 64
ROPE_A = 32
QK_A = NOPE_A + ROPE_A
N_HEADS = 8
NA_ROWS = 8
NA_COLS = 16
SWA_HALF = 128
ROPE_THETA = 10000.0
EPS = 1e-6
NEG_INF = -1e30
LOG2E = 1.4426950408889634

LANES = 128
Q_BLOCK = 512
NA_Q_BLOCK = 256
PROJ_BLOCK = 512
PROMPT_BATCHES_PER_STEP = 2
N_PAIRS = N_HEADS // 2
RPB_ROWS = 2 * NA_ROWS - 1
RPB_COLS = 2 * NA_COLS - 1
BIAS_CHUNKS = 16
VMEM_LIMIT = 48 * 1024 * 1024

E_QLAT, E_CKV, E_KROPE, E_GA, E_QB, E_KB, E_VB, E_GB, E_END = 0, 256, 384, 416, 928, 1440, 1952, 2464, 2976
O_QC, O_KC, O_VC, O_GC, O_QD, O_KD, O_VD, O_GD, O_END = 0, 512, 640, 768, 1280, 1792, 1920, 2048, 2560


def _dot(a, b):
    return lax.dot_general(a, b, (((1,), (0,)), ((), ())), preferred_element_type=F32)


def _dot_nt(a, b):
    return lax.dot_general(a, b, (((1,), (1,)), ((), ())), preferred_element_type=F32)


def _silu(x):
    return x / (1.0 + jnp.exp(-x))


def _rms(x, g, n):
    ss = jnp.sum(x * x, axis=-1, keepdims=True)
    return x * lax.rsqrt(ss / n + EPS) * g


def _rms_halves(x, g2, lo):
    x2 = x * x
    s_lo = jnp.sum(jnp.where(lo, x2, 0.0), axis=-1, keepdims=True)
    s_hi = jnp.sum(jnp.where(lo, 0.0, x2), axis=-1, keepdims=True)
    r = jnp.where(lo, lax.rsqrt(s_lo / HEAD_DIM + EPS), lax.rsqrt(s_hi / HEAD_DIM + EPS))
    return x * r * g2


def _modulate(x, g, m):
    d = x.shape[-1]
    xn = x * lax.rsqrt(jnp.mean(x * x, axis=-1, keepdims=True) + EPS) * g
    return xn * (1.0 + m[:, d:2 * d]) + m[:, :d], m[:, 2 * d:]


def _split_lanes(x):
    hi = x.astype(BF16)
    lo = (x - hi.astype(F32)).astype(BF16)
    return jnp.concatenate([hi, lo], axis=1)


def _lane_matrix2(entries):
    i = lax.broadcasted_iota(jnp.int32, (LANES, LANES), 0)
    j = lax.broadcasted_iota(jnp.int32, (LANES, LANES), 1)
    m = entries(i, j).astype(BF16)
    return jnp.concatenate([m, m], axis=0)


def _rope_matrix2(rot_dim, period, start):
    half = rot_dim // 2

    def entries(i, j):
        pos = jnp.bitwise_and(j, period - 1) - start
        neg = (pos >= 0) & (pos < half) & (i == j + half)
        plus = (pos >= half) & (pos < rot_dim) & (i == j - half)
        return jnp.where(neg, -1.0, jnp.where(plus, 1.0, 0.0))

    return _lane_matrix2(entries)


def _swap_matrix2():
    return _lane_matrix2(lambda i, j: jnp.where(i == jnp.bitwise_xor(j, HEAD_DIM), 1.0, 0.0))


def _lane_mix(x, m2):
    return _dot(_split_lanes(x), m2)


def _with_ones(v, transposed=False):
    if transposed:
        return jnp.concatenate([v, jnp.ones((LANES, v.shape[1]), v.dtype)], axis=0)
    return jnp.concatenate([v, jnp.ones((v.shape[0], LANES), v.dtype)], axis=1)


def _attend(parts, sink=None):
    mx = None
    for s, _, _ in parts:
        pm = jnp.max(s, axis=-1, keepdims=True)
        mx = pm if mx is None else jnp.maximum(mx, pm)
    if sink is not None:
        mx = jnp.maximum(mx, sink)
    acc, den = None, None
    for s, v, v_t in parts:
        e = jnp.exp2(s - mx)
        po = (_dot_nt if v_t else _dot)(e.astype(BF16), v)
        acc = po if acc is None else acc + po
        if po.shape[1] == LANES:
            ps = jnp.sum(e, axis=-1, keepdims=True)
            den = ps if den is None else den + ps
    if den is None:
        den = acc[:, LANES:]
    if sink is not None:
        den = den + jnp.exp2(sink - mx)
    return acc[:, :LANES] * (1.0 / den)


def _lane_lo():
    return lax.broadcasted_iota(jnp.int32, (1, LANES), 1) < HEAD_DIM


def _store_pair_transposed(ref, bi, p, x):
    xt = x.T
    ref[bi, 0, 2 * p] = xt[:HEAD_DIM]
    ref[bi, 0, 2 * p + 1] = xt[HEAD_DIM:]


def _rope_key_slab(win_ref):
    d = win_ref.shape[1]
    return jnp.concatenate([jnp.zeros((NOPE_A, d), BF16), win_ref[E_KROPE:E_GA, :],
                            jnp.zeros((LANES - QK_A, d), BF16)], axis=0)


def _swap_halves(a):
    return jnp.concatenate([a[HEAD_DIM:], a[:HEAD_DIM]], axis=0)


def _mod_step(n_cond, is_first, bias_row, c_ref, w_ref, o_ref):
    @pl.when(is_first)
    def _():
        o_ref[:n_cond, :] = jnp.broadcast_to(bias_row, (n_cond, o_ref.shape[1]))
        o_ref[n_cond:, :] = jnp.zeros((o_ref.shape[0] - n_cond, o_ref.shape[1]), F32)

    s = _silu(c_ref[...])
    cols = [jnp.broadcast_to(s[:, r:r + 1], (s.shape[0], LANES)) for r in range(n_cond)]
    for t in range(w_ref.shape[1] // LANES):
        sl = slice(t * LANES, (t + 1) * LANES)
        w = w_ref[:, sl]
        for r in range(n_cond):
            o_ref[r:r + 1, sl] += jnp.sum(w * cols[r], axis=0, keepdims=True)


def _mla_keys(cb, kr, wkk_ref, wkv_ref, kg, rope=None):
    kk = _dot(cb, wkk_ref[...])
    keys = []
    for h in range(N_HEADS):
        k = _rms(kk[:, h * LANES:(h + 1) * LANES] + kr, kg, QK_A)
        if rope is not None:
            k = rope(k)
        keys.append(k.astype(BF16))
    return keys, _dot(cb, wkv_ref[...]).astype(BF16)


def _p0_kernel(n_cond, x_ref, m_ref, ng_ref, win_ref, qag_ref, wq_ref, kvag_ref, wkk_ref, wkv_ref, qg_ref, kg_ref,
               naqg_ref, nakg_ref, wout_ref, ct_ref, wm_ref, bm_ref, wio_ref, woo_ref,
               xo_ref, ckv_ref, krope_ref, nak_ref, nav_ref, mo_ref, wino_ref, wouto_ref, y_scr):
    _mod_step(n_cond, pl.program_id(0) == 0, bm_ref[1:2, :], ct_ref, wm_ref, mo_ref)
    wino_ref[...] = wio_ref[0].astype(BF16)
    wouto_ref[...] = woo_ref[0].astype(BF16)

    nbs = x_ref.shape[0]
    x = x_ref[...].reshape(nbs * SEQ, D_MODEL)
    h, gate = _modulate(x, ng_ref[...], m_ref[0:1, :])
    hb = h.astype(BF16)
    lo = _lane_lo()
    hi = jnp.logical_not(lo)
    rows = [slice(bi * SEQ, (bi + 1) * SEQ) for bi in range(nbs)]

    qln = _rms(_dot_nt(hb, win_ref[E_QLAT:E_CKV, :]), qag_ref[...], Q_RANK).astype(BF16)
    q_all = _dot(qln, wq_ref[...])
    ckv_n = _rms(_dot_nt(hb, win_ref[E_CKV:E_KROPE, :]), kvag_ref[...], KV_RANK)
    kr = _dot_nt(hb, _rope_key_slab(win_ref))
    for bi, rs in enumerate(rows):
        ckv_ref[bi, 0] = ckv_n[rs]
        krope_ref[bi, 0] = kr[rs].T[NOPE_A:QK_A]
    keys, vals = _mla_keys(ckv_n.astype(BF16), kr, wkk_ref, wkv_ref, kg_ref[...])
    qg = qg_ref[...] * (QK_A ** -0.5 * LOG2E)

    ga = _dot_nt(hb, win_ref[E_GA:E_QB, :])
    zq = _dot_nt(hb, win_ref[E_QB:E_KB, :])
    zk = _dot_nt(hb, win_ref[E_KB:E_VB, :])
    zv = _dot_nt(hb, win_ref[E_VB:E_GB, :])
    gb = _dot_nt(hb, win_ref[E_GB:E_END, :])
    naqg = naqg_ref[...] * (HEAD_DIM ** -0.5 * LOG2E)

    for p in range(N_PAIRS):
        sl = slice(p * LANES, (p + 1) * LANES)
        ys = slice(4 * LANES + p * LANES, 4 * LANES + (p + 1) * LANES)
        qhs = [_rms(q_all[:, hh * LANES:(hh + 1) * LANES], qg, QK_A).astype(BF16) for hh in (2 * p, 2 * p + 1)]
        qb = _rms_halves(zq[:, sl], naqg, lo)
        kb = _rms_halves(zk[:, sl], nakg_ref[...], lo)
        vb = zv[:, sl]
        kbb, vbb = kb.astype(BF16), vb.astype(BF16)
        va = vals[:, sl]
        qms = [jnp.where(keep, qb, 0.0).astype(BF16) for keep in (lo, hi)]
        for bi, rs in enumerate(rows):
            o2 = [_attend([(_dot_nt(qhs[i][rs], keys[2 * p + i][rs]), va[rs], False)]) for i in (0, 1)]
            y_scr[rs, sl] = (jnp.where(lo, o2[0], o2[1]) * _silu(ga[rs, sl])).astype(BF16)
            _store_pair_transposed(nak_ref, bi, p, kb[rs])
            _store_pair_transposed(nav_ref, bi, p, vb[rs])
            o2 = [_attend([(_dot_nt(qms[i][rs], kbb[rs]), vbb[rs], False)]) for i in (0, 1)]
            y_scr[rs, ys] = (jnp.where(lo, o2[0], o2[1]) * _silu(gb[rs, sl])).astype(BF16)

    xo_ref[...] = (x + gate * _dot(y_scr[...], wout_ref[...])).reshape(nbs, SEQ, D_MODEL)


def _full(shape):
    n = len(shape)
    return pl.BlockSpec(shape, lambda *_: (0,) * n, pipeline_mode=pl.Buffered(1))


class _Row(NamedTuple):
    table: jax.Array
    row: int


def _spec(a):
    if isinstance(a, _Row):
        idx = (a.row,) + (0,) * (a.table.ndim - 1)
        return pl.BlockSpec((None,) + a.table.shape[1:], lambda *_: idx, pipeline_mode=pl.Buffered(1))
    return _full(a.shape)


def _arr(a):
    return a.table if isinstance(a, _Row) else a


def _prompt_even(x, m, ng, win, qag, wq, kvag, wkk, wkv, qg, kg, naqg, nakg, wout,
                 cond_t, n_cond, w_mod, b_mod, w_in_odd, w_out_odd):
    nb = x.shape[0]
    nbs = PROMPT_BATCHES_PER_STEP
    steps = nb // nbs
    assert nb % nbs == 0 and D_MODEL % (BF16_SUBLANES * steps) == 0
    tr = D_MODEL // steps
    ins = (m, ng, win, qag, wq, kvag, wkk, wkv, qg, kg, naqg, nakg, wout)
    return pl.pallas_call(
        functools.partial(_p0_kernel, n_cond),
        grid=(steps,),
        in_specs=[pl.BlockSpec((nbs, SEQ, D_MODEL), lambda b: (b, 0, 0))] + [_spec(a) for a in ins]
        + [pl.BlockSpec((tr, 8), lambda b: (b, 0)),
           pl.BlockSpec((None, tr, 3 * D_MODEL), lambda b: (1, b, 0)),
           _full(b_mod.shape),
           pl.BlockSpec((1, tr, O_END), lambda b: (0, b, 0)),
           pl.BlockSpec((1, tr, D_MODEL), lambda b: (0, b, 0))],
        out_specs=[pl.BlockSpec((nbs, SEQ, D_MODEL), lambda b: (b, 0, 0)),
                   pl.BlockSpec((nbs, 1, SEQ, KV_RANK), lambda b: (b, 0, 0, 0)),
                   pl.BlockSpec((nbs, 1, ROPE_A, SEQ), lambda b: (b, 0, 0, 0)),
                   pl.BlockSpec((nbs, 1, N_HEADS, HEAD_DIM, SEQ), lambda b: (b, 0, 0, 0, 0)),
                   pl.BlockSpec((nbs, 1, N_HEADS, HEAD_DIM, SEQ), lambda b: (b, 0, 0, 0, 0)),
                   pl.BlockSpec((8, 3 * D_MODEL), lambda b: (0, 0)),
                   pl.BlockSpec((tr, O_END), lambda b: (b, 0)),
                   pl.BlockSpec((tr, D_MODEL), lambda b: (b, 0))],
        out_shape=[jax.ShapeDtypeStruct((nb, SEQ, D_MODEL), F32),
                   jax.ShapeDtypeStruct((nb, 1, SEQ, KV_RANK), F32),
                   jax.ShapeDtypeStruct((nb, 1, ROPE_A, SEQ), F32),
                   jax.ShapeDtypeStruct((nb, 1, N_HEADS, HEAD_DIM, SEQ), F32),
                   jax.ShapeDtypeStruct((nb, 1, N_HEADS, HEAD_DIM, SEQ), F32),
                   jax.ShapeDtypeStruct((8, 3 * D_MODEL), F32),
                   jax.ShapeDtypeStruct((D_MODEL, O_END), BF16),
                   jax.ShapeDtypeStruct((D_MODEL, D_MODEL), BF16)],
        scratch_shapes=[pltpu.VMEM((nbs * SEQ, D_MODEL), BF16)],
        compiler_params=pltpu.CompilerParams(dimension_semantics=("arbitrary",), vmem_limit_bytes=VMEM_LIMIT),
        name="prompt_even",
    )(x, *map(_arr, ins), cond_t, w_mod, b_mod, w_in_odd, w_out_odd)


def _gqa_pair_operands(k, v, kg2, lo):
    kn = _rms_halves(k, kg2, lo)
    return kn, (kn.astype(BF16), pltpu.roll(kn, HEAD_DIM, 1).astype(BF16)), \
        (_with_ones(v.astype(BF16)), _with_ones(pltpu.roll(v, HEAD_DIM, 1).astype(BF16)))


def _p1_kernel(sink_ref, x_ref, m_ref, ng_ref, win_ref, gqg_ref, gkg_ref, sqg_ref, skg_ref, wout_ref,
               xo_ref, gk_ref, gv_ref, sk_ref, sv_ref, y_scr):
    nbs = x_ref.shape[0]
    x = x_ref[...].reshape(nbs * SEQ, D_MODEL)
    h, gate = _modulate(x, ng_ref[...], m_ref[0:1, :])
    hb = h.astype(BF16)
    lo = _lane_lo()
    hi = jnp.logical_not(lo)
    sc = HEAD_DIM ** -0.5 * LOG2E
    rows = [slice(bi * SEQ, (bi + 1) * SEQ) for bi in range(nbs)]

    branches = ((O_QC, O_KC, O_VC, O_GC, gqg_ref, gkg_ref, gk_ref, gv_ref, False, 0),
                (O_QD, O_KD, O_VD, O_GD, sqg_ref, skg_ref, sk_ref, sv_ref, True, 4 * LANES))
    for oq, ok, ov, og, qg_ref, kg_ref, ck_ref, cv_ref, has_sink, yoff in branches:
        zq = _dot(hb, win_ref[:, oq:oq + 4 * LANES])
        zkv = _dot(hb, win_ref[:, ok:ok + 2 * LANES])
        zg = _dot(hb, win_ref[:, og:og + 4 * LANES])
        v = zkv[:, LANES:]
        kn, ks, vs = _gqa_pair_operands(zkv[:, :LANES], v, kg_ref[...], lo)
        for bi, rs in enumerate(rows):
            _store_pair_transposed(ck_ref, bi, 0, kn[rs])
            _store_pair_transposed(cv_ref, bi, 0, v[rs])
        qg = qg_ref[...] * sc
        for p in range(N_PAIRS):
            sl = slice(p * LANES, (p + 1) * LANES)
            qn = _rms_halves(zq[:, sl], qg, lo)
            qms = [jnp.where(keep, qn, 0.0).astype(BF16) for keep in (lo, hi)]
            kv = p // 2
            for bi, rs in enumerate(rows):
                o2 = []
                for half in (0, 1):
                    swap = 0 if kv == half else 1
                    sink = sink_ref[2 * p + half] * LOG2E if has_sink else None
                    o2.append(_attend([(_dot_nt(qms[half][rs], ks[swap][rs]), vs[swap][rs], False)], sink))
                o = jnp.where(lo, o2[0], o2[1])
                y_scr[rs, yoff + p * LANES:yoff + (p + 1) * LANES] = (o * _silu(zg[rs, sl])).astype(BF16)

    xo_ref[...] = (x + gate * _dot(y_scr[...], wout_ref[...])).reshape(nbs, SEQ, D_MODEL)


def _prompt_odd(sink, x, m, ng, win, gqg, gkg, sqg, skg, wout):
    nb = x.shape[0]
    nbs = PROMPT_BATCHES_PER_STEP
    assert nb % nbs == 0
    ins = (m, ng, win, gqg, gkg, sqg, skg, wout)
    cache_spec = pl.BlockSpec((nbs, 1, 2, HEAD_DIM, SEQ), lambda b: (b, 0, 0, 0, 0))
    cache_shape = jax.ShapeDtypeStruct((nb, 1, 2, HEAD_DIM, SEQ), F32)
    return pl.pallas_call(
        _p1_kernel,
        grid=(nb // nbs,),
        in_specs=[pl.BlockSpec(memory_space=pltpu.SMEM),
                  pl.BlockSpec((nbs, SEQ, D_MODEL), lambda b: (b, 0, 0))] + [_spec(a) for a in ins],
        out_specs=[pl.BlockSpec((nbs, SEQ, D_MODEL), lambda b: (b, 0, 0))] + [cache_spec] * 4,
        out_shape=[jax.ShapeDtypeStruct((nb, SEQ, D_MODEL), F32)] + [cache_shape] * 4,
        scratch_shapes=[pltpu.VMEM((nbs * SEQ, D_MODEL), BF16)],
        compiler_params=pltpu.CompilerParams(dimension_semantics=("arbitrary",), vmem_limit_bytes=VMEM_LIMIT),
        name="prompt_odd",
    )(sink, x, *map(_arr, ins))


def _s0a_kernel(x_ref, m_ref, ng_ref, win_ref, qag_ref, wq_ref, kvag_ref, wkk_ref, wkv_ref, qg_ref, kg_ref,
                naqg_ref, nakg_ref, cos_ref, sin_ref,
                qa_ref, ka_ref, va_ref, qb_ref, kb_ref, vb_ref, g_ref):
    b = pl.program_id(0)
    lo = _lane_lo()
    partner = _rope_matrix2(ROPE_A, LANES, NOPE_A)
    hb = _modulate(x_ref[0], ng_ref[...], m_ref[pl.ds(1 + b, 1), :])[0].astype(BF16)
    cos, sin = cos_ref[...], sin_ref[...]

    qln = _rms(_dot_nt(hb, win_ref[E_QLAT:E_CKV, :]), qag_ref[...], Q_RANK).astype(BF16)
    q_all = _dot(qln, wq_ref[...])
    ckv_n = _rms(_dot_nt(hb, win_ref[E_CKV:E_KROPE, :]), kvag_ref[...], KV_RANK)
    kr = _dot_nt(hb, _rope_key_slab(win_ref))
    cb = ckv_n.astype(BF16)
    kk = _dot(cb, wkk_ref[...])
    va_ref[0] = _dot(cb, wkv_ref[...]).astype(BF16)
    zq = _dot_nt(hb, win_ref[E_QB:E_KB, :])
    zk = _dot_nt(hb, win_ref[E_KB:E_VB, :])
    vb_ref[0] = _dot_nt(hb, win_ref[E_VB:E_GB, :]).astype(BF16)
    g_ref[0, :, 0:4 * LANES] = _silu(_dot_nt(hb, win_ref[E_GA:E_QB, :]))
    g_ref[0, :, 4 * LANES:8 * LANES] = _silu(_dot_nt(hb, win_ref[E_GB:E_END, :]))

    qg = qg_ref[...] * (QK_A ** -0.5 * LOG2E)
    kg = kg_ref[...]
    k_partner = _lane_mix(kr * kg, partner) * sin
    for hh in range(N_HEADS):
        sl = slice(hh * LANES, (hh + 1) * LANES)
        qn = _rms(q_all[:, sl], qg, QK_A)
        qa_ref[0, :, sl] = (qn * cos + _lane_mix(qn, partner) * sin).astype(BF16)
        k_raw = kk[:, sl] + kr
        k_inv = lax.rsqrt(jnp.sum(k_raw * k_raw, axis=-1, keepdims=True) / QK_A + EPS)
        ka_ref[0, :, sl] = ((k_raw * kg * cos + k_partner) * k_inv).astype(BF16)
    naqg = naqg_ref[...] * (HEAD_DIM ** -0.5 * LOG2E)
    for p in range(N_PAIRS):
        sl = slice(p * LANES, (p + 1) * LANES)
        qb_ref[0, :, sl] = _rms_halves(zq[:, sl], naqg, lo).astype(BF16)
        kb_ref[0, :, sl] = _rms_halves(zk[:, sl], nakg_ref[...], lo).astype(BF16)


def _sample_even_proj(x, m, ng, win, qag, wq, kvag, wkk, wkv, qg, kg, naqg, nakg, cos, sin):
    nb, s, _ = x.shape
    nq = s // PROJ_BLOCK
    ins = (m, ng, win, qag, wq, kvag, wkk, wkv, qg, kg, naqg, nakg)
    tab = pl.BlockSpec((PROJ_BLOCK, LANES), lambda b, j: (j, 0))

    def blk(w):
        return pl.BlockSpec((1, PROJ_BLOCK, w), lambda b, j: (b, j, 0))

    def shp(w, dt):
        return jax.ShapeDtypeStruct((nb, s, w), dt)

    return pl.pallas_call(
        _s0a_kernel,
        grid=(nb, nq),
        in_specs=[blk(D_MODEL)] + [_spec(a) for a in ins] + [tab, tab],
        out_specs=[blk(1024), blk(1024), blk(512), blk(512), blk(512), blk(512), blk(1024)],
        out_shape=[shp(1024, BF16), shp(1024, BF16), shp(512, BF16), shp(512, BF16), shp(512, BF16),
                   shp(512, BF16), shp(1024, F32)],
        compiler_params=pltpu.CompilerParams(dimension_semantics=("arbitrary", "arbitrary"),
                                             vmem_limit_bytes=VMEM_LIMIT),
        name="sample_even_proj",
    )(x, *map(_arr, ins), cos, sin)


def _build_bias_table(rpb_ref, tile_scr, tab_ref):
    qc = lax.broadcasted_iota(jnp.int32, (GRID_W, LANES), 0)
    lane = lax.broadcasted_iota(jnp.int32, (GRID_W, LANES), 1)
    kc = jnp.bitwise_and(lane, GRID_W - 1)
    lo = lane < GRID_W
    diff = kc - qc + (NA_COLS - 1)
    cs = jnp.clip(qc - NA_COLS // 2, 0, GRID_W - NA_COLS)
    valid = (kc >= cs) & (kc < cs + NA_COLS)
    tab_ref[...] = jnp.zeros(tab_ref.shape, F32)
    tile_scr[RPB_ROWS] = jnp.zeros((GRID_W, LANES), F32)

    def per_head(h, carry):
        for dr in range(RPB_ROWS):
            t = jnp.zeros((GRID_W, LANES), F32)
            for dc in range(RPB_COLS):
                t = jnp.where(diff == dc, rpb_ref[(h * RPB_ROWS + dr) * RPB_COLS + dc], t)
            tile_scr[dr] = jnp.where(valid, t * LOG2E, NEG_INF)
        for c in range(NA_ROWS // 2, NA_ROWS // 2 + NA_ROWS):
            d0 = 2 * c - NA_ROWS
            tab_ref[0, h, c] = jnp.where(lo, tile_scr[d0], tile_scr[d0 + 1])
            tab_ref[1, h, c] = jnp.where(lo, tile_scr[d0 - 1 if d0 > 0 else RPB_ROWS], tile_scr[d0])
        return carry

    lax.fori_loop(0, N_HEADS, per_head, 0)


def _s0b_kernel(rpb_ref, x_ref, m_ref, qa_ref, ka_ref, va_ref, qb_ref, kb_ref, vb_ref, g_ref,
                cckv_ref, ckr_ref, cnk_ref, cnv_ref, wkk_ref, wkv_ref, kg_ref, wout_ref,
                xo_ref, kca_scr, vca_scr, tile_scr, tab_scr, y_scr):
    b = pl.program_id(0)
    j = pl.program_id(1)
    lo = _lane_lo()
    n_lat = ka_ref.shape[1]

    @pl.when((b == 0) & (j == 0))
    def _():
        _build_bias_table(rpb_ref, tile_scr, tab_scr)

    @pl.when(j == 0)
    def _():
        kr_t = jnp.concatenate([jnp.zeros((NOPE_A, PAST_LEN), F32), ckr_ref[0],
                                jnp.zeros((LANES - QK_A, PAST_LEN), F32)], axis=0)
        keys, vals = _mla_keys(cckv_ref[0].astype(BF16), kr_t.T, wkk_ref, wkv_ref, kg_ref[...])
        for hh in range(N_HEADS):
            kca_scr[:, hh * LANES:(hh + 1) * LANES] = keys[hh]
        vca_scr[...] = vals

    kidx = lax.broadcasted_iota(jnp.int32, (1, n_lat), 1)
    for p in range(N_PAIRS):
        sl = slice(p * LANES, (p + 1) * LANES)
        o2 = []
        va = _with_ones(va_ref[0, :, sl])
        vca = _with_ones(vca_scr[:, sl])
        for hh in (2 * p, 2 * p + 1):
            hs = slice(hh * LANES, (hh + 1) * LANES)
            q = qa_ref[0, :, hs]
            o2.append(_attend([(_dot_nt(q, ka_ref[0, :, hs]), va, False),
                               (_dot_nt(q, kca_scr[:, hs]), vca, False)]))
        oa = jnp.where(lo, o2[0], o2[1])
        y_scr[:, sl] = (oa * g_ref[0, :, sl]).astype(BF16)

        qb = qb_ref[0, :, sl]
        kb = kb_ref[0, :, sl]
        vb = _with_ones(vb_ref[0, :, sl])
        kcb = cnk_ref[0, sl, :].astype(BF16)
        vcb = _with_ones(cnv_ref[0, sl, :].astype(BF16), transposed=True)
        o2 = []
        for half in (0, 1):
            head = 2 * p + half
            qm = jnp.where(lo if half == 0 else jnp.logical_not(lo), qb, jnp.zeros_like(qb))
            s_lat = _dot_nt(qm, kb)
            rows = []
            for local in range(NA_Q_BLOCK // GRID_W):
                qr = j * (NA_Q_BLOCK // GRID_W) + local
                par = 0 if local % 2 == 1 else 1
                c0 = (RPB_ROWS + par - local) // 2 - (NA_Q_BLOCK // GRID_W // 2) * j
                bias = jnp.concatenate([tab_scr[par, head, c0 + t] for t in range(n_lat // LANES)], axis=1)
                r0 = jnp.clip(qr - NA_ROWS // 2, 0, n_lat // GRID_W - NA_ROWS) * GRID_W
                ok = (kidx >= r0) & (kidx < r0 + NA_ROWS * GRID_W)
                rows.append(jnp.where(ok, s_lat[local * GRID_W:(local + 1) * GRID_W] + bias, NEG_INF))
            s_lat = jnp.concatenate(rows, axis=0)
            o2.append(_attend([(s_lat, vb, False), (_dot(qm, kcb), vcb, True)]))
        ob = jnp.where(lo, o2[0], o2[1])
        ys = slice(4 * LANES + p * LANES, 4 * LANES + (p + 1) * LANES)
        y_scr[:, ys] = (ob * g_ref[0, :, ys]).astype(BF16)

    d = x_ref.shape[-1]
    gate = m_ref[pl.ds(1 + b, 1), 2 * d:]
    xo_ref[0] = x_ref[0] + gate * _dot(y_scr[...], wout_ref[...])


def _sample_even_attn(rpb, x, m, qa, ka, va, qb, kb, vb, g, cckv, ckr, cnk, cnv, wkk, wkv, kg, wout):
    nb, s, _ = x.shape
    nq = s // NA_Q_BLOCK

    def blk(w):
        return pl.BlockSpec((1, NA_Q_BLOCK, w), lambda b, j: (b, j, 0))

    def per_batch(a):
        return pl.BlockSpec((1,) + a.shape[1:], lambda b, j: (b, 0, 0))

    return pl.pallas_call(
        _s0b_kernel,
        grid=(nb, nq),
        in_specs=[pl.BlockSpec(memory_space=pltpu.SMEM), blk(D_MODEL), _spec(m),
                  blk(1024), per_batch(ka), per_batch(va), blk(512), per_batch(kb), per_batch(vb), blk(1024),
                  per_batch(cckv), per_batch(ckr), per_batch(cnk), per_batch(cnv),
                  _full(wkk.shape), _full(wkv.shape), _spec(kg), _full(wout.shape)],
        out_specs=blk(D_MODEL),
        out_shape=jax.ShapeDtypeStruct(x.shape, F32),
        scratch_shapes=[pltpu.VMEM((PAST_LEN, N_HEADS * LANES), BF16),
                        pltpu.VMEM((PAST_LEN, N_HEADS * HEAD_DIM), BF16),
                        pltpu.VMEM((RPB_ROWS + 1, GRID_W, LANES), F32),
                        pltpu.VMEM((2, N_HEADS, BIAS_CHUNKS, GRID_W, LANES), F32),
                        pltpu.VMEM((NA_Q_BLOCK, D_MODEL), BF16)],
        compiler_params=pltpu.CompilerParams(dimension_semantics=("arbitrary", "arbitrary"),
                                             vmem_limit_bytes=VMEM_LIMIT),
        name="sample_even_attn",
    )(rpb, x, _arr(m), qa, ka, va, qb, kb, vb, g, cckv, ckr, cnk, cnv, wkk, wkv, _arr(kg), wout)


def _s1a_kernel(x_ref, m_ref, ng_ref, win_ref, gqg_ref, gkg_ref, sqg_ref, skg_ref, cos_ref, sin_ref,
                qc_ref, kc_ref, vc_ref, qd_ref, kd_ref, vd_ref, g_ref):
    b = pl.program_id(0)
    lo = _lane_lo()
    partner = _rope_matrix2(HEAD_DIM, HEAD_DIM, 0)
    swap = _swap_matrix2()[:LANES]
    hb = _modulate(x_ref[0], ng_ref[...], m_ref[pl.ds(1 + b, 1), :])[0].astype(BF16)
    cos, sin = cos_ref[...], sin_ref[...]
    sc = HEAD_DIM ** -0.5 * LOG2E

    def rope(t):
        return t * cos + _lane_mix(t, partner) * sin

    branches = ((O_QC, O_KC, O_GC, gqg_ref, gkg_ref, qc_ref, kc_ref, vc_ref, 0),
                (O_QD, O_KD, O_GD, sqg_ref, skg_ref, qd_ref, kd_ref, vd_ref, 4 * LANES))
    for oq, ok, og, qg_ref, kg_ref, q_out, k_out, v_out, goff in branches:
        zq = _dot(hb, win_ref[:, oq:oq + 4 * LANES])
        zkv = _dot(hb, win_ref[:, ok:ok + 2 * LANES])
        qg = qg_ref[...] * sc
        for p in range(N_PAIRS):
            sl = slice(p * LANES, (p + 1) * LANES)
            q_out[0, :, sl] = rope(_rms_halves(zq[:, sl], qg, lo)).astype(BF16)
        kn = rope(_rms_halves(zkv[:, :LANES], kg_ref[...], lo))
        v = zkv[:, LANES:]
        for out, val in ((k_out, kn.astype(BF16)), (v_out, v.astype(BF16))):
            out[0, :, 0:LANES] = val
            out[0, :, LANES:2 * LANES] = _dot(val, swap).astype(BF16)
        g_ref[0, :, goff:goff + 4 * LANES] = _silu(_dot(hb, win_ref[:, og:og + 4 * LANES]))


def _sample_odd_proj(x, m, ng, win, gqg, gkg, sqg, skg, cos, sin):
    nb, s, _ = x.shape
    nq = s // PROJ_BLOCK
    ins = (m, ng, win, gqg, gkg, sqg, skg)
    tab = pl.BlockSpec((PROJ_BLOCK, LANES), lambda b, j: (j, 0))

    def blk(w):
        return pl.BlockSpec((1, PROJ_BLOCK, w), lambda b, j: (b, j, 0))

    def shp(w, dt):
        return jax.ShapeDtypeStruct((nb, s, w), dt)

    return pl.pallas_call(
        _s1a_kernel,
        grid=(nb, nq),
        in_specs=[blk(D_MODEL)] + [_spec(a) for a in ins] + [tab, tab],
        out_specs=[blk(512), blk(256), blk(256), blk(512), blk(256), blk(256), blk(1024)],
        out_shape=[shp(512, BF16), shp(256, BF16), shp(256, BF16), shp(512, BF16), shp(256, BF16),
                   shp(256, BF16), shp(1024, F32)],
        compiler_params=pltpu.CompilerParams(dimension_semantics=("arbitrary", "arbitrary"),
                                             vmem_limit_bytes=VMEM_LIMIT),
        name="sample_odd_proj",
    )(x, *map(_arr, ins), cos, sin)


def _s1b_kernel(sink_ref, x_ref, m_ref, qc_ref, kc_ref, vc_ref, qd_ref, kd_ref, vd_ref, g_ref,
                cgk_ref, cgv_ref, csk_ref, csv_ref, wout_ref, xo_ref, y_scr):
    b = pl.program_id(0)
    j = pl.program_id(1)
    lo = _lane_lo()
    n_lat = kc_ref.shape[1]
    win_keys = Q_BLOCK + 2 * SWA_HALF

    def ctx_pair(ref, values=False):
        a = ref[0].astype(BF16)
        pair = (a, _swap_halves(a))
        return tuple(_with_ones(t, transposed=True) for t in pair) if values else pair

    cgk, cgv, csk, csv = ctx_pair(cgk_ref), ctx_pair(cgv_ref, True), ctx_pair(csk_ref), ctx_pair(csv_ref, True)
    vcs = [_with_ones(vc_ref[0, :, w * LANES:(w + 1) * LANES]) for w in (0, 1)]

    ks = pl.multiple_of(jnp.clip(j * Q_BLOCK - SWA_HALF, 0, n_lat - win_keys), SWA_HALF)
    qpos = j * Q_BLOCK + lax.broadcasted_iota(jnp.int32, (Q_BLOCK, win_keys), 0)
    kpos = ks + lax.broadcasted_iota(jnp.int32, (Q_BLOCK, win_keys), 1)
    band = jnp.abs(qpos - kpos) <= SWA_HALF
    vds = [_with_ones(vd_ref[0, pl.ds(ks, win_keys), w * LANES:(w + 1) * LANES]) for w in (0, 1)]

    for p in range(N_PAIRS):
        sl = slice(p * LANES, (p + 1) * LANES)
        kv = p // 2
        qc = qc_ref[0, :, sl]
        qd = qd_ref[0, :, sl]
        oc2, od2 = [], []
        for half in (0, 1):
            swap = 0 if kv == half else 1
            ws = slice(swap * LANES, (swap + 1) * LANES)
            keep = lo if half == 0 else jnp.logical_not(lo)
            qm = jnp.where(keep, qc, jnp.zeros_like(qc))
            oc2.append(_attend([(_dot_nt(qm, kc_ref[0, :, ws]), vcs[swap], False),
                                (_dot(qm, cgk[swap]), cgv[swap], True)]))
            qm = jnp.where(keep, qd, jnp.zeros_like(qd))
            s_loc = jnp.where(band, _dot_nt(qm, kd_ref[0, pl.ds(ks, win_keys), ws]), NEG_INF)
            od2.append(_attend([(s_loc, vds[swap], False),
                                (_dot(qm, csk[swap]), csv[swap], True)], sink_ref[2 * p + half] * LOG2E))
        y_scr[:, sl] = (jnp.where(lo, oc2[0], oc2[1]) * g_ref[0, :, sl]).astype(BF16)
        ys = slice(4 * LANES + p * LANES, 4 * LANES + (p + 1) * LANES)
        y_scr[:, ys] = (jnp.where(lo, od2[0], od2[1]) * g_ref[0, :, ys]).astype(BF16)

    d = x_ref.shape[-1]
    gate = m_ref[pl.ds(1 + b, 1), 2 * d:]
    xo_ref[0] = x_ref[0] + gate * _dot(y_scr[...], wout_ref[...])


def _sample_odd_attn(sink, x, m, qc, kc, vc, qd, kd, vd, g, cgk, cgv, csk, csv, wout):
    nb, s, _ = x.shape
    nq = s // Q_BLOCK

    def blk(w):
        return pl.BlockSpec((1, Q_BLOCK, w), lambda b, j: (b, j, 0))

    def per_batch(a):
        return pl.BlockSpec((1,) + a.shape[1:], lambda b, j: (b, 0, 0))

    return pl.pallas_call(
        _s1b_kernel,
        grid=(nb, nq),
        in_specs=[pl.BlockSpec(memory_space=pltpu.SMEM), blk(D_MODEL), _spec(m),
                  blk(512), per_batch(kc), per_batch(vc), blk(512), per_batch(kd), per_batch(vd), blk(1024),
                  per_batch(cgk), per_batch(cgv), per_batch(csk), per_batch(csv), _full(wout.shape)],
        out_specs=blk(D_MODEL),
        out_shape=jax.ShapeDtypeStruct(x.shape, F32),
        scratch_shapes=[pltpu.VMEM((Q_BLOCK, D_MODEL), BF16)],
        compiler_params=pltpu.CompilerParams(dimension_semantics=("arbitrary", "arbitrary"),
                                             vmem_limit_bytes=VMEM_LIMIT),
        name="sample_odd_attn",
    )(sink, x, _arr(m), qc, kc, vc, qd, kd, vd, g, cgk, cgv, csk, csv, wout)


WEIGHT_PREP_STEPS = 8
EVEN_IN_CHUNKS = 6
COND_PREP_ROWS = 128
BF16_SUBLANES = 16


G_MLA_Q, G_MLA_K, G_NA_Q, G_NA_K, G_GQA_Q, G_GQA_K, G_SWA_Q, G_SWA_K, N_GAINS = range(9)


GAIN_WIDTHS = (QK_A, QK_A) + (HEAD_DIM,) * 6


def _cond_prep_kernel(n_cond, ct_ref, wm_ref, bm_ref, ng_ref, gains_ref, mo_ref, gt_ref, ngt_ref):
    _mod_step(n_cond, pl.program_id(0) == 0, bm_ref[0:1, :], ct_ref, wm_ref, mo_ref)
    gt_ref[...] = jnp.zeros(gt_ref.shape, F32)
    start = 0
    for r, w in enumerate(GAIN_WIDTHS):
        g = gains_ref[:, start:start + w]
        start += w
        for off in range(0, LANES - w + 1, w):
            gt_ref[r, :, off:off + w] = g
    for layer in range(ngt_ref.shape[0]):
        ngt_ref[layer] = ng_ref[layer:layer + 1, :]


def _cond_prep(cond_t, n_cond, w_mod, b_mod, norm_g, gains):
    assert len(gains) == N_GAINS and tuple(g.shape[-1] for g in gains) == GAIN_WIDTHS
    gains_row = jnp.concatenate([g.reshape(1, -1) for g in gains], axis=1)
    tk = COND_PREP_ROWS
    return pl.pallas_call(
        functools.partial(_cond_prep_kernel, n_cond),
        grid=(D_MODEL // tk,),
        in_specs=[pl.BlockSpec((tk, 8), lambda k: (k, 0)),
                  pl.BlockSpec((None, tk, 3 * D_MODEL), lambda k: (0, k, 0)),
                  _full(b_mod.shape), _full(norm_g.shape), _full(gains_row.shape)],
        out_specs=[_full((8, 3 * D_MODEL)), _full((N_GAINS, 1, LANES)), _full((norm_g.shape[0], 1, D_MODEL))],
        out_shape=[jax.ShapeDtypeStruct((8, 3 * D_MODEL), F32), jax.ShapeDtypeStruct((N_GAINS, 1, LANES), F32),
                   jax.ShapeDtypeStruct((norm_g.shape[0], 1, D_MODEL), F32)],
        compiler_params=pltpu.CompilerParams(dimension_semantics=("arbitrary",)),
        name="cond_prep",
    )(cond_t, w_mod, b_mod, norm_g, gains_row)


def _weight_prep_kernel(wie_ref, woe_ref, wqu_ref, wkv_ref, win_e_ref, wout_e_ref, wq_ref, wkk_ref, wkvv_ref):
    win_e_ref[...] = wie_ref[...].astype(BF16)
    wout_e_ref[...] = woe_ref[0].astype(BF16)

    wq_ref[...] = jnp.zeros(wq_ref.shape, BF16)
    for h in range(N_HEADS):
        wq_ref[:, h * LANES:h * LANES + QK_A] = wqu_ref[0, :, h * QK_A:(h + 1) * QK_A].astype(BF16)
    lo = _lane_lo()
    for p in range(N_PAIRS):
        a = wkv_ref[0, :, (2 * p) * LANES:(2 * p + 1) * LANES]
        c = wkv_ref[0, :, (2 * p + 1) * LANES:(2 * p + 2) * LANES]
        wkk_ref[:, (2 * p) * LANES:(2 * p + 1) * LANES] = jnp.where(lo, a, 0.0).astype(BF16)
        wkk_ref[:, (2 * p + 1) * LANES:(2 * p + 2) * LANES] = jnp.where(lo, c, 0.0).astype(BF16)
        wkvv_ref[:, p * LANES:(p + 1) * LANES] = jnp.where(lo, pltpu.roll(a, HEAD_DIM, 1), c).astype(BF16)


def _weight_prep(w_in_even_t, w_out_even, w_q_up, w_kv_up):
    n = WEIGHT_PREP_STEPS
    ins = (w_out_even, w_q_up, w_kv_up)
    out_cols = (D_MODEL, N_HEADS * LANES, N_HEADS * LANES, N_HEADS * HEAD_DIM)
    out_rows = (D_MODEL, Q_RANK, KV_RANK, KV_RANK)
    te = E_END // EVEN_IN_CHUNKS
    assert te * EVEN_IN_CHUNKS == E_END and te % BF16_SUBLANES == 0 and EVEN_IN_CHUNKS <= n
    even_spec = pl.BlockSpec((te, D_MODEL), lambda i: (jnp.minimum(i, EVEN_IN_CHUNKS - 1), 0))
    return pl.pallas_call(
        _weight_prep_kernel,
        grid=(n,),
        in_specs=[even_spec] + [pl.BlockSpec((1, a.shape[1] // n, a.shape[2]), lambda i: (0, i, 0)) for a in ins],
        out_specs=[even_spec] + [pl.BlockSpec((r // n, c), lambda i: (i, 0)) for r, c in zip(out_rows, out_cols)],
        out_shape=[jax.ShapeDtypeStruct((E_END, D_MODEL), BF16)]
        + [jax.ShapeDtypeStruct((r, c), BF16) for r, c in zip(out_rows, out_cols)],
        compiler_params=pltpu.CompilerParams(dimension_semantics=("arbitrary",), vmem_limit_bytes=VMEM_LIMIT),
        name="weight_prep",
    )(w_in_even_t, *ins)


def _feature_major(c):
    b, h, l, d = c.shape
    return jnp.swapaxes(c, -1, -2).reshape(b, h * d, l)


def _token_major(c):
    return jnp.swapaxes(c, -1, -2)


def _rope_tables(s, rot_dim, period, start):
    quarter = rot_dim // 4
    t = np.arange(s)
    inv = ROPE_THETA ** (-np.arange(quarter, dtype=np.float64) / quarter)
    row = (t // GRID_W).astype(np.float64)[:, None] * inv
    col = (t % GRID_W).astype(np.float64)[:, None] * inv
    ang = np.concatenate([row, col], axis=-1)
    cos, sin = np.cos(ang), np.sin(ang)
    pre = np.ones((s, start))
    post = np.zeros((s, period - start - rot_dim))
    c = np.concatenate([pre, cos, cos, post], axis=-1)
    sn = np.concatenate([0 * pre, sin, sin, post], axis=-1)
    rep = LANES // period
    return jnp.asarray(np.tile(c, (1, rep)), F32), jnp.asarray(np.tile(sn, (1, rep)), F32)


def kernel(x_prompt, x_sample, cache_mla_ckv, cache_mla_krope, cache_na_k, cache_na_v, cache_gqa_k, cache_gqa_v, cache_swa_k, cache_swa_v, c, c_ctx, norm_g, w_mod, b_mod, w_in_even, mla_qa_g, w_q_up, mla_kva_g, w_kv_up, mla_q_g, mla_k_g, na_q_g, na_k_g, na_rpb, w_out_even, w_in_odd, gqa_q_g, gqa_k_g, swa_q_g, swa_k_g, swa_sink, w_out_odd):
    n_dec = x_sample.shape[0]
    assert w_mod.shape[0] == 2 and n_dec + 1 <= 8

    cond_t = jnp.concatenate([c_ctx[:, None], c.T, jnp.zeros((D_MODEL, 7 - n_dec), F32)], axis=1)
    n_cond = 1 + n_dec
    gains = (mla_q_g, mla_k_g, na_q_g, na_k_g, gqa_q_g, gqa_k_g, swa_q_g, swa_k_g)
    win_e, wout_e, wq, wkk, wkv = _weight_prep(jnp.swapaxes(w_in_even[0], 0, 1), w_out_even, w_q_up, w_kv_up)
    m_even, gt, ngt = _cond_prep(cond_t, n_cond, w_mod, b_mod, norm_g, gains)
    even = (_Row(ngt, 0), win_e, mla_qa_g, wq, mla_kva_g, wkk, wkv,
            _Row(gt, G_MLA_Q), _Row(gt, G_MLA_K), _Row(gt, G_NA_Q), _Row(gt, G_NA_K))
    sink = swa_sink[0].astype(F32)

    xp1, new_ckv, new_krope, new_na_k, new_na_v, m_odd, win_o, wout_o = _prompt_even(
        x_prompt, m_even, *even, wout_e, cond_t, n_cond, w_mod, b_mod, w_in_odd, w_out_odd)
    odd = (_Row(ngt, 1), win_o, _Row(gt, G_GQA_Q), _Row(gt, G_GQA_K), _Row(gt, G_SWA_Q), _Row(gt, G_SWA_K))
    xp2, new_gqa_k, new_gqa_v, new_swa_k, new_swa_v = _prompt_odd(sink, xp1, m_odd, *odd, wout_o)

    cos_e, sin_e = _rope_tables(DEC_SEQ, ROPE_A, LANES, NOPE_A)
    qa, ka, va, qbs, kbs, vbs, g0 = _sample_even_proj(x_sample, m_even, *even, cos_e, sin_e)
    ckr = jnp.swapaxes(cache_mla_krope[:, 0], -1, -2)
    xs1 = _sample_even_attn(na_rpb[0].reshape(-1), x_sample, m_even, qa, ka, va, qbs, kbs, vbs, g0,
                            cache_mla_ckv[:, 0], ckr, _feature_major(cache_na_k[:, 0]),
                            _feature_major(cache_na_v[:, 0]), wkk, wkv, _Row(gt, G_MLA_K), wout_e)
    cos_o, sin_o = _rope_tables(DEC_SEQ, HEAD_DIM, HEAD_DIM, 0)
    qc, kc, vc, qd, kd, vd, g1 = _sample_odd_proj(xs1, m_odd, *odd, cos_o, sin_o)
    xs2 = _sample_odd_attn(sink, xs1, m_odd, qc, kc, vc, qd, kd, vd, g1,
                           _feature_major(cache_gqa_k[:, 0]), _feature_major(cache_gqa_v[:, 0]),
                           _feature_major(cache_swa_k[:, 0]), _feature_major(cache_swa_v[:, 0]), wout_o)

    caches = (new_krope, new_na_k, new_na_v, new_gqa_k, new_gqa_v, new_swa_k, new_swa_v)
    return (xp2, xs2, new_ckv) + tuple(_token_major(c) for c in caches)
```

```python
import functools
from typing import NamedTuple

import jax
import jax.numpy as jnp
import numpy as np
from jax import lax
from jax.experimental import pallas as pl
from jax.experimental.pallas import tpu as pltpu

F32 = jnp.float32
BF16 = jnp.bfloat16

D_MODEL = 1024
SEQ = 256
DEC_SEQ = 1024
PAST_LEN = 256
GRID_W = 64
HEAD_DIM = 64
Q_RANK = 256
KV_RANK = 128
NOPE_A = 64
ROPE_A = 32
QK_A = NOPE_A + ROPE_A
N_HEADS = 8
NA_ROWS = 8
NA_COLS = 16
SWA_HALF = 128
ROPE_THETA = 10000.0
EPS = 1e-6
NEG_INF = -1e30
LOG2E = 1.4426950408889634

LANES = 128
Q_BLOCK = 512
NA_Q_BLOCK = 256
PROJ_BLOCK = 512
ODD_PROJ_BLOCK = 1024
PROMPT_BATCHES_PER_STEP = 2
N_PAIRS = N_HEADS // 2
RPB_ROWS = 2 * NA_ROWS - 1
RPB_COLS = 2 * NA_COLS - 1
BIAS_CHUNKS = 16
VMEM_LIMIT = 48 * 1024 * 1024

E_QLAT, E_CKV, E_KROPE, E_GA, E_QB, E_KB, E_VB, E_GB, E_END = 0, 256, 384, 416, 928, 1440, 1952, 2464, 2976
O_QC, O_KC, O_VC, O_GC, O_QD, O_KD, O_VD, O_GD, O_END = 0, 512, 640, 768, 1280, 1792, 1920, 2048, 2560


def _dot(a, b):
    return lax.dot_general(a, b, (((1,), (0,)), ((), ())), preferred_element_type=F32)


def _dot_nt(a, b):
    return lax.dot_general(a, b, (((1,), (1,)), ((), ())), preferred_element_type=F32)


def _silu(x):
    return x / (1.0 + jnp.exp(-x))


def _rms(x, g, n):
    ss = jnp.sum(x * x, axis=-1, keepdims=True)
    return x * lax.rsqrt(ss / n + EPS) * g


def _rms_halves(x, g2, lo):
    x2 = x * x
    s_lo = jnp.sum(jnp.where(lo, x2, 0.0), axis=-1, keepdims=True)
    s_hi = jnp.sum(jnp.where(lo, 0.0, x2), axis=-1, keepdims=True)
    r = jnp.where(lo, lax.rsqrt(s_lo / HEAD_DIM + EPS), lax.rsqrt(s_hi / HEAD_DIM + EPS))
    return x * r * g2


def _modulate(x, g, m):
    d = x.shape[-1]
    xn = x * lax.rsqrt(jnp.mean(x * x, axis=-1, keepdims=True) + EPS) * g
    return xn * (1.0 + m[:, d:2 * d]) + m[:, :d], m[:, 2 * d:]


def _split_lanes(x):
    hi = x.astype(BF16)
    lo = (x - hi.astype(F32)).astype(BF16)
    return jnp.concatenate([hi, lo], axis=1)


def _lane_matrix2(entries):
    i = lax.broadcasted_iota(jnp.int32, (LANES, LANES), 0)
    j = lax.broadcasted_iota(jnp.int32, (LANES, LANES), 1)
    m = entries(i, j).astype(BF16)
    return jnp.concatenate([m, m], axis=0)


def _rope_matrix2(rot_dim, period, start):
    half = rot_dim // 2

    def entries(i, j):
        pos = jnp.bitwise_and(j, period - 1) - start
        neg = (pos >= 0) & (pos < half) & (i == j + half)
        plus = (pos >= half) & (pos < rot_dim) & (i == j - half)
        return jnp.where(neg, -1.0, jnp.where(plus, 1.0, 0.0))

    return _lane_matrix2(entries)


def _swap_matrix2():
    return _lane_matrix2(lambda i, j: jnp.where(i == jnp.bitwise_xor(j, HEAD_DIM), 1.0, 0.0))


def _lane_mix(x, m2):
    return _dot(_split_lanes(x), m2)


def _with_ones(v, transposed=False):
    if transposed:
        return jnp.concatenate([v, jnp.ones((LANES, v.shape[1]), v.dtype)], axis=0)
    return jnp.concatenate([v, jnp.ones((v.shape[0], LANES), v.dtype)], axis=1)


def _attend(parts, sink=None):
    mx = None
    for s, _, _ in parts:
        pm = jnp.max(s, axis=-1, keepdims=True)
        mx = pm if mx is None else jnp.maximum(mx, pm)
    if sink is not None:
        mx = jnp.maximum(mx, sink)
    acc, den = None, None
    for s, v, v_t in parts:
        e = jnp.exp2(s - mx)
        po = (_dot_nt if v_t else _dot)(e.astype(BF16), v)
        acc = po if acc is None else acc + po
        if po.shape[1] == LANES:
            ps = jnp.sum(e, axis=-1, keepdims=True)
            den = ps if den is None else den + ps
    if den is None:
        den = acc[:, LANES:]
    if sink is not None:
        den = den + jnp.exp2(sink - mx)
    return acc[:, :LANES] * (1.0 / den)


def _lane_lo():
    return lax.broadcasted_iota(jnp.int32, (1, LANES), 1) < HEAD_DIM


def _store_pair_transposed(ref, bi, p, x):
    xt = x.T
    ref[bi, 0, 2 * p] = xt[:HEAD_DIM]
    ref[bi, 0, 2 * p + 1] = xt[HEAD_DIM:]


def _rope_key_slab(win_ref):
    d = win_ref.shape[1]
    return jnp.concatenate([jnp.zeros((NOPE_A, d), BF16), win_ref[E_KROPE:E_GA, :],
                            jnp.zeros((LANES - QK_A, d), BF16)], axis=0)


def _swap_halves(a):
    return jnp.concatenate([a[HEAD_DIM:], a[:HEAD_DIM]], axis=0)


def _mod_step(n_cond, is_first, bias_row, c_ref, w_ref, o_ref):
    @pl.when(is_first)
    def _():
        o_ref[:n_cond, :] = jnp.broadcast_to(bias_row, (n_cond, o_ref.shape[1]))
        o_ref[n_cond:, :] = jnp.zeros((o_ref.shape[0] - n_cond, o_ref.shape[1]), F32)

    s = _silu(c_ref[...])
    cols = [jnp.broadcast_to(s[:, r:r + 1], (s.shape[0], LANES)) for r in range(n_cond)]
    for t in range(w_ref.shape[1] // LANES):
        sl = slice(t * LANES, (t + 1) * LANES)
        w = w_ref[:, sl]
        for r in range(n_cond):
            o_ref[r:r + 1, sl] += jnp.sum(w * cols[r], axis=0, keepdims=True)


def _mla_keys(cb, kr, wkk_ref, wkv_ref, kg, rope=None):
    kk = _dot(cb, wkk_ref[...])
    keys = []
    for h in range(N_HEADS):
        k = _rms(kk[:, h * LANES:(h + 1) * LANES] + kr, kg, QK_A)
        if rope is not None:
            k = rope(k)
        keys.append(k.astype(BF16))
    return keys, _dot(cb, wkv_ref[...]).astype(BF16)


def _p0_kernel(n_cond, x_ref, m_ref, ng_ref, win_ref, qag_ref, wq_ref, kvag_ref, wkk_ref, wkv_ref, qg_ref, kg_ref,
               naqg_ref, nakg_ref, wout_ref, ct_ref, wm_ref, bm_ref, wio_ref, woo_ref,
               xo_ref, ckv_ref, krope_ref, nak_ref, nav_ref, mo_ref, wino_ref, wouto_ref, y_scr):
    _mod_step(n_cond, pl.program_id(0) == 0, bm_ref[1:2, :], ct_ref, wm_ref, mo_ref)
    wino_ref[...] = wio_ref[0].astype(BF16)
    wouto_ref[...] = woo_ref[0].astype(BF16)

    nbs = x_ref.shape[0]
    x = x_ref[...].reshape(nbs * SEQ, D_MODEL)
    h, gate = _modulate(x, ng_ref[...], m_ref[0:1, :])
    hb = h.astype(BF16)
    lo = _lane_lo()
    hi = jnp.logical_not(lo)
    rows = [slice(bi * SEQ, (bi + 1) * SEQ) for bi in range(nbs)]

    qln = _rms(_dot_nt(hb, win_ref[E_QLAT:E_CKV, :]), qag_ref[...], Q_RANK).astype(BF16)
    q_all = _dot(qln, wq_ref[...])
    ckv_n = _rms(_dot_nt(hb, win_ref[E_CKV:E_KROPE, :]), kvag_ref[...], KV_RANK)
    kr = _dot_nt(hb, _rope_key_slab(win_ref))
    for bi, rs in enumerate(rows):
        ckv_ref[bi, 0] = ckv_n[rs]
        krope_ref[bi, 0] = kr[rs].T[NOPE_A:QK_A]
    keys, vals = _mla_keys(ckv_n.astype(BF16), kr, wkk_ref, wkv_ref, kg_ref[...])
    qg = qg_ref[...] * (QK_A ** -0.5 * LOG2E)

    ga = _dot_nt(hb, win_ref[E_GA:E_QB, :])
    zq = _dot_nt(hb, win_ref[E_QB:E_KB, :])
    zk = _dot_nt(hb, win_ref[E_KB:E_VB, :])
    zv = _dot_nt(hb, win_ref[E_VB:E_GB, :])
    gb = _dot_nt(hb, win_ref[E_GB:E_END, :])
    naqg = naqg_ref[...] * (HEAD_DIM ** -0.5 * LOG2E)

    for p in range(N_PAIRS):
        sl = slice(p * LANES, (p + 1) * LANES)
        ys = slice(4 * LANES + p * LANES, 4 * LANES + (p + 1) * LANES)
        qhs = [_rms(q_all[:, hh * LANES:(hh + 1) * LANES], qg, QK_A).astype(BF16) for hh in (2 * p, 2 * p + 1)]
        qb = _rms_halves(zq[:, sl], naqg, lo)
        kb = _rms_halves(zk[:, sl], nakg_ref[...], lo)
        vb = zv[:, sl]
        kbb, vbb = kb.astype(BF16), vb.astype(BF16)
        va = vals[:, sl]
        qms = [jnp.where(keep, qb, 0.0).astype(BF16) for keep in (lo, hi)]
        for bi, rs in enumerate(rows):
            o2 = [_attend([(_dot_nt(qhs[i][rs], keys[2 * p + i][rs]), va[rs], False)]) for i in (0, 1)]
            y_scr[rs, sl] = (jnp.where(lo, o2[0], o2[1]) * _silu(ga[rs, sl])).astype(BF16)
            _store_pair_transposed(nak_ref, bi, p, kb[rs])
            _store_pair_transposed(nav_ref, bi, p, vb[rs])
            o2 = [_attend([(_dot_nt(qms[i][rs], kbb[rs]), vbb[rs], False)]) for i in (0, 1)]
            y_scr[rs, ys] = (jnp.where(lo, o2[0], o2[1]) * _silu(gb[rs, sl])).astype(BF16)

    xo_ref[...] = (x + gate * _dot(y_scr[...], wout_ref[...])).reshape(nbs, SEQ, D_MODEL)


def _full(shape):
    n = len(shape)
    return pl.BlockSpec(shape, lambda *_: (0,) * n, pipeline_mode=pl.Buffered(1))


class _Row(NamedTuple):
    table: jax.Array
    row: int


def _spec(a):
    if isinstance(a, _Row):
        idx = (a.row,) + (0,) * (a.table.ndim - 1)
        return pl.BlockSpec((None,) + a.table.shape[1:], lambda *_: idx, pipeline_mode=pl.Buffered(1))
    return _full(a.shape)


def _arr(a):
    return a.table if isinstance(a, _Row) else a


def _prompt_even(x, m, ng, win, qag, wq, kvag, wkk, wkv, qg, kg, naqg, nakg, wout,
                 cond_t, n_cond, w_mod, b_mod, w_in_odd, w_out_odd):
    nb = x.shape[0]
    nbs = PROMPT_BATCHES_PER_STEP
    steps = nb // nbs
    assert nb % nbs == 0 and D_MODEL % (BF16_SUBLANES * steps) == 0
    tr = D_MODEL // steps
    ins = (m, ng, win, qag, wq, kvag, wkk, wkv, qg, kg, naqg, nakg, wout)
    return pl.pallas_call(
        functools.partial(_p0_kernel, n_cond),
        grid=(steps,),
        in_specs=[pl.BlockSpec((nbs, SEQ, D_MODEL), lambda b: (b, 0, 0))] + [_spec(a) for a in ins]
        + [pl.BlockSpec((tr, 8), lambda b: (b, 0)),
           pl.BlockSpec((None, tr, 3 * D_MODEL), lambda b: (1, b, 0)),
           _full(b_mod.shape),
           pl.BlockSpec((1, tr, O_END), lambda b: (0, b, 0)),
           pl.BlockSpec((1, tr, D_MODEL), lambda b: (0, b, 0))],
        out_specs=[pl.BlockSpec((nbs, SEQ, D_MODEL), lambda b: (b, 0, 0)),
                   pl.BlockSpec((nbs, 1, SEQ, KV_RANK), lambda b: (b, 0, 0, 0)),
                   pl.BlockSpec((nbs, 1, ROPE_A, SEQ), lambda b: (b, 0, 0, 0)),
                   pl.BlockSpec((nbs, 1, N_HEADS, HEAD_DIM, SEQ), lambda b: (b, 0, 0, 0, 0)),
                   pl.BlockSpec((nbs, 1, N_HEADS, HEAD_DIM, SEQ), lambda b: (b, 0, 0, 0, 0)),
                   pl.BlockSpec((8, 3 * D_MODEL), lambda b: (0, 0)),
                   pl.BlockSpec((tr, O_END), lambda b: (b, 0)),
                   pl.BlockSpec((tr, D_MODEL), lambda b: (b, 0))],
        out_shape=[jax.ShapeDtypeStruct((nb, SEQ, D_MODEL), F32),
                   jax.ShapeDtypeStruct((nb, 1, SEQ, KV_RANK), F32),
                   jax.ShapeDtypeStruct((nb, 1, ROPE_A, SEQ), F32),
                   jax.ShapeDtypeStruct((nb, 1, N_HEADS, HEAD_DIM, SEQ), F32),
                   jax.ShapeDtypeStruct((nb, 1, N_HEADS, HEAD_DIM, SEQ), F32),
                   jax.ShapeDtypeStruct((8, 3 * D_MODEL), F32),
                   jax.ShapeDtypeStruct((D_MODEL, O_END), BF16),
                   jax.ShapeDtypeStruct((D_MODEL, D_MODEL), BF16)],
        scratch_shapes=[pltpu.VMEM((nbs * SEQ, D_MODEL), BF16)],
        compiler_params=pltpu.CompilerParams(dimension_semantics=("arbitrary",), vmem_limit_bytes=VMEM_LIMIT),
        name="prompt_even",
    )(x, *map(_arr, ins), cond_t, w_mod, b_mod, w_in_odd, w_out_odd)


def _gqa_pair_operands(k, v, kg2, lo):
    kn = _rms_halves(k, kg2, lo)
    return kn, (kn.astype(BF16), pltpu.roll(kn, HEAD_DIM, 1).astype(BF16)), \
        (_with_ones(v.astype(BF16)), _with_ones(pltpu.roll(v, HEAD_DIM, 1).astype(BF16)))


def _p1_kernel(sink_ref, x_ref, m_ref, ng_ref, win_ref, gqg_ref, gkg_ref, sqg_ref, skg_ref, wout_ref,
               xo_ref, gk_ref, gv_ref, sk_ref, sv_ref, y_scr):
    nbs = x_ref.shape[0]
    x = x_ref[...].reshape(nbs * SEQ, D_MODEL)
    h, gate = _modulate(x, ng_ref[...], m_ref[0:1, :])
    hb = h.astype(BF16)
    lo = _lane_lo()
    hi = jnp.logical_not(lo)
    sc = HEAD_DIM ** -0.5 * LOG2E
    rows = [slice(bi * SEQ, (bi + 1) * SEQ) for bi in range(nbs)]

    branches = ((O_QC, O_KC, O_VC, O_GC, gqg_ref, gkg_ref, gk_ref, gv_ref, False, 0),
                (O_QD, O_KD, O_VD, O_GD, sqg_ref, skg_ref, sk_ref, sv_ref, True, 4 * LANES))
    for oq, ok, ov, og, qg_ref, kg_ref, ck_ref, cv_ref, has_sink, yoff in branches:
        zq = _dot(hb, win_ref[:, oq:oq + 4 * LANES])
        zkv = _dot(hb, win_ref[:, ok:ok + 2 * LANES])
        zg = _dot(hb, win_ref[:, og:og + 4 * LANES])
        v = zkv[:, LANES:]
        kn, ks, vs = _gqa_pair_operands(zkv[:, :LANES], v, kg_ref[...], lo)
        for bi, rs in enumerate(rows):
            _store_pair_transposed(ck_ref, bi, 0, kn[rs])
            _store_pair_transposed(cv_ref, bi, 0, v[rs])
        qg = qg_ref[...] * sc
        for p in range(N_PAIRS):
            sl = slice(p * LANES, (p + 1) * LANES)
            qn = _rms_halves(zq[:, sl], qg, lo)
            qms = [jnp.where(keep, qn, 0.0).astype(BF16) for keep in (lo, hi)]
            kv = p // 2
            for bi, rs in enumerate(rows):
                o2 = []
                for half in (0, 1):
                    swap = 0 if kv == half else 1
                    sink = sink_ref[2 * p + half] * LOG2E if has_sink else None
                    o2.append(_attend([(_dot_nt(qms[half][rs], ks[swap][rs]), vs[swap][rs], False)], sink))
                o = jnp.where(lo, o2[0], o2[1])
                y_scr[rs, yoff + p * LANES:yoff + (p + 1) * LANES] = (o * _silu(zg[rs, sl])).astype(BF16)

    xo_ref[...] = (x + gate * _dot(y_scr[...], wout_ref[...])).reshape(nbs, SEQ, D_MODEL)


def _prompt_odd(sink, x, m, ng, win, gqg, gkg, sqg, skg, wout):
    nb = x.shape[0]
    nbs = PROMPT_BATCHES_PER_STEP
    assert nb % nbs == 0
    ins = (m, ng, win, gqg, gkg, sqg, skg, wout)
    cache_spec = pl.BlockSpec((nbs, 1, 2, HEAD_DIM, SEQ), lambda b: (b, 0, 0, 0, 0))
    cache_shape = jax.ShapeDtypeStruct((nb, 1, 2, HEAD_DIM, SEQ), F32)
    return pl.pallas_call(
        _p1_kernel,
        grid=(nb // nbs,),
        in_specs=[pl.BlockSpec(memory_space=pltpu.SMEM),
                  pl.BlockSpec((nbs, SEQ, D_MODEL), lambda b: (b, 0, 0))] + [_spec(a) for a in ins],
        out_specs=[pl.BlockSpec((nbs, SEQ, D_MODEL), lambda b: (b, 0, 0))] + [cache_spec] * 4,
        out_shape=[jax.ShapeDtypeStruct((nb, SEQ, D_MODEL), F32)] + [cache_shape] * 4,
        scratch_shapes=[pltpu.VMEM((nbs * SEQ, D_MODEL), BF16)],
        compiler_params=pltpu.CompilerParams(dimension_semantics=("arbitrary",), vmem_limit_bytes=VMEM_LIMIT),
        name="prompt_odd",
    )(sink, x, *map(_arr, ins))


def _s0a_kernel(x_ref, m_ref, ng_ref, win_ref, qag_ref, wq_ref, kvag_ref, wkk_ref, wkv_ref, qg_ref, kg_ref,
                naqg_ref, nakg_ref, cos_ref, sin_ref,
                qa_ref, ka_ref, va_ref, qb_ref, kb_ref, vb_ref, g_ref):
    b = pl.program_id(0)
    lo = _lane_lo()
    partner = _rope_matrix2(ROPE_A, LANES, NOPE_A)
    hb = _modulate(x_ref[0], ng_ref[...], m_ref[pl.ds(1 + b, 1), :])[0].astype(BF16)
    cos, sin = cos_ref[...], sin_ref[...]

    qln = _rms(_dot_nt(hb, win_ref[E_QLAT:E_CKV, :]), qag_ref[...], Q_RANK).astype(BF16)
    q_all = _dot(qln, wq_ref[...])
    ckv_n = _rms(_dot_nt(hb, win_ref[E_CKV:E_KROPE, :]), kvag_ref[...], KV_RANK)
    kr = _dot_nt(hb, _rope_key_slab(win_ref))
    cb = ckv_n.astype(BF16)
    kk = _dot(cb, wkk_ref[...])
    va_ref[0] = _dot(cb, wkv_ref[...]).astype(BF16)
    zq = _dot_nt(hb, win_ref[E_QB:E_KB, :])
    zk = _dot_nt(hb, win_ref[E_KB:E_VB, :])
    vb_ref[0] = _dot_nt(hb, win_ref[E_VB:E_GB, :]).astype(BF16)
    g_ref[0, :, 0:4 * LANES] = _silu(_dot_nt(hb, win_ref[E_GA:E_QB, :]))
    g_ref[0, :, 4 * LANES:8 * LANES] = _silu(_dot_nt(hb, win_ref[E_GB:E_END, :]))

    qg = qg_ref[...] * (QK_A ** -0.5 * LOG2E)
    kg = kg_ref[...]
    k_partner = _lane_mix(kr * kg, partner) * sin
    for hh in range(N_HEADS):
        sl = slice(hh * LANES, (hh + 1) * LANES)
        qn = _rms(q_all[:, sl], qg, QK_A)
        qa_ref[0, :, sl] = (qn * cos + _lane_mix(qn, partner) * sin).astype(BF16)
        k_raw = kk[:, sl] + kr
        k_inv = lax.rsqrt(jnp.sum(k_raw * k_raw, axis=-1, keepdims=True) / QK_A + EPS)
        ka_ref[0, :, sl] = ((k_raw * kg * cos + k_partner) * k_inv).astype(BF16)
    naqg = naqg_ref[...] * (HEAD_DIM ** -0.5 * LOG2E)
    for p in range(N_PAIRS):
        sl = slice(p * LANES, (p + 1) * LANES)
        qb_ref[0, :, sl] = _rms_halves(zq[:, sl], naqg, lo).astype(BF16)
        kb_ref[0, :, sl] = _rms_halves(zk[:, sl], nakg_ref[...], lo).astype(BF16)


def _sample_even_proj(x, m, ng, win, qag, wq, kvag, wkk, wkv, qg, kg, naqg, nakg, cos, sin):
    nb, s, _ = x.shape
    nq = s // PROJ_BLOCK
    ins = (m, ng, win, qag, wq, kvag, wkk, wkv, qg, kg, naqg, nakg)
    tab = pl.BlockSpec((PROJ_BLOCK, LANES), lambda b, j: (j, 0))

    def blk(w):
        return pl.BlockSpec((1, PROJ_BLOCK, w), lambda b, j: (b, j, 0))

    def shp(w, dt):
        return jax.ShapeDtypeStruct((nb, s, w), dt)

    return pl.pallas_call(
        _s0a_kernel,
        grid=(nb, nq),
        in_specs=[blk(D_MODEL)] + [_spec(a) for a in ins] + [tab, tab],
        out_specs=[blk(1024), blk(1024), blk(512), blk(512), blk(512), blk(512), blk(1024)],
        out_shape=[shp(1024, BF16), shp(1024, BF16), shp(512, BF16), shp(512, BF16), shp(512, BF16),
                   shp(512, BF16), shp(1024, F32)],
        compiler_params=pltpu.CompilerParams(dimension_semantics=("arbitrary", "arbitrary"),
                                             vmem_limit_bytes=VMEM_LIMIT),
        name="sample_even_proj",
    )(x, *map(_arr, ins), cos, sin)


def _build_bias_table(rpb_ref, tile_scr, tab_ref):
    qc = lax.broadcasted_iota(jnp.int32, (GRID_W, LANES), 0)
    lane = lax.broadcasted_iota(jnp.int32, (GRID_W, LANES), 1)
    kc = jnp.bitwise_and(lane, GRID_W - 1)
    lo = lane < GRID_W
    diff = kc - qc + (NA_COLS - 1)
    cs = jnp.clip(qc - NA_COLS // 2, 0, GRID_W - NA_COLS)
    valid = (kc >= cs) & (kc < cs + NA_COLS)
    tab_ref[...] = jnp.zeros(tab_ref.shape, F32)
    tile_scr[RPB_ROWS] = jnp.zeros((GRID_W, LANES), F32)

    def per_head(h, carry):
        for dr in range(RPB_ROWS):
            t = jnp.zeros((GRID_W, LANES), F32)
            for dc in range(RPB_COLS):
                t = jnp.where(diff == dc, rpb_ref[(h * RPB_ROWS + dr) * RPB_COLS + dc], t)
            tile_scr[dr] = jnp.where(valid, t * LOG2E, NEG_INF)
        for c in range(NA_ROWS // 2, NA_ROWS // 2 + NA_ROWS):
            d0 = 2 * c - NA_ROWS
            tab_ref[0, h, c] = jnp.where(lo, tile_scr[d0], tile_scr[d0 + 1])
            tab_ref[1, h, c] = jnp.where(lo, tile_scr[d0 - 1 if d0 > 0 else RPB_ROWS], tile_scr[d0])
        return carry

    lax.fori_loop(0, N_HEADS, per_head, 0)


def _s0b_kernel(rpb_ref, x_ref, m_ref, qa_ref, ka_ref, va_ref, qb_ref, kb_ref, vb_ref, g_ref,
                cckv_ref, ckr_ref, cnk_ref, cnv_ref, wkk_ref, wkv_ref, kg_ref, wout_ref,
                xo_ref, kca_scr, vca_scr, tile_scr, tab_scr, y_scr):
    b = pl.program_id(0)
    j = pl.program_id(1)
    lo = _lane_lo()
    n_lat = ka_ref.shape[1]

    @pl.when((b == 0) & (j == 0))
    def _():
        _build_bias_table(rpb_ref, tile_scr, tab_scr)

    @pl.when(j == 0)
    def _():
        kr_t = jnp.concatenate([jnp.zeros((NOPE_A, PAST_LEN), F32), ckr_ref[0],
                                jnp.zeros((LANES - QK_A, PAST_LEN), F32)], axis=0)
        keys, vals = _mla_keys(cckv_ref[0].astype(BF16), kr_t.T, wkk_ref, wkv_ref, kg_ref[...])
        for hh in range(N_HEADS):
            kca_scr[:, hh * LANES:(hh + 1) * LANES] = keys[hh]
        vca_scr[...] = vals

    kidx = lax.broadcasted_iota(jnp.int32, (1, n_lat), 1)
    for p in range(N_PAIRS):
        sl = slice(p * LANES, (p + 1) * LANES)
        o2 = []
        va = _with_ones(va_ref[0, :, sl])
        vca = _with_ones(vca_scr[:, sl])
        for hh in (2 * p, 2 * p + 1):
            hs = slice(hh * LANES, (hh + 1) * LANES)
            q = qa_ref[0, :, hs]
            o2.append(_attend([(_dot_nt(q, ka_ref[0, :, hs]), va, False),
                               (_dot_nt(q, kca_scr[:, hs]), vca, False)]))
        oa = jnp.where(lo, o2[0], o2[1])
        y_scr[:, sl] = (oa * g_ref[0, :, sl]).astype(BF16)

        qb = qb_ref[0, :, sl]
        kb = kb_ref[0, :, sl]
        vb = _with_ones(vb_ref[0, :, sl])
        kcb = cnk_ref[0, sl, :].astype(BF16)
        vcb = _with_ones(cnv_ref[0, sl, :].astype(BF16), transposed=True)
        o2 = []
        for half in (0, 1):
            head = 2 * p + half
            qm = jnp.where(lo if half == 0 else jnp.logical_not(lo), qb, jnp.zeros_like(qb))
            s_lat = _dot_nt(qm, kb)
            rows = []
            for local in range(NA_Q_BLOCK // GRID_W):
                qr = j * (NA_Q_BLOCK // GRID_W) + local
                par = 0 if local % 2 == 1 else 1
                c0 = (RPB_ROWS + par - local) // 2 - (NA_Q_BLOCK // GRID_W // 2) * j
                bias = jnp.concatenate([tab_scr[par, head, c0 + t] for t in range(n_lat // LANES)], axis=1)
                r0 = jnp.clip(qr - NA_ROWS // 2, 0, n_lat // GRID_W - NA_ROWS) * GRID_W
                ok = (kidx >= r0) & (kidx < r0 + NA_ROWS * GRID_W)
                rows.append(jnp.where(ok, s_lat[local * GRID_W:(local + 1) * GRID_W] + bias, NEG_INF))
            s_lat = jnp.concatenate(rows, axis=0)
            o2.append(_attend([(s_lat, vb, False), (_dot(qm, kcb), vcb, True)]))
        ob = jnp.where(lo, o2[0], o2[1])
        ys = slice(4 * LANES + p * LANES, 4 * LANES + (p + 1) * LANES)
        y_scr[:, ys] = (ob * g_ref[0, :, ys]).astype(BF16)

    d = x_ref.shape[-1]
    gate = m_ref[pl.ds(1 + b, 1), 2 * d:]
    xo_ref[0] = x_ref[0] + gate * _dot(y_scr[...], wout_ref[...])


def _sample_even_attn(rpb, x, m, qa, ka, va, qb, kb, vb, g, cckv, ckr, cnk, cnv, wkk, wkv, kg, wout):
    nb, s, _ = x.shape
    nq = s // NA_Q_BLOCK

    def blk(w):
        return pl.BlockSpec((1, NA_Q_BLOCK, w), lambda b, j: (b, j, 0))

    def per_batch(a):
        return pl.BlockSpec((1,) + a.shape[1:], lambda b, j: (b, 0, 0))

    return pl.pallas_call(
        _s0b_kernel,
        grid=(nb, nq),
        in_specs=[pl.BlockSpec(memory_space=pltpu.SMEM), blk(D_MODEL), _spec(m),
                  blk(1024), per_batch(ka), per_batch(va), blk(512), per_batch(kb), per_batch(vb), blk(1024),
                  per_batch(cckv), per_batch(ckr), per_batch(cnk), per_batch(cnv),
                  _full(wkk.shape), _full(wkv.shape), _spec(kg), _full(wout.shape)],
        out_specs=blk(D_MODEL),
        out_shape=jax.ShapeDtypeStruct(x.shape, F32),
        scratch_shapes=[pltpu.VMEM((PAST_LEN, N_HEADS * LANES), BF16),
                        pltpu.VMEM((PAST_LEN, N_HEADS * HEAD_DIM), BF16),
                        pltpu.VMEM((RPB_ROWS + 1, GRID_W, LANES), F32),
                        pltpu.VMEM((2, N_HEADS, BIAS_CHUNKS, GRID_W, LANES), F32),
                        pltpu.VMEM((NA_Q_BLOCK, D_MODEL), BF16)],
        compiler_params=pltpu.CompilerParams(dimension_semantics=("arbitrary", "arbitrary"),
                                             vmem_limit_bytes=VMEM_LIMIT),
        name="sample_even_attn",
    )(rpb, x, _arr(m), qa, ka, va, qb, kb, vb, g, cckv, ckr, cnk, cnv, wkk, wkv, _arr(kg), wout)


def _s1a_kernel(x_ref, m_ref, ng_ref, win_ref, gqg_ref, gkg_ref, sqg_ref, skg_ref, cos_ref, sin_ref,
                qc_ref, kc_ref, vc_ref, qd_ref, kd_ref, vd_ref, g_ref):
    b = pl.program_id(0)
    lo = _lane_lo()
    partner = _rope_matrix2(HEAD_DIM, HEAD_DIM, 0)
    swap = _swap_matrix2()[:LANES]
    hb = _modulate(x_ref[0], ng_ref[...], m_ref[pl.ds(1 + b, 1), :])[0].astype(BF16)
    cos, sin = cos_ref[...], sin_ref[...]
    sc = HEAD_DIM ** -0.5 * LOG2E

    def rope(t):
        return t * cos + _lane_mix(t, partner) * sin

    branches = ((O_QC, O_KC, O_GC, gqg_ref, gkg_ref, qc_ref, kc_ref, vc_ref, 0),
                (O_QD, O_KD, O_GD, sqg_ref, skg_ref, qd_ref, kd_ref, vd_ref, 4 * LANES))
    for oq, ok, og, qg_ref, kg_ref, q_out, k_out, v_out, goff in branches:
        zq = _dot(hb, win_ref[:, oq:oq + 4 * LANES])
        zkv = _dot(hb, win_ref[:, ok:ok + 2 * LANES])
        qg = qg_ref[...] * sc
        for p in range(N_PAIRS):
            sl = slice(p * LANES, (p + 1) * LANES)
            q_out[0, :, sl] = rope(_rms_halves(zq[:, sl], qg, lo)).astype(BF16)
        kn = rope(_rms_halves(zkv[:, :LANES], kg_ref[...], lo))
        v = zkv[:, LANES:]
        for out, val in ((k_out, kn.astype(BF16)), (v_out, v.astype(BF16))):
            out[0, :, 0:LANES] = val
            out[0, :, LANES:2 * LANES] = _dot(val, swap).astype(BF16)
        g_ref[0, :, goff:goff + 4 * LANES] = _silu(_dot(hb, win_ref[:, og:og + 4 * LANES]))


def _sample_odd_proj(x, m, ng, win, gqg, gkg, sqg, skg, cos, sin):
    nb, s, _ = x.shape
    nq = s // ODD_PROJ_BLOCK
    ins = (m, ng, win, gqg, gkg, sqg, skg)
    tab = pl.BlockSpec((ODD_PROJ_BLOCK, LANES), lambda b, j: (j, 0))

    def blk(w):
        return pl.BlockSpec((1, ODD_PROJ_BLOCK, w), lambda b, j: (b, j, 0))

    def shp(w, dt):
        return jax.ShapeDtypeStruct((nb, s, w), dt)

    return pl.pallas_call(
        _s1a_kernel,
        grid=(nb, nq),
        in_specs=[blk(D_MODEL)] + [_spec(a) for a in ins] + [tab, tab],
        out_specs=[blk(512), blk(256), blk(256), blk(512), blk(256), blk(256), blk(1024)],
        out_shape=[shp(512, BF16), shp(256, BF16), shp(256, BF16), shp(512, BF16), shp(256, BF16),
                   shp(256, BF16), shp(1024, F32)],
        compiler_params=pltpu.CompilerParams(dimension_semantics=("arbitrary", "arbitrary"),
                                             vmem_limit_bytes=VMEM_LIMIT),
        name="sample_odd_proj",
    )(x, *map(_arr, ins), cos, sin)


def _s1b_kernel(sink_ref, x_ref, m_ref, qc_ref, kc_ref, vc_ref, qd_ref, kd_ref, vd_ref, g_ref,
                cgk_ref, cgv_ref, csk_ref, csv_ref, wout_ref, xo_ref, y_scr):
    b = pl.program_id(0)
    j = pl.program_id(1)
    lo = _lane_lo()
    n_lat = kc_ref.shape[1]
    win_keys = Q_BLOCK + 2 * SWA_HALF

    def ctx_pair(ref, values=False):
        a = ref[0].astype(BF16)
        pair = (a, _swap_halves(a))
        return tuple(_with_ones(t, transposed=True) for t in pair) if values else pair

    cgk, cgv, csk, csv = ctx_pair(cgk_ref), ctx_pair(cgv_ref, True), ctx_pair(csk_ref), ctx_pair(csv_ref, True)
    vcs = [_with_ones(vc_ref[0, :, w * LANES:(w + 1) * LANES]) for w in (0, 1)]

    ks = pl.multiple_of(jnp.clip(j * Q_BLOCK - SWA_HALF, 0, n_lat - win_keys), SWA_HALF)
    qpos = j * Q_BLOCK + lax.broadcasted_iota(jnp.int32, (Q_BLOCK, win_keys), 0)
    kpos = ks + lax.broadcasted_iota(jnp.int32, (Q_BLOCK, win_keys), 1)
    band = jnp.abs(qpos - kpos) <= SWA_HALF
    vds = [_with_ones(vd_ref[0, pl.ds(ks, win_keys), w * LANES:(w + 1) * LANES]) for w in (0, 1)]

    for p in range(N_PAIRS):
        sl = slice(p * LANES, (p + 1) * LANES)
        kv = p // 2
        qc = qc_ref[0, :, sl]
        qd = qd_ref[0, :, sl]
        oc2, od2 = [], []
        for half in (0, 1):
            swap = 0 if kv == half else 1
            ws = slice(swap * LANES, (swap + 1) * LANES)
            keep = lo if half == 0 else jnp.logical_not(lo)
            qm = jnp.where(keep, qc, jnp.zeros_like(qc))
            oc2.append(_attend([(_dot_nt(qm, kc_ref[0, :, ws]), vcs[swap], False),
                                (_dot(qm, cgk[swap]), cgv[swap], True)]))
            qm = jnp.where(keep, qd, jnp.zeros_like(qd))
            s_loc = jnp.where(band, _dot_nt(qm, kd_ref[0, pl.ds(ks, win_keys), ws]), NEG_INF)
            od2.append(_attend([(s_loc, vds[swap], False),
                                (_dot(qm, csk[swap]), csv[swap], True)], sink_ref[2 * p + half] * LOG2E))
        y_scr[:, sl] = (jnp.where(lo, oc2[0], oc2[1]) * g_ref[0, :, sl]).astype(BF16)
        ys = slice(4 * LANES + p * LANES, 4 * LANES + (p + 1) * LANES)
        y_scr[:, ys] = (jnp.where(lo, od2[0], od2[1]) * g_ref[0, :, ys]).astype(BF16)

    d = x_ref.shape[-1]
    gate = m_ref[pl.ds(1 + b, 1), 2 * d:]
    xo_ref[0] = x_ref[0] + gate * _dot(y_scr[...], wout_ref[...])


def _sample_odd_attn(sink, x, m, qc, kc, vc, qd, kd, vd, g, cgk, cgv, csk, csv, wout):
    nb, s, _ = x.shape
    nq = s // Q_BLOCK

    def blk(w):
        return pl.BlockSpec((1, Q_BLOCK, w), lambda b, j: (b, j, 0))

    def per_batch(a):
        return pl.BlockSpec((1,) + a.shape[1:], lambda b, j: (b, 0, 0))

    return pl.pallas_call(
        _s1b_kernel,
        grid=(nb, nq),
        in_specs=[pl.BlockSpec(memory_space=pltpu.SMEM), blk(D_MODEL), _spec(m),
                  blk(512), per_batch(kc), per_batch(vc), blk(512), per_batch(kd), per_batch(vd), blk(1024),
                  per_batch(cgk), per_batch(cgv), per_batch(csk), per_batch(csv), _full(wout.shape)],
        out_specs=blk(D_MODEL),
        out_shape=jax.ShapeDtypeStruct(x.shape, F32),
        scratch_shapes=[pltpu.VMEM((Q_BLOCK, D_MODEL), BF16)],
        compiler_params=pltpu.CompilerParams(dimension_semantics=("arbitrary", "arbitrary"),
                                             vmem_limit_bytes=VMEM_LIMIT),
        name="sample_odd_attn",
    )(sink, x, _arr(m), qc, kc, vc, qd, kd, vd, g, cgk, cgv, csk, csv, wout)


WEIGHT_PREP_STEPS = 8
EVEN_IN_CHUNKS = 6
COND_PREP_ROWS = 512
BF16_SUBLANES = 16


G_MLA_Q, G_MLA_K, G_NA_Q, G_NA_K, G_GQA_Q, G_GQA_K, G_SWA_Q, G_SWA_K, N_GAINS = range(9)


GAIN_WIDTHS = (QK_A, QK_A) + (HEAD_DIM,) * 6


def _cond_prep_kernel(n_cond, ct_ref, wm_ref, bm_ref, ng_ref, gains_ref, mo_ref, gt_ref, ngt_ref):
    _mod_step(n_cond, pl.program_id(0) == 0, bm_ref[0:1, :], ct_ref, wm_ref, mo_ref)
    gt_ref[...] = jnp.zeros(gt_ref.shape, F32)
    start = 0
    for r, w in enumerate(GAIN_WIDTHS):
        g = gains_ref[:, start:start + w]
        start += w
        for off in range(0, LANES - w + 1, w):
            gt_ref[r, :, off:off + w] = g
    for layer in range(ngt_ref.shape[0]):
        ngt_ref[layer] = ng_ref[layer:layer + 1, :]


def _cond_prep(cond_t, n_cond, w_mod, b_mod, norm_g, gains):
    assert len(gains) == N_GAINS and tuple(g.shape[-1] for g in gains) == GAIN_WIDTHS
    gains_row = jnp.concatenate([g.reshape(1, -1) for g in gains], axis=1)
    tk = COND_PREP_ROWS
    return pl.pallas_call(
        functools.partial(_cond_prep_kernel, n_cond),
        grid=(D_MODEL // tk,),
        in_specs=[pl.BlockSpec((tk, 8), lambda k: (k, 0)),
                  pl.BlockSpec((None, tk, 3 * D_MODEL), lambda k: (0, k, 0)),
                  _full(b_mod.shape), _full(norm_g.shape), _full(gains_row.shape)],
        out_specs=[_full((8, 3 * D_MODEL)), _full((N_GAINS, 1, LANES)), _full((norm_g.shape[0], 1, D_MODEL))],
        out_shape=[jax.ShapeDtypeStruct((8, 3 * D_MODEL), F32), jax.ShapeDtypeStruct((N_GAINS, 1, LANES), F32),
                   jax.ShapeDtypeStruct((norm_g.shape[0], 1, D_MODEL), F32)],
        compiler_params=pltpu.CompilerParams(dimension_semantics=("arbitrary",)),
        name="cond_prep",
    )(cond_t, w_mod, b_mod, norm_g, gains_row)


def _weight_prep_kernel(wie_ref, woe_ref, wqu_ref, wkv_ref, win_e_ref, wout_e_ref, wq_ref, wkk_ref, wkvv_ref):
    win_e_ref[...] = wie_ref[...].astype(BF16)
    wout_e_ref[...] = woe_ref[0].astype(BF16)

    wq_ref[...] = jnp.zeros(wq_ref.shape, BF16)
    for h in range(N_HEADS):
        wq_ref[:, h * LANES:h * LANES + QK_A] = wqu_ref[0, :, h * QK_A:(h + 1) * QK_A].astype(BF16)
    lo = _lane_lo()
    for p in range(N_PAIRS):
        a = wkv_ref[0, :, (2 * p) * LANES:(2 * p + 1) * LANES]
        c = wkv_ref[0, :, (2 * p + 1) * LANES:(2 * p + 2) * LANES]
        wkk_ref[:, (2 * p) * LANES:(2 * p + 1) * LANES] = jnp.where(lo, a, 0.0).astype(BF16)
        wkk_ref[:, (2 * p + 1) * LANES:(2 * p + 2) * LANES] = jnp.where(lo, c, 0.0).astype(BF16)
        wkvv_ref[:, p * LANES:(p + 1) * LANES] = jnp.where(lo, pltpu.roll(a, HEAD_DIM, 1), c).astype(BF16)


def _weight_prep(w_in_even_t, w_out_even, w_q_up, w_kv_up):
    n = WEIGHT_PREP_STEPS
    ins = (w_out_even, w_q_up, w_kv_up)
    out_cols = (D_MODEL, N_HEADS * LANES, N_HEADS * LANES, N_HEADS * HEAD_DIM)
    out_rows = (D_MODEL, Q_RANK, KV_RANK, KV_RANK)
    te = E_END // EVEN_IN_CHUNKS
    assert te * EVEN_IN_CHUNKS == E_END and te % BF16_SUBLANES == 0 and EVEN_IN_CHUNKS <= n
    even_spec = pl.BlockSpec((te, D_MODEL), lambda i: (jnp.minimum(i, EVEN_IN_CHUNKS - 1), 0))
    return pl.pallas_call(
        _weight_prep_kernel,
        grid=(n,),
        in_specs=[even_spec] + [pl.BlockSpec((1, a.shape[1] // n, a.shape[2]), lambda i: (0, i, 0)) for a in ins],
        out_specs=[even_spec] + [pl.BlockSpec((r // n, c), lambda i: (i, 0)) for r, c in zip(out_rows, out_cols)],
        out_shape=[jax.ShapeDtypeStruct((E_END, D_MODEL), BF16)]
        + [jax.ShapeDtypeStruct((r, c), BF16) for r, c in zip(out_rows, out_cols)],
        compiler_params=pltpu.CompilerParams(dimension_semantics=("arbitrary",), vmem_limit_bytes=VMEM_LIMIT),
        name="weight_prep",
    )(w_in_even_t, *ins)


def _feature_major(c):
    b, h, l, d = c.shape
    return jnp.swapaxes(c, -1, -2).reshape(b, h * d, l)


def _token_major(c):
    return jnp.swapaxes(c, -1, -2)


def _rope_tables(s, rot_dim, period, start):
    quarter = rot_dim // 4
    t = np.arange(s)
    inv = ROPE_THETA ** (-np.arange(quarter, dtype=np.float64) / quarter)
    row = (t // GRID_W).astype(np.float64)[:, None] * inv
    col = (t % GRID_W).astype(np.float64)[:, None] * inv
    ang = np.concatenate([row, col], axis=-1)
    cos, sin = np.cos(ang), np.sin(ang)
    pre = np.ones((s, start))
    post = np.zeros((s, period - start - rot_dim))
    c = np.concatenate([pre, cos, cos, post], axis=-1)
    sn = np.concatenate([0 * pre, sin, sin, post], axis=-1)
    rep = LANES // period
    return jnp.asarray(np.tile(c, (1, rep)), F32), jnp.asarray(np.tile(sn, (1, rep)), F32)


def kernel(x_prompt, x_sample, cache_mla_ckv, cache_mla_krope, cache_na_k, cache_na_v, cache_gqa_k, cache_gqa_v, cache_swa_k, cache_swa_v, c, c_ctx, norm_g, w_mod, b_mod, w_in_even, mla_qa_g, w_q_up, mla_kva_g, w_kv_up, mla_q_g, mla_k_g, na_q_g, na_k_g, na_rpb, w_out_even, w_in_odd, gqa_q_g, gqa_k_g, swa_q_g, swa_k_g, swa_sink, w_out_odd):
    n_dec = x_sample.shape[0]
    assert w_mod.shape[0] == 2 and n_dec + 1 <= 8

    cond_t = jnp.concatenate([c_ctx[:, None], c.T, jnp.zeros((D_MODEL, 7 - n_dec), F32)], axis=1)
    n_cond = 1 + n_dec
    gains = (mla_q_g, mla_k_g, na_q_g, na_k_g, gqa_q_g, gqa_k_g, swa_q_g, swa_k_g)
    win_e, wout_e, wq, wkk, wkv = _weight_prep(jnp.swapaxes(w_in_even[0], 0, 1), w_out_even, w_q_up, w_kv_up)
    m_even, gt, ngt = _cond_prep(cond_t, n_cond, w_mod, b_mod, norm_g, gains)
    even = (_Row(ngt, 0), win_e, mla_qa_g, wq, mla_kva_g, wkk, wkv,
            _Row(gt, G_MLA_Q), _Row(gt, G_MLA_K), _Row(gt, G_NA_Q), _Row(gt, G_NA_K))
    sink = swa_sink[0].astype(F32)

    xp1, new_ckv, new_krope, new_na_k, new_na_v, m_odd, win_o, wout_o = _prompt_even(
        x_prompt, m_even, *even, wout_e, cond_t, n_cond, w_mod, b_mod, w_in_odd, w_out_odd)
    odd = (_Row(ngt, 1), win_o, _Row(gt, G_GQA_Q), _Row(gt, G_GQA_K), _Row(gt, G_SWA_Q), _Row(gt, G_SWA_K))
    xp2, new_gqa_k, new_gqa_v, new_swa_k, new_swa_v = _prompt_odd(sink, xp1, m_odd, *odd, wout_o)

    cos_e, sin_e = _rope_tables(DEC_SEQ, ROPE_A, LANES, NOPE_A)
    qa, ka, va, qbs, kbs, vbs, g0 = _sample_even_proj(x_sample, m_even, *even, cos_e, sin_e)
    ckr = jnp.swapaxes(cache_mla_krope[:, 0], -1, -2)
    xs1 = _sample_even_attn(na_rpb[0].reshape(-1), x_sample, m_even, qa, ka, va, qbs, kbs, vbs, g0,
                            cache_mla_ckv[:, 0], ckr, _feature_major(cache_na_k[:, 0]),
                            _feature_major(cache_na_v[:, 0]), wkk, wkv, _Row(gt, G_MLA_K), wout_e)
    cos_o, sin_o = _rope_tables(DEC_SEQ, HEAD_DIM, HEAD_DIM, 0)
    qc, kc, vc, qd, kd, vd, g1 = _sample_odd_proj(xs1, m_odd, *odd, cos_o, sin_o)
    xs2 = _sample_odd_attn(sink, xs1, m_odd, qc, kc, vc, qd, kd, vd, g1,
                           _feature_major(cache_gqa_k[:, 0]), _feature_major(cache_gqa_v[:, 0]),
                           _feature_major(cache_swa_k[:, 0]), _feature_major(cache_swa_v[:, 0]), wout_o)

    caches = (new_krope, new_na_k, new_na_v, new_gqa_k, new_gqa_v, new_swa_k, new_swa_v)
    return (xp2, xs2, new_ckv) + tuple(_token_major(c) for c in caches)
```

```python
import functools
from typing import NamedTuple

import jax
import jax.numpy as jnp
import numpy as np
from jax import lax
from jax.experimental import pallas as pl
from jax.experimental.pallas import tpu as pltpu

F32 = jnp.float32
BF16 = jnp.bfloat16

D_MODEL = 1024
SEQ = 256
DEC_SEQ = 1024
PAST_LEN = 256
GRID_W = 64
HEAD_DIM = 64
Q_RANK = 256
KV_RANK = 128
NOPE_A = 64
ROPE_A = 32
QK_A = NOPE_A + ROPE_A
N_HEADS = 8
NA_ROWS = 8
NA_COLS = 16
SWA_HALF = 128
ROPE_THETA = 10000.0
EPS = 1e-6
NEG_INF = -1e30
LOG2E = 1.4426950408889634

LANES = 128
Q_BLOCK = 512
NA_Q_BLOCK = 256
PROJ_BLOCK = 512
ODD_PROJ_BLOCK = 512
PROMPT_BATCHES_PER_STEP = 2
N_PAIRS = N_HEADS // 2
RPB_ROWS = 2 * NA_ROWS - 1
RPB_COLS = 2 * NA_COLS - 1
BIAS_CHUNKS = 16
VMEM_LIMIT = 48 * 1024 * 1024

E_QLAT, E_CKV, E_KROPE, E_GA, E_QB, E_KB, E_VB, E_GB, E_END = 0, 256, 384, 416, 928, 1440, 1952, 2464, 2976
O_QC, O_KC, O_VC, O_GC, O_QD, O_KD, O_VD, O_GD, O_END = 0, 512, 640, 768, 1280, 1792, 1920, 2048, 2560


def _dot(a, b):
    return lax.dot_general(a, b, (((1,), (0,)), ((), ())), preferred_element_type=F32)


def _dot_nt(a, b):
    return lax.dot_general(a, b, (((1,), (1,)), ((), ())), preferred_element_type=F32)


def _silu(x):
    return x / (1.0 + jnp.exp(-x))


def _rms(x, g, n):
    ss = jnp.sum(x * x, axis=-1, keepdims=True)
    return x * lax.rsqrt(ss / n + EPS) * g


def _rms_halves(x, g2, lo):
    x2 = x * x
    s_lo = jnp.sum(jnp.where(lo, x2, 0.0), axis=-1, keepdims=True)
    s_hi = jnp.sum(jnp.where(lo, 0.0, x2), axis=-1, keepdims=True)
    r = jnp.where(lo, lax.rsqrt(s_lo / HEAD_DIM + EPS), lax.rsqrt(s_hi / HEAD_DIM + EPS))
    return x * r * g2


def _modulate(x, g, m):
    d = x.shape[-1]
    xn = x * lax.rsqrt(jnp.mean(x * x, axis=-1, keepdims=True) + EPS) * g
    return xn * (1.0 + m[:, d:2 * d]) + m[:, :d], m[:, 2 * d:]


def _split_lanes(x):
    hi = x.astype(BF16)
    lo = (x - hi.astype(F32)).astype(BF16)
    return jnp.concatenate([hi, lo], axis=1)


def _lane_matrix2(entries):
    i = lax.broadcasted_iota(jnp.int32, (LANES, LANES), 0)
    j = lax.broadcasted_iota(jnp.int32, (LANES, LANES), 1)
    m = entries(i, j).astype(BF16)
    return jnp.concatenate([m, m], axis=0)


def _rope_matrix2(rot_dim, period, start):
    half = rot_dim // 2

    def entries(i, j):
        pos = jnp.bitwise_and(j, period - 1) - start
        neg = (pos >= 0) & (pos < half) & (i == j + half)
        plus = (pos >= half) & (pos < rot_dim) & (i == j - half)
        return jnp.where(neg, -1.0, jnp.where(plus, 1.0, 0.0))

    return _lane_matrix2(entries)


def _swap_matrix2():
    return _lane_matrix2(lambda i, j: jnp.where(i == jnp.bitwise_xor(j, HEAD_DIM), 1.0, 0.0))


def _lane_mix(x, m2):
    return _dot(_split_lanes(x), m2)


def _with_ones(v, transposed=False):
    if transposed:
        return jnp.concatenate([v, jnp.ones((LANES, v.shape[1]), v.dtype)], axis=0)
    return jnp.concatenate([v, jnp.ones((v.shape[0], LANES), v.dtype)], axis=1)


def _attend(parts, sink=None):
    mx = None
    for s, _, _ in parts:
        pm = jnp.max(s, axis=-1, keepdims=True)
        mx = pm if mx is None else jnp.maximum(mx, pm)
    if sink is not None:
        mx = jnp.maximum(mx, sink)
    acc, den = None, None
    for s, v, v_t in parts:
        e = jnp.exp2(s - mx)
        po = (_dot_nt if v_t else _dot)(e.astype(BF16), v)
        acc = po if acc is None else acc + po
        if po.shape[1] == LANES:
            ps = jnp.sum(e, axis=-1, keepdims=True)
            den = ps if den is None else den + ps
    if den is None:
        den = acc[:, LANES:]
    if sink is not None:
        den = den + jnp.exp2(sink - mx)
    return acc[:, :LANES] * (1.0 / den)


def _lane_lo():
    return lax.broadcasted_iota(jnp.int32, (1, LANES), 1) < HEAD_DIM


def _store_pair_transposed(ref, bi, p, x):
    xt = x.T
    ref[bi, 0, 2 * p] = xt[:HEAD_DIM]
    ref[bi, 0, 2 * p + 1] = xt[HEAD_DIM:]


def _rope_key_slab(win_ref):
    d = win_ref.shape[1]
    return jnp.concatenate([jnp.zeros((NOPE_A, d), BF16), win_ref[E_KROPE:E_GA, :],
                            jnp.zeros((LANES - QK_A, d), BF16)], axis=0)


def _swap_halves(a):
    return jnp.concatenate([a[HEAD_DIM:], a[:HEAD_DIM]], axis=0)


def _mod_step(n_cond, is_first, bias_row, c_ref, w_ref, o_ref):
    @pl.when(is_first)
    def _():
        o_ref[:n_cond, :] = jnp.broadcast_to(bias_row, (n_cond, o_ref.shape[1]))
        o_ref[n_cond:, :] = jnp.zeros((o_ref.shape[0] - n_cond, o_ref.shape[1]), F32)

    s = _silu(c_ref[...])
    cols = [jnp.broadcast_to(s[:, r:r + 1], (s.shape[0], LANES)) for r in range(n_cond)]
    for t in range(w_ref.shape[1] // LANES):
        sl = slice(t * LANES, (t + 1) * LANES)
        w = w_ref[:, sl]
        for r in range(n_cond):
            o_ref[r:r + 1, sl] += jnp.sum(w * cols[r], axis=0, keepdims=True)


def _mla_keys(cb, kr, wkk_ref, wkv_ref, kg, rope=None):
    kk = _dot(cb, wkk_ref[...])
    keys = []
    for h in range(N_HEADS):
        k = _rms(kk[:, h * LANES:(h + 1) * LANES] + kr, kg, QK_A)
        if rope is not None:
            k = rope(k)
        keys.append(k.astype(BF16))
    return keys, _dot(cb, wkv_ref[...]).astype(BF16)


def _p0_kernel(n_cond, x_ref, m_ref, ng_ref, win_ref, qag_ref, wq_ref, kvag_ref, wkk_ref, wkv_ref, qg_ref, kg_ref,
               naqg_ref, nakg_ref, wout_ref, ct_ref, wm_ref, bm_ref, wio_ref, woo_ref,
               xo_ref, ckv_ref, krope_ref, nak_ref, nav_ref, mo_ref, wino_ref, wouto_ref, y_scr):
    _mod_step(n_cond, pl.program_id(0) == 0, bm_ref[1:2, :], ct_ref, wm_ref, mo_ref)
    wino_ref[...] = wio_ref[0].astype(BF16)
    wouto_ref[...] = woo_ref[0].astype(BF16)

    nbs = x_ref.shape[0]
    x = x_ref[...].reshape(nbs * SEQ, D_MODEL)
    h, gate = _modulate(x, ng_ref[...], m_ref[0:1, :])
    hb = h.astype(BF16)
    lo = _lane_lo()
    hi = jnp.logical_not(lo)
    rows = [slice(bi * SEQ, (bi + 1) * SEQ) for bi in range(nbs)]

    qln = _rms(_dot_nt(hb, win_ref[E_QLAT:E_CKV, :]), qag_ref[...], Q_RANK).astype(BF16)
    q_all = _dot(qln, wq_ref[...])
    ckv_n = _rms(_dot_nt(hb, win_ref[E_CKV:E_KROPE, :]), kvag_ref[...], KV_RANK)
    kr = _dot_nt(hb, _rope_key_slab(win_ref))
    for bi, rs in enumerate(rows):
        ckv_ref[bi, 0] = ckv_n[rs]
        krope_ref[bi, 0] = kr[rs].T[NOPE_A:QK_A]
    keys, vals = _mla_keys(ckv_n.astype(BF16), kr, wkk_ref, wkv_ref, kg_ref[...])
    qg = qg_ref[...] * (QK_A ** -0.5 * LOG2E)

    ga = _dot_nt(hb, win_ref[E_GA:E_QB, :])
    zq = _dot_nt(hb, win_ref[E_QB:E_KB, :])
    zk = _dot_nt(hb, win_ref[E_KB:E_VB, :])
    zv = _dot_nt(hb, win_ref[E_VB:E_GB, :])
    gb = _dot_nt(hb, win_ref[E_GB:E_END, :])
    naqg = naqg_ref[...] * (HEAD_DIM ** -0.5 * LOG2E)

    for p in range(N_PAIRS):
        sl = slice(p * LANES, (p + 1) * LANES)
        ys = slice(4 * LANES + p * LANES, 4 * LANES + (p + 1) * LANES)
        qhs = [_rms(q_all[:, hh * LANES:(hh + 1) * LANES], qg, QK_A).astype(BF16) for hh in (2 * p, 2 * p + 1)]
        qb = _rms_halves(zq[:, sl], naqg, lo)
        kb = _rms_halves(zk[:, sl], nakg_ref[...], lo)
        vb = zv[:, sl]
        kbb, vbb = kb.astype(BF16), vb.astype(BF16)
        va = vals[:, sl]
        qms = [jnp.where(keep, qb, 0.0).astype(BF16) for keep in (lo, hi)]
        for bi, rs in enumerate(rows):
            o2 = [_attend([(_dot_nt(qhs[i][rs], keys[2 * p + i][rs]), va[rs], False)]) for i in (0, 1)]
            y_scr[rs, sl] = (jnp.where(lo, o2[0], o2[1]) * _silu(ga[rs, sl])).astype(BF16)
            _store_pair_transposed(nak_ref, bi, p, kb[rs])
            _store_pair_transposed(nav_ref, bi, p, vb[rs])
            o2 = [_attend([(_dot_nt(qms[i][rs], kbb[rs]), vbb[rs], False)]) for i in (0, 1)]
            y_scr[rs, ys] = (jnp.where(lo, o2[0], o2[1]) * _silu(gb[rs, sl])).astype(BF16)

    xo_ref[...] = (x + gate * _dot(y_scr[...], wout_ref[...])).reshape(nbs, SEQ, D_MODEL)


def _full(shape):
    n = len(shape)
    return pl.BlockSpec(shape, lambda *_: (0,) * n, pipeline_mode=pl.Buffered(1))


class _Row(NamedTuple):
    table: jax.Array
    row: int


def _spec(a):
    if isinstance(a, _Row):
        idx = (a.row,) + (0,) * (a.table.ndim - 1)
        return pl.BlockSpec((None,) + a.table.shape[1:], lambda *_: idx, pipeline_mode=pl.Buffered(1))
    return _full(a.shape)


def _arr(a):
    return a.table if isinstance(a, _Row) else a


def _prompt_even(x, m, ng, win, qag, wq, kvag, wkk, wkv, qg, kg, naqg, nakg, wout,
                 cond_t, n_cond, w_mod, b_mod, w_in_odd, w_out_odd):
    nb = x.shape[0]
    nbs = PROMPT_BATCHES_PER_STEP
    steps = nb // nbs
    assert nb % nbs == 0 and D_MODEL % (BF16_SUBLANES * steps) == 0
    tr = D_MODEL // steps
    ins = (m, ng, win, qag, wq, kvag, wkk, wkv, qg, kg, naqg, nakg, wout)
    return pl.pallas_call(
        functools.partial(_p0_kernel, n_cond),
        grid=(steps,),
        in_specs=[pl.BlockSpec((nbs, SEQ, D_MODEL), lambda b: (b, 0, 0))] + [_spec(a) for a in ins]
        + [pl.BlockSpec((tr, 8), lambda b: (b, 0)),
           pl.BlockSpec((None, tr, 3 * D_MODEL), lambda b: (1, b, 0)),
           _full(b_mod.shape),
           pl.BlockSpec((1, tr, O_END), lambda b: (0, b, 0)),
           pl.BlockSpec((1, tr, D_MODEL), lambda b: (0, b, 0))],
        out_specs=[pl.BlockSpec((nbs, SEQ, D_MODEL), lambda b: (b, 0, 0)),
                   pl.BlockSpec((nbs, 1, SEQ, KV_RANK), lambda b: (b, 0, 0, 0)),
                   pl.BlockSpec((nbs, 1, ROPE_A, SEQ), lambda b: (b, 0, 0, 0)),
                   pl.BlockSpec((nbs, 1, N_HEADS, HEAD_DIM, SEQ), lambda b: (b, 0, 0, 0, 0)),
                   pl.BlockSpec((nbs, 1, N_HEADS, HEAD_DIM, SEQ), lambda b: (b, 0, 0, 0, 0)),
                   pl.BlockSpec((8, 3 * D_MODEL), lambda b: (0, 0)),
                   pl.BlockSpec((tr, O_END), lambda b: (b, 0)),
                   pl.BlockSpec((tr, D_MODEL), lambda b: (b, 0))],
        out_shape=[jax.ShapeDtypeStruct((nb, SEQ, D_MODEL), F32),
                   jax.ShapeDtypeStruct((nb, 1, SEQ, KV_RANK), F32),
                   jax.ShapeDtypeStruct((nb, 1, ROPE_A, SEQ), F32),
                   jax.ShapeDtypeStruct((nb, 1, N_HEADS, HEAD_DIM, SEQ), F32),
                   jax.ShapeDtypeStruct((nb, 1, N_HEADS, HEAD_DIM, SEQ), F32),
                   jax.ShapeDtypeStruct((8, 3 * D_MODEL), F32),
                   jax.ShapeDtypeStruct((D_MODEL, O_END), BF16),
                   jax.ShapeDtypeStruct((D_MODEL, D_MODEL), BF16)],
        scratch_shapes=[pltpu.VMEM((nbs * SEQ, D_MODEL), BF16)],
        compiler_params=pltpu.CompilerParams(dimension_semantics=("arbitrary",), vmem_limit_bytes=VMEM_LIMIT),
        name="prompt_even",
    )(x, *map(_arr, ins), cond_t, w_mod, b_mod, w_in_odd, w_out_odd)


def _gqa_pair_operands(k, v, kg2, lo):
    kn = _rms_halves(k, kg2, lo)
    return kn, (kn.astype(BF16), pltpu.roll(kn, HEAD_DIM, 1).astype(BF16)), \
        (_with_ones(v.astype(BF16)), _with_ones(pltpu.roll(v, HEAD_DIM, 1).astype(BF16)))


def _p1_kernel(sink_ref, x_ref, m_ref, ng_ref, win_ref, gqg_ref, gkg_ref, sqg_ref, skg_ref, wout_ref,
               xo_ref, gk_ref, gv_ref, sk_ref, sv_ref, y_scr):
    nbs = x_ref.shape[0]
    x = x_ref[...].reshape(nbs * SEQ, D_MODEL)
    h, gate = _modulate(x, ng_ref[...], m_ref[0:1, :])
    hb = h.astype(BF16)
    lo = _lane_lo()
    hi = jnp.logical_not(lo)
    sc = HEAD_DIM ** -0.5 * LOG2E
    rows = [slice(bi * SEQ, (bi + 1) * SEQ) for bi in range(nbs)]

    branches = ((O_QC, O_KC, O_VC, O_GC, gqg_ref, gkg_ref, gk_ref, gv_ref, False, 0),
                (O_QD, O_KD, O_VD, O_GD, sqg_ref, skg_ref, sk_ref, sv_ref, True, 4 * LANES))
    for oq, ok, ov, og, qg_ref, kg_ref, ck_ref, cv_ref, has_sink, yoff in branches:
        zq = _dot(hb, win_ref[:, oq:oq + 4 * LANES])
        zkv = _dot(hb, win_ref[:, ok:ok + 2 * LANES])
        zg = _dot(hb, win_ref[:, og:og + 4 * LANES])
        v = zkv[:, LANES:]
        kn, ks, vs = _gqa_pair_operands(zkv[:, :LANES], v, kg_ref[...], lo)
        for bi, rs in enumerate(rows):
            _store_pair_transposed(ck_ref, bi, 0, kn[rs])
            _store_pair_transposed(cv_ref, bi, 0, v[rs])
        qg = qg_ref[...] * sc
        for p in range(N_PAIRS):
            sl = slice(p * LANES, (p + 1) * LANES)
            qn = _rms_halves(zq[:, sl], qg, lo)
            qms = [jnp.where(keep, qn, 0.0).astype(BF16) for keep in (lo, hi)]
            kv = p // 2
            for bi, rs in enumerate(rows):
                o2 = []
                for half in (0, 1):
                    swap = 0 if kv == half else 1
                    sink = sink_ref[2 * p + half] * LOG2E if has_sink else None
                    o2.append(_attend([(_dot_nt(qms[half][rs], ks[swap][rs]), vs[swap][rs], False)], sink))
                o = jnp.where(lo, o2[0], o2[1])
                y_scr[rs, yoff + p * LANES:yoff + (p + 1) * LANES] = (o * _silu(zg[rs, sl])).astype(BF16)

    xo_ref[...] = (x + gate * _dot(y_scr[...], wout_ref[...])).reshape(nbs, SEQ, D_MODEL)


def _prompt_odd(sink, x, m, ng, win, gqg, gkg, sqg, skg, wout):
    nb = x.shape[0]
    nbs = PROMPT_BATCHES_PER_STEP
    assert nb % nbs == 0
    ins = (m, ng, win, gqg, gkg, sqg, skg, wout)
    cache_spec = pl.BlockSpec((nbs, 1, 2, HEAD_DIM, SEQ), lambda b: (b, 0, 0, 0, 0))
    cache_shape = jax.ShapeDtypeStruct((nb, 1, 2, HEAD_DIM, SEQ), F32)
    return pl.pallas_call(
        _p1_kernel,
        grid=(nb // nbs,),
        in_specs=[pl.BlockSpec(memory_space=pltpu.SMEM),
                  pl.BlockSpec((nbs, SEQ, D_MODEL), lambda b: (b, 0, 0))] + [_spec(a) for a in ins],
        out_specs=[pl.BlockSpec((nbs, SEQ, D_MODEL), lambda b: (b, 0, 0))] + [cache_spec] * 4,
        out_shape=[jax.ShapeDtypeStruct((nb, SEQ, D_MODEL), F32)] + [cache_shape] * 4,
        scratch_shapes=[pltpu.VMEM((nbs * SEQ, D_MODEL), BF16)],
        compiler_params=pltpu.CompilerParams(dimension_semantics=("arbitrary",), vmem_limit_bytes=VMEM_LIMIT),
        name="prompt_odd",
    )(sink, x, *map(_arr, ins))


def _s0a_kernel(x_ref, m_ref, ng_ref, win_ref, qag_ref, wq_ref, kvag_ref, wkk_ref, wkv_ref, qg_ref, kg_ref,
                naqg_ref, nakg_ref, cos_ref, sin_ref,
                qa_ref, ka_ref, va_ref, qb_ref, kvb_ref, g_ref):
    b = pl.program_id(0)
    lo = _lane_lo()
    partner = _rope_matrix2(ROPE_A, LANES, NOPE_A)
    hb = _modulate(x_ref[0], ng_ref[...], m_ref[pl.ds(1 + b, 1), :])[0].astype(BF16)
    cos, sin = cos_ref[...], sin_ref[...]

    qln = _rms(_dot_nt(hb, win_ref[E_QLAT:E_CKV, :]), qag_ref[...], Q_RANK).astype(BF16)
    q_all = _dot(qln, wq_ref[...])
    ckv_n = _rms(_dot_nt(hb, win_ref[E_CKV:E_KROPE, :]), kvag_ref[...], KV_RANK)
    kr = _dot_nt(hb, _rope_key_slab(win_ref))
    cb = ckv_n.astype(BF16)
    kk = _dot(cb, wkk_ref[...])
    va_ref[0] = _dot(cb, wkv_ref[...]).astype(BF16)
    zq = _dot_nt(hb, win_ref[E_QB:E_KB, :])
    zk = _dot_nt(hb, win_ref[E_KB:E_VB, :])
    kvb_ref[0, :, 4 * LANES:8 * LANES] = _dot_nt(hb, win_ref[E_VB:E_GB, :]).astype(BF16)
    g_ref[0, :, 0:4 * LANES] = _silu(_dot_nt(hb, win_ref[E_GA:E_QB, :]))
    g_ref[0, :, 4 * LANES:8 * LANES] = _silu(_dot_nt(hb, win_ref[E_GB:E_END, :]))

    qg = qg_ref[...] * (QK_A ** -0.5 * LOG2E)
    kg = kg_ref[...]
    k_partner = _lane_mix(kr * kg, partner) * sin
    for hh in range(N_HEADS):
        sl = slice(hh * LANES, (hh + 1) * LANES)
        qn = _rms(q_all[:, sl], qg, QK_A)
        qa_ref[0, :, sl] = (qn * cos + _lane_mix(qn, partner) * sin).astype(BF16)
        k_raw = kk[:, sl] + kr
        k_inv = lax.rsqrt(jnp.sum(k_raw * k_raw, axis=-1, keepdims=True) / QK_A + EPS)
        ka_ref[0, :, sl] = ((k_raw * kg * cos + k_partner) * k_inv).astype(BF16)
    naqg = naqg_ref[...] * (HEAD_DIM ** -0.5 * LOG2E)
    for p in range(N_PAIRS):
        sl = slice(p * LANES, (p + 1) * LANES)
        qb_ref[0, :, sl] = _rms_halves(zq[:, sl], naqg, lo).astype(BF16)
        kvb_ref[0, :, sl] = _rms_halves(zk[:, sl], nakg_ref[...], lo).astype(BF16)


def _sample_even_proj(x, m, ng, win, qag, wq, kvag, wkk, wkv, qg, kg, naqg, nakg, cos, sin):
    nb, s, _ = x.shape
    nq = s // PROJ_BLOCK
    ins = (m, ng, win, qag, wq, kvag, wkk, wkv, qg, kg, naqg, nakg)
    tab = pl.BlockSpec((PROJ_BLOCK, LANES), lambda b, j: (j, 0))

    def blk(w):
        return pl.BlockSpec((1, PROJ_BLOCK, w), lambda b, j: (b, j, 0))

    def shp(w, dt):
        return jax.ShapeDtypeStruct((nb, s, w), dt)

    return pl.pallas_call(
        _s0a_kernel,
        grid=(nb, nq),
        in_specs=[blk(D_MODEL)] + [_spec(a) for a in ins] + [tab, tab],
        out_specs=[blk(1024), blk(1024), blk(512), blk(512), blk(1024), blk(1024)],
        out_shape=[shp(1024, BF16), shp(1024, BF16), shp(512, BF16), shp(512, BF16), shp(1024, BF16),
                   shp(1024, F32)],
        compiler_params=pltpu.CompilerParams(dimension_semantics=("arbitrary", "arbitrary"),
                                             vmem_limit_bytes=VMEM_LIMIT),
        name="sample_even_proj",
    )(x, *map(_arr, ins), cos, sin)


def _build_bias_table(rpb_ref, tile_scr, tab_ref):
    qc = lax.broadcasted_iota(jnp.int32, (GRID_W, LANES), 0)
    lane = lax.broadcasted_iota(jnp.int32, (GRID_W, LANES), 1)
    kc = jnp.bitwise_and(lane, GRID_W - 1)
    lo = lane < GRID_W
    diff = kc - qc + (NA_COLS - 1)
    cs = jnp.clip(qc - NA_COLS // 2, 0, GRID_W - NA_COLS)
    valid = (kc >= cs) & (kc < cs + NA_COLS)
    tab_ref[...] = jnp.zeros(tab_ref.shape, F32)
    tile_scr[RPB_ROWS] = jnp.zeros((GRID_W, LANES), F32)

    def per_head(h, carry):
        for dr in range(RPB_ROWS):
            t = jnp.zeros((GRID_W, LANES), F32)
            for dc in range(RPB_COLS):
                t = jnp.where(diff == dc, rpb_ref[(h * RPB_ROWS + dr) * RPB_COLS + dc], t)
            tile_scr[dr] = jnp.where(valid, t * LOG2E, NEG_INF)
        for c in range(NA_ROWS // 2, NA_ROWS // 2 + NA_ROWS):
            d0 = 2 * c - NA_ROWS
            tab_ref[0, h, c] = jnp.where(lo, tile_scr[d0], tile_scr[d0 + 1])
            tab_ref[1, h, c] = jnp.where(lo, tile_scr[d0 - 1 if d0 > 0 else RPB_ROWS], tile_scr[d0])
        return carry

    lax.fori_loop(0, N_HEADS, per_head, 0)


def _s0b_kernel(rpb_ref, x_ref, m_ref, qa_ref, ka_ref, va_ref, qb_ref, kvb_ref, g_ref,
                cckv_ref, ckr_ref, cnk_ref, cnv_ref, wkk_ref, wkv_ref, kg_ref, wout_ref,
                xo_ref, kca_scr, vca_scr, tile_scr, tab_scr, y_scr):
    b = pl.program_id(0)
    j = pl.program_id(1)
    lo = _lane_lo()
    n_lat = ka_ref.shape[1]

    @pl.when((b == 0) & (j == 0))
    def _():
        _build_bias_table(rpb_ref, tile_scr, tab_scr)

    @pl.when(j == 0)
    def _():
        kr_t = jnp.concatenate([jnp.zeros((NOPE_A, PAST_LEN), F32), ckr_ref[0],
                                jnp.zeros((LANES - QK_A, PAST_LEN), F32)], axis=0)
        keys, vals = _mla_keys(cckv_ref[0].astype(BF16), kr_t.T, wkk_ref, wkv_ref, kg_ref[...])
        for hh in range(N_HEADS):
            kca_scr[:, hh * LANES:(hh + 1) * LANES] = keys[hh]
        vca_scr[...] = vals

    kidx = lax.broadcasted_iota(jnp.int32, (1, n_lat), 1)
    for p in range(N_PAIRS):
        sl = slice(p * LANES, (p + 1) * LANES)
        o2 = []
        va = _with_ones(va_ref[0, :, sl])
        vca = _with_ones(vca_scr[:, sl])
        for hh in (2 * p, 2 * p + 1):
            hs = slice(hh * LANES, (hh + 1) * LANES)
            q = qa_ref[0, :, hs]
            o2.append(_attend([(_dot_nt(q, ka_ref[0, :, hs]), va, False),
                               (_dot_nt(q, kca_scr[:, hs]), vca, False)]))
        oa = jnp.where(lo, o2[0], o2[1])
        y_scr[:, sl] = (oa * g_ref[0, :, sl]).astype(BF16)

        qb = qb_ref[0, :, sl]
        kb = kvb_ref[0, :, sl]
        vb = _with_ones(kvb_ref[0, :, 4 * LANES + p * LANES:4 * LANES + (p + 1) * LANES])
        kcb = cnk_ref[0, sl, :].astype(BF16)
        vcb = _with_ones(cnv_ref[0, sl, :].astype(BF16), transposed=True)
        o2 = []
        for half in (0, 1):
            head = 2 * p + half
            qm = jnp.where(lo if half == 0 else jnp.logical_not(lo), qb, jnp.zeros_like(qb))
            s_lat = _dot_nt(qm, kb)
            rows = []
            for local in range(NA_Q_BLOCK // GRID_W):
                qr = j * (NA_Q_BLOCK // GRID_W) + local
                par = 0 if local % 2 == 1 else 1
                c0 = (RPB_ROWS + par - local) // 2 - (NA_Q_BLOCK // GRID_W // 2) * j
                bias = jnp.concatenate([tab_scr[par, head, c0 + t] for t in range(n_lat // LANES)], axis=1)
                r0 = jnp.clip(qr - NA_ROWS // 2, 0, n_lat // GRID_W - NA_ROWS) * GRID_W
                ok = (kidx >= r0) & (kidx < r0 + NA_ROWS * GRID_W)
                rows.append(jnp.where(ok, s_lat[local * GRID_W:(local + 1) * GRID_W] + bias, NEG_INF))
            s_lat = jnp.concatenate(rows, axis=0)
            o2.append(_attend([(s_lat, vb, False), (_dot(qm, kcb), vcb, True)]))
        ob = jnp.where(lo, o2[0], o2[1])
        ys = slice(4 * LANES + p * LANES, 4 * LANES + (p + 1) * LANES)
        y_scr[:, ys] = (ob * g_ref[0, :, ys]).astype(BF16)

    d = x_ref.shape[-1]
    gate = m_ref[pl.ds(1 + b, 1), 2 * d:]
    xo_ref[0] = x_ref[0] + gate * _dot(y_scr[...], wout_ref[...])


def _sample_even_attn(rpb, x, m, qa, ka, va, qb, kvb, g, cckv, ckr, cnk, cnv, wkk, wkv, kg, wout):
    nb, s, _ = x.shape
    nq = s // NA_Q_BLOCK

    def blk(w):
        return pl.BlockSpec((1, NA_Q_BLOCK, w), lambda b, j: (b, j, 0))

    def per_batch(a):
        return pl.BlockSpec((1,) + a.shape[1:], lambda b, j: (b, 0, 0))

    return pl.pallas_call(
        _s0b_kernel,
        grid=(nb, nq),
        in_specs=[pl.BlockSpec(memory_space=pltpu.SMEM), blk(D_MODEL), _spec(m),
                  blk(1024), per_batch(ka), per_batch(va), blk(512), per_batch(kvb), blk(1024),
                  per_batch(cckv), per_batch(ckr), per_batch(cnk), per_batch(cnv),
                  _full(wkk.shape), _full(wkv.shape), _spec(kg), _full(wout.shape)],
        out_specs=blk(D_MODEL),
        out_shape=jax.ShapeDtypeStruct(x.shape, F32),
        scratch_shapes=[pltpu.VMEM((PAST_LEN, N_HEADS * LANES), BF16),
                        pltpu.VMEM((PAST_LEN, N_HEADS * HEAD_DIM), BF16),
                        pltpu.VMEM((RPB_ROWS + 1, GRID_W, LANES), F32),
                        pltpu.VMEM((2, N_HEADS, BIAS_CHUNKS, GRID_W, LANES), F32),
                        pltpu.VMEM((NA_Q_BLOCK, D_MODEL), BF16)],
        compiler_params=pltpu.CompilerParams(dimension_semantics=("arbitrary", "arbitrary"),
                                             vmem_limit_bytes=VMEM_LIMIT),
        name="sample_even_attn",
    )(rpb, x, _arr(m), qa, ka, va, qb, kvb, g, cckv, ckr, cnk, cnv, wkk, wkv, _arr(kg), wout)


def _s1a_kernel(x_ref, m_ref, ng_ref, win_ref, gqg_ref, gkg_ref, sqg_ref, skg_ref, cos_ref, sin_ref,
                qc_ref, kc_ref, vc_ref, qd_ref, kd_ref, vd_ref, g_ref):
    b = pl.program_id(0)
    lo = _lane_lo()
    partner = _rope_matrix2(HEAD_DIM, HEAD_DIM, 0)
    swap = _swap_matrix2()[:LANES]
    hb = _modulate(x_ref[0], ng_ref[...], m_ref[pl.ds(1 + b, 1), :])[0].astype(BF16)
    cos, sin = cos_ref[...], sin_ref[...]
    sc = HEAD_DIM ** -0.5 * LOG2E

    def rope(t):
        return t * cos + _lane_mix(t, partner) * sin

    branches = ((O_QC, O_KC, O_GC, gqg_ref, gkg_ref, qc_ref, kc_ref, vc_ref, 0),
                (O_QD, O_KD, O_GD, sqg_ref, skg_ref, qd_ref, kd_ref, vd_ref, 4 * LANES))
    for oq, ok, og, qg_ref, kg_ref, q_out, k_out, v_out, goff in branches:
        zq = _dot(hb, win_ref[:, oq:oq + 4 * LANES])
        zkv = _dot(hb, win_ref[:, ok:ok + 2 * LANES])
        qg = qg_ref[...] * sc
        for p in range(N_PAIRS):
            sl = slice(p * LANES, (p + 1) * LANES)
            q_out[0, :, sl] = rope(_rms_halves(zq[:, sl], qg, lo)).astype(BF16)
        kn = rope(_rms_halves(zkv[:, :LANES], kg_ref[...], lo))
        v = zkv[:, LANES:]
        for out, val in ((k_out, kn.astype(BF16)), (v_out, v.astype(BF16))):
            out[0, :, 0:LANES] = val
            out[0, :, LANES:2 * LANES] = _dot(val, swap).astype(BF16)
        g_ref[0, :, goff:goff + 4 * LANES] = _silu(_dot(hb, win_ref[:, og:og + 4 * LANES]))


def _sample_odd_proj(x, m, ng, win, gqg, gkg, sqg, skg, cos, sin):
    nb, s, _ = x.shape
    nq = s // ODD_PROJ_BLOCK
    ins = (m, ng, win, gqg, gkg, sqg, skg)
    tab = pl.BlockSpec((ODD_PROJ_BLOCK, LANES), lambda b, j: (j, 0))

    def blk(w):
        return pl.BlockSpec((1, ODD_PROJ_BLOCK, w), lambda b, j: (b, j, 0))

    def shp(w, dt):
        return jax.ShapeDtypeStruct((nb, s, w), dt)

    return pl.pallas_call(
        _s1a_kernel,
        grid=(nb, nq),
        in_specs=[blk(D_MODEL)] + [_spec(a) for a in ins] + [tab, tab],
        out_specs=[blk(512), blk(256), blk(256), blk(512), blk(256), blk(256), blk(1024)],
        out_shape=[shp(512, BF16), shp(256, BF16), shp(256, BF16), shp(512, BF16), shp(256, BF16),
                   shp(256, BF16), shp(1024, F32)],
        compiler_params=pltpu.CompilerParams(dimension_semantics=("arbitrary", "arbitrary"),
                                             vmem_limit_bytes=VMEM_LIMIT),
        name="sample_odd_proj",
    )(x, *map(_arr, ins), cos, sin)


def _s1b_kernel(sink_ref, x_ref, m_ref, qc_ref, kc_ref, vc_ref, qd_ref, kd_ref, vd_ref, g_ref,
                cgk_ref, cgv_ref, csk_ref, csv_ref, wout_ref, xo_ref, y_scr):
    b = pl.program_id(0)
    j = pl.program_id(1)
    lo = _lane_lo()
    n_lat = kc_ref.shape[1]
    win_keys = Q_BLOCK + 2 * SWA_HALF

    def ctx_pair(ref, values=False):
        a = ref[0].astype(BF16)
        pair = (a, _swap_halves(a))
        return tuple(_with_ones(t, transposed=True) for t in pair) if values else pair

    cgk, cgv, csk, csv = ctx_pair(cgk_ref), ctx_pair(cgv_ref, True), ctx_pair(csk_ref), ctx_pair(csv_ref, True)
    vcs = [_with_ones(vc_ref[0, :, w * LANES:(w + 1) * LANES]) for w in (0, 1)]

    ks = pl.multiple_of(jnp.clip(j * Q_BLOCK - SWA_HALF, 0, n_lat - win_keys), SWA_HALF)
    qpos = j * Q_BLOCK + lax.broadcasted_iota(jnp.int32, (Q_BLOCK, win_keys), 0)
    kpos = ks + lax.broadcasted_iota(jnp.int32, (Q_BLOCK, win_keys), 1)
    band = jnp.abs(qpos - kpos) <= SWA_HALF
    vds = [_with_ones(vd_ref[0, pl.ds(ks, win_keys), w * LANES:(w + 1) * LANES]) for w in (0, 1)]

    for p in range(N_PAIRS):
        sl = slice(p * LANES, (p + 1) * LANES)
        kv = p // 2
        qc = qc_ref[0, :, sl]
        qd = qd_ref[0, :, sl]
        oc2, od2 = [], []
        for half in (0, 1):
            swap = 0 if kv == half else 1
            ws = slice(swap * LANES, (swap + 1) * LANES)
            keep = lo if half == 0 else jnp.logical_not(lo)
            qm = jnp.where(keep, qc, jnp.zeros_like(qc))
            oc2.append(_attend([(_dot_nt(qm, kc_ref[0, :, ws]), vcs[swap], False),
                                (_dot(qm, cgk[swap]), cgv[swap], True)]))
            qm = jnp.where(keep, qd, jnp.zeros_like(qd))
            s_loc = jnp.where(band, _dot_nt(qm, kd_ref[0, pl.ds(ks, win_keys), ws]), NEG_INF)
            od2.append(_attend([(s_loc, vds[swap], False),
                                (_dot(qm, csk[swap]), csv[swap], True)], sink_ref[2 * p + half] * LOG2E))
        y_scr[:, sl] = (jnp.where(lo, oc2[0], oc2[1]) * g_ref[0, :, sl]).astype(BF16)
        ys = slice(4 * LANES + p * LANES, 4 * LANES + (p + 1) * LANES)
        y_scr[:, ys] = (jnp.where(lo, od2[0], od2[1]) * g_ref[0, :, ys]).astype(BF16)

    d = x_ref.shape[-1]
    gate = m_ref[pl.ds(1 + b, 1), 2 * d:]
    xo_ref[0] = x_ref[0] + gate * _dot(y_scr[...], wout_ref[...])


def _sample_odd_attn(sink, x, m, qc, kc, vc, qd, kd, vd, g, cgk, cgv, csk, csv, wout):
    nb, s, _ = x.shape
    nq = s // Q_BLOCK

    def blk(w):
        return pl.BlockSpec((1, Q_BLOCK, w), lambda b, j: (b, j, 0))

    def per_batch(a):
        return pl.BlockSpec((1,) + a.shape[1:], lambda b, j: (b, 0, 0))

    return pl.pallas_call(
        _s1b_kernel,
        grid=(nb, nq),
        in_specs=[pl.BlockSpec(memory_space=pltpu.SMEM), blk(D_MODEL), _spec(m),
                  blk(512), per_batch(kc), per_batch(vc), blk(512), per_batch(kd), per_batch(vd), blk(1024),
                  per_batch(cgk), per_batch(cgv), per_batch(csk), per_batch(csv), _full(wout.shape)],
        out_specs=blk(D_MODEL),
        out_shape=jax.ShapeDtypeStruct(x.shape, F32),
        scratch_shapes=[pltpu.VMEM((Q_BLOCK, D_MODEL), BF16)],
        compiler_params=pltpu.CompilerParams(dimension_semantics=("arbitrary", "arbitrary"),
                                             vmem_limit_bytes=VMEM_LIMIT),
        name="sample_odd_attn",
    )(sink, x, _arr(m), qc, kc, vc, qd, kd, vd, g, cgk, cgv, csk, csv, wout)


WEIGHT_PREP_STEPS = 8
EVEN_IN_CHUNKS = 6
COND_PREP_ROWS = 256
BF16_SUBLANES = 16


G_MLA_Q, G_MLA_K, G_NA_Q, G_NA_K, G_GQA_Q, G_GQA_K, G_SWA_Q, G_SWA_K, N_GAINS = range(9)


GAIN_WIDTHS = (QK_A, QK_A) + (HEAD_DIM,) * 6


def _cond_prep_kernel(n_cond, ct_ref, wm_ref, bm_ref, ng_ref, gains_ref, after_ref, mo_ref, gt_ref, ngt_ref):
    del after_ref
    _mod_step(n_cond, pl.program_id(0) == 0, bm_ref[0:1, :], ct_ref, wm_ref, mo_ref)
    gt_ref[...] = jnp.zeros(gt_ref.shape, F32)
    start = 0
    for r, w in enumerate(GAIN_WIDTHS):
        g = gains_ref[:, start:start + w]
        start += w
        for off in range(0, LANES - w + 1, w):
            gt_ref[r, :, off:off + w] = g
    for layer in range(ngt_ref.shape[0]):
        ngt_ref[layer] = ng_ref[layer:layer + 1, :]


def _cond_prep(cond_t, n_cond, w_mod, b_mod, norm_g, gains, after):
    assert len(gains) == N_GAINS and tuple(g.shape[-1] for g in gains) == GAIN_WIDTHS
    gains_row = jnp.concatenate([g.reshape(1, -1) for g in gains], axis=1)
    tk = COND_PREP_ROWS
    return pl.pallas_call(
        functools.partial(_cond_prep_kernel, n_cond),
        grid=(D_MODEL // tk,),
        in_specs=[pl.BlockSpec((tk, 8), lambda k: (k, 0)),
                  pl.BlockSpec((None, tk, 3 * D_MODEL), lambda k: (0, k, 0)),
                  _full(b_mod.shape), _full(norm_g.shape), _full(gains_row.shape), _full(after.shape)],
        out_specs=[_full((8, 3 * D_MODEL)), _full((N_GAINS, 1, LANES)), _full((norm_g.shape[0], 1, D_MODEL))],
        out_shape=[jax.ShapeDtypeStruct((8, 3 * D_MODEL), F32), jax.ShapeDtypeStruct((N_GAINS, 1, LANES), F32),
                   jax.ShapeDtypeStruct((norm_g.shape[0], 1, D_MODEL), F32)],
        compiler_params=pltpu.CompilerParams(dimension_semantics=("arbitrary",)),
        name="cond_prep",
    )(cond_t, w_mod, b_mod, norm_g, gains_row, after)


def _weight_prep_kernel(wie_ref, woe_ref, wqu_ref, wkv_ref, win_e_ref, wout_e_ref, wq_ref, wkk_ref, wkvv_ref):
    win_e_ref[...] = wie_ref[...].astype(BF16)
    wout_e_ref[...] = woe_ref[0].astype(BF16)

    wq_ref[...] = jnp.zeros(wq_ref.shape, BF16)
    for h in range(N_HEADS):
        wq_ref[:, h * LANES:h * LANES + QK_A] = wqu_ref[0, :, h * QK_A:(h + 1) * QK_A].astype(BF16)
    lo = _lane_lo()
    for p in range(N_PAIRS):
        a = wkv_ref[0, :, (2 * p) * LANES:(2 * p + 1) * LANES]
        c = wkv_ref[0, :, (2 * p + 1) * LANES:(2 * p + 2) * LANES]
        wkk_ref[:, (2 * p) * LANES:(2 * p + 1) * LANES] = jnp.where(lo, a, 0.0).astype(BF16)
        wkk_ref[:, (2 * p + 1) * LANES:(2 * p + 2) * LANES] = jnp.where(lo, c, 0.0).astype(BF16)
        wkvv_ref[:, p * LANES:(p + 1) * LANES] = jnp.where(lo, pltpu.roll(a, HEAD_DIM, 1), c).astype(BF16)


def _weight_prep(w_in_even_t, w_out_even, w_q_up, w_kv_up):
    n = WEIGHT_PREP_STEPS
    ins = (w_out_even, w_q_up, w_kv_up)
    out_cols = (D_MODEL, N_HEADS * LANES, N_HEADS * LANES, N_HEADS * HEAD_DIM)
    out_rows = (D_MODEL, Q_RANK, KV_RANK, KV_RANK)
    te = E_END // EVEN_IN_CHUNKS
    assert te * EVEN_IN_CHUNKS == E_END and te % BF16_SUBLANES == 0 and EVEN_IN_CHUNKS <= n
    even_spec = pl.BlockSpec((te, D_MODEL), lambda i: (jnp.minimum(i, EVEN_IN_CHUNKS - 1), 0))
    return pl.pallas_call(
        _weight_prep_kernel,
        grid=(n,),
        in_specs=[even_spec] + [pl.BlockSpec((1, a.shape[1] // n, a.shape[2]), lambda i: (0, i, 0)) for a in ins],
        out_specs=[even_spec] + [pl.BlockSpec((r // n, c), lambda i: (i, 0)) for r, c in zip(out_rows, out_cols)],
        out_shape=[jax.ShapeDtypeStruct((E_END, D_MODEL), BF16)]
        + [jax.ShapeDtypeStruct((r, c), BF16) for r, c in zip(out_rows, out_cols)],
        compiler_params=pltpu.CompilerParams(dimension_semantics=("arbitrary",), vmem_limit_bytes=VMEM_LIMIT),
        name="weight_prep",
    )(w_in_even_t, *ins)


def _feature_major(c):
    b, h, l, d = c.shape
    return jnp.swapaxes(c, -1, -2).reshape(b, h * d, l)


def _token_major(c):
    return jnp.swapaxes(c, -1, -2)


def _rope_tables(s, rot_dim, period, start):
    quarter = rot_dim // 4
    t = np.arange(s)
    inv = ROPE_THETA ** (-np.arange(quarter, dtype=np.float64) / quarter)
    row = (t // GRID_W).astype(np.float64)[:, None] * inv
    col = (t % GRID_W).astype(np.float64)[:, None] * inv
    ang = np.concatenate([row, col], axis=-1)
    cos, sin = np.cos(ang), np.sin(ang)
    pre = np.ones((s, start))
    post = np.zeros((s, period - start - rot_dim))
    c = np.concatenate([pre, cos, cos, post], axis=-1)
    sn = np.concatenate([0 * pre, sin, sin, post], axis=-1)
    rep = LANES // period
    return jnp.asarray(np.tile(c, (1, rep)), F32), jnp.asarray(np.tile(sn, (1, rep)), F32)


def kernel(x_prompt, x_sample, cache_mla_ckv, cache_mla_krope, cache_na_k, cache_na_v, cache_gqa_k, cache_gqa_v, cache_swa_k, cache_swa_v, c, c_ctx, norm_g, w_mod, b_mod, w_in_even, mla_qa_g, w_q_up, mla_kva_g, w_kv_up, mla_q_g, mla_k_g, na_q_g, na_k_g, na_rpb, w_out_even, w_in_odd, gqa_q_g, gqa_k_g, swa_q_g, swa_k_g, swa_sink, w_out_odd):
    n_dec = x_sample.shape[0]
    assert w_mod.shape[0] == 2 and n_dec + 1 <= 8

    cond_t = jnp.concatenate([c_ctx[:, None], c.T, jnp.zeros((D_MODEL, 7 - n_dec), F32)], axis=1)
    n_cond = 1 + n_dec
    gains = (mla_q_g, mla_k_g, na_q_g, na_k_g, gqa_q_g, gqa_k_g, swa_q_g, swa_k_g)
    win_e, wout_e, wq, wkk, wkv = _weight_prep(jnp.swapaxes(w_in_even[0], 0, 1), w_out_even, w_q_up, w_kv_up)
    m_even, gt, ngt = _cond_prep(cond_t, n_cond, w_mod, b_mod, norm_g, gains, after=wkv)
    even = (_Row(ngt, 0), win_e, mla_qa_g, wq, mla_kva_g, wkk, wkv,
            _Row(gt, G_MLA_Q), _Row(gt, G_MLA_K), _Row(gt, G_NA_Q), _Row(gt, G_NA_K))
    sink = swa_sink[0].astype(F32)

    xp1, new_ckv, new_krope, new_na_k, new_na_v, m_odd, win_o, wout_o = _prompt_even(
        x_prompt, m_even, *even, wout_e, cond_t, n_cond, w_mod, b_mod, w_in_odd, w_out_odd)
    odd = (_Row(ngt, 1), win_o, _Row(gt, G_GQA_Q), _Row(gt, G_GQA_K), _Row(gt, G_SWA_Q), _Row(gt, G_SWA_K))
    xp2, new_gqa_k, new_gqa_v, new_swa_k, new_swa_v = _prompt_odd(sink, xp1, m_odd, *odd, wout_o)

    cos_e, sin_e = _rope_tables(DEC_SEQ, ROPE_A, LANES, NOPE_A)
    qa, ka, va, qbs, kvbs, g0 = _sample_even_proj(x_sample, m_even, *even, cos_e, sin_e)
    ckr = jnp.swapaxes(cache_mla_krope[:, 0], -1, -2)
    xs1 = _sample_even_attn(na_rpb[0].reshape(-1), x_sample, m_even, qa, ka, va, qbs, kvbs, g0,
                            cache_mla_ckv[:, 0], ckr, _feature_major(cache_na_k[:, 0]),
                            _feature_major(cache_na_v[:, 0]), wkk, wkv, _Row(gt, G_MLA_K), wout_e)
    cos_o, sin_o = _rope_tables(DEC_SEQ, HEAD_DIM, HEAD_DIM, 0)
    qc, kc, vc, qd, kd, vd, g1 = _sample_odd_proj(xs1, m_odd, *odd, cos_o, sin_o)
    xs2 = _sample_odd_attn(sink, xs1, m_odd, qc, kc, vc, qd, kd, vd, g1,
                           _feature_major(cache_gqa_k[:, 0]), _feature_major(cache_gqa_v[:, 0]),
                           _feature_major(cache_swa_k[:, 0]), _feature_major(cache_swa_v[:, 0]), wout_o)

    caches = (new_krope, new_na_k, new_na_v, new_gqa_k, new_gqa_v, new_swa_k, new_swa_v)
    return (xp2, xs2, new_ckv) + tuple(_token_major(c) for c in caches)
```

```python
import functools
from typing import NamedTuple

import jax
import jax.numpy as jnp
import numpy as np
from jax import lax
from jax.experimental import pallas as pl
from jax.experimental.pallas import tpu as pltpu

F32 = jnp.float32
BF16 = jnp.bfloat16

D_MODEL = 1024
SEQ = 256
DEC_SEQ = 1024
PAST_LEN = 256
GRID_W = 64
HEAD_DIM = 64
Q_RANK = 256
KV_RANK = 128
NOPE_A = 64
ROPE_A = 32
QK_A = NOPE_A + ROPE_A
N_HEADS = 8
NA_ROWS = 8
NA_COLS = 16
SWA_HALF = 128
ROPE_THETA = 10000.0
EPS = 1e-6
NEG_INF = -1e30
LOG2E = 1.4426950408889634

LANES = 128
Q_BLOCK = 512
NA_Q_BLOCK = 256
PROJ_BLOCK = 512
PROMPT_BATCHES_PER_STEP = 2
N_PAIRS = N_HEADS // 2
RPB_ROWS = 2 * NA_ROWS - 1
RPB_COLS = 2 * NA_COLS - 1
BIAS_CHUNKS = 16
VMEM_LIMIT = 48 * 1024 * 1024

E_QLAT, E_CKV, E_KROPE, E_GA, E_QB, E_KB, E_VB, E_GB, E_END = 0, 256, 384, 416, 928, 1440, 1952, 2464, 2976
O_QC, O_KC, O_VC, O_GC, O_QD, O_KD, O_VD, O_GD, O_END = 0, 512, 640, 768, 1280, 1792, 1920, 2048, 2560


def _dot(a, b):
    return lax.dot_general(a, b, (((1,), (0,)), ((), ())), preferred_element_type=F32)


def _dot_nt(a, b):
    return lax.dot_general(a, b, (((1,), (1,)), ((), ())), preferred_element_type=F32)


def _silu(x):
    return x / (1.0 + jnp.exp(-x))


def _rms(x, g, n):
    ss = jnp.sum(x * x, axis=-1, keepdims=True)
    return x * lax.rsqrt(ss / n + EPS) * g


def _rms_halves(x, g2, lo):
    x2 = x * x
    s_lo = jnp.sum(jnp.where(lo, x2, 0.0), axis=-1, keepdims=True)
    s_hi = jnp.sum(jnp.where(lo, 0.0, x2), axis=-1, keepdims=True)
    r = jnp.where(lo, lax.rsqrt(s_lo / HEAD_DIM + EPS), lax.rsqrt(s_hi / HEAD_DIM + EPS))
    return x * r * g2


def _modulate(x, g, m):
    d = x.shape[-1]
    xn = x * lax.rsqrt(jnp.mean(x * x, axis=-1, keepdims=True) + EPS) * g
    return xn * (1.0 + m[:, d:2 * d]) + m[:, :d], m[:, 2 * d:]


def _split_lanes(x):
    hi = x.astype(BF16)
    lo = (x - hi.astype(F32)).astype(BF16)
    return jnp.concatenate([hi, lo], axis=1)


def _lane_matrix2(entries):
    i = lax.broadcasted_iota(jnp.int32, (LANES, LANES), 0)
    j = lax.broadcasted_iota(jnp.int32, (LANES, LANES), 1)
    m = entries(i, j).astype(BF16)
    return jnp.concatenate([m, m], axis=0)


def _rope_matrix2(rot_dim, period, start):
    half = rot_dim // 2

    def entries(i, j):
        pos = jnp.bitwise_and(j, period - 1) - start
        neg = (pos >= 0) & (pos < half) & (i == j + half)
        plus = (pos >= half) & (pos < rot_dim) & (i == j - half)
        return jnp.where(neg, -1.0, jnp.where(plus, 1.0, 0.0))

    return _lane_matrix2(entries)


def _swap_matrix2():
    return _lane_matrix2(lambda i, j: jnp.where(i == jnp.bitwise_xor(j, HEAD_DIM), 1.0, 0.0))


def _lane_mix(x, m2):
    return _dot(_split_lanes(x), m2)


def _with_ones(v, transposed=False):
    if transposed:
        return jnp.concatenate([v, jnp.ones((LANES, v.shape[1]), v.dtype)], axis=0)
    return jnp.concatenate([v, jnp.ones((v.shape[0], LANES), v.dtype)], axis=1)


def _attend(parts, sink=None):
    mx = None
    for s, _, _ in parts:
        pm = jnp.max(s, axis=-1, keepdims=True)
        mx = pm if mx is None else jnp.maximum(mx, pm)
    if sink is not None:
        mx = jnp.maximum(mx, sink)
    acc, den = None, None
    for s, v, v_t in parts:
        e = jnp.exp2(s - mx)
        po = (_dot_nt if v_t else _dot)(e.astype(BF16), v)
        acc = po if acc is None else acc + po
        if po.shape[1] == LANES:
            ps = jnp.sum(e, axis=-1, keepdims=True)
            den = ps if den is None else den + ps
    if den is None:
        den = acc[:, LANES:]
    if sink is not None:
        den = den + jnp.exp2(sink - mx)
    return acc[:, :LANES] * (1.0 / den)


def _lane_lo():
    return lax.broadcasted_iota(jnp.int32, (1, LANES), 1) < HEAD_DIM


def _store_pair_transposed(ref, bi, p, x):
    xt = x.T
    ref[bi, 0, 2 * p] = xt[:HEAD_DIM]
    ref[bi, 0, 2 * p + 1] = xt[HEAD_DIM:]


def _rope_key_slab(win_ref):
    d = win_ref.shape[1]
    return jnp.concatenate([jnp.zeros((NOPE_A, d), BF16), win_ref[E_KROPE:E_GA, :],
                            jnp.zeros((LANES - QK_A, d), BF16)], axis=0)


def _swap_halves(a):
    return jnp.concatenate([a[HEAD_DIM:], a[:HEAD_DIM]], axis=0)


def _mod_step(n_cond, is_first, bias_row, c_ref, w_ref, o_ref):
    @pl.when(is_first)
    def _():
        o_ref[:n_cond, :] = jnp.broadcast_to(bias_row, (n_cond, o_ref.shape[1]))
        o_ref[n_cond:, :] = jnp.zeros((o_ref.shape[0] - n_cond, o_ref.shape[1]), F32)

    s = _silu(c_ref[...])
    cols = [jnp.broadcast_to(s[:, r:r + 1], (s.shape[0], LANES)) for r in range(n_cond)]
    for t in range(w_ref.shape[1] // LANES):
        sl = slice(t * LANES, (t + 1) * LANES)
        w = w_ref[:, sl]
        for r in range(n_cond):
            o_ref[r:r + 1, sl] += jnp.sum(w * cols[r], axis=0, keepdims=True)


def _mla_keys(cb, kr, wkk_ref, wkv_ref, kg, rope=None):
    kk = _dot(cb, wkk_ref[...])
    keys = []
    for h in range(N_HEADS):
        k = _rms(kk[:, h * LANES:(h + 1) * LANES] + kr, kg, QK_A)
        if rope is not None:
            k = rope(k)
        keys.append(k.astype(BF16))
    return keys, _dot(cb, wkv_ref[...]).astype(BF16)


def _p0_kernel(n_cond, x_ref, m_ref, ng_ref, win_ref, qag_ref, wq_ref, kvag_ref, wkk_ref, wkv_ref, qg_ref, kg_ref,
               naqg_ref, nakg_ref, wout_ref, ct_ref, wm_ref, bm_ref, wio_ref, woo_ref,
               xo_ref, ckv_ref, krope_ref, nak_ref, nav_ref, mo_ref, wino_ref, wouto_ref, y_scr):
    _mod_step(n_cond, pl.program_id(0) == 0, bm_ref[1:2, :], ct_ref, wm_ref, mo_ref)
    wino_ref[...] = wio_ref[0].astype(BF16)
    wouto_ref[...] = woo_ref[0].astype(BF16)

    nbs = x_ref.shape[0]
    x = x_ref[...].reshape(nbs * SEQ, D_MODEL)
    h, gate = _modulate(x, ng_ref[...], m_ref[0:1, :])
    hb = h.astype(BF16)
    lo = _lane_lo()
    hi = jnp.logical_not(lo)
    rows = [slice(bi * SEQ, (bi + 1) * SEQ) for bi in range(nbs)]

    qln = _rms(_dot_nt(hb, win_ref[E_QLAT:E_CKV, :]), qag_ref[...], Q_RANK).astype(BF16)
    q_all = _dot(qln, wq_ref[...])
    ckv_n = _rms(_dot_nt(hb, win_ref[E_CKV:E_KROPE, :]), kvag_ref[...], KV_RANK)
    kr = _dot_nt(hb, _rope_key_slab(win_ref))
    for bi, rs in enumerate(rows):
        ckv_ref[bi, 0] = ckv_n[rs]
        krope_ref[bi, 0] = kr[rs].T[NOPE_A:QK_A]
    keys, vals = _mla_keys(ckv_n.astype(BF16), kr, wkk_ref, wkv_ref, kg_ref[...])
    qg = qg_ref[...] * (QK_A ** -0.5 * LOG2E)

    ga = _dot_nt(hb, win_ref[E_GA:E_QB, :])
    zq = _dot_nt(hb, win_ref[E_QB:E_KB, :])
    zk = _dot_nt(hb, win_ref[E_KB:E_VB, :])
    zv = _dot_nt(hb, win_ref[E_VB:E_GB, :])
    gb = _dot_nt(hb, win_ref[E_GB:E_END, :])
    naqg = naqg_ref[...] * (HEAD_DIM ** -0.5 * LOG2E)

    for p in range(N_PAIRS):
        sl = slice(p * LANES, (p + 1) * LANES)
        ys = slice(4 * LANES + p * LANES, 4 * LANES + (p + 1) * LANES)
        qhs = [_rms(q_all[:, hh * LANES:(hh + 1) * LANES], qg, QK_A).astype(BF16) for hh in (2 * p, 2 * p + 1)]
        qb = _rms_halves(zq[:, sl], naqg, lo)
        kb = _rms_halves(zk[:, sl], nakg_ref[...], lo)
        vb = zv[:, sl]
        kbb, vbb = kb.astype(BF16), vb.astype(BF16)
        va = vals[:, sl]
        qms = [jnp.where(keep, qb, 0.0).astype(BF16) for keep in (lo, hi)]
        for bi, rs in enumerate(rows):
            o2 = [_attend([(_dot_nt(qhs[i][rs], keys[2 * p + i][rs]), va[rs], False)]) for i in (0, 1)]
            y_scr[rs, sl] = (jnp.where(lo, o2[0], o2[1]) * _silu(ga[rs, sl])).astype(BF16)
            _store_pair_transposed(nak_ref, bi, p, kb[rs])
            _store_pair_transposed(nav_ref, bi, p, vb[rs])
            o2 = [_attend([(_dot_nt(qms[i][rs], kbb[rs]), vbb[rs], False)]) for i in (0, 1)]
            y_scr[rs, ys] = (jnp.where(lo, o2[0], o2[1]) * _silu(gb[rs, sl])).astype(BF16)

    xo_ref[...] = (x + gate * _dot(y_scr[...], wout_ref[...])).reshape(nbs, SEQ, D_MODEL)


def _full(shape):
    n = len(shape)
    return pl.BlockSpec(shape, lambda *_: (0,) * n, pipeline_mode=pl.Buffered(1))


class _Row(NamedTuple):
    table: jax.Array
    row: int


def _spec(a):
    if isinstance(a, _Row):
        idx = (a.row,) + (0,) * (a.table.ndim - 1)
        return pl.BlockSpec((None,) + a.table.shape[1:], lambda *_: idx, pipeline_mode=pl.Buffered(1))
    return _full(a.shape)


def _arr(a):
    return a.table if isinstance(a, _Row) else a


def _prompt_even(x, m, ng, win, qag, wq, kvag, wkk, wkv, qg, kg, naqg, nakg, wout,
                 cond_t, n_cond, w_mod, b_mod, w_in_odd, w_out_odd):
    nb = x.shape[0]
    nbs = PROMPT_BATCHES_PER_STEP
    steps = nb // nbs
    assert nb % nbs == 0 and D_MODEL % (BF16_SUBLANES * steps) == 0
    tr = D_MODEL // steps
    ins = (m, ng, win, qag, wq, kvag, wkk, wkv, qg, kg, naqg, nakg, wout)
    return pl.pallas_call(
        functools.partial(_p0_kernel, n_cond),
        grid=(steps,),
        in_specs=[pl.BlockSpec((nbs, SEQ, D_MODEL), lambda b: (b, 0, 0))] + [_spec(a) for a in ins]
        + [pl.BlockSpec((tr, 8), lambda b: (b, 0)),
           pl.BlockSpec((None, tr, 3 * D_MODEL), lambda b: (1, b, 0)),
           _full(b_mod.shape),
           pl.BlockSpec((1, tr, O_END), lambda b: (0, b, 0)),
           pl.BlockSpec((1, tr, D_MODEL), lambda b: (0, b, 0))],
        out_specs=[pl.BlockSpec((nbs, SEQ, D_MODEL), lambda b: (b, 0, 0)),
                   pl.BlockSpec((nbs, 1, SEQ, KV_RANK), lambda b: (b, 0, 0, 0)),
                   pl.BlockSpec((nbs, 1, ROPE_A, SEQ), lambda b: (b, 0, 0, 0)),
                   pl.BlockSpec((nbs, 1, N_HEADS, HEAD_DIM, SEQ), lambda b: (b, 0, 0, 0, 0)),
                   pl.BlockSpec((nbs, 1, N_HEADS, HEAD_DIM, SEQ), lambda b: (b, 0, 0, 0, 0)),
                   pl.BlockSpec((8, 3 * D_MODEL), lambda b: (0, 0)),
                   pl.BlockSpec((tr, O_END), lambda b: (b, 0)),
                   pl.BlockSpec((tr, D_MODEL), lambda b: (b, 0))],
        out_shape=[jax.ShapeDtypeStruct((nb, SEQ, D_MODEL), F32),
                   jax.ShapeDtypeStruct((nb, 1, SEQ, KV_RANK), F32),
                   jax.ShapeDtypeStruct((nb, 1, ROPE_A, SEQ), F32),
                   jax.ShapeDtypeStruct((nb, 1, N_HEADS, HEAD_DIM, SEQ), F32),
                   jax.ShapeDtypeStruct((nb, 1, N_HEADS, HEAD_DIM, SEQ), F32),
                   jax.ShapeDtypeStruct((8, 3 * D_MODEL), F32),
                   jax.ShapeDtypeStruct((D_MODEL, O_END), BF16),
                   jax.ShapeDtypeStruct((D_MODEL, D_MODEL), BF16)],
        scratch_shapes=[pltpu.VMEM((nbs * SEQ, D_MODEL), BF16)],
        compiler_params=pltpu.CompilerParams(dimension_semantics=("arbitrary",), vmem_limit_bytes=VMEM_LIMIT),
        name="prompt_even",
    )(x, *map(_arr, ins), cond_t, w_mod, b_mod, w_in_odd, w_out_odd)


def _gqa_pair_operands(k, v, kg2, lo):
    kn = _rms_halves(k, kg2, lo)
    return kn, (kn.astype(BF16), pltpu.roll(kn, HEAD_DIM, 1).astype(BF16)), \
        (_with_ones(v.astype(BF16)), _with_ones(pltpu.roll(v, HEAD_DIM, 1).astype(BF16)))


def _p1_kernel(sink_ref, x_ref, m_ref, ng_ref, win_ref, gqg_ref, gkg_ref, sqg_ref, skg_ref, wout_ref,
               xo_ref, gk_ref, gv_ref, sk_ref, sv_ref, y_scr):
    nbs = x_ref.shape[0]
    x = x_ref[...].reshape(nbs * SEQ, D_MODEL)
    h, gate = _modulate(x, ng_ref[...], m_ref[0:1, :])
    hb = h.astype(BF16)
    lo = _lane_lo()
    hi = jnp.logical_not(lo)
    sc = HEAD_DIM ** -0.5 * LOG2E
    rows = [slice(bi * SEQ, (bi + 1) * SEQ) for bi in range(nbs)]

    branches = ((O_QC, O_KC, O_VC, O_GC, gqg_ref, gkg_ref, gk_ref, gv_ref, False, 0),
                (O_QD, O_KD, O_VD, O_GD, sqg_ref, skg_ref, sk_ref, sv_ref, True, 4 * LANES))
    for oq, ok, ov, og, qg_ref, kg_ref, ck_ref, cv_ref, has_sink, yoff in branches:
        zq = _dot(hb, win_ref[:, oq:oq + 4 * LANES])
        zkv = _dot(hb, win_ref[:, ok:ok + 2 * LANES])
        zg = _dot(hb, win_ref[:, og:og + 4 * LANES])
        v = zkv[:, LANES:]
        kn, ks, vs = _gqa_pair_operands(zkv[:, :LANES], v, kg_ref[...], lo)
        for bi, rs in enumerate(rows):
            _store_pair_transposed(ck_ref, bi, 0, kn[rs])
            _store_pair_transposed(cv_ref, bi, 0, v[rs])
        qg = qg_ref[...] * sc
        for p in range(N_PAIRS):
            sl = slice(p * LANES, (p + 1) * LANES)
            qn = _rms_halves(zq[:, sl], qg, lo)
            qms = [jnp.where(keep, qn, 0.0).astype(BF16) for keep in (lo, hi)]
            kv = p // 2
            for bi, rs in enumerate(rows):
                o2 = []
                for half in (0, 1):
                    swap = 0 if kv == half else 1
                    sink = sink_ref[2 * p + half] * LOG2E if has_sink else None
                    o2.append(_attend([(_dot_nt(qms[half][rs], ks[swap][rs]), vs[swap][rs], False)], sink))
                o = jnp.where(lo, o2[0], o2[1])
                y_scr[rs, yoff + p * LANES:yoff + (p + 1) * LANES] = (o * _silu(zg[rs, sl])).astype(BF16)

    xo_ref[...] = (x + gate * _dot(y_scr[...], wout_ref[...])).reshape(nbs, SEQ, D_MODEL)


def _prompt_odd(sink, x, m, ng, win, gqg, gkg, sqg, skg, wout):
    nb = x.shape[0]
    nbs = PROMPT_BATCHES_PER_STEP
    assert nb % nbs == 0
    ins = (m, ng, win, gqg, gkg, sqg, skg, wout)
    cache_spec = pl.BlockSpec((nbs, 1, 2, HEAD_DIM, SEQ), lambda b: (b, 0, 0, 0, 0))
    cache_shape = jax.ShapeDtypeStruct((nb, 1, 2, HEAD_DIM, SEQ), F32)
    return pl.pallas_call(
        _p1_kernel,
        grid=(nb // nbs,),
        in_specs=[pl.BlockSpec(memory_space=pltpu.SMEM),
                  pl.BlockSpec((nbs, SEQ, D_MODEL), lambda b: (b, 0, 0))] + [_spec(a) for a in ins],
        out_specs=[pl.BlockSpec((nbs, SEQ, D_MODEL), lambda b: (b, 0, 0))] + [cache_spec] * 4,
        out_shape=[jax.ShapeDtypeStruct((nb, SEQ, D_MODEL), F32)] + [cache_shape] * 4,
        scratch_shapes=[pltpu.VMEM((nbs * SEQ, D_MODEL), BF16)],
        compiler_params=pltpu.CompilerParams(dimension_semantics=("arbitrary",), vmem_limit_bytes=VMEM_LIMIT),
        name="prompt_odd",
    )(sink, x, *map(_arr, ins))


def _s0a_kernel(x_ref, m_ref, ng_ref, win_ref, qag_ref, wq_ref, kvag_ref, wkk_ref, wkv_ref, qg_ref, kg_ref,
                naqg_ref, nakg_ref, cos_ref, sin_ref,
                qa_ref, ka_ref, va_ref, qb_ref, kb_ref, vb_ref, g_ref):
    b = pl.program_id(0)
    lo = _lane_lo()
    partner = _rope_matrix2(ROPE_A, LANES, NOPE_A)
    hb = _modulate(x_ref[0], ng_ref[...], m_ref[pl.ds(1 + b, 1), :])[0].astype(BF16)
    cos, sin = cos_ref[...], sin_ref[...]

    qln = _rms(_dot_nt(hb, win_ref[E_QLAT:E_CKV, :]), qag_ref[...], Q_RANK).astype(BF16)
    q_all = _dot(qln, wq_ref[...])
    ckv_n = _rms(_dot_nt(hb, win_ref[E_CKV:E_KROPE, :]), kvag_ref[...], KV_RANK)
    kr = _dot_nt(hb, _rope_key_slab(win_ref))
    cb = ckv_n.astype(BF16)
    kk = _dot(cb, wkk_ref[...])
    va_ref[0] = _dot(cb, wkv_ref[...]).astype(BF16)
    zq = _dot_nt(hb, win_ref[E_QB:E_KB, :])
    zk = _dot_nt(hb, win_ref[E_KB:E_VB, :])
    vb_ref[0] = _dot_nt(hb, win_ref[E_VB:E_GB, :]).astype(BF16)
    g_ref[0, :, 0:4 * LANES] = _silu(_dot_nt(hb, win_ref[E_GA:E_QB, :]))
    g_ref[0, :, 4 * LANES:8 * LANES] = _silu(_dot_nt(hb, win_ref[E_GB:E_END, :]))

    qg = qg_ref[...] * (QK_A ** -0.5 * LOG2E)
    kg = kg_ref[...]
    k_partner = _lane_mix(kr * kg, partner) * sin
    for hh in range(N_HEADS):
        sl = slice(hh * LANES, (hh + 1) * LANES)
        qn = _rms(q_all[:, sl], qg, QK_A)
        qa_ref[0, :, sl] = (qn * cos + _lane_mix(qn, partner) * sin).astype(BF16)
        k_raw = kk[:, sl] + kr
        k_inv = lax.rsqrt(jnp.sum(k_raw * k_raw, axis=-1, keepdims=True) / QK_A + EPS)
        ka_ref[0, :, sl] = ((k_raw * kg * cos + k_partner) * k_inv).astype(BF16)
    naqg = naqg_ref[...] * (HEAD_DIM ** -0.5 * LOG2E)
    for p in range(N_PAIRS):
        sl = slice(p * LANES, (p + 1) * LANES)
        qb_ref[0, :, sl] = _rms_halves(zq[:, sl], naqg, lo).astype(BF16)
        kb_ref[0, :, sl] = _rms_halves(zk[:, sl], nakg_ref[...], lo).astype(BF16)


def _sample_even_proj(x, m, ng, win, qag, wq, kvag, wkk, wkv, qg, kg, naqg, nakg, cos, sin):
    nb, s, _ = x.shape
    nq = s // PROJ_BLOCK
    ins = (m, ng, win, qag, wq, kvag, wkk, wkv, qg, kg, naqg, nakg)
    tab = pl.BlockSpec((PROJ_BLOCK, LANES), lambda b, j: (j, 0))

    def blk(w):
        return pl.BlockSpec((1, PROJ_BLOCK, w), lambda b, j: (b, j, 0))

    def shp(w, dt):
        return jax.ShapeDtypeStruct((nb, s, w), dt)

    return pl.pallas_call(
        _s0a_kernel,
        grid=(nb, nq),
        in_specs=[blk(D_MODEL)] + [_spec(a) for a in ins] + [tab, tab],
        out_specs=[blk(1024), blk(1024), blk(512), blk(512), blk(512), blk(512), blk(1024)],
        out_shape=[shp(1024, BF16), shp(1024, BF16), shp(512, BF16), shp(512, BF16), shp(512, BF16),
                   shp(512, BF16), shp(1024, F32)],
        compiler_params=pltpu.CompilerParams(dimension_semantics=("arbitrary", "arbitrary"),
                                             vmem_limit_bytes=VMEM_LIMIT),
        name="sample_even_proj",
    )(x, *map(_arr, ins), cos, sin)


def _build_bias_table(rpb_ref, tile_scr, tab_ref):
    qc = lax.broadcasted_iota(jnp.int32, (GRID_W, LANES), 0)
    lane = lax.broadcasted_iota(jnp.int32, (GRID_W, LANES), 1)
    kc = jnp.bitwise_and(lane, GRID_W - 1)
    lo = lane < GRID_W
    diff = kc - qc + (NA_COLS - 1)
    cs = jnp.clip(qc - NA_COLS // 2, 0, GRID_W - NA_COLS)
    valid = (kc >= cs) & (kc < cs + NA_COLS)
    tab_ref[...] = jnp.zeros(tab_ref.shape, F32)
    tile_scr[RPB_ROWS] = jnp.zeros((GRID_W, LANES), F32)

    def per_head(h, carry):
        for dr in range(RPB_ROWS):
            t = jnp.zeros((GRID_W, LANES), F32)
            for dc in range(RPB_COLS):
                t = jnp.where(diff == dc, rpb_ref[(h * RPB_ROWS + dr) * RPB_COLS + dc], t)
            tile_scr[dr] = jnp.where(valid, t * LOG2E, NEG_INF)
        for c in range(NA_ROWS // 2, NA_ROWS // 2 + NA_ROWS):
            d0 = 2 * c - NA_ROWS
            tab_ref[0, h, c] = jnp.where(lo, tile_scr[d0], tile_scr[d0 + 1])
            tab_ref[1, h, c] = jnp.where(lo, tile_scr[d0 - 1 if d0 > 0 else RPB_ROWS], tile_scr[d0])
        return carry

    lax.fori_loop(0, N_HEADS, per_head, 0)


def _s0b_kernel(rpb_ref, x_ref, m_ref, qa_ref, ka_ref, va_ref, qb_ref, kb_ref, vb_ref, g_ref,
                cckv_ref, ckr_ref, cnk_ref, cnv_ref, wkk_ref, wkv_ref, kg_ref, wout_ref,
                xo_ref, kca_scr, vca_scr, tile_scr, tab_scr, y_scr):
    b = pl.program_id(0)
    j = pl.program_id(1)
    lo = _lane_lo()
    n_lat = ka_ref.shape[1]

    @pl.when((b == 0) & (j == 0))
    def _():
        _build_bias_table(rpb_ref, tile_scr, tab_scr)

    @pl.when(j == 0)
    def _():
        kr_t = jnp.concatenate([jnp.zeros((NOPE_A, PAST_LEN), F32), ckr_ref[0],
                                jnp.zeros((LANES - QK_A, PAST_LEN), F32)], axis=0)
        keys, vals = _mla_keys(cckv_ref[0].astype(BF16), kr_t.T, wkk_ref, wkv_ref, kg_ref[...])
        for hh in range(N_HEADS):
            kca_scr[:, hh * LANES:(hh + 1) * LANES] = keys[hh]
        vca_scr[...] = vals

    kidx = lax.broadcasted_iota(jnp.int32, (1, n_lat), 1)
    for p in range(N_PAIRS):
        sl = slice(p * LANES, (p + 1) * LANES)
        o2 = []
        va = _with_ones(va_ref[0, :, sl])
        vca = _with_ones(vca_scr[:, sl])
        for hh in (2 * p, 2 * p + 1):
            hs = slice(hh * LANES, (hh + 1) * LANES)
            q = qa_ref[0, :, hs]
            o2.append(_attend([(_dot_nt(q, ka_ref[0, :, hs]), va, False),
                               (_dot_nt(q, kca_scr[:, hs]), vca, False)]))
        oa = jnp.where(lo, o2[0], o2[1])
        y_scr[:, sl] = (oa * g_ref[0, :, sl]).astype(BF16)

        qb = qb_ref[0, :, sl]
        kb = kb_ref[0, :, sl]
        vb = _with_ones(vb_ref[0, :, sl])
        kcb = cnk_ref[0, sl, :].astype(BF16)
        vcb = _with_ones(cnv_ref[0, sl, :].astype(BF16), transposed=True)
        o2 = []
        for half in (0, 1):
            head = 2 * p + half
            qm = jnp.where(lo if half == 0 else jnp.logical_not(lo), qb, jnp.zeros_like(qb))
            s_lat = _dot_nt(qm, kb)
            rows = []
            for local in range(NA_Q_BLOCK // GRID_W):
                qr = j * (NA_Q_BLOCK // GRID_W) + local
                par = 0 if local % 2 == 1 else 1
                c0 = (RPB_ROWS + par - local) // 2 - (NA_Q_BLOCK // GRID_W // 2) * j
                bias = jnp.concatenate([tab_scr[par, head, c0 + t] for t in range(n_lat // LANES)], axis=1)
                r0 = jnp.clip(qr - NA_ROWS // 2, 0, n_lat // GRID_W - NA_ROWS) * GRID_W
                ok = (kidx >= r0) & (kidx < r0 + NA_ROWS * GRID_W)
                bias = bias + jnp.where(ok, 0.0, NEG_INF)
                rows.append(s_lat[local * GRID_W:(local + 1) * GRID_W] + bias)
            s_lat = jnp.concatenate(rows, axis=0)
            o2.append(_attend([(s_lat, vb, False), (_dot(qm, kcb), vcb, True)]))
        ob = jnp.where(lo, o2[0], o2[1])
        ys = slice(4 * LANES + p * LANES, 4 * LANES + (p + 1) * LANES)
        y_scr[:, ys] = (ob * g_ref[0, :, ys]).astype(BF16)

    d = x_ref.shape[-1]
    gate = m_ref[pl.ds(1 + b, 1), 2 * d:]
    xo_ref[0] = x_ref[0] + gate * _dot(y_scr[...], wout_ref[...])


def _sample_even_attn(rpb, x, m, qa, ka, va, qb, kb, vb, g, cckv, ckr, cnk, cnv, wkk, wkv, kg, wout):
    nb, s, _ = x.shape
    nq = s // NA_Q_BLOCK

    def blk(w):
        return pl.BlockSpec((1, NA_Q_BLOCK, w), lambda b, j: (b, j, 0))

    def per_batch(a):
        return pl.BlockSpec((1,) + a.shape[1:], lambda b, j: (b, 0, 0))

    return pl.pallas_call(
        _s0b_kernel,
        grid=(nb, nq),
        in_specs=[pl.BlockSpec(memory_space=pltpu.SMEM), blk(D_MODEL), _spec(m),
                  blk(1024), per_batch(ka), per_batch(va), blk(512), per_batch(kb), per_batch(vb), blk(1024),
                  per_batch(cckv), per_batch(ckr), per_batch(cnk), per_batch(cnv),
                  _full(wkk.shape), _full(wkv.shape), _spec(kg), _full(wout.shape)],
        out_specs=blk(D_MODEL),
        out_shape=jax.ShapeDtypeStruct(x.shape, F32),
        scratch_shapes=[pltpu.VMEM((PAST_LEN, N_HEADS * LANES), BF16),
                        pltpu.VMEM((PAST_LEN, N_HEADS * HEAD_DIM), BF16),
                        pltpu.VMEM((RPB_ROWS + 1, GRID_W, LANES), F32),
                        pltpu.VMEM((2, N_HEADS, BIAS_CHUNKS, GRID_W, LANES), F32),
                        pltpu.VMEM((NA_Q_BLOCK, D_MODEL), BF16)],
        compiler_params=pltpu.CompilerParams(dimension_semantics=("arbitrary", "arbitrary"),
                                             vmem_limit_bytes=VMEM_LIMIT),
        name="sample_even_attn",
    )(rpb, x, _arr(m), qa, ka, va, qb, kb, vb, g, cckv, ckr, cnk, cnv, wkk, wkv, _arr(kg), wout)


def _s1a_kernel(x_ref, m_ref, ng_ref, win_ref, gqg_ref, gkg_ref, sqg_ref, skg_ref, cos_ref, sin_ref,
                qc_ref, kc_ref, vc_ref, qd_ref, kd_ref, vd_ref, g_ref):
    b = pl.program_id(0)
    lo = _lane_lo()
    partner = _rope_matrix2(HEAD_DIM, HEAD_DIM, 0)
    swap = _swap_matrix2()[:LANES]
    hb = _modulate(x_ref[0], ng_ref[...], m_ref[pl.ds(1 + b, 1), :])[0].astype(BF16)
    cos, sin = cos_ref[...], sin_ref[...]
    sc = HEAD_DIM ** -0.5 * LOG2E

    def rope(t):
        return t * cos + _lane_mix(t, partner) * sin

    branches = ((O_QC, O_KC, O_GC, gqg_ref, gkg_ref, qc_ref, kc_ref, vc_ref, 0),
                (O_QD, O_KD, O_GD, sqg_ref, skg_ref, qd_ref, kd_ref, vd_ref, 4 * LANES))
    for oq, ok, og, qg_ref, kg_ref, q_out, k_out, v_out, goff in branches:
        zq = _dot(hb, win_ref[:, oq:oq + 4 * LANES])
        zkv = _dot(hb, win_ref[:, ok:ok + 2 * LANES])
        qg = qg_ref[...] * sc
        for p in range(N_PAIRS):
            sl = slice(p * LANES, (p + 1) * LANES)
            q_out[0, :, sl] = rope(_rms_halves(zq[:, sl], qg, lo)).astype(BF16)
        kn = rope(_rms_halves(zkv[:, :LANES], kg_ref[...], lo))
        v = zkv[:, LANES:]
        for out, val in ((k_out, kn.astype(BF16)), (v_out, v.astype(BF16))):
            out[0, :, 0:LANES] = val
            out[0, :, LANES:2 * LANES] = _dot(val, swap).astype(BF16)
        g_ref[0, :, goff:goff + 4 * LANES] = _silu(_dot(hb, win_ref[:, og:og + 4 * LANES]))


def _sample_odd_proj(x, m, ng, win, gqg, gkg, sqg, skg, cos, sin):
    nb, s, _ = x.shape
    nq = s // PROJ_BLOCK
    ins = (m, ng, win, gqg, gkg, sqg, skg)
    tab = pl.BlockSpec((PROJ_BLOCK, LANES), lambda b, j: (j, 0))

    def blk(w):
        return pl.BlockSpec((1, PROJ_BLOCK, w), lambda b, j: (b, j, 0))

    def shp(w, dt):
        return jax.ShapeDtypeStruct((nb, s, w), dt)

    return pl.pallas_call(
        _s1a_kernel,
        grid=(nb, nq),
        in_specs=[blk(D_MODEL)] + [_spec(a) for a in ins] + [tab, tab],
        out_specs=[blk(512), blk(256), blk(256), blk(512), blk(256), blk(256), blk(1024)],
        out_shape=[shp(512, BF16), shp(256, BF16), shp(256, BF16), shp(512, BF16), shp(256, BF16),
                   shp(256, BF16), shp(1024, F32)],
        compiler_params=pltpu.CompilerParams(dimension_semantics=("arbitrary", "arbitrary"),
                                             vmem_limit_bytes=VMEM_LIMIT),
        name="sample_odd_proj",
    )(x, *map(_arr, ins), cos, sin)


def _s1b_kernel(sink_ref, x_ref, m_ref, qc_ref, kc_ref, vc_ref, qd_ref, kd_ref, vd_ref, g_ref,
                cgk_ref, cgv_ref, csk_ref, csv_ref, wout_ref, xo_ref, y_scr):
    b = pl.program_id(0)
    j = pl.program_id(1)
    lo = _lane_lo()
    n_lat = kc_ref.shape[1]
    win_keys = Q_BLOCK + 2 * SWA_HALF

    def ctx_pair(ref, values=False):
        a = ref[0].astype(BF16)
        pair = (a, _swap_halves(a))
        return tuple(_with_ones(t, transposed=True) for t in pair) if values else pair

    cgk, cgv, csk, csv = ctx_pair(cgk_ref), ctx_pair(cgv_ref, True), ctx_pair(csk_ref), ctx_pair(csv_ref, True)
    vcs = [_with_ones(vc_ref[0, :, w * LANES:(w + 1) * LANES]) for w in (0, 1)]

    ks = pl.multiple_of(jnp.clip(j * Q_BLOCK - SWA_HALF, 0, n_lat - win_keys), SWA_HALF)
    qpos = j * Q_BLOCK + lax.broadcasted_iota(jnp.int32, (Q_BLOCK, win_keys), 0)
    kpos = ks + lax.broadcasted_iota(jnp.int32, (Q_BLOCK, win_keys), 1)
    band = jnp.abs(qpos - kpos) <= SWA_HALF
    vds = [_with_ones(vd_ref[0, pl.ds(ks, win_keys), w * LANES:(w + 1) * LANES]) for w in (0, 1)]

    for p in range(N_PAIRS):
        sl = slice(p * LANES, (p + 1) * LANES)
        kv = p // 2
        qc = qc_ref[0, :, sl]
        qd = qd_ref[0, :, sl]
        oc2, od2 = [], []
        for half in (0, 1):
            swap = 0 if kv == half else 1
            ws = slice(swap * LANES, (swap + 1) * LANES)
            keep = lo if half == 0 else jnp.logical_not(lo)
            qm = jnp.where(keep, qc, jnp.zeros_like(qc))
            oc2.append(_attend([(_dot_nt(qm, kc_ref[0, :, ws]), vcs[swap], False),
                                (_dot(qm, cgk[swap]), cgv[swap], True)]))
            qm = jnp.where(keep, qd, jnp.zeros_like(qd))
            s_loc = jnp.where(band, _dot_nt(qm, kd_ref[0, pl.ds(ks, win_keys), ws]), NEG_INF)
            od2.append(_attend([(s_loc, vds[swap], False),
                                (_dot(qm, csk[swap]), csv[swap], True)], sink_ref[2 * p + half] * LOG2E))
        y_scr[:, sl] = (jnp.where(lo, oc2[0], oc2[1]) * g_ref[0, :, sl]).astype(BF16)
        ys = slice(4 * LANES + p * LANES, 4 * LANES + (p + 1) * LANES)
        y_scr[:, ys] = (jnp.where(lo, od2[0], od2[1]) * g_ref[0, :, ys]).astype(BF16)

    d = x_ref.shape[-1]
    gate = m_ref[pl.ds(1 + b, 1), 2 * d:]
    xo_ref[0] = x_ref[0] + gate * _dot(y_scr[...], wout_ref[...])


def _sample_odd_attn(sink, x, m, qc, kc, vc, qd, kd, vd, g, cgk, cgv, csk, csv, wout):
    nb, s, _ = x.shape
    nq = s // Q_BLOCK

    def blk(w):
        return pl.BlockSpec((1, Q_BLOCK, w), lambda b, j: (b, j, 0))

    def per_batch(a):
        return pl.BlockSpec((1,) + a.shape[1:], lambda b, j: (b, 0, 0))

    return pl.pallas_call(
        _s1b_kernel,
        grid=(nb, nq),
        in_specs=[pl.BlockSpec(memory_space=pltpu.SMEM), blk(D_MODEL), _spec(m),
                  blk(512), per_batch(kc), per_batch(vc), blk(512), per_batch(kd), per_batch(vd), blk(1024),
                  per_batch(cgk), per_batch(cgv), per_batch(csk), per_batch(csv), _full(wout.shape)],
        out_specs=blk(D_MODEL),
        out_shape=jax.ShapeDtypeStruct(x.shape, F32),
        scratch_shapes=[pltpu.VMEM((Q_BLOCK, D_MODEL), BF16)],
        compiler_params=pltpu.CompilerParams(dimension_semantics=("arbitrary", "arbitrary"),
                                             vmem_limit_bytes=VMEM_LIMIT),
        name="sample_odd_attn",
    )(sink, x, _arr(m), qc, kc, vc, qd, kd, vd, g, cgk, cgv, csk, csv, wout)


WEIGHT_PREP_STEPS = 8
EVEN_IN_CHUNKS = 6
COND_PREP_ROWS = 256
BF16_SUBLANES = 16


G_MLA_Q, G_MLA_K, G_NA_Q, G_NA_K, G_GQA_Q, G_GQA_K, G_SWA_Q, G_SWA_K, N_GAINS = range(9)


GAIN_WIDTHS = (QK_A, QK_A) + (HEAD_DIM,) * 6


def _cond_prep_kernel(n_cond, ct_ref, wm_ref, bm_ref, ng_ref, gains_ref, mo_ref, gt_ref, ngt_ref):
    _mod_step(n_cond, pl.program_id(0) == 0, bm_ref[0:1, :], ct_ref, wm_ref, mo_ref)
    gt_ref[...] = jnp.zeros(gt_ref.shape, F32)
    start = 0
    for r, w in enumerate(GAIN_WIDTHS):
        g = gains_ref[:, start:start + w]
        start += w
        for off in range(0, LANES - w + 1, w):
            gt_ref[r, :, off:off + w] = g
    for layer in range(ngt_ref.shape[0]):
        ngt_ref[layer] = ng_ref[layer:layer + 1, :]


def _cond_prep(cond_t, n_cond, w_mod, b_mod, norm_g, gains):
    assert len(gains) == N_GAINS and tuple(g.shape[-1] for g in gains) == GAIN_WIDTHS
    gains_row = jnp.concatenate([g.reshape(1, -1) for g in gains], axis=1)
    tk = COND_PREP_ROWS
    return pl.pallas_call(
        functools.partial(_cond_prep_kernel, n_cond),
        grid=(D_MODEL // tk,),
        in_specs=[pl.BlockSpec((tk, 8), lambda k: (k, 0)),
                  pl.BlockSpec((None, tk, 3 * D_MODEL), lambda k: (0, k, 0)),
                  _full(b_mod.shape), _full(norm_g.shape), _full(gains_row.shape)],
        out_specs=[_full((8, 3 * D_MODEL)), _full((N_GAINS, 1, LANES)), _full((norm_g.shape[0], 1, D_MODEL))],
        out_shape=[jax.ShapeDtypeStruct((8, 3 * D_MODEL), F32), jax.ShapeDtypeStruct((N_GAINS, 1, LANES), F32),
                   jax.ShapeDtypeStruct((norm_g.shape[0], 1, D_MODEL), F32)],
        compiler_params=pltpu.CompilerParams(dimension_semantics=("arbitrary",)),
        name="cond_prep",
    )(cond_t, w_mod, b_mod, norm_g, gains_row)


def _weight_prep_kernel(wie_ref, woe_ref, wqu_ref, wkv_ref, win_e_ref, wout_e_ref, wq_ref, wkk_ref, wkvv_ref):
    win_e_ref[...] = wie_ref[...].astype(BF16)
    wout_e_ref[...] = woe_ref[0].astype(BF16)

    wq_ref[...] = jnp.zeros(wq_ref.shape, BF16)
    for h in range(N_HEADS):
        wq_ref[:, h * LANES:h * LANES + QK_A] = wqu_ref[0, :, h * QK_A:(h + 1) * QK_A].astype(BF16)
    lo = _lane_lo()
    for p in range(N_PAIRS):
        a = wkv_ref[0, :, (2 * p) * LANES:(2 * p + 1) * LANES]
        c = wkv_ref[0, :, (2 * p + 1) * LANES:(2 * p + 2) * LANES]
        wkk_ref[:, (2 * p) * LANES:(2 * p + 1) * LANES] = jnp.where(lo, a, 0.0).astype(BF16)
        wkk_ref[:, (2 * p + 1) * LANES:(2 * p + 2) * LANES] = jnp.where(lo, c, 0.0).astype(BF16)
        wkvv_ref[:, p * LANES:(p + 1) * LANES] = jnp.where(lo, pltpu.roll(a, HEAD_DIM, 1), c).astype(BF16)


def _weight_prep(w_in_even_t, w_out_even, w_q_up, w_kv_up):
    n = WEIGHT_PREP_STEPS
    ins = (w_out_even, w_q_up, w_kv_up)
    out_cols = (D_MODEL, N_HEADS * LANES, N_HEADS * LANES, N_HEADS * HEAD_DIM)
    out_rows = (D_MODEL, Q_RANK, KV_RANK, KV_RANK)
    te = E_END // EVEN_IN_CHUNKS
    assert te * EVEN_IN_CHUNKS == E_END and te % BF16_SUBLANES == 0 and EVEN_IN_CHUNKS <= n
    even_spec = pl.BlockSpec((te, D_MODEL), lambda i: (jnp.minimum(i, EVEN_IN_CHUNKS - 1), 0))
    return pl.pallas_call(
        _weight_prep_kernel,
        grid=(n,),
        in_specs=[even_spec] + [pl.BlockSpec((1, a.shape[1] // n, a.shape[2]), lambda i: (0, i, 0)) for a in ins],
        out_specs=[even_spec] + [pl.BlockSpec((r // n, c), lambda i: (i, 0)) for r, c in zip(out_rows, out_cols)],
        out_shape=[jax.ShapeDtypeStruct((E_END, D_MODEL), BF16)]
        + [jax.ShapeDtypeStruct((r, c), BF16) for r, c in zip(out_rows, out_cols)],
        compiler_params=pltpu.CompilerParams(dimension_semantics=("arbitrary",), vmem_limit_bytes=VMEM_LIMIT),
        name="weight_prep",
    )(w_in_even_t, *ins)


def _feature_major(c):
    b, h, l, d = c.shape
    return jnp.swapaxes(c, -1, -2).reshape(b, h * d, l)


def _token_major(c):
    return jnp.swapaxes(c, -1, -2)


def _rope_tables(s, rot_dim, period, start):
    quarter = rot_dim // 4
    t = np.arange(s)
    inv = ROPE_THETA ** (-np.arange(quarter, dtype=np.float64) / quarter)
    row = (t // GRID_W).astype(np.float64)[:, None] * inv
    col = (t % GRID_W).astype(np.float64)[:, None] * inv
    ang = np.concatenate([row, col], axis=-1)
    cos, sin = np.cos(ang), np.sin(ang)
    pre = np.ones((s, start))
    post = np.zeros((s, period - start - rot_dim))
    c = np.concatenate([pre, cos, cos, post], axis=-1)
    sn = np.concatenate([0 * pre, sin, sin, post], axis=-1)
    rep = LANES // period
    return jnp.asarray(np.tile(c, (1, rep)), F32), jnp.asarray(np.tile(sn, (1, rep)), F32)


def kernel(x_prompt, x_sample, cache_mla_ckv, cache_mla_krope, cache_na_k, cache_na_v, cache_gqa_k, cache_gqa_v, cache_swa_k, cache_swa_v, c, c_ctx, norm_g, w_mod, b_mod, w_in_even, mla_qa_g, w_q_up, mla_kva_g, w_kv_up, mla_q_g, mla_k_g, na_q_g, na_k_g, na_rpb, w_out_even, w_in_odd, gqa_q_g, gqa_k_g, swa_q_g, swa_k_g, swa_sink, w_out_odd):
    n_dec = x_sample.shape[0]
    assert w_mod.shape[0] == 2 and n_dec + 1 <= 8

    cond_t = jnp.concatenate([c_ctx[:, None], c.T, jnp.zeros((D_MODEL, 7 - n_dec), F32)], axis=1)
    n_cond = 1 + n_dec
    gains = (mla_q_g, mla_k_g, na_q_g, na_k_g, gqa_q_g, gqa_k_g, swa_q_g, swa_k_g)
    win_e, wout_e, wq, wkk, wkv = _weight_prep(jnp.swapaxes(w_in_even[0], 0, 1), w_out_even, w_q_up, w_kv_up)
    m_even, gt, ngt = _cond_prep(cond_t, n_cond, w_mod, b_mod, norm_g, gains)
    even = (_Row(ngt, 0), win_e, mla_qa_g, wq, mla_kva_g, wkk, wkv,
            _Row(gt, G_MLA_Q), _Row(gt, G_MLA_K), _Row(gt, G_NA_Q), _Row(gt, G_NA_K))
    sink = swa_sink[0].astype(F32)

    xp1, new_ckv, new_krope, new_na_k, new_na_v, m_odd, win_o, wout_o = _prompt_even(
        x_prompt, m_even, *even, wout_e, cond_t, n_cond, w_mod, b_mod, w_in_odd, w_out_odd)
    odd = (_Row(ngt, 1), win_o, _Row(gt, G_GQA_Q), _Row(gt, G_GQA_K), _Row(gt, G_SWA_Q), _Row(gt, G_SWA_K))
    xp2, new_gqa_k, new_gqa_v, new_swa_k, new_swa_v = _prompt_odd(sink, xp1, m_odd, *odd, wout_o)

    cos_e, sin_e = _rope_tables(DEC_SEQ, ROPE_A, LANES, NOPE_A)
    qa, ka, va, qbs, kbs, vbs, g0 = _sample_even_proj(x_sample, m_even, *even, cos_e, sin_e)
    ckr = jnp.swapaxes(cache_mla_krope[:, 0], -1, -2)
    xs1 = _sample_even_attn(na_rpb[0].reshape(-1), x_sample, m_even, qa, ka, va, qbs, kbs, vbs, g0,
                            cache_mla_ckv[:, 0], ckr, _feature_major(cache_na_k[:, 0]),
                            _feature_major(cache_na_v[:, 0]), wkk, wkv, _Row(gt, G_MLA_K), wout_e)
    cos_o, sin_o = _rope_tables(DEC_SEQ, HEAD_DIM, HEAD_DIM, 0)
    qc, kc, vc, qd, kd, vd, g1 = _sample_odd_proj(xs1, m_odd, *odd, cos_o, sin_o)
    xs2 = _sample_odd_attn(sink, xs1, m_odd, qc, kc, vc, qd, kd, vd, g1,
                           _feature_major(cache_gqa_k[:, 0]), _feature_major(cache_gqa_v[:, 0]),
                           _feature_major(cache_swa_k[:, 0]), _feature_major(cache_swa_v[:, 0]), wout_o)

    caches = (new_krope, new_na_k, new_na_v, new_gqa_k, new_gqa_v, new_swa_k, new_swa_v)
    return (xp2, xs2, new_ckv) + tuple(_token_major(c) for c in caches)
```

```python
import functools
from typing import NamedTuple

import jax
import jax.numpy as jnp
import numpy as np
from jax import lax
from jax.experimental import pallas as pl
from jax.experimental.pallas import tpu as pltpu

F32 = jnp.float32
BF16 = jnp.bfloat16

D_MODEL = 1024
SEQ = 256
DEC_SEQ = 1024
PAST_LEN = 256
GRID_W = 64
HEAD_DIM = 64
Q_RANK = 256
KV_RANK = 128
NOPE_A = 64
ROPE_A = 32
QK_A = NOPE_A + ROPE_A
N_HEADS = 8
NA_ROWS = 8
NA_COLS = 16
SWA_HALF = 128
ROPE_THETA = 10000.0
EPS = 1e-6
NEG_INF = -1e30
LOG2E = 1.4426950408889634

LANES = 128
Q_BLOCK = 512
NA_Q_BLOCK = 256
PROJ_BLOCK = 512
PROMPT_BATCHES_PER_STEP = 2
N_PAIRS = N_HEADS // 2
RPB_ROWS = 2 * NA_ROWS - 1
RPB_COLS = 2 * NA_COLS - 1
BIAS_CHUNKS = 16
VMEM_LIMIT = 48 * 1024 * 1024

E_QLAT, E_CKV, E_KROPE, E_GA, E_QB, E_KB, E_VB, E_GB, E_END = 0, 256, 384, 416, 928, 1440, 1952, 2464, 2976
O_QC, O_KC, O_VC, O_GC, O_QD, O_KD, O_VD, O_GD, O_END = 0, 512, 640, 768, 1280, 1792, 1920, 2048, 2560


def _dot(a, b):
    return lax.dot_general(a, b, (((1,), (0,)), ((), ())), preferred_element_type=F32)


def _dot_nt(a, b):
    return lax.dot_general(a, b, (((1,), (1,)), ((), ())), preferred_element_type=F32)


def _silu(x):
    return x / (1.0 + jnp.exp(-x))


def _rms(x, g, n):
    ss = jnp.sum(x * x, axis=-1, keepdims=True)
    return x * lax.rsqrt(ss / n + EPS) * g


def _rms_halves(x, g2, lo):
    x2 = x * x
    s_lo = jnp.sum(jnp.where(lo, x2, 0.0), axis=-1, keepdims=True)
    s_hi = jnp.sum(jnp.where(lo, 0.0, x2), axis=-1, keepdims=True)
    r = jnp.where(lo, lax.rsqrt(s_lo / HEAD_DIM + EPS), lax.rsqrt(s_hi / HEAD_DIM + EPS))
    return x * r * g2


def _modulate(x, g, m):
    d = x.shape[-1]
    xn = x * lax.rsqrt(jnp.mean(x * x, axis=-1, keepdims=True) + EPS) * g
    return xn * (1.0 + m[:, d:2 * d]) + m[:, :d], m[:, 2 * d:]


def _split_lanes(x):
    hi = x.astype(BF16)
    lo = (x - hi.astype(F32)).astype(BF16)
    return jnp.concatenate([hi, lo], axis=1)


def _lane_matrix2(entries):
    i = lax.broadcasted_iota(jnp.int32, (LANES, LANES), 0)
    j = lax.broadcasted_iota(jnp.int32, (LANES, LANES), 1)
    m = entries(i, j).astype(BF16)
    return jnp.concatenate([m, m], axis=0)


def _rope_matrix2(rot_dim, period, start):
    half = rot_dim // 2

    def entries(i, j):
        pos = jnp.bitwise_and(j, period - 1) - start
        neg = (pos >= 0) & (pos < half) & (i == j + half)
        plus = (pos >= half) & (pos < rot_dim) & (i == j - half)
        return jnp.where(neg, -1.0, jnp.where(plus, 1.0, 0.0))

    return _lane_matrix2(entries)


def _swap_matrix2():
    return _lane_matrix2(lambda i, j: jnp.where(i == jnp.bitwise_xor(j, HEAD_DIM), 1.0, 0.0))


def _lane_mix(x, m2):
    return _dot(_split_lanes(x), m2)


def _with_ones(v, transposed=False):
    if transposed:
        return jnp.concatenate([v, jnp.ones((LANES, v.shape[1]), v.dtype)], axis=0)
    return jnp.concatenate([v, jnp.ones((v.shape[0], LANES), v.dtype)], axis=1)


def _attend(parts, sink=None):
    mx = None
    for s, _, _ in parts:
        pm = jnp.max(s, axis=-1, keepdims=True)
        mx = pm if mx is None else jnp.maximum(mx, pm)
    if sink is not None:
        mx = jnp.maximum(mx, sink)
    acc, den = None, None
    for s, v, v_t in parts:
        e = jnp.exp2(s - mx)
        po = (_dot_nt if v_t else _dot)(e.astype(BF16), v)
        acc = po if acc is None else acc + po
        if po.shape[1] == LANES:
            ps = jnp.sum(e, axis=-1, keepdims=True)
            den = ps if den is None else den + ps
    if den is None:
        den = acc[:, LANES:]
    if sink is not None:
        den = den + jnp.exp2(sink - mx)
    return acc[:, :LANES] * (1.0 / den)


def _lane_lo():
    return lax.broadcasted_iota(jnp.int32, (1, LANES), 1) < HEAD_DIM


def _store_pair_transposed(ref, bi, p, x):
    xt = x.T
    ref[bi, 0, 2 * p] = xt[:HEAD_DIM]
    ref[bi, 0, 2 * p + 1] = xt[HEAD_DIM:]


def _rope_key_slab(win_ref):
    d = win_ref.shape[1]
    return jnp.concatenate([jnp.zeros((NOPE_A, d), BF16), win_ref[E_KROPE:E_GA, :],
                            jnp.zeros((LANES - QK_A, d), BF16)], axis=0)


def _swap_halves(a):
    return jnp.concatenate([a[HEAD_DIM:], a[:HEAD_DIM]], axis=0)


def _mod_step(n_cond, is_first, bias_row, c_ref, w_ref, o_ref):
    @pl.when(is_first)
    def _():
        o_ref[:n_cond, :] = jnp.broadcast_to(bias_row, (n_cond, o_ref.shape[1]))
        o_ref[n_cond:, :] = jnp.zeros((o_ref.shape[0] - n_cond, o_ref.shape[1]), F32)

    s = _silu(c_ref[...])
    cols = [jnp.broadcast_to(s[:, r:r + 1], (s.shape[0], LANES)) for r in range(n_cond)]
    for t in range(w_ref.shape[1] // LANES):
        sl = slice(t * LANES, (t + 1) * LANES)
        w = w_ref[:, sl]
        for r in range(n_cond):
            o_ref[r:r + 1, sl] += jnp.sum(w * cols[r], axis=0, keepdims=True)


def _mla_keys(cb, kr, wkk_ref, wkv_ref, kg, rope=None):
    kk = _dot(cb, wkk_ref[...])
    keys = []
    for h in range(N_HEADS):
        k = _rms(kk[:, h * LANES:(h + 1) * LANES] + kr, kg, QK_A)
        if rope is not None:
            k = rope(k)
        keys.append(k.astype(BF16))
    return keys, _dot(cb, wkv_ref[...]).astype(BF16)


def _p0_kernel(x_ref, m_ref, ng_ref, win_ref, qag_ref, wq_ref, kvag_ref, wkk_ref, wkv_ref, qg_ref, kg_ref,
               naqg_ref, nakg_ref, wout_ref,
               xo_ref, ckv_ref, krope_ref, nak_ref, nav_ref, y_scr):
    nbs = x_ref.shape[0]
    x = x_ref[...].reshape(nbs * SEQ, D_MODEL)
    h, gate = _modulate(x, ng_ref[...], m_ref[0:1, :])
    hb = h.astype(BF16)
    lo = _lane_lo()
    hi = jnp.logical_not(lo)
    rows = [slice(bi * SEQ, (bi + 1) * SEQ) for bi in range(nbs)]

    qln = _rms(_dot_nt(hb, win_ref[E_QLAT:E_CKV, :]), qag_ref[...], Q_RANK).astype(BF16)
    q_all = _dot(qln, wq_ref[...])
    ckv_n = _rms(_dot_nt(hb, win_ref[E_CKV:E_KROPE, :]), kvag_ref[...], KV_RANK)
    kr = _dot_nt(hb, _rope_key_slab(win_ref))
    for bi, rs in enumerate(rows):
        ckv_ref[bi, 0] = ckv_n[rs]
        krope_ref[bi, 0] = kr[rs].T[NOPE_A:QK_A]
    keys, vals = _mla_keys(ckv_n.astype(BF16), kr, wkk_ref, wkv_ref, kg_ref[...])
    qg = qg_ref[...] * (QK_A ** -0.5 * LOG2E)

    ga = _dot_nt(hb, win_ref[E_GA:E_QB, :])
    zq = _dot_nt(hb, win_ref[E_QB:E_KB, :])
    zk = _dot_nt(hb, win_ref[E_KB:E_VB, :])
    zv = _dot_nt(hb, win_ref[E_VB:E_GB, :])
    gb = _dot_nt(hb, win_ref[E_GB:E_END, :])
    naqg = naqg_ref[...] * (HEAD_DIM ** -0.5 * LOG2E)

    for p in range(N_PAIRS):
        sl = slice(p * LANES, (p + 1) * LANES)
        ys = slice(4 * LANES + p * LANES, 4 * LANES + (p + 1) * LANES)
        qhs = [_rms(q_all[:, hh * LANES:(hh + 1) * LANES], qg, QK_A).astype(BF16) for hh in (2 * p, 2 * p + 1)]
        qb = _rms_halves(zq[:, sl], naqg, lo)
        kb = _rms_halves(zk[:, sl], nakg_ref[...], lo)
        vb = zv[:, sl]
        kbb, vbb = kb.astype(BF16), vb.astype(BF16)
        va = vals[:, sl]
        qms = [jnp.where(keep, qb, 0.0).astype(BF16) for keep in (lo, hi)]
        for bi, rs in enumerate(rows):
            o2 = [_attend([(_dot_nt(qhs[i][rs], keys[2 * p + i][rs]), va[rs], False)]) for i in (0, 1)]
            y_scr[rs, sl] = (jnp.where(lo, o2[0], o2[1]) * _silu(ga[rs, sl])).astype(BF16)
            _store_pair_transposed(nak_ref, bi, p, kb[rs])
            _store_pair_transposed(nav_ref, bi, p, vb[rs])
            o2 = [_attend([(_dot_nt(qms[i][rs], kbb[rs]), vbb[rs], False)]) for i in (0, 1)]
            y_scr[rs, ys] = (jnp.where(lo, o2[0], o2[1]) * _silu(gb[rs, sl])).astype(BF16)

    xo_ref[...] = (x + gate * _dot(y_scr[...], wout_ref[...])).reshape(nbs, SEQ, D_MODEL)


def _full(shape):
    n = len(shape)
    return pl.BlockSpec(shape, lambda *_: (0,) * n, pipeline_mode=pl.Buffered(1))


class _Row(NamedTuple):
    table: jax.Array
    row: int


def _spec(a):
    if isinstance(a, _Row):
        idx = (a.row,) + (0,) * (a.table.ndim - 1)
        return pl.BlockSpec((None,) + a.table.shape[1:], lambda *_: idx, pipeline_mode=pl.Buffered(1))
    return _full(a.shape)


def _arr(a):
    return a.table if isinstance(a, _Row) else a


def _prompt_even(x, m, ng, win, qag, wq, kvag, wkk, wkv, qg, kg, naqg, nakg, wout):
    nb = x.shape[0]
    nbs = PROMPT_BATCHES_PER_STEP
    assert nb % nbs == 0
    ins = (m, ng, win, qag, wq, kvag, wkk, wkv, qg, kg, naqg, nakg, wout)
    return pl.pallas_call(
        _p0_kernel,
        grid=(nb // nbs,),
        in_specs=[pl.BlockSpec((nbs, SEQ, D_MODEL), lambda b: (b, 0, 0))] + [_spec(a) for a in ins],
        out_specs=[pl.BlockSpec((nbs, SEQ, D_MODEL), lambda b: (b, 0, 0)),
                   pl.BlockSpec((nbs, 1, SEQ, KV_RANK), lambda b: (b, 0, 0, 0)),
                   pl.BlockSpec((nbs, 1, ROPE_A, SEQ), lambda b: (b, 0, 0, 0)),
                   pl.BlockSpec((nbs, 1, N_HEADS, HEAD_DIM, SEQ), lambda b: (b, 0, 0, 0, 0)),
                   pl.BlockSpec((nbs, 1, N_HEADS, HEAD_DIM, SEQ), lambda b: (b, 0, 0, 0, 0))],
        out_shape=[jax.ShapeDtypeStruct((nb, SEQ, D_MODEL), F32),
                   jax.ShapeDtypeStruct((nb, 1, SEQ, KV_RANK), F32),
                   jax.ShapeDtypeStruct((nb, 1, ROPE_A, SEQ), F32),
                   jax.ShapeDtypeStruct((nb, 1, N_HEADS, HEAD_DIM, SEQ), F32),
                   jax.ShapeDtypeStruct((nb, 1, N_HEADS, HEAD_DIM, SEQ), F32)],
        scratch_shapes=[pltpu.VMEM((nbs * SEQ, D_MODEL), BF16)],
        compiler_params=pltpu.CompilerParams(dimension_semantics=("arbitrary",), vmem_limit_bytes=VMEM_LIMIT),
        name="prompt_even",
    )(x, *map(_arr, ins))


def _gqa_pair_operands(k, v, kg2, lo):
    kn = _rms_halves(k, kg2, lo)
    return kn, (kn.astype(BF16), pltpu.roll(kn, HEAD_DIM, 1).astype(BF16)), \
        (_with_ones(v.astype(BF16)), _with_ones(pltpu.roll(v, HEAD_DIM, 1).astype(BF16)))


def _p1_kernel(sink_ref, x_ref, m_ref, ng_ref, win_ref, gqg_ref, gkg_ref, sqg_ref, skg_ref, wout_ref,
               xo_ref, gk_ref, gv_ref, sk_ref, sv_ref, y_scr):
    nbs = x_ref.shape[0]
    x = x_ref[...].reshape(nbs * SEQ, D_MODEL)
    h, gate = _modulate(x, ng_ref[...], m_ref[0:1, :])
    hb = h.astype(BF16)
    lo = _lane_lo()
    hi = jnp.logical_not(lo)
    sc = HEAD_DIM ** -0.5 * LOG2E
    rows = [slice(bi * SEQ, (bi + 1) * SEQ) for bi in range(nbs)]

    branches = ((O_QC, O_KC, O_VC, O_GC, gqg_ref, gkg_ref, gk_ref, gv_ref, False, 0),
                (O_QD, O_KD, O_VD, O_GD, sqg_ref, skg_ref, sk_ref, sv_ref, True, 4 * LANES))
    for oq, ok, ov, og, qg_ref, kg_ref, ck_ref, cv_ref, has_sink, yoff in branches:
        zq = _dot(hb, win_ref[:, oq:oq + 4 * LANES])
        zkv = _dot(hb, win_ref[:, ok:ok + 2 * LANES])
        zg = _dot(hb, win_ref[:, og:og + 4 * LANES])
        v = zkv[:, LANES:]
        kn, ks, vs = _gqa_pair_operands(zkv[:, :LANES], v, kg_ref[...], lo)
        for bi, rs in enumerate(rows):
            _store_pair_transposed(ck_ref, bi, 0, kn[rs])
            _store_pair_transposed(cv_ref, bi, 0, v[rs])
        qg = qg_ref[...] * sc
        for p in range(N_PAIRS):
            sl = slice(p * LANES, (p + 1) * LANES)
            qn = _rms_halves(zq[:, sl], qg, lo)
            qms = [jnp.where(keep, qn, 0.0).astype(BF16) for keep in (lo, hi)]
            kv = p // 2
            for bi, rs in enumerate(rows):
                o2 = []
                for half in (0, 1):
                    swap = 0 if kv == half else 1
                    sink = sink_ref[2 * p + half] * LOG2E if has_sink else None
                    o2.append(_attend([(_dot_nt(qms[half][rs], ks[swap][rs]), vs[swap][rs], False)], sink))
                o = jnp.where(lo, o2[0], o2[1])
                y_scr[rs, yoff + p * LANES:yoff + (p + 1) * LANES] = (o * _silu(zg[rs, sl])).astype(BF16)

    xo_ref[...] = (x + gate * _dot(y_scr[...], wout_ref[...])).reshape(nbs, SEQ, D_MODEL)


def _prompt_odd(sink, x, m, ng, win, gqg, gkg, sqg, skg, wout):
    nb = x.shape[0]
    nbs = PROMPT_BATCHES_PER_STEP
    assert nb % nbs == 0
    ins = (m, ng, win, gqg, gkg, sqg, skg, wout)
    cache_spec = pl.BlockSpec((nbs, 1, 2, HEAD_DIM, SEQ), lambda b: (b, 0, 0, 0, 0))
    cache_shape = jax.ShapeDtypeStruct((nb, 1, 2, HEAD_DIM, SEQ), F32)
    return pl.pallas_call(
        _p1_kernel,
        grid=(nb // nbs,),
        in_specs=[pl.BlockSpec(memory_space=pltpu.SMEM),
                  pl.BlockSpec((nbs, SEQ, D_MODEL), lambda b: (b, 0, 0))] + [_spec(a) for a in ins],
        out_specs=[pl.BlockSpec((nbs, SEQ, D_MODEL), lambda b: (b, 0, 0))] + [cache_spec] * 4,
        out_shape=[jax.ShapeDtypeStruct((nb, SEQ, D_MODEL), F32)] + [cache_shape] * 4,
        scratch_shapes=[pltpu.VMEM((nbs * SEQ, D_MODEL), BF16)],
        compiler_params=pltpu.CompilerParams(dimension_semantics=("arbitrary",), vmem_limit_bytes=VMEM_LIMIT),
        name="prompt_odd",
    )(sink, x, *map(_arr, ins))


def _s0a_kernel(n_cond, x_ref, m_ref, ng_ref, win_ref, qag_ref, wq_ref, kvag_ref, wkk_ref, wkv_ref, qg_ref, kg_ref,
                naqg_ref, nakg_ref, cos_ref, sin_ref, ct_ref, wm_ref, bm_ref, wio_ref, woo_ref,
                qa_ref, ka_ref, va_ref, qb_ref, kb_ref, vb_ref, g_ref, mo_ref, wino_ref, wouto_ref):
    step = pl.program_id(0) * pl.num_programs(1) + pl.program_id(1)
    _mod_step(n_cond, step == 0, bm_ref[1:2, :], ct_ref, wm_ref, mo_ref)
    wino_ref[...] = wio_ref[0].astype(BF16)
    wouto_ref[...] = woo_ref[0].astype(BF16)

    b = pl.program_id(0)
    lo = _lane_lo()
    partner = _rope_matrix2(ROPE_A, LANES, NOPE_A)
    hb = _modulate(x_ref[0], ng_ref[...], m_ref[pl.ds(1 + b, 1), :])[0].astype(BF16)
    cos, sin = cos_ref[...], sin_ref[...]

    qln = _rms(_dot_nt(hb, win_ref[E_QLAT:E_CKV, :]), qag_ref[...], Q_RANK).astype(BF16)
    q_all = _dot(qln, wq_ref[...])
    ckv_n = _rms(_dot_nt(hb, win_ref[E_CKV:E_KROPE, :]), kvag_ref[...], KV_RANK)
    kr = _dot_nt(hb, _rope_key_slab(win_ref))
    cb = ckv_n.astype(BF16)
    kk = _dot(cb, wkk_ref[...])
    va_ref[0] = _dot(cb, wkv_ref[...]).astype(BF16)
    zq = _dot_nt(hb, win_ref[E_QB:E_KB, :])
    zk = _dot_nt(hb, win_ref[E_KB:E_VB, :])
    vb_ref[0] = _dot_nt(hb, win_ref[E_VB:E_GB, :]).astype(BF16)
    g_ref[0, :, 0:4 * LANES] = _silu(_dot_nt(hb, win_ref[E_GA:E_QB, :]))
    g_ref[0, :, 4 * LANES:8 * LANES] = _silu(_dot_nt(hb, win_ref[E_GB:E_END, :]))

    qg = qg_ref[...] * (QK_A ** -0.5 * LOG2E)
    kg = kg_ref[...]
    k_partner = _lane_mix(kr * kg, partner) * sin
    for hh in range(N_HEADS):
        sl = slice(hh * LANES, (hh + 1) * LANES)
        qn = _rms(q_all[:, sl], qg, QK_A)
        qa_ref[0, :, sl] = (qn * cos + _lane_mix(qn, partner) * sin).astype(BF16)
        k_raw = kk[:, sl] + kr
        k_inv = lax.rsqrt(jnp.sum(k_raw * k_raw, axis=-1, keepdims=True) / QK_A + EPS)
        ka_ref[0, :, sl] = ((k_raw * kg * cos + k_partner) * k_inv).astype(BF16)
    naqg = naqg_ref[...] * (HEAD_DIM ** -0.5 * LOG2E)
    for p in range(N_PAIRS):
        sl = slice(p * LANES, (p + 1) * LANES)
        qb_ref[0, :, sl] = _rms_halves(zq[:, sl], naqg, lo).astype(BF16)
        kb_ref[0, :, sl] = _rms_halves(zk[:, sl], nakg_ref[...], lo).astype(BF16)


def _sample_even_proj(x, m, ng, win, qag, wq, kvag, wkk, wkv, qg, kg, naqg, nakg, cos, sin,
                      cond_t, n_cond, w_mod, b_mod, w_in_odd, w_out_odd):
    nb, s, _ = x.shape
    nq = s // PROJ_BLOCK
    steps = nb * nq
    assert D_MODEL % (BF16_SUBLANES * steps) == 0
    tr = D_MODEL // steps
    ins = (m, ng, win, qag, wq, kvag, wkk, wkv, qg, kg, naqg, nakg)
    tab = pl.BlockSpec((PROJ_BLOCK, LANES), lambda b, j: (j, 0))

    def blk(w):
        return pl.BlockSpec((1, PROJ_BLOCK, w), lambda b, j: (b, j, 0))

    def shp(w, dt):
        return jax.ShapeDtypeStruct((nb, s, w), dt)

    return pl.pallas_call(
        functools.partial(_s0a_kernel, n_cond),
        grid=(nb, nq),
        in_specs=[blk(D_MODEL)] + [_spec(a) for a in ins] + [tab, tab]
        + [pl.BlockSpec((tr, 8), lambda b, j: (b * nq + j, 0)),
           pl.BlockSpec((None, tr, 3 * D_MODEL), lambda b, j: (1, b * nq + j, 0)),
           _full(b_mod.shape),
           pl.BlockSpec((1, tr, O_END), lambda b, j: (0, b * nq + j, 0)),
           pl.BlockSpec((1, tr, D_MODEL), lambda b, j: (0, b * nq + j, 0))],
        out_specs=[blk(1024), blk(1024), blk(512), blk(512), blk(512), blk(512), blk(1024),
                   pl.BlockSpec((8, 3 * D_MODEL), lambda b, j: (0, 0)),
                   pl.BlockSpec((tr, O_END), lambda b, j: (b * nq + j, 0)),
                   pl.BlockSpec((tr, D_MODEL), lambda b, j: (b * nq + j, 0))],
        out_shape=[shp(1024, BF16), shp(1024, BF16), shp(512, BF16), shp(512, BF16), shp(512, BF16),
                   shp(512, BF16), shp(1024, F32),
                   jax.ShapeDtypeStruct((8, 3 * D_MODEL), F32),
                   jax.ShapeDtypeStruct((D_MODEL, O_END), BF16),
                   jax.ShapeDtypeStruct((D_MODEL, D_MODEL), BF16)],
        compiler_params=pltpu.CompilerParams(dimension_semantics=("arbitrary", "arbitrary"),
                                             vmem_limit_bytes=VMEM_LIMIT),
        name="sample_even_proj",
    )(x, *map(_arr, ins), cos, sin, cond_t, w_mod, b_mod, w_in_odd, w_out_odd)


def _build_bias_table(rpb_ref, tile_scr, tab_ref):
    qc = lax.broadcasted_iota(jnp.int32, (GRID_W, LANES), 0)
    lane = lax.broadcasted_iota(jnp.int32, (GRID_W, LANES), 1)
    kc = jnp.bitwise_and(lane, GRID_W - 1)
    lo = lane < GRID_W
    diff = kc - qc + (NA_COLS - 1)
    cs = jnp.clip(qc - NA_COLS // 2, 0, GRID_W - NA_COLS)
    valid = (kc >= cs) & (kc < cs + NA_COLS)
    tab_ref[...] = jnp.zeros(tab_ref.shape, F32)
    tile_scr[RPB_ROWS] = jnp.zeros((GRID_W, LANES), F32)

    def per_head(h, carry):
        for dr in range(RPB_ROWS):
            t = jnp.zeros((GRID_W, LANES), F32)
            for dc in range(RPB_COLS):
                t = jnp.where(diff == dc, rpb_ref[(h * RPB_ROWS + dr) * RPB_COLS + dc], t)
            tile_scr[dr] = jnp.where(valid, t * LOG2E, NEG_INF)
        for c in range(NA_ROWS // 2, NA_ROWS // 2 + NA_ROWS):
            d0 = 2 * c - NA_ROWS
            tab_ref[0, h, c] = jnp.where(lo, tile_scr[d0], tile_scr[d0 + 1])
            tab_ref[1, h, c] = jnp.where(lo, tile_scr[d0 - 1 if d0 > 0 else RPB_ROWS], tile_scr[d0])
        return carry

    lax.fori_loop(0, N_HEADS, per_head, 0)


def _s0b_kernel(rpb_ref, x_ref, m_ref, qa_ref, ka_ref, va_ref, qb_ref, kb_ref, vb_ref, g_ref,
                cckv_ref, ckr_ref, cnk_ref, cnv_ref, wkk_ref, wkv_ref, kg_ref, wout_ref,
                xo_ref, kca_scr, vca_scr, tile_scr, tab_scr, y_scr):
    b = pl.program_id(0)
    j = pl.program_id(1)
    lo = _lane_lo()
    n_lat = ka_ref.shape[1]

    @pl.when((b == 0) & (j == 0))
    def _():
        _build_bias_table(rpb_ref, tile_scr, tab_scr)

    @pl.when(j == 0)
    def _():
        kr_t = jnp.concatenate([jnp.zeros((NOPE_A, PAST_LEN), F32), ckr_ref[0],
                                jnp.zeros((LANES - QK_A, PAST_LEN), F32)], axis=0)
        keys, vals = _mla_keys(cckv_ref[0].astype(BF16), kr_t.T, wkk_ref, wkv_ref, kg_ref[...])
        for hh in range(N_HEADS):
            kca_scr[:, hh * LANES:(hh + 1) * LANES] = keys[hh]
        vca_scr[...] = vals

    kidx = lax.broadcasted_iota(jnp.int32, (1, n_lat), 1)
    for p in range(N_PAIRS):
        sl = slice(p * LANES, (p + 1) * LANES)
        o2 = []
        va = _with_ones(va_ref[0, :, sl])
        vca = _with_ones(vca_scr[:, sl])
        for hh in (2 * p, 2 * p + 1):
            hs = slice(hh * LANES, (hh + 1) * LANES)
            q = qa_ref[0, :, hs]
            o2.append(_attend([(_dot_nt(q, ka_ref[0, :, hs]), va, False),
                               (_dot_nt(q, kca_scr[:, hs]), vca, False)]))
        oa = jnp.where(lo, o2[0], o2[1])
        y_scr[:, sl] = (oa * g_ref[0, :, sl]).astype(BF16)

        qb = qb_ref[0, :, sl]
        kb = kb_ref[0, :, sl]
        vb = _with_ones(vb_ref[0, :, sl])
        kcb = cnk_ref[0, sl, :].astype(BF16)
        vcb = _with_ones(cnv_ref[0, sl, :].astype(BF16), transposed=True)
        o2 = []
        for half in (0, 1):
            head = 2 * p + half
            qm = jnp.where(lo if half == 0 else jnp.logical_not(lo), qb, jnp.zeros_like(qb))
            s_lat = _dot_nt(qm, kb)
            rows = []
            for local in range(NA_Q_BLOCK // GRID_W):
                qr = j * (NA_Q_BLOCK // GRID_W) + local
                par = 0 if local % 2 == 1 else 1
                c0 = (RPB_ROWS + par - local) // 2 - (NA_Q_BLOCK // GRID_W // 2) * j
                bias = jnp.concatenate([tab_scr[par, head, c0 + t] for t in range(n_lat // LANES)], axis=1)
                r0 = jnp.clip(qr - NA_ROWS // 2, 0, n_lat // GRID_W - NA_ROWS) * GRID_W
                ok = (kidx >= r0) & (kidx < r0 + NA_ROWS * GRID_W)
                bias = bias + jnp.where(ok, 0.0, NEG_INF)
                rows.append(s_lat[local * GRID_W:(local + 1) * GRID_W] + bias)
            s_lat = jnp.concatenate(rows, axis=0)
            o2.append(_attend([(s_lat, vb, False), (_dot(qm, kcb), vcb, True)]))
        ob = jnp.where(lo, o2[0], o2[1])
        ys = slice(4 * LANES + p * LANES, 4 * LANES + (p + 1) * LANES)
        y_scr[:, ys] = (ob * g_ref[0, :, ys]).astype(BF16)

    d = x_ref.shape[-1]
    gate = m_ref[pl.ds(1 + b, 1), 2 * d:]
    xo_ref[0] = x_ref[0] + gate * _dot(y_scr[...], wout_ref[...])


def _sample_even_attn(rpb, x, m, qa, ka, va, qb, kb, vb, g, cckv, ckr, cnk, cnv, wkk, wkv, kg, wout):
    nb, s, _ = x.shape
    nq = s // NA_Q_BLOCK

    def blk(w):
        return pl.BlockSpec((1, NA_Q_BLOCK, w), lambda b, j: (b, j, 0))

    def per_batch(a):
        return pl.BlockSpec((1,) + a.shape[1:], lambda b, j: (b, 0, 0))

    return pl.pallas_call(
        _s0b_kernel,
        grid=(nb, nq),
        in_specs=[pl.BlockSpec(memory_space=pltpu.SMEM), blk(D_MODEL), _spec(m),
                  blk(1024), per_batch(ka), per_batch(va), blk(512), per_batch(kb), per_batch(vb), blk(1024),
                  per_batch(cckv), per_batch(ckr), per_batch(cnk), per_batch(cnv),
                  _full(wkk.shape), _full(wkv.shape), _spec(kg), _full(wout.shape)],
        out_specs=blk(D_MODEL),
        out_shape=jax.ShapeDtypeStruct(x.shape, F32),
        scratch_shapes=[pltpu.VMEM((PAST_LEN, N_HEADS * LANES), BF16),
                        pltpu.VMEM((PAST_LEN, N_HEADS * HEAD_DIM), BF16),
                        pltpu.VMEM((RPB_ROWS + 1, GRID_W, LANES), F32),
                        pltpu.VMEM((2, N_HEADS, BIAS_CHUNKS, GRID_W, LANES), F32),
                        pltpu.VMEM((NA_Q_BLOCK, D_MODEL), BF16)],
        compiler_params=pltpu.CompilerParams(dimension_semantics=("arbitrary", "arbitrary"),
                                             vmem_limit_bytes=VMEM_LIMIT),
        name="sample_even_attn",
    )(rpb, x, _arr(m), qa, ka, va, qb, kb, vb, g, cckv, ckr, cnk, cnv, wkk, wkv, _arr(kg), wout)


def _s1a_kernel(x_ref, m_ref, ng_ref, win_ref, gqg_ref, gkg_ref, sqg_ref, skg_ref, cos_ref, sin_ref,
                qc_ref, kc_ref, vc_ref, qd_ref, kd_ref, vd_ref, g_ref):
    b = pl.program_id(0)
    lo = _lane_lo()
    partner = _rope_matrix2(HEAD_DIM, HEAD_DIM, 0)
    swap = _swap_matrix2()[:LANES]
    hb = _modulate(x_ref[0], ng_ref[...], m_ref[pl.ds(1 + b, 1), :])[0].astype(BF16)
    cos, sin = cos_ref[...], sin_ref[...]
    sc = HEAD_DIM ** -0.5 * LOG2E

    def rope(t):
        return t * cos + _lane_mix(t, partner) * sin

    branches = ((O_QC, O_KC, O_GC, gqg_ref, gkg_ref, qc_ref, kc_ref, vc_ref, 0),
                (O_QD, O_KD, O_GD, sqg_ref, skg_ref, qd_ref, kd_ref, vd_ref, 4 * LANES))
    for oq, ok, og, qg_ref, kg_ref, q_out, k_out, v_out, goff in branches:
        zq = _dot(hb, win_ref[:, oq:oq + 4 * LANES])
        zkv = _dot(hb, win_ref[:, ok:ok + 2 * LANES])
        qg = qg_ref[...] * sc
        for p in range(N_PAIRS):
            sl = slice(p * LANES, (p + 1) * LANES)
            q_out[0, :, sl] = rope(_rms_halves(zq[:, sl], qg, lo)).astype(BF16)
        kn = rope(_rms_halves(zkv[:, :LANES], kg_ref[...], lo))
        v = zkv[:, LANES:]
        for out, val in ((k_out, kn.astype(BF16)), (v_out, v.astype(BF16))):
            out[0, :, 0:LANES] = val
            out[0, :, LANES:2 * LANES] = _dot(val, swap).astype(BF16)
        g_ref[0, :, goff:goff + 4 * LANES] = _silu(_dot(hb, win_ref[:, og:og + 4 * LANES]))


def _sample_odd_proj(x, m, ng, win, gqg, gkg, sqg, skg, cos, sin):
    nb, s, _ = x.shape
    nq = s // PROJ_BLOCK
    ins = (m, ng, win, gqg, gkg, sqg, skg)
    tab = pl.BlockSpec((PROJ_BLOCK, LANES), lambda b, j: (j, 0))

    def blk(w):
        return pl.BlockSpec((1, PROJ_BLOCK, w), lambda b, j: (b, j, 0))

    def shp(w, dt):
        return jax.ShapeDtypeStruct((nb, s, w), dt)

    return pl.pallas_call(
        _s1a_kernel,
        grid=(nb, nq),
        in_specs=[blk(D_MODEL)] + [_spec(a) for a in ins] + [tab, tab],
        out_specs=[blk(512), blk(256), blk(256), blk(512), blk(256), blk(256), blk(1024)],
        out_shape=[shp(512, BF16), shp(256, BF16), shp(256, BF16), shp(512, BF16), shp(256, BF16),
                   shp(256, BF16), shp(1024, F32)],
        compiler_params=pltpu.CompilerParams(dimension_semantics=("arbitrary", "arbitrary"),
                                             vmem_limit_bytes=VMEM_LIMIT),
        name="sample_odd_proj",
    )(x, *map(_arr, ins), cos, sin)


def _s1b_kernel(sink_ref, x_ref, m_ref, qc_ref, kc_ref, vc_ref, qd_ref, kd_ref, vd_ref, g_ref,
                cgk_ref, cgv_ref, csk_ref, csv_ref, wout_ref, xo_ref, y_scr):
    b = pl.program_id(0)
    j = pl.program_id(1)
    lo = _lane_lo()
    n_lat = kc_ref.shape[1]
    win_keys = Q_BLOCK + 2 * SWA_HALF

    def ctx_pair(ref, values=False):
        a = ref[0].astype(BF16)
        pair = (a, _swap_halves(a))
        return tuple(_with_ones(t, transposed=True) for t in pair) if values else pair

    cgk, cgv, csk, csv = ctx_pair(cgk_ref), ctx_pair(cgv_ref, True), ctx_pair(csk_ref), ctx_pair(csv_ref, True)
    vcs = [_with_ones(vc_ref[0, :, w * LANES:(w + 1) * LANES]) for w in (0, 1)]

    ks = pl.multiple_of(jnp.clip(j * Q_BLOCK - SWA_HALF, 0, n_lat - win_keys), SWA_HALF)
    qpos = j * Q_BLOCK + lax.broadcasted_iota(jnp.int32, (Q_BLOCK, win_keys), 0)
    kpos = ks + lax.broadcasted_iota(jnp.int32, (Q_BLOCK, win_keys), 1)
    band = jnp.abs(qpos - kpos) <= SWA_HALF
    vds = [_with_ones(vd_ref[0, pl.ds(ks, win_keys), w * LANES:(w + 1) * LANES]) for w in (0, 1)]

    for p in range(N_PAIRS):
        sl = slice(p * LANES, (p + 1) * LANES)
        kv = p // 2
        qc = qc_ref[0, :, sl]
        qd = qd_ref[0, :, sl]
        oc2, od2 = [], []
        for half in (0, 1):
            swap = 0 if kv == half else 1
            ws = slice(swap * LANES, (swap + 1) * LANES)
            keep = lo if half == 0 else jnp.logical_not(lo)
            qm = jnp.where(keep, qc, jnp.zeros_like(qc))
            oc2.append(_attend([(_dot_nt(qm, kc_ref[0, :, ws]), vcs[swap], False),
                                (_dot(qm, cgk[swap]), cgv[swap], True)]))
            qm = jnp.where(keep, qd, jnp.zeros_like(qd))
            s_loc = jnp.where(band, _dot_nt(qm, kd_ref[0, pl.ds(ks, win_keys), ws]), NEG_INF)
            od2.append(_attend([(s_loc, vds[swap], False),
                                (_dot(qm, csk[swap]), csv[swap], True)], sink_ref[2 * p + half] * LOG2E))
        y_scr[:, sl] = (jnp.where(lo, oc2[0], oc2[1]) * g_ref[0, :, sl]).astype(BF16)
        ys = slice(4 * LANES + p * LANES, 4 * LANES + (p + 1) * LANES)
        y_scr[:, ys] = (jnp.where(lo, od2[0], od2[1]) * g_ref[0, :, ys]).astype(BF16)

    d = x_ref.shape[-1]
    gate = m_ref[pl.ds(1 + b, 1), 2 * d:]
    xo_ref[0] = x_ref[0] + gate * _dot(y_scr[...], wout_ref[...])


def _sample_odd_attn(sink, x, m, qc, kc, vc, qd, kd, vd, g, cgk, cgv, csk, csv, wout):
    nb, s, _ = x.shape
    nq = s // Q_BLOCK

    def blk(w):
        return pl.BlockSpec((1, Q_BLOCK, w), lambda b, j: (b, j, 0))

    def per_batch(a):
        return pl.BlockSpec((1,) + a.shape[1:], lambda b, j: (b, 0, 0))

    return pl.pallas_call(
        _s1b_kernel,
        grid=(nb, nq),
        in_specs=[pl.BlockSpec(memory_space=pltpu.SMEM), blk(D_MODEL), _spec(m),
                  blk(512), per_batch(kc), per_batch(vc), blk(512), per_batch(kd), per_batch(vd), blk(1024),
                  per_batch(cgk), per_batch(cgv), per_batch(csk), per_batch(csv), _full(wout.shape)],
        out_specs=blk(D_MODEL),
        out_shape=jax.ShapeDtypeStruct(x.shape, F32),
        scratch_shapes=[pltpu.VMEM((Q_BLOCK, D_MODEL), BF16)],
        compiler_params=pltpu.CompilerParams(dimension_semantics=("arbitrary", "arbitrary"),
                                             vmem_limit_bytes=VMEM_LIMIT),
        name="sample_odd_attn",
    )(sink, x, _arr(m), qc, kc, vc, qd, kd, vd, g, cgk, cgv, csk, csv, wout)


WEIGHT_PREP_STEPS = 8
EVEN_IN_CHUNKS = 6
COND_PREP_ROWS = 256
BF16_SUBLANES = 16


G_MLA_Q, G_MLA_K, G_NA_Q, G_NA_K, G_GQA_Q, G_GQA_K, G_SWA_Q, G_SWA_K, N_GAINS = range(9)


GAIN_WIDTHS = (QK_A, QK_A) + (HEAD_DIM,) * 6


def _cond_prep_kernel(n_cond, ct_ref, wm_ref, bm_ref, ng_ref, gains_ref, mo_ref, gt_ref, ngt_ref):
    _mod_step(n_cond, pl.program_id(0) == 0, bm_ref[0:1, :], ct_ref, wm_ref, mo_ref)
    gt_ref[...] = jnp.zeros(gt_ref.shape, F32)
    start = 0
    for r, w in enumerate(GAIN_WIDTHS):
        g = gains_ref[:, start:start + w]
        start += w
        for off in range(0, LANES - w + 1, w):
            gt_ref[r, :, off:off + w] = g
    for layer in range(ngt_ref.shape[0]):
        ngt_ref[layer] = ng_ref[layer:layer + 1, :]


def _cond_prep(cond_t, n_cond, w_mod, b_mod, norm_g, gains):
    assert len(gains) == N_GAINS and tuple(g.shape[-1] for g in gains) == GAIN_WIDTHS
    gains_row = jnp.concatenate([g.reshape(1, -1) for g in gains], axis=1)
    tk = COND_PREP_ROWS
    return pl.pallas_call(
        functools.partial(_cond_prep_kernel, n_cond),
        grid=(D_MODEL // tk,),
        in_specs=[pl.BlockSpec((tk, 8), lambda k: (k, 0)),
                  pl.BlockSpec((None, tk, 3 * D_MODEL), lambda k: (0, k, 0)),
                  _full(b_mod.shape), _full(norm_g.shape), _full(gains_row.shape)],
        out_specs=[_full((8, 3 * D_MODEL)), _full((N_GAINS, 1, LANES)), _full((norm_g.shape[0], 1, D_MODEL))],
        out_shape=[jax.ShapeDtypeStruct((8, 3 * D_MODEL), F32), jax.ShapeDtypeStruct((N_GAINS, 1, LANES), F32),
                   jax.ShapeDtypeStruct((norm_g.shape[0], 1, D_MODEL), F32)],
        compiler_params=pltpu.CompilerParams(dimension_semantics=("arbitrary",)),
        name="cond_prep",
    )(cond_t, w_mod, b_mod, norm_g, gains_row)


def _weight_prep_kernel(wie_ref, woe_ref, wqu_ref, wkv_ref, win_e_ref, wout_e_ref, wq_ref, wkk_ref, wkvv_ref):
    win_e_ref[...] = wie_ref[...].astype(BF16)
    wout_e_ref[...] = woe_ref[0].astype(BF16)

    wq_ref[...] = jnp.zeros(wq_ref.shape, BF16)
    for h in range(N_HEADS):
        wq_ref[:, h * LANES:h * LANES + QK_A] = wqu_ref[0, :, h * QK_A:(h + 1) * QK_A].astype(BF16)
    lo = _lane_lo()
    for p in range(N_PAIRS):
        a = wkv_ref[0, :, (2 * p) * LANES:(2 * p + 1) * LANES]
        c = wkv_ref[0, :, (2 * p + 1) * LANES:(2 * p + 2) * LANES]
        wkk_ref[:, (2 * p) * LANES:(2 * p + 1) * LANES] = jnp.where(lo, a, 0.0).astype(BF16)
        wkk_ref[:, (2 * p + 1) * LANES:(2 * p + 2) * LANES] = jnp.where(lo, c, 0.0).astype(BF16)
        wkvv_ref[:, p * LANES:(p + 1) * LANES] = jnp.where(lo, pltpu.roll(a, HEAD_DIM, 1), c).astype(BF16)


def _weight_prep(w_in_even_t, w_out_even, w_q_up, w_kv_up):
    n = WEIGHT_PREP_STEPS
    ins = (w_out_even, w_q_up, w_kv_up)
    out_cols = (D_MODEL, N_HEADS * LANES, N_HEADS * LANES, N_HEADS * HEAD_DIM)
    out_rows = (D_MODEL, Q_RANK, KV_RANK, KV_RANK)
    te = E_END // EVEN_IN_CHUNKS
    assert te * EVEN_IN_CHUNKS == E_END and te % BF16_SUBLANES == 0 and EVEN_IN_CHUNKS <= n
    even_spec = pl.BlockSpec((te, D_MODEL), lambda i: (jnp.minimum(i, EVEN_IN_CHUNKS - 1), 0))
    return pl.pallas_call(
        _weight_prep_kernel,
        grid=(n,),
        in_specs=[even_spec] + [pl.BlockSpec((1, a.shape[1] // n, a.shape[2]), lambda i: (0, i, 0)) for a in ins],
        out_specs=[even_spec] + [pl.BlockSpec((r // n, c), lambda i: (i, 0)) for r, c in zip(out_rows, out_cols)],
        out_shape=[jax.ShapeDtypeStruct((E_END, D_MODEL), BF16)]
        + [jax.ShapeDtypeStruct((r, c), BF16) for r, c in zip(out_rows, out_cols)],
        compiler_params=pltpu.CompilerParams(dimension_semantics=("arbitrary",), vmem_limit_bytes=VMEM_LIMIT),
        name="weight_prep",
    )(w_in_even_t, *ins)


def _feature_major(c):
    b, h, l, d = c.shape
    return jnp.swapaxes(c, -1, -2).reshape(b, h * d, l)


def _token_major(c):
    return jnp.swapaxes(c, -1, -2)


def _rope_tables(s, rot_dim, period, start):
    quarter = rot_dim // 4
    t = np.arange(s)
    inv = ROPE_THETA ** (-np.arange(quarter, dtype=np.float64) / quarter)
    row = (t // GRID_W).astype(np.float64)[:, None] * inv
    col = (t % GRID_W).astype(np.float64)[:, None] * inv
    ang = np.concatenate([row, col], axis=-1)
    cos, sin = np.cos(ang), np.sin(ang)
    pre = np.ones((s, start))
    post = np.zeros((s, period - start - rot_dim))
    c = np.concatenate([pre, cos, cos, post], axis=-1)
    sn = np.concatenate([0 * pre, sin, sin, post], axis=-1)
    rep = LANES // period
    return jnp.asarray(np.tile(c, (1, rep)), F32), jnp.asarray(np.tile(sn, (1, rep)), F32)


def kernel(x_prompt, x_sample, cache_mla_ckv, cache_mla_krope, cache_na_k, cache_na_v, cache_gqa_k, cache_gqa_v, cache_swa_k, cache_swa_v, c, c_ctx, norm_g, w_mod, b_mod, w_in_even, mla_qa_g, w_q_up, mla_kva_g, w_kv_up, mla_q_g, mla_k_g, na_q_g, na_k_g, na_rpb, w_out_even, w_in_odd, gqa_q_g, gqa_k_g, swa_q_g, swa_k_g, swa_sink, w_out_odd):
    n_dec = x_sample.shape[0]
    assert w_mod.shape[0] == 2 and n_dec + 1 <= 8

    cond_t = jnp.concatenate([c_ctx[:, None], c.T, jnp.zeros((D_MODEL, 7 - n_dec), F32)], axis=1)
    n_cond = 1 + n_dec
    gains = (mla_q_g, mla_k_g, na_q_g, na_k_g, gqa_q_g, gqa_k_g, swa_q_g, swa_k_g)
    win_e, wout_e, wq, wkk, wkv = _weight_prep(jnp.swapaxes(w_in_even[0], 0, 1), w_out_even, w_q_up, w_kv_up)
    m_even, gt, ngt = _cond_prep(cond_t, n_cond, w_mod, b_mod, norm_g, gains)
    even = (_Row(ngt, 0), win_e, mla_qa_g, wq, mla_kva_g, wkk, wkv,
            _Row(gt, G_MLA_Q), _Row(gt, G_MLA_K), _Row(gt, G_NA_Q), _Row(gt, G_NA_K))
    sink = swa_sink[0].astype(F32)

    cos_e, sin_e = _rope_tables(DEC_SEQ, ROPE_A, LANES, NOPE_A)
    qa, ka, va, qbs, kbs, vbs, g0, m_odd, win_o, wout_o = _sample_even_proj(
        x_sample, m_even, *even, cos_e, sin_e, cond_t, n_cond, w_mod, b_mod, w_in_odd, w_out_odd)
    odd = (_Row(ngt, 1), win_o, _Row(gt, G_GQA_Q), _Row(gt, G_GQA_K), _Row(gt, G_SWA_Q), _Row(gt, G_SWA_K))

    xp1, new_ckv, new_krope, new_na_k, new_na_v = _prompt_even(x_prompt, m_even, *even, wout_e)
    xp2, new_gqa_k, new_gqa_v, new_swa_k, new_swa_v = _prompt_odd(sink, xp1, m_odd, *odd, wout_o)

    ckr = jnp.swapaxes(cache_mla_krope[:, 0], -1, -2)
    xs1 = _sample_even_attn(na_rpb[0].reshape(-1), x_sample, m_even, qa, ka, va, qbs, kbs, vbs, g0,
                            cache_mla_ckv[:, 0], ckr, _feature_major(cache_na_k[:, 0]),
                            _feature_major(cache_na_v[:, 0]), wkk, wkv, _Row(gt, G_MLA_K), wout_e)
    cos_o, sin_o = _rope_tables(DEC_SEQ, HEAD_DIM, HEAD_DIM, 0)
    qc, kc, vc, qd, kd, vd, g1 = _sample_odd_proj(xs1, m_odd, *odd, cos_o, sin_o)
    xs2 = _sample_odd_attn(sink, xs1, m_odd, qc, kc, vc, qd, kd, vd, g1,
                           _feature_major(cache_gqa_k[:, 0]), _feature_major(cache_gqa_v[:, 0]),
                           _feature_major(cache_swa_k[:, 0]), _feature_major(cache_swa_v[:, 0]), wout_o)

    caches = (new_krope, new_na_k, new_na_v, new_gqa_k, new_gqa_v, new_swa_k, new_swa_v)
    return (xp2, xs2, new_ckv) + tuple(_token_major(c) for c in caches)
```

```python
import functools
from typing import NamedTuple

import jax
import jax.numpy as jnp
import numpy as np
from jax import lax
from jax.experimental import pallas as pl
from jax.experimental.pallas import tpu as pltpu

F32 = jnp.float32
BF16 = jnp.bfloat16

D_MODEL = 1024
SEQ = 256
DEC_SEQ = 1024
PAST_LEN = 256
GRID_W = 64
HEAD_DIM = 64
Q_RANK = 256
KV_RANK = 128
NOPE_A = 64
ROPE_A = 32
QK_A = NOPE_A + ROPE_A
N_HEADS = 8
NA_ROWS = 8
NA_COLS = 16
SWA_HALF = 128
ROPE_THETA = 10000.0
EPS = 1e-6
NEG_INF = -1e30
LOG2E = 1.4426950408889634

LANES = 128
Q_BLOCK = 512
NA_Q_BLOCK = 256
PROJ_BLOCK = 512
PROMPT_BATCHES_PER_STEP = 2
PROMPT_ODD_BATCHES_PER_STEP = 4
N_PAIRS = N_HEADS // 2
RPB_ROWS = 2 * NA_ROWS - 1
RPB_COLS = 2 * NA_COLS - 1
BIAS_CHUNKS = 16
VMEM_LIMIT = 48 * 1024 * 1024

E_QLAT, E_CKV, E_KROPE, E_GA, E_QB, E_KB, E_VB, E_GB, E_END = 0, 256, 384, 416, 928, 1440, 1952, 2464, 2976
O_QC, O_KC, O_VC, O_GC, O_QD, O_KD, O_VD, O_GD, O_END = 0, 512, 640, 768, 1280, 1792, 1920, 2048, 2560


def _dot(a, b):
    return lax.dot_general(a, b, (((1,), (0,)), ((), ())), preferred_element_type=F32)


def _dot_nt(a, b):
    return lax.dot_general(a, b, (((1,), (1,)), ((), ())), preferred_element_type=F32)


def _silu(x):
    return x / (1.0 + jnp.exp(-x))


def _rms(x, g, n):
    ss = jnp.sum(x * x, axis=-1, keepdims=True)
    return x * lax.rsqrt(ss / n + EPS) * g


def _rms_halves(x, g2, lo):
    x2 = x * x
    s_lo = jnp.sum(jnp.where(lo, x2, 0.0), axis=-1, keepdims=True)
    s_hi = jnp.sum(jnp.where(lo, 0.0, x2), axis=-1, keepdims=True)
    r = jnp.where(lo, lax.rsqrt(s_lo / HEAD_DIM + EPS), lax.rsqrt(s_hi / HEAD_DIM + EPS))
    return x * r * g2


def _modulate(x, g, m):
    d = x.shape[-1]
    xn = x * lax.rsqrt(jnp.mean(x * x, axis=-1, keepdims=True) + EPS) * g
    return xn * (1.0 + m[:, d:2 * d]) + m[:, :d], m[:, 2 * d:]


def _split_lanes(x):
    hi = x.astype(BF16)
    lo = (x - hi.astype(F32)).astype(BF16)
    return jnp.concatenate([hi, lo], axis=1)


def _lane_matrix2(entries):
    i = lax.broadcasted_iota(jnp.int32, (LANES, LANES), 0)
    j = lax.broadcasted_iota(jnp.int32, (LANES, LANES), 1)
    m = entries(i, j).astype(BF16)
    return jnp.concatenate([m, m], axis=0)


def _rope_matrix2(rot_dim, period, start):
    half = rot_dim // 2

    def entries(i, j):
        pos = jnp.bitwise_and(j, period - 1) - start
        neg = (pos >= 0) & (pos < half) & (i == j + half)
        plus = (pos >= half) & (pos < rot_dim) & (i == j - half)
        return jnp.where(neg, -1.0, jnp.where(plus, 1.0, 0.0))

    return _lane_matrix2(entries)


def _swap_matrix2():
    return _lane_matrix2(lambda i, j: jnp.where(i == jnp.bitwise_xor(j, HEAD_DIM), 1.0, 0.0))


def _lane_mix(x, m2):
    return _dot(_split_lanes(x), m2)


def _with_ones(v, transposed=False):
    if transposed:
        return jnp.concatenate([v, jnp.ones((LANES, v.shape[1]), v.dtype)], axis=0)
    return jnp.concatenate([v, jnp.ones((v.shape[0], LANES), v.dtype)], axis=1)


def _attend(parts, sink=None):
    mx = None
    for s, _, _ in parts:
        pm = jnp.max(s, axis=-1, keepdims=True)
        mx = pm if mx is None else jnp.maximum(mx, pm)
    if sink is not None:
        mx = jnp.maximum(mx, sink)
    acc, den = None, None
    for s, v, v_t in parts:
        e = jnp.exp2(s - mx)
        po = (_dot_nt if v_t else _dot)(e.astype(BF16), v)
        acc = po if acc is None else acc + po
        if po.shape[1] == LANES:
            ps = jnp.sum(e, axis=-1, keepdims=True)
            den = ps if den is None else den + ps
    if den is None:
        den = acc[:, LANES:]
    if sink is not None:
        den = den + jnp.exp2(sink - mx)
    return acc[:, :LANES] * (1.0 / den)


def _lane_lo():
    return lax.broadcasted_iota(jnp.int32, (1, LANES), 1) < HEAD_DIM


def _store_pair_transposed(ref, bi, p, x):
    xt = x.T
    ref[bi, 0, 2 * p] = xt[:HEAD_DIM]
    ref[bi, 0, 2 * p + 1] = xt[HEAD_DIM:]


def _rope_key_slab(win_ref):
    d = win_ref.shape[1]
    return jnp.concatenate([jnp.zeros((NOPE_A, d), BF16), win_ref[E_KROPE:E_GA, :],
                            jnp.zeros((LANES - QK_A, d), BF16)], axis=0)


def _swap_halves(a):
    return jnp.concatenate([a[HEAD_DIM:], a[:HEAD_DIM]], axis=0)


def _mod_step(n_cond, is_first, bias_row, c_ref, w_ref, o_ref):
    @pl.when(is_first)
    def _():
        o_ref[:n_cond, :] = jnp.broadcast_to(bias_row, (n_cond, o_ref.shape[1]))
        o_ref[n_cond:, :] = jnp.zeros((o_ref.shape[0] - n_cond, o_ref.shape[1]), F32)

    s = _silu(c_ref[...])
    cols = [jnp.broadcast_to(s[:, r:r + 1], (s.shape[0], LANES)) for r in range(n_cond)]
    for t in range(w_ref.shape[1] // LANES):
        sl = slice(t * LANES, (t + 1) * LANES)
        w = w_ref[:, sl]
        for r in range(n_cond):
            o_ref[r:r + 1, sl] += jnp.sum(w * cols[r], axis=0, keepdims=True)


def _mla_keys(cb, kr, wkk_ref, wkv_ref, kg, rope=None):
    kk = _dot(cb, wkk_ref[...])
    keys = []
    for h in range(N_HEADS):
        k = _rms(kk[:, h * LANES:(h + 1) * LANES] + kr, kg, QK_A)
        if rope is not None:
            k = rope(k)
        keys.append(k.astype(BF16))
    return keys, _dot(cb, wkv_ref[...]).astype(BF16)


def _p0_kernel(n_cond, x_ref, m_ref, ng_ref, win_ref, qag_ref, wq_ref, kvag_ref, wkk_ref, wkv_ref, qg_ref, kg_ref,
               naqg_ref, nakg_ref, wout_ref, ct_ref, wm_ref, bm_ref, wio_ref, woo_ref,
               xo_ref, ckv_ref, krope_ref, nak_ref, nav_ref, mo_ref, wino_ref, wouto_ref, y_scr):
    _mod_step(n_cond, pl.program_id(0) == 0, bm_ref[1:2, :], ct_ref, wm_ref, mo_ref)
    wino_ref[...] = wio_ref[0].astype(BF16)
    wouto_ref[...] = woo_ref[0].astype(BF16)

    nbs = x_ref.shape[0]
    x = x_ref[...].reshape(nbs * SEQ, D_MODEL)
    h, gate = _modulate(x, ng_ref[...], m_ref[0:1, :])
    hb = h.astype(BF16)
    lo = _lane_lo()
    hi = jnp.logical_not(lo)
    rows = [slice(bi * SEQ, (bi + 1) * SEQ) for bi in range(nbs)]

    qln = _rms(_dot_nt(hb, win_ref[E_QLAT:E_CKV, :]), qag_ref[...], Q_RANK).astype(BF16)
    q_all = _dot(qln, wq_ref[...])
    ckv_n = _rms(_dot_nt(hb, win_ref[E_CKV:E_KROPE, :]), kvag_ref[...], KV_RANK)
    kr = _dot_nt(hb, _rope_key_slab(win_ref))
    for bi, rs in enumerate(rows):
        ckv_ref[bi, 0] = ckv_n[rs]
        krope_ref[bi, 0] = kr[rs].T[NOPE_A:QK_A]
    keys, vals = _mla_keys(ckv_n.astype(BF16), kr, wkk_ref, wkv_ref, kg_ref[...])
    qg = qg_ref[...] * (QK_A ** -0.5 * LOG2E)

    ga = _dot_nt(hb, win_ref[E_GA:E_QB, :])
    zq = _dot_nt(hb, win_ref[E_QB:E_KB, :])
    zk = _dot_nt(hb, win_ref[E_KB:E_VB, :])
    zv = _dot_nt(hb, win_ref[E_VB:E_GB, :])
    gb = _dot_nt(hb, win_ref[E_GB:E_END, :])
    naqg = naqg_ref[...] * (HEAD_DIM ** -0.5 * LOG2E)

    for p in range(N_PAIRS):
        sl = slice(p * LANES, (p + 1) * LANES)
        ys = slice(4 * LANES + p * LANES, 4 * LANES + (p + 1) * LANES)
        qhs = [_rms(q_all[:, hh * LANES:(hh + 1) * LANES], qg, QK_A).astype(BF16) for hh in (2 * p, 2 * p + 1)]
        qb = _rms_halves(zq[:, sl], naqg, lo)
        kb = _rms_halves(zk[:, sl], nakg_ref[...], lo)
        vb = zv[:, sl]
        kbb, vbb = kb.astype(BF16), vb.astype(BF16)
        va = vals[:, sl]
        qms = [jnp.where(keep, qb, 0.0).astype(BF16) for keep in (lo, hi)]
        for bi, rs in enumerate(rows):
            o2 = [_attend([(_dot_nt(qhs[i][rs], keys[2 * p + i][rs]), va[rs], False)]) for i in (0, 1)]
            y_scr[rs, sl] = (jnp.where(lo, o2[0], o2[1]) * _silu(ga[rs, sl])).astype(BF16)
            _store_pair_transposed(nak_ref, bi, p, kb[rs])
            _store_pair_transposed(nav_ref, bi, p, vb[rs])
            o2 = [_attend([(_dot_nt(qms[i][rs], kbb[rs]), vbb[rs], False)]) for i in (0, 1)]
            y_scr[rs, ys] = (jnp.where(lo, o2[0], o2[1]) * _silu(gb[rs, sl])).astype(BF16)

    xo_ref[...] = (x + gate * _dot(y_scr[...], wout_ref[...])).reshape(nbs, SEQ, D_MODEL)


def _full(shape):
    n = len(shape)
    return pl.BlockSpec(shape, lambda *_: (0,) * n, pipeline_mode=pl.Buffered(1))


class _Row(NamedTuple):
    table: jax.Array
    row: int


def _spec(a):
    if isinstance(a, _Row):
        idx = (a.row,) + (0,) * (a.table.ndim - 1)
        return pl.BlockSpec((None,) + a.table.shape[1:], lambda *_: idx, pipeline_mode=pl.Buffered(1))
    return _full(a.shape)


def _arr(a):
    return a.table if isinstance(a, _Row) else a


def _prompt_even(x, m, ng, win, qag, wq, kvag, wkk, wkv, qg, kg, naqg, nakg, wout,
                 cond_t, n_cond, w_mod, b_mod, w_in_odd, w_out_odd):
    nb = x.shape[0]
    nbs = PROMPT_BATCHES_PER_STEP
    steps = nb // nbs
    assert nb % nbs == 0 and D_MODEL % (BF16_SUBLANES * steps) == 0
    tr = D_MODEL // steps
    ins = (m, ng, win, qag, wq, kvag, wkk, wkv, qg, kg, naqg, nakg, wout)
    return pl.pallas_call(
        functools.partial(_p0_kernel, n_cond),
        grid=(steps,),
        in_specs=[pl.BlockSpec((nbs, SEQ, D_MODEL), lambda b: (b, 0, 0))] + [_spec(a) for a in ins]
        + [pl.BlockSpec((tr, 8), lambda b: (b, 0)),
           pl.BlockSpec((None, tr, 3 * D_MODEL), lambda b: (1, b, 0)),
           _full(b_mod.shape),
           pl.BlockSpec((1, tr, O_END), lambda b: (0, b, 0)),
           pl.BlockSpec((1, tr, D_MODEL), lambda b: (0, b, 0))],
        out_specs=[pl.BlockSpec((nbs, SEQ, D_MODEL), lambda b: (b, 0, 0)),
                   pl.BlockSpec((nbs, 1, SEQ, KV_RANK), lambda b: (b, 0, 0, 0)),
                   pl.BlockSpec((nbs, 1, ROPE_A, SEQ), lambda b: (b, 0, 0, 0)),
                   pl.BlockSpec((nbs, 1, N_HEADS, HEAD_DIM, SEQ), lambda b: (b, 0, 0, 0, 0)),
                   pl.BlockSpec((nbs, 1, N_HEADS, HEAD_DIM, SEQ), lambda b: (b, 0, 0, 0, 0)),
                   pl.BlockSpec((8, 3 * D_MODEL), lambda b: (0, 0)),
                   pl.BlockSpec((tr, O_END), lambda b: (b, 0)),
                   pl.BlockSpec((tr, D_MODEL), lambda b: (b, 0))],
        out_shape=[jax.ShapeDtypeStruct((nb, SEQ, D_MODEL), F32),
                   jax.ShapeDtypeStruct((nb, 1, SEQ, KV_RANK), F32),
                   jax.ShapeDtypeStruct((nb, 1, ROPE_A, SEQ), F32),
                   jax.ShapeDtypeStruct((nb, 1, N_HEADS, HEAD_DIM, SEQ), F32),
                   jax.ShapeDtypeStruct((nb, 1, N_HEADS, HEAD_DIM, SEQ), F32),
                   jax.ShapeDtypeStruct((8, 3 * D_MODEL), F32),
                   jax.ShapeDtypeStruct((D_MODEL, O_END), BF16),
                   jax.ShapeDtypeStruct((D_MODEL, D_MODEL), BF16)],
        scratch_shapes=[pltpu.VMEM((nbs * SEQ, D_MODEL), BF16)],
        compiler_params=pltpu.CompilerParams(dimension_semantics=("arbitrary",), vmem_limit_bytes=VMEM_LIMIT),
        name="prompt_even",
    )(x, *map(_arr, ins), cond_t, w_mod, b_mod, w_in_odd, w_out_odd)


def _gqa_pair_operands(k, v, kg2, lo):
    kn = _rms_halves(k, kg2, lo)
    return kn, (kn.astype(BF16), pltpu.roll(kn, HEAD_DIM, 1).astype(BF16)), \
        (_with_ones(v.astype(BF16)), _with_ones(pltpu.roll(v, HEAD_DIM, 1).astype(BF16)))


def _p1_kernel(sink_ref, x_ref, m_ref, ng_ref, win_ref, gqg_ref, gkg_ref, sqg_ref, skg_ref, wout_ref,
               xo_ref, gk_ref, gv_ref, sk_ref, sv_ref, y_scr):
    nbs = x_ref.shape[0]
    x = x_ref[...].reshape(nbs * SEQ, D_MODEL)
    h, gate = _modulate(x, ng_ref[...], m_ref[0:1, :])
    hb = h.astype(BF16)
    lo = _lane_lo()
    hi = jnp.logical_not(lo)
    sc = HEAD_DIM ** -0.5 * LOG2E
    rows = [slice(bi * SEQ, (bi + 1) * SEQ) for bi in range(nbs)]

    branches = ((O_QC, O_KC, O_VC, O_GC, gqg_ref, gkg_ref, gk_ref, gv_ref, False, 0),
                (O_QD, O_KD, O_VD, O_GD, sqg_ref, skg_ref, sk_ref, sv_ref, True, 4 * LANES))
    for oq, ok, ov, og, qg_ref, kg_ref, ck_ref, cv_ref, has_sink, yoff in branches:
        zq = _dot(hb, win_ref[:, oq:oq + 4 * LANES])
        zkv = _dot(hb, win_ref[:, ok:ok + 2 * LANES])
        zg = _dot(hb, win_ref[:, og:og + 4 * LANES])
        v = zkv[:, LANES:]
        kn, ks, vs = _gqa_pair_operands(zkv[:, :LANES], v, kg_ref[...], lo)
        for bi, rs in enumerate(rows):
            _store_pair_transposed(ck_ref, bi, 0, kn[rs])
            _store_pair_transposed(cv_ref, bi, 0, v[rs])
        qg = qg_ref[...] * sc
        for p in range(N_PAIRS):
            sl = slice(p * LANES, (p + 1) * LANES)
            qn = _rms_halves(zq[:, sl], qg, lo)
            qms = [jnp.where(keep, qn, 0.0).astype(BF16) for keep in (lo, hi)]
            kv = p // 2
            for bi, rs in enumerate(rows):
                o2 = []
                for half in (0, 1):
                    swap = 0 if kv == half else 1
                    sink = sink_ref[2 * p + half] * LOG2E if has_sink else None
                    o2.append(_attend([(_dot_nt(qms[half][rs], ks[swap][rs]), vs[swap][rs], False)], sink))
                o = jnp.where(lo, o2[0], o2[1])
                y_scr[rs, yoff + p * LANES:yoff + (p + 1) * LANES] = (o * _silu(zg[rs, sl])).astype(BF16)

    xo_ref[...] = (x + gate * _dot(y_scr[...], wout_ref[...])).reshape(nbs, SEQ, D_MODEL)


def _prompt_odd(sink, x, m, ng, win, gqg, gkg, sqg, skg, wout):
    nb = x.shape[0]
    nbs = PROMPT_ODD_BATCHES_PER_STEP
    assert nb % nbs == 0
    ins = (m, ng, win, gqg, gkg, sqg, skg, wout)
    cache_spec = pl.BlockSpec((nbs, 1, 2, HEAD_DIM, SEQ), lambda b: (b, 0, 0, 0, 0))
    cache_shape = jax.ShapeDtypeStruct((nb, 1, 2, HEAD_DIM, SEQ), F32)
    return pl.pallas_call(
        _p1_kernel,
        grid=(nb // nbs,),
        in_specs=[pl.BlockSpec(memory_space=pltpu.SMEM),
                  pl.BlockSpec((nbs, SEQ, D_MODEL), lambda b: (b, 0, 0))] + [_spec(a) for a in ins],
        out_specs=[pl.BlockSpec((nbs, SEQ, D_MODEL), lambda b: (b, 0, 0))] + [cache_spec] * 4,
        out_shape=[jax.ShapeDtypeStruct((nb, SEQ, D_MODEL), F32)] + [cache_shape] * 4,
        scratch_shapes=[pltpu.VMEM((nbs * SEQ, D_MODEL), BF16)],
        compiler_params=pltpu.CompilerParams(dimension_semantics=("arbitrary",), vmem_limit_bytes=VMEM_LIMIT),
        name="prompt_odd",
    )(sink, x, *map(_arr, ins))


def _s0a_kernel(x_ref, m_ref, ng_ref, win_ref, qag_ref, wq_ref, kvag_ref, wkk_ref, wkv_ref, qg_ref, kg_ref,
                naqg_ref, nakg_ref, cos_ref, sin_ref,
                qa_ref, ka_ref, va_ref, qb_ref, kb_ref, vb_ref, g_ref):
    b = pl.program_id(0)
    lo = _lane_lo()
    partner = _rope_matrix2(ROPE_A, LANES, NOPE_A)
    hb = _modulate(x_ref[0], ng_ref[...], m_ref[pl.ds(1 + b, 1), :])[0].astype(BF16)
    cos, sin = cos_ref[...], sin_ref[...]

    qln = _rms(_dot_nt(hb, win_ref[E_QLAT:E_CKV, :]), qag_ref[...], Q_RANK).astype(BF16)
    q_all = _dot(qln, wq_ref[...])
    ckv_n = _rms(_dot_nt(hb, win_ref[E_CKV:E_KROPE, :]), kvag_ref[...], KV_RANK)
    kr = _dot_nt(hb, _rope_key_slab(win_ref))
    cb = ckv_n.astype(BF16)
    kk = _dot(cb, wkk_ref[...])
    va_ref[0] = _dot(cb, wkv_ref[...]).astype(BF16)
    zq = _dot_nt(hb, win_ref[E_QB:E_KB, :])
    zk = _dot_nt(hb, win_ref[E_KB:E_VB, :])
    vb_ref[0] = _dot_nt(hb, win_ref[E_VB:E_GB, :]).astype(BF16)
    g_ref[0, :, 0:4 * LANES] = _silu(_dot_nt(hb, win_ref[E_GA:E_QB, :]))
    g_ref[0, :, 4 * LANES:8 * LANES] = _silu(_dot_nt(hb, win_ref[E_GB:E_END, :]))

    qg = qg_ref[...] * (QK_A ** -0.5 * LOG2E)
    kg = kg_ref[...]
    k_partner = _lane_mix(kr * kg, partner) * sin
    for hh in range(N_HEADS):
        sl = slice(hh * LANES, (hh + 1) * LANES)
        qn = _rms(q_all[:, sl], qg, QK_A)
        qa_ref[0, :, sl] = (qn * cos + _lane_mix(qn, partner) * sin).astype(BF16)
        k_raw = kk[:, sl] + kr
        k_inv = lax.rsqrt(jnp.sum(k_raw * k_raw, axis=-1, keepdims=True) / QK_A + EPS)
        ka_ref[0, :, sl] = ((k_raw * kg * cos + k_partner) * k_inv).astype(BF16)
    naqg = naqg_ref[...] * (HEAD_DIM ** -0.5 * LOG2E)
    for p in range(N_PAIRS):
        sl = slice(p * LANES, (p + 1) * LANES)
        qb_ref[0, :, sl] = _rms_halves(zq[:, sl], naqg, lo).astype(BF16)
        kb_ref[0, :, sl] = _rms_halves(zk[:, sl], nakg_ref[...], lo).astype(BF16)


def _sample_even_proj(x, m, ng, win, qag, wq, kvag, wkk, wkv, qg, kg, naqg, nakg, cos, sin):
    nb, s, _ = x.shape
    nq = s // PROJ_BLOCK
    ins = (m, ng, win, qag, wq, kvag, wkk, wkv, qg, kg, naqg, nakg)
    tab = pl.BlockSpec((PROJ_BLOCK, LANES), lambda b, j: (j, 0))

    def blk(w):
        return pl.BlockSpec((1, PROJ_BLOCK, w), lambda b, j: (b, j, 0))

    def shp(w, dt):
        return jax.ShapeDtypeStruct((nb, s, w), dt)

    return pl.pallas_call(
        _s0a_kernel,
        grid=(nb, nq),
        in_specs=[blk(D_MODEL)] + [_spec(a) for a in ins] + [tab, tab],
        out_specs=[blk(1024), blk(1024), blk(512), blk(512), blk(512), blk(512), blk(1024)],
        out_shape=[shp(1024, BF16), shp(1024, BF16), shp(512, BF16), shp(512, BF16), shp(512, BF16),
                   shp(512, BF16), shp(1024, F32)],
        compiler_params=pltpu.CompilerParams(dimension_semantics=("arbitrary", "arbitrary"),
                                             vmem_limit_bytes=VMEM_LIMIT),
        name="sample_even_proj",
    )(x, *map(_arr, ins), cos, sin)


def _build_bias_table(rpb_ref, tile_scr, tab_ref):
    qc = lax.broadcasted_iota(jnp.int32, (GRID_W, LANES), 0)
    lane = lax.broadcasted_iota(jnp.int32, (GRID_W, LANES), 1)
    kc = jnp.bitwise_and(lane, GRID_W - 1)
    lo = lane < GRID_W
    diff = kc - qc + (NA_COLS - 1)
    cs = jnp.clip(qc - NA_COLS // 2, 0, GRID_W - NA_COLS)
    valid = (kc >= cs) & (kc < cs + NA_COLS)
    tab_ref[...] = jnp.zeros(tab_ref.shape, F32)
    tile_scr[RPB_ROWS] = jnp.zeros((GRID_W, LANES), F32)

    def per_head(h, carry):
        for dr in range(RPB_ROWS):
            t = jnp.zeros((GRID_W, LANES), F32)
            for dc in range(RPB_COLS):
                t = jnp.where(diff == dc, rpb_ref[(h * RPB_ROWS + dr) * RPB_COLS + dc], t)
            tile_scr[dr] = jnp.where(valid, t * LOG2E, NEG_INF)
        for c in range(NA_ROWS // 2, NA_ROWS // 2 + NA_ROWS):
            d0 = 2 * c - NA_ROWS
            tab_ref[0, h, c] = jnp.where(lo, tile_scr[d0], tile_scr[d0 + 1])
            tab_ref[1, h, c] = jnp.where(lo, tile_scr[d0 - 1 if d0 > 0 else RPB_ROWS], tile_scr[d0])
        return carry

    lax.fori_loop(0, N_HEADS, per_head, 0)


def _s0b_kernel(rpb_ref, x_ref, m_ref, qa_ref, ka_ref, va_ref, qb_ref, kb_ref, vb_ref, g_ref,
                cckv_ref, ckr_ref, cnk_ref, cnv_ref, wkk_ref, wkv_ref, kg_ref, wout_ref,
                xo_ref, kca_scr, vca_scr, tile_scr, tab_scr, y_scr):
    b = pl.program_id(0)
    j = pl.program_id(1)
    lo = _lane_lo()
    n_lat = ka_ref.shape[1]

    @pl.when((b == 0) & (j == 0))
    def _():
        _build_bias_table(rpb_ref, tile_scr, tab_scr)

    @pl.when(j == 0)
    def _():
        kr_t = jnp.concatenate([jnp.zeros((NOPE_A, PAST_LEN), F32), ckr_ref[0],
                                jnp.zeros((LANES - QK_A, PAST_LEN), F32)], axis=0)
        keys, vals = _mla_keys(cckv_ref[0].astype(BF16), kr_t.T, wkk_ref, wkv_ref, kg_ref[...])
        for hh in range(N_HEADS):
            kca_scr[:, hh * LANES:(hh + 1) * LANES] = keys[hh]
        vca_scr[...] = vals

    kidx = lax.broadcasted_iota(jnp.int32, (1, n_lat), 1)
    for p in range(N_PAIRS):
        sl = slice(p * LANES, (p + 1) * LANES)
        o2 = []
        va = _with_ones(va_ref[0, :, sl])
        vca = _with_ones(vca_scr[:, sl])
        for hh in (2 * p, 2 * p + 1):
            hs = slice(hh * LANES, (hh + 1) * LANES)
            q = qa_ref[0, :, hs]
            o2.append(_attend([(_dot_nt(q, ka_ref[0, :, hs]), va, False),
                               (_dot_nt(q, kca_scr[:, hs]), vca, False)]))
        oa = jnp.where(lo, o2[0], o2[1])
        y_scr[:, sl] = (oa * g_ref[0, :, sl]).astype(BF16)

        qb = qb_ref[0, :, sl]
        kb = kb_ref[0, :, sl]
        vb = _with_ones(vb_ref[0, :, sl])
        kcb = cnk_ref[0, sl, :].astype(BF16)
        vcb = _with_ones(cnv_ref[0, sl, :].astype(BF16), transposed=True)
        o2 = []
        for half in (0, 1):
            head = 2 * p + half
            qm = jnp.where(lo if half == 0 else jnp.logical_not(lo), qb, jnp.zeros_like(qb))
            s_lat = _dot_nt(qm, kb)
            rows = []
            for local in range(NA_Q_BLOCK // GRID_W):
                qr = j * (NA_Q_BLOCK // GRID_W) + local
                par = 0 if local % 2 == 1 else 1
                c0 = (RPB_ROWS + par - local) // 2 - (NA_Q_BLOCK // GRID_W // 2) * j
                bias = jnp.concatenate([tab_scr[par, head, c0 + t] for t in range(n_lat // LANES)], axis=1)
                r0 = jnp.clip(qr - NA_ROWS // 2, 0, n_lat // GRID_W - NA_ROWS) * GRID_W
                ok = (kidx >= r0) & (kidx < r0 + NA_ROWS * GRID_W)
                bias = bias + jnp.where(ok, 0.0, NEG_INF)
                rows.append(s_lat[local * GRID_W:(local + 1) * GRID_W] + bias)
            s_lat = jnp.concatenate(rows, axis=0)
            o2.append(_attend([(s_lat, vb, False), (_dot(qm, kcb), vcb, True)]))
        ob = jnp.where(lo, o2[0], o2[1])
        ys = slice(4 * LANES + p * LANES, 4 * LANES + (p + 1) * LANES)
        y_scr[:, ys] = (ob * g_ref[0, :, ys]).astype(BF16)

    d = x_ref.shape[-1]
    gate = m_ref[pl.ds(1 + b, 1), 2 * d:]
    xo_ref[0] = x_ref[0] + gate * _dot(y_scr[...], wout_ref[...])


def _sample_even_attn(rpb, x, m, qa, ka, va, qb, kb, vb, g, cckv, ckr, cnk, cnv, wkk, wkv, kg, wout):
    nb, s, _ = x.shape
    nq = s // NA_Q_BLOCK

    def blk(w):
        return pl.BlockSpec((1, NA_Q_BLOCK, w), lambda b, j: (b, j, 0))

    def per_batch(a):
        return pl.BlockSpec((1,) + a.shape[1:], lambda b, j: (b, 0, 0))

    return pl.pallas_call(
        _s0b_kernel,
        grid=(nb, nq),
        in_specs=[pl.BlockSpec(memory_space=pltpu.SMEM), blk(D_MODEL), _spec(m),
                  blk(1024), per_batch(ka), per_batch(va), blk(512), per_batch(kb), per_batch(vb), blk(1024),
                  per_batch(cckv), per_batch(ckr), per_batch(cnk), per_batch(cnv),
                  _full(wkk.shape), _full(wkv.shape), _spec(kg), _full(wout.shape)],
        out_specs=blk(D_MODEL),
        out_shape=jax.ShapeDtypeStruct(x.shape, F32),
        scratch_shapes=[pltpu.VMEM((PAST_LEN, N_HEADS * LANES), BF16),
                        pltpu.VMEM((PAST_LEN, N_HEADS * HEAD_DIM), BF16),
                        pltpu.VMEM((RPB_ROWS + 1, GRID_W, LANES), F32),
                        pltpu.VMEM((2, N_HEADS, BIAS_CHUNKS, GRID_W, LANES), F32),
                        pltpu.VMEM((NA_Q_BLOCK, D_MODEL), BF16)],
        compiler_params=pltpu.CompilerParams(dimension_semantics=("arbitrary", "arbitrary"),
                                             vmem_limit_bytes=VMEM_LIMIT),
        name="sample_even_attn",
    )(rpb, x, _arr(m), qa, ka, va, qb, kb, vb, g, cckv, ckr, cnk, cnv, wkk, wkv, _arr(kg), wout)


def _s1a_kernel(x_ref, m_ref, ng_ref, win_ref, gqg_ref, gkg_ref, sqg_ref, skg_ref, cos_ref, sin_ref,
                qc_ref, kc_ref, vc_ref, qd_ref, kd_ref, vd_ref, g_ref):
    b = pl.program_id(0)
    lo = _lane_lo()
    partner = _rope_matrix2(HEAD_DIM, HEAD_DIM, 0)
    swap = _swap_matrix2()[:LANES]
    hb = _modulate(x_ref[0], ng_ref[...], m_ref[pl.ds(1 + b, 1), :])[0].astype(BF16)
    cos, sin = cos_ref[...], sin_ref[...]
    sc = HEAD_DIM ** -0.5 * LOG2E

    def rope(t):
        return t * cos + _lane_mix(t, partner) * sin

    branches = ((O_QC, O_KC, O_GC, gqg_ref, gkg_ref, qc_ref, kc_ref, vc_ref, 0),
                (O_QD, O_KD, O_GD, sqg_ref, skg_ref, qd_ref, kd_ref, vd_ref, 4 * LANES))
    for oq, ok, og, qg_ref, kg_ref, q_out, k_out, v_out, goff in branches:
        zq = _dot(hb, win_ref[:, oq:oq + 4 * LANES])
        zkv = _dot(hb, win_ref[:, ok:ok + 2 * LANES])
        qg = qg_ref[...] * sc
        for p in range(N_PAIRS):
            sl = slice(p * LANES, (p + 1) * LANES)
            q_out[0, :, sl] = rope(_rms_halves(zq[:, sl], qg, lo)).astype(BF16)
        kn = rope(_rms_halves(zkv[:, :LANES], kg_ref[...], lo))
        v = zkv[:, LANES:]
        for out, val in ((k_out, kn.astype(BF16)), (v_out, v.astype(BF16))):
            out[0, :, 0:LANES] = val
            out[0, :, LANES:2 * LANES] = _dot(val, swap).astype(BF16)
        g_ref[0, :, goff:goff + 4 * LANES] = _silu(_dot(hb, win_ref[:, og:og + 4 * LANES]))


def _sample_odd_proj(x, m, ng, win, gqg, gkg, sqg, skg, cos, sin):
    nb, s, _ = x.shape
    nq = s // PROJ_BLOCK
    ins = (m, ng, win, gqg, gkg, sqg, skg)
    tab = pl.BlockSpec((PROJ_BLOCK, LANES), lambda b, j: (j, 0))

    def blk(w):
        return pl.BlockSpec((1, PROJ_BLOCK, w), lambda b, j: (b, j, 0))

    def shp(w, dt):
        return jax.ShapeDtypeStruct((nb, s, w), dt)

    return pl.pallas_call(
        _s1a_kernel,
        grid=(nb, nq),
        in_specs=[blk(D_MODEL)] + [_spec(a) for a in ins] + [tab, tab],
        out_specs=[blk(512), blk(256), blk(256), blk(512), blk(256), blk(256), blk(1024)],
        out_shape=[shp(512, BF16), shp(256, BF16), shp(256, BF16), shp(512, BF16), shp(256, BF16),
                   shp(256, BF16), shp(1024, F32)],
        compiler_params=pltpu.CompilerParams(dimension_semantics=("arbitrary", "arbitrary"),
                                             vmem_limit_bytes=VMEM_LIMIT),
        name="sample_odd_proj",
    )(x, *map(_arr, ins), cos, sin)


def _s1b_kernel(sink_ref, x_ref, m_ref, qc_ref, kc_ref, vc_ref, qd_ref, kd_ref, vd_ref, g_ref,
                cgk_ref, cgv_ref, csk_ref, csv_ref, wout_ref, xo_ref, y_scr):
    b = pl.program_id(0)
    j = pl.program_id(1)
    lo = _lane_lo()
    n_lat = kc_ref.shape[1]
    win_keys = Q_BLOCK + 2 * SWA_HALF

    def ctx_pair(ref, values=False):
        a = ref[0].astype(BF16)
        pair = (a, _swap_halves(a))
        return tuple(_with_ones(t, transposed=True) for t in pair) if values else pair

    cgk, cgv, csk, csv = ctx_pair(cgk_ref), ctx_pair(cgv_ref, True), ctx_pair(csk_ref), ctx_pair(csv_ref, True)
    vcs = [_with_ones(vc_ref[0, :, w * LANES:(w + 1) * LANES]) for w in (0, 1)]

    ks = pl.multiple_of(jnp.clip(j * Q_BLOCK - SWA_HALF, 0, n_lat - win_keys), SWA_HALF)
    qpos = j * Q_BLOCK + lax.broadcasted_iota(jnp.int32, (Q_BLOCK, win_keys), 0)
    kpos = ks + lax.broadcasted_iota(jnp.int32, (Q_BLOCK, win_keys), 1)
    band = jnp.abs(qpos - kpos) <= SWA_HALF
    vds = [_with_ones(vd_ref[0, pl.ds(ks, win_keys), w * LANES:(w + 1) * LANES]) for w in (0, 1)]

    for p in range(N_PAIRS):
        sl = slice(p * LANES, (p + 1) * LANES)
        kv = p // 2
        qc = qc_ref[0, :, sl]
        qd = qd_ref[0, :, sl]
        oc2, od2 = [], []
        for half in (0, 1):
            swap = 0 if kv == half else 1
            ws = slice(swap * LANES, (swap + 1) * LANES)
            keep = lo if half == 0 else jnp.logical_not(lo)
            qm = jnp.where(keep, qc, jnp.zeros_like(qc))
            oc2.append(_attend([(_dot_nt(qm, kc_ref[0, :, ws]), vcs[swap], False),
                                (_dot(qm, cgk[swap]), cgv[swap], True)]))
            qm = jnp.where(keep, qd, jnp.zeros_like(qd))
            s_loc = jnp.where(band, _dot_nt(qm, kd_ref[0, pl.ds(ks, win_keys), ws]), NEG_INF)
            od2.append(_attend([(s_loc, vds[swap], False),
                                (_dot(qm, csk[swap]), csv[swap], True)], sink_ref[2 * p + half] * LOG2E))
        y_scr[:, sl] = (jnp.where(lo, oc2[0], oc2[1]) * g_ref[0, :, sl]).astype(BF16)
        ys = slice(4 * LANES + p * LANES, 4 * LANES + (p + 1) * LANES)
        y_scr[:, ys] = (jnp.where(lo, od2[0], od2[1]) * g_ref[0, :, ys]).astype(BF16)

    d = x_ref.shape[-1]
    gate = m_ref[pl.ds(1 + b, 1), 2 * d:]
    xo_ref[0] = x_ref[0] + gate * _dot(y_scr[...], wout_ref[...])


def _sample_odd_attn(sink, x, m, qc, kc, vc, qd, kd, vd, g, cgk, cgv, csk, csv, wout):
    nb, s, _ = x.shape
    nq = s // Q_BLOCK

    def blk(w):
        return pl.BlockSpec((1, Q_BLOCK, w), lambda b, j: (b, j, 0))

    def per_batch(a):
        return pl.BlockSpec((1,) + a.shape[1:], lambda b, j: (b, 0, 0))

    return pl.pallas_call(
        _s1b_kernel,
        grid=(nb, nq),
        in_specs=[pl.BlockSpec(memory_space=pltpu.SMEM), blk(D_MODEL), _spec(m),
                  blk(512), per_batch(kc), per_batch(vc), blk(512), per_batch(kd), per_batch(vd), blk(1024),
                  per_batch(cgk), per_batch(cgv), per_batch(csk), per_batch(csv), _full(wout.shape)],
        out_specs=blk(D_MODEL),
        out_shape=jax.ShapeDtypeStruct(x.shape, F32),
        scratch_shapes=[pltpu.VMEM((Q_BLOCK, D_MODEL), BF16)],
        compiler_params=pltpu.CompilerParams(dimension_semantics=("arbitrary", "arbitrary"),
                                             vmem_limit_bytes=VMEM_LIMIT),
        name="sample_odd_attn",
    )(sink, x, _arr(m), qc, kc, vc, qd, kd, vd, g, cgk, cgv, csk, csv, wout)


WEIGHT_PREP_STEPS = 8
EVEN_IN_CHUNKS = 6
COND_PREP_ROWS = 256
BF16_SUBLANES = 16


G_MLA_Q, G_MLA_K, G_NA_Q, G_NA_K, G_GQA_Q, G_GQA_K, G_SWA_Q, G_SWA_K, N_GAINS = range(9)


GAIN_WIDTHS = (QK_A, QK_A) + (HEAD_DIM,) * 6


def _cond_prep_kernel(n_cond, ct_ref, wm_ref, bm_ref, ng_ref, gains_ref, mo_ref, gt_ref, ngt_ref):
    _mod_step(n_cond, pl.program_id(0) == 0, bm_ref[0:1, :], ct_ref, wm_ref, mo_ref)
    gt_ref[...] = jnp.zeros(gt_ref.shape, F32)
    start = 0
    for r, w in enumerate(GAIN_WIDTHS):
        g = gains_ref[:, start:start + w]
        start += w
        for off in range(0, LANES - w + 1, w):
            gt_ref[r, :, off:off + w] = g
    for layer in range(ngt_ref.shape[0]):
        ngt_ref[layer] = ng_ref[layer:layer + 1, :]


def _cond_prep(cond_t, n_cond, w_mod, b_mod, norm_g, gains):
    assert len(gains) == N_GAINS and tuple(g.shape[-1] for g in gains) == GAIN_WIDTHS
    gains_row = jnp.concatenate([g.reshape(1, -1) for g in gains], axis=1)
    tk = COND_PREP_ROWS
    return pl.pallas_call(
        functools.partial(_cond_prep_kernel, n_cond),
        grid=(D_MODEL // tk,),
        in_specs=[pl.BlockSpec((tk, 8), lambda k: (k, 0)),
                  pl.BlockSpec((None, tk, 3 * D_MODEL), lambda k: (0, k, 0)),
                  _full(b_mod.shape), _full(norm_g.shape), _full(gains_row.shape)],
        out_specs=[_full((8, 3 * D_MODEL)), _full((N_GAINS, 1, LANES)), _full((norm_g.shape[0], 1, D_MODEL))],
        out_shape=[jax.ShapeDtypeStruct((8, 3 * D_MODEL), F32), jax.ShapeDtypeStruct((N_GAINS, 1, LANES), F32),
                   jax.ShapeDtypeStruct((norm_g.shape[0], 1, D_MODEL), F32)],
        compiler_params=pltpu.CompilerParams(dimension_semantics=("arbitrary",)),
        name="cond_prep",
    )(cond_t, w_mod, b_mod, norm_g, gains_row)


def _weight_prep_kernel(wie_ref, woe_ref, wqu_ref, wkv_ref, win_e_ref, wout_e_ref, wq_ref, wkk_ref, wkvv_ref):
    win_e_ref[...] = wie_ref[...].astype(BF16)
    wout_e_ref[...] = woe_ref[0].astype(BF16)

    wq_ref[...] = jnp.zeros(wq_ref.shape, BF16)
    for h in range(N_HEADS):
        wq_ref[:, h * LANES:h * LANES + QK_A] = wqu_ref[0, :, h * QK_A:(h + 1) * QK_A].astype(BF16)
    lo = _lane_lo()
    for p in range(N_PAIRS):
        a = wkv_ref[0, :, (2 * p) * LANES:(2 * p + 1) * LANES]
        c = wkv_ref[0, :, (2 * p + 1) * LANES:(2 * p + 2) * LANES]
        wkk_ref[:, (2 * p) * LANES:(2 * p + 1) * LANES] = jnp.where(lo, a, 0.0).astype(BF16)
        wkk_ref[:, (2 * p + 1) * LANES:(2 * p + 2) * LANES] = jnp.where(lo, c, 0.0).astype(BF16)
        wkvv_ref[:, p * LANES:(p + 1) * LANES] = jnp.where(lo, pltpu.roll(a, HEAD_DIM, 1), c).astype(BF16)


def _weight_prep(w_in_even_t, w_out_even, w_q_up, w_kv_up):
    n = WEIGHT_PREP_STEPS
    ins = (w_out_even, w_q_up, w_kv_up)
    out_cols = (D_MODEL, N_HEADS * LANES, N_HEADS * LANES, N_HEADS * HEAD_DIM)
    out_rows = (D_MODEL, Q_RANK, KV_RANK, KV_RANK)
    te = E_END // EVEN_IN_CHUNKS
    assert te * EVEN_IN_CHUNKS == E_END and te % BF16_SUBLANES == 0 and EVEN_IN_CHUNKS <= n
    even_spec = pl.BlockSpec((te, D_MODEL), lambda i: (jnp.minimum(i, EVEN_IN_CHUNKS - 1), 0))
    return pl.pallas_call(
        _weight_prep_kernel,
        grid=(n,),
        in_specs=[even_spec] + [pl.BlockSpec((1, a.shape[1] // n, a.shape[2]), lambda i: (0, i, 0)) for a in ins],
        out_specs=[even_spec] + [pl.BlockSpec((r // n, c), lambda i: (i, 0)) for r, c in zip(out_rows, out_cols)],
        out_shape=[jax.ShapeDtypeStruct((E_END, D_MODEL), BF16)]
        + [jax.ShapeDtypeStruct((r, c), BF16) for r, c in zip(out_rows, out_cols)],
        compiler_params=pltpu.CompilerParams(dimension_semantics=("arbitrary",), vmem_limit_bytes=VMEM_LIMIT),
        name="weight_prep",
    )(w_in_even_t, *ins)


def _feature_major(c):
    b, h, l, d = c.shape
    return jnp.swapaxes(c, -1, -2).reshape(b, h * d, l)


def _token_major(c):
    return jnp.swapaxes(c, -1, -2)


def _rope_tables(s, rot_dim, period, start):
    quarter = rot_dim // 4
    t = np.arange(s)
    inv = ROPE_THETA ** (-np.arange(quarter, dtype=np.float64) / quarter)
    row = (t // GRID_W).astype(np.float64)[:, None] * inv
    col = (t % GRID_W).astype(np.float64)[:, None] * inv
    ang = np.concatenate([row, col], axis=-1)
    cos, sin = np.cos(ang), np.sin(ang)
    pre = np.ones((s, start))
    post = np.zeros((s, period - start - rot_dim))
    c = np.concatenate([pre, cos, cos, post], axis=-1)
    sn = np.concatenate([0 * pre, sin, sin, post], axis=-1)
    rep = LANES // period
    return jnp.asarray(np.tile(c, (1, rep)), F32), jnp.asarray(np.tile(sn, (1, rep)), F32)


def kernel(x_prompt, x_sample, cache_mla_ckv, cache_mla_krope, cache_na_k, cache_na_v, cache_gqa_k, cache_gqa_v, cache_swa_k, cache_swa_v, c, c_ctx, norm_g, w_mod, b_mod, w_in_even, mla_qa_g, w_q_up, mla_kva_g, w_kv_up, mla_q_g, mla_k_g, na_q_g, na_k_g, na_rpb, w_out_even, w_in_odd, gqa_q_g, gqa_k_g, swa_q_g, swa_k_g, swa_sink, w_out_odd):
    n_dec = x_sample.shape[0]
    assert w_mod.shape[0] == 2 and n_dec + 1 <= 8

    cond_t = jnp.concatenate([c_ctx[:, None], c.T, jnp.zeros((D_MODEL, 7 - n_dec), F32)], axis=1)
    n_cond = 1 + n_dec
    gains = (mla_q_g, mla_k_g, na_q_g, na_k_g, gqa_q_g, gqa_k_g, swa_q_g, swa_k_g)
    win_e, wout_e, wq, wkk, wkv = _weight_prep(jnp.swapaxes(w_in_even[0], 0, 1), w_out_even, w_q_up, w_kv_up)
    m_even, gt, ngt = _cond_prep(cond_t, n_cond, w_mod, b_mod, norm_g, gains)
    even = (_Row(ngt, 0), win_e, mla_qa_g, wq, mla_kva_g, wkk, wkv,
            _Row(gt, G_MLA_Q), _Row(gt, G_MLA_K), _Row(gt, G_NA_Q), _Row(gt, G_NA_K))
    sink = swa_sink[0].astype(F32)

    xp1, new_ckv, new_krope, new_na_k, new_na_v, m_odd, win_o, wout_o = _prompt_even(
        x_prompt, m_even, *even, wout_e, cond_t, n_cond, w_mod, b_mod, w_in_odd, w_out_odd)
    odd = (_Row(ngt, 1), win_o, _Row(gt, G_GQA_Q), _Row(gt, G_GQA_K), _Row(gt, G_SWA_Q), _Row(gt, G_SWA_K))
    xp2, new_gqa_k, new_gqa_v, new_swa_k, new_swa_v = _prompt_odd(sink, xp1, m_odd, *odd, wout_o)

    cos_e, sin_e = _rope_tables(DEC_SEQ, ROPE_A, LANES, NOPE_A)
    qa, ka, va, qbs, kbs, vbs, g0 = _sample_even_proj(x_sample, m_even, *even, cos_e, sin_e)
    ckr = jnp.swapaxes(cache_mla_krope[:, 0], -1, -2)
    xs1 = _sample_even_attn(na_rpb[0].reshape(-1), x_sample, m_even, qa, ka, va, qbs, kbs, vbs, g0,
                            cache_mla_ckv[:, 0], ckr, _feature_major(cache_na_k[:, 0]),
                            _feature_major(cache_na_v[:, 0]), wkk, wkv, _Row(gt, G_MLA_K), wout_e)
    cos_o, sin_o = _rope_tables(DEC_SEQ, HEAD_DIM, HEAD_DIM, 0)
    qc, kc, vc, qd, kd, vd, g1 = _sample_odd_proj(xs1, m_odd, *odd, cos_o, sin_o)
    xs2 = _sample_odd_attn(sink, xs1, m_odd, qc, kc, vc, qd, kd, vd, g1,
                           _feature_major(cache_gqa_k[:, 0]), _feature_major(cache_gqa_v[:, 0]),
                           _feature_major(cache_swa_k[:, 0]), _feature_major(cache_swa_v[:, 0]), wout_o)

    caches = (new_krope, new_na_k, new_na_v, new_gqa_k, new_gqa_v, new_swa_k, new_swa_v)
    return (xp2, xs2, new_ckv) + tuple(_token_major(c) for c in caches)
```

```python
import functools
from typing import NamedTuple

import jax
import jax.numpy as jnp
import numpy as np
from jax import lax
from jax.experimental import pallas as pl
from jax.experimental.pallas import tpu as pltpu

F32 = jnp.float32
BF16 = jnp.bfloat16

D_MODEL = 1024
SEQ = 256
DEC_SEQ = 1024
PAST_LEN = 256
GRID_W = 64
HEAD_DIM = 64
Q_RANK = 256
KV_RANK = 128
NOPE_A = 64
ROPE_A = 32
QK_A = NOPE_A + ROPE_A
N_HEADS = 8
NA_ROWS = 8
NA_COLS = 16
SWA_HALF = 128
ROPE_THETA = 10000.0
EPS = 1e-6
NEG_INF = -1e30
LOG2E = 1.4426950408889634

LANES = 128
Q_BLOCK = 512
NA_Q_BLOCK = 256
PROJ_BLOCK = 512
PROMPT_BATCHES_PER_STEP = 2
PROMPT_ODD_BATCHES_PER_STEP = 4
N_PAIRS = N_HEADS // 2
RPB_ROWS = 2 * NA_ROWS - 1
RPB_COLS = 2 * NA_COLS - 1
BIAS_CHUNKS = 16
VMEM_LIMIT = 48 * 1024 * 1024

E_QLAT, E_CKV, E_KROPE, E_GA, E_QB, E_KB, E_VB, E_GB, E_END = 0, 256, 384, 416, 928, 1440, 1952, 2464, 2976
O_QC, O_KC, O_VC, O_GC, O_QD, O_KD, O_VD, O_GD, O_END = 0, 512, 640, 768, 1280, 1792, 1920, 2048, 2560


def _dot(a, b):
    return lax.dot_general(a, b, (((1,), (0,)), ((), ())), preferred_element_type=F32)


def _dot_nt(a, b):
    return lax.dot_general(a, b, (((1,), (1,)), ((), ())), preferred_element_type=F32)


def _silu(x):
    return x / (1.0 + jnp.exp(-x))


def _rms(x, g, n):
    ss = jnp.sum(x * x, axis=-1, keepdims=True)
    return x * lax.rsqrt(ss / n + EPS) * g


def _rms_halves(x, g2, lo):
    x2 = x * x
    s_lo = jnp.sum(jnp.where(lo, x2, 0.0), axis=-1, keepdims=True)
    s_hi = jnp.sum(jnp.where(lo, 0.0, x2), axis=-1, keepdims=True)
    r = jnp.where(lo, lax.rsqrt(s_lo / HEAD_DIM + EPS), lax.rsqrt(s_hi / HEAD_DIM + EPS))
    return x * r * g2


def _modulate(x, g, m):
    d = x.shape[-1]
    xn = x * lax.rsqrt(jnp.mean(x * x, axis=-1, keepdims=True) + EPS) * g
    return xn * (1.0 + m[:, d:2 * d]) + m[:, :d], m[:, 2 * d:]


def _split_lanes(x):
    hi = x.astype(BF16)
    lo = (x - hi.astype(F32)).astype(BF16)
    return jnp.concatenate([hi, lo], axis=1)


def _lane_matrix2(entries):
    i = lax.broadcasted_iota(jnp.int32, (LANES, LANES), 0)
    j = lax.broadcasted_iota(jnp.int32, (LANES, LANES), 1)
    m = entries(i, j).astype(BF16)
    return jnp.concatenate([m, m], axis=0)


def _rope_matrix2(rot_dim, period, start):
    half = rot_dim // 2

    def entries(i, j):
        pos = jnp.bitwise_and(j, period - 1) - start
        neg = (pos >= 0) & (pos < half) & (i == j + half)
        plus = (pos >= half) & (pos < rot_dim) & (i == j - half)
        return jnp.where(neg, -1.0, jnp.where(plus, 1.0, 0.0))

    return _lane_matrix2(entries)


def _swap_matrix2():
    return _lane_matrix2(lambda i, j: jnp.where(i == jnp.bitwise_xor(j, HEAD_DIM), 1.0, 0.0))


def _lane_mix(x, m2):
    return _dot(_split_lanes(x), m2)


def _with_ones(v, transposed=False):
    if transposed:
        return jnp.concatenate([v, jnp.ones((LANES, v.shape[1]), v.dtype)], axis=0)
    return jnp.concatenate([v, jnp.ones((v.shape[0], LANES), v.dtype)], axis=1)


def _attend(parts, sink=None):
    mx = None
    for s, _, _ in parts:
        pm = jnp.max(s, axis=-1, keepdims=True)
        mx = pm if mx is None else jnp.maximum(mx, pm)
    if sink is not None:
        mx = jnp.maximum(mx, sink)
    acc, den = None, None
    for s, v, v_t in parts:
        e = jnp.exp2((s - mx).astype(BF16))
        po = (_dot_nt if v_t else _dot)(e, v)
        acc = po if acc is None else acc + po
        if po.shape[1] == LANES:
            ps = jnp.sum(e.astype(F32), axis=-1, keepdims=True)
            den = ps if den is None else den + ps
    if den is None:
        den = acc[:, LANES:]
    if sink is not None:
        den = den + jnp.exp2(sink - mx)
    return acc[:, :LANES] * (1.0 / den)


def _lane_lo():
    return lax.broadcasted_iota(jnp.int32, (1, LANES), 1) < HEAD_DIM


def _store_pair_transposed(ref, bi, p, x):
    xt = x.T
    ref[bi, 0, 2 * p] = xt[:HEAD_DIM]
    ref[bi, 0, 2 * p + 1] = xt[HEAD_DIM:]


def _rope_key_slab(win_ref):
    d = win_ref.shape[1]
    return jnp.concatenate([jnp.zeros((NOPE_A, d), BF16), win_ref[E_KROPE:E_GA, :],
                            jnp.zeros((LANES - QK_A, d), BF16)], axis=0)


def _swap_halves(a):
    return jnp.concatenate([a[HEAD_DIM:], a[:HEAD_DIM]], axis=0)


def _mod_step(n_cond, is_first, bias_row, c_ref, w_ref, o_ref):
    @pl.when(is_first)
    def _():
        o_ref[:n_cond, :] = jnp.broadcast_to(bias_row, (n_cond, o_ref.shape[1]))
        o_ref[n_cond:, :] = jnp.zeros((o_ref.shape[0] - n_cond, o_ref.shape[1]), F32)

    s = _silu(c_ref[...])
    cols = [jnp.broadcast_to(s[:, r:r + 1], (s.shape[0], LANES)) for r in range(n_cond)]
    for t in range(w_ref.shape[1] // LANES):
        sl = slice(t * LANES, (t + 1) * LANES)
        w = w_ref[:, sl]
        for r in range(n_cond):
            o_ref[r:r + 1, sl] += jnp.sum(w * cols[r], axis=0, keepdims=True)


def _mla_keys(cb, kr, wkk_ref, wkv_ref, kg, rope=None):
    kk = _dot(cb, wkk_ref[...])
    keys = []
    for h in range(N_HEADS):
        k = _rms(kk[:, h * LANES:(h + 1) * LANES] + kr, kg, QK_A)
        if rope is not None:
            k = rope(k)
        keys.append(k.astype(BF16))
    return keys, _dot(cb, wkv_ref[...]).astype(BF16)


def _p0_kernel(n_cond, x_ref, m_ref, ng_ref, win_ref, qag_ref, wq_ref, kvag_ref, wkk_ref, wkv_ref, qg_ref, kg_ref,
               naqg_ref, nakg_ref, wout_ref, ct_ref, wm_ref, bm_ref, wio_ref, woo_ref,
               xo_ref, ckv_ref, krope_ref, nak_ref, nav_ref, mo_ref, wino_ref, wouto_ref, y_scr):
    _mod_step(n_cond, pl.program_id(0) == 0, bm_ref[1:2, :], ct_ref, wm_ref, mo_ref)
    wino_ref[...] = wio_ref[0].astype(BF16)
    wouto_ref[...] = woo_ref[0].astype(BF16)

    nbs = x_ref.shape[0]
    x = x_ref[...].reshape(nbs * SEQ, D_MODEL)
    h, gate = _modulate(x, ng_ref[...], m_ref[0:1, :])
    hb = h.astype(BF16)
    lo = _lane_lo()
    hi = jnp.logical_not(lo)
    rows = [slice(bi * SEQ, (bi + 1) * SEQ) for bi in range(nbs)]

    qln = _rms(_dot_nt(hb, win_ref[E_QLAT:E_CKV, :]), qag_ref[...], Q_RANK).astype(BF16)
    q_all = _dot(qln, wq_ref[...])
    ckv_n = _rms(_dot_nt(hb, win_ref[E_CKV:E_KROPE, :]), kvag_ref[...], KV_RANK)
    kr = _dot_nt(hb, _rope_key_slab(win_ref))
    for bi, rs in enumerate(rows):
        ckv_ref[bi, 0] = ckv_n[rs]
        krope_ref[bi, 0] = kr[rs].T[NOPE_A:QK_A]
    keys, vals = _mla_keys(ckv_n.astype(BF16), kr, wkk_ref, wkv_ref, kg_ref[...])
    qg = qg_ref[...] * (QK_A ** -0.5 * LOG2E)

    ga = _dot_nt(hb, win_ref[E_GA:E_QB, :])
    zq = _dot_nt(hb, win_ref[E_QB:E_KB, :])
    zk = _dot_nt(hb, win_ref[E_KB:E_VB, :])
    zv = _dot_nt(hb, win_ref[E_VB:E_GB, :])
    gb = _dot_nt(hb, win_ref[E_GB:E_END, :])
    naqg = naqg_ref[...] * (HEAD_DIM ** -0.5 * LOG2E)

    for p in range(N_PAIRS):
        sl = slice(p * LANES, (p + 1) * LANES)
        ys = slice(4 * LANES + p * LANES, 4 * LANES + (p + 1) * LANES)
        qhs = [_rms(q_all[:, hh * LANES:(hh + 1) * LANES], qg, QK_A).astype(BF16) for hh in (2 * p, 2 * p + 1)]
        qb = _rms_halves(zq[:, sl], naqg, lo)
        kb = _rms_halves(zk[:, sl], nakg_ref[...], lo)
        vb = zv[:, sl]
        kbb, vbb = kb.astype(BF16), vb.astype(BF16)
        va = vals[:, sl]
        qms = [jnp.where(keep, qb, 0.0).astype(BF16) for keep in (lo, hi)]
        for bi, rs in enumerate(rows):
            o2 = [_attend([(_dot_nt(qhs[i][rs], keys[2 * p + i][rs]), va[rs], False)]) for i in (0, 1)]
            y_scr[rs, sl] = (jnp.where(lo, o2[0], o2[1]) * _silu(ga[rs, sl])).astype(BF16)
            _store_pair_transposed(nak_ref, bi, p, kb[rs])
            _store_pair_transposed(nav_ref, bi, p, vb[rs])
            o2 = [_attend([(_dot_nt(qms[i][rs], kbb[rs]), vbb[rs], False)]) for i in (0, 1)]
            y_scr[rs, ys] = (jnp.where(lo, o2[0], o2[1]) * _silu(gb[rs, sl])).astype(BF16)

    xo_ref[...] = (x + gate * _dot(y_scr[...], wout_ref[...])).reshape(nbs, SEQ, D_MODEL)


def _full(shape):
    n = len(shape)
    return pl.BlockSpec(shape, lambda *_: (0,) * n, pipeline_mode=pl.Buffered(1))


class _Row(NamedTuple):
    table: jax.Array
    row: int


def _spec(a):
    if isinstance(a, _Row):
        idx = (a.row,) + (0,) * (a.table.ndim - 1)
        return pl.BlockSpec((None,) + a.table.shape[1:], lambda *_: idx, pipeline_mode=pl.Buffered(1))
    return _full(a.shape)


def _arr(a):
    return a.table if isinstance(a, _Row) else a


def _prompt_even(x, m, ng, win, qag, wq, kvag, wkk, wkv, qg, kg, naqg, nakg, wout,
                 cond_t, n_cond, w_mod, b_mod, w_in_odd, w_out_odd):
    nb = x.shape[0]
    nbs = PROMPT_BATCHES_PER_STEP
    steps = nb // nbs
    assert nb % nbs == 0 and D_MODEL % (BF16_SUBLANES * steps) == 0
    tr = D_MODEL // steps
    ins = (m, ng, win, qag, wq, kvag, wkk, wkv, qg, kg, naqg, nakg, wout)
    return pl.pallas_call(
        functools.partial(_p0_kernel, n_cond),
        grid=(steps,),
        in_specs=[pl.BlockSpec((nbs, SEQ, D_MODEL), lambda b: (b, 0, 0))] + [_spec(a) for a in ins]
        + [pl.BlockSpec((tr, 8), lambda b: (b, 0)),
           pl.BlockSpec((None, tr, 3 * D_MODEL), lambda b: (1, b, 0)),
           _full(b_mod.shape),
           pl.BlockSpec((1, tr, O_END), lambda b: (0, b, 0)),
           pl.BlockSpec((1, tr, D_MODEL), lambda b: (0, b, 0))],
        out_specs=[pl.BlockSpec((nbs, SEQ, D_MODEL), lambda b: (b, 0, 0)),
                   pl.BlockSpec((nbs, 1, SEQ, KV_RANK), lambda b: (b, 0, 0, 0)),
                   pl.BlockSpec((nbs, 1, ROPE_A, SEQ), lambda b: (b, 0, 0, 0)),
                   pl.BlockSpec((nbs, 1, N_HEADS, HEAD_DIM, SEQ), lambda b: (b, 0, 0, 0, 0)),
                   pl.BlockSpec((nbs, 1, N_HEADS, HEAD_DIM, SEQ), lambda b: (b, 0, 0, 0, 0)),
                   pl.BlockSpec((8, 3 * D_MODEL), lambda b: (0, 0)),
                   pl.BlockSpec((tr, O_END), lambda b: (b, 0)),
                   pl.BlockSpec((tr, D_MODEL), lambda b: (b, 0))],
        out_shape=[jax.ShapeDtypeStruct((nb, SEQ, D_MODEL), F32),
                   jax.ShapeDtypeStruct((nb, 1, SEQ, KV_RANK), F32),
                   jax.ShapeDtypeStruct((nb, 1, ROPE_A, SEQ), F32),
                   jax.ShapeDtypeStruct((nb, 1, N_HEADS, HEAD_DIM, SEQ), F32),
                   jax.ShapeDtypeStruct((nb, 1, N_HEADS, HEAD_DIM, SEQ), F32),
                   jax.ShapeDtypeStruct((8, 3 * D_MODEL), F32),
                   jax.ShapeDtypeStruct((D_MODEL, O_END), BF16),
                   jax.ShapeDtypeStruct((D_MODEL, D_MODEL), BF16)],
        scratch_shapes=[pltpu.VMEM((nbs * SEQ, D_MODEL), BF16)],
        compiler_params=pltpu.CompilerParams(dimension_semantics=("arbitrary",), vmem_limit_bytes=VMEM_LIMIT),
        name="prompt_even",
    )(x, *map(_arr, ins), cond_t, w_mod, b_mod, w_in_odd, w_out_odd)


def _gqa_pair_operands(k, v, kg2, lo):
    kn = _rms_halves(k, kg2, lo)
    return kn, (kn.astype(BF16), pltpu.roll(kn, HEAD_DIM, 1).astype(BF16)), \
        (_with_ones(v.astype(BF16)), _with_ones(pltpu.roll(v, HEAD_DIM, 1).astype(BF16)))


def _p1_kernel(sink_ref, x_ref, m_ref, ng_ref, win_ref, gqg_ref, gkg_ref, sqg_ref, skg_ref, wout_ref,
               xo_ref, gk_ref, gv_ref, sk_ref, sv_ref, y_scr):
    nbs = x_ref.shape[0]
    x = x_ref[...].reshape(nbs * SEQ, D_MODEL)
    h, gate = _modulate(x, ng_ref[...], m_ref[0:1, :])
    hb = h.astype(BF16)
    lo = _lane_lo()
    hi = jnp.logical_not(lo)
    sc = HEAD_DIM ** -0.5 * LOG2E
    rows = [slice(bi * SEQ, (bi + 1) * SEQ) for bi in range(nbs)]

    branches = ((O_QC, O_KC, O_VC, O_GC, gqg_ref, gkg_ref, gk_ref, gv_ref, False, 0),
                (O_QD, O_KD, O_VD, O_GD, sqg_ref, skg_ref, sk_ref, sv_ref, True, 4 * LANES))
    for oq, ok, ov, og, qg_ref, kg_ref, ck_ref, cv_ref, has_sink, yoff in branches:
        zq = _dot(hb, win_ref[:, oq:oq + 4 * LANES])
        zkv = _dot(hb, win_ref[:, ok:ok + 2 * LANES])
        zg = _dot(hb, win_ref[:, og:og + 4 * LANES])
        v = zkv[:, LANES:]
        kn, ks, vs = _gqa_pair_operands(zkv[:, :LANES], v, kg_ref[...], lo)
        for bi, rs in enumerate(rows):
            _store_pair_transposed(ck_ref, bi, 0, kn[rs])
            _store_pair_transposed(cv_ref, bi, 0, v[rs])
        qg = qg_ref[...] * sc
        for p in range(N_PAIRS):
            sl = slice(p * LANES, (p + 1) * LANES)
            qn = _rms_halves(zq[:, sl], qg, lo)
            qms = [jnp.where(keep, qn, 0.0).astype(BF16) for keep in (lo, hi)]
            kv = p // 2
            for bi, rs in enumerate(rows):
                o2 = []
                for half in (0, 1):
                    swap = 0 if kv == half else 1
                    sink = sink_ref[2 * p + half] * LOG2E if has_sink else None
                    o2.append(_attend([(_dot_nt(qms[half][rs], ks[swap][rs]), vs[swap][rs], False)], sink))
                o = jnp.where(lo, o2[0], o2[1])
                y_scr[rs, yoff + p * LANES:yoff + (p + 1) * LANES] = (o * _silu(zg[rs, sl])).astype(BF16)

    xo_ref[...] = (x + gate * _dot(y_scr[...], wout_ref[...])).reshape(nbs, SEQ, D_MODEL)


def _prompt_odd(sink, x, m, ng, win, gqg, gkg, sqg, skg, wout):
    nb = x.shape[0]
    nbs = PROMPT_ODD_BATCHES_PER_STEP
    assert nb % nbs == 0
    ins = (m, ng, win, gqg, gkg, sqg, skg, wout)
    cache_spec = pl.BlockSpec((nbs, 1, 2, HEAD_DIM, SEQ), lambda b: (b, 0, 0, 0, 0))
    cache_shape = jax.ShapeDtypeStruct((nb, 1, 2, HEAD_DIM, SEQ), F32)
    return pl.pallas_call(
        _p1_kernel,
        grid=(nb // nbs,),
        in_specs=[pl.BlockSpec(memory_space=pltpu.SMEM),
                  pl.BlockSpec((nbs, SEQ, D_MODEL), lambda b: (b, 0, 0))] + [_spec(a) for a in ins],
        out_specs=[pl.BlockSpec((nbs, SEQ, D_MODEL), lambda b: (b, 0, 0))] + [cache_spec] * 4,
        out_shape=[jax.ShapeDtypeStruct((nb, SEQ, D_MODEL), F32)] + [cache_shape] * 4,
        scratch_shapes=[pltpu.VMEM((nbs * SEQ, D_MODEL), BF16)],
        compiler_params=pltpu.CompilerParams(dimension_semantics=("arbitrary",), vmem_limit_bytes=VMEM_LIMIT),
        name="prompt_odd",
    )(sink, x, *map(_arr, ins))


def _s0a_kernel(x_ref, m_ref, ng_ref, win_ref, qag_ref, wq_ref, kvag_ref, wkk_ref, wkv_ref, qg_ref, kg_ref,
                naqg_ref, nakg_ref, cos_ref, sin_ref,
                qa_ref, ka_ref, va_ref, qb_ref, kb_ref, vb_ref, g_ref):
    b = pl.program_id(0)
    lo = _lane_lo()
    partner = _rope_matrix2(ROPE_A, LANES, NOPE_A)
    hb = _modulate(x_ref[0], ng_ref[...], m_ref[pl.ds(1 + b, 1), :])[0].astype(BF16)
    cos, sin = cos_ref[...], sin_ref[...]

    qln = _rms(_dot_nt(hb, win_ref[E_QLAT:E_CKV, :]), qag_ref[...], Q_RANK).astype(BF16)
    q_all = _dot(qln, wq_ref[...])
    ckv_n = _rms(_dot_nt(hb, win_ref[E_CKV:E_KROPE, :]), kvag_ref[...], KV_RANK)
    kr = _dot_nt(hb, _rope_key_slab(win_ref))
    cb = ckv_n.astype(BF16)
    kk = _dot(cb, wkk_ref[...])
    va_ref[0] = _dot(cb, wkv_ref[...]).astype(BF16)
    zq = _dot_nt(hb, win_ref[E_QB:E_KB, :])
    zk = _dot_nt(hb, win_ref[E_KB:E_VB, :])
    vb_ref[0] = _dot_nt(hb, win_ref[E_VB:E_GB, :]).astype(BF16)
    g_ref[0, :, 0:4 * LANES] = _silu(_dot_nt(hb, win_ref[E_GA:E_QB, :]))
    g_ref[0, :, 4 * LANES:8 * LANES] = _silu(_dot_nt(hb, win_ref[E_GB:E_END, :]))

    qg = qg_ref[...] * (QK_A ** -0.5 * LOG2E)
    kg = kg_ref[...]
    k_partner = _lane_mix(kr * kg, partner) * sin
    for hh in range(N_HEADS):
        sl = slice(hh * LANES, (hh + 1) * LANES)
        qn = _rms(q_all[:, sl], qg, QK_A)
        qa_ref[0, :, sl] = (qn * cos + _lane_mix(qn, partner) * sin).astype(BF16)
        k_raw = kk[:, sl] + kr
        k_inv = lax.rsqrt(jnp.sum(k_raw * k_raw, axis=-1, keepdims=True) / QK_A + EPS)
        ka_ref[0, :, sl] = ((k_raw * kg * cos + k_partner) * k_inv).astype(BF16)
    naqg = naqg_ref[...] * (HEAD_DIM ** -0.5 * LOG2E)
    for p in range(N_PAIRS):
        sl = slice(p * LANES, (p + 1) * LANES)
        qb_ref[0, :, sl] = _rms_halves(zq[:, sl], naqg, lo).astype(BF16)
        kb_ref[0, :, sl] = _rms_halves(zk[:, sl], nakg_ref[...], lo).astype(BF16)


def _sample_even_proj(x, m, ng, win, qag, wq, kvag, wkk, wkv, qg, kg, naqg, nakg, cos, sin):
    nb, s, _ = x.shape
    nq = s // PROJ_BLOCK
    ins = (m, ng, win, qag, wq, kvag, wkk, wkv, qg, kg, naqg, nakg)
    tab = pl.BlockSpec((PROJ_BLOCK, LANES), lambda b, j: (j, 0))

    def blk(w):
        return pl.BlockSpec((1, PROJ_BLOCK, w), lambda b, j: (b, j, 0))

    def shp(w, dt):
        return jax.ShapeDtypeStruct((nb, s, w), dt)

    return pl.pallas_call(
        _s0a_kernel,
        grid=(nb, nq),
        in_specs=[blk(D_MODEL)] + [_spec(a) for a in ins] + [tab, tab],
        out_specs=[blk(1024), blk(1024), blk(512), blk(512), blk(512), blk(512), blk(1024)],
        out_shape=[shp(1024, BF16), shp(1024, BF16), shp(512, BF16), shp(512, BF16), shp(512, BF16),
                   shp(512, BF16), shp(1024, F32)],
        compiler_params=pltpu.CompilerParams(dimension_semantics=("arbitrary", "arbitrary"),
                                             vmem_limit_bytes=VMEM_LIMIT),
        name="sample_even_proj",
    )(x, *map(_arr, ins), cos, sin)


def _build_bias_table(rpb_ref, tile_scr, tab_ref):
    qc = lax.broadcasted_iota(jnp.int32, (GRID_W, LANES), 0)
    lane = lax.broadcasted_iota(jnp.int32, (GRID_W, LANES), 1)
    kc = jnp.bitwise_and(lane, GRID_W - 1)
    lo = lane < GRID_W
    diff = kc - qc + (NA_COLS - 1)
    cs = jnp.clip(qc - NA_COLS // 2, 0, GRID_W - NA_COLS)
    valid = (kc >= cs) & (kc < cs + NA_COLS)
    tab_ref[...] = jnp.zeros(tab_ref.shape, F32)
    tile_scr[RPB_ROWS] = jnp.zeros((GRID_W, LANES), F32)

    def per_head(h, carry):
        for dr in range(RPB_ROWS):
            t = jnp.zeros((GRID_W, LANES), F32)
            for dc in range(RPB_COLS):
                t = jnp.where(diff == dc, rpb_ref[(h * RPB_ROWS + dr) * RPB_COLS + dc], t)
            tile_scr[dr] = jnp.where(valid, t * LOG2E, NEG_INF)
        for c in range(NA_ROWS // 2, NA_ROWS // 2 + NA_ROWS):
            d0 = 2 * c - NA_ROWS
            tab_ref[0, h, c] = jnp.where(lo, tile_scr[d0], tile_scr[d0 + 1])
            tab_ref[1, h, c] = jnp.where(lo, tile_scr[d0 - 1 if d0 > 0 else RPB_ROWS], tile_scr[d0])
        return carry

    lax.fori_loop(0, N_HEADS, per_head, 0)


def _s0b_kernel(rpb_ref, x_ref, m_ref, qa_ref, ka_ref, va_ref, qb_ref, kb_ref, vb_ref, g_ref,
                cckv_ref, ckr_ref, cnk_ref, cnv_ref, wkk_ref, wkv_ref, kg_ref, wout_ref,
                xo_ref, kca_scr, vca_scr, tile_scr, tab_scr, y_scr):
    b = pl.program_id(0)
    j = pl.program_id(1)
    lo = _lane_lo()
    n_lat = ka_ref.shape[1]

    @pl.when((b == 0) & (j == 0))
    def _():
        _build_bias_table(rpb_ref, tile_scr, tab_scr)

    @pl.when(j == 0)
    def _():
        kr_t = jnp.concatenate([jnp.zeros((NOPE_A, PAST_LEN), F32), ckr_ref[0],
                                jnp.zeros((LANES - QK_A, PAST_LEN), F32)], axis=0)
        keys, vals = _mla_keys(cckv_ref[0].astype(BF16), kr_t.T, wkk_ref, wkv_ref, kg_ref[...])
        for hh in range(N_HEADS):
            kca_scr[:, hh * LANES:(hh + 1) * LANES] = keys[hh]
        vca_scr[...] = vals

    kidx = lax.broadcasted_iota(jnp.int32, (1, n_lat), 1)
    for p in range(N_PAIRS):
        sl = slice(p * LANES, (p + 1) * LANES)
        o2 = []
        va = _with_ones(va_ref[0, :, sl])
        vca = _with_ones(vca_scr[:, sl])
        for hh in (2 * p, 2 * p + 1):
            hs = slice(hh * LANES, (hh + 1) * LANES)
            q = qa_ref[0, :, hs]
            o2.append(_attend([(_dot_nt(q, ka_ref[0, :, hs]), va, False),
                               (_dot_nt(q, kca_scr[:, hs]), vca, False)]))
        oa = jnp.where(lo, o2[0], o2[1])
        y_scr[:, sl] = (oa * g_ref[0, :, sl]).astype(BF16)

        qb = qb_ref[0, :, sl]
        kb = kb_ref[0, :, sl]
        vb = _with_ones(vb_ref[0, :, sl])
        kcb = cnk_ref[0, sl, :].astype(BF16)
        vcb = _with_ones(cnv_ref[0, sl, :].astype(BF16), transposed=True)
        o2 = []
        for half in (0, 1):
            head = 2 * p + half
            qm = jnp.where(lo if half == 0 else jnp.logical_not(lo), qb, jnp.zeros_like(qb))
            s_lat = _dot_nt(qm, kb)
            rows = []
            for local in range(NA_Q_BLOCK // GRID_W):
                qr = j * (NA_Q_BLOCK // GRID_W) + local
                par = 0 if local % 2 == 1 else 1
                c0 = (RPB_ROWS + par - local) // 2 - (NA_Q_BLOCK // GRID_W // 2) * j
                bias = jnp.concatenate([tab_scr[par, head, c0 + t] for t in range(n_lat // LANES)], axis=1)
                r0 = jnp.clip(qr - NA_ROWS // 2, 0, n_lat // GRID_W - NA_ROWS) * GRID_W
                ok = (kidx >= r0) & (kidx < r0 + NA_ROWS * GRID_W)
                bias = bias + jnp.where(ok, 0.0, NEG_INF)
                rows.append(s_lat[local * GRID_W:(local + 1) * GRID_W] + bias)
            s_lat = jnp.concatenate(rows, axis=0)
            o2.append(_attend([(s_lat, vb, False), (_dot(qm, kcb), vcb, True)]))
        ob = jnp.where(lo, o2[0], o2[1])
        ys = slice(4 * LANES + p * LANES, 4 * LANES + (p + 1) * LANES)
        y_scr[:, ys] = (ob * g_ref[0, :, ys]).astype(BF16)

    d = x_ref.shape[-1]
    gate = m_ref[pl.ds(1 + b, 1), 2 * d:]
    xo_ref[0] = x_ref[0] + gate * _dot(y_scr[...], wout_ref[...])


def _sample_even_attn(rpb, x, m, qa, ka, va, qb, kb, vb, g, cckv, ckr, cnk, cnv, wkk, wkv, kg, wout):
    nb, s, _ = x.shape
    nq = s // NA_Q_BLOCK

    def blk(w):
        return pl.BlockSpec((1, NA_Q_BLOCK, w), lambda b, j: (b, j, 0))

    def per_batch(a):
        return pl.BlockSpec((1,) + a.shape[1:], lambda b, j: (b, 0, 0))

    return pl.pallas_call(
        _s0b_kernel,
        grid=(nb, nq),
        in_specs=[pl.BlockSpec(memory_space=pltpu.SMEM), blk(D_MODEL), _spec(m),
                  blk(1024), per_batch(ka), per_batch(va), blk(512), per_batch(kb), per_batch(vb), blk(1024),
                  per_batch(cckv), per_batch(ckr), per_batch(cnk), per_batch(cnv),
                  _full(wkk.shape), _full(wkv.shape), _spec(kg), _full(wout.shape)],
        out_specs=blk(D_MODEL),
        out_shape=jax.ShapeDtypeStruct(x.shape, F32),
        scratch_shapes=[pltpu.VMEM((PAST_LEN, N_HEADS * LANES), BF16),
                        pltpu.VMEM((PAST_LEN, N_HEADS * HEAD_DIM), BF16),
                        pltpu.VMEM((RPB_ROWS + 1, GRID_W, LANES), F32),
                        pltpu.VMEM((2, N_HEADS, BIAS_CHUNKS, GRID_W, LANES), F32),
                        pltpu.VMEM((NA_Q_BLOCK, D_MODEL), BF16)],
        compiler_params=pltpu.CompilerParams(dimension_semantics=("arbitrary", "arbitrary"),
                                             vmem_limit_bytes=VMEM_LIMIT),
        name="sample_even_attn",
    )(rpb, x, _arr(m), qa, ka, va, qb, kb, vb, g, cckv, ckr, cnk, cnv, wkk, wkv, _arr(kg), wout)


def _s1a_kernel(x_ref, m_ref, ng_ref, win_ref, gqg_ref, gkg_ref, sqg_ref, skg_ref, cos_ref, sin_ref,
                qc_ref, kc_ref, vc_ref, qd_ref, kd_ref, vd_ref, g_ref):
    b = pl.program_id(0)
    lo = _lane_lo()
    partner = _rope_matrix2(HEAD_DIM, HEAD_DIM, 0)
    swap = _swap_matrix2()[:LANES]
    hb = _modulate(x_ref[0], ng_ref[...], m_ref[pl.ds(1 + b, 1), :])[0].astype(BF16)
    cos, sin = cos_ref[...], sin_ref[...]
    sc = HEAD_DIM ** -0.5 * LOG2E

    def rope(t):
        return t * cos + _lane_mix(t, partner) * sin

    branches = ((O_QC, O_KC, O_GC, gqg_ref, gkg_ref, qc_ref, kc_ref, vc_ref, 0),
                (O_QD, O_KD, O_GD, sqg_ref, skg_ref, qd_ref, kd_ref, vd_ref, 4 * LANES))
    for oq, ok, og, qg_ref, kg_ref, q_out, k_out, v_out, goff in branches:
        zq = _dot(hb, win_ref[:, oq:oq + 4 * LANES])
        zkv = _dot(hb, win_ref[:, ok:ok + 2 * LANES])
        qg = qg_ref[...] * sc
        for p in range(N_PAIRS):
            sl = slice(p * LANES, (p + 1) * LANES)
            q_out[0, :, sl] = rope(_rms_halves(zq[:, sl], qg, lo)).astype(BF16)
        kn = rope(_rms_halves(zkv[:, :LANES], kg_ref[...], lo))
        v = zkv[:, LANES:]
        for out, val in ((k_out, kn.astype(BF16)), (v_out, v.astype(BF16))):
            out[0, :, 0:LANES] = val
            out[0, :, LANES:2 * LANES] = _dot(val, swap).astype(BF16)
        g_ref[0, :, goff:goff + 4 * LANES] = _silu(_dot(hb, win_ref[:, og:og + 4 * LANES]))


def _sample_odd_proj(x, m, ng, win, gqg, gkg, sqg, skg, cos, sin):
    nb, s, _ = x.shape
    nq = s // PROJ_BLOCK
    ins = (m, ng, win, gqg, gkg, sqg, skg)
    tab = pl.BlockSpec((PROJ_BLOCK, LANES), lambda b, j: (j, 0))

    def blk(w):
        return pl.BlockSpec((1, PROJ_BLOCK, w), lambda b, j: (b, j, 0))

    def shp(w, dt):
        return jax.ShapeDtypeStruct((nb, s, w), dt)

    return pl.pallas_call(
        _s1a_kernel,
        grid=(nb, nq),
        in_specs=[blk(D_MODEL)] + [_spec(a) for a in ins] + [tab, tab],
        out_specs=[blk(512), blk(256), blk(256), blk(512), blk(256), blk(256), blk(1024)],
        out_shape=[shp(512, BF16), shp(256, BF16), shp(256, BF16), shp(512, BF16), shp(256, BF16),
                   shp(256, BF16), shp(1024, F32)],
        compiler_params=pltpu.CompilerParams(dimension_semantics=("arbitrary", "arbitrary"),
                                             vmem_limit_bytes=VMEM_LIMIT),
        name="sample_odd_proj",
    )(x, *map(_arr, ins), cos, sin)


def _s1b_kernel(sink_ref, x_ref, m_ref, qc_ref, kc_ref, vc_ref, qd_ref, kd_ref, vd_ref, g_ref,
                cgk_ref, cgv_ref, csk_ref, csv_ref, wout_ref, xo_ref, y_scr):
    b = pl.program_id(0)
    j = pl.program_id(1)
    lo = _lane_lo()
    n_lat = kc_ref.shape[1]
    win_keys = Q_BLOCK + 2 * SWA_HALF

    def ctx_pair(ref, values=False):
        a = ref[0].astype(BF16)
        pair = (a, _swap_halves(a))
        return tuple(_with_ones(t, transposed=True) for t in pair) if values else pair

    cgk, cgv, csk, csv = ctx_pair(cgk_ref), ctx_pair(cgv_ref, True), ctx_pair(csk_ref), ctx_pair(csv_ref, True)
    vcs = [_with_ones(vc_ref[0, :, w * LANES:(w + 1) * LANES]) for w in (0, 1)]

    ks = pl.multiple_of(jnp.clip(j * Q_BLOCK - SWA_HALF, 0, n_lat - win_keys), SWA_HALF)
    qpos = j * Q_BLOCK + lax.broadcasted_iota(jnp.int32, (Q_BLOCK, win_keys), 0)
    kpos = ks + lax.broadcasted_iota(jnp.int32, (Q_BLOCK, win_keys), 1)
    band = jnp.abs(qpos - kpos) <= SWA_HALF
    vds = [_with_ones(vd_ref[0, pl.ds(ks, win_keys), w * LANES:(w + 1) * LANES]) for w in (0, 1)]

    for p in range(N_PAIRS):
        sl = slice(p * LANES, (p + 1) * LANES)
        kv = p // 2
        qc = qc_ref[0, :, sl]
        qd = qd_ref[0, :, sl]
        oc2, od2 = [], []
        for half in (0, 1):
            swap = 0 if kv == half else 1
            ws = slice(swap * LANES, (swap + 1) * LANES)
            keep = lo if half == 0 else jnp.logical_not(lo)
            qm = jnp.where(keep, qc, jnp.zeros_like(qc))
            oc2.append(_attend([(_dot_nt(qm, kc_ref[0, :, ws]), vcs[swap], False),
                                (_dot(qm, cgk[swap]), cgv[swap], True)]))
            qm = jnp.where(keep, qd, jnp.zeros_like(qd))
            s_loc = jnp.where(band, _dot_nt(qm, kd_ref[0, pl.ds(ks, win_keys), ws]), NEG_INF)
            od2.append(_attend([(s_loc, vds[swap], False),
                                (_dot(qm, csk[swap]), csv[swap], True)], sink_ref[2 * p + half] * LOG2E))
        y_scr[:, sl] = (jnp.where(lo, oc2[0], oc2[1]) * g_ref[0, :, sl]).astype(BF16)
        ys = slice(4 * LANES + p * LANES, 4 * LANES + (p + 1) * LANES)
        y_scr[:, ys] = (jnp.where(lo, od2[0], od2[1]) * g_ref[0, :, ys]).astype(BF16)

    d = x_ref.shape[-1]
    gate = m_ref[pl.ds(1 + b, 1), 2 * d:]
    xo_ref[0] = x_ref[0] + gate * _dot(y_scr[...], wout_ref[...])


def _sample_odd_attn(sink, x, m, qc, kc, vc, qd, kd, vd, g, cgk, cgv, csk, csv, wout):
    nb, s, _ = x.shape
    nq = s // Q_BLOCK

    def blk(w):
        return pl.BlockSpec((1, Q_BLOCK, w), lambda b, j: (b, j, 0))

    def per_batch(a):
        return pl.BlockSpec((1,) + a.shape[1:], lambda b, j: (b, 0, 0))

    return pl.pallas_call(
        _s1b_kernel,
        grid=(nb, nq),
        in_specs=[pl.BlockSpec(memory_space=pltpu.SMEM), blk(D_MODEL), _spec(m),
                  blk(512), per_batch(kc), per_batch(vc), blk(512), per_batch(kd), per_batch(vd), blk(1024),
                  per_batch(cgk), per_batch(cgv), per_batch(csk), per_batch(csv), _full(wout.shape)],
        out_specs=blk(D_MODEL),
        out_shape=jax.ShapeDtypeStruct(x.shape, F32),
        scratch_shapes=[pltpu.VMEM((Q_BLOCK, D_MODEL), BF16)],
        compiler_params=pltpu.CompilerParams(dimension_semantics=("arbitrary", "arbitrary"),
                                             vmem_limit_bytes=VMEM_LIMIT),
        name="sample_odd_attn",
    )(sink, x, _arr(m), qc, kc, vc, qd, kd, vd, g, cgk, cgv, csk, csv, wout)


WEIGHT_PREP_STEPS = 8
EVEN_IN_CHUNKS = 6
COND_PREP_ROWS = 256
BF16_SUBLANES = 16


G_MLA_Q, G_MLA_K, G_NA_Q, G_NA_K, G_GQA_Q, G_GQA_K, G_SWA_Q, G_SWA_K, N_GAINS = range(9)


GAIN_WIDTHS = (QK_A, QK_A) + (HEAD_DIM,) * 6


def _cond_prep_kernel(n_cond, ct_ref, wm_ref, bm_ref, ng_ref, gains_ref, mo_ref, gt_ref, ngt_ref):
    _mod_step(n_cond, pl.program_id(0) == 0, bm_ref[0:1, :], ct_ref, wm_ref, mo_ref)
    gt_ref[...] = jnp.zeros(gt_ref.shape, F32)
    start = 0
    for r, w in enumerate(GAIN_WIDTHS):
        g = gains_ref[:, start:start + w]
        start += w
        for off in range(0, LANES - w + 1, w):
            gt_ref[r, :, off:off + w] = g
    for layer in range(ngt_ref.shape[0]):
        ngt_ref[layer] = ng_ref[layer:layer + 1, :]


def _cond_prep(cond_t, n_cond, w_mod, b_mod, norm_g, gains):
    assert len(gains) == N_GAINS and tuple(g.shape[-1] for g in gains) == GAIN_WIDTHS
    gains_row = jnp.concatenate([g.reshape(1, -1) for g in gains], axis=1)
    tk = COND_PREP_ROWS
    return pl.pallas_call(
        functools.partial(_cond_prep_kernel, n_cond),
        grid=(D_MODEL // tk,),
        in_specs=[pl.BlockSpec((tk, 8), lambda k: (k, 0)),
                  pl.BlockSpec((None, tk, 3 * D_MODEL), lambda k: (0, k, 0)),
                  _full(b_mod.shape), _full(norm_g.shape), _full(gains_row.shape)],
        out_specs=[_full((8, 3 * D_MODEL)), _full((N_GAINS, 1, LANES)), _full((norm_g.shape[0], 1, D_MODEL))],
        out_shape=[jax.ShapeDtypeStruct((8, 3 * D_MODEL), F32), jax.ShapeDtypeStruct((N_GAINS, 1, LANES), F32),
                   jax.ShapeDtypeStruct((norm_g.shape[0], 1, D_MODEL), F32)],
        compiler_params=pltpu.CompilerParams(dimension_semantics=("arbitrary",)),
        name="cond_prep",
    )(cond_t, w_mod, b_mod, norm_g, gains_row)


def _weight_prep_kernel(wie_ref, woe_ref, wqu_ref, wkv_ref, win_e_ref, wout_e_ref, wq_ref, wkk_ref, wkvv_ref):
    win_e_ref[...] = wie_ref[...].astype(BF16)
    wout_e_ref[...] = woe_ref[0].astype(BF16)

    wq_ref[...] = jnp.zeros(wq_ref.shape, BF16)
    for h in range(N_HEADS):
        wq_ref[:, h * LANES:h * LANES + QK_A] = wqu_ref[0, :, h * QK_A:(h + 1) * QK_A].astype(BF16)
    lo = _lane_lo()
    for p in range(N_PAIRS):
        a = wkv_ref[0, :, (2 * p) * LANES:(2 * p + 1) * LANES]
        c = wkv_ref[0, :, (2 * p + 1) * LANES:(2 * p + 2) * LANES]
        wkk_ref[:, (2 * p) * LANES:(2 * p + 1) * LANES] = jnp.where(lo, a, 0.0).astype(BF16)
        wkk_ref[:, (2 * p + 1) * LANES:(2 * p + 2) * LANES] = jnp.where(lo, c, 0.0).astype(BF16)
        wkvv_ref[:, p * LANES:(p + 1) * LANES] = jnp.where(lo, pltpu.roll(a, HEAD_DIM, 1), c).astype(BF16)


def _weight_prep(w_in_even_t, w_out_even, w_q_up, w_kv_up):
    n = WEIGHT_PREP_STEPS
    ins = (w_out_even, w_q_up, w_kv_up)
    out_cols = (D_MODEL, N_HEADS * LANES, N_HEADS * LANES, N_HEADS * HEAD_DIM)
    out_rows = (D_MODEL, Q_RANK, KV_RANK, KV_RANK)
    te = E_END // EVEN_IN_CHUNKS
    assert te * EVEN_IN_CHUNKS == E_END and te % BF16_SUBLANES == 0 and EVEN_IN_CHUNKS <= n
    even_spec = pl.BlockSpec((te, D_MODEL), lambda i: (jnp.minimum(i, EVEN_IN_CHUNKS - 1), 0))
    return pl.pallas_call(
        _weight_prep_kernel,
        grid=(n,),
        in_specs=[even_spec] + [pl.BlockSpec((1, a.shape[1] // n, a.shape[2]), lambda i: (0, i, 0)) for a in ins],
        out_specs=[even_spec] + [pl.BlockSpec((r // n, c), lambda i: (i, 0)) for r, c in zip(out_rows, out_cols)],
        out_shape=[jax.ShapeDtypeStruct((E_END, D_MODEL), BF16)]
        + [jax.ShapeDtypeStruct((r, c), BF16) for r, c in zip(out_rows, out_cols)],
        compiler_params=pltpu.CompilerParams(dimension_semantics=("arbitrary",), vmem_limit_bytes=VMEM_LIMIT),
        name="weight_prep",
    )(w_in_even_t, *ins)


def _feature_major(c):
    b, h, l, d = c.shape
    return jnp.swapaxes(c, -1, -2).reshape(b, h * d, l)


def _token_major(c):
    return jnp.swapaxes(c, -1, -2)


def _rope_tables(s, rot_dim, period, start):
    quarter = rot_dim // 4
    t = np.arange(s)
    inv = ROPE_THETA ** (-np.arange(quarter, dtype=np.float64) / quarter)
    row = (t // GRID_W).astype(np.float64)[:, None] * inv
    col = (t % GRID_W).astype(np.float64)[:, None] * inv
    ang = np.concatenate([row, col], axis=-1)
    cos, sin = np.cos(ang), np.sin(ang)
    pre = np.ones((s, start))
    post = np.zeros((s, period - start - rot_dim))
    c = np.concatenate([pre, cos, cos, post], axis=-1)
    sn = np.concatenate([0 * pre, sin, sin, post], axis=-1)
    rep = LANES // period
    return jnp.asarray(np.tile(c, (1, rep)), F32), jnp.asarray(np.tile(sn, (1, rep)), F32)


def kernel(x_prompt, x_sample, cache_mla_ckv, cache_mla_krope, cache_na_k, cache_na_v, cache_gqa_k, cache_gqa_v, cache_swa_k, cache_swa_v, c, c_ctx, norm_g, w_mod, b_mod, w_in_even, mla_qa_g, w_q_up, mla_kva_g, w_kv_up, mla_q_g, mla_k_g, na_q_g, na_k_g, na_rpb, w_out_even, w_in_odd, gqa_q_g, gqa_k_g, swa_q_g, swa_k_g, swa_sink, w_out_odd):
    n_dec = x_sample.shape[0]
    assert w_mod.shape[0] == 2 and n_dec + 1 <= 8

    cond_t = jnp.concatenate([c_ctx[:, None], c.T, jnp.zeros((D_MODEL, 7 - n_dec), F32)], axis=1)
    n_cond = 1 + n_dec
    gains = (mla_q_g, mla_k_g, na_q_g, na_k_g, gqa_q_g, gqa_k_g, swa_q_g, swa_k_g)
    win_e, wout_e, wq, wkk, wkv = _weight_prep(jnp.swapaxes(w_in_even[0], 0, 1), w_out_even, w_q_up, w_kv_up)
    m_even, gt, ngt = _cond_prep(cond_t, n_cond, w_mod, b_mod, norm_g, gains)
    even = (_Row(ngt, 0), win_e, mla_qa_g, wq, mla_kva_g, wkk, wkv,
            _Row(gt, G_MLA_Q), _Row(gt, G_MLA_K), _Row(gt, G_NA_Q), _Row(gt, G_NA_K))
    sink = swa_sink[0].astype(F32)

    xp1, new_ckv, new_krope, new_na_k, new_na_v, m_odd, win_o, wout_o = _prompt_even(
        x_prompt, m_even, *even, wout_e, cond_t, n_cond, w_mod, b_mod, w_in_odd, w_out_odd)
    odd = (_Row(ngt, 1), win_o, _Row(gt, G_GQA_Q), _Row(gt, G_GQA_K), _Row(gt, G_SWA_Q), _Row(gt, G_SWA_K))
    xp2, new_gqa_k, new_gqa_v, new_swa_k, new_swa_v = _prompt_odd(sink, xp1, m_odd, *odd, wout_o)

    cos_e, sin_e = _rope_tables(DEC_SEQ, ROPE_A, LANES, NOPE_A)
    qa, ka, va, qbs, kbs, vbs, g0 = _sample_even_proj(x_sample, m_even, *even, cos_e, sin_e)
    ckr = jnp.swapaxes(cache_mla_krope[:, 0], -1, -2)
    xs1 = _sample_even_attn(na_rpb[0].reshape(-1), x_sample, m_even, qa, ka, va, qbs, kbs, vbs, g0,
                            cache_mla_ckv[:, 0], ckr, _feature_major(cache_na_k[:, 0]),
                            _feature_major(cache_na_v[:, 0]), wkk, wkv, _Row(gt, G_MLA_K), wout_e)
    cos_o, sin_o = _rope_tables(DEC_SEQ, HEAD_DIM, HEAD_DIM, 0)
    qc, kc, vc, qd, kd, vd, g1 = _sample_odd_proj(xs1, m_odd, *odd, cos_o, sin_o)
    xs2 = _sample_odd_attn(sink, xs1, m_odd, qc, kc, vc, qd, kd, vd, g1,
                           _feature_major(cache_gqa_k[:, 0]), _feature_major(cache_gqa_v[:, 0]),
                           _feature_major(cache_swa_k[:, 0]), _feature_major(cache_swa_v[:, 0]), wout_o)

    caches = (new_krope, new_na_k, new_na_v, new_gqa_k, new_gqa_v, new_swa_k, new_swa_v)
    return (xp2, xs2, new_ckv) + tuple(_token_major(c) for c in caches)
```

```python
import functools
from typing import NamedTuple

import jax
import jax.numpy as jnp
import numpy as np
from jax import lax
from jax.experimental import pallas as pl
from jax.experimental.pallas import tpu as pltpu

F32 = jnp.float32
BF16 = jnp.bfloat16

D_MODEL = 1024
SEQ = 256
DEC_SEQ = 1024
PAST_LEN = 256
GRID_W = 64
HEAD_DIM = 64
Q_RANK = 256
KV_RANK = 128
NOPE_A = 64
ROPE_A = 32
QK_A = NOPE_A + ROPE_A
N_HEADS = 8
NA_ROWS = 8
NA_COLS = 16
SWA_HALF = 128
ROPE_THETA = 10000.0
EPS = 1e-6
NEG_INF = -1e30
LOG2E = 1.4426950408889634

LANES = 128
Q_BLOCK = 512
NA_Q_BLOCK = 256
PROJ_BLOCK = 512
PROMPT_BATCHES_PER_STEP = 2
PROMPT_ODD_BATCHES_PER_STEP = 4
N_PAIRS = N_HEADS // 2
RPB_ROWS = 2 * NA_ROWS - 1
RPB_COLS = 2 * NA_COLS - 1
BIAS_CHUNKS = 16
VMEM_LIMIT = 48 * 1024 * 1024

E_QLAT, E_CKV, E_KROPE, E_GA, E_QB, E_KB, E_VB, E_GB, E_END = 0, 256, 384, 416, 928, 1440, 1952, 2464, 2976
O_QC, O_KC, O_VC, O_GC, O_QD, O_KD, O_VD, O_GD, O_END = 0, 512, 640, 768, 1280, 1792, 1920, 2048, 2560


def _dot(a, b):
    return lax.dot_general(a, b, (((1,), (0,)), ((), ())), preferred_element_type=F32)


def _dot_nt(a, b):
    return lax.dot_general(a, b, (((1,), (1,)), ((), ())), preferred_element_type=F32)


def _silu(x):
    return x / (1.0 + jnp.exp(-x))


def _rms(x, g, n):
    ss = jnp.sum(x * x, axis=-1, keepdims=True)
    return x * lax.rsqrt(ss / n + EPS) * g


def _rms_halves(x, g2, lo):
    x2 = x * x
    s_lo = jnp.sum(jnp.where(lo, x2, 0.0), axis=-1, keepdims=True)
    s_hi = jnp.sum(jnp.where(lo, 0.0, x2), axis=-1, keepdims=True)
    r = jnp.where(lo, lax.rsqrt(s_lo / HEAD_DIM + EPS), lax.rsqrt(s_hi / HEAD_DIM + EPS))
    return x * r * g2


def _modulate(x, g, m):
    d = x.shape[-1]
    xn = x * lax.rsqrt(jnp.mean(x * x, axis=-1, keepdims=True) + EPS) * g
    return xn * (1.0 + m[:, d:2 * d]) + m[:, :d], m[:, 2 * d:]


def _split_lanes(x):
    hi = x.astype(BF16)
    lo = (x - hi.astype(F32)).astype(BF16)
    return jnp.concatenate([hi, lo], axis=1)


def _lane_matrix2(entries):
    i = lax.broadcasted_iota(jnp.int32, (LANES, LANES), 0)
    j = lax.broadcasted_iota(jnp.int32, (LANES, LANES), 1)
    m = entries(i, j).astype(BF16)
    return jnp.concatenate([m, m], axis=0)


def _rope_matrix2(rot_dim, period, start):
    half = rot_dim // 2

    def entries(i, j):
        pos = jnp.bitwise_and(j, period - 1) - start
        neg = (pos >= 0) & (pos < half) & (i == j + half)
        plus = (pos >= half) & (pos < rot_dim) & (i == j - half)
        return jnp.where(neg, -1.0, jnp.where(plus, 1.0, 0.0))

    return _lane_matrix2(entries)


def _swap_matrix2():
    return _lane_matrix2(lambda i, j: jnp.where(i == jnp.bitwise_xor(j, HEAD_DIM), 1.0, 0.0))


def _lane_mix(x, m2):
    return _dot(_split_lanes(x), m2)


def _with_ones(v, transposed=False):
    if transposed:
        return jnp.concatenate([v, jnp.ones((LANES, v.shape[1]), v.dtype)], axis=0)
    return jnp.concatenate([v, jnp.ones((v.shape[0], LANES), v.dtype)], axis=1)


def _attend(parts, sink=None):
    mx = None
    for s, _, _ in parts:
        pm = jnp.max(s, axis=-1, keepdims=True)
        mx = pm if mx is None else jnp.maximum(mx, pm)
    if sink is not None:
        mx = jnp.maximum(mx, sink)
    acc, den = None, None
    for s, v, v_t in parts:
        e = jnp.exp2(s - mx)
        po = (_dot_nt if v_t else _dot)(e.astype(BF16), v)
        acc = po if acc is None else acc + po
        if po.shape[1] == LANES:
            ps = jnp.sum(e, axis=-1, keepdims=True)
            den = ps if den is None else den + ps
    if den is None:
        den = acc[:, LANES:]
    if sink is not None:
        den = den + jnp.exp2(sink - mx)
    return acc[:, :LANES] * (1.0 / den)


def _lane_lo():
    return lax.broadcasted_iota(jnp.int32, (1, LANES), 1) < HEAD_DIM


def _store_pair_transposed(ref, bi, p, x):
    xt = x.T
    ref[bi, 0, 2 * p] = xt[:HEAD_DIM]
    ref[bi, 0, 2 * p + 1] = xt[HEAD_DIM:]


def _rope_key_slab(win_ref):
    d = win_ref.shape[1]
    return jnp.concatenate([jnp.zeros((NOPE_A, d), BF16), win_ref[E_KROPE:E_GA, :],
                            jnp.zeros((LANES - QK_A, d), BF16)], axis=0)


def _swap_halves(a):
    return jnp.concatenate([a[HEAD_DIM:], a[:HEAD_DIM]], axis=0)


def _mod_step(n_cond, is_first, bias_row, c_ref, w_ref, o_ref):
    @pl.when(is_first)
    def _():
        o_ref[:n_cond, :] = jnp.broadcast_to(bias_row, (n_cond, o_ref.shape[1]))
        o_ref[n_cond:, :] = jnp.zeros((o_ref.shape[0] - n_cond, o_ref.shape[1]), F32)

    s = _silu(c_ref[...])
    cols = [jnp.broadcast_to(s[:, r:r + 1], (s.shape[0], LANES)) for r in range(n_cond)]
    for t in range(w_ref.shape[1] // LANES):
        sl = slice(t * LANES, (t + 1) * LANES)
        w = w_ref[:, sl]
        for r in range(n_cond):
            o_ref[r:r + 1, sl] += jnp.sum(w * cols[r], axis=0, keepdims=True)


def _mla_keys(cb, kr, wkk_ref, wkv_ref, kg, rope=None):
    kk = _dot(cb, wkk_ref[...])
    keys = []
    for h in range(N_HEADS):
        k = _rms(kk[:, h * LANES:(h + 1) * LANES] + kr, kg, QK_A)
        if rope is not None:
            k = rope(k)
        keys.append(k.astype(BF16))
    return keys, _dot(cb, wkv_ref[...]).astype(BF16)


def _p0_kernel(n_cond, x_ref, m_ref, ng_ref, win_ref, qag_ref, wq_ref, kvag_ref, wkk_ref, wkv_ref, qg_ref, kg_ref,
               naqg_ref, nakg_ref, wout_ref, ct_ref, wm_ref, bm_ref, wio_ref, woo_ref,
               xo_ref, ckv_ref, krope_ref, nak_ref, nav_ref, mo_ref, wino_ref, wouto_ref, y_scr):
    _mod_step(n_cond, pl.program_id(0) == 0, bm_ref[1:2, :], ct_ref, wm_ref, mo_ref)
    wino_ref[...] = wio_ref[0].astype(BF16)
    wouto_ref[...] = woo_ref[0].astype(BF16)

    nbs = x_ref.shape[0]
    x = x_ref[...].reshape(nbs * SEQ, D_MODEL)
    h, gate = _modulate(x, ng_ref[...], m_ref[0:1, :])
    hb = h.astype(BF16)
    lo = _lane_lo()
    hi = jnp.logical_not(lo)
    rows = [slice(bi * SEQ, (bi + 1) * SEQ) for bi in range(nbs)]

    qln = _rms(_dot_nt(hb, win_ref[E_QLAT:E_CKV, :]), qag_ref[...], Q_RANK).astype(BF16)
    q_all = _dot(qln, wq_ref[...])
    ckv_n = _rms(_dot_nt(hb, win_ref[E_CKV:E_KROPE, :]), kvag_ref[...], KV_RANK)
    kr = _dot_nt(hb, _rope_key_slab(win_ref))
    for bi, rs in enumerate(rows):
        ckv_ref[bi, 0] = ckv_n[rs]
        krope_ref[bi, 0] = kr[rs].T[NOPE_A:QK_A]
    keys, vals = _mla_keys(ckv_n.astype(BF16), kr, wkk_ref, wkv_ref, kg_ref[...])
    qg = qg_ref[...] * (QK_A ** -0.5 * LOG2E)

    ga = _dot_nt(hb, win_ref[E_GA:E_QB, :])
    zq = _dot_nt(hb, win_ref[E_QB:E_KB, :])
    zk = _dot_nt(hb, win_ref[E_KB:E_VB, :])
    zv = _dot_nt(hb, win_ref[E_VB:E_GB, :])
    gb = _dot_nt(hb, win_ref[E_GB:E_END, :])
    naqg = naqg_ref[...] * (HEAD_DIM ** -0.5 * LOG2E)

    for p in range(N_PAIRS):
        sl = slice(p * LANES, (p + 1) * LANES)
        ys = slice(4 * LANES + p * LANES, 4 * LANES + (p + 1) * LANES)
        qhs = [_rms(q_all[:, hh * LANES:(hh + 1) * LANES], qg, QK_A).astype(BF16) for hh in (2 * p, 2 * p + 1)]
        qb = _rms_halves(zq[:, sl], naqg, lo)
        kb = _rms_halves(zk[:, sl], nakg_ref[...], lo)
        vb = zv[:, sl]
        kbb, vbb = kb.astype(BF16), vb.astype(BF16)
        va = vals[:, sl]
        qms = [jnp.where(keep, qb, 0.0).astype(BF16) for keep in (lo, hi)]
        for bi, rs in enumerate(rows):
            o2 = [_attend([(_dot_nt(qhs[i][rs], keys[2 * p + i][rs]), va[rs], False)]) for i in (0, 1)]
            y_scr[rs, sl] = (jnp.where(lo, o2[0], o2[1]) * _silu(ga[rs, sl])).astype(BF16)
            _store_pair_transposed(nak_ref, bi, p, kb[rs])
            _store_pair_transposed(nav_ref, bi, p, vb[rs])
            o2 = [_attend([(_dot_nt(qms[i][rs], kbb[rs]), vbb[rs], False)]) for i in (0, 1)]
            y_scr[rs, ys] = (jnp.where(lo, o2[0], o2[1]) * _silu(gb[rs, sl])).astype(BF16)

    xo_ref[...] = (x + gate * _dot(y_scr[...], wout_ref[...])).reshape(nbs, SEQ, D_MODEL)


def _full(shape):
    n = len(shape)
    return pl.BlockSpec(shape, lambda *_: (0,) * n, pipeline_mode=pl.Buffered(1))


class _Row(NamedTuple):
    table: jax.Array
    row: int


def _spec(a):
    if isinstance(a, _Row):
        idx = (a.row,) + (0,) * (a.table.ndim - 1)
        return pl.BlockSpec((None,) + a.table.shape[1:], lambda *_: idx, pipeline_mode=pl.Buffered(1))
    return _full(a.shape)


def _arr(a):
    return a.table if isinstance(a, _Row) else a


def _prompt_even(x, m, ng, win, qag, wq, kvag, wkk, wkv, qg, kg, naqg, nakg, wout,
                 cond_t, n_cond, w_mod, b_mod, w_in_odd, w_out_odd):
    nb = x.shape[0]
    nbs = PROMPT_BATCHES_PER_STEP
    steps = nb // nbs
    assert nb % nbs == 0 and D_MODEL % (BF16_SUBLANES * steps) == 0
    tr = D_MODEL // steps
    ins = (m, ng, win, qag, wq, kvag, wkk, wkv, qg, kg, naqg, nakg, wout)
    return pl.pallas_call(
        functools.partial(_p0_kernel, n_cond),
        grid=(steps,),
        in_specs=[pl.BlockSpec((nbs, SEQ, D_MODEL), lambda b: (b, 0, 0))] + [_spec(a) for a in ins]
        + [pl.BlockSpec((tr, 8), lambda b: (b, 0)),
           pl.BlockSpec((None, tr, 3 * D_MODEL), lambda b: (1, b, 0)),
           _full(b_mod.shape),
           pl.BlockSpec((1, tr, O_END), lambda b: (0, b, 0)),
           pl.BlockSpec((1, tr, D_MODEL), lambda b: (0, b, 0))],
        out_specs=[pl.BlockSpec((nbs, SEQ, D_MODEL), lambda b: (b, 0, 0)),
                   pl.BlockSpec((nbs, 1, SEQ, KV_RANK), lambda b: (b, 0, 0, 0)),
                   pl.BlockSpec((nbs, 1, ROPE_A, SEQ), lambda b: (b, 0, 0, 0)),
                   pl.BlockSpec((nbs, 1, N_HEADS, HEAD_DIM, SEQ), lambda b: (b, 0, 0, 0, 0)),
                   pl.BlockSpec((nbs, 1, N_HEADS, HEAD_DIM, SEQ), lambda b: (b, 0, 0, 0, 0)),
                   pl.BlockSpec((8, 3 * D_MODEL), lambda b: (0, 0)),
                   pl.BlockSpec((tr, O_END), lambda b: (b, 0)),
                   pl.BlockSpec((tr, D_MODEL), lambda b: (b, 0))],
        out_shape=[jax.ShapeDtypeStruct((nb, SEQ, D_MODEL), F32),
                   jax.ShapeDtypeStruct((nb, 1, SEQ, KV_RANK), F32),
                   jax.ShapeDtypeStruct((nb, 1, ROPE_A, SEQ), F32),
                   jax.ShapeDtypeStruct((nb, 1, N_HEADS, HEAD_DIM, SEQ), F32),
                   jax.ShapeDtypeStruct((nb, 1, N_HEADS, HEAD_DIM, SEQ), F32),
                   jax.ShapeDtypeStruct((8, 3 * D_MODEL), F32),
                   jax.ShapeDtypeStruct((D_MODEL, O_END), BF16),
                   jax.ShapeDtypeStruct((D_MODEL, D_MODEL), BF16)],
        scratch_shapes=[pltpu.VMEM((nbs * SEQ, D_MODEL), BF16)],
        compiler_params=pltpu.CompilerParams(dimension_semantics=("arbitrary",), vmem_limit_bytes=VMEM_LIMIT),
        name="prompt_even",
    )(x, *map(_arr, ins), cond_t, w_mod, b_mod, w_in_odd, w_out_odd)


def _gqa_pair_operands(k, v, kg2, lo):
    kn = _rms_halves(k, kg2, lo)
    return kn, (kn.astype(BF16), pltpu.roll(kn, HEAD_DIM, 1).astype(BF16)), \
        (_with_ones(v.astype(BF16)), _with_ones(pltpu.roll(v, HEAD_DIM, 1).astype(BF16)))


def _p1_kernel(sink_ref, x_ref, m_ref, ng_ref, win_ref, gqg_ref, gkg_ref, sqg_ref, skg_ref, wout_ref,
               xo_ref, gk_ref, gv_ref, sk_ref, sv_ref, y_scr):
    nbs = x_ref.shape[0]
    x = x_ref[...].reshape(nbs * SEQ, D_MODEL)
    h, gate = _modulate(x, ng_ref[...], m_ref[0:1, :])
    hb = h.astype(BF16)
    lo = _lane_lo()
    hi = jnp.logical_not(lo)
    sc = HEAD_DIM ** -0.5 * LOG2E
    rows = [slice(bi * SEQ, (bi + 1) * SEQ) for bi in range(nbs)]

    branches = ((O_QC, O_KC, O_VC, O_GC, gqg_ref, gkg_ref, gk_ref, gv_ref, False, 0),
                (O_QD, O_KD, O_VD, O_GD, sqg_ref, skg_ref, sk_ref, sv_ref, True, 4 * LANES))
    for oq, ok, ov, og, qg_ref, kg_ref, ck_ref, cv_ref, has_sink, yoff in branches:
        zq = _dot(hb, win_ref[:, oq:oq + 4 * LANES])
        zkv = _dot(hb, win_ref[:, ok:ok + 2 * LANES])
        zg = _dot(hb, win_ref[:, og:og + 4 * LANES])
        v = zkv[:, LANES:]
        kn, ks, vs = _gqa_pair_operands(zkv[:, :LANES], v, kg_ref[...], lo)
        for bi, rs in enumerate(rows):
            _store_pair_transposed(ck_ref, bi, 0, kn[rs])
            _store_pair_transposed(cv_ref, bi, 0, v[rs])
        qg = qg_ref[...] * sc
        for p in range(N_PAIRS):
            sl = slice(p * LANES, (p + 1) * LANES)
            qn = _rms_halves(zq[:, sl], qg, lo)
            qms = [jnp.where(keep, qn, 0.0).astype(BF16) for keep in (lo, hi)]
            kv = p // 2
            for bi, rs in enumerate(rows):
                o2 = []
                for half in (0, 1):
                    swap = 0 if kv == half else 1
                    sink = sink_ref[2 * p + half] * LOG2E if has_sink else None
                    o2.append(_attend([(_dot_nt(qms[half][rs], ks[swap][rs]), vs[swap][rs], False)], sink))
                o = jnp.where(lo, o2[0], o2[1])
                y_scr[rs, yoff + p * LANES:yoff + (p + 1) * LANES] = (o * _silu(zg[rs, sl])).astype(BF16)

    xo_ref[...] = (x + gate * _dot(y_scr[...], wout_ref[...])).reshape(nbs, SEQ, D_MODEL)


def _prompt_odd(sink, x, m, ng, win, gqg, gkg, sqg, skg, wout):
    nb = x.shape[0]
    nbs = PROMPT_ODD_BATCHES_PER_STEP
    assert nb % nbs == 0
    ins = (m, ng, win, gqg, gkg, sqg, skg, wout)
    cache_spec = pl.BlockSpec((nbs, 1, 2, HEAD_DIM, SEQ), lambda b: (b, 0, 0, 0, 0))
    cache_shape = jax.ShapeDtypeStruct((nb, 1, 2, HEAD_DIM, SEQ), F32)
    return pl.pallas_call(
        _p1_kernel,
        grid=(nb // nbs,),
        in_specs=[pl.BlockSpec(memory_space=pltpu.SMEM),
                  pl.BlockSpec((nbs, SEQ, D_MODEL), lambda b: (b, 0, 0))] + [_spec(a) for a in ins],
        out_specs=[pl.BlockSpec((nbs, SEQ, D_MODEL), lambda b: (b, 0, 0))] + [cache_spec] * 4,
        out_shape=[jax.ShapeDtypeStruct((nb, SEQ, D_MODEL), F32)] + [cache_shape] * 4,
        scratch_shapes=[pltpu.VMEM((nbs * SEQ, D_MODEL), BF16)],
        compiler_params=pltpu.CompilerParams(dimension_semantics=("arbitrary",), vmem_limit_bytes=VMEM_LIMIT),
        name="prompt_odd",
    )(sink, x, *map(_arr, ins))


def _s0a_kernel(x_ref, m_ref, ng_ref, win_ref, qag_ref, wq_ref, kvag_ref, wkk_ref, wkv_ref, qg_ref, kg_ref,
                naqg_ref, nakg_ref, cos_ref, sin_ref,
                qa_ref, ka_ref, va_ref, qb_ref, kb_ref, vb_ref, g_ref):
    b = pl.program_id(0)
    lo = _lane_lo()
    partner = _rope_matrix2(ROPE_A, LANES, NOPE_A)
    hb = _modulate(x_ref[0], ng_ref[...], m_ref[pl.ds(1 + b, 1), :])[0].astype(BF16)
    cos, sin = cos_ref[...], sin_ref[...]

    qln = _rms(_dot_nt(hb, win_ref[E_QLAT:E_CKV, :]), qag_ref[...], Q_RANK).astype(BF16)
    q_all = _dot(qln, wq_ref[...])
    ckv_n = _rms(_dot_nt(hb, win_ref[E_CKV:E_KROPE, :]), kvag_ref[...], KV_RANK)
    kr = _dot_nt(hb, _rope_key_slab(win_ref))
    cb = ckv_n.astype(BF16)
    kk = _dot(cb, wkk_ref[...])
    va_ref[0] = _dot(cb, wkv_ref[...]).astype(BF16)
    zq = _dot_nt(hb, win_ref[E_QB:E_KB, :])
    zk = _dot_nt(hb, win_ref[E_KB:E_VB, :])
    vb_ref[0] = _dot_nt(hb, win_ref[E_VB:E_GB, :]).astype(BF16)
    g_ref[0, :, 0:4 * LANES] = _silu(_dot_nt(hb, win_ref[E_GA:E_QB, :]))
    g_ref[0, :, 4 * LANES:8 * LANES] = _silu(_dot_nt(hb, win_ref[E_GB:E_END, :]))

    qg = qg_ref[...] * (QK_A ** -0.5 * LOG2E)
    kg = kg_ref[...]
    k_partner = _lane_mix(kr * kg, partner) * sin
    for hh in range(N_HEADS):
        sl = slice(hh * LANES, (hh + 1) * LANES)
        qn = _rms(q_all[:, sl], qg, QK_A)
        qa_ref[0, :, sl] = (qn * cos + _lane_mix(qn, partner) * sin).astype(BF16)
        k_raw = kk[:, sl] + kr
        k_inv = lax.rsqrt(jnp.sum(k_raw * k_raw, axis=-1, keepdims=True) / QK_A + EPS)
        ka_ref[0, :, sl] = ((k_raw * kg * cos + k_partner) * k_inv).astype(BF16)
    naqg = naqg_ref[...] * (HEAD_DIM ** -0.5 * LOG2E)
    for p in range(N_PAIRS):
        sl = slice(p * LANES, (p + 1) * LANES)
        qb_ref[0, :, sl] = _rms_halves(zq[:, sl], naqg, lo).astype(BF16)
        kb_ref[0, :, sl] = _rms_halves(zk[:, sl], nakg_ref[...], lo).astype(BF16)


def _sample_even_proj(x, m, ng, win, qag, wq, kvag, wkk, wkv, qg, kg, naqg, nakg, cos, sin):
    nb, s, _ = x.shape
    nq = s // PROJ_BLOCK
    ins = (m, ng, win, qag, wq, kvag, wkk, wkv, qg, kg, naqg, nakg)
    tab = pl.BlockSpec((PROJ_BLOCK, LANES), lambda b, j: (j, 0))

    def blk(w):
        return pl.BlockSpec((1, PROJ_BLOCK, w), lambda b, j: (b, j, 0))

    def shp(w, dt):
        return jax.ShapeDtypeStruct((nb, s, w), dt)

    return pl.pallas_call(
        _s0a_kernel,
        grid=(nb, nq),
        in_specs=[blk(D_MODEL)] + [_spec(a) for a in ins] + [tab, tab],
        out_specs=[blk(1024), blk(1024), blk(512), blk(512), blk(512), blk(512), blk(1024)],
        out_shape=[shp(1024, BF16), shp(1024, BF16), shp(512, BF16), shp(512, BF16), shp(512, BF16),
                   shp(512, BF16), shp(1024, F32)],
        compiler_params=pltpu.CompilerParams(dimension_semantics=("arbitrary", "arbitrary"),
                                             vmem_limit_bytes=VMEM_LIMIT),
        name="sample_even_proj",
    )(x, *map(_arr, ins), cos, sin)


def _build_bias_table(rpb_ref, tile_scr, tab_ref):
    qc = lax.broadcasted_iota(jnp.int32, (GRID_W, LANES), 0)
    lane = lax.broadcasted_iota(jnp.int32, (GRID_W, LANES), 1)
    kc = jnp.bitwise_and(lane, GRID_W - 1)
    lo = lane < GRID_W
    diff = kc - qc + (NA_COLS - 1)
    cs = jnp.clip(qc - NA_COLS // 2, 0, GRID_W - NA_COLS)
    valid = (kc >= cs) & (kc < cs + NA_COLS)
    tab_ref[...] = jnp.zeros(tab_ref.shape, F32)
    tile_scr[RPB_ROWS] = jnp.zeros((GRID_W, LANES), F32)

    def per_head(h, carry):
        for dr in range(RPB_ROWS):
            t = jnp.zeros((GRID_W, LANES), F32)
            for dc in range(RPB_COLS):
                t = jnp.where(diff == dc, rpb_ref[(h * RPB_ROWS + dr) * RPB_COLS + dc], t)
            tile_scr[dr] = jnp.where(valid, t * LOG2E, NEG_INF)
        for c in range(NA_ROWS // 2, NA_ROWS // 2 + NA_ROWS):
            d0 = 2 * c - NA_ROWS
            tab_ref[0, h, c] = jnp.where(lo, tile_scr[d0], tile_scr[d0 + 1])
            tab_ref[1, h, c] = jnp.where(lo, tile_scr[d0 - 1 if d0 > 0 else RPB_ROWS], tile_scr[d0])
        return carry

    lax.fori_loop(0, N_HEADS, per_head, 0)


def _na_latent_block(jq, n_lat, qb_ref, kb_ref, vb_ref, cnk_ref, cnv_ref, g_ref, tab_scr, y_scr):
    lo = _lane_lo()
    q_rows = NA_Q_BLOCK // GRID_W
    starts = [min(max(jq * q_rows + local - NA_ROWS // 2, 0), n_lat // GRID_W - NA_ROWS) for local in range(q_rows)]
    t_lo = (starts[0] * GRID_W) // LANES
    t_hi = -((-(starts[-1] + NA_ROWS) * GRID_W) // LANES)
    keys = slice(t_lo * LANES, t_hi * LANES)
    kidx = t_lo * LANES + lax.broadcasted_iota(jnp.int32, (1, (t_hi - t_lo) * LANES), 1)
    for p in range(N_PAIRS):
        sl = slice(p * LANES, (p + 1) * LANES)
        qb = qb_ref[0, :, sl]
        kb = kb_ref[0, keys, sl]
        vb = _with_ones(vb_ref[0, keys, sl])
        kcb = cnk_ref[0, sl, :].astype(BF16)
        vcb = _with_ones(cnv_ref[0, sl, :].astype(BF16), transposed=True)
        o2 = []
        for half in (0, 1):
            head = 2 * p + half
            qm = jnp.where(lo if half == 0 else jnp.logical_not(lo), qb, jnp.zeros_like(qb))
            s_lat = _dot_nt(qm, kb)
            rows = []
            for local in range(q_rows):
                par = 0 if local % 2 == 1 else 1
                c0 = (RPB_ROWS + par - local) // 2 - (q_rows // 2) * jq
                bias = jnp.concatenate([tab_scr[par, head, c0 + t] for t in range(t_lo, t_hi)], axis=1)
                r0 = starts[local] * GRID_W
                ok = (kidx >= r0) & (kidx < r0 + NA_ROWS * GRID_W)
                bias = bias + jnp.where(ok, 0.0, NEG_INF)
                rows.append(s_lat[local * GRID_W:(local + 1) * GRID_W] + bias)
            s_lat = jnp.concatenate(rows, axis=0)
            o2.append(_attend([(s_lat, vb, False), (_dot(qm, kcb), vcb, True)]))
        ob = jnp.where(lo, o2[0], o2[1])
        ys = slice(4 * LANES + p * LANES, 4 * LANES + (p + 1) * LANES)
        y_scr[:, ys] = (ob * g_ref[0, :, ys]).astype(BF16)


def _s0b_kernel(rpb_ref, x_ref, m_ref, qa_ref, ka_ref, va_ref, qb_ref, kb_ref, vb_ref, g_ref,
                cckv_ref, ckr_ref, cnk_ref, cnv_ref, wkk_ref, wkv_ref, kg_ref, wout_ref,
                xo_ref, kca_scr, vca_scr, tile_scr, tab_scr, y_scr):
    b = pl.program_id(0)
    j = pl.program_id(1)
    lo = _lane_lo()
    n_lat = ka_ref.shape[1]

    @pl.when((b == 0) & (j == 0))
    def _():
        _build_bias_table(rpb_ref, tile_scr, tab_scr)

    @pl.when(j == 0)
    def _():
        kr_t = jnp.concatenate([jnp.zeros((NOPE_A, PAST_LEN), F32), ckr_ref[0],
                                jnp.zeros((LANES - QK_A, PAST_LEN), F32)], axis=0)
        keys, vals = _mla_keys(cckv_ref[0].astype(BF16), kr_t.T, wkk_ref, wkv_ref, kg_ref[...])
        for hh in range(N_HEADS):
            kca_scr[:, hh * LANES:(hh + 1) * LANES] = keys[hh]
        vca_scr[...] = vals

    for p in range(N_PAIRS):
        sl = slice(p * LANES, (p + 1) * LANES)
        o2 = []
        va = _with_ones(va_ref[0, :, sl])
        vca = _with_ones(vca_scr[:, sl])
        for hh in (2 * p, 2 * p + 1):
            hs = slice(hh * LANES, (hh + 1) * LANES)
            q = qa_ref[0, :, hs]
            o2.append(_attend([(_dot_nt(q, ka_ref[0, :, hs]), va, False),
                               (_dot_nt(q, kca_scr[:, hs]), vca, False)]))
        oa = jnp.where(lo, o2[0], o2[1])
        y_scr[:, sl] = (oa * g_ref[0, :, sl]).astype(BF16)

    for jq in range(n_lat // NA_Q_BLOCK):
        pl.when(j == jq)(functools.partial(_na_latent_block, jq, n_lat, qb_ref, kb_ref, vb_ref, cnk_ref, cnv_ref,
                                           g_ref, tab_scr, y_scr))

    d = x_ref.shape[-1]
    gate = m_ref[pl.ds(1 + b, 1), 2 * d:]
    xo_ref[0] = x_ref[0] + gate * _dot(y_scr[...], wout_ref[...])


def _sample_even_attn(rpb, x, m, qa, ka, va, qb, kb, vb, g, cckv, ckr, cnk, cnv, wkk, wkv, kg, wout):
    nb, s, _ = x.shape
    nq = s // NA_Q_BLOCK

    def blk(w):
        return pl.BlockSpec((1, NA_Q_BLOCK, w), lambda b, j: (b, j, 0))

    def per_batch(a):
        return pl.BlockSpec((1,) + a.shape[1:], lambda b, j: (b, 0, 0))

    return pl.pallas_call(
        _s0b_kernel,
        grid=(nb, nq),
        in_specs=[pl.BlockSpec(memory_space=pltpu.SMEM), blk(D_MODEL), _spec(m),
                  blk(1024), per_batch(ka), per_batch(va), blk(512), per_batch(kb), per_batch(vb), blk(1024),
                  per_batch(cckv), per_batch(ckr), per_batch(cnk), per_batch(cnv),
                  _full(wkk.shape), _full(wkv.shape), _spec(kg), _full(wout.shape)],
        out_specs=blk(D_MODEL),
        out_shape=jax.ShapeDtypeStruct(x.shape, F32),
        scratch_shapes=[pltpu.VMEM((PAST_LEN, N_HEADS * LANES), BF16),
                        pltpu.VMEM((PAST_LEN, N_HEADS * HEAD_DIM), BF16),
                        pltpu.VMEM((RPB_ROWS + 1, GRID_W, LANES), F32),
                        pltpu.VMEM((2, N_HEADS, BIAS_CHUNKS, GRID_W, LANES), F32),
                        pltpu.VMEM((NA_Q_BLOCK, D_MODEL), BF16)],
        compiler_params=pltpu.CompilerParams(dimension_semantics=("arbitrary", "arbitrary"),
                                             vmem_limit_bytes=VMEM_LIMIT),
        name="sample_even_attn",
    )(rpb, x, _arr(m), qa, ka, va, qb, kb, vb, g, cckv, ckr, cnk, cnv, wkk, wkv, _arr(kg), wout)


def _s1a_kernel(x_ref, m_ref, ng_ref, win_ref, gqg_ref, gkg_ref, sqg_ref, skg_ref, cos_ref, sin_ref,
                qc_ref, kc_ref, vc_ref, qd_ref, kd_ref, vd_ref, g_ref):
    b = pl.program_id(0)
    lo = _lane_lo()
    partner = _rope_matrix2(HEAD_DIM, HEAD_DIM, 0)
    swap = _swap_matrix2()[:LANES]
    hb = _modulate(x_ref[0], ng_ref[...], m_ref[pl.ds(1 + b, 1), :])[0].astype(BF16)
    cos, sin = cos_ref[...], sin_ref[...]
    sc = HEAD_DIM ** -0.5 * LOG2E

    def rope(t):
        return t * cos + _lane_mix(t, partner) * sin

    branches = ((O_QC, O_KC, O_GC, gqg_ref, gkg_ref, qc_ref, kc_ref, vc_ref, 0),
                (O_QD, O_KD, O_GD, sqg_ref, skg_ref, qd_ref, kd_ref, vd_ref, 4 * LANES))
    for oq, ok, og, qg_ref, kg_ref, q_out, k_out, v_out, goff in branches:
        zq = _dot(hb, win_ref[:, oq:oq + 4 * LANES])
        zkv = _dot(hb, win_ref[:, ok:ok + 2 * LANES])
        qg = qg_ref[...] * sc
        for p in range(N_PAIRS):
            sl = slice(p * LANES, (p + 1) * LANES)
            q_out[0, :, sl] = rope(_rms_halves(zq[:, sl], qg, lo)).astype(BF16)
        kn = rope(_rms_halves(zkv[:, :LANES], kg_ref[...], lo))
        v = zkv[:, LANES:]
        for out, val in ((k_out, kn.astype(BF16)), (v_out, v.astype(BF16))):
            out[0, :, 0:LANES] = val
            out[0, :, LANES:2 * LANES] = _dot(val, swap).astype(BF16)
        g_ref[0, :, goff:goff + 4 * LANES] = _silu(_dot(hb, win_ref[:, og:og + 4 * LANES]))


def _sample_odd_proj(x, m, ng, win, gqg, gkg, sqg, skg, cos, sin):
    nb, s, _ = x.shape
    nq = s // PROJ_BLOCK
    ins = (m, ng, win, gqg, gkg, sqg, skg)
    tab = pl.BlockSpec((PROJ_BLOCK, LANES), lambda b, j: (j, 0))

    def blk(w):
        return pl.BlockSpec((1, PROJ_BLOCK, w), lambda b, j: (b, j, 0))

    def shp(w, dt):
        return jax.ShapeDtypeStruct((nb, s, w), dt)

    return pl.pallas_call(
        _s1a_kernel,
        grid=(nb, nq),
        in_specs=[blk(D_MODEL)] + [_spec(a) for a in ins] + [tab, tab],
        out_specs=[blk(512), blk(256), blk(256), blk(512), blk(256), blk(256), blk(1024)],
        out_shape=[shp(512, BF16), shp(256, BF16), shp(256, BF16), shp(512, BF16), shp(256, BF16),
                   shp(256, BF16), shp(1024, F32)],
        compiler_params=pltpu.CompilerParams(dimension_semantics=("arbitrary", "arbitrary"),
                                             vmem_limit_bytes=VMEM_LIMIT),
        name="sample_odd_proj",
    )(x, *map(_arr, ins), cos, sin)


def _s1b_kernel(sink_ref, x_ref, m_ref, qc_ref, kc_ref, vc_ref, qd_ref, kd_ref, vd_ref, g_ref,
                cgk_ref, cgv_ref, csk_ref, csv_ref, wout_ref, xo_ref, y_scr):
    b = pl.program_id(0)
    j = pl.program_id(1)
    lo = _lane_lo()
    n_lat = kc_ref.shape[1]
    win_keys = Q_BLOCK + 2 * SWA_HALF

    def ctx_pair(ref, values=False):
        a = ref[0].astype(BF16)
        pair = (a, _swap_halves(a))
        return tuple(_with_ones(t, transposed=True) for t in pair) if values else pair

    cgk, cgv, csk, csv = ctx_pair(cgk_ref), ctx_pair(cgv_ref, True), ctx_pair(csk_ref), ctx_pair(csv_ref, True)
    vcs = [_with_ones(vc_ref[0, :, w * LANES:(w + 1) * LANES]) for w in (0, 1)]

    ks = pl.multiple_of(jnp.clip(j * Q_BLOCK - SWA_HALF, 0, n_lat - win_keys), SWA_HALF)
    qpos = j * Q_BLOCK + lax.broadcasted_iota(jnp.int32, (Q_BLOCK, win_keys), 0)
    kpos = ks + lax.broadcasted_iota(jnp.int32, (Q_BLOCK, win_keys), 1)
    band = jnp.abs(qpos - kpos) <= SWA_HALF
    vds = [_with_ones(vd_ref[0, pl.ds(ks, win_keys), w * LANES:(w + 1) * LANES]) for w in (0, 1)]

    for p in range(N_PAIRS):
        sl = slice(p * LANES, (p + 1) * LANES)
        kv = p // 2
        qc = qc_ref[0, :, sl]
        qd = qd_ref[0, :, sl]
        oc2, od2 = [], []
        for half in (0, 1):
            swap = 0 if kv == half else 1
            ws = slice(swap * LANES, (swap + 1) * LANES)
            keep = lo if half == 0 else jnp.logical_not(lo)
            qm = jnp.where(keep, qc, jnp.zeros_like(qc))
            oc2.append(_attend([(_dot_nt(qm, kc_ref[0, :, ws]), vcs[swap], False),
                                (_dot(qm, cgk[swap]), cgv[swap], True)]))
            qm = jnp.where(keep, qd, jnp.zeros_like(qd))
            s_loc = jnp.where(band, _dot_nt(qm, kd_ref[0, pl.ds(ks, win_keys), ws]), NEG_INF)
            od2.append(_attend([(s_loc, vds[swap], False),
                                (_dot(qm, csk[swap]), csv[swap], True)], sink_ref[2 * p + half] * LOG2E))
        y_scr[:, sl] = (jnp.where(lo, oc2[0], oc2[1]) * g_ref[0, :, sl]).astype(BF16)
        ys = slice(4 * LANES + p * LANES, 4 * LANES + (p + 1) * LANES)
        y_scr[:, ys] = (jnp.where(lo, od2[0], od2[1]) * g_ref[0, :, ys]).astype(BF16)

    d = x_ref.shape[-1]
    gate = m_ref[pl.ds(1 + b, 1), 2 * d:]
    xo_ref[0] = x_ref[0] + gate * _dot(y_scr[...], wout_ref[...])


def _sample_odd_attn(sink, x, m, qc, kc, vc, qd, kd, vd, g, cgk, cgv, csk, csv, wout):
    nb, s, _ = x.shape
    nq = s // Q_BLOCK

    def blk(w):
        return pl.BlockSpec((1, Q_BLOCK, w), lambda b, j: (b, j, 0))

    def per_batch(a):
        return pl.BlockSpec((1,) + a.shape[1:], lambda b, j: (b, 0, 0))

    return pl.pallas_call(
        _s1b_kernel,
        grid=(nb, nq),
        in_specs=[pl.BlockSpec(memory_space=pltpu.SMEM), blk(D_MODEL), _spec(m),
                  blk(512), per_batch(kc), per_batch(vc), blk(512), per_batch(kd), per_batch(vd), blk(1024),
                  per_batch(cgk), per_batch(cgv), per_batch(csk), per_batch(csv), _full(wout.shape)],
        out_specs=blk(D_MODEL),
        out_shape=jax.ShapeDtypeStruct(x.shape, F32),
        scratch_shapes=[pltpu.VMEM((Q_BLOCK, D_MODEL), BF16)],
        compiler_params=pltpu.CompilerParams(dimension_semantics=("arbitrary", "arbitrary"),
                                             vmem_limit_bytes=VMEM_LIMIT),
        name="sample_odd_attn",
    )(sink, x, _arr(m), qc, kc, vc, qd, kd, vd, g, cgk, cgv, csk, csv, wout)


WEIGHT_PREP_STEPS = 8
EVEN_IN_CHUNKS = 6
COND_PREP_ROWS = 256
BF16_SUBLANES = 16


G_MLA_Q, G_MLA_K, G_NA_Q, G_NA_K, G_GQA_Q, G_GQA_K, G_SWA_Q, G_SWA_K, N_GAINS = range(9)


GAIN_WIDTHS = (QK_A, QK_A) + (HEAD_DIM,) * 6


def _cond_prep_kernel(n_cond, ct_ref, wm_ref, bm_ref, ng_ref, gains_ref, mo_ref, gt_ref, ngt_ref):
    _mod_step(n_cond, pl.program_id(0) == 0, bm_ref[0:1, :], ct_ref, wm_ref, mo_ref)
    gt_ref[...] = jnp.zeros(gt_ref.shape, F32)
    start = 0
    for r, w in enumerate(GAIN_WIDTHS):
        g = gains_ref[:, start:start + w]
        start += w
        for off in range(0, LANES - w + 1, w):
            gt_ref[r, :, off:off + w] = g
    for layer in range(ngt_ref.shape[0]):
        ngt_ref[layer] = ng_ref[layer:layer + 1, :]


def _cond_prep(cond_t, n_cond, w_mod, b_mod, norm_g, gains):
    assert len(gains) == N_GAINS and tuple(g.shape[-1] for g in gains) == GAIN_WIDTHS
    gains_row = jnp.concatenate([g.reshape(1, -1) for g in gains], axis=1)
    tk = COND_PREP_ROWS
    return pl.pallas_call(
        functools.partial(_cond_prep_kernel, n_cond),
        grid=(D_MODEL // tk,),
        in_specs=[pl.BlockSpec((tk, 8), lambda k: (k, 0)),
                  pl.BlockSpec((None, tk, 3 * D_MODEL), lambda k: (0, k, 0)),
                  _full(b_mod.shape), _full(norm_g.shape), _full(gains_row.shape)],
        out_specs=[_full((8, 3 * D_MODEL)), _full((N_GAINS, 1, LANES)), _full((norm_g.shape[0], 1, D_MODEL))],
        out_shape=[jax.ShapeDtypeStruct((8, 3 * D_MODEL), F32), jax.ShapeDtypeStruct((N_GAINS, 1, LANES), F32),
                   jax.ShapeDtypeStruct((norm_g.shape[0], 1, D_MODEL), F32)],
        compiler_params=pltpu.CompilerParams(dimension_semantics=("arbitrary",)),
        name="cond_prep",
    )(cond_t, w_mod, b_mod, norm_g, gains_row)


def _weight_prep_kernel(wie_ref, woe_ref, wqu_ref, wkv_ref, win_e_ref, wout_e_ref, wq_ref, wkk_ref, wkvv_ref):
    win_e_ref[...] = wie_ref[...].astype(BF16)
    wout_e_ref[...] = woe_ref[0].astype(BF16)

    wq_ref[...] = jnp.zeros(wq_ref.shape, BF16)
    for h in range(N_HEADS):
        wq_ref[:, h * LANES:h * LANES + QK_A] = wqu_ref[0, :, h * QK_A:(h + 1) * QK_A].astype(BF16)
    lo = _lane_lo()
    for p in range(N_PAIRS):
        a = wkv_ref[0, :, (2 * p) * LANES:(2 * p + 1) * LANES]
        c = wkv_ref[0, :, (2 * p + 1) * LANES:(2 * p + 2) * LANES]
        wkk_ref[:, (2 * p) * LANES:(2 * p + 1) * LANES] = jnp.where(lo, a, 0.0).astype(BF16)
        wkk_ref[:, (2 * p + 1) * LANES:(2 * p + 2) * LANES] = jnp.where(lo, c, 0.0).astype(BF16)
        wkvv_ref[:, p * LANES:(p + 1) * LANES] = jnp.where(lo, pltpu.roll(a, HEAD_DIM, 1), c).astype(BF16)


def _weight_prep(w_in_even_t, w_out_even, w_q_up, w_kv_up):
    n = WEIGHT_PREP_STEPS
    ins = (w_out_even, w_q_up, w_kv_up)
    out_cols = (D_MODEL, N_HEADS * LANES, N_HEADS * LANES, N_HEADS * HEAD_DIM)
    out_rows = (D_MODEL, Q_RANK, KV_RANK, KV_RANK)
    te = E_END // EVEN_IN_CHUNKS
    assert te * EVEN_IN_CHUNKS == E_END and te % BF16_SUBLANES == 0 and EVEN_IN_CHUNKS <= n
    even_spec = pl.BlockSpec((te, D_MODEL), lambda i: (jnp.minimum(i, EVEN_IN_CHUNKS - 1), 0))
    return pl.pallas_call(
        _weight_prep_kernel,
        grid=(n,),
        in_specs=[even_spec] + [pl.BlockSpec((1, a.shape[1] // n, a.shape[2]), lambda i: (0, i, 0)) for a in ins],
        out_specs=[even_spec] + [pl.BlockSpec((r // n, c), lambda i: (i, 0)) for r, c in zip(out_rows, out_cols)],
        out_shape=[jax.ShapeDtypeStruct((E_END, D_MODEL), BF16)]
        + [jax.ShapeDtypeStruct((r, c), BF16) for r, c in zip(out_rows, out_cols)],
        compiler_params=pltpu.CompilerParams(dimension_semantics=("arbitrary",), vmem_limit_bytes=VMEM_LIMIT),
        name="weight_prep",
    )(w_in_even_t, *ins)


def _feature_major(c):
    b, h, l, d = c.shape
    return jnp.swapaxes(c, -1, -2).reshape(b, h * d, l)


def _token_major(c):
    return jnp.swapaxes(c, -1, -2)


def _rope_tables(s, rot_dim, period, start):
    quarter = rot_dim // 4
    t = np.arange(s)
    inv = ROPE_THETA ** (-np.arange(quarter, dtype=np.float64) / quarter)
    row = (t // GRID_W).astype(np.float64)[:, None] * inv
    col = (t % GRID_W).astype(np.float64)[:, None] * inv
    ang = np.concatenate([row, col], axis=-1)
    cos, sin = np.cos(ang), np.sin(ang)
    pre = np.ones((s, start))
    post = np.zeros((s, period - start - rot_dim))
    c = np.concatenate([pre, cos, cos, post], axis=-1)
    sn = np.concatenate([0 * pre, sin, sin, post], axis=-1)
    rep = LANES // period
    return jnp.asarray(np.tile(c, (1, rep)), F32), jnp.asarray(np.tile(sn, (1, rep)), F32)


def kernel(x_prompt, x_sample, cache_mla_ckv, cache_mla_krope, cache_na_k, cache_na_v, cache_gqa_k, cache_gqa_v, cache_swa_k, cache_swa_v, c, c_ctx, norm_g, w_mod, b_mod, w_in_even, mla_qa_g, w_q_up, mla_kva_g, w_kv_up, mla_q_g, mla_k_g, na_q_g, na_k_g, na_rpb, w_out_even, w_in_odd, gqa_q_g, gqa_k_g, swa_q_g, swa_k_g, swa_sink, w_out_odd):
    n_dec = x_sample.shape[0]
    assert w_mod.shape[0] == 2 and n_dec + 1 <= 8

    cond_t = jnp.concatenate([c_ctx[:, None], c.T, jnp.zeros((D_MODEL, 7 - n_dec), F32)], axis=1)
    n_cond = 1 + n_dec
    gains = (mla_q_g, mla_k_g, na_q_g, na_k_g, gqa_q_g, gqa_k_g, swa_q_g, swa_k_g)
    win_e, wout_e, wq, wkk, wkv = _weight_prep(jnp.swapaxes(w_in_even[0], 0, 1), w_out_even, w_q_up, w_kv_up)
    m_even, gt, ngt = _cond_prep(cond_t, n_cond, w_mod, b_mod, norm_g, gains)
    even = (_Row(ngt, 0), win_e, mla_qa_g, wq, mla_kva_g, wkk, wkv,
            _Row(gt, G_MLA_Q), _Row(gt, G_MLA_K), _Row(gt, G_NA_Q), _Row(gt, G_NA_K))
    sink = swa_sink[0].astype(F32)

    xp1, new_ckv, new_krope, new_na_k, new_na_v, m_odd, win_o, wout_o = _prompt_even(
        x_prompt, m_even, *even, wout_e, cond_t, n_cond, w_mod, b_mod, w_in_odd, w_out_odd)
    odd = (_Row(ngt, 1), win_o, _Row(gt, G_GQA_Q), _Row(gt, G_GQA_K), _Row(gt, G_SWA_Q), _Row(gt, G_SWA_K))
    xp2, new_gqa_k, new_gqa_v, new_swa_k, new_swa_v = _prompt_odd(sink, xp1, m_odd, *odd, wout_o)

    cos_e, sin_e = _rope_tables(DEC_SEQ, ROPE_A, LANES, NOPE_A)
    qa, ka, va, qbs, kbs, vbs, g0 = _sample_even_proj(x_sample, m_even, *even, cos_e, sin_e)
    ckr = jnp.swapaxes(cache_mla_krope[:, 0], -1, -2)
    xs1 = _sample_even_attn(na_rpb[0].reshape(-1), x_sample, m_even, qa, ka, va, qbs, kbs, vbs, g0,
                            cache_mla_ckv[:, 0], ckr, _feature_major(cache_na_k[:, 0]),
                            _feature_major(cache_na_v[:, 0]), wkk, wkv, _Row(gt, G_MLA_K), wout_e)
    cos_o, sin_o = _rope_tables(DEC_SEQ, HEAD_DIM, HEAD_DIM, 0)
    qc, kc, vc, qd, kd, vd, g1 = _sample_odd_proj(xs1, m_odd, *odd, cos_o, sin_o)
    xs2 = _sample_odd_attn(sink, xs1, m_odd, qc, kc, vc, qd, kd, vd, g1,
                           _feature_major(cache_gqa_k[:, 0]), _feature_major(cache_gqa_v[:, 0]),
                           _feature_major(cache_swa_k[:, 0]), _feature_major(cache_swa_v[:, 0]), wout_o)

    caches = (new_krope, new_na_k, new_na_v, new_gqa_k, new_gqa_v, new_swa_k, new_swa_v)
    return (xp2, xs2, new_ckv) + tuple(_token_major(c) for c in caches)
```

```python
import functools
from typing import NamedTuple

import jax
import jax.numpy as jnp
import numpy as np
from jax import lax
from jax.experimental import pallas as pl
from jax.experimental.pallas import tpu as pltpu

F32 = jnp.float32
BF16 = jnp.bfloat16

D_MODEL = 1024
SEQ = 256
DEC_SEQ = 1024
PAST_LEN = 256
GRID_W = 64
HEAD_DIM = 64
Q_RANK = 256
KV_RANK = 128
NOPE_A = 64
ROPE_A = 32
QK_A = NOPE_A + ROPE_A
N_HEADS = 8
NA_ROWS = 8
NA_COLS = 16
SWA_HALF = 128
ROPE_THETA = 10000.0
EPS = 1e-6
NEG_INF = -1e30
LOG2E = 1.4426950408889634

LANES = 128
Q_BLOCK = 512
NA_Q_BLOCK = 256
PROJ_BLOCK = 512
PROMPT_BATCHES_PER_STEP = 2
PROMPT_ODD_BATCHES_PER_STEP = 4
N_PAIRS = N_HEADS // 2
RPB_ROWS = 2 * NA_ROWS - 1
RPB_COLS = 2 * NA_COLS - 1
BIAS_CHUNKS = 16
VMEM_LIMIT = 48 * 1024 * 1024

E_QLAT, E_CKV, E_KROPE, E_GA, E_QB, E_KB, E_VB, E_GB, E_END = 0, 256, 384, 416, 928, 1440, 1952, 2464, 2976
O_QC, O_KC, O_VC, O_GC, O_QD, O_KD, O_VD, O_GD, O_END = 0, 512, 640, 768, 1280, 1792, 1920, 2048, 2560


def _dot(a, b):
    return lax.dot_general(a, b, (((1,), (0,)), ((), ())), preferred_element_type=F32)


def _dot_nt(a, b):
    return lax.dot_general(a, b, (((1,), (1,)), ((), ())), preferred_element_type=F32)


def _silu(x):
    return x / (1.0 + jnp.exp(-x))


def _rms(x, g, n):
    ss = jnp.sum(x * x, axis=-1, keepdims=True)
    return x * lax.rsqrt(ss / n + EPS) * g


def _rms_halves(x, g2, lo):
    x2 = x * x
    s_lo = jnp.sum(jnp.where(lo, x2, 0.0), axis=-1, keepdims=True)
    s_hi = jnp.sum(jnp.where(lo, 0.0, x2), axis=-1, keepdims=True)
    r = jnp.where(lo, lax.rsqrt(s_lo / HEAD_DIM + EPS), lax.rsqrt(s_hi / HEAD_DIM + EPS))
    return x * r * g2


def _modulate(x, g, m):
    d = x.shape[-1]
    xn = x * lax.rsqrt(jnp.mean(x * x, axis=-1, keepdims=True) + EPS) * g
    return xn * (1.0 + m[:, d:2 * d]) + m[:, :d], m[:, 2 * d:]


def _split_lanes(x):
    hi = x.astype(BF16)
    lo = (x - hi.astype(F32)).astype(BF16)
    return jnp.concatenate([hi, lo], axis=1)


def _lane_matrix2(entries):
    i = lax.broadcasted_iota(jnp.int32, (LANES, LANES), 0)
    j = lax.broadcasted_iota(jnp.int32, (LANES, LANES), 1)
    m = entries(i, j).astype(BF16)
    return jnp.concatenate([m, m], axis=0)


def _rope_matrix2(rot_dim, period, start):
    half = rot_dim // 2

    def entries(i, j):
        pos = jnp.bitwise_and(j, period - 1) - start
        neg = (pos >= 0) & (pos < half) & (i == j + half)
        plus = (pos >= half) & (pos < rot_dim) & (i == j - half)
        return jnp.where(neg, -1.0, jnp.where(plus, 1.0, 0.0))

    return _lane_matrix2(entries)


def _swap_matrix2():
    return _lane_matrix2(lambda i, j: jnp.where(i == jnp.bitwise_xor(j, HEAD_DIM), 1.0, 0.0))


def _lane_mix(x, m2):
    return _dot(_split_lanes(x), m2)


def _with_ones(v, transposed=False):
    if transposed:
        return jnp.concatenate([v, jnp.ones((LANES, v.shape[1]), v.dtype)], axis=0)
    return jnp.concatenate([v, jnp.ones((v.shape[0], LANES), v.dtype)], axis=1)


def _attend(parts, sink=None):
    mx = None
    for s, _, _ in parts:
        pm = jnp.max(s, axis=-1, keepdims=True)
        mx = pm if mx is None else jnp.maximum(mx, pm)
    if sink is not None:
        mx = jnp.maximum(mx, sink)
    acc, den = None, None
    for s, v, v_t in parts:
        e = jnp.exp2(s - mx)
        po = (_dot_nt if v_t else _dot)(e.astype(BF16), v)
        acc = po if acc is None else acc + po
        if po.shape[1] == LANES:
            ps = jnp.sum(e, axis=-1, keepdims=True)
            den = ps if den is None else den + ps
    if den is None:
        den = acc[:, LANES:]
    if sink is not None:
        den = den + jnp.exp2(sink - mx)
    return acc[:, :LANES] * (1.0 / den)


def _lane_lo():
    return lax.broadcasted_iota(jnp.int32, (1, LANES), 1) < HEAD_DIM


def _store_pair_transposed(ref, bi, p, x):
    xt = x.T
    ref[bi, 0, 2 * p] = xt[:HEAD_DIM]
    ref[bi, 0, 2 * p + 1] = xt[HEAD_DIM:]


def _rope_key_slab(win_ref):
    d = win_ref.shape[1]
    return jnp.concatenate([jnp.zeros((NOPE_A, d), BF16), win_ref[E_KROPE:E_GA, :],
                            jnp.zeros((LANES - QK_A, d), BF16)], axis=0)


def _swap_halves(a):
    return jnp.concatenate([a[HEAD_DIM:], a[:HEAD_DIM]], axis=0)


def _mod_step(n_cond, is_first, bias_row, c_ref, w_ref, o_ref):
    @pl.when(is_first)
    def _():
        o_ref[:n_cond, :] = jnp.broadcast_to(bias_row, (n_cond, o_ref.shape[1]))
        o_ref[n_cond:, :] = jnp.zeros((o_ref.shape[0] - n_cond, o_ref.shape[1]), F32)

    s = _silu(c_ref[...])
    cols = [jnp.broadcast_to(s[:, r:r + 1], (s.shape[0], LANES)) for r in range(n_cond)]
    for t in range(w_ref.shape[1] // LANES):
        sl = slice(t * LANES, (t + 1) * LANES)
        w = w_ref[:, sl]
        for r in range(n_cond):
            o_ref[r:r + 1, sl] += jnp.sum(w * cols[r], axis=0, keepdims=True)


def _mla_keys(cb, kr, wkk_ref, wkv_ref, kg, rope=None):
    kk = _dot(cb, wkk_ref[...])
    keys = []
    for h in range(N_HEADS):
        k = _rms(kk[:, h * LANES:(h + 1) * LANES] + kr, kg, QK_A)
        if rope is not None:
            k = rope(k)
        keys.append(k.astype(BF16))
    return keys, _dot(cb, wkv_ref[...]).astype(BF16)


def _p0_kernel(n_cond, x_ref, m_ref, ng_ref, win_ref, qag_ref, wq_ref, kvag_ref, wkk_ref, wkv_ref, qg_ref, kg_ref,
               naqg_ref, nakg_ref, wout_ref, ct_ref, wm_ref, bm_ref, wio_ref, woo_ref,
               xo_ref, ckv_ref, krope_ref, nak_ref, nav_ref, mo_ref, wino_ref, wouto_ref, y_scr):
    _mod_step(n_cond, pl.program_id(0) == 0, bm_ref[1:2, :], ct_ref, wm_ref, mo_ref)
    wino_ref[...] = wio_ref[0].astype(BF16)
    wouto_ref[...] = woo_ref[0].astype(BF16)

    nbs = x_ref.shape[0]
    x = x_ref[...].reshape(nbs * SEQ, D_MODEL)
    h, gate = _modulate(x, ng_ref[...], m_ref[0:1, :])
    hb = h.astype(BF16)
    lo = _lane_lo()
    hi = jnp.logical_not(lo)
    rows = [slice(bi * SEQ, (bi + 1) * SEQ) for bi in range(nbs)]

    qln = _rms(_dot_nt(hb, win_ref[E_QLAT:E_CKV, :]), qag_ref[...], Q_RANK).astype(BF16)
    q_all = _dot(qln, wq_ref[...])
    ckv_n = _rms(_dot_nt(hb, win_ref[E_CKV:E_KROPE, :]), kvag_ref[...], KV_RANK)
    kr = _dot_nt(hb, _rope_key_slab(win_ref))
    for bi, rs in enumerate(rows):
        ckv_ref[bi, 0] = ckv_n[rs]
        krope_ref[bi, 0] = kr[rs].T[NOPE_A:QK_A]
    keys, vals = _mla_keys(ckv_n.astype(BF16), kr, wkk_ref, wkv_ref, kg_ref[...])
    qg = qg_ref[...] * (QK_A ** -0.5 * LOG2E)

    ga = _dot_nt(hb, win_ref[E_GA:E_QB, :])
    zq = _dot_nt(hb, win_ref[E_QB:E_KB, :])
    zk = _dot_nt(hb, win_ref[E_KB:E_VB, :])
    zv = _dot_nt(hb, win_ref[E_VB:E_GB, :])
    gb = _dot_nt(hb, win_ref[E_GB:E_END, :])
    naqg = naqg_ref[...] * (HEAD_DIM ** -0.5 * LOG2E)

    for p in range(N_PAIRS):
        sl = slice(p * LANES, (p + 1) * LANES)
        ys = slice(4 * LANES + p * LANES, 4 * LANES + (p + 1) * LANES)
        qhs = [_rms(q_all[:, hh * LANES:(hh + 1) * LANES], qg, QK_A).astype(BF16) for hh in (2 * p, 2 * p + 1)]
        qb = _rms_halves(zq[:, sl], naqg, lo)
        kb = _rms_halves(zk[:, sl], nakg_ref[...], lo)
        vb = zv[:, sl]
        kbb, vbb = kb.astype(BF16), vb.astype(BF16)
        va = vals[:, sl]
        qms = [jnp.where(keep, qb, 0.0).astype(BF16) for keep in (lo, hi)]
        for bi, rs in enumerate(rows):
            o2 = [_attend([(_dot_nt(qhs[i][rs], keys[2 * p + i][rs]), va[rs], False)]) for i in (0, 1)]
            y_scr[rs, sl] = (jnp.where(lo, o2[0], o2[1]) * _silu(ga[rs, sl])).astype(BF16)
            _store_pair_transposed(nak_ref, bi, p, kb[rs])
            _store_pair_transposed(nav_ref, bi, p, vb[rs])
            o2 = [_attend([(_dot_nt(qms[i][rs], kbb[rs]), vbb[rs], False)]) for i in (0, 1)]
            y_scr[rs, ys] = (jnp.where(lo, o2[0], o2[1]) * _silu(gb[rs, sl])).astype(BF16)

    xo_ref[...] = (x + gate * _dot(y_scr[...], wout_ref[...])).reshape(nbs, SEQ, D_MODEL)


def _full(shape):
    n = len(shape)
    return pl.BlockSpec(shape, lambda *_: (0,) * n, pipeline_mode=pl.Buffered(1))


class _Row(NamedTuple):
    table: jax.Array
    row: int


def _spec(a):
    if isinstance(a, _Row):
        idx = (a.row,) + (0,) * (a.table.ndim - 1)
        return pl.BlockSpec((None,) + a.table.shape[1:], lambda *_: idx, pipeline_mode=pl.Buffered(1))
    return _full(a.shape)


def _arr(a):
    return a.table if isinstance(a, _Row) else a


def _prompt_even(x, m, ng, win, qag, wq, kvag, wkk, wkv, qg, kg, naqg, nakg, wout,
                 cond_t, n_cond, w_mod, b_mod, w_in_odd, w_out_odd):
    nb = x.shape[0]
    nbs = PROMPT_BATCHES_PER_STEP
    steps = nb // nbs
    assert nb % nbs == 0 and D_MODEL % (BF16_SUBLANES * steps) == 0
    tr = D_MODEL // steps
    ins = (m, ng, win, qag, wq, kvag, wkk, wkv, qg, kg, naqg, nakg, wout)
    return pl.pallas_call(
        functools.partial(_p0_kernel, n_cond),
        grid=(steps,),
        in_specs=[pl.BlockSpec((nbs, SEQ, D_MODEL), lambda b: (b, 0, 0))] + [_spec(a) for a in ins]
        + [pl.BlockSpec((tr, 8), lambda b: (b, 0)),
           pl.BlockSpec((None, tr, 3 * D_MODEL), lambda b: (1, b, 0)),
           _full(b_mod.shape),
           pl.BlockSpec((1, tr, O_END), lambda b: (0, b, 0)),
           pl.BlockSpec((1, tr, D_MODEL), lambda b: (0, b, 0))],
        out_specs=[pl.BlockSpec((nbs, SEQ, D_MODEL), lambda b: (b, 0, 0)),
                   pl.BlockSpec((nbs, 1, SEQ, KV_RANK), lambda b: (b, 0, 0, 0)),
                   pl.BlockSpec((nbs, 1, ROPE_A, SEQ), lambda b: (b, 0, 0, 0)),
                   pl.BlockSpec((nbs, 1, N_HEADS, HEAD_DIM, SEQ), lambda b: (b, 0, 0, 0, 0)),
                   pl.BlockSpec((nbs, 1, N_HEADS, HEAD_DIM, SEQ), lambda b: (b, 0, 0, 0, 0)),
                   pl.BlockSpec((8, 3 * D_MODEL), lambda b: (0, 0)),
                   pl.BlockSpec((tr, O_END), lambda b: (b, 0)),
                   pl.BlockSpec((tr, D_MODEL), lambda b: (b, 0))],
        out_shape=[jax.ShapeDtypeStruct((nb, SEQ, D_MODEL), F32),
                   jax.ShapeDtypeStruct((nb, 1, SEQ, KV_RANK), F32),
                   jax.ShapeDtypeStruct((nb, 1, ROPE_A, SEQ), F32),
                   jax.ShapeDtypeStruct((nb, 1, N_HEADS, HEAD_DIM, SEQ), F32),
                   jax.ShapeDtypeStruct((nb, 1, N_HEADS, HEAD_DIM, SEQ), F32),
                   jax.ShapeDtypeStruct((8, 3 * D_MODEL), F32),
                   jax.ShapeDtypeStruct((D_MODEL, O_END), BF16),
                   jax.ShapeDtypeStruct((D_MODEL, D_MODEL), BF16)],
        scratch_shapes=[pltpu.VMEM((nbs * SEQ, D_MODEL), BF16)],
        compiler_params=pltpu.CompilerParams(dimension_semantics=("arbitrary",), vmem_limit_bytes=VMEM_LIMIT),
        name="prompt_even",
    )(x, *map(_arr, ins), cond_t, w_mod, b_mod, w_in_odd, w_out_odd)


def _gqa_pair_operands(k, v, kg2, lo):
    kn = _rms_halves(k, kg2, lo)
    return kn, (kn.astype(BF16), pltpu.roll(kn, HEAD_DIM, 1).astype(BF16)), \
        (_with_ones(v.astype(BF16)), _with_ones(pltpu.roll(v, HEAD_DIM, 1).astype(BF16)))


def _p1_kernel(sink_ref, x_ref, m_ref, ng_ref, win_ref, gqg_ref, gkg_ref, sqg_ref, skg_ref, wout_ref,
               xo_ref, gk_ref, gv_ref, sk_ref, sv_ref, y_scr):
    nbs = x_ref.shape[0]
    x = x_ref[...].reshape(nbs * SEQ, D_MODEL)
    h, gate = _modulate(x, ng_ref[...], m_ref[0:1, :])
    hb = h.astype(BF16)
    lo = _lane_lo()
    hi = jnp.logical_not(lo)
    sc = HEAD_DIM ** -0.5 * LOG2E
    rows = [slice(bi * SEQ, (bi + 1) * SEQ) for bi in range(nbs)]

    branches = ((O_QC, O_KC, O_VC, O_GC, gqg_ref, gkg_ref, gk_ref, gv_ref, False, 0),
                (O_QD, O_KD, O_VD, O_GD, sqg_ref, skg_ref, sk_ref, sv_ref, True, 4 * LANES))
    for oq, ok, ov, og, qg_ref, kg_ref, ck_ref, cv_ref, has_sink, yoff in branches:
        zq = _dot(hb, win_ref[:, oq:oq + 4 * LANES])
        zkv = _dot(hb, win_ref[:, ok:ok + 2 * LANES])
        zg = _dot(hb, win_ref[:, og:og + 4 * LANES])
        v = zkv[:, LANES:]
        kn, ks, vs = _gqa_pair_operands(zkv[:, :LANES], v, kg_ref[...], lo)
        for bi, rs in enumerate(rows):
            _store_pair_transposed(ck_ref, bi, 0, kn[rs])
            _store_pair_transposed(cv_ref, bi, 0, v[rs])
        qg = qg_ref[...] * sc
        for p in range(N_PAIRS):
            sl = slice(p * LANES, (p + 1) * LANES)
            qn = _rms_halves(zq[:, sl], qg, lo)
            qms = [jnp.where(keep, qn, 0.0).astype(BF16) for keep in (lo, hi)]
            kv = p // 2
            for bi, rs in enumerate(rows):
                o2 = []
                for half in (0, 1):
                    swap = 0 if kv == half else 1
                    sink = sink_ref[2 * p + half] * LOG2E if has_sink else None
                    o2.append(_attend([(_dot_nt(qms[half][rs], ks[swap][rs]), vs[swap][rs], False)], sink))
                o = jnp.where(lo, o2[0], o2[1])
                y_scr[rs, yoff + p * LANES:yoff + (p + 1) * LANES] = (o * _silu(zg[rs, sl])).astype(BF16)

    xo_ref[...] = (x + gate * _dot(y_scr[...], wout_ref[...])).reshape(nbs, SEQ, D_MODEL)


def _prompt_odd(sink, x, m, ng, win, gqg, gkg, sqg, skg, wout):
    nb = x.shape[0]
    nbs = PROMPT_ODD_BATCHES_PER_STEP
    assert nb % nbs == 0
    ins = (m, ng, win, gqg, gkg, sqg, skg, wout)
    cache_spec = pl.BlockSpec((nbs, 1, 2, HEAD_DIM, SEQ), lambda b: (b, 0, 0, 0, 0))
    cache_shape = jax.ShapeDtypeStruct((nb, 1, 2, HEAD_DIM, SEQ), F32)
    return pl.pallas_call(
        _p1_kernel,
        grid=(nb // nbs,),
        in_specs=[pl.BlockSpec(memory_space=pltpu.SMEM),
                  pl.BlockSpec((nbs, SEQ, D_MODEL), lambda b: (b, 0, 0))] + [_spec(a) for a in ins],
        out_specs=[pl.BlockSpec((nbs, SEQ, D_MODEL), lambda b: (b, 0, 0))] + [cache_spec] * 4,
        out_shape=[jax.ShapeDtypeStruct((nb, SEQ, D_MODEL), F32)] + [cache_shape] * 4,
        scratch_shapes=[pltpu.VMEM((nbs * SEQ, D_MODEL), BF16)],
        compiler_params=pltpu.CompilerParams(dimension_semantics=("arbitrary",), vmem_limit_bytes=VMEM_LIMIT),
        name="prompt_odd",
    )(sink, x, *map(_arr, ins))


def _s0a_kernel(x_ref, m_ref, ng_ref, win_ref, qag_ref, wq_ref, kvag_ref, wkk_ref, wkv_ref, qg_ref, kg_ref,
                naqg_ref, nakg_ref, cos_ref, sin_ref,
                qa_ref, ka_ref, va_ref, qb_ref, kb_ref, vb_ref, g_ref):
    b = pl.program_id(0)
    lo = _lane_lo()
    partner = _rope_matrix2(ROPE_A, LANES, NOPE_A)
    hb = _modulate(x_ref[0], ng_ref[...], m_ref[pl.ds(1 + b, 1), :])[0].astype(BF16)
    cos, sin = cos_ref[...], sin_ref[...]

    qln = _rms(_dot_nt(hb, win_ref[E_QLAT:E_CKV, :]), qag_ref[...], Q_RANK).astype(BF16)
    q_all = _dot(qln, wq_ref[...])
    ckv_n = _rms(_dot_nt(hb, win_ref[E_CKV:E_KROPE, :]), kvag_ref[...], KV_RANK)
    kr = _dot_nt(hb, _rope_key_slab(win_ref))
    cb = ckv_n.astype(BF16)
    kk = _dot(cb, wkk_ref[...])
    va_ref[0] = _dot(cb, wkv_ref[...]).astype(BF16)
    zq = _dot_nt(hb, win_ref[E_QB:E_KB, :])
    zk = _dot_nt(hb, win_ref[E_KB:E_VB, :])
    vb_ref[0] = _dot_nt(hb, win_ref[E_VB:E_GB, :]).astype(BF16)
    g_ref[0, :, 0:4 * LANES] = _silu(_dot_nt(hb, win_ref[E_GA:E_QB, :]))
    g_ref[0, :, 4 * LANES:8 * LANES] = _silu(_dot_nt(hb, win_ref[E_GB:E_END, :]))

    qg = qg_ref[...] * (QK_A ** -0.5 * LOG2E)
    kg = kg_ref[...]
    k_partner = _lane_mix(kr * kg, partner) * sin
    for hh in range(N_HEADS):
        sl = slice(hh * LANES, (hh + 1) * LANES)
        qn = _rms(q_all[:, sl], qg, QK_A)
        qa_ref[0, :, sl] = (qn * cos + _lane_mix(qn, partner) * sin).astype(BF16)
        k_raw = kk[:, sl] + kr
        k_inv = lax.rsqrt(jnp.sum(k_raw * k_raw, axis=-1, keepdims=True) / QK_A + EPS)
        ka_ref[0, :, sl] = ((k_raw * kg * cos + k_partner) * k_inv).astype(BF16)
    naqg = naqg_ref[...] * (HEAD_DIM ** -0.5 * LOG2E)
    for p in range(N_PAIRS):
        sl = slice(p * LANES, (p + 1) * LANES)
        qb_ref[0, :, sl] = _rms_halves(zq[:, sl], naqg, lo).astype(BF16)
        kb_ref[0, :, sl] = _rms_halves(zk[:, sl], nakg_ref[...], lo).astype(BF16)


def _sample_even_proj(x, m, ng, win, qag, wq, kvag, wkk, wkv, qg, kg, naqg, nakg, cos, sin):
    nb, s, _ = x.shape
    nq = s // PROJ_BLOCK
    ins = (m, ng, win, qag, wq, kvag, wkk, wkv, qg, kg, naqg, nakg)
    tab = pl.BlockSpec((PROJ_BLOCK, LANES), lambda b, j: (j, 0))

    def blk(w):
        return pl.BlockSpec((1, PROJ_BLOCK, w), lambda b, j: (b, j, 0))

    def shp(w, dt):
        return jax.ShapeDtypeStruct((nb, s, w), dt)

    return pl.pallas_call(
        _s0a_kernel,
        grid=(nb, nq),
        in_specs=[blk(D_MODEL)] + [_spec(a) for a in ins] + [tab, tab],
        out_specs=[blk(1024), blk(1024), blk(512), blk(512), blk(512), blk(512), blk(1024)],
        out_shape=[shp(1024, BF16), shp(1024, BF16), shp(512, BF16), shp(512, BF16), shp(512, BF16),
                   shp(512, BF16), shp(1024, F32)],
        compiler_params=pltpu.CompilerParams(dimension_semantics=("arbitrary", "arbitrary"),
                                             vmem_limit_bytes=VMEM_LIMIT),
        name="sample_even_proj",
    )(x, *map(_arr, ins), cos, sin)


def _build_bias_table(rpb_ref, tile_scr, tab_ref):
    qc = lax.broadcasted_iota(jnp.int32, (GRID_W, LANES), 0)
    lane = lax.broadcasted_iota(jnp.int32, (GRID_W, LANES), 1)
    kc = jnp.bitwise_and(lane, GRID_W - 1)
    lo = lane < GRID_W
    diff = kc - qc + (NA_COLS - 1)
    cs = jnp.clip(qc - NA_COLS // 2, 0, GRID_W - NA_COLS)
    valid = (kc >= cs) & (kc < cs + NA_COLS)
    tab_ref[...] = jnp.zeros(tab_ref.shape, F32)
    tile_scr[RPB_ROWS] = jnp.zeros((GRID_W, LANES), F32)

    def per_head(h, carry):
        for dr in range(RPB_ROWS):
            t = jnp.zeros((GRID_W, LANES), F32)
            for dc in range(RPB_COLS):
                t = jnp.where(diff == dc, rpb_ref[(h * RPB_ROWS + dr) * RPB_COLS + dc], t)
            tile_scr[dr] = jnp.where(valid, t * LOG2E, NEG_INF)
        for c in range(NA_ROWS // 2, NA_ROWS // 2 + NA_ROWS):
            d0 = 2 * c - NA_ROWS
            tab_ref[0, h, c] = jnp.where(lo, tile_scr[d0], tile_scr[d0 + 1])
            tab_ref[1, h, c] = jnp.where(lo, tile_scr[d0 - 1 if d0 > 0 else RPB_ROWS], tile_scr[d0])
        return carry

    lax.fori_loop(0, N_HEADS, per_head, 0)


def _latent_block(jq, n_lat, qa_ref, ka_ref, va_ref, kca_scr, vca_scr,
                  qb_ref, kb_ref, vb_ref, cnk_ref, cnv_ref, g_ref, tab_scr, y_scr):
    lo = _lane_lo()
    q_rows = NA_Q_BLOCK // GRID_W
    starts = [min(max(jq * q_rows + local - NA_ROWS // 2, 0), n_lat // GRID_W - NA_ROWS) for local in range(q_rows)]
    t_lo = (starts[0] * GRID_W) // LANES
    t_hi = -((-(starts[-1] + NA_ROWS) * GRID_W) // LANES)
    keys = slice(t_lo * LANES, t_hi * LANES)
    kidx = t_lo * LANES + lax.broadcasted_iota(jnp.int32, (1, (t_hi - t_lo) * LANES), 1)
    for p in range(N_PAIRS):
        sl = slice(p * LANES, (p + 1) * LANES)
        o2 = []
        va = _with_ones(va_ref[0, :, sl])
        vca = _with_ones(vca_scr[:, sl])
        for hh in (2 * p, 2 * p + 1):
            hs = slice(hh * LANES, (hh + 1) * LANES)
            q = qa_ref[0, :, hs]
            o2.append(_attend([(_dot_nt(q, ka_ref[0, :, hs]), va, False),
                               (_dot_nt(q, kca_scr[:, hs]), vca, False)]))
        oa = jnp.where(lo, o2[0], o2[1])
        y_scr[:, sl] = (oa * g_ref[0, :, sl]).astype(BF16)

        qb = qb_ref[0, :, sl]
        kb = kb_ref[0, keys, sl]
        vb = _with_ones(vb_ref[0, keys, sl])
        kcb = cnk_ref[0, sl, :].astype(BF16)
        vcb = _with_ones(cnv_ref[0, sl, :].astype(BF16), transposed=True)
        o2 = []
        for half in (0, 1):
            head = 2 * p + half
            qm = jnp.where(lo if half == 0 else jnp.logical_not(lo), qb, jnp.zeros_like(qb))
            s_lat = _dot_nt(qm, kb)
            rows = []
            for local in range(q_rows):
                par = 0 if local % 2 == 1 else 1
                c0 = (RPB_ROWS + par - local) // 2 - (q_rows // 2) * jq
                bias = jnp.concatenate([tab_scr[par, head, c0 + t] for t in range(t_lo, t_hi)], axis=1)
                r0 = starts[local] * GRID_W
                ok = (kidx >= r0) & (kidx < r0 + NA_ROWS * GRID_W)
                bias = bias + jnp.where(ok, 0.0, NEG_INF)
                rows.append(s_lat[local * GRID_W:(local + 1) * GRID_W] + bias)
            s_lat = jnp.concatenate(rows, axis=0)
            o2.append(_attend([(s_lat, vb, False), (_dot(qm, kcb), vcb, True)]))
        ob = jnp.where(lo, o2[0], o2[1])
        ys = slice(4 * LANES + p * LANES, 4 * LANES + (p + 1) * LANES)
        y_scr[:, ys] = (ob * g_ref[0, :, ys]).astype(BF16)


def _s0b_kernel(rpb_ref, x_ref, m_ref, qa_ref, ka_ref, va_ref, qb_ref, kb_ref, vb_ref, g_ref,
                cckv_ref, ckr_ref, cnk_ref, cnv_ref, wkk_ref, wkv_ref, kg_ref, wout_ref,
                xo_ref, kca_scr, vca_scr, tile_scr, tab_scr, y_scr):
    b = pl.program_id(0)
    j = pl.program_id(1)
    lo = _lane_lo()
    n_lat = ka_ref.shape[1]

    @pl.when((b == 0) & (j == 0))
    def _():
        _build_bias_table(rpb_ref, tile_scr, tab_scr)

    @pl.when(j == 0)
    def _():
        kr_t = jnp.concatenate([jnp.zeros((NOPE_A, PAST_LEN), F32), ckr_ref[0],
                                jnp.zeros((LANES - QK_A, PAST_LEN), F32)], axis=0)
        keys, vals = _mla_keys(cckv_ref[0].astype(BF16), kr_t.T, wkk_ref, wkv_ref, kg_ref[...])
        for hh in range(N_HEADS):
            kca_scr[:, hh * LANES:(hh + 1) * LANES] = keys[hh]
        vca_scr[...] = vals

    for jq in range(n_lat // NA_Q_BLOCK):
        pl.when(j == jq)(functools.partial(_latent_block, jq, n_lat, qa_ref, ka_ref, va_ref, kca_scr, vca_scr,
                                           qb_ref, kb_ref, vb_ref, cnk_ref, cnv_ref, g_ref, tab_scr, y_scr))

    d = x_ref.shape[-1]
    gate = m_ref[pl.ds(1 + b, 1), 2 * d:]
    xo_ref[0] = x_ref[0] + gate * _dot(y_scr[...], wout_ref[...])


def _sample_even_attn(rpb, x, m, qa, ka, va, qb, kb, vb, g, cckv, ckr, cnk, cnv, wkk, wkv, kg, wout):
    nb, s, _ = x.shape
    nq = s // NA_Q_BLOCK

    def blk(w):
        return pl.BlockSpec((1, NA_Q_BLOCK, w), lambda b, j: (b, j, 0))

    def per_batch(a):
        return pl.BlockSpec((1,) + a.shape[1:], lambda b, j: (b, 0, 0))

    return pl.pallas_call(
        _s0b_kernel,
        grid=(nb, nq),
        in_specs=[pl.BlockSpec(memory_space=pltpu.SMEM), blk(D_MODEL), _spec(m),
                  blk(1024), per_batch(ka), per_batch(va), blk(512), per_batch(kb), per_batch(vb), blk(1024),
                  per_batch(cckv), per_batch(ckr), per_batch(cnk), per_batch(cnv),
                  _full(wkk.shape), _full(wkv.shape), _spec(kg), _full(wout.shape)],
        out_specs=blk(D_MODEL),
        out_shape=jax.ShapeDtypeStruct(x.shape, F32),
        scratch_shapes=[pltpu.VMEM((PAST_LEN, N_HEADS * LANES), BF16),
                        pltpu.VMEM((PAST_LEN, N_HEADS * HEAD_DIM), BF16),
                        pltpu.VMEM((RPB_ROWS + 1, GRID_W, LANES), F32),
                        pltpu.VMEM((2, N_HEADS, BIAS_CHUNKS, GRID_W, LANES), F32),
                        pltpu.VMEM((NA_Q_BLOCK, D_MODEL), BF16)],
        compiler_params=pltpu.CompilerParams(dimension_semantics=("arbitrary", "arbitrary"),
                                             vmem_limit_bytes=VMEM_LIMIT),
        name="sample_even_attn",
    )(rpb, x, _arr(m), qa, ka, va, qb, kb, vb, g, cckv, ckr, cnk, cnv, wkk, wkv, _arr(kg), wout)


def _s1a_kernel(x_ref, m_ref, ng_ref, win_ref, gqg_ref, gkg_ref, sqg_ref, skg_ref, cos_ref, sin_ref,
                qc_ref, kc_ref, vc_ref, qd_ref, kd_ref, vd_ref, g_ref):
    b = pl.program_id(0)
    lo = _lane_lo()
    partner = _rope_matrix2(HEAD_DIM, HEAD_DIM, 0)
    swap = _swap_matrix2()[:LANES]
    hb = _modulate(x_ref[0], ng_ref[...], m_ref[pl.ds(1 + b, 1), :])[0].astype(BF16)
    cos, sin = cos_ref[...], sin_ref[...]
    sc = HEAD_DIM ** -0.5 * LOG2E

    def rope(t):
        return t * cos + _lane_mix(t, partner) * sin

    branches = ((O_QC, O_KC, O_GC, gqg_ref, gkg_ref, qc_ref, kc_ref, vc_ref, 0),
                (O_QD, O_KD, O_GD, sqg_ref, skg_ref, qd_ref, kd_ref, vd_ref, 4 * LANES))
    for oq, ok, og, qg_ref, kg_ref, q_out, k_out, v_out, goff in branches:
        zq = _dot(hb, win_ref[:, oq:oq + 4 * LANES])
        zkv = _dot(hb, win_ref[:, ok:ok + 2 * LANES])
        qg = qg_ref[...] * sc
        for p in range(N_PAIRS):
            sl = slice(p * LANES, (p + 1) * LANES)
            q_out[0, :, sl] = rope(_rms_halves(zq[:, sl], qg, lo)).astype(BF16)
        kn = rope(_rms_halves(zkv[:, :LANES], kg_ref[...], lo))
        v = zkv[:, LANES:]
        for out, val in ((k_out, kn.astype(BF16)), (v_out, v.astype(BF16))):
            out[0, :, 0:LANES] = val
            out[0, :, LANES:2 * LANES] = _dot(val, swap).astype(BF16)
        g_ref[0, :, goff:goff + 4 * LANES] = _silu(_dot(hb, win_ref[:, og:og + 4 * LANES]))


def _sample_odd_proj(x, m, ng, win, gqg, gkg, sqg, skg, cos, sin):
    nb, s, _ = x.shape
    nq = s // PROJ_BLOCK
    ins = (m, ng, win, gqg, gkg, sqg, skg)
    tab = pl.BlockSpec((PROJ_BLOCK, LANES), lambda b, j: (j, 0))

    def blk(w):
        return pl.BlockSpec((1, PROJ_BLOCK, w), lambda b, j: (b, j, 0))

    def shp(w, dt):
        return jax.ShapeDtypeStruct((nb, s, w), dt)

    return pl.pallas_call(
        _s1a_kernel,
        grid=(nb, nq),
        in_specs=[blk(D_MODEL)] + [_spec(a) for a in ins] + [tab, tab],
        out_specs=[blk(512), blk(256), blk(256), blk(512), blk(256), blk(256), blk(1024)],
        out_shape=[shp(512, BF16), shp(256, BF16), shp(256, BF16), shp(512, BF16), shp(256, BF16),
                   shp(256, BF16), shp(1024, F32)],
        compiler_params=pltpu.CompilerParams(dimension_semantics=("arbitrary", "arbitrary"),
                                             vmem_limit_bytes=VMEM_LIMIT),
        name="sample_odd_proj",
    )(x, *map(_arr, ins), cos, sin)


def _s1b_kernel(sink_ref, x_ref, m_ref, qc_ref, kc_ref, vc_ref, qd_ref, kd_ref, vd_ref, g_ref,
                cgk_ref, cgv_ref, csk_ref, csv_ref, wout_ref, xo_ref, y_scr):
    b = pl.program_id(0)
    j = pl.program_id(1)
    lo = _lane_lo()
    n_lat = kc_ref.shape[1]
    win_keys = Q_BLOCK + 2 * SWA_HALF

    def ctx_pair(ref, values=False):
        a = ref[0].astype(BF16)
        pair = (a, _swap_halves(a))
        return tuple(_with_ones(t, transposed=True) for t in pair) if values else pair

    cgk, cgv, csk, csv = ctx_pair(cgk_ref), ctx_pair(cgv_ref, True), ctx_pair(csk_ref), ctx_pair(csv_ref, True)
    vcs = [_with_ones(vc_ref[0, :, w * LANES:(w + 1) * LANES]) for w in (0, 1)]

    ks = pl.multiple_of(jnp.clip(j * Q_BLOCK - SWA_HALF, 0, n_lat - win_keys), SWA_HALF)
    qpos = j * Q_BLOCK + lax.broadcasted_iota(jnp.int32, (Q_BLOCK, win_keys), 0)
    kpos = ks + lax.broadcasted_iota(jnp.int32, (Q_BLOCK, win_keys), 1)
    band = jnp.abs(qpos - kpos) <= SWA_HALF
    vds = [_with_ones(vd_ref[0, pl.ds(ks, win_keys), w * LANES:(w + 1) * LANES]) for w in (0, 1)]

    for p in range(N_PAIRS):
        sl = slice(p * LANES, (p + 1) * LANES)
        kv = p // 2
        qc = qc_ref[0, :, sl]
        qd = qd_ref[0, :, sl]
        oc2, od2 = [], []
        for half in (0, 1):
            swap = 0 if kv == half else 1
            ws = slice(swap * LANES, (swap + 1) * LANES)
            keep = lo if half == 0 else jnp.logical_not(lo)
            qm = jnp.where(keep, qc, jnp.zeros_like(qc))
            oc2.append(_attend([(_dot_nt(qm, kc_ref[0, :, ws]), vcs[swap], False),
                                (_dot(qm, cgk[swap]), cgv[swap], True)]))
            qm = jnp.where(keep, qd, jnp.zeros_like(qd))
            s_loc = jnp.where(band, _dot_nt(qm, kd_ref[0, pl.ds(ks, win_keys), ws]), NEG_INF)
            od2.append(_attend([(s_loc, vds[swap], False),
                                (_dot(qm, csk[swap]), csv[swap], True)], sink_ref[2 * p + half] * LOG2E))
        y_scr[:, sl] = (jnp.where(lo, oc2[0], oc2[1]) * g_ref[0, :, sl]).astype(BF16)
        ys = slice(4 * LANES + p * LANES, 4 * LANES + (p + 1) * LANES)
        y_scr[:, ys] = (jnp.where(lo, od2[0], od2[1]) * g_ref[0, :, ys]).astype(BF16)

    d = x_ref.shape[-1]
    gate = m_ref[pl.ds(1 + b, 1), 2 * d:]
    xo_ref[0] = x_ref[0] + gate * _dot(y_scr[...], wout_ref[...])


def _sample_odd_attn(sink, x, m, qc, kc, vc, qd, kd, vd, g, cgk, cgv, csk, csv, wout):
    nb, s, _ = x.shape
    nq = s // Q_BLOCK

    def blk(w):
        return pl.BlockSpec((1, Q_BLOCK, w), lambda b, j: (b, j, 0))

    def per_batch(a):
        return pl.BlockSpec((1,) + a.shape[1:], lambda b, j: (b, 0, 0))

    return pl.pallas_call(
        _s1b_kernel,
        grid=(nb, nq),
        in_specs=[pl.BlockSpec(memory_space=pltpu.SMEM), blk(D_MODEL), _spec(m),
                  blk(512), per_batch(kc), per_batch(vc), blk(512), per_batch(kd), per_batch(vd), blk(1024),
                  per_batch(cgk), per_batch(cgv), per_batch(csk), per_batch(csv), _full(wout.shape)],
        out_specs=blk(D_MODEL),
        out_shape=jax.ShapeDtypeStruct(x.shape, F32),
        scratch_shapes=[pltpu.VMEM((Q_BLOCK, D_MODEL), BF16)],
        compiler_params=pltpu.CompilerParams(dimension_semantics=("arbitrary", "arbitrary"),
                                             vmem_limit_bytes=VMEM_LIMIT),
        name="sample_odd_attn",
    )(sink, x, _arr(m), qc, kc, vc, qd, kd, vd, g, cgk, cgv, csk, csv, wout)


WEIGHT_PREP_STEPS = 8
EVEN_IN_CHUNKS = 6
COND_PREP_ROWS = 256
BF16_SUBLANES = 16


G_MLA_Q, G_MLA_K, G_NA_Q, G_NA_K, G_GQA_Q, G_GQA_K, G_SWA_Q, G_SWA_K, N_GAINS = range(9)


GAIN_WIDTHS = (QK_A, QK_A) + (HEAD_DIM,) * 6


def _cond_prep_kernel(n_cond, ct_ref, wm_ref, bm_ref, ng_ref, gains_ref, mo_ref, gt_ref, ngt_ref):
    _mod_step(n_cond, pl.program_id(0) == 0, bm_ref[0:1, :], ct_ref, wm_ref, mo_ref)
    gt_ref[...] = jnp.zeros(gt_ref.shape, F32)
    start = 0
    for r, w in enumerate(GAIN_WIDTHS):
        g = gains_ref[:, start:start + w]
        start += w
        for off in range(0, LANES - w + 1, w):
            gt_ref[r, :, off:off + w] = g
    for layer in range(ngt_ref.shape[0]):
        ngt_ref[layer] = ng_ref[layer:layer + 1, :]


def _cond_prep(cond_t, n_cond, w_mod, b_mod, norm_g, gains):
    assert len(gains) == N_GAINS and tuple(g.shape[-1] for g in gains) == GAIN_WIDTHS
    gains_row = jnp.concatenate([g.reshape(1, -1) for g in gains], axis=1)
    tk = COND_PREP_ROWS
    return pl.pallas_call(
        functools.partial(_cond_prep_kernel, n_cond),
        grid=(D_MODEL // tk,),
        in_specs=[pl.BlockSpec((tk, 8), lambda k: (k, 0)),
                  pl.BlockSpec((None, tk, 3 * D_MODEL), lambda k: (0, k, 0)),
                  _full(b_mod.shape), _full(norm_g.shape), _full(gains_row.shape)],
        out_specs=[_full((8, 3 * D_MODEL)), _full((N_GAINS, 1, LANES)), _full((norm_g.shape[0], 1, D_MODEL))],
        out_shape=[jax.ShapeDtypeStruct((8, 3 * D_MODEL), F32), jax.ShapeDtypeStruct((N_GAINS, 1, LANES), F32),
                   jax.ShapeDtypeStruct((norm_g.shape[0], 1, D_MODEL), F32)],
        compiler_params=pltpu.CompilerParams(dimension_semantics=("arbitrary",)),
        name="cond_prep",
    )(cond_t, w_mod, b_mod, norm_g, gains_row)


def _weight_prep_kernel(wie_ref, woe_ref, wqu_ref, wkv_ref, win_e_ref, wout_e_ref, wq_ref, wkk_ref, wkvv_ref):
    win_e_ref[...] = wie_ref[...].astype(BF16)
    wout_e_ref[...] = woe_ref[0].astype(BF16)

    wq_ref[...] = jnp.zeros(wq_ref.shape, BF16)
    for h in range(N_HEADS):
        wq_ref[:, h * LANES:h * LANES + QK_A] = wqu_ref[0, :, h * QK_A:(h + 1) * QK_A].astype(BF16)
    lo = _lane_lo()
    for p in range(N_PAIRS):
        a = wkv_ref[0, :, (2 * p) * LANES:(2 * p + 1) * LANES]
        c = wkv_ref[0, :, (2 * p + 1) * LANES:(2 * p + 2) * LANES]
        wkk_ref[:, (2 * p) * LANES:(2 * p + 1) * LANES] = jnp.where(lo, a, 0.0).astype(BF16)
        wkk_ref[:, (2 * p + 1) * LANES:(2 * p + 2) * LANES] = jnp.where(lo, c, 0.0).astype(BF16)
        wkvv_ref[:, p * LANES:(p + 1) * LANES] = jnp.where(lo, pltpu.roll(a, HEAD_DIM, 1), c).astype(BF16)


def _weight_prep(w_in_even_t, w_out_even, w_q_up, w_kv_up):
    n = WEIGHT_PREP_STEPS
    ins = (w_out_even, w_q_up, w_kv_up)
    out_cols = (D_MODEL, N_HEADS * LANES, N_HEADS * LANES, N_HEADS * HEAD_DIM)
    out_rows = (D_MODEL, Q_RANK, KV_RANK, KV_RANK)
    te = E_END // EVEN_IN_CHUNKS
    assert te * EVEN_IN_CHUNKS == E_END and te % BF16_SUBLANES == 0 and EVEN_IN_CHUNKS <= n
    even_spec = pl.BlockSpec((te, D_MODEL), lambda i: (jnp.minimum(i, EVEN_IN_CHUNKS - 1), 0))
    return pl.pallas_call(
        _weight_prep_kernel,
        grid=(n,),
        in_specs=[even_spec] + [pl.BlockSpec((1, a.shape[1] // n, a.shape[2]), lambda i: (0, i, 0)) for a in ins],
        out_specs=[even_spec] + [pl.BlockSpec((r // n, c), lambda i: (i, 0)) for r, c in zip(out_rows, out_cols)],
        out_shape=[jax.ShapeDtypeStruct((E_END, D_MODEL), BF16)]
        + [jax.ShapeDtypeStruct((r, c), BF16) for r, c in zip(out_rows, out_cols)],
        compiler_params=pltpu.CompilerParams(dimension_semantics=("arbitrary",), vmem_limit_bytes=VMEM_LIMIT),
        name="weight_prep",
    )(w_in_even_t, *ins)


def _feature_major(c):
    b, h, l, d = c.shape
    return jnp.swapaxes(c, -1, -2).reshape(b, h * d, l)


def _token_major(c):
    return jnp.swapaxes(c, -1, -2)


def _rope_tables(s, rot_dim, period, start):
    quarter = rot_dim // 4
    t = np.arange(s)
    inv = ROPE_THETA ** (-np.arange(quarter, dtype=np.float64) / quarter)
    row = (t // GRID_W).astype(np.float64)[:, None] * inv
    col = (t % GRID_W).astype(np.float64)[:, None] * inv
    ang = np.concatenate([row, col], axis=-1)
    cos, sin = np.cos(ang), np.sin(ang)
    pre = np.ones((s, start))
    post = np.zeros((s, period - start - rot_dim))
    c = np.concatenate([pre, cos, cos, post], axis=-1)
    sn = np.concatenate([0 * pre, sin, sin, post], axis=-1)
    rep = LANES // period
    return jnp.asarray(np.tile(c, (1, rep)), F32), jnp.asarray(np.tile(sn, (1, rep)), F32)


def kernel(x_prompt, x_sample, cache_mla_ckv, cache_mla_krope, cache_na_k, cache_na_v, cache_gqa_k, cache_gqa_v, cache_swa_k, cache_swa_v, c, c_ctx, norm_g, w_mod, b_mod, w_in_even, mla_qa_g, w_q_up, mla_kva_g, w_kv_up, mla_q_g, mla_k_g, na_q_g, na_k_g, na_rpb, w_out_even, w_in_odd, gqa_q_g, gqa_k_g, swa_q_g, swa_k_g, swa_sink, w_out_odd):
    n_dec = x_sample.shape[0]
    assert w_mod.shape[0] == 2 and n_dec + 1 <= 8

    cond_t = jnp.concatenate([c_ctx[:, None], c.T, jnp.zeros((D_MODEL, 7 - n_dec), F32)], axis=1)
    n_cond = 1 + n_dec
    gains = (mla_q_g, mla_k_g, na_q_g, na_k_g, gqa_q_g, gqa_k_g, swa_q_g, swa_k_g)
    win_e, wout_e, wq, wkk, wkv = _weight_prep(jnp.swapaxes(w_in_even[0], 0, 1), w_out_even, w_q_up, w_kv_up)
    m_even, gt, ngt = _cond_prep(cond_t, n_cond, w_mod, b_mod, norm_g, gains)
    even = (_Row(ngt, 0), win_e, mla_qa_g, wq, mla_kva_g, wkk, wkv,
            _Row(gt, G_MLA_Q), _Row(gt, G_MLA_K), _Row(gt, G_NA_Q), _Row(gt, G_NA_K))
    sink = swa_sink[0].astype(F32)

    xp1, new_ckv, new_krope, new_na_k, new_na_v, m_odd, win_o, wout_o = _prompt_even(
        x_prompt, m_even, *even, wout_e, cond_t, n_cond, w_mod, b_mod, w_in_odd, w_out_odd)
    odd = (_Row(ngt, 1), win_o, _Row(gt, G_GQA_Q), _Row(gt, G_GQA_K), _Row(gt, G_SWA_Q), _Row(gt, G_SWA_K))
    xp2, new_gqa_k, new_gqa_v, new_swa_k, new_swa_v = _prompt_odd(sink, xp1, m_odd, *odd, wout_o)

    cos_e, sin_e = _rope_tables(DEC_SEQ, ROPE_A, LANES, NOPE_A)
    qa, ka, va, qbs, kbs, vbs, g0 = _sample_even_proj(x_sample, m_even, *even, cos_e, sin_e)
    ckr = jnp.swapaxes(cache_mla_krope[:, 0], -1, -2)
    xs1 = _sample_even_attn(na_rpb[0].reshape(-1), x_sample, m_even, qa, ka, va, qbs, kbs, vbs, g0,
                            cache_mla_ckv[:, 0], ckr, _feature_major(cache_na_k[:, 0]),
                            _feature_major(cache_na_v[:, 0]), wkk, wkv, _Row(gt, G_MLA_K), wout_e)
    cos_o, sin_o = _rope_tables(DEC_SEQ, HEAD_DIM, HEAD_DIM, 0)
    qc, kc, vc, qd, kd, vd, g1 = _sample_odd_proj(xs1, m_odd, *odd, cos_o, sin_o)
    xs2 = _sample_odd_attn(sink, xs1, m_odd, qc, kc, vc, qd, kd, vd, g1,
                           _feature_major(cache_gqa_k[:, 0]), _feature_major(cache_gqa_v[:, 0]),
                           _feature_major(cache_swa_k[:, 0]), _feature_major(cache_swa_v[:, 0]), wout_o)

    caches = (new_krope, new_na_k, new_na_v, new_gqa_k, new_gqa_v, new_swa_k, new_swa_v)
    return (xp2, xs2, new_ckv) + tuple(_token_major(c) for c in caches)
```

```python
import functools
from typing import NamedTuple

import jax
import jax.numpy as jnp
import numpy as np
from jax import lax
from jax.experimental import pallas as pl
from jax.experimental.pallas import tpu as pltpu

F32 = jnp.float32
BF16 = jnp.bfloat16

D_MODEL = 1024
SEQ = 256
DEC_SEQ = 1024
PAST_LEN = 256
GRID_W = 64
HEAD_DIM = 64
Q_RANK = 256
KV_RANK = 128
NOPE_A = 64
ROPE_A = 32
QK_A = NOPE_A + ROPE_A
N_HEADS = 8
NA_ROWS = 8
NA_COLS = 16
SWA_HALF = 128
ROPE_THETA = 10000.0
EPS = 1e-6
NEG_INF = -1e30
LOG2E = 1.4426950408889634

LANES = 128
Q_BLOCK = 512
NA_Q_BLOCK = 256
PROJ_BLOCK = 512
PROMPT_BATCHES_PER_STEP = 2
PROMPT_ODD_BATCHES_PER_STEP = 4
N_PAIRS = N_HEADS // 2
RPB_ROWS = 2 * NA_ROWS - 1
RPB_COLS = 2 * NA_COLS - 1
BIAS_CHUNKS = 16
VMEM_LIMIT = 48 * 1024 * 1024

E_QLAT, E_CKV, E_KROPE, E_GA, E_QB, E_KB, E_VB, E_GB, E_END = 0, 256, 384, 416, 928, 1440, 1952, 2464, 2976
O_QC, O_KC, O_VC, O_GC, O_QD, O_KD, O_VD, O_GD, O_END = 0, 512, 640, 768, 1280, 1792, 1920, 2048, 2560


def _dot(a, b):
    return lax.dot_general(a, b, (((1,), (0,)), ((), ())), preferred_element_type=F32)


def _dot_nt(a, b):
    return lax.dot_general(a, b, (((1,), (1,)), ((), ())), preferred_element_type=F32)


def _silu(x):
    return x / (1.0 + jnp.exp(-x))


def _rms(x, g, n):
    ss = jnp.sum(x * x, axis=-1, keepdims=True)
    return x * lax.rsqrt(ss / n + EPS) * g


def _rms_halves(x, g2, lo):
    x2 = x * x
    s_lo = jnp.sum(jnp.where(lo, x2, 0.0), axis=-1, keepdims=True)
    s_hi = jnp.sum(jnp.where(lo, 0.0, x2), axis=-1, keepdims=True)
    r = jnp.where(lo, lax.rsqrt(s_lo / HEAD_DIM + EPS), lax.rsqrt(s_hi / HEAD_DIM + EPS))
    return x * r * g2


def _modulate(x, g, m):
    d = x.shape[-1]
    xn = x * lax.rsqrt(jnp.mean(x * x, axis=-1, keepdims=True) + EPS) * g
    return xn * (1.0 + m[:, d:2 * d]) + m[:, :d], m[:, 2 * d:]


def _split_lanes(x):
    hi = x.astype(BF16)
    lo = (x - hi.astype(F32)).astype(BF16)
    return jnp.concatenate([hi, lo], axis=1)


def _lane_matrix2(entries):
    i = lax.broadcasted_iota(jnp.int32, (LANES, LANES), 0)
    j = lax.broadcasted_iota(jnp.int32, (LANES, LANES), 1)
    m = entries(i, j).astype(BF16)
    return jnp.concatenate([m, m], axis=0)


def _rope_matrix2(rot_dim, period, start):
    half = rot_dim // 2

    def entries(i, j):
        pos = jnp.bitwise_and(j, period - 1) - start
        neg = (pos >= 0) & (pos < half) & (i == j + half)
        plus = (pos >= half) & (pos < rot_dim) & (i == j - half)
        return jnp.where(neg, -1.0, jnp.where(plus, 1.0, 0.0))

    return _lane_matrix2(entries)


def _swap_matrix2():
    return _lane_matrix2(lambda i, j: jnp.where(i == jnp.bitwise_xor(j, HEAD_DIM), 1.0, 0.0))


def _lane_mix(x, m2):
    return _dot(_split_lanes(x), m2)


def _with_ones(v, transposed=False):
    if transposed:
        return jnp.concatenate([v, jnp.ones((LANES, v.shape[1]), v.dtype)], axis=0)
    return jnp.concatenate([v, jnp.ones((v.shape[0], LANES), v.dtype)], axis=1)


def _attend(parts, sink=None):
    mx = None
    for s, _, _ in parts:
        pm = jnp.max(s, axis=-1, keepdims=True)
        mx = pm if mx is None else jnp.maximum(mx, pm)
    if sink is not None:
        mx = jnp.maximum(mx, sink)
    acc, den = None, None
    for s, v, v_t in parts:
        e = jnp.exp2(s - mx)
        po = (_dot_nt if v_t else _dot)(e.astype(BF16), v)
        acc = po if acc is None else acc + po
        if po.shape[1] == LANES:
            ps = jnp.sum(e, axis=-1, keepdims=True)
            den = ps if den is None else den + ps
    if den is None:
        den = acc[:, LANES:]
    if sink is not None:
        den = den + jnp.exp2(sink - mx)
    return acc[:, :LANES] * (1.0 / den)


def _lane_lo():
    return lax.broadcasted_iota(jnp.int32, (1, LANES), 1) < HEAD_DIM


def _store_pair_transposed(ref, bi, p, x):
    xt = x.T
    ref[bi, 0, 2 * p] = xt[:HEAD_DIM]
    ref[bi, 0, 2 * p + 1] = xt[HEAD_DIM:]


def _rope_key_slab(win_ref):
    d = win_ref.shape[1]
    return jnp.concatenate([jnp.zeros((NOPE_A, d), BF16), win_ref[E_KROPE:E_GA, :],
                            jnp.zeros((LANES - QK_A, d), BF16)], axis=0)


def _swap_halves(a):
    return jnp.concatenate([a[HEAD_DIM:], a[:HEAD_DIM]], axis=0)


def _mod_step(n_cond, is_first, bias_row, c_ref, w_ref, o_ref):
    @pl.when(is_first)
    def _():
        o_ref[:n_cond, :] = jnp.broadcast_to(bias_row, (n_cond, o_ref.shape[1]))
        o_ref[n_cond:, :] = jnp.zeros((o_ref.shape[0] - n_cond, o_ref.shape[1]), F32)

    s = _silu(c_ref[...])
    cols = [jnp.broadcast_to(s[:, r:r + 1], (s.shape[0], LANES)) for r in range(n_cond)]
    for t in range(w_ref.shape[1] // LANES):
        sl = slice(t * LANES, (t + 1) * LANES)
        w = w_ref[:, sl]
        for r in range(n_cond):
            o_ref[r:r + 1, sl] += jnp.sum(w * cols[r], axis=0, keepdims=True)


def _mla_keys(cb, kr, wkk_ref, wkv_ref, kg, rope=None):
    kk = _dot(cb, wkk_ref[...])
    keys = []
    for h in range(N_HEADS):
        k = _rms(kk[:, h * LANES:(h + 1) * LANES] + kr, kg, QK_A)
        if rope is not None:
            k = rope(k)
        keys.append(k.astype(BF16))
    return keys, _dot(cb, wkv_ref[...]).astype(BF16)


def _p0_kernel(n_cond, x_ref, m_ref, ng_ref, win_ref, qag_ref, wq_ref, kvag_ref, wkk_ref, wkv_ref, qg_ref, kg_ref,
               naqg_ref, nakg_ref, wout_ref, ct_ref, wm_ref, bm_ref, wio_ref, woo_ref,
               xo_ref, ckv_ref, krope_ref, nak_ref, nav_ref, mo_ref, wino_ref, wouto_ref, y_scr):
    _mod_step(n_cond, pl.program_id(0) == 0, bm_ref[1:2, :], ct_ref, wm_ref, mo_ref)
    wino_ref[...] = wio_ref[0].astype(BF16)
    wouto_ref[...] = woo_ref[0].astype(BF16)

    nbs = x_ref.shape[0]
    x = x_ref[...].reshape(nbs * SEQ, D_MODEL)
    h, gate = _modulate(x, ng_ref[...], m_ref[0:1, :])
    hb = h.astype(BF16)
    lo = _lane_lo()
    hi = jnp.logical_not(lo)
    rows = [slice(bi * SEQ, (bi + 1) * SEQ) for bi in range(nbs)]

    qln = _rms(_dot_nt(hb, win_ref[E_QLAT:E_CKV, :]), qag_ref[...], Q_RANK).astype(BF16)
    q_all = _dot(qln, wq_ref[...])
    ckv_n = _rms(_dot_nt(hb, win_ref[E_CKV:E_KROPE, :]), kvag_ref[...], KV_RANK)
    kr = _dot_nt(hb, _rope_key_slab(win_ref))
    for bi, rs in enumerate(rows):
        ckv_ref[bi, 0] = ckv_n[rs]
        krope_ref[bi, 0] = kr[rs].T[NOPE_A:QK_A]
    keys, vals = _mla_keys(ckv_n.astype(BF16), kr, wkk_ref, wkv_ref, kg_ref[...])
    qg = qg_ref[...] * (QK_A ** -0.5 * LOG2E)

    ga = _dot_nt(hb, win_ref[E_GA:E_QB, :])
    zq = _dot_nt(hb, win_ref[E_QB:E_KB, :])
    zk = _dot_nt(hb, win_ref[E_KB:E_VB, :])
    zv = _dot_nt(hb, win_ref[E_VB:E_GB, :])
    gb = _dot_nt(hb, win_ref[E_GB:E_END, :])
    naqg = naqg_ref[...] * (HEAD_DIM ** -0.5 * LOG2E)

    for p in range(N_PAIRS):
        sl = slice(p * LANES, (p + 1) * LANES)
        ys = slice(4 * LANES + p * LANES, 4 * LANES + (p + 1) * LANES)
        qhs = [_rms(q_all[:, hh * LANES:(hh + 1) * LANES], qg, QK_A).astype(BF16) for hh in (2 * p, 2 * p + 1)]
        qb = _rms_halves(zq[:, sl], naqg, lo)
        kb = _rms_halves(zk[:, sl], nakg_ref[...], lo)
        vb = zv[:, sl]
        kbb, vbb = kb.astype(BF16), vb.astype(BF16)
        va = vals[:, sl]
        qms = [jnp.where(keep, qb, 0.0).astype(BF16) for keep in (lo, hi)]
        for bi, rs in enumerate(rows):
            o2 = [_attend([(_dot_nt(qhs[i][rs], keys[2 * p + i][rs]), va[rs], False)]) for i in (0, 1)]
            y_scr[rs, sl] = (jnp.where(lo, o2[0], o2[1]) * _silu(ga[rs, sl])).astype(BF16)
            _store_pair_transposed(nak_ref, bi, p, kb[rs])
            _store_pair_transposed(nav_ref, bi, p, vb[rs])
            o2 = [_attend([(_dot_nt(qms[i][rs], kbb[rs]), vbb[rs], False)]) for i in (0, 1)]
            y_scr[rs, ys] = (jnp.where(lo, o2[0], o2[1]) * _silu(gb[rs, sl])).astype(BF16)

    xo_ref[...] = (x + gate * _dot(y_scr[...], wout_ref[...])).reshape(nbs, SEQ, D_MODEL)


def _full(shape):
    n = len(shape)
    return pl.BlockSpec(shape, lambda *_: (0,) * n, pipeline_mode=pl.Buffered(1))


class _Row(NamedTuple):
    table: jax.Array
    row: int


def _spec(a):
    if isinstance(a, _Row):
        idx = (a.row,) + (0,) * (a.table.ndim - 1)
        return pl.BlockSpec((None,) + a.table.shape[1:], lambda *_: idx, pipeline_mode=pl.Buffered(1))
    return _full(a.shape)


def _arr(a):
    return a.table if isinstance(a, _Row) else a


def _prompt_even(x, m, ng, win, qag, wq, kvag, wkk, wkv, qg, kg, naqg, nakg, wout,
                 cond_t, n_cond, w_mod, b_mod, w_in_odd, w_out_odd):
    nb = x.shape[0]
    nbs = PROMPT_BATCHES_PER_STEP
    steps = nb // nbs
    assert nb % nbs == 0 and D_MODEL % (BF16_SUBLANES * steps) == 0
    tr = D_MODEL // steps
    ins = (m, ng, win, qag, wq, kvag, wkk, wkv, qg, kg, naqg, nakg, wout)
    return pl.pallas_call(
        functools.partial(_p0_kernel, n_cond),
        grid=(steps,),
        in_specs=[pl.BlockSpec((nbs, SEQ, D_MODEL), lambda b: (b, 0, 0))] + [_spec(a) for a in ins]
        + [pl.BlockSpec((tr, 8), lambda b: (b, 0)),
           pl.BlockSpec((None, tr, 3 * D_MODEL), lambda b: (1, b, 0)),
           _full(b_mod.shape),
           pl.BlockSpec((1, tr, O_END), lambda b: (0, b, 0)),
           pl.BlockSpec((1, tr, D_MODEL), lambda b: (0, b, 0))],
        out_specs=[pl.BlockSpec((nbs, SEQ, D_MODEL), lambda b: (b, 0, 0)),
                   pl.BlockSpec((nbs, 1, SEQ, KV_RANK), lambda b: (b, 0, 0, 0)),
                   pl.BlockSpec((nbs, 1, ROPE_A, SEQ), lambda b: (b, 0, 0, 0)),
                   pl.BlockSpec((nbs, 1, N_HEADS, HEAD_DIM, SEQ), lambda b: (b, 0, 0, 0, 0)),
                   pl.BlockSpec((nbs, 1, N_HEADS, HEAD_DIM, SEQ), lambda b: (b, 0, 0, 0, 0)),
                   pl.BlockSpec((8, 3 * D_MODEL), lambda b: (0, 0)),
                   pl.BlockSpec((tr, O_END), lambda b: (b, 0)),
                   pl.BlockSpec((tr, D_MODEL), lambda b: (b, 0))],
        out_shape=[jax.ShapeDtypeStruct((nb, SEQ, D_MODEL), F32),
                   jax.ShapeDtypeStruct((nb, 1, SEQ, KV_RANK), F32),
                   jax.ShapeDtypeStruct((nb, 1, ROPE_A, SEQ), F32),
                   jax.ShapeDtypeStruct((nb, 1, N_HEADS, HEAD_DIM, SEQ), F32),
                   jax.ShapeDtypeStruct((nb, 1, N_HEADS, HEAD_DIM, SEQ), F32),
                   jax.ShapeDtypeStruct((8, 3 * D_MODEL), F32),
                   jax.ShapeDtypeStruct((D_MODEL, O_END), BF16),
                   jax.ShapeDtypeStruct((D_MODEL, D_MODEL), BF16)],
        scratch_shapes=[pltpu.VMEM((nbs * SEQ, D_MODEL), BF16)],
        compiler_params=pltpu.CompilerParams(dimension_semantics=("arbitrary",), vmem_limit_bytes=VMEM_LIMIT),
        name="prompt_even",
    )(x, *map(_arr, ins), cond_t, w_mod, b_mod, w_in_odd, w_out_odd)


def _gqa_pair_operands(k, v, kg2, lo):
    kn = _rms_halves(k, kg2, lo)
    return kn, (kn.astype(BF16), pltpu.roll(kn, HEAD_DIM, 1).astype(BF16)), \
        (_with_ones(v.astype(BF16)), _with_ones(pltpu.roll(v, HEAD_DIM, 1).astype(BF16)))


def _p1_kernel(sink_ref, x_ref, m_ref, ng_ref, win_ref, gqg_ref, gkg_ref, sqg_ref, skg_ref, wout_ref,
               xo_ref, gk_ref, gv_ref, sk_ref, sv_ref, y_scr):
    nbs = x_ref.shape[0]
    x = x_ref[...].reshape(nbs * SEQ, D_MODEL)
    h, gate = _modulate(x, ng_ref[...], m_ref[0:1, :])
    hb = h.astype(BF16)
    lo = _lane_lo()
    hi = jnp.logical_not(lo)
    sc = HEAD_DIM ** -0.5 * LOG2E
    rows = [slice(bi * SEQ, (bi + 1) * SEQ) for bi in range(nbs)]

    branches = ((O_QC, O_KC, O_VC, O_GC, gqg_ref, gkg_ref, gk_ref, gv_ref, False, 0),
                (O_QD, O_KD, O_VD, O_GD, sqg_ref, skg_ref, sk_ref, sv_ref, True, 4 * LANES))
    for oq, ok, ov, og, qg_ref, kg_ref, ck_ref, cv_ref, has_sink, yoff in branches:
        zq = _dot(hb, win_ref[:, oq:oq + 4 * LANES])
        zkv = _dot(hb, win_ref[:, ok:ok + 2 * LANES])
        zg = _dot(hb, win_ref[:, og:og + 4 * LANES])
        v = zkv[:, LANES:]
        kn, ks, vs = _gqa_pair_operands(zkv[:, :LANES], v, kg_ref[...], lo)
        for bi, rs in enumerate(rows):
            _store_pair_transposed(ck_ref, bi, 0, kn[rs])
            _store_pair_transposed(cv_ref, bi, 0, v[rs])
        qg = qg_ref[...] * sc
        for p in range(N_PAIRS):
            sl = slice(p * LANES, (p + 1) * LANES)
            qn = _rms_halves(zq[:, sl], qg, lo)
            qms = [jnp.where(keep, qn, 0.0).astype(BF16) for keep in (lo, hi)]
            kv = p // 2
            for bi, rs in enumerate(rows):
                o2 = []
                for half in (0, 1):
                    swap = 0 if kv == half else 1
                    sink = sink_ref[2 * p + half] * LOG2E if has_sink else None
                    o2.append(_attend([(_dot_nt(qms[half][rs], ks[swap][rs]), vs[swap][rs], False)], sink))
                o = jnp.where(lo, o2[0], o2[1])
                y_scr[rs, yoff + p * LANES:yoff + (p + 1) * LANES] = (o * _silu(zg[rs, sl])).astype(BF16)

    xo_ref[...] = (x + gate * _dot(y_scr[...], wout_ref[...])).reshape(nbs, SEQ, D_MODEL)


def _prompt_odd(sink, x, m, ng, win, gqg, gkg, sqg, skg, wout):
    nb = x.shape[0]
    nbs = PROMPT_ODD_BATCHES_PER_STEP
    assert nb % nbs == 0
    ins = (m, ng, win, gqg, gkg, sqg, skg, wout)
    cache_spec = pl.BlockSpec((nbs, 1, 2, HEAD_DIM, SEQ), lambda b: (b, 0, 0, 0, 0))
    cache_shape = jax.ShapeDtypeStruct((nb, 1, 2, HEAD_DIM, SEQ), F32)
    return pl.pallas_call(
        _p1_kernel,
        grid=(nb // nbs,),
        in_specs=[pl.BlockSpec(memory_space=pltpu.SMEM),
                  pl.BlockSpec((nbs, SEQ, D_MODEL), lambda b: (b, 0, 0))] + [_spec(a) for a in ins],
        out_specs=[pl.BlockSpec((nbs, SEQ, D_MODEL), lambda b: (b, 0, 0))] + [cache_spec] * 4,
        out_shape=[jax.ShapeDtypeStruct((nb, SEQ, D_MODEL), F32)] + [cache_shape] * 4,
        scratch_shapes=[pltpu.VMEM((nbs * SEQ, D_MODEL), BF16)],
        compiler_params=pltpu.CompilerParams(dimension_semantics=("arbitrary",), vmem_limit_bytes=VMEM_LIMIT),
        name="prompt_odd",
    )(sink, x, *map(_arr, ins))


def _s0a_kernel(x_ref, m_ref, ng_ref, win_ref, qag_ref, wq_ref, kvag_ref, wkk_ref, wkv_ref, qg_ref, kg_ref,
                naqg_ref, nakg_ref, cos_ref, sin_ref,
                qa_ref, ka_ref, va_ref, qb_ref, kb_ref, vb_ref, g_ref):
    b = pl.program_id(0)
    lo = _lane_lo()
    partner = _rope_matrix2(ROPE_A, LANES, NOPE_A)
    hb = _modulate(x_ref[0], ng_ref[...], m_ref[pl.ds(1 + b, 1), :])[0].astype(BF16)
    cos, sin = cos_ref[...], sin_ref[...]

    qln = _rms(_dot_nt(hb, win_ref[E_QLAT:E_CKV, :]), qag_ref[...], Q_RANK).astype(BF16)
    q_all = _dot(qln, wq_ref[...])
    ckv_n = _rms(_dot_nt(hb, win_ref[E_CKV:E_KROPE, :]), kvag_ref[...], KV_RANK)
    kr = _dot_nt(hb, _rope_key_slab(win_ref))
    cb = ckv_n.astype(BF16)
    kk = _dot(cb, wkk_ref[...])
    va_ref[0] = _dot(cb, wkv_ref[...]).astype(BF16)
    zq = _dot_nt(hb, win_ref[E_QB:E_KB, :])
    zk = _dot_nt(hb, win_ref[E_KB:E_VB, :])
    vb_ref[0] = _dot_nt(hb, win_ref[E_VB:E_GB, :]).astype(BF16)
    g_ref[0, :, 0:4 * LANES] = _silu(_dot_nt(hb, win_ref[E_GA:E_QB, :]))
    g_ref[0, :, 4 * LANES:8 * LANES] = _silu(_dot_nt(hb, win_ref[E_GB:E_END, :]))

    qg = qg_ref[...] * (QK_A ** -0.5 * LOG2E)
    kg = kg_ref[...]
    k_partner = _lane_mix(kr * kg, partner) * sin
    for hh in range(N_HEADS):
        sl = slice(hh * LANES, (hh + 1) * LANES)
        qn = _rms(q_all[:, sl], qg, QK_A)
        qa_ref[0, :, sl] = (qn * cos + _lane_mix(qn, partner) * sin).astype(BF16)
        k_raw = kk[:, sl] + kr
        k_inv = lax.rsqrt(jnp.sum(k_raw * k_raw, axis=-1, keepdims=True) / QK_A + EPS)
        ka_ref[0, :, sl] = ((k_raw * kg * cos + k_partner) * k_inv).astype(BF16)
    naqg = naqg_ref[...] * (HEAD_DIM ** -0.5 * LOG2E)
    for p in range(N_PAIRS):
        sl = slice(p * LANES, (p + 1) * LANES)
        qb_ref[0, :, sl] = _rms_halves(zq[:, sl], naqg, lo).astype(BF16)
        kb_ref[0, :, sl] = _rms_halves(zk[:, sl], nakg_ref[...], lo).astype(BF16)


def _sample_even_proj(x, m, ng, win, qag, wq, kvag, wkk, wkv, qg, kg, naqg, nakg, cos, sin):
    nb, s, _ = x.shape
    nq = s // PROJ_BLOCK
    ins = (m, ng, win, qag, wq, kvag, wkk, wkv, qg, kg, naqg, nakg)
    tab = pl.BlockSpec((PROJ_BLOCK, LANES), lambda b, j: (j, 0))

    def blk(w):
        return pl.BlockSpec((1, PROJ_BLOCK, w), lambda b, j: (b, j, 0))

    def shp(w, dt):
        return jax.ShapeDtypeStruct((nb, s, w), dt)

    return pl.pallas_call(
        _s0a_kernel,
        grid=(nb, nq),
        in_specs=[blk(D_MODEL)] + [_spec(a) for a in ins] + [tab, tab],
        out_specs=[blk(1024), blk(1024), blk(512), blk(512), blk(512), blk(512), blk(1024)],
        out_shape=[shp(1024, BF16), shp(1024, BF16), shp(512, BF16), shp(512, BF16), shp(512, BF16),
                   shp(512, BF16), shp(1024, F32)],
        compiler_params=pltpu.CompilerParams(dimension_semantics=("arbitrary", "arbitrary"),
                                             vmem_limit_bytes=VMEM_LIMIT),
        name="sample_even_proj",
    )(x, *map(_arr, ins), cos, sin)


def _build_bias_table(rpb_ref, tile_scr, tab_ref):
    qc = lax.broadcasted_iota(jnp.int32, (GRID_W, LANES), 0)
    lane = lax.broadcasted_iota(jnp.int32, (GRID_W, LANES), 1)
    kc = jnp.bitwise_and(lane, GRID_W - 1)
    lo = lane < GRID_W
    diff = kc - qc + (NA_COLS - 1)
    cs = jnp.clip(qc - NA_COLS // 2, 0, GRID_W - NA_COLS)
    valid = (kc >= cs) & (kc < cs + NA_COLS)
    tab_ref[...] = jnp.zeros(tab_ref.shape, F32)
    tile_scr[RPB_ROWS] = jnp.zeros((GRID_W, LANES), F32)

    def per_head(h, carry):
        for dr in range(RPB_ROWS):
            t = jnp.zeros((GRID_W, LANES), F32)
            for dc in range(RPB_COLS):
                t = jnp.where(diff == dc, rpb_ref[(h * RPB_ROWS + dr) * RPB_COLS + dc], t)
            tile_scr[dr] = jnp.where(valid, t * LOG2E, NEG_INF)
        for c in range(NA_ROWS // 2, NA_ROWS // 2 + NA_ROWS):
            d0 = 2 * c - NA_ROWS
            tab_ref[0, h, c] = jnp.where(lo, tile_scr[d0], tile_scr[d0 + 1])
            tab_ref[1, h, c] = jnp.where(lo, tile_scr[d0 - 1 if d0 > 0 else RPB_ROWS], tile_scr[d0])
        return carry

    lax.fori_loop(0, N_HEADS, per_head, 0)


def _latent_block(jq, n_lat, qa_ref, ka_ref, va_ref, kca_scr, vca_scr,
                  qb_ref, kb_ref, vb_ref, cnk_ref, cnv_ref, g_ref, tab_scr, y_scr):
    lo = _lane_lo()
    q_rows = NA_Q_BLOCK // GRID_W
    starts = [min(max(jq * q_rows + local - NA_ROWS // 2, 0), n_lat // GRID_W - NA_ROWS) for local in range(q_rows)]
    t_lo = (starts[0] * GRID_W) // LANES
    t_hi = -((-(starts[-1] + NA_ROWS) * GRID_W) // LANES)
    keys = slice(t_lo * LANES, t_hi * LANES)
    kidx = t_lo * LANES + lax.broadcasted_iota(jnp.int32, (1, (t_hi - t_lo) * LANES), 1)
    for p in range(N_PAIRS):
        sl = slice(p * LANES, (p + 1) * LANES)
        o2 = []
        va = _with_ones(va_ref[0, :, sl])
        vca = _with_ones(vca_scr[:, sl])
        for hh in (2 * p, 2 * p + 1):
            hs = slice(hh * LANES, (hh + 1) * LANES)
            q = qa_ref[0, :, hs]
            o2.append(_attend([(_dot_nt(q, ka_ref[0, :, hs]), va, False),
                               (_dot_nt(q, kca_scr[:, hs]), vca, False)]))
        oa = jnp.where(lo, o2[0], o2[1])
        y_scr[:, sl] = (oa * g_ref[0, :, sl]).astype(BF16)

        qb = qb_ref[0, :, sl]
        kb = kb_ref[0, keys, sl]
        vb = _with_ones(vb_ref[0, keys, sl])
        kcb = cnk_ref[0, sl, :].astype(BF16)
        vcb = _with_ones(cnv_ref[0, sl, :].astype(BF16), transposed=True)
        o2 = []
        for half in (0, 1):
            head = 2 * p + half
            qm = jnp.where(lo if half == 0 else jnp.logical_not(lo), qb, jnp.zeros_like(qb))
            s_lat = _dot_nt(qm, kb)
            rows = []
            for local in range(q_rows):
                par = 0 if local % 2 == 1 else 1
                c0 = (RPB_ROWS + par - local) // 2 - (q_rows // 2) * jq
                bias = jnp.concatenate([tab_scr[par, head, c0 + t] for t in range(t_lo, t_hi)], axis=1)
                r0 = starts[local] * GRID_W
                ok = (kidx >= r0) & (kidx < r0 + NA_ROWS * GRID_W)
                bias = bias + jnp.where(ok, 0.0, NEG_INF)
                rows.append(s_lat[local * GRID_W:(local + 1) * GRID_W] + bias)
            s_lat = jnp.concatenate(rows, axis=0)
            o2.append(_attend([(s_lat, vb, False), (_dot(qm, kcb), vcb, True)]))
        ob = jnp.where(lo, o2[0], o2[1])
        ys = slice(4 * LANES + p * LANES, 4 * LANES + (p + 1) * LANES)
        y_scr[:, ys] = (ob * g_ref[0, :, ys]).astype(BF16)


def _s0b_kernel(rpb_ref, x_ref, m_ref, qa_ref, ka_ref, va_ref, qb_ref, kb_ref, vb_ref, g_ref,
                cckv_ref, ckr_ref, cnk_ref, cnv_ref, wkk_ref, wkv_ref, kg_ref, wout_ref,
                xo_ref, kca_scr, vca_scr, tile_scr, tab_scr, y_scr):
    j = pl.program_id(0)
    b = pl.program_id(1)
    n_lat = ka_ref.shape[1]

    @pl.when((b == 0) & (j == 0))
    def _():
        _build_bias_table(rpb_ref, tile_scr, tab_scr)

    @pl.when(j == 0)
    def _():
        kr_t = jnp.concatenate([jnp.zeros((NOPE_A, PAST_LEN), F32), ckr_ref[0],
                                jnp.zeros((LANES - QK_A, PAST_LEN), F32)], axis=0)
        keys, vals = _mla_keys(cckv_ref[0].astype(BF16), kr_t.T, wkk_ref, wkv_ref, kg_ref[...])
        for hh in range(N_HEADS):
            kca_scr[b, :, hh * LANES:(hh + 1) * LANES] = keys[hh]
        vca_scr[b] = vals

    for jq in range(n_lat // NA_Q_BLOCK):
        pl.when(j == jq)(functools.partial(_latent_block, jq, n_lat, qa_ref, ka_ref, va_ref, kca_scr.at[b],
                                           vca_scr.at[b], qb_ref, kb_ref, vb_ref, cnk_ref, cnv_ref, g_ref,
                                           tab_scr, y_scr))

    d = x_ref.shape[-1]
    gate = m_ref[pl.ds(1 + b, 1), 2 * d:]
    xo_ref[0] = x_ref[0] + gate * _dot(y_scr[...], wout_ref[...])


def _sample_even_attn(rpb, x, m, qa, ka, va, qb, kb, vb, g, cckv, ckr, cnk, cnv, wkk, wkv, kg, wout):
    nb, s, _ = x.shape
    nq = s // NA_Q_BLOCK

    def blk(w):
        return pl.BlockSpec((1, NA_Q_BLOCK, w), lambda j, b: (b, j, 0))

    def per_batch(a):
        return pl.BlockSpec((1,) + a.shape[1:], lambda j, b: (b, 0, 0))

    return pl.pallas_call(
        _s0b_kernel,
        grid=(nq, nb),
        in_specs=[pl.BlockSpec(memory_space=pltpu.SMEM), blk(D_MODEL), _spec(m),
                  blk(1024), per_batch(ka), per_batch(va), blk(512), per_batch(kb), per_batch(vb), blk(1024),
                  per_batch(cckv), per_batch(ckr), per_batch(cnk), per_batch(cnv),
                  _full(wkk.shape), _full(wkv.shape), _spec(kg), _full(wout.shape)],
        out_specs=blk(D_MODEL),
        out_shape=jax.ShapeDtypeStruct(x.shape, F32),
        scratch_shapes=[pltpu.VMEM((nb, PAST_LEN, N_HEADS * LANES), BF16),
                        pltpu.VMEM((nb, PAST_LEN, N_HEADS * HEAD_DIM), BF16),
                        pltpu.VMEM((RPB_ROWS + 1, GRID_W, LANES), F32),
                        pltpu.VMEM((2, N_HEADS, BIAS_CHUNKS, GRID_W, LANES), F32),
                        pltpu.VMEM((NA_Q_BLOCK, D_MODEL), BF16)],
        compiler_params=pltpu.CompilerParams(dimension_semantics=("arbitrary", "arbitrary"),
                                             vmem_limit_bytes=VMEM_LIMIT),
        name="sample_even_attn",
    )(rpb, x, _arr(m), qa, ka, va, qb, kb, vb, g, cckv, ckr, cnk, cnv, wkk, wkv, _arr(kg), wout)


def _s1a_kernel(x_ref, m_ref, ng_ref, win_ref, gqg_ref, gkg_ref, sqg_ref, skg_ref, cos_ref, sin_ref,
                qc_ref, kc_ref, vc_ref, qd_ref, kd_ref, vd_ref, g_ref):
    b = pl.program_id(0)
    lo = _lane_lo()
    partner = _rope_matrix2(HEAD_DIM, HEAD_DIM, 0)
    swap = _swap_matrix2()[:LANES]
    hb = _modulate(x_ref[0], ng_ref[...], m_ref[pl.ds(1 + b, 1), :])[0].astype(BF16)
    cos, sin = cos_ref[...], sin_ref[...]
    sc = HEAD_DIM ** -0.5 * LOG2E

    def rope(t):
        return t * cos + _lane_mix(t, partner) * sin

    branches = ((O_QC, O_KC, O_GC, gqg_ref, gkg_ref, qc_ref, kc_ref, vc_ref, 0),
                (O_QD, O_KD, O_GD, sqg_ref, skg_ref, qd_ref, kd_ref, vd_ref, 4 * LANES))
    for oq, ok, og, qg_ref, kg_ref, q_out, k_out, v_out, goff in branches:
        zq = _dot(hb, win_ref[:, oq:oq + 4 * LANES])
        zkv = _dot(hb, win_ref[:, ok:ok + 2 * LANES])
        qg = qg_ref[...] * sc
        for p in range(N_PAIRS):
            sl = slice(p * LANES, (p + 1) * LANES)
            q_out[0, :, sl] = rope(_rms_halves(zq[:, sl], qg, lo)).astype(BF16)
        kn = rope(_rms_halves(zkv[:, :LANES], kg_ref[...], lo))
        v = zkv[:, LANES:]
        for out, val in ((k_out, kn.astype(BF16)), (v_out, v.astype(BF16))):
            out[0, :, 0:LANES] = val
            out[0, :, LANES:2 * LANES] = _dot(val, swap).astype(BF16)
        g_ref[0, :, goff:goff + 4 * LANES] = _silu(_dot(hb, win_ref[:, og:og + 4 * LANES]))


def _sample_odd_proj(x, m, ng, win, gqg, gkg, sqg, skg, cos, sin):
    nb, s, _ = x.shape
    nq = s // PROJ_BLOCK
    ins = (m, ng, win, gqg, gkg, sqg, skg)
    tab = pl.BlockSpec((PROJ_BLOCK, LANES), lambda b, j: (j, 0))

    def blk(w):
        return pl.BlockSpec((1, PROJ_BLOCK, w), lambda b, j: (b, j, 0))

    def shp(w, dt):
        return jax.ShapeDtypeStruct((nb, s, w), dt)

    return pl.pallas_call(
        _s1a_kernel,
        grid=(nb, nq),
        in_specs=[blk(D_MODEL)] + [_spec(a) for a in ins] + [tab, tab],
        out_specs=[blk(512), blk(256), blk(256), blk(512), blk(256), blk(256), blk(1024)],
        out_shape=[shp(512, BF16), shp(256, BF16), shp(256, BF16), shp(512, BF16), shp(256, BF16),
                   shp(256, BF16), shp(1024, F32)],
        compiler_params=pltpu.CompilerParams(dimension_semantics=("arbitrary", "arbitrary"),
                                             vmem_limit_bytes=VMEM_LIMIT),
        name="sample_odd_proj",
    )(x, *map(_arr, ins), cos, sin)


def _s1b_kernel(sink_ref, x_ref, m_ref, qc_ref, kc_ref, vc_ref, qd_ref, kd_ref, vd_ref, g_ref,
                cgk_ref, cgv_ref, csk_ref, csv_ref, wout_ref, xo_ref, y_scr):
    b = pl.program_id(0)
    j = pl.program_id(1)
    lo = _lane_lo()
    n_lat = kc_ref.shape[1]
    win_keys = Q_BLOCK + 2 * SWA_HALF

    def ctx_pair(ref, values=False):
        a = ref[0].astype(BF16)
        pair = (a, _swap_halves(a))
        return tuple(_with_ones(t, transposed=True) for t in pair) if values else pair

    cgk, cgv, csk, csv = ctx_pair(cgk_ref), ctx_pair(cgv_ref, True), ctx_pair(csk_ref), ctx_pair(csv_ref, True)
    vcs = [_with_ones(vc_ref[0, :, w * LANES:(w + 1) * LANES]) for w in (0, 1)]

    ks = pl.multiple_of(jnp.clip(j * Q_BLOCK - SWA_HALF, 0, n_lat - win_keys), SWA_HALF)
    qpos = j * Q_BLOCK + lax.broadcasted_iota(jnp.int32, (Q_BLOCK, win_keys), 0)
    kpos = ks + lax.broadcasted_iota(jnp.int32, (Q_BLOCK, win_keys), 1)
    band = jnp.abs(qpos - kpos) <= SWA_HALF
    vds = [_with_ones(vd_ref[0, pl.ds(ks, win_keys), w * LANES:(w + 1) * LANES]) for w in (0, 1)]

    for p in range(N_PAIRS):
        sl = slice(p * LANES, (p + 1) * LANES)
        kv = p // 2
        qc = qc_ref[0, :, sl]
        qd = qd_ref[0, :, sl]
        oc2, od2 = [], []
        for half in (0, 1):
            swap = 0 if kv == half else 1
            ws = slice(swap * LANES, (swap + 1) * LANES)
            keep = lo if half == 0 else jnp.logical_not(lo)
            qm = jnp.where(keep, qc, jnp.zeros_like(qc))
            oc2.append(_attend([(_dot_nt(qm, kc_ref[0, :, ws]), vcs[swap], False),
                                (_dot(qm, cgk[swap]), cgv[swap], True)]))
            qm = jnp.where(keep, qd, jnp.zeros_like(qd))
            s_loc = jnp.where(band, _dot_nt(qm, kd_ref[0, pl.ds(ks, win_keys), ws]), NEG_INF)
            od2.append(_attend([(s_loc, vds[swap], False),
                                (_dot(qm, csk[swap]), csv[swap], True)], sink_ref[2 * p + half] * LOG2E))
        y_scr[:, sl] = (jnp.where(lo, oc2[0], oc2[1]) * g_ref[0, :, sl]).astype(BF16)
        ys = slice(4 * LANES + p * LANES, 4 * LANES + (p + 1) * LANES)
        y_scr[:, ys] = (jnp.where(lo, od2[0], od2[1]) * g_ref[0, :, ys]).astype(BF16)

    d = x_ref.shape[-1]
    gate = m_ref[pl.ds(1 + b, 1), 2 * d:]
    xo_ref[0] = x_ref[0] + gate * _dot(y_scr[...], wout_ref[...])


def _sample_odd_attn(sink, x, m, qc, kc, vc, qd, kd, vd, g, cgk, cgv, csk, csv, wout):
    nb, s, _ = x.shape
    nq = s // Q_BLOCK

    def blk(w):
        return pl.BlockSpec((1, Q_BLOCK, w), lambda b, j: (b, j, 0))

    def per_batch(a):
        return pl.BlockSpec((1,) + a.shape[1:], lambda b, j: (b, 0, 0))

    return pl.pallas_call(
        _s1b_kernel,
        grid=(nb, nq),
        in_specs=[pl.BlockSpec(memory_space=pltpu.SMEM), blk(D_MODEL), _spec(m),
                  blk(512), per_batch(kc), per_batch(vc), blk(512), per_batch(kd), per_batch(vd), blk(1024),
                  per_batch(cgk), per_batch(cgv), per_batch(csk), per_batch(csv), _full(wout.shape)],
        out_specs=blk(D_MODEL),
        out_shape=jax.ShapeDtypeStruct(x.shape, F32),
        scratch_shapes=[pltpu.VMEM((Q_BLOCK, D_MODEL), BF16)],
        compiler_params=pltpu.CompilerParams(dimension_semantics=("arbitrary", "arbitrary"),
                                             vmem_limit_bytes=VMEM_LIMIT),
        name="sample_odd_attn",
    )(sink, x, _arr(m), qc, kc, vc, qd, kd, vd, g, cgk, cgv, csk, csv, wout)


WEIGHT_PREP_STEPS = 8
EVEN_IN_CHUNKS = 6
COND_PREP_ROWS = 256
BF16_SUBLANES = 16


G_MLA_Q, G_MLA_K, G_NA_Q, G_NA_K, G_GQA_Q, G_GQA_K, G_SWA_Q, G_SWA_K, N_GAINS = range(9)


GAIN_WIDTHS = (QK_A, QK_A) + (HEAD_DIM,) * 6


def _cond_prep_kernel(n_cond, ct_ref, wm_ref, bm_ref, ng_ref, gains_ref, mo_ref, gt_ref, ngt_ref):
    _mod_step(n_cond, pl.program_id(0) == 0, bm_ref[0:1, :], ct_ref, wm_ref, mo_ref)
    gt_ref[...] = jnp.zeros(gt_ref.shape, F32)
    start = 0
    for r, w in enumerate(GAIN_WIDTHS):
        g = gains_ref[:, start:start + w]
        start += w
        for off in range(0, LANES - w + 1, w):
            gt_ref[r, :, off:off + w] = g
    for layer in range(ngt_ref.shape[0]):
        ngt_ref[layer] = ng_ref[layer:layer + 1, :]


def _cond_prep(cond_t, n_cond, w_mod, b_mod, norm_g, gains):
    assert len(gains) == N_GAINS and tuple(g.shape[-1] for g in gains) == GAIN_WIDTHS
    gains_row = jnp.concatenate([g.reshape(1, -1) for g in gains], axis=1)
    tk = COND_PREP_ROWS
    return pl.pallas_call(
        functools.partial(_cond_prep_kernel, n_cond),
        grid=(D_MODEL // tk,),
        in_specs=[pl.BlockSpec((tk, 8), lambda k: (k, 0)),
                  pl.BlockSpec((None, tk, 3 * D_MODEL), lambda k: (0, k, 0)),
                  _full(b_mod.shape), _full(norm_g.shape), _full(gains_row.shape)],
        out_specs=[_full((8, 3 * D_MODEL)), _full((N_GAINS, 1, LANES)), _full((norm_g.shape[0], 1, D_MODEL))],
        out_shape=[jax.ShapeDtypeStruct((8, 3 * D_MODEL), F32), jax.ShapeDtypeStruct((N_GAINS, 1, LANES), F32),
                   jax.ShapeDtypeStruct((norm_g.shape[0], 1, D_MODEL), F32)],
        compiler_params=pltpu.CompilerParams(dimension_semantics=("arbitrary",)),
        name="cond_prep",
    )(cond_t, w_mod, b_mod, norm_g, gains_row)


def _weight_prep_kernel(wie_ref, woe_ref, wqu_ref, wkv_ref, win_e_ref, wout_e_ref, wq_ref, wkk_ref, wkvv_ref):
    win_e_ref[...] = wie_ref[...].astype(BF16)
    wout_e_ref[...] = woe_ref[0].astype(BF16)

    wq_ref[...] = jnp.zeros(wq_ref.shape, BF16)
    for h in range(N_HEADS):
        wq_ref[:, h * LANES:h * LANES + QK_A] = wqu_ref[0, :, h * QK_A:(h + 1) * QK_A].astype(BF16)
    lo = _lane_lo()
    for p in range(N_PAIRS):
        a = wkv_ref[0, :, (2 * p) * LANES:(2 * p + 1) * LANES]
        c = wkv_ref[0, :, (2 * p + 1) * LANES:(2 * p + 2) * LANES]
        wkk_ref[:, (2 * p) * LANES:(2 * p + 1) * LANES] = jnp.where(lo, a, 0.0).astype(BF16)
        wkk_ref[:, (2 * p + 1) * LANES:(2 * p + 2) * LANES] = jnp.where(lo, c, 0.0).astype(BF16)
        wkvv_ref[:, p * LANES:(p + 1) * LANES] = jnp.where(lo, pltpu.roll(a, HEAD_DIM, 1), c).astype(BF16)


def _weight_prep(w_in_even_t, w_out_even, w_q_up, w_kv_up):
    n = WEIGHT_PREP_STEPS
    ins = (w_out_even, w_q_up, w_kv_up)
    out_cols = (D_MODEL, N_HEADS * LANES, N_HEADS * LANES, N_HEADS * HEAD_DIM)
    out_rows = (D_MODEL, Q_RANK, KV_RANK, KV_RANK)
    te = E_END // EVEN_IN_CHUNKS
    assert te * EVEN_IN_CHUNKS == E_END and te % BF16_SUBLANES == 0 and EVEN_IN_CHUNKS <= n
    even_spec = pl.BlockSpec((te, D_MODEL), lambda i: (jnp.minimum(i, EVEN_IN_CHUNKS - 1), 0))
    return pl.pallas_call(
        _weight_prep_kernel,
        grid=(n,),
        in_specs=[even_spec] + [pl.BlockSpec((1, a.shape[1] // n, a.shape[2]), lambda i: (0, i, 0)) for a in ins],
        out_specs=[even_spec] + [pl.BlockSpec((r // n, c), lambda i: (i, 0)) for r, c in zip(out_rows, out_cols)],
        out_shape=[jax.ShapeDtypeStruct((E_END, D_MODEL), BF16)]
        + [jax.ShapeDtypeStruct((r, c), BF16) for r, c in zip(out_rows, out_cols)],
        compiler_params=pltpu.CompilerParams(dimension_semantics=("arbitrary",), vmem_limit_bytes=VMEM_LIMIT),
        name="weight_prep",
    )(w_in_even_t, *ins)


def _feature_major(c):
    b, h, l, d = c.shape
    return jnp.swapaxes(c, -1, -2).reshape(b, h * d, l)


def _token_major(c):
    return jnp.swapaxes(c, -1, -2)


def _rope_tables(s, rot_dim, period, start):
    quarter = rot_dim // 4
    t = np.arange(s)
    inv = ROPE_THETA ** (-np.arange(quarter, dtype=np.float64) / quarter)
    row = (t // GRID_W).astype(np.float64)[:, None] * inv
    col = (t % GRID_W).astype(np.float64)[:, None] * inv
    ang = np.concatenate([row, col], axis=-1)
    cos, sin = np.cos(ang), np.sin(ang)
    pre = np.ones((s, start))
    post = np.zeros((s, period - start - rot_dim))
    c = np.concatenate([pre, cos, cos, post], axis=-1)
    sn = np.concatenate([0 * pre, sin, sin, post], axis=-1)
    rep = LANES // period
    return jnp.asarray(np.tile(c, (1, rep)), F32), jnp.asarray(np.tile(sn, (1, rep)), F32)


def kernel(x_prompt, x_sample, cache_mla_ckv, cache_mla_krope, cache_na_k, cache_na_v, cache_gqa_k, cache_gqa_v, cache_swa_k, cache_swa_v, c, c_ctx, norm_g, w_mod, b_mod, w_in_even, mla_qa_g, w_q_up, mla_kva_g, w_kv_up, mla_q_g, mla_k_g, na_q_g, na_k_g, na_rpb, w_out_even, w_in_odd, gqa_q_g, gqa_k_g, swa_q_g, swa_k_g, swa_sink, w_out_odd):
    n_dec = x_sample.shape[0]
    assert w_mod.shape[0] == 2 and n_dec + 1 <= 8

    cond_t = jnp.concatenate([c_ctx[:, None], c.T, jnp.zeros((D_MODEL, 7 - n_dec), F32)], axis=1)
    n_cond = 1 + n_dec
    gains = (mla_q_g, mla_k_g, na_q_g, na_k_g, gqa_q_g, gqa_k_g, swa_q_g, swa_k_g)
    win_e, wout_e, wq, wkk, wkv = _weight_prep(jnp.swapaxes(w_in_even[0], 0, 1), w_out_even, w_q_up, w_kv_up)
    m_even, gt, ngt = _cond_prep(cond_t, n_cond, w_mod, b_mod, norm_g, gains)
    even = (_Row(ngt, 0), win_e, mla_qa_g, wq, mla_kva_g, wkk, wkv,
            _Row(gt, G_MLA_Q), _Row(gt, G_MLA_K), _Row(gt, G_NA_Q), _Row(gt, G_NA_K))
    sink = swa_sink[0].astype(F32)

    xp1, new_ckv, new_krope, new_na_k, new_na_v, m_odd, win_o, wout_o = _prompt_even(
        x_prompt, m_even, *even, wout_e, cond_t, n_cond, w_mod, b_mod, w_in_odd, w_out_odd)
    odd = (_Row(ngt, 1), win_o, _Row(gt, G_GQA_Q), _Row(gt, G_GQA_K), _Row(gt, G_SWA_Q), _Row(gt, G_SWA_K))
    xp2, new_gqa_k, new_gqa_v, new_swa_k, new_swa_v = _prompt_odd(sink, xp1, m_odd, *odd, wout_o)

    cos_e, sin_e = _rope_tables(DEC_SEQ, ROPE_A, LANES, NOPE_A)
    qa, ka, va, qbs, kbs, vbs, g0 = _sample_even_proj(x_sample, m_even, *even, cos_e, sin_e)
    ckr = jnp.swapaxes(cache_mla_krope[:, 0], -1, -2)
    xs1 = _sample_even_attn(na_rpb[0].reshape(-1), x_sample, m_even, qa, ka, va, qbs, kbs, vbs, g0,
                            cache_mla_ckv[:, 0], ckr, _feature_major(cache_na_k[:, 0]),
                            _feature_major(cache_na_v[:, 0]), wkk, wkv, _Row(gt, G_MLA_K), wout_e)
    cos_o, sin_o = _rope_tables(DEC_SEQ, HEAD_DIM, HEAD_DIM, 0)
    qc, kc, vc, qd, kd, vd, g1 = _sample_odd_proj(xs1, m_odd, *odd, cos_o, sin_o)
    xs2 = _sample_odd_attn(sink, xs1, m_odd, qc, kc, vc, qd, kd, vd, g1,
                           _feature_major(cache_gqa_k[:, 0]), _feature_major(cache_gqa_v[:, 0]),
                           _feature_major(cache_swa_k[:, 0]), _feature_major(cache_swa_v[:, 0]), wout_o)

    caches = (new_krope, new_na_k, new_na_v, new_gqa_k, new_gqa_v, new_swa_k, new_swa_v)
    return (xp2, xs2, new_ckv) + tuple(_token_major(c) for c in caches)
```

```python
import functools
from typing import NamedTuple

import jax
import jax.numpy as jnp
import numpy as np
from jax import lax
from jax.experimental import pallas as pl
from jax.experimental.pallas import tpu as pltpu

F32 = jnp.float32
BF16 = jnp.bfloat16

D_MODEL = 1024
SEQ = 256
DEC_SEQ = 1024
PAST_LEN = 256
GRID_W = 64
HEAD_DIM = 64
Q_RANK = 256
KV_RANK = 128
NOPE_A = 64
ROPE_A = 32
QK_A = NOPE_A + ROPE_A
N_HEADS = 8
NA_ROWS = 8
NA_COLS = 16
SWA_HALF = 128
ROPE_THETA = 10000.0
EPS = 1e-6
NEG_INF = -1e30
LOG2E = 1.4426950408889634

LANES = 128
Q_BLOCK = 512
NA_Q_BLOCK = 256
PROJ_BLOCK = 512
PROMPT_BATCHES_PER_STEP = 2
PROMPT_ODD_BATCHES_PER_STEP = 4
N_PAIRS = N_HEADS // 2
RPB_ROWS = 2 * NA_ROWS - 1
RPB_COLS = 2 * NA_COLS - 1
BIAS_CHUNKS = 16
VMEM_LIMIT = 48 * 1024 * 1024

E_QLAT, E_CKV, E_KROPE, E_GA, E_QB, E_KB, E_VB, E_GB, E_END = 0, 256, 384, 416, 928, 1440, 1952, 2464, 2976
O_QC, O_KC, O_VC, O_GC, O_QD, O_KD, O_VD, O_GD, O_END = 0, 512, 640, 768, 1280, 1792, 1920, 2048, 2560


def _dot(a, b):
    return lax.dot_general(a, b, (((1,), (0,)), ((), ())), preferred_element_type=F32)


def _dot_nt(a, b):
    return lax.dot_general(a, b, (((1,), (1,)), ((), ())), preferred_element_type=F32)


def _silu(x):
    return x / (1.0 + jnp.exp(-x))


def _rms(x, g, n):
    ss = jnp.sum(x * x, axis=-1, keepdims=True)
    return x * lax.rsqrt(ss / n + EPS) * g


def _rms_halves(x, g2, lo):
    x2 = x * x
    s_lo = jnp.sum(jnp.where(lo, x2, 0.0), axis=-1, keepdims=True)
    s_hi = jnp.sum(jnp.where(lo, 0.0, x2), axis=-1, keepdims=True)
    r = jnp.where(lo, lax.rsqrt(s_lo / HEAD_DIM + EPS), lax.rsqrt(s_hi / HEAD_DIM + EPS))
    return x * r * g2


def _modulate(x, g, m):
    d = x.shape[-1]
    xn = x * lax.rsqrt(jnp.mean(x * x, axis=-1, keepdims=True) + EPS) * g
    return xn * (1.0 + m[:, d:2 * d]) + m[:, :d], m[:, 2 * d:]


def _split_lanes(x):
    hi = x.astype(BF16)
    lo = (x - hi.astype(F32)).astype(BF16)
    return jnp.concatenate([hi, lo], axis=1)


def _lane_matrix2(entries):
    i = lax.broadcasted_iota(jnp.int32, (LANES, LANES), 0)
    j = lax.broadcasted_iota(jnp.int32, (LANES, LANES), 1)
    m = entries(i, j).astype(BF16)
    return jnp.concatenate([m, m], axis=0)


def _rope_matrix2(rot_dim, period, start):
    half = rot_dim // 2

    def entries(i, j):
        pos = jnp.bitwise_and(j, period - 1) - start
        neg = (pos >= 0) & (pos < half) & (i == j + half)
        plus = (pos >= half) & (pos < rot_dim) & (i == j - half)
        return jnp.where(neg, -1.0, jnp.where(plus, 1.0, 0.0))

    return _lane_matrix2(entries)


def _swap_matrix2():
    return _lane_matrix2(lambda i, j: jnp.where(i == jnp.bitwise_xor(j, HEAD_DIM), 1.0, 0.0))


def _lane_mix(x, m2):
    return _dot(_split_lanes(x), m2)


def _with_ones(v, transposed=False):
    if transposed:
        return jnp.concatenate([v, jnp.ones((LANES, v.shape[1]), v.dtype)], axis=0)
    return jnp.concatenate([v, jnp.ones((v.shape[0], LANES), v.dtype)], axis=1)


def _attend(parts, sink=None):
    mx = None
    for s, _, _ in parts:
        pm = jnp.max(s, axis=-1, keepdims=True)
        mx = pm if mx is None else jnp.maximum(mx, pm)
    if sink is not None:
        mx = jnp.maximum(mx, sink)
    acc, den = None, None
    for s, v, v_t in parts:
        e = jnp.exp2(s - mx)
        po = (_dot_nt if v_t else _dot)(e.astype(BF16), v)
        acc = po if acc is None else acc + po
        if po.shape[1] == LANES:
            ps = jnp.sum(e, axis=-1, keepdims=True)
            den = ps if den is None else den + ps
    if den is None:
        den = acc[:, LANES:]
    if sink is not None:
        den = den + jnp.exp2(sink - mx)
    return acc[:, :LANES] * (1.0 / den)


def _lane_lo():
    return lax.broadcasted_iota(jnp.int32, (1, LANES), 1) < HEAD_DIM


def _store_pair_transposed(ref, bi, p, x):
    xt = x.T
    ref[bi, 0, 2 * p] = xt[:HEAD_DIM]
    ref[bi, 0, 2 * p + 1] = xt[HEAD_DIM:]


def _rope_key_slab(win_ref):
    d = win_ref.shape[1]
    return jnp.concatenate([jnp.zeros((NOPE_A, d), BF16), win_ref[E_KROPE:E_GA, :],
                            jnp.zeros((LANES - QK_A, d), BF16)], axis=0)


def _swap_halves(a):
    return jnp.concatenate([a[HEAD_DIM:], a[:HEAD_DIM]], axis=0)


def _mod_step(n_cond, is_first, bias_row, c_ref, w_ref, o_ref):
    @pl.when(is_first)
    def _():
        o_ref[:n_cond, :] = jnp.broadcast_to(bias_row, (n_cond, o_ref.shape[1]))
        o_ref[n_cond:, :] = jnp.zeros((o_ref.shape[0] - n_cond, o_ref.shape[1]), F32)

    s = _silu(c_ref[...])
    cols = [jnp.broadcast_to(s[:, r:r + 1], (s.shape[0], LANES)) for r in range(n_cond)]
    for t in range(w_ref.shape[1] // LANES):
        sl = slice(t * LANES, (t + 1) * LANES)
        w = w_ref[:, sl]
        for r in range(n_cond):
            o_ref[r:r + 1, sl] += jnp.sum(w * cols[r], axis=0, keepdims=True)


def _mla_keys(cb, kr, wkk_ref, wkv_ref, kg, rope=None):
    kk = _dot(cb, wkk_ref[...])
    keys = []
    for h in range(N_HEADS):
        k = _rms(kk[:, h * LANES:(h + 1) * LANES] + kr, kg, QK_A)
        if rope is not None:
            k = rope(k)
        keys.append(k.astype(BF16))
    return keys, _dot(cb, wkv_ref[...]).astype(BF16)


def _p0_kernel(n_cond, x_ref, m_ref, ng_ref, win_ref, qag_ref, wq_ref, kvag_ref, wkk_ref, wkv_ref, qg_ref, kg_ref,
               naqg_ref, nakg_ref, wout_ref, ct_ref, wm_ref, bm_ref, wio_ref, woo_ref,
               xo_ref, ckv_ref, krope_ref, nak_ref, nav_ref, mo_ref, wino_ref, wouto_ref, y_scr):
    _mod_step(n_cond, pl.program_id(0) == 0, bm_ref[1:2, :], ct_ref, wm_ref, mo_ref)
    wino_ref[...] = wio_ref[0].astype(BF16)
    wouto_ref[...] = woo_ref[0].astype(BF16)

    nbs = x_ref.shape[0]
    x = x_ref[...].reshape(nbs * SEQ, D_MODEL)
    h, gate = _modulate(x, ng_ref[...], m_ref[0:1, :])
    hb = h.astype(BF16)
    lo = _lane_lo()
    hi = jnp.logical_not(lo)
    rows = [slice(bi * SEQ, (bi + 1) * SEQ) for bi in range(nbs)]

    qln = _rms(_dot_nt(hb, win_ref[E_QLAT:E_CKV, :]), qag_ref[...], Q_RANK).astype(BF16)
    q_all = _dot(qln, wq_ref[...])
    ckv_n = _rms(_dot_nt(hb, win_ref[E_CKV:E_KROPE, :]), kvag_ref[...], KV_RANK)
    kr = _dot_nt(hb, _rope_key_slab(win_ref))
    for bi, rs in enumerate(rows):
        ckv_ref[bi, 0] = ckv_n[rs]
        krope_ref[bi, 0] = kr[rs].T[NOPE_A:QK_A]
    keys, vals = _mla_keys(ckv_n.astype(BF16), kr, wkk_ref, wkv_ref, kg_ref[...])
    qg = qg_ref[...] * (QK_A ** -0.5 * LOG2E)

    ga = _dot_nt(hb, win_ref[E_GA:E_QB, :])
    zq = _dot_nt(hb, win_ref[E_QB:E_KB, :])
    zk = _dot_nt(hb, win_ref[E_KB:E_VB, :])
    zv = _dot_nt(hb, win_ref[E_VB:E_GB, :])
    gb = _dot_nt(hb, win_ref[E_GB:E_END, :])
    naqg = naqg_ref[...] * (HEAD_DIM ** -0.5 * LOG2E)

    for p in range(N_PAIRS):
        sl = slice(p * LANES, (p + 1) * LANES)
        ys = slice(4 * LANES + p * LANES, 4 * LANES + (p + 1) * LANES)
        qhs = [_rms(q_all[:, hh * LANES:(hh + 1) * LANES], qg, QK_A).astype(BF16) for hh in (2 * p, 2 * p + 1)]
        qb = _rms_halves(zq[:, sl], naqg, lo)
        kb = _rms_halves(zk[:, sl], nakg_ref[...], lo)
        vb = zv[:, sl]
        kbb, vbb = kb.astype(BF16), vb.astype(BF16)
        va = vals[:, sl]
        qms = [jnp.where(keep, qb, 0.0).astype(BF16) for keep in (lo, hi)]
        for bi, rs in enumerate(rows):
            o2 = [_attend([(_dot_nt(qhs[i][rs], keys[2 * p + i][rs]), va[rs], False)]) for i in (0, 1)]
            y_scr[rs, sl] = (jnp.where(lo, o2[0], o2[1]) * _silu(ga[rs, sl])).astype(BF16)
            _store_pair_transposed(nak_ref, bi, p, kb[rs])
            _store_pair_transposed(nav_ref, bi, p, vb[rs])
            o2 = [_attend([(_dot_nt(qms[i][rs], kbb[rs]), vbb[rs], False)]) for i in (0, 1)]
            y_scr[rs, ys] = (jnp.where(lo, o2[0], o2[1]) * _silu(gb[rs, sl])).astype(BF16)

    xo_ref[...] = (x + gate * _dot(y_scr[...], wout_ref[...])).reshape(nbs, SEQ, D_MODEL)


def _full(shape):
    n = len(shape)
    return pl.BlockSpec(shape, lambda *_: (0,) * n, pipeline_mode=pl.Buffered(1))


class _Row(NamedTuple):
    table: jax.Array
    row: int


def _spec(a):
    if isinstance(a, _Row):
        idx = (a.row,) + (0,) * (a.table.ndim - 1)
        return pl.BlockSpec((None,) + a.table.shape[1:], lambda *_: idx, pipeline_mode=pl.Buffered(1))
    return _full(a.shape)


def _arr(a):
    return a.table if isinstance(a, _Row) else a


def _prompt_even(x, m, ng, win, qag, wq, kvag, wkk, wkv, qg, kg, naqg, nakg, wout,
                 cond_t, n_cond, w_mod, b_mod, w_in_odd, w_out_odd):
    nb = x.shape[0]
    nbs = PROMPT_BATCHES_PER_STEP
    steps = nb // nbs
    assert nb % nbs == 0 and D_MODEL % (BF16_SUBLANES * steps) == 0
    tr = D_MODEL // steps
    ins = (m, ng, win, qag, wq, kvag, wkk, wkv, qg, kg, naqg, nakg, wout)
    return pl.pallas_call(
        functools.partial(_p0_kernel, n_cond),
        grid=(steps,),
        in_specs=[pl.BlockSpec((nbs, SEQ, D_MODEL), lambda b: (b, 0, 0))] + [_spec(a) for a in ins]
        + [pl.BlockSpec((tr, 8), lambda b: (b, 0)),
           pl.BlockSpec((None, tr, 3 * D_MODEL), lambda b: (1, b, 0)),
           _full(b_mod.shape),
           pl.BlockSpec((1, tr, O_END), lambda b: (0, b, 0)),
           pl.BlockSpec((1, tr, D_MODEL), lambda b: (0, b, 0))],
        out_specs=[pl.BlockSpec((nbs, SEQ, D_MODEL), lambda b: (b, 0, 0)),
                   pl.BlockSpec((nbs, 1, SEQ, KV_RANK), lambda b: (b, 0, 0, 0)),
                   pl.BlockSpec((nbs, 1, ROPE_A, SEQ), lambda b: (b, 0, 0, 0)),
                   pl.BlockSpec((nbs, 1, N_HEADS, HEAD_DIM, SEQ), lambda b: (b, 0, 0, 0, 0)),
                   pl.BlockSpec((nbs, 1, N_HEADS, HEAD_DIM, SEQ), lambda b: (b, 0, 0, 0, 0)),
                   pl.BlockSpec((8, 3 * D_MODEL), lambda b: (0, 0)),
                   pl.BlockSpec((tr, O_END), lambda b: (b, 0)),
                   pl.BlockSpec((tr, D_MODEL), lambda b: (b, 0))],
        out_shape=[jax.ShapeDtypeStruct((nb, SEQ, D_MODEL), F32),
                   jax.ShapeDtypeStruct((nb, 1, SEQ, KV_RANK), F32),
                   jax.ShapeDtypeStruct((nb, 1, ROPE_A, SEQ), F32),
                   jax.ShapeDtypeStruct((nb, 1, N_HEADS, HEAD_DIM, SEQ), F32),
                   jax.ShapeDtypeStruct((nb, 1, N_HEADS, HEAD_DIM, SEQ), F32),
                   jax.ShapeDtypeStruct((8, 3 * D_MODEL), F32),
                   jax.ShapeDtypeStruct((D_MODEL, O_END), BF16),
                   jax.ShapeDtypeStruct((D_MODEL, D_MODEL), BF16)],
        scratch_shapes=[pltpu.VMEM((nbs * SEQ, D_MODEL), BF16)],
        compiler_params=pltpu.CompilerParams(dimension_semantics=("arbitrary",), vmem_limit_bytes=VMEM_LIMIT),
        name="prompt_even",
    )(x, *map(_arr, ins), cond_t, w_mod, b_mod, w_in_odd, w_out_odd)


def _gqa_pair_operands(k, v, kg2, lo):
    kn = _rms_halves(k, kg2, lo)
    return kn, (kn.astype(BF16), pltpu.roll(kn, HEAD_DIM, 1).astype(BF16)), \
        (_with_ones(v.astype(BF16)), _with_ones(pltpu.roll(v, HEAD_DIM, 1).astype(BF16)))


def _p1_kernel(sink_ref, x_ref, m_ref, ng_ref, win_ref, gqg_ref, gkg_ref, sqg_ref, skg_ref, wout_ref,
               xo_ref, gk_ref, gv_ref, sk_ref, sv_ref, y_scr):
    nbs = x_ref.shape[0]
    x = x_ref[...].reshape(nbs * SEQ, D_MODEL)
    h, gate = _modulate(x, ng_ref[...], m_ref[0:1, :])
    hb = h.astype(BF16)
    lo = _lane_lo()
    hi = jnp.logical_not(lo)
    sc = HEAD_DIM ** -0.5 * LOG2E
    rows = [slice(bi * SEQ, (bi + 1) * SEQ) for bi in range(nbs)]

    branches = ((O_QC, O_KC, O_VC, O_GC, gqg_ref, gkg_ref, gk_ref, gv_ref, False, 0),
                (O_QD, O_KD, O_VD, O_GD, sqg_ref, skg_ref, sk_ref, sv_ref, True, 4 * LANES))
    for oq, ok, ov, og, qg_ref, kg_ref, ck_ref, cv_ref, has_sink, yoff in branches:
        zq = _dot(hb, win_ref[:, oq:oq + 4 * LANES])
        zkv = _dot(hb, win_ref[:, ok:ok + 2 * LANES])
        zg = _dot(hb, win_ref[:, og:og + 4 * LANES])
        v = zkv[:, LANES:]
        kn, ks, vs = _gqa_pair_operands(zkv[:, :LANES], v, kg_ref[...], lo)
        for bi, rs in enumerate(rows):
            _store_pair_transposed(ck_ref, bi, 0, kn[rs])
            _store_pair_transposed(cv_ref, bi, 0, v[rs])
        qg = qg_ref[...] * sc
        for p in range(N_PAIRS):
            sl = slice(p * LANES, (p + 1) * LANES)
            qn = _rms_halves(zq[:, sl], qg, lo)
            qms = [jnp.where(keep, qn, 0.0).astype(BF16) for keep in (lo, hi)]
            kv = p // 2
            for bi, rs in enumerate(rows):
                o2 = []
                for half in (0, 1):
                    swap = 0 if kv == half else 1
                    sink = sink_ref[2 * p + half] * LOG2E if has_sink else None
                    o2.append(_attend([(_dot_nt(qms[half][rs], ks[swap][rs]), vs[swap][rs], False)], sink))
                o = jnp.where(lo, o2[0], o2[1])
                y_scr[rs, yoff + p * LANES:yoff + (p + 1) * LANES] = (o * _silu(zg[rs, sl])).astype(BF16)

    xo_ref[...] = (x + gate * _dot(y_scr[...], wout_ref[...])).reshape(nbs, SEQ, D_MODEL)


def _prompt_odd(sink, x, m, ng, win, gqg, gkg, sqg, skg, wout):
    nb = x.shape[0]
    nbs = PROMPT_ODD_BATCHES_PER_STEP
    assert nb % nbs == 0
    ins = (m, ng, win, gqg, gkg, sqg, skg, wout)
    cache_spec = pl.BlockSpec((nbs, 1, 2, HEAD_DIM, SEQ), lambda b: (b, 0, 0, 0, 0))
    cache_shape = jax.ShapeDtypeStruct((nb, 1, 2, HEAD_DIM, SEQ), F32)
    return pl.pallas_call(
        _p1_kernel,
        grid=(nb // nbs,),
        in_specs=[pl.BlockSpec(memory_space=pltpu.SMEM),
                  pl.BlockSpec((nbs, SEQ, D_MODEL), lambda b: (b, 0, 0))] + [_spec(a) for a in ins],
        out_specs=[pl.BlockSpec((nbs, SEQ, D_MODEL), lambda b: (b, 0, 0))] + [cache_spec] * 4,
        out_shape=[jax.ShapeDtypeStruct((nb, SEQ, D_MODEL), F32)] + [cache_shape] * 4,
        scratch_shapes=[pltpu.VMEM((nbs * SEQ, D_MODEL), BF16)],
        compiler_params=pltpu.CompilerParams(dimension_semantics=("arbitrary",), vmem_limit_bytes=VMEM_LIMIT),
        name="prompt_odd",
    )(sink, x, *map(_arr, ins))


def _s0a_kernel(x_ref, m_ref, ng_ref, win_ref, qag_ref, wq_ref, kvag_ref, wkk_ref, wkv_ref, qg_ref, kg_ref,
                naqg_ref, nakg_ref, cos_ref, sin_ref,
                qa_ref, ka_ref, va_ref, qb_ref, kb_ref, vb_ref, g_ref):
    b = pl.program_id(0)
    lo = _lane_lo()
    partner = _rope_matrix2(ROPE_A, LANES, NOPE_A)
    hb = _modulate(x_ref[0], ng_ref[...], m_ref[pl.ds(1 + b, 1), :])[0].astype(BF16)
    cos, sin = cos_ref[...], sin_ref[...]

    qln = _rms(_dot_nt(hb, win_ref[E_QLAT:E_CKV, :]), qag_ref[...], Q_RANK).astype(BF16)
    q_all = _dot(qln, wq_ref[...])
    ckv_n = _rms(_dot_nt(hb, win_ref[E_CKV:E_KROPE, :]), kvag_ref[...], KV_RANK)
    kr = _dot_nt(hb, _rope_key_slab(win_ref))
    cb = ckv_n.astype(BF16)
    kk = _dot(cb, wkk_ref[...])
    va_ref[0] = _dot(cb, wkv_ref[...]).astype(BF16)
    zq = _dot_nt(hb, win_ref[E_QB:E_KB, :])
    zk = _dot_nt(hb, win_ref[E_KB:E_VB, :])
    vb_ref[0] = _dot_nt(hb, win_ref[E_VB:E_GB, :]).astype(BF16)
    g_ref[0, :, 0:4 * LANES] = _silu(_dot_nt(hb, win_ref[E_GA:E_QB, :]))
    g_ref[0, :, 4 * LANES:8 * LANES] = _silu(_dot_nt(hb, win_ref[E_GB:E_END, :]))

    qg = qg_ref[...] * (QK_A ** -0.5 * LOG2E)
    kg = kg_ref[...]
    k_partner = _lane_mix(kr * kg, partner) * sin
    for hh in range(N_HEADS):
        sl = slice(hh * LANES, (hh + 1) * LANES)
        qn = _rms(q_all[:, sl], qg, QK_A)
        qa_ref[0, :, sl] = (qn * cos + _lane_mix(qn, partner) * sin).astype(BF16)
        k_raw = kk[:, sl] + kr
        k_inv = lax.rsqrt(jnp.sum(k_raw * k_raw, axis=-1, keepdims=True) / QK_A + EPS)
        ka_ref[0, :, sl] = ((k_raw * kg * cos + k_partner) * k_inv).astype(BF16)
    naqg = naqg_ref[...] * (HEAD_DIM ** -0.5 * LOG2E)
    for p in range(N_PAIRS):
        sl = slice(p * LANES, (p + 1) * LANES)
        qb_ref[0, :, sl] = _rms_halves(zq[:, sl], naqg, lo).astype(BF16)
        kb_ref[0, :, sl] = _rms_halves(zk[:, sl], nakg_ref[...], lo).astype(BF16)


def _sample_even_proj(x, m, ng, win, qag, wq, kvag, wkk, wkv, qg, kg, naqg, nakg, cos, sin):
    nb, s, _ = x.shape
    nq = s // PROJ_BLOCK
    ins = (m, ng, win, qag, wq, kvag, wkk, wkv, qg, kg, naqg, nakg)
    tab = pl.BlockSpec((PROJ_BLOCK, LANES), lambda b, j: (j, 0))

    def blk(w):
        return pl.BlockSpec((1, PROJ_BLOCK, w), lambda b, j: (b, j, 0))

    def shp(w, dt):
        return jax.ShapeDtypeStruct((nb, s, w), dt)

    return pl.pallas_call(
        _s0a_kernel,
        grid=(nb, nq),
        in_specs=[blk(D_MODEL)] + [_spec(a) for a in ins] + [tab, tab],
        out_specs=[blk(1024), blk(1024), blk(512), blk(512), blk(512), blk(512), blk(1024)],
        out_shape=[shp(1024, BF16), shp(1024, BF16), shp(512, BF16), shp(512, BF16), shp(512, BF16),
                   shp(512, BF16), shp(1024, F32)],
        compiler_params=pltpu.CompilerParams(dimension_semantics=("arbitrary", "arbitrary"),
                                             vmem_limit_bytes=VMEM_LIMIT),
        name="sample_even_proj",
    )(x, *map(_arr, ins), cos, sin)


def _build_bias_table(rpb_ref, tile_scr, tab_ref):
    qc = lax.broadcasted_iota(jnp.int32, (GRID_W, LANES), 0)
    lane = lax.broadcasted_iota(jnp.int32, (GRID_W, LANES), 1)
    kc = jnp.bitwise_and(lane, GRID_W - 1)
    lo = lane < GRID_W
    cs = jnp.clip(qc - NA_COLS // 2, 0, GRID_W - NA_COLS)
    valid = (kc >= cs) & (kc < cs + NA_COLS)
    tab_ref[...] = jnp.zeros(tab_ref.shape, F32)
    tile_scr[RPB_ROWS] = jnp.zeros((GRID_W, LANES), F32)

    def per_head(h, carry):
        r = rpb_ref[h] * LOG2E
        half = jnp.concatenate([r[:, NA_COLS - 1:], jnp.zeros((RPB_ROWS, GRID_W - RPB_COLS), F32),
                                r[:, :NA_COLS - 1]], axis=1)
        base = jnp.concatenate([half, half], axis=1)
        for dr in range(RPB_ROWS):
            t = pltpu.roll(jnp.broadcast_to(base[dr:dr + 1], (GRID_W, LANES)), 0, 1, stride=1, stride_axis=0)
            tile_scr[dr] = jnp.where(valid, t, NEG_INF)
        for c in range(NA_ROWS // 2, NA_ROWS // 2 + NA_ROWS):
            d0 = 2 * c - NA_ROWS
            tab_ref[0, h, c] = jnp.where(lo, tile_scr[d0], tile_scr[d0 + 1])
            tab_ref[1, h, c] = jnp.where(lo, tile_scr[d0 - 1 if d0 > 0 else RPB_ROWS], tile_scr[d0])
        return carry

    lax.fori_loop(0, N_HEADS, per_head, 0)


def _s0b_kernel(rpb_ref, x_ref, m_ref, qa_ref, ka_ref, va_ref, qb_ref, kb_ref, vb_ref, g_ref,
                cckv_ref, ckr_ref, cnk_ref, cnv_ref, wkk_ref, wkv_ref, kg_ref, wout_ref,
                xo_ref, kca_scr, vca_scr, tile_scr, tab_scr, y_scr):
    b = pl.program_id(0)
    j = pl.program_id(1)
    lo = _lane_lo()
    n_lat = ka_ref.shape[1]

    @pl.when((b == 0) & (j == 0))
    def _():
        _build_bias_table(rpb_ref, tile_scr, tab_scr)

    @pl.when(j == 0)
    def _():
        kr_t = jnp.concatenate([jnp.zeros((NOPE_A, PAST_LEN), F32), ckr_ref[0],
                                jnp.zeros((LANES - QK_A, PAST_LEN), F32)], axis=0)
        keys, vals = _mla_keys(cckv_ref[0].astype(BF16), kr_t.T, wkk_ref, wkv_ref, kg_ref[...])
        for hh in range(N_HEADS):
            kca_scr[:, hh * LANES:(hh + 1) * LANES] = keys[hh]
        vca_scr[...] = vals

    kidx = lax.broadcasted_iota(jnp.int32, (1, n_lat), 1)
    for p in range(N_PAIRS):
        sl = slice(p * LANES, (p + 1) * LANES)
        o2 = []
        va = _with_ones(va_ref[0, :, sl])
        vca = _with_ones(vca_scr[:, sl])
        for hh in (2 * p, 2 * p + 1):
            hs = slice(hh * LANES, (hh + 1) * LANES)
            q = qa_ref[0, :, hs]
            o2.append(_attend([(_dot_nt(q, ka_ref[0, :, hs]), va, False),
                               (_dot_nt(q, kca_scr[:, hs]), vca, False)]))
        oa = jnp.where(lo, o2[0], o2[1])
        y_scr[:, sl] = (oa * g_ref[0, :, sl]).astype(BF16)

        qb = qb_ref[0, :, sl]
        kb = kb_ref[0, :, sl]
        vb = _with_ones(vb_ref[0, :, sl])
        kcb = cnk_ref[0, sl, :].astype(BF16)
        vcb = _with_ones(cnv_ref[0, sl, :].astype(BF16), transposed=True)
        o2 = []
        for half in (0, 1):
            head = 2 * p + half
            qm = jnp.where(lo if half == 0 else jnp.logical_not(lo), qb, jnp.zeros_like(qb))
            s_lat = _dot_nt(qm, kb)
            rows = []
            for local in range(NA_Q_BLOCK // GRID_W):
                qr = j * (NA_Q_BLOCK // GRID_W) + local
                par = 0 if local % 2 == 1 else 1
                c0 = (RPB_ROWS + par - local) // 2 - (NA_Q_BLOCK // GRID_W // 2) * j
                bias = jnp.concatenate([tab_scr[par, head, c0 + t] for t in range(n_lat // LANES)], axis=1)
                r0 = jnp.clip(qr - NA_ROWS // 2, 0, n_lat // GRID_W - NA_ROWS) * GRID_W
                ok = (kidx >= r0) & (kidx < r0 + NA_ROWS * GRID_W)
                bias = bias + jnp.where(ok, 0.0, NEG_INF)
                rows.append(s_lat[local * GRID_W:(local + 1) * GRID_W] + bias)
            s_lat = jnp.concatenate(rows, axis=0)
            o2.append(_attend([(s_lat, vb, False), (_dot(qm, kcb), vcb, True)]))
        ob = jnp.where(lo, o2[0], o2[1])
        ys = slice(4 * LANES + p * LANES, 4 * LANES + (p + 1) * LANES)
        y_scr[:, ys] = (ob * g_ref[0, :, ys]).astype(BF16)

    d = x_ref.shape[-1]
    gate = m_ref[pl.ds(1 + b, 1), 2 * d:]
    xo_ref[0] = x_ref[0] + gate * _dot(y_scr[...], wout_ref[...])


def _sample_even_attn(rpb, x, m, qa, ka, va, qb, kb, vb, g, cckv, ckr, cnk, cnv, wkk, wkv, kg, wout):
    nb, s, _ = x.shape
    nq = s // NA_Q_BLOCK

    def blk(w):
        return pl.BlockSpec((1, NA_Q_BLOCK, w), lambda b, j: (b, j, 0))

    def per_batch(a):
        return pl.BlockSpec((1,) + a.shape[1:], lambda b, j: (b, 0, 0))

    return pl.pallas_call(
        _s0b_kernel,
        grid=(nb, nq),
        in_specs=[_full(rpb.shape), blk(D_MODEL), _spec(m),
                  blk(1024), per_batch(ka), per_batch(va), blk(512), per_batch(kb), per_batch(vb), blk(1024),
                  per_batch(cckv), per_batch(ckr), per_batch(cnk), per_batch(cnv),
                  _full(wkk.shape), _full(wkv.shape), _spec(kg), _full(wout.shape)],
        out_specs=blk(D_MODEL),
        out_shape=jax.ShapeDtypeStruct(x.shape, F32),
        scratch_shapes=[pltpu.VMEM((PAST_LEN, N_HEADS * LANES), BF16),
                        pltpu.VMEM((PAST_LEN, N_HEADS * HEAD_DIM), BF16),
                        pltpu.VMEM((RPB_ROWS + 1, GRID_W, LANES), F32),
                        pltpu.VMEM((2, N_HEADS, BIAS_CHUNKS, GRID_W, LANES), F32),
                        pltpu.VMEM((NA_Q_BLOCK, D_MODEL), BF16)],
        compiler_params=pltpu.CompilerParams(dimension_semantics=("arbitrary", "arbitrary"),
                                             vmem_limit_bytes=VMEM_LIMIT),
        name="sample_even_attn",
    )(rpb, x, _arr(m), qa, ka, va, qb, kb, vb, g, cckv, ckr, cnk, cnv, wkk, wkv, _arr(kg), wout)


def _s1a_kernel(x_ref, m_ref, ng_ref, win_ref, gqg_ref, gkg_ref, sqg_ref, skg_ref, cos_ref, sin_ref,
                qc_ref, kc_ref, vc_ref, qd_ref, kd_ref, vd_ref, g_ref):
    b = pl.program_id(0)
    lo = _lane_lo()
    partner = _rope_matrix2(HEAD_DIM, HEAD_DIM, 0)
    swap = _swap_matrix2()[:LANES]
    hb = _modulate(x_ref[0], ng_ref[...], m_ref[pl.ds(1 + b, 1), :])[0].astype(BF16)
    cos, sin = cos_ref[...], sin_ref[...]
    sc = HEAD_DIM ** -0.5 * LOG2E

    def rope(t):
        return t * cos + _lane_mix(t, partner) * sin

    branches = ((O_QC, O_KC, O_GC, gqg_ref, gkg_ref, qc_ref, kc_ref, vc_ref, 0),
                (O_QD, O_KD, O_GD, sqg_ref, skg_ref, qd_ref, kd_ref, vd_ref, 4 * LANES))
    for oq, ok, og, qg_ref, kg_ref, q_out, k_out, v_out, goff in branches:
        zq = _dot(hb, win_ref[:, oq:oq + 4 * LANES])
        zkv = _dot(hb, win_ref[:, ok:ok + 2 * LANES])
        qg = qg_ref[...] * sc
        for p in range(N_PAIRS):
            sl = slice(p * LANES, (p + 1) * LANES)
            q_out[0, :, sl] = rope(_rms_halves(zq[:, sl], qg, lo)).astype(BF16)
        kn = rope(_rms_halves(zkv[:, :LANES], kg_ref[...], lo))
        v = zkv[:, LANES:]
        for out, val in ((k_out, kn.astype(BF16)), (v_out, v.astype(BF16))):
            out[0, :, 0:LANES] = val
            out[0, :, LANES:2 * LANES] = _dot(val, swap).astype(BF16)
        g_ref[0, :, goff:goff + 4 * LANES] = _silu(_dot(hb, win_ref[:, og:og + 4 * LANES]))


def _sample_odd_proj(x, m, ng, win, gqg, gkg, sqg, skg, cos, sin):
    nb, s, _ = x.shape
    nq = s // PROJ_BLOCK
    ins = (m, ng, win, gqg, gkg, sqg, skg)
    tab = pl.BlockSpec((PROJ_BLOCK, LANES), lambda b, j: (j, 0))

    def blk(w):
        return pl.BlockSpec((1, PROJ_BLOCK, w), lambda b, j: (b, j, 0))

    def shp(w, dt):
        return jax.ShapeDtypeStruct((nb, s, w), dt)

    return pl.pallas_call(
        _s1a_kernel,
        grid=(nb, nq),
        in_specs=[blk(D_MODEL)] + [_spec(a) for a in ins] + [tab, tab],
        out_specs=[blk(512), blk(256), blk(256), blk(512), blk(256), blk(256), blk(1024)],
        out_shape=[shp(512, BF16), shp(256, BF16), shp(256, BF16), shp(512, BF16), shp(256, BF16),
                   shp(256, BF16), shp(1024, F32)],
        compiler_params=pltpu.CompilerParams(dimension_semantics=("arbitrary", "arbitrary"),
                                             vmem_limit_bytes=VMEM_LIMIT),
        name="sample_odd_proj",
    )(x, *map(_arr, ins), cos, sin)


def _s1b_kernel(sink_ref, x_ref, m_ref, qc_ref, kc_ref, vc_ref, qd_ref, kd_ref, vd_ref, g_ref,
                cgk_ref, cgv_ref, csk_ref, csv_ref, wout_ref, xo_ref, y_scr):
    b = pl.program_id(0)
    j = pl.program_id(1)
    lo = _lane_lo()
    n_lat = kc_ref.shape[1]
    win_keys = Q_BLOCK + 2 * SWA_HALF

    def ctx_pair(ref, values=False):
        a = ref[0].astype(BF16)
        pair = (a, _swap_halves(a))
        return tuple(_with_ones(t, transposed=True) for t in pair) if values else pair

    cgk, cgv, csk, csv = ctx_pair(cgk_ref), ctx_pair(cgv_ref, True), ctx_pair(csk_ref), ctx_pair(csv_ref, True)
    vcs = [_with_ones(vc_ref[0, :, w * LANES:(w + 1) * LANES]) for w in (0, 1)]

    ks = pl.multiple_of(jnp.clip(j * Q_BLOCK - SWA_HALF, 0, n_lat - win_keys), SWA_HALF)
    qpos = j * Q_BLOCK + lax.broadcasted_iota(jnp.int32, (Q_BLOCK, win_keys), 0)
    kpos = ks + lax.broadcasted_iota(jnp.int32, (Q_BLOCK, win_keys), 1)
    band = jnp.abs(qpos - kpos) <= SWA_HALF
    vds = [_with_ones(vd_ref[0, pl.ds(ks, win_keys), w * LANES:(w + 1) * LANES]) for w in (0, 1)]

    for p in range(N_PAIRS):
        sl = slice(p * LANES, (p + 1) * LANES)
        kv = p // 2
        qc = qc_ref[0, :, sl]
        qd = qd_ref[0, :, sl]
        oc2, od2 = [], []
        for half in (0, 1):
            swap = 0 if kv == half else 1
            ws = slice(swap * LANES, (swap + 1) * LANES)
            keep = lo if half == 0 else jnp.logical_not(lo)
            qm = jnp.where(keep, qc, jnp.zeros_like(qc))
            oc2.append(_attend([(_dot_nt(qm, kc_ref[0, :, ws]), vcs[swap], False),
                                (_dot(qm, cgk[swap]), cgv[swap], True)]))
            qm = jnp.where(keep, qd, jnp.zeros_like(qd))
            s_loc = jnp.where(band, _dot_nt(qm, kd_ref[0, pl.ds(ks, win_keys), ws]), NEG_INF)
            od2.append(_attend([(s_loc, vds[swap], False),
                                (_dot(qm, csk[swap]), csv[swap], True)], sink_ref[2 * p + half] * LOG2E))
        y_scr[:, sl] = (jnp.where(lo, oc2[0], oc2[1]) * g_ref[0, :, sl]).astype(BF16)
        ys = slice(4 * LANES + p * LANES, 4 * LANES + (p + 1) * LANES)
        y_scr[:, ys] = (jnp.where(lo, od2[0], od2[1]) * g_ref[0, :, ys]).astype(BF16)

    d = x_ref.shape[-1]
    gate = m_ref[pl.ds(1 + b, 1), 2 * d:]
    xo_ref[0] = x_ref[0] + gate * _dot(y_scr[...], wout_ref[...])


def _sample_odd_attn(sink, x, m, qc, kc, vc, qd, kd, vd, g, cgk, cgv, csk, csv, wout):
    nb, s, _ = x.shape
    nq = s // Q_BLOCK

    def blk(w):
        return pl.BlockSpec((1, Q_BLOCK, w), lambda b, j: (b, j, 0))

    def per_batch(a):
        return pl.BlockSpec((1,) + a.shape[1:], lambda b, j: (b, 0, 0))

    return pl.pallas_call(
        _s1b_kernel,
        grid=(nb, nq),
        in_specs=[pl.BlockSpec(memory_space=pltpu.SMEM), blk(D_MODEL), _spec(m),
                  blk(512), per_batch(kc), per_batch(vc), blk(512), per_batch(kd), per_batch(vd), blk(1024),
                  per_batch(cgk), per_batch(cgv), per_batch(csk), per_batch(csv), _full(wout.shape)],
        out_specs=blk(D_MODEL),
        out_shape=jax.ShapeDtypeStruct(x.shape, F32),
        scratch_shapes=[pltpu.VMEM((Q_BLOCK, D_MODEL), BF16)],
        compiler_params=pltpu.CompilerParams(dimension_semantics=("arbitrary", "arbitrary"),
                                             vmem_limit_bytes=VMEM_LIMIT),
        name="sample_odd_attn",
    )(sink, x, _arr(m), qc, kc, vc, qd, kd, vd, g, cgk, cgv, csk, csv, wout)


WEIGHT_PREP_STEPS = 8
EVEN_IN_CHUNKS = 6
COND_PREP_ROWS = 256
BF16_SUBLANES = 16


G_MLA_Q, G_MLA_K, G_NA_Q, G_NA_K, G_GQA_Q, G_GQA_K, G_SWA_Q, G_SWA_K, N_GAINS = range(9)


GAIN_WIDTHS = (QK_A, QK_A) + (HEAD_DIM,) * 6


def _cond_prep_kernel(n_cond, ct_ref, wm_ref, bm_ref, ng_ref, gains_ref, mo_ref, gt_ref, ngt_ref):
    _mod_step(n_cond, pl.program_id(0) == 0, bm_ref[0:1, :], ct_ref, wm_ref, mo_ref)
    gt_ref[...] = jnp.zeros(gt_ref.shape, F32)
    start = 0
    for r, w in enumerate(GAIN_WIDTHS):
        g = gains_ref[:, start:start + w]
        start += w
        for off in range(0, LANES - w + 1, w):
            gt_ref[r, :, off:off + w] = g
    for layer in range(ngt_ref.shape[0]):
        ngt_ref[layer] = ng_ref[layer:layer + 1, :]


def _cond_prep(cond_t, n_cond, w_mod, b_mod, norm_g, gains):
    assert len(gains) == N_GAINS and tuple(g.shape[-1] for g in gains) == GAIN_WIDTHS
    gains_row = jnp.concatenate([g.reshape(1, -1) for g in gains], axis=1)
    tk = COND_PREP_ROWS
    return pl.pallas_call(
        functools.partial(_cond_prep_kernel, n_cond),
        grid=(D_MODEL // tk,),
        in_specs=[pl.BlockSpec((tk, 8), lambda k: (k, 0)),
                  pl.BlockSpec((None, tk, 3 * D_MODEL), lambda k: (0, k, 0)),
                  _full(b_mod.shape), _full(norm_g.shape), _full(gains_row.shape)],
        out_specs=[_full((8, 3 * D_MODEL)), _full((N_GAINS, 1, LANES)), _full((norm_g.shape[0], 1, D_MODEL))],
        out_shape=[jax.ShapeDtypeStruct((8, 3 * D_MODEL), F32), jax.ShapeDtypeStruct((N_GAINS, 1, LANES), F32),
                   jax.ShapeDtypeStruct((norm_g.shape[0], 1, D_MODEL), F32)],
        compiler_params=pltpu.CompilerParams(dimension_semantics=("arbitrary",)),
        name="cond_prep",
    )(cond_t, w_mod, b_mod, norm_g, gains_row)


def _weight_prep_kernel(wie_ref, woe_ref, wqu_ref, wkv_ref, win_e_ref, wout_e_ref, wq_ref, wkk_ref, wkvv_ref):
    win_e_ref[...] = wie_ref[...].astype(BF16)
    wout_e_ref[...] = woe_ref[0].astype(BF16)

    wq_ref[...] = jnp.zeros(wq_ref.shape, BF16)
    for h in range(N_HEADS):
        wq_ref[:, h * LANES:h * LANES + QK_A] = wqu_ref[0, :, h * QK_A:(h + 1) * QK_A].astype(BF16)
    lo = _lane_lo()
    for p in range(N_PAIRS):
        a = wkv_ref[0, :, (2 * p) * LANES:(2 * p + 1) * LANES]
        c = wkv_ref[0, :, (2 * p + 1) * LANES:(2 * p + 2) * LANES]
        wkk_ref[:, (2 * p) * LANES:(2 * p + 1) * LANES] = jnp.where(lo, a, 0.0).astype(BF16)
        wkk_ref[:, (2 * p + 1) * LANES:(2 * p + 2) * LANES] = jnp.where(lo, c, 0.0).astype(BF16)
        wkvv_ref[:, p * LANES:(p + 1) * LANES] = jnp.where(lo, pltpu.roll(a, HEAD_DIM, 1), c).astype(BF16)


def _weight_prep(w_in_even_t, w_out_even, w_q_up, w_kv_up):
    n = WEIGHT_PREP_STEPS
    ins = (w_out_even, w_q_up, w_kv_up)
    out_cols = (D_MODEL, N_HEADS * LANES, N_HEADS * LANES, N_HEADS * HEAD_DIM)
    out_rows = (D_MODEL, Q_RANK, KV_RANK, KV_RANK)
    te = E_END // EVEN_IN_CHUNKS
    assert te * EVEN_IN_CHUNKS == E_END and te % BF16_SUBLANES == 0 and EVEN_IN_CHUNKS <= n
    even_spec = pl.BlockSpec((te, D_MODEL), lambda i: (jnp.minimum(i, EVEN_IN_CHUNKS - 1), 0))
    return pl.pallas_call(
        _weight_prep_kernel,
        grid=(n,),
        in_specs=[even_spec] + [pl.BlockSpec((1, a.shape[1] // n, a.shape[2]), lambda i: (0, i, 0)) for a in ins],
        out_specs=[even_spec] + [pl.BlockSpec((r // n, c), lambda i: (i, 0)) for r, c in zip(out_rows, out_cols)],
        out_shape=[jax.ShapeDtypeStruct((E_END, D_MODEL), BF16)]
        + [jax.ShapeDtypeStruct((r, c), BF16) for r, c in zip(out_rows, out_cols)],
        compiler_params=pltpu.CompilerParams(dimension_semantics=("arbitrary",), vmem_limit_bytes=VMEM_LIMIT),
        name="weight_prep",
    )(w_in_even_t, *ins)


def _feature_major(c):
    b, h, l, d = c.shape
    return jnp.swapaxes(c, -1, -2).reshape(b, h * d, l)


def _token_major(c):
    return jnp.swapaxes(c, -1, -2)


def _rope_tables(s, rot_dim, period, start):
    quarter = rot_dim // 4
    t = np.arange(s)
    inv = ROPE_THETA ** (-np.arange(quarter, dtype=np.float64) / quarter)
    row = (t // GRID_W).astype(np.float64)[:, None] * inv
    col = (t % GRID_W).astype(np.float64)[:, None] * inv
    ang = np.concatenate([row, col], axis=-1)
    cos, sin = np.cos(ang), np.sin(ang)
    pre = np.ones((s, start))
    post = np.zeros((s, period - start - rot_dim))
    c = np.concatenate([pre, cos, cos, post], axis=-1)
    sn = np.concatenate([0 * pre, sin, sin, post], axis=-1)
    rep = LANES // period
    return jnp.asarray(np.tile(c, (1, rep)), F32), jnp.asarray(np.tile(sn, (1, rep)), F32)


def kernel(x_prompt, x_sample, cache_mla_ckv, cache_mla_krope, cache_na_k, cache_na_v, cache_gqa_k, cache_gqa_v, cache_swa_k, cache_swa_v, c, c_ctx, norm_g, w_mod, b_mod, w_in_even, mla_qa_g, w_q_up, mla_kva_g, w_kv_up, mla_q_g, mla_k_g, na_q_g, na_k_g, na_rpb, w_out_even, w_in_odd, gqa_q_g, gqa_k_g, swa_q_g, swa_k_g, swa_sink, w_out_odd):
    n_dec = x_sample.shape[0]
    assert w_mod.shape[0] == 2 and n_dec + 1 <= 8

    cond_t = jnp.concatenate([c_ctx[:, None], c.T, jnp.zeros((D_MODEL, 7 - n_dec), F32)], axis=1)
    n_cond = 1 + n_dec
    gains = (mla_q_g, mla_k_g, na_q_g, na_k_g, gqa_q_g, gqa_k_g, swa_q_g, swa_k_g)
    win_e, wout_e, wq, wkk, wkv = _weight_prep(jnp.swapaxes(w_in_even[0], 0, 1), w_out_even, w_q_up, w_kv_up)
    m_even, gt, ngt = _cond_prep(cond_t, n_cond, w_mod, b_mod, norm_g, gains)
    even = (_Row(ngt, 0), win_e, mla_qa_g, wq, mla_kva_g, wkk, wkv,
            _Row(gt, G_MLA_Q), _Row(gt, G_MLA_K), _Row(gt, G_NA_Q), _Row(gt, G_NA_K))
    sink = swa_sink[0].astype(F32)

    xp1, new_ckv, new_krope, new_na_k, new_na_v, m_odd, win_o, wout_o = _prompt_even(
        x_prompt, m_even, *even, wout_e, cond_t, n_cond, w_mod, b_mod, w_in_odd, w_out_odd)
    odd = (_Row(ngt, 1), win_o, _Row(gt, G_GQA_Q), _Row(gt, G_GQA_K), _Row(gt, G_SWA_Q), _Row(gt, G_SWA_K))
    xp2, new_gqa_k, new_gqa_v, new_swa_k, new_swa_v = _prompt_odd(sink, xp1, m_odd, *odd, wout_o)

    cos_e, sin_e = _rope_tables(DEC_SEQ, ROPE_A, LANES, NOPE_A)
    qa, ka, va, qbs, kbs, vbs, g0 = _sample_even_proj(x_sample, m_even, *even, cos_e, sin_e)
    ckr = jnp.swapaxes(cache_mla_krope[:, 0], -1, -2)
    xs1 = _sample_even_attn(na_rpb[0], x_sample, m_even, qa, ka, va, qbs, kbs, vbs, g0,
                            cache_mla_ckv[:, 0], ckr, _feature_major(cache_na_k[:, 0]),
                            _feature_major(cache_na_v[:, 0]), wkk, wkv, _Row(gt, G_MLA_K), wout_e)
    cos_o, sin_o = _rope_tables(DEC_SEQ, HEAD_DIM, HEAD_DIM, 0)
    qc, kc, vc, qd, kd, vd, g1 = _sample_odd_proj(xs1, m_odd, *odd, cos_o, sin_o)
    xs2 = _sample_odd_attn(sink, xs1, m_odd, qc, kc, vc, qd, kd, vd, g1,
                           _feature_major(cache_gqa_k[:, 0]), _feature_major(cache_gqa_v[:, 0]),
                           _feature_major(cache_swa_k[:, 0]), _feature_major(cache_swa_v[:, 0]), wout_o)

    caches = (new_krope, new_na_k, new_na_v, new_gqa_k, new_gqa_v, new_swa_k, new_swa_v)
    return (xp2, xs2, new_ckv) + tuple(_token_major(c) for c in caches)
```

```python
import functools
from typing import NamedTuple

import jax
import jax.numpy as jnp
import numpy as np
from jax import lax
from jax.experimental import pallas as pl
from jax.experimental.pallas import tpu as pltpu

F32 = jnp.float32
BF16 = jnp.bfloat16

D_MODEL = 1024
SEQ = 256
DEC_SEQ = 1024
PAST_LEN = 256
GRID_W = 64
HEAD_DIM = 64
Q_RANK = 256
KV_RANK = 128
NOPE_A = 64
ROPE_A = 32
QK_A = NOPE_A + ROPE_A
N_HEADS = 8
NA_ROWS = 8
NA_COLS = 16
SWA_HALF = 128
ROPE_THETA = 10000.0
EPS = 1e-6
NEG_INF = -1e30
LOG2E = 1.4426950408889634

LANES = 128
Q_BLOCK = 512
NA_Q_BLOCK = 256
PROJ_BLOCK = 512
PROMPT_BATCHES_PER_STEP = 2
PROMPT_ODD_BATCHES_PER_STEP = 4
N_PAIRS = N_HEADS // 2
RPB_ROWS = 2 * NA_ROWS - 1
RPB_COLS = 2 * NA_COLS - 1
BIAS_CHUNKS = 16
VMEM_LIMIT = 48 * 1024 * 1024

E_QLAT, E_CKV, E_KROPE, E_GA, E_QB, E_KB, E_VB, E_GB, E_END = 0, 256, 384, 416, 928, 1440, 1952, 2464, 2976
O_QC, O_KC, O_VC, O_GC, O_QD, O_KD, O_VD, O_GD, O_END = 0, 512, 640, 768, 1280, 1792, 1920, 2048, 2560


def _dot(a, b):
    return lax.dot_general(a, b, (((1,), (0,)), ((), ())), preferred_element_type=F32)


def _dot_nt(a, b):
    return lax.dot_general(a, b, (((1,), (1,)), ((), ())), preferred_element_type=F32)


def _silu(x):
    return x / (1.0 + jnp.exp(-x))


def _rms(x, g, n):
    ss = jnp.sum(x * x, axis=-1, keepdims=True)
    return x * lax.rsqrt(ss / n + EPS) * g


def _rms_halves(x, g2, lo):
    x2 = x * x
    s_lo = jnp.sum(jnp.where(lo, x2, 0.0), axis=-1, keepdims=True)
    s_hi = jnp.sum(jnp.where(lo, 0.0, x2), axis=-1, keepdims=True)
    r = jnp.where(lo, lax.rsqrt(s_lo / HEAD_DIM + EPS), lax.rsqrt(s_hi / HEAD_DIM + EPS))
    return x * r * g2


def _modulate(x, g, m):
    d = x.shape[-1]
    xn = x * lax.rsqrt(jnp.mean(x * x, axis=-1, keepdims=True) + EPS) * g
    return xn * (1.0 + m[:, d:2 * d]) + m[:, :d], m[:, 2 * d:]


def _split_lanes(x):
    hi = x.astype(BF16)
    lo = (x - hi.astype(F32)).astype(BF16)
    return jnp.concatenate([hi, lo], axis=1)


def _lane_matrix2(entries):
    i = lax.broadcasted_iota(jnp.int32, (LANES, LANES), 0)
    j = lax.broadcasted_iota(jnp.int32, (LANES, LANES), 1)
    m = entries(i, j).astype(BF16)
    return jnp.concatenate([m, m], axis=0)


def _rope_matrix2(rot_dim, period, start):
    half = rot_dim // 2

    def entries(i, j):
        pos = jnp.bitwise_and(j, period - 1) - start
        neg = (pos >= 0) & (pos < half) & (i == j + half)
        plus = (pos >= half) & (pos < rot_dim) & (i == j - half)
        return jnp.where(neg, -1.0, jnp.where(plus, 1.0, 0.0))

    return _lane_matrix2(entries)


def _swap_matrix2():
    return _lane_matrix2(lambda i, j: jnp.where(i == jnp.bitwise_xor(j, HEAD_DIM), 1.0, 0.0))


def _lane_mix(x, m2):
    return _dot(_split_lanes(x), m2)


def _with_ones(v, transposed=False):
    if transposed:
        return jnp.concatenate([v, jnp.ones((LANES, v.shape[1]), v.dtype)], axis=0)
    return jnp.concatenate([v, jnp.ones((v.shape[0], LANES), v.dtype)], axis=1)


def _attend(parts, sink=None):
    mx = None
    for s, _, _ in parts:
        pm = jnp.max(s, axis=-1, keepdims=True)
        mx = pm if mx is None else jnp.maximum(mx, pm)
    if sink is not None:
        mx = jnp.maximum(mx, sink)
    acc, den = None, None
    for s, v, v_t in parts:
        e = jnp.exp2(s - mx)
        po = (_dot_nt if v_t else _dot)(e.astype(BF16), v)
        acc = po if acc is None else acc + po
        if po.shape[1] == LANES:
            ps = jnp.sum(e, axis=-1, keepdims=True)
            den = ps if den is None else den + ps
    if den is None:
        den = acc[:, LANES:]
    if sink is not None:
        den = den + jnp.exp2(sink - mx)
    return acc[:, :LANES] * (1.0 / den)


def _lane_lo():
    return lax.broadcasted_iota(jnp.int32, (1, LANES), 1) < HEAD_DIM


def _store_pair_transposed(ref, bi, p, x):
    xt = x.T
    ref[bi, 0, 2 * p] = xt[:HEAD_DIM]
    ref[bi, 0, 2 * p + 1] = xt[HEAD_DIM:]


def _rope_key_slab(win_ref):
    d = win_ref.shape[1]
    return jnp.concatenate([jnp.zeros((NOPE_A, d), BF16), win_ref[E_KROPE:E_GA, :],
                            jnp.zeros((LANES - QK_A, d), BF16)], axis=0)


def _swap_halves(a):
    return jnp.concatenate([a[HEAD_DIM:], a[:HEAD_DIM]], axis=0)


def _mod_step(n_cond, is_first, bias_row, c_ref, w_ref, o_ref):
    @pl.when(is_first)
    def _():
        o_ref[:n_cond, :] = jnp.broadcast_to(bias_row, (n_cond, o_ref.shape[1]))
        o_ref[n_cond:, :] = jnp.zeros((o_ref.shape[0] - n_cond, o_ref.shape[1]), F32)

    s = _silu(c_ref[...])
    cols = [jnp.broadcast_to(s[:, r:r + 1], (s.shape[0], LANES)) for r in range(n_cond)]
    for t in range(w_ref.shape[1] // LANES):
        sl = slice(t * LANES, (t + 1) * LANES)
        w = w_ref[:, sl]
        for r in range(n_cond):
            o_ref[r:r + 1, sl] += jnp.sum(w * cols[r], axis=0, keepdims=True)


def _mla_keys(cb, kr, wkk_ref, wkv_ref, kg, rope=None):
    kk = _dot(cb, wkk_ref[...])
    keys = []
    for h in range(N_HEADS):
        k = _rms(kk[:, h * LANES:(h + 1) * LANES] + kr, kg, QK_A)
        if rope is not None:
            k = rope(k)
        keys.append(k.astype(BF16))
    return keys, _dot(cb, wkv_ref[...]).astype(BF16)


def _p0_kernel(n_cond, x_ref, m_ref, ng_ref, win_ref, qag_ref, wq_ref, kvag_ref, wkk_ref, wkv_ref, qg_ref, kg_ref,
               naqg_ref, nakg_ref, wout_ref, ct_ref, wm_ref, bm_ref, wio_ref, woo_ref,
               xo_ref, ckv_ref, krope_ref, nak_ref, nav_ref, mo_ref, wino_ref, wouto_ref, y_scr):
    _mod_step(n_cond, pl.program_id(0) == 0, bm_ref[1:2, :], ct_ref, wm_ref, mo_ref)
    wino_ref[...] = wio_ref[0].astype(BF16)
    wouto_ref[...] = woo_ref[0].astype(BF16)

    nbs = x_ref.shape[0]
    x = x_ref[...].reshape(nbs * SEQ, D_MODEL)
    h, gate = _modulate(x, ng_ref[...], m_ref[0:1, :])
    hb = h.astype(BF16)
    lo = _lane_lo()
    hi = jnp.logical_not(lo)
    rows = [slice(bi * SEQ, (bi + 1) * SEQ) for bi in range(nbs)]

    qln = _rms(_dot_nt(hb, win_ref[E_QLAT:E_CKV, :]), qag_ref[...], Q_RANK).astype(BF16)
    q_all = _dot(qln, wq_ref[...])
    ckv_n = _rms(_dot_nt(hb, win_ref[E_CKV:E_KROPE, :]), kvag_ref[...], KV_RANK)
    kr = _dot_nt(hb, _rope_key_slab(win_ref))
    for bi, rs in enumerate(rows):
        ckv_ref[bi, 0] = ckv_n[rs]
        krope_ref[bi, 0] = kr[rs].T[NOPE_A:QK_A]
    keys, vals = _mla_keys(ckv_n.astype(BF16), kr, wkk_ref, wkv_ref, kg_ref[...])
    qg = qg_ref[...] * (QK_A ** -0.5 * LOG2E)

    ga = _dot_nt(hb, win_ref[E_GA:E_QB, :])
    zq = _dot_nt(hb, win_ref[E_QB:E_KB, :])
    zk = _dot_nt(hb, win_ref[E_KB:E_VB, :])
    zv = _dot_nt(hb, win_ref[E_VB:E_GB, :])
    gb = _dot_nt(hb, win_ref[E_GB:E_END, :])
    naqg = naqg_ref[...] * (HEAD_DIM ** -0.5 * LOG2E)

    for p in range(N_PAIRS):
        sl = slice(p * LANES, (p + 1) * LANES)
        ys = slice(4 * LANES + p * LANES, 4 * LANES + (p + 1) * LANES)
        qhs = [_rms(q_all[:, hh * LANES:(hh + 1) * LANES], qg, QK_A).astype(BF16) for hh in (2 * p, 2 * p + 1)]
        qb = _rms_halves(zq[:, sl], naqg, lo)
        kb = _rms_halves(zk[:, sl], nakg_ref[...], lo)
        vb = zv[:, sl]
        kbb, vbb = kb.astype(BF16), vb.astype(BF16)
        va = vals[:, sl]
        qms = [jnp.where(keep, qb, 0.0).astype(BF16) for keep in (lo, hi)]
        for bi, rs in enumerate(rows):
            o2 = [_attend([(_dot_nt(qhs[i][rs], keys[2 * p + i][rs]), va[rs], False)]) for i in (0, 1)]
            y_scr[rs, sl] = (jnp.where(lo, o2[0], o2[1]) * _silu(ga[rs, sl])).astype(BF16)
            _store_pair_transposed(nak_ref, bi, p, kb[rs])
            _store_pair_transposed(nav_ref, bi, p, vb[rs])
            o2 = [_attend([(_dot_nt(qms[i][rs], kbb[rs]), vbb[rs], False)]) for i in (0, 1)]
            y_scr[rs, ys] = (jnp.where(lo, o2[0], o2[1]) * _silu(gb[rs, sl])).astype(BF16)

    xo_ref[...] = (x + gate * _dot(y_scr[...], wout_ref[...])).reshape(nbs, SEQ, D_MODEL)


def _full(shape):
    n = len(shape)
    return pl.BlockSpec(shape, lambda *_: (0,) * n, pipeline_mode=pl.Buffered(1))


class _Row(NamedTuple):
    table: jax.Array
    row: int


class _Hbm(NamedTuple):
    array: jax.Array


def _fetch_once(first, hbm_ref, scr_ref, sem):
    copy = pltpu.make_async_copy(hbm_ref, scr_ref, sem)
    if first is None:
        copy.start()
        return copy.wait
    pl.when(first)(copy.start)

    def wait():
        pl.when(first)(copy.wait)
    return wait


def _spec(a):
    if isinstance(a, _Hbm):
        return pl.BlockSpec(memory_space=pl.ANY)
    if isinstance(a, _Row):
        idx = (a.row,) + (0,) * (a.table.ndim - 1)
        return pl.BlockSpec((None,) + a.table.shape[1:], lambda *_: idx, pipeline_mode=pl.Buffered(1))
    return _full(a.shape)


def _arr(a):
    return a.table if isinstance(a, _Row) else a.array if isinstance(a, _Hbm) else a


def _prompt_even(x, m, ng, win, qag, wq, kvag, wkk, wkv, qg, kg, naqg, nakg, wout,
                 cond_t, n_cond, w_mod, b_mod, w_in_odd, w_out_odd):
    nb = x.shape[0]
    nbs = PROMPT_BATCHES_PER_STEP
    steps = nb // nbs
    assert nb % nbs == 0 and D_MODEL % (BF16_SUBLANES * steps) == 0
    tr = D_MODEL // steps
    ins = (m, ng, win, qag, wq, kvag, wkk, wkv, qg, kg, naqg, nakg, wout)
    return pl.pallas_call(
        functools.partial(_p0_kernel, n_cond),
        grid=(steps,),
        in_specs=[pl.BlockSpec((nbs, SEQ, D_MODEL), lambda b: (b, 0, 0))] + [_spec(a) for a in ins]
        + [pl.BlockSpec((tr, 8), lambda b: (b, 0)),
           pl.BlockSpec((None, tr, 3 * D_MODEL), lambda b: (1, b, 0)),
           _full(b_mod.shape),
           pl.BlockSpec((1, tr, O_END), lambda b: (0, b, 0)),
           pl.BlockSpec((1, tr, D_MODEL), lambda b: (0, b, 0))],
        out_specs=[pl.BlockSpec((nbs, SEQ, D_MODEL), lambda b: (b, 0, 0)),
                   pl.BlockSpec((nbs, 1, SEQ, KV_RANK), lambda b: (b, 0, 0, 0)),
                   pl.BlockSpec((nbs, 1, ROPE_A, SEQ), lambda b: (b, 0, 0, 0)),
                   pl.BlockSpec((nbs, 1, N_HEADS, HEAD_DIM, SEQ), lambda b: (b, 0, 0, 0, 0)),
                   pl.BlockSpec((nbs, 1, N_HEADS, HEAD_DIM, SEQ), lambda b: (b, 0, 0, 0, 0)),
                   pl.BlockSpec((8, 3 * D_MODEL), lambda b: (0, 0)),
                   pl.BlockSpec((tr, O_END), lambda b: (b, 0)),
                   pl.BlockSpec((tr, D_MODEL), lambda b: (b, 0))],
        out_shape=[jax.ShapeDtypeStruct((nb, SEQ, D_MODEL), F32),
                   jax.ShapeDtypeStruct((nb, 1, SEQ, KV_RANK), F32),
                   jax.ShapeDtypeStruct((nb, 1, ROPE_A, SEQ), F32),
                   jax.ShapeDtypeStruct((nb, 1, N_HEADS, HEAD_DIM, SEQ), F32),
                   jax.ShapeDtypeStruct((nb, 1, N_HEADS, HEAD_DIM, SEQ), F32),
                   jax.ShapeDtypeStruct((8, 3 * D_MODEL), F32),
                   jax.ShapeDtypeStruct((D_MODEL, O_END), BF16),
                   jax.ShapeDtypeStruct((D_MODEL, D_MODEL), BF16)],
        scratch_shapes=[pltpu.VMEM((nbs * SEQ, D_MODEL), BF16)],
        compiler_params=pltpu.CompilerParams(dimension_semantics=("arbitrary",), vmem_limit_bytes=VMEM_LIMIT),
        name="prompt_even",
    )(x, *map(_arr, ins), cond_t, w_mod, b_mod, w_in_odd, w_out_odd)


def _gqa_pair_operands(k, v, kg2, lo):
    kn = _rms_halves(k, kg2, lo)
    return kn, (kn.astype(BF16), pltpu.roll(kn, HEAD_DIM, 1).astype(BF16)), \
        (_with_ones(v.astype(BF16)), _with_ones(pltpu.roll(v, HEAD_DIM, 1).astype(BF16)))


def _p1_kernel(sink_ref, x_ref, m_ref, ng_ref, win_hbm, gqg_ref, gkg_ref, sqg_ref, skg_ref, wout_hbm,
               xo_ref, gk_ref, gv_ref, sk_ref, sv_ref, y_scr, win_ref, wout_ref, sem):
    first = pl.program_id(0) == 0
    wait_win = _fetch_once(first, win_hbm, win_ref, sem.at[0])
    wait_wout = _fetch_once(first, wout_hbm, wout_ref, sem.at[1])
    wait_win()
    nbs = x_ref.shape[0]
    x = x_ref[...].reshape(nbs * SEQ, D_MODEL)
    h, gate = _modulate(x, ng_ref[...], m_ref[0:1, :])
    hb = h.astype(BF16)
    lo = _lane_lo()
    hi = jnp.logical_not(lo)
    sc = HEAD_DIM ** -0.5 * LOG2E
    rows = [slice(bi * SEQ, (bi + 1) * SEQ) for bi in range(nbs)]

    branches = ((O_QC, O_KC, O_VC, O_GC, gqg_ref, gkg_ref, gk_ref, gv_ref, False, 0),
                (O_QD, O_KD, O_VD, O_GD, sqg_ref, skg_ref, sk_ref, sv_ref, True, 4 * LANES))
    for oq, ok, ov, og, qg_ref, kg_ref, ck_ref, cv_ref, has_sink, yoff in branches:
        zq = _dot(hb, win_ref[:, oq:oq + 4 * LANES])
        zkv = _dot(hb, win_ref[:, ok:ok + 2 * LANES])
        zg = _dot(hb, win_ref[:, og:og + 4 * LANES])
        v = zkv[:, LANES:]
        kn, ks, vs = _gqa_pair_operands(zkv[:, :LANES], v, kg_ref[...], lo)
        for bi, rs in enumerate(rows):
            _store_pair_transposed(ck_ref, bi, 0, kn[rs])
            _store_pair_transposed(cv_ref, bi, 0, v[rs])
        qg = qg_ref[...] * sc
        for p in range(N_PAIRS):
            sl = slice(p * LANES, (p + 1) * LANES)
            qn = _rms_halves(zq[:, sl], qg, lo)
            qms = [jnp.where(keep, qn, 0.0).astype(BF16) for keep in (lo, hi)]
            kv = p // 2
            for bi, rs in enumerate(rows):
                o2 = []
                for half in (0, 1):
                    swap = 0 if kv == half else 1
                    sink = sink_ref[2 * p + half] * LOG2E if has_sink else None
                    o2.append(_attend([(_dot_nt(qms[half][rs], ks[swap][rs]), vs[swap][rs], False)], sink))
                o = jnp.where(lo, o2[0], o2[1])
                y_scr[rs, yoff + p * LANES:yoff + (p + 1) * LANES] = (o * _silu(zg[rs, sl])).astype(BF16)

    wait_wout()
    xo_ref[...] = (x + gate * _dot(y_scr[...], wout_ref[...])).reshape(nbs, SEQ, D_MODEL)


def _prompt_odd(sink, x, m, ng, win, gqg, gkg, sqg, skg, wout):
    nb = x.shape[0]
    nbs = PROMPT_ODD_BATCHES_PER_STEP
    assert nb % nbs == 0
    ins = (m, ng, _Hbm(win), gqg, gkg, sqg, skg, _Hbm(wout))
    cache_spec = pl.BlockSpec((nbs, 1, 2, HEAD_DIM, SEQ), lambda b: (b, 0, 0, 0, 0))
    cache_shape = jax.ShapeDtypeStruct((nb, 1, 2, HEAD_DIM, SEQ), F32)
    return pl.pallas_call(
        _p1_kernel,
        grid=(nb // nbs,),
        in_specs=[pl.BlockSpec(memory_space=pltpu.SMEM),
                  pl.BlockSpec((nbs, SEQ, D_MODEL), lambda b: (b, 0, 0))] + [_spec(a) for a in ins],
        out_specs=[pl.BlockSpec((nbs, SEQ, D_MODEL), lambda b: (b, 0, 0))] + [cache_spec] * 4,
        out_shape=[jax.ShapeDtypeStruct((nb, SEQ, D_MODEL), F32)] + [cache_shape] * 4,
        scratch_shapes=[pltpu.VMEM((nbs * SEQ, D_MODEL), BF16), pltpu.VMEM(win.shape, BF16),
                        pltpu.VMEM(wout.shape, BF16), pltpu.SemaphoreType.DMA((2,))],
        compiler_params=pltpu.CompilerParams(dimension_semantics=("arbitrary",), vmem_limit_bytes=VMEM_LIMIT),
        name="prompt_odd",
    )(sink, x, *map(_arr, ins))


def _s0a_kernel(x_ref, m_ref, ng_ref, win_ref, qag_ref, wq_ref, kvag_ref, wkk_ref, wkv_ref, qg_ref, kg_ref,
                naqg_ref, nakg_ref, cos_ref, sin_ref,
                qa_ref, ka_ref, va_ref, qb_ref, kb_ref, vb_ref, g_ref):
    b = pl.program_id(0)
    lo = _lane_lo()
    partner = _rope_matrix2(ROPE_A, LANES, NOPE_A)
    hb = _modulate(x_ref[0], ng_ref[...], m_ref[pl.ds(1 + b, 1), :])[0].astype(BF16)
    cos, sin = cos_ref[...], sin_ref[...]

    qln = _rms(_dot_nt(hb, win_ref[E_QLAT:E_CKV, :]), qag_ref[...], Q_RANK).astype(BF16)
    q_all = _dot(qln, wq_ref[...])
    ckv_n = _rms(_dot_nt(hb, win_ref[E_CKV:E_KROPE, :]), kvag_ref[...], KV_RANK)
    kr = _dot_nt(hb, _rope_key_slab(win_ref))
    cb = ckv_n.astype(BF16)
    kk = _dot(cb, wkk_ref[...])
    va_ref[0] = _dot(cb, wkv_ref[...]).astype(BF16)
    zq = _dot_nt(hb, win_ref[E_QB:E_KB, :])
    zk = _dot_nt(hb, win_ref[E_KB:E_VB, :])
    vb_ref[0] = _dot_nt(hb, win_ref[E_VB:E_GB, :]).astype(BF16)
    g_ref[0, :, 0:4 * LANES] = _silu(_dot_nt(hb, win_ref[E_GA:E_QB, :]))
    g_ref[0, :, 4 * LANES:8 * LANES] = _silu(_dot_nt(hb, win_ref[E_GB:E_END, :]))

    qg = qg_ref[...] * (QK_A ** -0.5 * LOG2E)
    kg = kg_ref[...]
    k_partner = _lane_mix(kr * kg, partner) * sin
    for hh in range(N_HEADS):
        sl = slice(hh * LANES, (hh + 1) * LANES)
        qn = _rms(q_all[:, sl], qg, QK_A)
        qa_ref[0, :, sl] = (qn * cos + _lane_mix(qn, partner) * sin).astype(BF16)
        k_raw = kk[:, sl] + kr
        k_inv = lax.rsqrt(jnp.sum(k_raw * k_raw, axis=-1, keepdims=True) / QK_A + EPS)
        ka_ref[0, :, sl] = ((k_raw * kg * cos + k_partner) * k_inv).astype(BF16)
    naqg = naqg_ref[...] * (HEAD_DIM ** -0.5 * LOG2E)
    for p in range(N_PAIRS):
        sl = slice(p * LANES, (p + 1) * LANES)
        qb_ref[0, :, sl] = _rms_halves(zq[:, sl], naqg, lo).astype(BF16)
        kb_ref[0, :, sl] = _rms_halves(zk[:, sl], nakg_ref[...], lo).astype(BF16)


def _sample_even_proj(x, m, ng, win, qag, wq, kvag, wkk, wkv, qg, kg, naqg, nakg, cos, sin):
    nb, s, _ = x.shape
    nq = s // PROJ_BLOCK
    ins = (m, ng, win, qag, wq, kvag, wkk, wkv, qg, kg, naqg, nakg)
    tab = pl.BlockSpec((PROJ_BLOCK, LANES), lambda b, j: (j, 0))

    def blk(w):
        return pl.BlockSpec((1, PROJ_BLOCK, w), lambda b, j: (b, j, 0))

    def shp(w, dt):
        return jax.ShapeDtypeStruct((nb, s, w), dt)

    return pl.pallas_call(
        _s0a_kernel,
        grid=(nb, nq),
        in_specs=[blk(D_MODEL)] + [_spec(a) for a in ins] + [tab, tab],
        out_specs=[blk(1024), blk(1024), blk(512), blk(512), blk(512), blk(512), blk(1024)],
        out_shape=[shp(1024, BF16), shp(1024, BF16), shp(512, BF16), shp(512, BF16), shp(512, BF16),
                   shp(512, BF16), shp(1024, F32)],
        compiler_params=pltpu.CompilerParams(dimension_semantics=("arbitrary", "arbitrary"),
                                             vmem_limit_bytes=VMEM_LIMIT),
        name="sample_even_proj",
    )(x, *map(_arr, ins), cos, sin)


def _build_bias_table(rpb_ref, tile_scr, tab_ref):
    qc = lax.broadcasted_iota(jnp.int32, (GRID_W, LANES), 0)
    lane = lax.broadcasted_iota(jnp.int32, (GRID_W, LANES), 1)
    kc = jnp.bitwise_and(lane, GRID_W - 1)
    lo = lane < GRID_W
    cs = jnp.clip(qc - NA_COLS // 2, 0, GRID_W - NA_COLS)
    valid = (kc >= cs) & (kc < cs + NA_COLS)
    tab_ref[...] = jnp.zeros(tab_ref.shape, F32)
    tile_scr[RPB_ROWS] = jnp.zeros((GRID_W, LANES), F32)

    def per_head(h, carry):
        r = rpb_ref[h] * LOG2E
        half = jnp.concatenate([r[:, NA_COLS - 1:], jnp.zeros((RPB_ROWS, GRID_W - RPB_COLS), F32),
                                r[:, :NA_COLS - 1]], axis=1)
        base = jnp.concatenate([half, half], axis=1)
        for dr in range(RPB_ROWS):
            t = pltpu.roll(jnp.broadcast_to(base[dr:dr + 1], (GRID_W, LANES)), 0, 1, stride=1, stride_axis=0)
            tile_scr[dr] = jnp.where(valid, t, NEG_INF)
        for c in range(NA_ROWS // 2, NA_ROWS // 2 + NA_ROWS):
            d0 = 2 * c - NA_ROWS
            tab_ref[0, h, c] = jnp.where(lo, tile_scr[d0], tile_scr[d0 + 1])
            tab_ref[1, h, c] = jnp.where(lo, tile_scr[d0 - 1 if d0 > 0 else RPB_ROWS], tile_scr[d0])
        return carry

    lax.fori_loop(0, N_HEADS, per_head, 0)


def _s0b_kernel(rpb_ref, x_ref, m_ref, qa_ref, ka_ref, va_ref, qb_ref, kb_ref, vb_ref, g_ref,
                cckv_ref, ckr_ref, cnk_ref, cnv_ref, wkk_ref, wkv_ref, kg_ref, wout_ref,
                xo_ref, kca_scr, vca_scr, tile_scr, tab_scr, y_scr):
    b = pl.program_id(0)
    j = pl.program_id(1)
    lo = _lane_lo()
    n_lat = ka_ref.shape[1]

    @pl.when((b == 0) & (j == 0))
    def _():
        _build_bias_table(rpb_ref, tile_scr, tab_scr)

    @pl.when(j == 0)
    def _():
        kr_t = jnp.concatenate([jnp.zeros((NOPE_A, PAST_LEN), F32), ckr_ref[0],
                                jnp.zeros((LANES - QK_A, PAST_LEN), F32)], axis=0)
        keys, vals = _mla_keys(cckv_ref[0].astype(BF16), kr_t.T, wkk_ref, wkv_ref, kg_ref[...])
        for hh in range(N_HEADS):
            kca_scr[:, hh * LANES:(hh + 1) * LANES] = keys[hh]
        vca_scr[...] = vals

    kidx = lax.broadcasted_iota(jnp.int32, (1, n_lat), 1)
    for p in range(N_PAIRS):
        sl = slice(p * LANES, (p + 1) * LANES)
        o2 = []
        va = _with_ones(va_ref[0, :, sl])
        vca = _with_ones(vca_scr[:, sl])
        for hh in (2 * p, 2 * p + 1):
            hs = slice(hh * LANES, (hh + 1) * LANES)
            q = qa_ref[0, :, hs]
            o2.append(_attend([(_dot_nt(q, ka_ref[0, :, hs]), va, False),
                               (_dot_nt(q, kca_scr[:, hs]), vca, False)]))
        oa = jnp.where(lo, o2[0], o2[1])
        y_scr[:, sl] = (oa * g_ref[0, :, sl]).astype(BF16)

        qb = qb_ref[0, :, sl]
        kb = kb_ref[0, :, sl]
        vb = _with_ones(vb_ref[0, :, sl])
        kcb = cnk_ref[0, sl, :].astype(BF16)
        vcb = _with_ones(cnv_ref[0, sl, :].astype(BF16), transposed=True)
        o2 = []
        for half in (0, 1):
            head = 2 * p + half
            qm = jnp.where(lo if half == 0 else jnp.logical_not(lo), qb, jnp.zeros_like(qb))
            s_lat = _dot_nt(qm, kb)
            rows = []
            for local in range(NA_Q_BLOCK // GRID_W):
                qr = j * (NA_Q_BLOCK // GRID_W) + local
                par = 0 if local % 2 == 1 else 1
                c0 = (RPB_ROWS + par - local) // 2 - (NA_Q_BLOCK // GRID_W // 2) * j
                bias = jnp.concatenate([tab_scr[par, head, c0 + t] for t in range(n_lat // LANES)], axis=1)
                r0 = jnp.clip(qr - NA_ROWS // 2, 0, n_lat // GRID_W - NA_ROWS) * GRID_W
                ok = (kidx >= r0) & (kidx < r0 + NA_ROWS * GRID_W)
                bias = bias + jnp.where(ok, 0.0, NEG_INF)
                rows.append(s_lat[local * GRID_W:(local + 1) * GRID_W] + bias)
            s_lat = jnp.concatenate(rows, axis=0)
            o2.append(_attend([(s_lat, vb, False), (_dot(qm, kcb), vcb, True)]))
        ob = jnp.where(lo, o2[0], o2[1])
        ys = slice(4 * LANES + p * LANES, 4 * LANES + (p + 1) * LANES)
        y_scr[:, ys] = (ob * g_ref[0, :, ys]).astype(BF16)

    d = x_ref.shape[-1]
    gate = m_ref[pl.ds(1 + b, 1), 2 * d:]
    xo_ref[0] = x_ref[0] + gate * _dot(y_scr[...], wout_ref[...])


def _sample_even_attn(rpb, x, m, qa, ka, va, qb, kb, vb, g, cckv, ckr, cnk, cnv, wkk, wkv, kg, wout):
    nb, s, _ = x.shape
    nq = s // NA_Q_BLOCK

    def blk(w):
        return pl.BlockSpec((1, NA_Q_BLOCK, w), lambda b, j: (b, j, 0))

    def per_batch(a):
        return pl.BlockSpec((1,) + a.shape[1:], lambda b, j: (b, 0, 0))

    return pl.pallas_call(
        _s0b_kernel,
        grid=(nb, nq),
        in_specs=[_full(rpb.shape), blk(D_MODEL), _spec(m),
                  blk(1024), per_batch(ka), per_batch(va), blk(512), per_batch(kb), per_batch(vb), blk(1024),
                  per_batch(cckv), per_batch(ckr), per_batch(cnk), per_batch(cnv),
                  _full(wkk.shape), _full(wkv.shape), _spec(kg), _full(wout.shape)],
        out_specs=blk(D_MODEL),
        out_shape=jax.ShapeDtypeStruct(x.shape, F32),
        scratch_shapes=[pltpu.VMEM((PAST_LEN, N_HEADS * LANES), BF16),
                        pltpu.VMEM((PAST_LEN, N_HEADS * HEAD_DIM), BF16),
                        pltpu.VMEM((RPB_ROWS + 1, GRID_W, LANES), F32),
                        pltpu.VMEM((2, N_HEADS, BIAS_CHUNKS, GRID_W, LANES), F32),
                        pltpu.VMEM((NA_Q_BLOCK, D_MODEL), BF16)],
        compiler_params=pltpu.CompilerParams(dimension_semantics=("arbitrary", "arbitrary"),
                                             vmem_limit_bytes=VMEM_LIMIT),
        name="sample_even_attn",
    )(rpb, x, _arr(m), qa, ka, va, qb, kb, vb, g, cckv, ckr, cnk, cnv, wkk, wkv, _arr(kg), wout)


def _s1a_kernel(x_ref, m_ref, ng_ref, win_ref, gqg_ref, gkg_ref, sqg_ref, skg_ref, cos_ref, sin_ref,
                qc_ref, kc_ref, vc_ref, qd_ref, kd_ref, vd_ref, g_ref):
    b = pl.program_id(0)
    lo = _lane_lo()
    partner = _rope_matrix2(HEAD_DIM, HEAD_DIM, 0)
    swap = _swap_matrix2()[:LANES]
    hb = _modulate(x_ref[0], ng_ref[...], m_ref[pl.ds(1 + b, 1), :])[0].astype(BF16)
    cos, sin = cos_ref[...], sin_ref[...]
    sc = HEAD_DIM ** -0.5 * LOG2E

    def rope(t):
        return t * cos + _lane_mix(t, partner) * sin

    branches = ((O_QC, O_KC, O_GC, gqg_ref, gkg_ref, qc_ref, kc_ref, vc_ref, 0),
                (O_QD, O_KD, O_GD, sqg_ref, skg_ref, qd_ref, kd_ref, vd_ref, 4 * LANES))
    for oq, ok, og, qg_ref, kg_ref, q_out, k_out, v_out, goff in branches:
        zq = _dot(hb, win_ref[:, oq:oq + 4 * LANES])
        zkv = _dot(hb, win_ref[:, ok:ok + 2 * LANES])
        qg = qg_ref[...] * sc
        for p in range(N_PAIRS):
            sl = slice(p * LANES, (p + 1) * LANES)
            q_out[0, :, sl] = rope(_rms_halves(zq[:, sl], qg, lo)).astype(BF16)
        kn = rope(_rms_halves(zkv[:, :LANES], kg_ref[...], lo))
        v = zkv[:, LANES:]
        for out, val in ((k_out, kn.astype(BF16)), (v_out, v.astype(BF16))):
            out[0, :, 0:LANES] = val
            out[0, :, LANES:2 * LANES] = _dot(val, swap).astype(BF16)
        g_ref[0, :, goff:goff + 4 * LANES] = _silu(_dot(hb, win_ref[:, og:og + 4 * LANES]))


def _sample_odd_proj(x, m, ng, win, gqg, gkg, sqg, skg, cos, sin):
    nb, s, _ = x.shape
    nq = s // PROJ_BLOCK
    ins = (m, ng, win, gqg, gkg, sqg, skg)
    tab = pl.BlockSpec((PROJ_BLOCK, LANES), lambda b, j: (j, 0))

    def blk(w):
        return pl.BlockSpec((1, PROJ_BLOCK, w), lambda b, j: (b, j, 0))

    def shp(w, dt):
        return jax.ShapeDtypeStruct((nb, s, w), dt)

    return pl.pallas_call(
        _s1a_kernel,
        grid=(nb, nq),
        in_specs=[blk(D_MODEL)] + [_spec(a) for a in ins] + [tab, tab],
        out_specs=[blk(512), blk(256), blk(256), blk(512), blk(256), blk(256), blk(1024)],
        out_shape=[shp(512, BF16), shp(256, BF16), shp(256, BF16), shp(512, BF16), shp(256, BF16),
                   shp(256, BF16), shp(1024, F32)],
        compiler_params=pltpu.CompilerParams(dimension_semantics=("arbitrary", "arbitrary"),
                                             vmem_limit_bytes=VMEM_LIMIT),
        name="sample_odd_proj",
    )(x, *map(_arr, ins), cos, sin)


def _s1b_kernel(sink_ref, x_ref, m_ref, qc_ref, kc_ref, vc_ref, qd_ref, kd_ref, vd_ref, g_ref,
                cgk_ref, cgv_ref, csk_ref, csv_ref, wout_ref, xo_ref, y_scr):
    b = pl.program_id(0)
    j = pl.program_id(1)
    lo = _lane_lo()
    n_lat = kc_ref.shape[1]
    win_keys = Q_BLOCK + 2 * SWA_HALF

    def ctx_pair(ref, values=False):
        a = ref[0].astype(BF16)
        pair = (a, _swap_halves(a))
        return tuple(_with_ones(t, transposed=True) for t in pair) if values else pair

    cgk, cgv, csk, csv = ctx_pair(cgk_ref), ctx_pair(cgv_ref, True), ctx_pair(csk_ref), ctx_pair(csv_ref, True)
    vcs = [_with_ones(vc_ref[0, :, w * LANES:(w + 1) * LANES]) for w in (0, 1)]

    ks = pl.multiple_of(jnp.clip(j * Q_BLOCK - SWA_HALF, 0, n_lat - win_keys), SWA_HALF)
    qpos = j * Q_BLOCK + lax.broadcasted_iota(jnp.int32, (Q_BLOCK, win_keys), 0)
    kpos = ks + lax.broadcasted_iota(jnp.int32, (Q_BLOCK, win_keys), 1)
    band = jnp.abs(qpos - kpos) <= SWA_HALF
    vds = [_with_ones(vd_ref[0, pl.ds(ks, win_keys), w * LANES:(w + 1) * LANES]) for w in (0, 1)]

    for p in range(N_PAIRS):
        sl = slice(p * LANES, (p + 1) * LANES)
        kv = p // 2
        qc = qc_ref[0, :, sl]
        qd = qd_ref[0, :, sl]
        oc2, od2 = [], []
        for half in (0, 1):
            swap = 0 if kv == half else 1
            ws = slice(swap * LANES, (swap + 1) * LANES)
            keep = lo if half == 0 else jnp.logical_not(lo)
            qm = jnp.where(keep, qc, jnp.zeros_like(qc))
            oc2.append(_attend([(_dot_nt(qm, kc_ref[0, :, ws]), vcs[swap], False),
                                (_dot(qm, cgk[swap]), cgv[swap], True)]))
            qm = jnp.where(keep, qd, jnp.zeros_like(qd))
            s_loc = jnp.where(band, _dot_nt(qm, kd_ref[0, pl.ds(ks, win_keys), ws]), NEG_INF)
            od2.append(_attend([(s_loc, vds[swap], False),
                                (_dot(qm, csk[swap]), csv[swap], True)], sink_ref[2 * p + half] * LOG2E))
        y_scr[:, sl] = (jnp.where(lo, oc2[0], oc2[1]) * g_ref[0, :, sl]).astype(BF16)
        ys = slice(4 * LANES + p * LANES, 4 * LANES + (p + 1) * LANES)
        y_scr[:, ys] = (jnp.where(lo, od2[0], od2[1]) * g_ref[0, :, ys]).astype(BF16)

    d = x_ref.shape[-1]
    gate = m_ref[pl.ds(1 + b, 1), 2 * d:]
    xo_ref[0] = x_ref[0] + gate * _dot(y_scr[...], wout_ref[...])


def _sample_odd_attn(sink, x, m, qc, kc, vc, qd, kd, vd, g, cgk, cgv, csk, csv, wout):
    nb, s, _ = x.shape
    nq = s // Q_BLOCK

    def blk(w):
        return pl.BlockSpec((1, Q_BLOCK, w), lambda b, j: (b, j, 0))

    def per_batch(a):
        return pl.BlockSpec((1,) + a.shape[1:], lambda b, j: (b, 0, 0))

    return pl.pallas_call(
        _s1b_kernel,
        grid=(nb, nq),
        in_specs=[pl.BlockSpec(memory_space=pltpu.SMEM), blk(D_MODEL), _spec(m),
                  blk(512), per_batch(kc), per_batch(vc), blk(512), per_batch(kd), per_batch(vd), blk(1024),
                  per_batch(cgk), per_batch(cgv), per_batch(csk), per_batch(csv), _full(wout.shape)],
        out_specs=blk(D_MODEL),
        out_shape=jax.ShapeDtypeStruct(x.shape, F32),
        scratch_shapes=[pltpu.VMEM((Q_BLOCK, D_MODEL), BF16)],
        compiler_params=pltpu.CompilerParams(dimension_semantics=("arbitrary", "arbitrary"),
                                             vmem_limit_bytes=VMEM_LIMIT),
        name="sample_odd_attn",
    )(sink, x, _arr(m), qc, kc, vc, qd, kd, vd, g, cgk, cgv, csk, csv, wout)


WEIGHT_PREP_STEPS = 8
EVEN_IN_CHUNKS = 6
COND_PREP_ROWS = 256
BF16_SUBLANES = 16


G_MLA_Q, G_MLA_K, G_NA_Q, G_NA_K, G_GQA_Q, G_GQA_K, G_SWA_Q, G_SWA_K, N_GAINS = range(9)


GAIN_WIDTHS = (QK_A, QK_A) + (HEAD_DIM,) * 6


def _cond_prep_kernel(n_cond, ct_ref, wm_ref, bm_ref, ng_ref, gains_ref, mo_ref, gt_ref, ngt_ref):
    _mod_step(n_cond, pl.program_id(0) == 0, bm_ref[0:1, :], ct_ref, wm_ref, mo_ref)
    gt_ref[...] = jnp.zeros(gt_ref.shape, F32)
    start = 0
    for r, w in enumerate(GAIN_WIDTHS):
        g = gains_ref[:, start:start + w]
        start += w
        for off in range(0, LANES - w + 1, w):
            gt_ref[r, :, off:off + w] = g
    for layer in range(ngt_ref.shape[0]):
        ngt_ref[layer] = ng_ref[layer:layer + 1, :]


def _cond_prep(cond_t, n_cond, w_mod, b_mod, norm_g, gains):
    assert len(gains) == N_GAINS and tuple(g.shape[-1] for g in gains) == GAIN_WIDTHS
    gains_row = jnp.concatenate([g.reshape(1, -1) for g in gains], axis=1)
    tk = COND_PREP_ROWS
    return pl.pallas_call(
        functools.partial(_cond_prep_kernel, n_cond),
        grid=(D_MODEL // tk,),
        in_specs=[pl.BlockSpec((tk, 8), lambda k: (k, 0)),
                  pl.BlockSpec((None, tk, 3 * D_MODEL), lambda k: (0, k, 0)),
                  _full(b_mod.shape), _full(norm_g.shape), _full(gains_row.shape)],
        out_specs=[_full((8, 3 * D_MODEL)), _full((N_GAINS, 1, LANES)), _full((norm_g.shape[0], 1, D_MODEL))],
        out_shape=[jax.ShapeDtypeStruct((8, 3 * D_MODEL), F32), jax.ShapeDtypeStruct((N_GAINS, 1, LANES), F32),
                   jax.ShapeDtypeStruct((norm_g.shape[0], 1, D_MODEL), F32)],
        compiler_params=pltpu.CompilerParams(dimension_semantics=("arbitrary",)),
        name="cond_prep",
    )(cond_t, w_mod, b_mod, norm_g, gains_row)


def _weight_prep_kernel(wie_ref, woe_ref, wqu_ref, wkv_ref, win_e_ref, wout_e_ref, wq_ref, wkk_ref, wkvv_ref):
    win_e_ref[...] = wie_ref[...].astype(BF16)
    wout_e_ref[...] = woe_ref[0].astype(BF16)

    wq_ref[...] = jnp.zeros(wq_ref.shape, BF16)
    for h in range(N_HEADS):
        wq_ref[:, h * LANES:h * LANES + QK_A] = wqu_ref[0, :, h * QK_A:(h + 1) * QK_A].astype(BF16)
    lo = _lane_lo()
    for p in range(N_PAIRS):
        a = wkv_ref[0, :, (2 * p) * LANES:(2 * p + 1) * LANES]
        c = wkv_ref[0, :, (2 * p + 1) * LANES:(2 * p + 2) * LANES]
        wkk_ref[:, (2 * p) * LANES:(2 * p + 1) * LANES] = jnp.where(lo, a, 0.0).astype(BF16)
        wkk_ref[:, (2 * p + 1) * LANES:(2 * p + 2) * LANES] = jnp.where(lo, c, 0.0).astype(BF16)
        wkvv_ref[:, p * LANES:(p + 1) * LANES] = jnp.where(lo, pltpu.roll(a, HEAD_DIM, 1), c).astype(BF16)


def _weight_prep(w_in_even_t, w_out_even, w_q_up, w_kv_up):
    n = WEIGHT_PREP_STEPS
    ins = (w_out_even, w_q_up, w_kv_up)
    out_cols = (D_MODEL, N_HEADS * LANES, N_HEADS * LANES, N_HEADS * HEAD_DIM)
    out_rows = (D_MODEL, Q_RANK, KV_RANK, KV_RANK)
    te = E_END // EVEN_IN_CHUNKS
    assert te * EVEN_IN_CHUNKS == E_END and te % BF16_SUBLANES == 0 and EVEN_IN_CHUNKS <= n
    even_spec = pl.BlockSpec((te, D_MODEL), lambda i: (jnp.minimum(i, EVEN_IN_CHUNKS - 1), 0))
    return pl.pallas_call(
        _weight_prep_kernel,
        grid=(n,),
        in_specs=[even_spec] + [pl.BlockSpec((1, a.shape[1] // n, a.shape[2]), lambda i: (0, i, 0)) for a in ins],
        out_specs=[even_spec] + [pl.BlockSpec((r // n, c), lambda i: (i, 0)) for r, c in zip(out_rows, out_cols)],
        out_shape=[jax.ShapeDtypeStruct((E_END, D_MODEL), BF16)]
        + [jax.ShapeDtypeStruct((r, c), BF16) for r, c in zip(out_rows, out_cols)],
        compiler_params=pltpu.CompilerParams(dimension_semantics=("arbitrary",), vmem_limit_bytes=VMEM_LIMIT),
        name="weight_prep",
    )(w_in_even_t, *ins)


def _feature_major(c):
    b, h, l, d = c.shape
    return jnp.swapaxes(c, -1, -2).reshape(b, h * d, l)


def _token_major(c):
    return jnp.swapaxes(c, -1, -2)


def _rope_tables(s, rot_dim, period, start):
    quarter = rot_dim // 4
    t = np.arange(s)
    inv = ROPE_THETA ** (-np.arange(quarter, dtype=np.float64) / quarter)
    row = (t // GRID_W).astype(np.float64)[:, None] * inv
    col = (t % GRID_W).astype(np.float64)[:, None] * inv
    ang = np.concatenate([row, col], axis=-1)
    cos, sin = np.cos(ang), np.sin(ang)
    pre = np.ones((s, start))
    post = np.zeros((s, period - start - rot_dim))
    c = np.concatenate([pre, cos, cos, post], axis=-1)
    sn = np.concatenate([0 * pre, sin, sin, post], axis=-1)
    rep = LANES // period
    return jnp.asarray(np.tile(c, (1, rep)), F32), jnp.asarray(np.tile(sn, (1, rep)), F32)


def kernel(x_prompt, x_sample, cache_mla_ckv, cache_mla_krope, cache_na_k, cache_na_v, cache_gqa_k, cache_gqa_v, cache_swa_k, cache_swa_v, c, c_ctx, norm_g, w_mod, b_mod, w_in_even, mla_qa_g, w_q_up, mla_kva_g, w_kv_up, mla_q_g, mla_k_g, na_q_g, na_k_g, na_rpb, w_out_even, w_in_odd, gqa_q_g, gqa_k_g, swa_q_g, swa_k_g, swa_sink, w_out_odd):
    n_dec = x_sample.shape[0]
    assert w_mod.shape[0] == 2 and n_dec + 1 <= 8

    cond_t = jnp.concatenate([c_ctx[:, None], c.T, jnp.zeros((D_MODEL, 7 - n_dec), F32)], axis=1)
    n_cond = 1 + n_dec
    gains = (mla_q_g, mla_k_g, na_q_g, na_k_g, gqa_q_g, gqa_k_g, swa_q_g, swa_k_g)
    win_e, wout_e, wq, wkk, wkv = _weight_prep(jnp.swapaxes(w_in_even[0], 0, 1), w_out_even, w_q_up, w_kv_up)
    m_even, gt, ngt = _cond_prep(cond_t, n_cond, w_mod, b_mod, norm_g, gains)
    even = (_Row(ngt, 0), win_e, mla_qa_g, wq, mla_kva_g, wkk, wkv,
            _Row(gt, G_MLA_Q), _Row(gt, G_MLA_K), _Row(gt, G_NA_Q), _Row(gt, G_NA_K))
    sink = swa_sink[0].astype(F32)

    xp1, new_ckv, new_krope, new_na_k, new_na_v, m_odd, win_o, wout_o = _prompt_even(
        x_prompt, m_even, *even, wout_e, cond_t, n_cond, w_mod, b_mod, w_in_odd, w_out_odd)
    odd = (_Row(ngt, 1), win_o, _Row(gt, G_GQA_Q), _Row(gt, G_GQA_K), _Row(gt, G_SWA_Q), _Row(gt, G_SWA_K))
    xp2, new_gqa_k, new_gqa_v, new_swa_k, new_swa_v = _prompt_odd(sink, xp1, m_odd, *odd, wout_o)

    cos_e, sin_e = _rope_tables(DEC_SEQ, ROPE_A, LANES, NOPE_A)
    qa, ka, va, qbs, kbs, vbs, g0 = _sample_even_proj(x_sample, m_even, *even, cos_e, sin_e)
    ckr = jnp.swapaxes(cache_mla_krope[:, 0], -1, -2)
    xs1 = _sample_even_attn(na_rpb[0], x_sample, m_even, qa, ka, va, qbs, kbs, vbs, g0,
                            cache_mla_ckv[:, 0], ckr, _feature_major(cache_na_k[:, 0]),
                            _feature_major(cache_na_v[:, 0]), wkk, wkv, _Row(gt, G_MLA_K), wout_e)
    cos_o, sin_o = _rope_tables(DEC_SEQ, HEAD_DIM, HEAD_DIM, 0)
    qc, kc, vc, qd, kd, vd, g1 = _sample_odd_proj(xs1, m_odd, *odd, cos_o, sin_o)
    xs2 = _sample_odd_attn(sink, xs1, m_odd, qc, kc, vc, qd, kd, vd, g1,
                           _feature_major(cache_gqa_k[:, 0]), _feature_major(cache_gqa_v[:, 0]),
                           _feature_major(cache_swa_k[:, 0]), _feature_major(cache_swa_v[:, 0]), wout_o)

    caches = (new_krope, new_na_k, new_na_v, new_gqa_k, new_gqa_v, new_swa_k, new_swa_v)
    return (xp2, xs2, new_ckv) + tuple(_token_major(c) for c in caches)
```

```python
import functools
from typing import NamedTuple

import jax
import jax.numpy as jnp
import numpy as np
from jax import lax
from jax.experimental import pallas as pl
from jax.experimental.pallas import tpu as pltpu

F32 = jnp.float32
BF16 = jnp.bfloat16

D_MODEL = 1024
SEQ = 256
DEC_SEQ = 1024
PAST_LEN = 256
GRID_W = 64
HEAD_DIM = 64
Q_RANK = 256
KV_RANK = 128
NOPE_A = 64
ROPE_A = 32
QK_A = NOPE_A + ROPE_A
N_HEADS = 8
NA_ROWS = 8
NA_COLS = 16
SWA_HALF = 128
ROPE_THETA = 10000.0
EPS = 1e-6
NEG_INF = -1e30
LOG2E = 1.4426950408889634

LANES = 128
Q_BLOCK = 512
NA_Q_BLOCK = 256
PROJ_BLOCK = 512
PROMPT_BATCHES_PER_STEP = 2
PROMPT_ODD_BATCHES_PER_STEP = 4
N_PAIRS = N_HEADS // 2
RPB_ROWS = 2 * NA_ROWS - 1
RPB_COLS = 2 * NA_COLS - 1
BIAS_CHUNKS = 16
VMEM_LIMIT = 48 * 1024 * 1024

E_QLAT, E_CKV, E_KROPE, E_GA, E_QB, E_KB, E_VB, E_GB, E_END = 0, 256, 384, 416, 928, 1440, 1952, 2464, 2976
O_QC, O_KC, O_VC, O_GC, O_QD, O_KD, O_VD, O_GD, O_END = 0, 512, 640, 768, 1280, 1792, 1920, 2048, 2560


def _dot(a, b):
    return lax.dot_general(a, b, (((1,), (0,)), ((), ())), preferred_element_type=F32)


def _dot_nt(a, b):
    return lax.dot_general(a, b, (((1,), (1,)), ((), ())), preferred_element_type=F32)


def _silu(x):
    return x / (1.0 + jnp.exp(-x))


def _rms(x, g, n):
    ss = jnp.sum(x * x, axis=-1, keepdims=True)
    return x * lax.rsqrt(ss / n + EPS) * g


def _rms_halves(x, g2, lo):
    x2 = x * x
    s_lo = jnp.sum(jnp.where(lo, x2, 0.0), axis=-1, keepdims=True)
    s_hi = jnp.sum(jnp.where(lo, 0.0, x2), axis=-1, keepdims=True)
    r = jnp.where(lo, lax.rsqrt(s_lo / HEAD_DIM + EPS), lax.rsqrt(s_hi / HEAD_DIM + EPS))
    return x * r * g2


def _modulate(x, g, m):
    d = x.shape[-1]
    xn = x * lax.rsqrt(jnp.mean(x * x, axis=-1, keepdims=True) + EPS) * g
    return xn * (1.0 + m[:, d:2 * d]) + m[:, :d], m[:, 2 * d:]


def _split_lanes(x):
    hi = x.astype(BF16)
    lo = (x - hi.astype(F32)).astype(BF16)
    return jnp.concatenate([hi, lo], axis=1)


def _lane_matrix2(entries):
    i = lax.broadcasted_iota(jnp.int32, (LANES, LANES), 0)
    j = lax.broadcasted_iota(jnp.int32, (LANES, LANES), 1)
    m = entries(i, j).astype(BF16)
    return jnp.concatenate([m, m], axis=0)


def _rope_matrix2(rot_dim, period, start):
    half = rot_dim // 2

    def entries(i, j):
        pos = jnp.bitwise_and(j, period - 1) - start
        neg = (pos >= 0) & (pos < half) & (i == j + half)
        plus = (pos >= half) & (pos < rot_dim) & (i == j - half)
        return jnp.where(neg, -1.0, jnp.where(plus, 1.0, 0.0))

    return _lane_matrix2(entries)


def _swap_matrix2():
    return _lane_matrix2(lambda i, j: jnp.where(i == jnp.bitwise_xor(j, HEAD_DIM), 1.0, 0.0))


def _lane_mix(x, m2):
    return _dot(_split_lanes(x), m2)


def _with_ones(v, transposed=False):
    if transposed:
        return jnp.concatenate([v, jnp.ones((LANES, v.shape[1]), v.dtype)], axis=0)
    return jnp.concatenate([v, jnp.ones((v.shape[0], LANES), v.dtype)], axis=1)


def _attend(parts, sink=None):
    mx = None
    for s, _, _ in parts:
        pm = jnp.max(s, axis=-1, keepdims=True)
        mx = pm if mx is None else jnp.maximum(mx, pm)
    if sink is not None:
        mx = jnp.maximum(mx, sink)
    acc, den = None, None
    for s, v, v_t in parts:
        e = jnp.exp2(s - mx)
        po = (_dot_nt if v_t else _dot)(e.astype(BF16), v)
        acc = po if acc is None else acc + po
        if po.shape[1] == LANES:
            ps = jnp.sum(e, axis=-1, keepdims=True)
            den = ps if den is None else den + ps
    if den is None:
        den = acc[:, LANES:]
    if sink is not None:
        den = den + jnp.exp2(sink - mx)
    return acc[:, :LANES] * (1.0 / den)


def _lane_lo():
    return lax.broadcasted_iota(jnp.int32, (1, LANES), 1) < HEAD_DIM


def _store_pair_transposed(ref, bi, p, x):
    xt = x.T
    ref[bi, 0, 2 * p] = xt[:HEAD_DIM]
    ref[bi, 0, 2 * p + 1] = xt[HEAD_DIM:]


def _rope_key_slab(win_ref):
    d = win_ref.shape[1]
    return jnp.concatenate([jnp.zeros((NOPE_A, d), BF16), win_ref[E_KROPE:E_GA, :],
                            jnp.zeros((LANES - QK_A, d), BF16)], axis=0)


def _swap_halves(a):
    return jnp.concatenate([a[HEAD_DIM:], a[:HEAD_DIM]], axis=0)


def _mod_step(n_cond, is_first, bias_row, c_ref, w_ref, o_ref):
    @pl.when(is_first)
    def _():
        o_ref[:n_cond, :] = jnp.broadcast_to(bias_row, (n_cond, o_ref.shape[1]))
        o_ref[n_cond:, :] = jnp.zeros((o_ref.shape[0] - n_cond, o_ref.shape[1]), F32)

    s = _silu(c_ref[...])
    cols = [jnp.broadcast_to(s[:, r:r + 1], (s.shape[0], LANES)) for r in range(n_cond)]
    for t in range(w_ref.shape[1] // LANES):
        sl = slice(t * LANES, (t + 1) * LANES)
        w = w_ref[:, sl]
        for r in range(n_cond):
            o_ref[r:r + 1, sl] += jnp.sum(w * cols[r], axis=0, keepdims=True)


def _mla_keys(cb, kr, wkk_ref, wkv_ref, kg, rope=None):
    kk = _dot(cb, wkk_ref[...])
    keys = []
    for h in range(N_HEADS):
        k = _rms(kk[:, h * LANES:(h + 1) * LANES] + kr, kg, QK_A)
        if rope is not None:
            k = rope(k)
        keys.append(k.astype(BF16))
    return keys, _dot(cb, wkv_ref[...]).astype(BF16)


def _p0_kernel(n_cond, x_ref, m_ref, ng_ref, win_ref, qag_ref, wq_ref, kvag_ref, wkk_ref, wkv_ref, qg_ref, kg_ref,
               naqg_ref, nakg_ref, wout_ref, ct_ref, wm_ref, bm_ref, wio_ref, woo_ref,
               xo_ref, ckv_ref, krope_ref, nak_ref, nav_ref, mo_ref, wino_ref, wouto_ref, y_scr):
    _mod_step(n_cond, pl.program_id(0) == 0, bm_ref[1:2, :], ct_ref, wm_ref, mo_ref)
    wino_ref[...] = wio_ref[0].astype(BF16)
    wouto_ref[...] = woo_ref[0].astype(BF16)

    nbs = x_ref.shape[0]
    x = x_ref[...].reshape(nbs * SEQ, D_MODEL)
    h, gate = _modulate(x, ng_ref[...], m_ref[0:1, :])
    hb = h.astype(BF16)
    lo = _lane_lo()
    hi = jnp.logical_not(lo)
    rows = [slice(bi * SEQ, (bi + 1) * SEQ) for bi in range(nbs)]

    qln = _rms(_dot_nt(hb, win_ref[E_QLAT:E_CKV, :]), qag_ref[...], Q_RANK).astype(BF16)
    q_all = _dot(qln, wq_ref[...])
    ckv_n = _rms(_dot_nt(hb, win_ref[E_CKV:E_KROPE, :]), kvag_ref[...], KV_RANK)
    kr = _dot_nt(hb, _rope_key_slab(win_ref))
    for bi, rs in enumerate(rows):
        ckv_ref[bi, 0] = ckv_n[rs]
        krope_ref[bi, 0] = kr[rs].T[NOPE_A:QK_A]
    keys, vals = _mla_keys(ckv_n.astype(BF16), kr, wkk_ref, wkv_ref, kg_ref[...])
    qg = qg_ref[...] * (QK_A ** -0.5 * LOG2E)

    ga = _dot_nt(hb, win_ref[E_GA:E_QB, :])
    zq = _dot_nt(hb, win_ref[E_QB:E_KB, :])
    zk = _dot_nt(hb, win_ref[E_KB:E_VB, :])
    zv = _dot_nt(hb, win_ref[E_VB:E_GB, :])
    gb = _dot_nt(hb, win_ref[E_GB:E_END, :])
    naqg = naqg_ref[...] * (HEAD_DIM ** -0.5 * LOG2E)

    for p in range(N_PAIRS):
        sl = slice(p * LANES, (p + 1) * LANES)
        ys = slice(4 * LANES + p * LANES, 4 * LANES + (p + 1) * LANES)
        qhs = [_rms(q_all[:, hh * LANES:(hh + 1) * LANES], qg, QK_A).astype(BF16) for hh in (2 * p, 2 * p + 1)]
        qb = _rms_halves(zq[:, sl], naqg, lo)
        kb = _rms_halves(zk[:, sl], nakg_ref[...], lo)
        vb = zv[:, sl]
        kbb, vbb = kb.astype(BF16), vb.astype(BF16)
        va = vals[:, sl]
        qms = [jnp.where(keep, qb, 0.0).astype(BF16) for keep in (lo, hi)]
        for bi, rs in enumerate(rows):
            o2 = [_attend([(_dot_nt(qhs[i][rs], keys[2 * p + i][rs]), va[rs], False)]) for i in (0, 1)]
            y_scr[rs, sl] = (jnp.where(lo, o2[0], o2[1]) * _silu(ga[rs, sl])).astype(BF16)
            _store_pair_transposed(nak_ref, bi, p, kb[rs])
            _store_pair_transposed(nav_ref, bi, p, vb[rs])
            o2 = [_attend([(_dot_nt(qms[i][rs], kbb[rs]), vbb[rs], False)]) for i in (0, 1)]
            y_scr[rs, ys] = (jnp.where(lo, o2[0], o2[1]) * _silu(gb[rs, sl])).astype(BF16)

    xo_ref[...] = (x + gate * _dot(y_scr[...], wout_ref[...])).reshape(nbs, SEQ, D_MODEL)


def _full(shape):
    n = len(shape)
    return pl.BlockSpec(shape, lambda *_: (0,) * n, pipeline_mode=pl.Buffered(1))


class _Row(NamedTuple):
    table: jax.Array
    row: int


def _spec(a):
    if isinstance(a, _Row):
        idx = (a.row,) + (0,) * (a.table.ndim - 1)
        return pl.BlockSpec((None,) + a.table.shape[1:], lambda *_: idx, pipeline_mode=pl.Buffered(1))
    return _full(a.shape)


def _arr(a):
    return a.table if isinstance(a, _Row) else a


def _prompt_even(x, m, ng, win, qag, wq, kvag, wkk, wkv, qg, kg, naqg, nakg, wout,
                 cond_t, n_cond, w_mod, b_mod, w_in_odd, w_out_odd):
    nb = x.shape[0]
    nbs = PROMPT_BATCHES_PER_STEP
    steps = nb // nbs
    assert nb % nbs == 0 and D_MODEL % (BF16_SUBLANES * steps) == 0
    tr = D_MODEL // steps
    ins = (m, ng, win, qag, wq, kvag, wkk, wkv, qg, kg, naqg, nakg, wout)
    return pl.pallas_call(
        functools.partial(_p0_kernel, n_cond),
        grid=(steps,),
        in_specs=[pl.BlockSpec((nbs, SEQ, D_MODEL), lambda b: (b, 0, 0))] + [_spec(a) for a in ins]
        + [pl.BlockSpec((tr, 8), lambda b: (b, 0)),
           pl.BlockSpec((None, tr, 3 * D_MODEL), lambda b: (1, b, 0)),
           _full(b_mod.shape),
           pl.BlockSpec((1, tr, O_END), lambda b: (0, b, 0)),
           pl.BlockSpec((1, tr, D_MODEL), lambda b: (0, b, 0))],
        out_specs=[pl.BlockSpec((nbs, SEQ, D_MODEL), lambda b: (b, 0, 0)),
                   pl.BlockSpec((nbs, 1, SEQ, KV_RANK), lambda b: (b, 0, 0, 0)),
                   pl.BlockSpec((nbs, 1, ROPE_A, SEQ), lambda b: (b, 0, 0, 0)),
                   pl.BlockSpec((nbs, 1, N_HEADS, HEAD_DIM, SEQ), lambda b: (b, 0, 0, 0, 0)),
                   pl.BlockSpec((nbs, 1, N_HEADS, HEAD_DIM, SEQ), lambda b: (b, 0, 0, 0, 0)),
                   pl.BlockSpec((8, 3 * D_MODEL), lambda b: (0, 0)),
                   pl.BlockSpec((tr, O_END), lambda b: (b, 0)),
                   pl.BlockSpec((tr, D_MODEL), lambda b: (b, 0))],
        out_shape=[jax.ShapeDtypeStruct((nb, SEQ, D_MODEL), F32),
                   jax.ShapeDtypeStruct((nb, 1, SEQ, KV_RANK), F32),
                   jax.ShapeDtypeStruct((nb, 1, ROPE_A, SEQ), F32),
                   jax.ShapeDtypeStruct((nb, 1, N_HEADS, HEAD_DIM, SEQ), F32),
                   jax.ShapeDtypeStruct((nb, 1, N_HEADS, HEAD_DIM, SEQ), F32),
                   jax.ShapeDtypeStruct((8, 3 * D_MODEL), F32),
                   jax.ShapeDtypeStruct((D_MODEL, O_END), BF16),
                   jax.ShapeDtypeStruct((D_MODEL, D_MODEL), BF16)],
        scratch_shapes=[pltpu.VMEM((nbs * SEQ, D_MODEL), BF16)],
        compiler_params=pltpu.CompilerParams(dimension_semantics=("arbitrary",), vmem_limit_bytes=VMEM_LIMIT),
        name="prompt_even",
    )(x, *map(_arr, ins), cond_t, w_mod, b_mod, w_in_odd, w_out_odd)


def _gqa_pair_operands(k, v, kg2, lo):
    kn = _rms_halves(k, kg2, lo)
    return kn, (kn.astype(BF16), pltpu.roll(kn, HEAD_DIM, 1).astype(BF16)), \
        (_with_ones(v.astype(BF16)), _with_ones(pltpu.roll(v, HEAD_DIM, 1).astype(BF16)))


def _p1_kernel(sink_ref, x_ref, m_ref, ng_ref, win_ref, gqg_ref, gkg_ref, sqg_ref, skg_ref, wout_ref,
               xo_ref, gk_ref, gv_ref, sk_ref, sv_ref, y_scr):
    nbs = x_ref.shape[0]
    x = x_ref[...].reshape(nbs * SEQ, D_MODEL)
    h, gate = _modulate(x, ng_ref[...], m_ref[0:1, :])
    hb = h.astype(BF16)
    lo = _lane_lo()
    hi = jnp.logical_not(lo)
    sc = HEAD_DIM ** -0.5 * LOG2E
    rows = [slice(bi * SEQ, (bi + 1) * SEQ) for bi in range(nbs)]

    branches = ((O_QC, O_KC, O_VC, O_GC, gqg_ref, gkg_ref, gk_ref, gv_ref, False, 0),
                (O_QD, O_KD, O_VD, O_GD, sqg_ref, skg_ref, sk_ref, sv_ref, True, 4 * LANES))
    for oq, ok, ov, og, qg_ref, kg_ref, ck_ref, cv_ref, has_sink, yoff in branches:
        zq = _dot(hb, win_ref[:, oq:oq + 4 * LANES])
        zkv = _dot(hb, win_ref[:, ok:ok + 2 * LANES])
        zg = _dot(hb, win_ref[:, og:og + 4 * LANES])
        v = zkv[:, LANES:]
        kn, ks, vs = _gqa_pair_operands(zkv[:, :LANES], v, kg_ref[...], lo)
        for bi, rs in enumerate(rows):
            _store_pair_transposed(ck_ref, bi, 0, kn[rs])
            _store_pair_transposed(cv_ref, bi, 0, v[rs])
        qg = qg_ref[...] * sc
        for p in range(N_PAIRS):
            sl = slice(p * LANES, (p + 1) * LANES)
            qn = _rms_halves(zq[:, sl], qg, lo)
            qms = [jnp.where(keep, qn, 0.0).astype(BF16) for keep in (lo, hi)]
            kv = p // 2
            for bi, rs in enumerate(rows):
                o2 = []
                for half in (0, 1):
                    swap = 0 if kv == half else 1
                    sink = sink_ref[2 * p + half] * LOG2E if has_sink else None
                    o2.append(_attend([(_dot_nt(qms[half][rs], ks[swap][rs]), vs[swap][rs], False)], sink))
                o = jnp.where(lo, o2[0], o2[1])
                y_scr[rs, yoff + p * LANES:yoff + (p + 1) * LANES] = (o * _silu(zg[rs, sl])).astype(BF16)

    xo_ref[...] = (x + gate * _dot(y_scr[...], wout_ref[...])).reshape(nbs, SEQ, D_MODEL)


def _prompt_odd(sink, x, m, ng, win, gqg, gkg, sqg, skg, wout):
    nb = x.shape[0]
    nbs = PROMPT_ODD_BATCHES_PER_STEP
    assert nb % nbs == 0
    ins = (m, ng, win, gqg, gkg, sqg, skg, wout)
    cache_spec = pl.BlockSpec((nbs, 1, 2, HEAD_DIM, SEQ), lambda b: (b, 0, 0, 0, 0))
    cache_shape = jax.ShapeDtypeStruct((nb, 1, 2, HEAD_DIM, SEQ), F32)
    return pl.pallas_call(
        _p1_kernel,
        grid=(nb // nbs,),
        in_specs=[pl.BlockSpec(memory_space=pltpu.SMEM),
                  pl.BlockSpec((nbs, SEQ, D_MODEL), lambda b: (b, 0, 0))] + [_spec(a) for a in ins],
        out_specs=[pl.BlockSpec((nbs, SEQ, D_MODEL), lambda b: (b, 0, 0))] + [cache_spec] * 4,
        out_shape=[jax.ShapeDtypeStruct((nb, SEQ, D_MODEL), F32)] + [cache_shape] * 4,
        scratch_shapes=[pltpu.VMEM((nbs * SEQ, D_MODEL), BF16)],
        compiler_params=pltpu.CompilerParams(dimension_semantics=("arbitrary",), vmem_limit_bytes=VMEM_LIMIT),
        name="prompt_odd",
    )(sink, x, *map(_arr, ins))


def _s0a_kernel(x_ref, m_ref, ng_ref, win_ref, qag_ref, wq_ref, kvag_ref, wkk_ref, wkv_ref, qg_ref, kg_ref,
                naqg_ref, nakg_ref, cos_ref, sin_ref,
                qa_ref, ka_ref, va_ref, qb_ref, kb_ref, vb_ref, g_ref):
    b = pl.program_id(0)
    lo = _lane_lo()
    partner = _rope_matrix2(ROPE_A, LANES, NOPE_A)
    hb = _modulate(x_ref[0], ng_ref[...], m_ref[pl.ds(1 + b, 1), :])[0].astype(BF16)
    cos, sin = cos_ref[...], sin_ref[...]

    qln = _rms(_dot_nt(hb, win_ref[E_QLAT:E_CKV, :]), qag_ref[...], Q_RANK).astype(BF16)
    q_all = _dot(qln, wq_ref[...])
    ckv_n = _rms(_dot_nt(hb, win_ref[E_CKV:E_KROPE, :]), kvag_ref[...], KV_RANK)
    kr = _dot_nt(hb, _rope_key_slab(win_ref))
    cb = ckv_n.astype(BF16)
    kk = _dot(cb, wkk_ref[...])
    va_ref[0] = _dot(cb, wkv_ref[...]).astype(BF16)
    zq = _dot_nt(hb, win_ref[E_QB:E_KB, :])
    zk = _dot_nt(hb, win_ref[E_KB:E_VB, :])
    vb_ref[0] = _dot_nt(hb, win_ref[E_VB:E_GB, :]).astype(BF16)
    g_ref[0, :, 0:4 * LANES] = _silu(_dot_nt(hb, win_ref[E_GA:E_QB, :]))
    g_ref[0, :, 4 * LANES:8 * LANES] = _silu(_dot_nt(hb, win_ref[E_GB:E_END, :]))

    qg = qg_ref[...] * (QK_A ** -0.5 * LOG2E)
    kg = kg_ref[...]
    k_partner = _lane_mix(kr * kg, partner) * sin
    for hh in range(N_HEADS):
        sl = slice(hh * LANES, (hh + 1) * LANES)
        qn = _rms(q_all[:, sl], qg, QK_A)
        qa_ref[0, :, sl] = (qn * cos + _lane_mix(qn, partner) * sin).astype(BF16)
        k_raw = kk[:, sl] + kr
        k_inv = lax.rsqrt(jnp.sum(k_raw * k_raw, axis=-1, keepdims=True) / QK_A + EPS)
        ka_ref[0, :, sl] = ((k_raw * kg * cos + k_partner) * k_inv).astype(BF16)
    naqg = naqg_ref[...] * (HEAD_DIM ** -0.5 * LOG2E)
    for p in range(N_PAIRS):
        sl = slice(p * LANES, (p + 1) * LANES)
        qb_ref[0, :, sl] = _rms_halves(zq[:, sl], naqg, lo).astype(BF16)
        kb_ref[0, :, sl] = _rms_halves(zk[:, sl], nakg_ref[...], lo).astype(BF16)


def _sample_even_proj(x, m, ng, win, qag, wq, kvag, wkk, wkv, qg, kg, naqg, nakg, cos, sin):
    nb, s, _ = x.shape
    nq = s // PROJ_BLOCK
    ins = (m, ng, win, qag, wq, kvag, wkk, wkv, qg, kg, naqg, nakg)
    tab = pl.BlockSpec((PROJ_BLOCK, LANES), lambda b, j: (j, 0))

    def blk(w):
        return pl.BlockSpec((1, PROJ_BLOCK, w), lambda b, j: (b, j, 0))

    def shp(w, dt):
        return jax.ShapeDtypeStruct((nb, s, w), dt)

    return pl.pallas_call(
        _s0a_kernel,
        grid=(nb, nq),
        in_specs=[blk(D_MODEL)] + [_spec(a) for a in ins] + [tab, tab],
        out_specs=[blk(1024), blk(1024), blk(512), blk(512), blk(512), blk(512), blk(1024)],
        out_shape=[shp(1024, BF16), shp(1024, BF16), shp(512, BF16), shp(512, BF16), shp(512, BF16),
                   shp(512, BF16), shp(1024, F32)],
        compiler_params=pltpu.CompilerParams(dimension_semantics=("arbitrary", "arbitrary"),
                                             vmem_limit_bytes=VMEM_LIMIT),
        name="sample_even_proj",
    )(x, *map(_arr, ins), cos, sin)


def _build_bias_table(rpb_ref, tile_scr, tab_ref):
    qc = lax.broadcasted_iota(jnp.int32, (GRID_W, LANES), 0)
    lane = lax.broadcasted_iota(jnp.int32, (GRID_W, LANES), 1)
    kc = jnp.bitwise_and(lane, GRID_W - 1)
    lo = lane < GRID_W
    cs = jnp.clip(qc - NA_COLS // 2, 0, GRID_W - NA_COLS)
    valid = (kc >= cs) & (kc < cs + NA_COLS)
    tab_ref[...] = jnp.zeros(tab_ref.shape, F32)
    tile_scr[RPB_ROWS] = jnp.zeros((GRID_W, LANES), F32)

    def per_head(h, carry):
        r = rpb_ref[h] * LOG2E
        half = jnp.concatenate([r[:, NA_COLS - 1:], jnp.zeros((RPB_ROWS, GRID_W - RPB_COLS), F32),
                                r[:, :NA_COLS - 1]], axis=1)
        base = jnp.concatenate([half, half], axis=1)
        for dr in range(RPB_ROWS):
            t = pltpu.roll(jnp.broadcast_to(base[dr:dr + 1], (GRID_W, LANES)), 0, 1, stride=1, stride_axis=0)
            tile_scr[dr] = jnp.where(valid, t, NEG_INF)
        for c in range(NA_ROWS // 2, NA_ROWS // 2 + NA_ROWS):
            d0 = 2 * c - NA_ROWS
            tab_ref[0, h, c] = jnp.where(lo, tile_scr[d0], tile_scr[d0 + 1])
            tab_ref[1, h, c] = jnp.where(lo, tile_scr[d0 - 1 if d0 > 0 else RPB_ROWS], tile_scr[d0])
        return carry

    lax.fori_loop(0, N_HEADS, per_head, 0)


def _s0b_kernel(rpb_ref, x_ref, m_ref, qa_ref, ka_ref, va_ref, qb_ref, kb_ref, vb_ref, g_ref,
                cckv_ref, ckr_ref, cnk_ref, cnv_ref, wkk_ref, wkv_ref, kg_ref, wout_ref,
                xo_ref, kca_scr, vca_scr, tile_scr, tab_scr, y_scr):
    b = pl.program_id(0)
    j = pl.program_id(1)
    lo = _lane_lo()
    n_lat = ka_ref.shape[1]

    @pl.when((b == 0) & (j == 0))
    def _():
        _build_bias_table(rpb_ref, tile_scr, tab_scr)

    @pl.when(j == 0)
    def _():
        kr_t = jnp.concatenate([jnp.zeros((NOPE_A, PAST_LEN), F32), ckr_ref[0],
                                jnp.zeros((LANES - QK_A, PAST_LEN), F32)], axis=0)
        keys, vals = _mla_keys(cckv_ref[0].astype(BF16), kr_t.T, wkk_ref, wkv_ref, kg_ref[...])
        for hh in range(N_HEADS):
            kca_scr[:, hh * LANES:(hh + 1) * LANES] = keys[hh]
        vca_scr[...] = vals

    kidx = lax.broadcasted_iota(jnp.int32, (1, n_lat), 1)
    for p in range(N_PAIRS):
        sl = slice(p * LANES, (p + 1) * LANES)
        o2 = []
        va = _with_ones(va_ref[0, :, sl])
        vca = _with_ones(vca_scr[:, sl])
        for hh in (2 * p, 2 * p + 1):
            hs = slice(hh * LANES, (hh + 1) * LANES)
            q = qa_ref[0, :, hs]
            o2.append(_attend([(_dot_nt(q, ka_ref[0, :, hs]), va, False),
                               (_dot_nt(q, kca_scr[:, hs]), vca, False)]))
        oa = jnp.where(lo, o2[0], o2[1])
        y_scr[:, sl] = (oa * g_ref[0, :, sl]).astype(BF16)

        qb = qb_ref[0, :, sl]
        kb = kb_ref[0, :, sl]
        vb = _with_ones(vb_ref[0, :, sl])
        kcb = cnk_ref[0, sl, :].astype(BF16)
        vcb = _with_ones(cnv_ref[0, sl, :].astype(BF16), transposed=True)
        o2 = []
        for half in (0, 1):
            head = 2 * p + half
            qm = jnp.where(lo if half == 0 else jnp.logical_not(lo), qb, jnp.zeros_like(qb))
            s_lat = _dot_nt(qm, kb)
            rows = []
            for local in range(NA_Q_BLOCK // GRID_W):
                qr = j * (NA_Q_BLOCK // GRID_W) + local
                par = 0 if local % 2 == 1 else 1
                c0 = (RPB_ROWS + par - local) // 2 - (NA_Q_BLOCK // GRID_W // 2) * j
                bias = jnp.concatenate([tab_scr[par, head, c0 + t] for t in range(n_lat // LANES)], axis=1)
                r0 = jnp.clip(qr - NA_ROWS // 2, 0, n_lat // GRID_W - NA_ROWS) * GRID_W
                ok = (kidx >= r0) & (kidx < r0 + NA_ROWS * GRID_W)
                bias = bias + jnp.where(ok, 0.0, NEG_INF)
                rows.append(s_lat[local * GRID_W:(local + 1) * GRID_W] + bias)
            s_lat = jnp.concatenate(rows, axis=0)
            o2.append(_attend([(s_lat, vb, False), (_dot(qm, kcb), vcb, True)]))
        ob = jnp.where(lo, o2[0], o2[1])
        ys = slice(4 * LANES + p * LANES, 4 * LANES + (p + 1) * LANES)
        y_scr[:, ys] = (ob * g_ref[0, :, ys]).astype(BF16)

    d = x_ref.shape[-1]
    gate = m_ref[pl.ds(1 + b, 1), 2 * d:]
    xo_ref[0] = x_ref[0] + gate * _dot(y_scr[...], wout_ref[...])


def _sample_even_attn(rpb, x, m, qa, ka, va, qb, kb, vb, g, cckv, ckr, cnk, cnv, wkk, wkv, kg, wout):
    nb, s, _ = x.shape
    nq = s // NA_Q_BLOCK

    def blk(w):
        return pl.BlockSpec((1, NA_Q_BLOCK, w), lambda b, j: (b, j, 0))

    def per_batch(a):
        return pl.BlockSpec((1,) + a.shape[1:], lambda b, j: (b, 0, 0), pipeline_mode=pl.Buffered(1))

    return pl.pallas_call(
        _s0b_kernel,
        grid=(nb, nq),
        in_specs=[_full(rpb.shape), blk(D_MODEL), _spec(m),
                  blk(1024), per_batch(ka), per_batch(va), blk(512), per_batch(kb), per_batch(vb), blk(1024),
                  per_batch(cckv), per_batch(ckr), per_batch(cnk), per_batch(cnv),
                  _full(wkk.shape), _full(wkv.shape), _spec(kg), _full(wout.shape)],
        out_specs=blk(D_MODEL),
        out_shape=jax.ShapeDtypeStruct(x.shape, F32),
        scratch_shapes=[pltpu.VMEM((PAST_LEN, N_HEADS * LANES), BF16),
                        pltpu.VMEM((PAST_LEN, N_HEADS * HEAD_DIM), BF16),
                        pltpu.VMEM((RPB_ROWS + 1, GRID_W, LANES), F32),
                        pltpu.VMEM((2, N_HEADS, BIAS_CHUNKS, GRID_W, LANES), F32),
                        pltpu.VMEM((NA_Q_BLOCK, D_MODEL), BF16)],
        compiler_params=pltpu.CompilerParams(dimension_semantics=("arbitrary", "arbitrary"),
                                             vmem_limit_bytes=VMEM_LIMIT),
        name="sample_even_attn",
    )(rpb, x, _arr(m), qa, ka, va, qb, kb, vb, g, cckv, ckr, cnk, cnv, wkk, wkv, _arr(kg), wout)


def _s1a_kernel(x_ref, m_ref, ng_ref, win_ref, gqg_ref, gkg_ref, sqg_ref, skg_ref, cos_ref, sin_ref,
                qc_ref, kc_ref, vc_ref, qd_ref, kd_ref, vd_ref, g_ref):
    b = pl.program_id(0)
    lo = _lane_lo()
    partner = _rope_matrix2(HEAD_DIM, HEAD_DIM, 0)
    swap = _swap_matrix2()[:LANES]
    hb = _modulate(x_ref[0], ng_ref[...], m_ref[pl.ds(1 + b, 1), :])[0].astype(BF16)
    cos, sin = cos_ref[...], sin_ref[...]
    sc = HEAD_DIM ** -0.5 * LOG2E

    def rope(t):
        return t * cos + _lane_mix(t, partner) * sin

    branches = ((O_QC, O_KC, O_GC, gqg_ref, gkg_ref, qc_ref, kc_ref, vc_ref, 0),
                (O_QD, O_KD, O_GD, sqg_ref, skg_ref, qd_ref, kd_ref, vd_ref, 4 * LANES))
    for oq, ok, og, qg_ref, kg_ref, q_out, k_out, v_out, goff in branches:
        zq = _dot(hb, win_ref[:, oq:oq + 4 * LANES])
        zkv = _dot(hb, win_ref[:, ok:ok + 2 * LANES])
        qg = qg_ref[...] * sc
        for p in range(N_PAIRS):
            sl = slice(p * LANES, (p + 1) * LANES)
            q_out[0, :, sl] = rope(_rms_halves(zq[:, sl], qg, lo)).astype(BF16)
        kn = rope(_rms_halves(zkv[:, :LANES], kg_ref[...], lo))
        v = zkv[:, LANES:]
        for out, val in ((k_out, kn.astype(BF16)), (v_out, v.astype(BF16))):
            out[0, :, 0:LANES] = val
            out[0, :, LANES:2 * LANES] = _dot(val, swap).astype(BF16)
        g_ref[0, :, goff:goff + 4 * LANES] = _silu(_dot(hb, win_ref[:, og:og + 4 * LANES]))


def _sample_odd_proj(x, m, ng, win, gqg, gkg, sqg, skg, cos, sin):
    nb, s, _ = x.shape
    nq = s // PROJ_BLOCK
    ins = (m, ng, win, gqg, gkg, sqg, skg)
    tab = pl.BlockSpec((PROJ_BLOCK, LANES), lambda b, j: (j, 0))

    def blk(w):
        return pl.BlockSpec((1, PROJ_BLOCK, w), lambda b, j: (b, j, 0))

    def shp(w, dt):
        return jax.ShapeDtypeStruct((nb, s, w), dt)

    return pl.pallas_call(
        _s1a_kernel,
        grid=(nb, nq),
        in_specs=[blk(D_MODEL)] + [_spec(a) for a in ins] + [tab, tab],
        out_specs=[blk(512), blk(256), blk(256), blk(512), blk(256), blk(256), blk(1024)],
        out_shape=[shp(512, BF16), shp(256, BF16), shp(256, BF16), shp(512, BF16), shp(256, BF16),
                   shp(256, BF16), shp(1024, F32)],
        compiler_params=pltpu.CompilerParams(dimension_semantics=("arbitrary", "arbitrary"),
                                             vmem_limit_bytes=VMEM_LIMIT),
        name="sample_odd_proj",
    )(x, *map(_arr, ins), cos, sin)


def _s1b_kernel(sink_ref, x_ref, m_ref, qc_ref, kc_ref, vc_ref, qd_ref, kd_ref, vd_ref, g_ref,
                cgk_ref, cgv_ref, csk_ref, csv_ref, wout_ref, xo_ref, y_scr):
    b = pl.program_id(0)
    j = pl.program_id(1)
    lo = _lane_lo()
    n_lat = kc_ref.shape[1]
    win_keys = Q_BLOCK + 2 * SWA_HALF

    def ctx_pair(ref, values=False):
        a = ref[0].astype(BF16)
        pair = (a, _swap_halves(a))
        return tuple(_with_ones(t, transposed=True) for t in pair) if values else pair

    cgk, cgv, csk, csv = ctx_pair(cgk_ref), ctx_pair(cgv_ref, True), ctx_pair(csk_ref), ctx_pair(csv_ref, True)
    vcs = [_with_ones(vc_ref[0, :, w * LANES:(w + 1) * LANES]) for w in (0, 1)]

    ks = pl.multiple_of(jnp.clip(j * Q_BLOCK - SWA_HALF, 0, n_lat - win_keys), SWA_HALF)
    qpos = j * Q_BLOCK + lax.broadcasted_iota(jnp.int32, (Q_BLOCK, win_keys), 0)
    kpos = ks + lax.broadcasted_iota(jnp.int32, (Q_BLOCK, win_keys), 1)
    band = jnp.abs(qpos - kpos) <= SWA_HALF
    vds = [_with_ones(vd_ref[0, pl.ds(ks, win_keys), w * LANES:(w + 1) * LANES]) for w in (0, 1)]

    for p in range(N_PAIRS):
        sl = slice(p * LANES, (p + 1) * LANES)
        kv = p // 2
        qc = qc_ref[0, :, sl]
        qd = qd_ref[0, :, sl]
        oc2, od2 = [], []
        for half in (0, 1):
            swap = 0 if kv == half else 1
            ws = slice(swap * LANES, (swap + 1) * LANES)
            keep = lo if half == 0 else jnp.logical_not(lo)
            qm = jnp.where(keep, qc, jnp.zeros_like(qc))
            oc2.append(_attend([(_dot_nt(qm, kc_ref[0, :, ws]), vcs[swap], False),
                                (_dot(qm, cgk[swap]), cgv[swap], True)]))
            qm = jnp.where(keep, qd, jnp.zeros_like(qd))
            s_loc = jnp.where(band, _dot_nt(qm, kd_ref[0, pl.ds(ks, win_keys), ws]), NEG_INF)
            od2.append(_attend([(s_loc, vds[swap], False),
                                (_dot(qm, csk[swap]), csv[swap], True)], sink_ref[2 * p + half] * LOG2E))
        y_scr[:, sl] = (jnp.where(lo, oc2[0], oc2[1]) * g_ref[0, :, sl]).astype(BF16)
        ys = slice(4 * LANES + p * LANES, 4 * LANES + (p + 1) * LANES)
        y_scr[:, ys] = (jnp.where(lo, od2[0], od2[1]) * g_ref[0, :, ys]).astype(BF16)

    d = x_ref.shape[-1]
    gate = m_ref[pl.ds(1 + b, 1), 2 * d:]
    xo_ref[0] = x_ref[0] + gate * _dot(y_scr[...], wout_ref[...])


def _sample_odd_attn(sink, x, m, qc, kc, vc, qd, kd, vd, g, cgk, cgv, csk, csv, wout):
    nb, s, _ = x.shape
    nq = s // Q_BLOCK

    def blk(w):
        return pl.BlockSpec((1, Q_BLOCK, w), lambda b, j: (b, j, 0))

    def per_batch(a):
        return pl.BlockSpec((1,) + a.shape[1:], lambda b, j: (b, 0, 0))

    return pl.pallas_call(
        _s1b_kernel,
        grid=(nb, nq),
        in_specs=[pl.BlockSpec(memory_space=pltpu.SMEM), blk(D_MODEL), _spec(m),
                  blk(512), per_batch(kc), per_batch(vc), blk(512), per_batch(kd), per_batch(vd), blk(1024),
                  per_batch(cgk), per_batch(cgv), per_batch(csk), per_batch(csv), _full(wout.shape)],
        out_specs=blk(D_MODEL),
        out_shape=jax.ShapeDtypeStruct(x.shape, F32),
        scratch_shapes=[pltpu.VMEM((Q_BLOCK, D_MODEL), BF16)],
        compiler_params=pltpu.CompilerParams(dimension_semantics=("arbitrary", "arbitrary"),
                                             vmem_limit_bytes=VMEM_LIMIT),
        name="sample_odd_attn",
    )(sink, x, _arr(m), qc, kc, vc, qd, kd, vd, g, cgk, cgv, csk, csv, wout)


WEIGHT_PREP_STEPS = 8
EVEN_IN_CHUNKS = 6
COND_PREP_ROWS = 256
BF16_SUBLANES = 16


G_MLA_Q, G_MLA_K, G_NA_Q, G_NA_K, G_GQA_Q, G_GQA_K, G_SWA_Q, G_SWA_K, N_GAINS = range(9)


GAIN_WIDTHS = (QK_A, QK_A) + (HEAD_DIM,) * 6


def _cond_prep_kernel(n_cond, ct_ref, wm_ref, bm_ref, ng_ref, gains_ref, mo_ref, gt_ref, ngt_ref):
    _mod_step(n_cond, pl.program_id(0) == 0, bm_ref[0:1, :], ct_ref, wm_ref, mo_ref)
    gt_ref[...] = jnp.zeros(gt_ref.shape, F32)
    start = 0
    for r, w in enumerate(GAIN_WIDTHS):
        g = gains_ref[:, start:start + w]
        start += w
        for off in range(0, LANES - w + 1, w):
            gt_ref[r, :, off:off + w] = g
    for layer in range(ngt_ref.shape[0]):
        ngt_ref[layer] = ng_ref[layer:layer + 1, :]


def _cond_prep(cond_t, n_cond, w_mod, b_mod, norm_g, gains):
    assert len(gains) == N_GAINS and tuple(g.shape[-1] for g in gains) == GAIN_WIDTHS
    gains_row = jnp.concatenate([g.reshape(1, -1) for g in gains], axis=1)
    tk = COND_PREP_ROWS
    return pl.pallas_call(
        functools.partial(_cond_prep_kernel, n_cond),
        grid=(D_MODEL // tk,),
        in_specs=[pl.BlockSpec((tk, 8), lambda k: (k, 0)),
                  pl.BlockSpec((None, tk, 3 * D_MODEL), lambda k: (0, k, 0)),
                  _full(b_mod.shape), _full(norm_g.shape), _full(gains_row.shape)],
        out_specs=[_full((8, 3 * D_MODEL)), _full((N_GAINS, 1, LANES)), _full((norm_g.shape[0], 1, D_MODEL))],
        out_shape=[jax.ShapeDtypeStruct((8, 3 * D_MODEL), F32), jax.ShapeDtypeStruct((N_GAINS, 1, LANES), F32),
                   jax.ShapeDtypeStruct((norm_g.shape[0], 1, D_MODEL), F32)],
        compiler_params=pltpu.CompilerParams(dimension_semantics=("arbitrary",)),
        name="cond_prep",
    )(cond_t, w_mod, b_mod, norm_g, gains_row)


def _weight_prep_kernel(wie_ref, woe_ref, wqu_ref, wkv_ref, win_e_ref, wout_e_ref, wq_ref, wkk_ref, wkvv_ref):
    win_e_ref[...] = wie_ref[...].astype(BF16)
    wout_e_ref[...] = woe_ref[0].astype(BF16)

    wq_ref[...] = jnp.zeros(wq_ref.shape, BF16)
    for h in range(N_HEADS):
        wq_ref[:, h * LANES:h * LANES + QK_A] = wqu_ref[0, :, h * QK_A:(h + 1) * QK_A].astype(BF16)
    lo = _lane_lo()
    for p in range(N_PAIRS):
        a = wkv_ref[0, :, (2 * p) * LANES:(2 * p + 1) * LANES]
        c = wkv_ref[0, :, (2 * p + 1) * LANES:(2 * p + 2) * LANES]
        wkk_ref[:, (2 * p) * LANES:(2 * p + 1) * LANES] = jnp.where(lo, a, 0.0).astype(BF16)
        wkk_ref[:, (2 * p + 1) * LANES:(2 * p + 2) * LANES] = jnp.where(lo, c, 0.0).astype(BF16)
        wkvv_ref[:, p * LANES:(p + 1) * LANES] = jnp.where(lo, pltpu.roll(a, HEAD_DIM, 1), c).astype(BF16)


def _weight_prep(w_in_even_t, w_out_even, w_q_up, w_kv_up):
    n = WEIGHT_PREP_STEPS
    ins = (w_out_even, w_q_up, w_kv_up)
    out_cols = (D_MODEL, N_HEADS * LANES, N_HEADS * LANES, N_HEADS * HEAD_DIM)
    out_rows = (D_MODEL, Q_RANK, KV_RANK, KV_RANK)
    te = E_END // EVEN_IN_CHUNKS
    assert te * EVEN_IN_CHUNKS == E_END and te % BF16_SUBLANES == 0 and EVEN_IN_CHUNKS <= n
    even_spec = pl.BlockSpec((te, D_MODEL), lambda i: (jnp.minimum(i, EVEN_IN_CHUNKS - 1), 0))
    return pl.pallas_call(
        _weight_prep_kernel,
        grid=(n,),
        in_specs=[even_spec] + [pl.BlockSpec((1, a.shape[1] // n, a.shape[2]), lambda i: (0, i, 0)) for a in ins],
        out_specs=[even_spec] + [pl.BlockSpec((r // n, c), lambda i: (i, 0)) for r, c in zip(out_rows, out_cols)],
        out_shape=[jax.ShapeDtypeStruct((E_END, D_MODEL), BF16)]
        + [jax.ShapeDtypeStruct((r, c), BF16) for r, c in zip(out_rows, out_cols)],
        compiler_params=pltpu.CompilerParams(dimension_semantics=("arbitrary",), vmem_limit_bytes=VMEM_LIMIT),
        name="weight_prep",
    )(w_in_even_t, *ins)


def _feature_major(c):
    b, h, l, d = c.shape
    return jnp.swapaxes(c, -1, -2).reshape(b, h * d, l)


def _token_major(c):
    return jnp.swapaxes(c, -1, -2)


def _rope_tables(s, rot_dim, period, start):
    quarter = rot_dim // 4
    t = np.arange(s)
    inv = ROPE_THETA ** (-np.arange(quarter, dtype=np.float64) / quarter)
    row = (t // GRID_W).astype(np.float64)[:, None] * inv
    col = (t % GRID_W).astype(np.float64)[:, None] * inv
    ang = np.concatenate([row, col], axis=-1)
    cos, sin = np.cos(ang), np.sin(ang)
    pre = np.ones((s, start))
    post = np.zeros((s, period - start - rot_dim))
    c = np.concatenate([pre, cos, cos, post], axis=-1)
    sn = np.concatenate([0 * pre, sin, sin, post], axis=-1)
    rep = LANES // period
    return jnp.asarray(np.tile(c, (1, rep)), F32), jnp.asarray(np.tile(sn, (1, rep)), F32)


def kernel(x_prompt, x_sample, cache_mla_ckv, cache_mla_krope, cache_na_k, cache_na_v, cache_gqa_k, cache_gqa_v, cache_swa_k, cache_swa_v, c, c_ctx, norm_g, w_mod, b_mod, w_in_even, mla_qa_g, w_q_up, mla_kva_g, w_kv_up, mla_q_g, mla_k_g, na_q_g, na_k_g, na_rpb, w_out_even, w_in_odd, gqa_q_g, gqa_k_g, swa_q_g, swa_k_g, swa_sink, w_out_odd):
    n_dec = x_sample.shape[0]
    assert w_mod.shape[0] == 2 and n_dec + 1 <= 8

    cond_t = jnp.concatenate([c_ctx[:, None], c.T, jnp.zeros((D_MODEL, 7 - n_dec), F32)], axis=1)
    n_cond = 1 + n_dec
    gains = (mla_q_g, mla_k_g, na_q_g, na_k_g, gqa_q_g, gqa_k_g, swa_q_g, swa_k_g)
    win_e, wout_e, wq, wkk, wkv = _weight_prep(jnp.swapaxes(w_in_even[0], 0, 1), w_out_even, w_q_up, w_kv_up)
    m_even, gt, ngt = _cond_prep(cond_t, n_cond, w_mod, b_mod, norm_g, gains)
    even = (_Row(ngt, 0), win_e, mla_qa_g, wq, mla_kva_g, wkk, wkv,
            _Row(gt, G_MLA_Q), _Row(gt, G_MLA_K), _Row(gt, G_NA_Q), _Row(gt, G_NA_K))
    sink = swa_sink[0].astype(F32)

    xp1, new_ckv, new_krope, new_na_k, new_na_v, m_odd, win_o, wout_o = _prompt_even(
        x_prompt, m_even, *even, wout_e, cond_t, n_cond, w_mod, b_mod, w_in_odd, w_out_odd)
    odd = (_Row(ngt, 1), win_o, _Row(gt, G_GQA_Q), _Row(gt, G_GQA_K), _Row(gt, G_SWA_Q), _Row(gt, G_SWA_K))
    xp2, new_gqa_k, new_gqa_v, new_swa_k, new_swa_v = _prompt_odd(sink, xp1, m_odd, *odd, wout_o)

    cos_e, sin_e = _rope_tables(DEC_SEQ, ROPE_A, LANES, NOPE_A)
    qa, ka, va, qbs, kbs, vbs, g0 = _sample_even_proj(x_sample, m_even, *even, cos_e, sin_e)
    ckr = jnp.swapaxes(cache_mla_krope[:, 0], -1, -2)
    xs1 = _sample_even_attn(na_rpb[0], x_sample, m_even, qa, ka, va, qbs, kbs, vbs, g0,
                            cache_mla_ckv[:, 0], ckr, _feature_major(cache_na_k[:, 0]),
                            _feature_major(cache_na_v[:, 0]), wkk, wkv, _Row(gt, G_MLA_K), wout_e)
    cos_o, sin_o = _rope_tables(DEC_SEQ, HEAD_DIM, HEAD_DIM, 0)
    qc, kc, vc, qd, kd, vd, g1 = _sample_odd_proj(xs1, m_odd, *odd, cos_o, sin_o)
    xs2 = _sample_odd_attn(sink, xs1, m_odd, qc, kc, vc, qd, kd, vd, g1,
                           _feature_major(cache_gqa_k[:, 0]), _feature_major(cache_gqa_v[:, 0]),
                           _feature_major(cache_swa_k[:, 0]), _feature_major(cache_swa_v[:, 0]), wout_o)

    caches = (new_krope, new_na_k, new_na_v, new_gqa_k, new_gqa_v, new_swa_k, new_swa_v)
    return (xp2, xs2, new_ckv) + tuple(_token_major(c) for c in caches)
```
